```python
import math
import jax, jax.numpy as jnp
from jax import lax
import numpy as np

D_MODEL = 1024
BATCH = 8
SEQ = 2048
DEPTH = 1
DEC_BATCH = 128
DEC_SEQ = 1
PAST_LEN = 16384
PAGE_SIZE = 128

MIX_WIDTH = D_MODEL
SSM_WIDTH = MIX_WIDTH // 2
CONV_WIDTH = MIX_WIDTH - SSM_WIDTH
SSM_GROUP = 16
N_SSM_GROUPS = SSM_WIDTH // SSM_GROUP
SSM_STATE = 64
CONV_K = 3
CONV_HEADS = 8
D_FF = 2816
IN_PROJ_WIDTH = SSM_WIDTH + 3 * CONV_WIDTH
EPS = 1e-6
DT_MIN = 1e-3
DT_MAX = 1e-1

kernel_name = "hymba_s5_shortconv_macaron_step"


def rmsnorm(x, g):
    xf = x.astype(jnp.float32)
    r = lax.rsqrt(jnp.mean(xf * xf, axis=-1, keepdims=True) + EPS)
    return (xf * r * g.astype(jnp.float32)).astype(x.dtype)


def swiglu(x, w_gate, w_up, w_down):
    return (jax.nn.silu(x @ w_gate) * (x @ w_up)) @ w_down


def s5_discretize(lam_re, lam_im, log_dt, b_re, b_im):
    f32 = jnp.float32
    dt = jnp.exp(log_dt.astype(f32))[:, None]
    lr = lam_re.astype(f32)
    li = lam_im.astype(f32)
    mag = jnp.exp(lr * dt)
    ab_re = mag * jnp.cos(li * dt)
    ab_im = mag * jnp.sin(li * dt)
    den = lr * lr + li * li
    nr = ab_re - 1.0
    ni = ab_im
    coef_re = ((nr * lr + ni * li) / den)[..., None]
    coef_im = ((ni * lr - nr * li) / den)[..., None]
    br = b_re.astype(f32)
    bi = b_im.astype(f32)
    bb_re = coef_re * br - coef_im * bi
    bb_im = coef_re * bi + coef_im * br
    return ab_re, ab_im, bb_re, bb_im


def _complex_affine_combine(e1, e2):
    a1r, a1i, b1r, b1i = e1
    a2r, a2i, b2r, b2i = e2
    return (a1r * a2r - a1i * a2i,
            a1r * a2i + a1i * a2r,
            a2r * b1r - a2i * b1i + b2r,
            a2r * b1i + a2i * b1r + b2i)


def s5_mixer(u, h0_re, h0_im, lam_re, lam_im, log_dt, b_re, b_im, c_re, c_im, d_skip, w_glu, b_glu):
    f32 = jnp.float32
    bsz, seq_len, _ = u.shape
    uf = u.astype(f32)
    ug = uf.reshape(bsz, seq_len, N_SSM_GROUPS, SSM_GROUP)
    ab_re, ab_im, bb_re, bb_im = s5_discretize(lam_re, lam_im, log_dt, b_re, b_im)
    bu_re = jnp.einsum('blgc,gpc->blgp', ug, bb_re)
    bu_im = jnp.einsum('blgc,gpc->blgp', ug, bb_im)
    h0r = h0_re.astype(f32)
    h0i = h0_im.astype(f32)
    bu_re = bu_re.at[:, 0].add(ab_re * h0r - ab_im * h0i)
    bu_im = bu_im.at[:, 0].add(ab_re * h0i + ab_im * h0r)
    a_re = jnp.broadcast_to(ab_re, bu_re.shape)
    a_im = jnp.broadcast_to(ab_im, bu_im.shape)
    _, _, x_re, x_im = lax.associative_scan(_complex_affine_combine, (a_re, a_im, bu_re, bu_im), axis=1)
    y = (jnp.einsum('blgp,gcp->blgc', x_re, c_re.astype(f32))
         - jnp.einsum('blgp,gcp->blgc', x_im, c_im.astype(f32)))
    y = y.reshape(bsz, seq_len, SSM_WIDTH) + d_skip.astype(f32) * uf
    z = jax.nn.gelu(y)
    out = z * jax.nn.sigmoid(z @ w_glu.astype(f32) + b_glu.astype(f32))
    return out.astype(u.dtype), x_re[:, -1], x_im[:, -1]


def short_conv_mixer(x_in, gate_b, gate_c, buf, conv_w):
    seq_len = x_in.shape[1]
    v = gate_c * x_in
    vp = jnp.concatenate([buf.astype(v.dtype), v], axis=1)
    z = conv_w[0] * vp[:, 0:seq_len]
    for k in range(1, CONV_K):
        z = z + conv_w[k] * vp[:, k:k + seq_len]
    return gate_b * z, vp[:, -(CONV_K - 1):]


def decoder_layer(x, h_re, h_im, conv_buf,
                  ffn1_norm, ffn1_w_gate, ffn1_w_up, ffn1_w_down,
                  mix_norm, w_in,
                  ssm_lambda_re, ssm_lambda_im, ssm_log_dt, ssm_b_re, ssm_b_im,
                  ssm_c_re, ssm_c_im, ssm_d, ssm_w_glu, ssm_b_glu,
                  conv_w, w_out,
                  ffn2_norm, ffn2_w_gate, ffn2_w_up, ffn2_w_down):
    h = x + 0.5 * swiglu(rmsnorm(x, ffn1_norm), ffn1_w_gate, ffn1_w_up, ffn1_w_down)
    p = rmsnorm(h, mix_norm) @ w_in
    u = p[..., :SSM_WIDTH]
    x_conv = p[..., SSM_WIDTH:SSM_WIDTH + CONV_WIDTH]
    gate_b = p[..., SSM_WIDTH + CONV_WIDTH:SSM_WIDTH + 2 * CONV_WIDTH]
    gate_c = p[..., SSM_WIDTH + 2 * CONV_WIDTH:]
    s_out, new_re, new_im = s5_mixer(u, h_re, h_im, ssm_lambda_re, ssm_lambda_im, ssm_log_dt,
                                     ssm_b_re, ssm_b_im, ssm_c_re, ssm_c_im, ssm_d, ssm_w_glu, ssm_b_glu)
    c_out, new_buf = short_conv_mixer(x_conv, gate_b, gate_c, conv_buf, conv_w)
    h = h + jnp.concatenate([s_out, c_out.astype(s_out.dtype)], axis=-1) @ w_out
    h = h + 0.5 * swiglu(rmsnorm(h, ffn2_norm), ffn2_w_gate, ffn2_w_up, ffn2_w_down)
    return h, new_re, new_im, new_buf


def setup_inputs(seed: int = 0) -> dict:
    key = jax.random.key(seed)
    ks = jax.random.split(key, 40)
    f32 = jnp.float32

    def nrm(k, shape, scale):
        return scale * jax.random.normal(k, shape, f32)

    n_idx = jnp.arange(SSM_STATE, dtype=f32)
    gp = (DEPTH, N_SSM_GROUPS, SSM_STATE)
    return {
        "x_prompt": nrm(ks[0], (BATCH, SEQ, D_MODEL), 1.0),
        "x_sample": nrm(ks[1], (DEC_BATCH, DEC_SEQ, D_MODEL), 1.0),
        "state_ssm_re": nrm(ks[2], (DEPTH, DEC_BATCH, N_SSM_GROUPS, SSM_STATE), 0.1),
        "state_ssm_im": nrm(ks[3], (DEPTH, DEC_BATCH, N_SSM_GROUPS, SSM_STATE), 0.1),
        "state_conv": nrm(ks[4], (DEPTH, DEC_BATCH, CONV_K - 1, CONV_WIDTH), 1.0),
        "ffn1_norm": 1.0 + nrm(ks[5], (DEPTH, D_MODEL), 0.02),
        "ffn1_w_gate": nrm(ks[6], (DEPTH, D_MODEL, D_FF), D_MODEL ** -0.5),
        "ffn1_w_up": nrm(ks[7], (DEPTH, D_MODEL, D_FF), D_MODEL ** -0.5),
        "ffn1_w_down": nrm(ks[8], (DEPTH, D_FF, D_MODEL), D_FF ** -0.5),
        "mix_norm": 1.0 + nrm(ks[9], (DEPTH, D_MODEL), 0.02),
        "w_in": nrm(ks[10], (DEPTH, D_MODEL, IN_PROJ_WIDTH), D_MODEL ** -0.5),
        "ssm_lambda_re": -0.5 + nrm(ks[11], gp, 0.01),
        "ssm_lambda_im": math.pi * n_idx + nrm(ks[12], gp, 0.01),
        "ssm_log_dt": jax.random.uniform(ks[13], (DEPTH, N_SSM_GROUPS), f32,
                                         minval=math.log(DT_MIN), maxval=math.log(DT_MAX)),
        "ssm_b_re": nrm(ks[14], (DEPTH, N_SSM_GROUPS, SSM_STATE, SSM_GROUP), (2 * SSM_GROUP) ** -0.5),
        "ssm_b_im": nrm(ks[15], (DEPTH, N_SSM_GROUPS, SSM_STATE, SSM_GROUP), (2 * SSM_GROUP) ** -0.5),
        "ssm_c_re": nrm(ks[16], (DEPTH, N_SSM_GROUPS, SSM_GROUP, SSM_STATE), SSM_STATE ** -0.5),
        "ssm_c_im": nrm(ks[17], (DEPTH, N_SSM_GROUPS, SSM_GROUP, SSM_STATE), SSM_STATE ** -0.5),
        "ssm_d": nrm(ks[18], (DEPTH, SSM_WIDTH), 1.0),
        "ssm_w_glu": nrm(ks[19], (DEPTH, SSM_WIDTH, SSM_WIDTH), SSM_WIDTH ** -0.5),
        "ssm_b_glu": nrm(ks[20], (DEPTH, SSM_WIDTH), 0.02),
        "conv_w": nrm(ks[21], (DEPTH, CONV_K, CONV_WIDTH), CONV_K ** -0.5),
        "w_out": nrm(ks[22], (DEPTH, MIX_WIDTH, D_MODEL), MIX_WIDTH ** -0.5),
        "ffn2_norm": 1.0 + nrm(ks[23], (DEPTH, D_MODEL), 0.02),
        "ffn2_w_gate": nrm(ks[24], (DEPTH, D_MODEL, D_FF), D_MODEL ** -0.5),
        "ffn2_w_up": nrm(ks[25], (DEPTH, D_MODEL, D_FF), D_MODEL ** -0.5),
        "ffn2_w_down": nrm(ks[26], (DEPTH, D_FF, D_MODEL), D_FF ** -0.5),
        "final_norm": 1.0 + nrm(ks[27], (D_MODEL,), 0.02),
    }


def reference(x_prompt, x_sample, state_ssm_re, state_ssm_im, state_conv,
              ffn1_norm, ffn1_w_gate, ffn1_w_up, ffn1_w_down,
              mix_norm, w_in,
              ssm_lambda_re, ssm_lambda_im, ssm_log_dt, ssm_b_re, ssm_b_im,
              ssm_c_re, ssm_c_im, ssm_d, ssm_w_glu, ssm_b_glu,
              conv_w, w_out,
              ffn2_norm, ffn2_w_gate, ffn2_w_up, ffn2_w_down,
              final_norm):
    n_prompt = x_prompt.shape[0]
    hp = x_prompt
    hs = x_sample
    p_re, p_im, p_conv, s_re, s_im, s_conv = [], [], [], [], [], []
    for l in range(DEPTH):
        lw = (ffn1_norm[l], ffn1_w_gate[l], ffn1_w_up[l], ffn1_w_down[l],
              mix_norm[l], w_in[l],
              ssm_lambda_re[l], ssm_lambda_im[l], ssm_log_dt[l], ssm_b_re[l], ssm_b_im[l],
              ssm_c_re[l], ssm_c_im[l], ssm_d[l], ssm_w_glu[l], ssm_b_glu[l],
              conv_w[l], w_out[l],
              ffn2_norm[l], ffn2_w_gate[l], ffn2_w_up[l], ffn2_w_down[l])
        zero_h = jnp.zeros((n_prompt, N_SSM_GROUPS, SSM_STATE), jnp.float32)
        zero_buf = jnp.zeros((n_prompt, CONV_K - 1, CONV_WIDTH), x_prompt.dtype)
        hp, r_p, i_p, b_p = decoder_layer(hp, zero_h, zero_h, zero_buf, *lw)
        hs, r_s, i_s, b_s = decoder_layer(hs, state_ssm_re[l], state_ssm_im[l], state_conv[l], *lw)
        p_re.append(r_p); p_im.append(i_p); p_conv.append(b_p)
        s_re.append(r_s); s_im.append(i_s); s_conv.append(b_s)
    y_prompt = rmsnorm(hp, final_norm)
    y_sample = rmsnorm(hs, final_norm)
    new_ssm_re_prompt = jnp.stack(p_re, axis=0)
    new_ssm_im_prompt = jnp.stack(p_im, axis=0)
    new_conv_prompt = jnp.stack(p_conv, axis=0)
    new_ssm_re_sample = jnp.stack(s_re, axis=0)
    new_ssm_im_sample = jnp.stack(s_im, axis=0)
    new_conv_sample = jnp.stack(s_conv, axis=0)
    return (y_prompt, y_sample, new_ssm_re_prompt, new_ssm_im_prompt, new_conv_prompt,
            new_ssm_re_sample, new_ssm_im_sample, new_conv_sample)
```

```python
import functools
import math

import jax
import jax.numpy as jnp
from jax import lax
from jax.experimental import pallas as pl
from jax.experimental.pallas import tpu as pltpu

D_MODEL = 1024
D_FF = 2816
SSM_WIDTH = 512
CONV_WIDTH = 512
SSM_GROUP = 16
N_SSM_GROUPS = 32
SSM_STATE = 64
CONV_K = 3
IN_PROJ_WIDTH = SSM_WIDTH + 3 * CONV_WIDTH
EPS = 1e-6

LANES = 128
SUBLANES = 8
N_STATE = N_SSM_GROUPS * SSM_STATE
N_BLK = SSM_WIDTH // LANES
BLK_STATE = N_STATE // N_BLK
SLABS_PER_BLK = 2 * BLK_STATE // LANES
N_SLAB = N_BLK * SLABS_PER_BLK
HALF = SLABS_PER_BLK // 2

FFN_TM = 512
FFN_TF = 256
SCAN_T = 64
PITCH = SCAN_T + 4
VMEM_LIMIT = 56 * 1024 * 1024


def _rms(x, g):
    r = lax.rsqrt(jnp.mean(x * x, axis=-1, keepdims=True) + EPS)
    return x * r * g


def _bdot(a, b):
    return jnp.dot(a.astype(jnp.bfloat16), b.astype(jnp.bfloat16), preferred_element_type=jnp.float32)


def _swiglu(xn, wg_ref, wu_ref, wd_ref):
    acc = jnp.zeros((xn.shape[0], D_MODEL), jnp.float32)
    for j in range(0, D_FF, FFN_TF):
        g = jnp.dot(xn, wg_ref[:, j:j + FFN_TF], preferred_element_type=jnp.float32)
        u = jnp.dot(xn, wu_ref[:, j:j + FFN_TF], preferred_element_type=jnp.float32)
        a = (g * jax.nn.sigmoid(g)) * u
        acc = acc + jnp.dot(a.astype(jnp.bfloat16), wd_ref[j:j + FFN_TF, :],
                            preferred_element_type=jnp.float32)
    return acc


def _ffn1_kernel(x_ref, n1_ref, wg_ref, wu_ref, wd_ref, nm_ref, win_ref, h_ref, p_ref):
    x = x_ref[...]
    xn = _rms(x, n1_ref[...]).astype(jnp.bfloat16)
    h = x + 0.5 * _swiglu(xn, wg_ref, wu_ref, wd_ref)
    h_ref[...] = h
    hn = _rms(h, nm_ref[...]).astype(jnp.bfloat16)
    p_ref[...] = jnp.dot(hn, win_ref[...], preferred_element_type=jnp.float32)


def _const_spec(shape):
    return pl.BlockSpec(shape, lambda i: (0,) * len(shape), pipeline_mode=pl.Buffered(1))


def _ffn1(x, n1, wg, wu, wd, nm, win, tm):
    rows = x.shape[0]
    return pl.pallas_call(
        _ffn1_kernel,
        grid=(rows // tm,),
        in_specs=[
            pl.BlockSpec((tm, D_MODEL), lambda i: (i, 0)),
            _const_spec((1, D_MODEL)),
            _const_spec((D_MODEL, D_FF)),
            _const_spec((D_MODEL, D_FF)),
            _const_spec((D_FF, D_MODEL)),
            _const_spec((1, D_MODEL)),
            _const_spec((D_MODEL, IN_PROJ_WIDTH)),
        ],
        out_specs=[
            pl.BlockSpec((tm, D_MODEL), lambda i: (i, 0)),
            pl.BlockSpec((tm, IN_PROJ_WIDTH), lambda i: (i, 0)),
        ],
        out_shape=[
            jax.ShapeDtypeStruct((rows, D_MODEL), jnp.float32),
            jax.ShapeDtypeStruct((rows, IN_PROJ_WIDTH), jnp.float32),
        ],
        compiler_params=pltpu.CompilerParams(
            dimension_semantics=("arbitrary",), vmem_limit_bytes=VMEM_LIMIT),
        name="ffn1_inproj",
    )(x, n1, wg, wu, wd, nm, win)


def _ffn2_kernel(h_ref, mix_ref, wout_ref, n2_ref, wg_ref, wu_ref, wd_ref, nf_ref, y_ref):
    h = h_ref[...] + jnp.dot(mix_ref[...], wout_ref[...], preferred_element_type=jnp.float32)
    hn = _rms(h, n2_ref[...]).astype(jnp.bfloat16)
    h = h + 0.5 * _swiglu(hn, wg_ref, wu_ref, wd_ref)
    y_ref[...] = _rms(h, nf_ref[...])


def _ffn2(h, mix, wout, n2, wg, wu, wd, nf, tm):
    rows = h.shape[0]
    return pl.pallas_call(
        _ffn2_kernel,
        grid=(rows // tm,),
        in_specs=[
            pl.BlockSpec((tm, D_MODEL), lambda i: (i, 0)),
            pl.BlockSpec((tm, D_MODEL), lambda i: (i, 0)),
            _const_spec((D_MODEL, D_MODEL)),
            _const_spec((1, D_MODEL)),
            _const_spec((D_MODEL, D_FF)),
            _const_spec((D_MODEL, D_FF)),
            _const_spec((D_FF, D_MODEL)),
            _const_spec((1, D_MODEL)),
        ],
        out_specs=pl.BlockSpec((tm, D_MODEL), lambda i: (i, 0)),
        out_shape=jax.ShapeDtypeStruct((rows, D_MODEL), jnp.float32),
        compiler_params=pltpu.CompilerParams(
            dimension_semantics=("arbitrary",), vmem_limit_bytes=VMEM_LIMIT),
        name="outproj_ffn2",
    )(h, mix, wout, n2, wg, wu, wd, nf)


def _discretize(lr, li, ldt):
    dt = jnp.exp(ldt)
    mag = jnp.exp(lr * dt)
    ab_re = mag * jnp.cos(li * dt)
    ab_im = mag * jnp.sin(li * dt)
    den = lr * lr + li * li
    nr = ab_re - 1.0
    ni = ab_im
    coef_re = (nr * lr + ni * li) / den
    coef_im = (ni * lr - nr * li) / den
    return ab_re, ab_im, coef_re, coef_im


def _bbar_block(coef_re, coef_im, bre, bim, k):
    cr = coef_re[k:k + 1, :]
    ci = coef_im[k:k + 1, :]
    bb_re = cr * bre - ci * bim
    bb_im = cr * bim + ci * bre
    return jnp.concatenate([bb_re, bb_im], axis=1).astype(jnp.bfloat16)


def _glu_tail(y, u, d_ref, wglu_ref, bglu_ref):
    y = y + d_ref[...] * u
    z = jax.nn.gelu(y)
    gate = jax.nn.sigmoid(_bdot(z, wglu_ref[...]) + bglu_ref[...])
    return z * gate


def _mixer_prompt_kernel(p_ref, lr_ref, li_ref, ldt_ref, bre_ref, bim_ref, cre_ref, cim_ref,
                         d_ref, wglu_ref, bglu_ref, cw_ref,
                         mix_ref, sre_ref, sim_ref, nconv_ref,
                         xs_ref, u_ref, y_ref, state_ref, are_ref, aim_ref, bb_ref, v_ref):
    step = pl.program_id(0)
    nb = p_ref.shape[0]
    t_len = p_ref.shape[1]

    @pl.when(step == 0)
    def _init():
        ab_re, ab_im, coef_re, coef_im = _discretize(lr_ref[...], li_ref[...], ldt_ref[...])
        for k in range(N_BLK):
            bb_ref[k] = _bbar_block(coef_re, coef_im, bre_ref[k], bim_ref[k], k)
            for j in range(HALF):
                are_ref[k * HALF + j] = jnp.broadcast_to(
                    ab_re[k:k + 1, j * LANES:(j + 1) * LANES], (SUBLANES, LANES))
                aim_ref[k * HALF + j] = jnp.broadcast_to(
                    ab_im[k:k + 1, j * LANES:(j + 1) * LANES], (SUBLANES, LANES))
        state_ref[...] = jnp.zeros_like(state_ref)
        u_ref[...] = jnp.zeros_like(u_ref)
        v_ref[:, 0:SUBLANES, :] = jnp.zeros((nb, SUBLANES, CONV_WIDTH), jnp.float32)

    for b in range(nb):
        u_ref[pl.ds(b * PITCH, t_len), :] = p_ref[b, :, 0:SSM_WIDTH]
    u_all = u_ref[...]
    u_bf = u_all.astype(jnp.bfloat16)

    for k in range(N_BLK):
        bu = jnp.dot(u_bf[:, k * LANES:(k + 1) * LANES], bb_ref[k], preferred_element_type=jnp.float32)
        for j in range(SLABS_PER_BLK):
            xs_ref[k * SLABS_PER_BLK + j] = bu[:, j * LANES:(j + 1) * LANES]

    for k in range(N_BLK):
        a_re = [are_ref[k * HALF + j] for j in range(HALF)]
        a_im = [aim_ref[k * HALF + j] for j in range(HALF)]
        re_slab = [k * SLABS_PER_BLK + j for j in range(HALF)]
        im_slab = [k * SLABS_PER_BLK + HALF + j for j in range(HALF)]

        def body(t, carry, a_re=a_re, a_im=a_im, re_slab=re_slab, im_slab=im_slab):
            xr, xi = carry
            rows = pl.ds(t, SUBLANES, stride=PITCH)
            new_r, new_i = [], []
            for j in range(HALF):
                br = xs_ref[re_slab[j], rows, :]
                bi = xs_ref[im_slab[j], rows, :]
                nr = a_re[j] * xr[j] - a_im[j] * xi[j] + br
                ni = a_re[j] * xi[j] + a_im[j] * xr[j] + bi
                xs_ref[re_slab[j], rows, :] = nr
                xs_ref[im_slab[j], rows, :] = ni
                new_r.append(nr)
                new_i.append(ni)
            return tuple(new_r), tuple(new_i)

        init = (tuple(state_ref[s] for s in re_slab), tuple(state_ref[s] for s in im_slab))
        xr, xi = lax.fori_loop(0, t_len, body, init, unroll=4)
        for j in range(HALF):
            state_ref[re_slab[j]] = xr[j]
            state_ref[im_slab[j]] = xi[j]

    ys = []
    for k in range(N_BLK):
        x_re = jnp.concatenate([xs_ref[k * SLABS_PER_BLK + j] for j in range(HALF)], axis=1)
        x_im = jnp.concatenate([xs_ref[k * SLABS_PER_BLK + HALF + j] for j in range(HALF)], axis=1)
        ys.append(_bdot(x_re, cre_ref[k]) - _bdot(x_im, cim_ref[k]))
    y = jnp.concatenate(ys, axis=1)
    y_ref[...] = _glu_tail(y, u_all, d_ref, wglu_ref, bglu_ref)
    for b in range(nb):
        mix_ref[b, :, 0:SSM_WIDTH] = y_ref[pl.ds(b * PITCH, t_len), :].astype(mix_ref.dtype)

    x_conv = p_ref[:, :, SSM_WIDTH:SSM_WIDTH + CONV_WIDTH]
    gate_b = p_ref[:, :, SSM_WIDTH + CONV_WIDTH:SSM_WIDTH + 2 * CONV_WIDTH]
    gate_c = p_ref[:, :, SSM_WIDTH + 2 * CONV_WIDTH:]
    v = gate_c * x_conv
    v_ref[:, SUBLANES:SUBLANES + t_len, :] = v
    z = cw_ref[0:1, :] * v_ref[:, SUBLANES - 2:SUBLANES - 2 + t_len, :]
    z = z + cw_ref[1:2, :] * v_ref[:, SUBLANES - 1:SUBLANES - 1 + t_len, :]
    z = z + cw_ref[2:3, :] * v
    mix_ref[:, :, SSM_WIDTH:] = (gate_b * z).astype(mix_ref.dtype)
    v_ref[:, 0:SUBLANES, :] = v_ref[:, t_len:t_len + SUBLANES, :]

    @pl.when(step == pl.num_programs(0) - 1)
    def _final():
        for k in range(N_BLK):
            for j in range(HALF):
                lanes = slice(k * BLK_STATE + j * LANES, k * BLK_STATE + (j + 1) * LANES)
                sre_ref[:, lanes] = state_ref[k * SLABS_PER_BLK + j]
                sim_ref[:, lanes] = state_ref[k * SLABS_PER_BLK + HALF + j]
        nconv_ref[...] = v_ref[:, SUBLANES - 2:SUBLANES, :]


def _mixer_prompt(p, ssm, d, wglu, bglu, cw):
    nb, seq, _ = p.shape
    lr, li, ldt, bre, bim, cre, cim = ssm
    full = lambda shape: pl.BlockSpec(shape, lambda i: (0,) * len(shape))
    rows = nb * PITCH
    return pl.pallas_call(
        _mixer_prompt_kernel,
        grid=(seq // SCAN_T,),
        in_specs=[
            pl.BlockSpec((nb, SCAN_T, IN_PROJ_WIDTH), lambda i: (0, i, 0)),
            full((N_BLK, BLK_STATE)), full((N_BLK, BLK_STATE)), full((N_BLK, BLK_STATE)),
            full((N_BLK, LANES, BLK_STATE)), full((N_BLK, LANES, BLK_STATE)),
            full((N_BLK, BLK_STATE, LANES)), full((N_BLK, BLK_STATE, LANES)),
            full((1, SSM_WIDTH)), full((SSM_WIDTH, SSM_WIDTH)), full((1, SSM_WIDTH)),
            full((CONV_K, CONV_WIDTH)),
        ],
        out_specs=[
            pl.BlockSpec((nb, SCAN_T, D_MODEL), lambda i: (0, i, 0)),
            full((nb, N_STATE)), full((nb, N_STATE)),
            full((nb, CONV_K - 1, CONV_WIDTH)),
        ],
        out_shape=[
            jax.ShapeDtypeStruct((nb, seq, D_MODEL), jnp.bfloat16),
            jax.ShapeDtypeStruct((nb, N_STATE), jnp.float32),
            jax.ShapeDtypeStruct((nb, N_STATE), jnp.float32),
            jax.ShapeDtypeStruct((nb, CONV_K - 1, CONV_WIDTH), jnp.float32),
        ],
        scratch_shapes=[
            pltpu.VMEM((N_SLAB, rows, LANES), jnp.float32),
            pltpu.VMEM((rows, SSM_WIDTH), jnp.float32),
            pltpu.VMEM((rows, SSM_WIDTH), jnp.float32),
            pltpu.VMEM((N_SLAB, SUBLANES, LANES), jnp.float32),
            pltpu.VMEM((N_SLAB // 2, SUBLANES, LANES), jnp.float32),
            pltpu.VMEM((N_SLAB // 2, SUBLANES, LANES), jnp.float32),
            pltpu.VMEM((N_BLK, LANES, 2 * BLK_STATE), jnp.bfloat16),
            pltpu.VMEM((nb, SCAN_T + SUBLANES, CONV_WIDTH), jnp.float32),
        ],
        compiler_params=pltpu.CompilerParams(
            dimension_semantics=("arbitrary",), vmem_limit_bytes=VMEM_LIMIT),
        name="mixer_prompt",
    )(p, lr, li, ldt, bre, bim, cre, cim, d, wglu, bglu, cw)


def _mixer_step_kernel(p_ref, hre_ref, him_ref, buf_ref, lr_ref, li_ref, ldt_ref, bre_ref, bim_ref,
                       cre_ref, cim_ref, d_ref, wglu_ref, bglu_ref, cw_ref,
                       mix_ref, sre_ref, sim_ref, nconv_ref):
    ab_re, ab_im, coef_re, coef_im = _discretize(lr_ref[...], li_ref[...], ldt_ref[...])
    u = p_ref[:, 0:SSM_WIDTH]
    u_bf = u.astype(jnp.bfloat16)
    ys = []
    for k in range(N_BLK):
        bb = _bbar_block(coef_re, coef_im, bre_ref[k], bim_ref[k], k)
        bu = jnp.dot(u_bf[:, k * LANES:(k + 1) * LANES], bb, preferred_element_type=jnp.float32)
        st = slice(k * BLK_STATE, (k + 1) * BLK_STATE)
        a_re = ab_re[k:k + 1, :]
        a_im = ab_im[k:k + 1, :]
        h_re = hre_ref[:, st]
        h_im = him_ref[:, st]
        x_re = a_re * h_re - a_im * h_im + bu[:, :BLK_STATE]
        x_im = a_re * h_im + a_im * h_re + bu[:, BLK_STATE:]
        sre_ref[:, st] = x_re
        sim_ref[:, st] = x_im
        ys.append(_bdot(x_re, cre_ref[k]) - _bdot(x_im, cim_ref[k]))
    y = jnp.concatenate(ys, axis=1)
    mix_ref[:, 0:SSM_WIDTH] = _glu_tail(y, u, d_ref, wglu_ref, bglu_ref).astype(mix_ref.dtype)

    x_conv = p_ref[:, SSM_WIDTH:SSM_WIDTH + CONV_WIDTH]
    gate_b = p_ref[:, SSM_WIDTH + CONV_WIDTH:SSM_WIDTH + 2 * CONV_WIDTH]
    gate_c = p_ref[:, SSM_WIDTH + 2 * CONV_WIDTH:]
    v = gate_c * x_conv
    z = cw_ref[0:1, :] * buf_ref[0] + cw_ref[1:2, :] * buf_ref[1] + cw_ref[2:3, :] * v
    mix_ref[:, SSM_WIDTH:] = (gate_b * z).astype(mix_ref.dtype)
    nconv_ref[0] = buf_ref[1]
    nconv_ref[1] = v


def _mixer_step(p, h_re, h_im, buf, ssm, d, wglu, bglu, cw):
    nb = p.shape[0]
    lr, li, ldt, bre, bim, cre, cim = ssm
    return pl.pallas_call(
        _mixer_step_kernel,
        out_shape=[
            jax.ShapeDtypeStruct((nb, D_MODEL), jnp.bfloat16),
            jax.ShapeDtypeStruct((nb, N_STATE), jnp.float32),
            jax.ShapeDtypeStruct((nb, N_STATE), jnp.float32),
            jax.ShapeDtypeStruct((CONV_K - 1, nb, CONV_WIDTH), jnp.float32),
        ],
        compiler_params=pltpu.CompilerParams(vmem_limit_bytes=VMEM_LIMIT),
        name="mixer_step",
    )(p, h_re, h_im, buf, lr, li, ldt, bre, bim, cre, cim, d, wglu, bglu, cw)


def _ssm_layout(lam_re, lam_im, log_dt, b_re, b_im, c_re, c_im):
    gpb = N_SSM_GROUPS // N_BLK
    eye = jnp.eye(gpb, dtype=jnp.float32)
    lr = lam_re.reshape(N_BLK, BLK_STATE)
    li = lam_im.reshape(N_BLK, BLK_STATE)
    ldt = jnp.broadcast_to(log_dt[:, None], (N_SSM_GROUPS, SSM_STATE)).reshape(N_BLK, BLK_STATE)

    def b_blocks(b):
        b = b.reshape(N_BLK, gpb, SSM_STATE, SSM_GROUP).transpose(0, 1, 3, 2)
        return jnp.einsum("kgcp,gh->kgchp", b, eye).reshape(N_BLK, LANES, BLK_STATE)

    def c_blocks(c):
        c = c.reshape(N_BLK, gpb, SSM_GROUP, SSM_STATE).transpose(0, 1, 3, 2)
        return jnp.einsum("kgpc,gh->kgphc", c, eye).reshape(N_BLK, BLK_STATE, LANES).astype(jnp.bfloat16)

    return lr, li, ldt, b_blocks(b_re), b_blocks(b_im), c_blocks(c_re), c_blocks(c_im)


def kernel(x_prompt, x_sample, state_ssm_re, state_ssm_im, state_conv, ffn1_norm, ffn1_w_gate, ffn1_w_up, ffn1_w_down, mix_norm, w_in, ssm_lambda_re, ssm_lambda_im, ssm_log_dt, ssm_b_re, ssm_b_im, ssm_c_re, ssm_c_im, ssm_d, ssm_w_glu, ssm_b_glu, conv_w, w_out, ffn2_norm, ffn2_w_gate, ffn2_w_up, ffn2_w_down, final_norm):
    assert ffn1_norm.shape[0] == 1, "single-layer trunk"
    bf = jnp.bfloat16
    n_prompt, seq, _ = x_prompt.shape
    n_dec, dec_seq, _ = x_sample.shape
    assert dec_seq == 1 and seq % SCAN_T == 0 and (n_prompt * seq) % FFN_TM == 0

    n1 = ffn1_norm[0][None, :]
    nm = mix_norm[0][None, :]
    n2 = ffn2_norm[0][None, :]
    nf = final_norm[None, :]
    w1g, w1u, w1d = ffn1_w_gate[0].astype(bf), ffn1_w_up[0].astype(bf), ffn1_w_down[0].astype(bf)
    w2g, w2u, w2d = ffn2_w_gate[0].astype(bf), ffn2_w_up[0].astype(bf), ffn2_w_down[0].astype(bf)
    win = w_in[0].astype(bf)
    wout = w_out[0].astype(bf)
    wglu = ssm_w_glu[0].astype(bf)
    ssm = _ssm_layout(ssm_lambda_re[0], ssm_lambda_im[0], ssm_log_dt[0], ssm_b_re[0], ssm_b_im[0],
                      ssm_c_re[0], ssm_c_im[0])
    d = ssm_d[0][None, :]
    bglu = ssm_b_glu[0][None, :]
    cw = conv_w[0]

    xp = x_prompt.reshape(n_prompt * seq, D_MODEL)
    hp, pp = _ffn1(xp, n1, w1g, w1u, w1d, nm, win, FFN_TM)
    mixp, p_re, p_im, p_conv = _mixer_prompt(pp.reshape(n_prompt, seq, IN_PROJ_WIDTH), ssm, d, wglu, bglu, cw)
    yp = _ffn2(hp, mixp.reshape(n_prompt * seq, D_MODEL), wout, n2, w2g, w2u, w2d, nf, FFN_TM)

    xs = x_sample.reshape(n_dec, D_MODEL)
    hs, ps = _ffn1(xs, n1, w1g, w1u, w1d, nm, win, n_dec)
    s_re0 = state_ssm_re[0].reshape(n_dec, N_STATE)
    s_im0 = state_ssm_im[0].reshape(n_dec, N_STATE)
    buf = state_conv[0].transpose(1, 0, 2)
    mixs, s_re, s_im, s_conv = _mixer_step(ps, s_re0, s_im0, buf, ssm, d, wglu, bglu, cw)
    ys = _ffn2(hs, mixs, wout, n2, w2g, w2u, w2d, nf, n_dec)

    gs = (N_SSM_GROUPS, SSM_STATE)
    return (yp.reshape(n_prompt, seq, D_MODEL),
            ys.reshape(n_dec, 1, D_MODEL),
            p_re.reshape(1, n_prompt, *gs),
            p_im.reshape(1, n_prompt, *gs),
            p_conv[None],
            s_re.reshape(1, n_dec, *gs),
            s_im.reshape(1, n_dec, *gs),
            s_conv.transpose(1, 0, 2)[None])
```

```python
import jax
import jax.numpy as jnp
from jax import lax
from jax.experimental import pallas as pl
from jax.experimental.pallas import tpu as pltpu

D_MODEL = 1024
D_FF = 2816
SSM_WIDTH = 512
CONV_WIDTH = 512
SSM_GROUP = 16
N_SSM_GROUPS = 32
SSM_STATE = 64
CONV_K = 3
IN_PROJ_WIDTH = SSM_WIDTH + 3 * CONV_WIDTH
EPS = 1e-6

LANES = 128
SUBLANES = 8
N_STATE = N_SSM_GROUPS * SSM_STATE
N_BLK = SSM_WIDTH // LANES
BLK_STATE = N_STATE // N_BLK
SLABS_PER_BLK = 2 * BLK_STATE // LANES
HALF = SLABS_PER_BLK // 2
N_TILE = N_BLK * HALF

FFN_TM = 512
FFN_TF = 256
SCAN_T = 64
PITCH = SCAN_T + 4
VMEM_LIMIT = 58 * 1024 * 1024


def _rms(x, g):
    r = lax.rsqrt(jnp.mean(x * x, axis=-1, keepdims=True) + EPS)
    return x * r * g


def _bdot(a, b):
    return jnp.dot(a.astype(jnp.bfloat16), b.astype(jnp.bfloat16), preferred_element_type=jnp.float32)


def _swiglu_chunk(xn, wg_ref, wu_ref, wd_ref, j):
    g = jnp.dot(xn, wg_ref[:, j:j + FFN_TF], preferred_element_type=jnp.float32)
    u = jnp.dot(xn, wu_ref[:, j:j + FFN_TF], preferred_element_type=jnp.float32)
    a = (g * jax.nn.sigmoid(g)) * u
    return jnp.dot(a.astype(jnp.bfloat16), wd_ref[j:j + FFN_TF, :], preferred_element_type=jnp.float32)


def _swiglu(xn, wg_ref, wu_ref, wd_ref):
    acc = jnp.zeros((xn.shape[0], D_MODEL), jnp.float32)
    for j in range(0, D_FF, FFN_TF):
        acc = acc + _swiglu_chunk(xn, wg_ref, wu_ref, wd_ref, j)
    return acc


def _ffn1_rows(x, n1_ref, wg_ref, wu_ref, wd_ref, nm_ref, win_ref):
    xn = _rms(x, n1_ref[...]).astype(jnp.bfloat16)
    h = x + 0.5 * _swiglu(xn, wg_ref, wu_ref, wd_ref)
    hn = _rms(h, nm_ref[...]).astype(jnp.bfloat16)
    return h, jnp.dot(hn, win_ref[...], preferred_element_type=jnp.float32)


def _const_spec(shape):
    return pl.BlockSpec(shape, lambda i: (0,) * len(shape), pipeline_mode=pl.Buffered(1))


def _glu_tail(y, u, d_ref, wglu_ref, bglu_ref):
    y = y + d_ref[...] * u
    z = jax.nn.gelu(y)
    gate = jax.nn.sigmoid(_bdot(z, wglu_ref[...]) + bglu_ref[...])
    return z * gate


def _discretize_kernel(lr_ref, li_ref, ldt_ref, bre_ref, bim_ref, are_ref, aim_ref, bb_ref):
    lr, li = lr_ref[...], li_ref[...]
    dt = jnp.exp(ldt_ref[...])
    mag = jnp.exp(lr * dt)
    ab_re = mag * jnp.cos(li * dt)
    ab_im = mag * jnp.sin(li * dt)
    den = lr * lr + li * li
    nr = ab_re - 1.0
    ni = ab_im
    coef_re = (nr * lr + ni * li) / den
    coef_im = (ni * lr - nr * li) / den
    are_ref[...] = ab_re
    aim_ref[...] = ab_im
    for k in range(N_BLK):
        cr = coef_re[k:k + 1, :]
        ci = coef_im[k:k + 1, :]
        bre, bim = bre_ref[k], bim_ref[k]
        bb_ref[k, :, 0:BLK_STATE] = (cr * bre - ci * bim).astype(bb_ref.dtype)
        bb_ref[k, :, BLK_STATE:] = (cr * bim + ci * bre).astype(bb_ref.dtype)


def _discretize(lr, li, ldt, bre, bim):
    return pl.pallas_call(
        _discretize_kernel,
        out_shape=[
            jax.ShapeDtypeStruct((N_BLK, BLK_STATE), jnp.float32),
            jax.ShapeDtypeStruct((N_BLK, BLK_STATE), jnp.float32),
            jax.ShapeDtypeStruct((N_BLK, LANES, 2 * BLK_STATE), jnp.bfloat16),
        ],
        name="ssm_discretize",
    )(lr, li, ldt, bre, bim)


def _ffn1_mixer_kernel(x_ref, n1_ref, wg_ref, wu_ref, wd_ref, nm_ref, win_ref,
                       are_ref, aim_ref, bb_ref, cre_ref, cim_ref, d_ref, wglu_ref, bglu_ref, cw_ref,
                       h_ref, mix_ref, sre_ref, sim_ref, nconv_ref,
                       xs0_ref, xs1_ref, xs2_ref, xs3_ref, u_ref, y_ref, v_ref, g_ref, state_ref):
    xs_refs = (xs0_ref, xs1_ref, xs2_ref, xs3_ref)
    step = pl.program_id(0)
    nb, t_len, _ = x_ref.shape

    @pl.when(step == 0)
    def _init():
        state_ref[...] = jnp.zeros_like(state_ref)
        u_ref[...] = jnp.zeros_like(u_ref)
        v_ref[...] = jnp.zeros_like(v_ref)
        g_ref[...] = jnp.zeros_like(g_ref)

    tiles = [(k, j) for k in range(N_BLK) for j in range(HALF)]
    carry = {}

    def bbar_u(ks):
        for k in ks:
            u_bf = u_ref[:, k * LANES:(k + 1) * LANES].astype(jnp.bfloat16)
            bu = jnp.dot(u_bf, bb_ref[k], preferred_element_type=jnp.float32)
            for j in range(SLABS_PER_BLK):
                xs_refs[k][j] = bu[:, j * LANES:(j + 1) * LANES]

    def scan(t0, t1):
        if t0 == 0:
            carry["re"] = [state_ref[k * SLABS_PER_BLK + j] for k, j in tiles]
            carry["im"] = [state_ref[k * SLABS_PER_BLK + HALF + j] for k, j in tiles]
        xr, xi = carry["re"], carry["im"]
        for t in range(t0, t1):
            rows = pl.ds(t, SUBLANES, stride=PITCH)
            for n, (k, j) in enumerate(tiles):
                a_re = are_ref[k * HALF + j]
                a_im = aim_ref[k * HALF + j]
                nr = (a_re * xr[n] + xs_refs[k][j, rows, :]) - a_im * xi[n]
                ni = (a_re * xi[n] + xs_refs[k][HALF + j, rows, :]) + a_im * xr[n]
                xs_refs[k][j, rows, :] = nr
                xs_refs[k][HALF + j, rows, :] = ni
                xr[n], xi[n] = nr, ni
        if t1 == t_len:
            for n, (k, j) in enumerate(tiles):
                state_ref[k * SLABS_PER_BLK + j] = xr[n]
                state_ref[k * SLABS_PER_BLK + HALF + j] = xi[n]

    def c_proj(ks):
        for k in ks:
            x_re = jnp.concatenate([xs_refs[k][j] for j in range(HALF)], axis=1)
            x_im = jnp.concatenate([xs_refs[k][HALF + j] for j in range(HALF)], axis=1)
            y_ref[:, k * LANES:(k + 1) * LANES] = _bdot(x_re, cre_ref[k]) - _bdot(x_im, cim_ref[k])

    def glu():
        y_ref[...] = _glu_tail(y_ref[...], u_ref[...], d_ref, wglu_ref, bglu_ref)

    def emit_ssm():
        for b in range(nb):
            mix_ref[b, :, 0:SSM_WIDTH] = y_ref[pl.ds(b * PITCH, t_len), :].astype(mix_ref.dtype)

    def conv():
        v = v_ref[:, SUBLANES:SUBLANES + t_len, :]
        z = cw_ref[0:1, :] * v_ref[:, SUBLANES - 2:SUBLANES - 2 + t_len, :]
        z = z + cw_ref[1:2, :] * v_ref[:, SUBLANES - 1:SUBLANES - 1 + t_len, :]
        z = z + cw_ref[2:3, :] * v
        mix_ref[:, :, SSM_WIDTH:] = (g_ref[...] * z).astype(mix_ref.dtype)
        v_ref[:, 0:SUBLANES, :] = v_ref[:, t_len:t_len + SUBLANES, :]

    q = t_len // 4
    pieces = [
        lambda: (conv(), bbar_u((0,))), lambda: bbar_u((1,)), lambda: bbar_u((2,)), lambda: bbar_u((3,)),
        lambda: scan(0, q), lambda: scan(q, 2 * q), lambda: scan(2 * q, 3 * q), lambda: scan(3 * q, t_len),
        lambda: c_proj((0, 1)), lambda: c_proj((2, 3)),
        lambda: (glu(), emit_ssm()),
    ]

    x = x_ref[...].reshape(nb * t_len, D_MODEL)
    xn = _rms(x, n1_ref[...]).astype(jnp.bfloat16)
    acc = jnp.zeros((nb * t_len, D_MODEL), jnp.float32)
    for c, j in enumerate(range(0, D_FF, FFN_TF)):
        acc = acc + _swiglu_chunk(xn, wg_ref, wu_ref, wd_ref, j)
        if c < len(pieces):
            pieces[c]()
    assert len(pieces) <= D_FF // FFN_TF
    h = x + 0.5 * acc
    hn = _rms(h, nm_ref[...]).astype(jnp.bfloat16)
    p = jnp.dot(hn, win_ref[...], preferred_element_type=jnp.float32)
    h_ref[...] = h.reshape(nb, t_len, D_MODEL)
    v_new = p[:, SSM_WIDTH + 2 * CONV_WIDTH:] * p[:, SSM_WIDTH:SSM_WIDTH + CONV_WIDTH]
    v_ref[:, SUBLANES:SUBLANES + t_len, :] = v_new.reshape(nb, t_len, CONV_WIDTH)
    g_ref[...] = p[:, SSM_WIDTH + CONV_WIDTH:SSM_WIDTH + 2 * CONV_WIDTH].reshape(nb, t_len, CONV_WIDTH)
    for b in range(nb):
        u_ref[pl.ds(b * PITCH, t_len), :] = p[b * t_len:(b + 1) * t_len, 0:SSM_WIDTH]

    @pl.when(step == pl.num_programs(0) - 1)
    def _final():
        for n, (k, j) in enumerate(tiles):
            lanes = slice(k * BLK_STATE + j * LANES, k * BLK_STATE + (j + 1) * LANES)
            sre_ref[:, lanes] = state_ref[k * SLABS_PER_BLK + j]
            sim_ref[:, lanes] = state_ref[k * SLABS_PER_BLK + HALF + j]
        nconv_ref[...] = v_ref[:, SUBLANES - 2:SUBLANES, :]


def _ffn1_mixer(x, n1, wg, wu, wd, nm, win, a_re, a_im, bb, cre, cim, d, wglu, bglu, cw):
    nb, seq, _ = x.shape
    n_chunk = seq // SCAN_T
    rows = nb * PITCH
    cur = lambda i: (0, jnp.minimum(i, n_chunk - 1), 0)
    prev = lambda i: (0, jnp.maximum(i - 1, 0), 0)
    return pl.pallas_call(
        _ffn1_mixer_kernel,
        grid=(n_chunk + 1,),
        in_specs=[
            pl.BlockSpec((nb, SCAN_T, D_MODEL), cur),
            _const_spec((1, D_MODEL)),
            _const_spec((D_MODEL, D_FF)),
            _const_spec((D_MODEL, D_FF)),
            _const_spec((D_FF, D_MODEL)),
            _const_spec((1, D_MODEL)),
            _const_spec((D_MODEL, IN_PROJ_WIDTH)),
            _const_spec((N_TILE, SUBLANES, LANES)),
            _const_spec((N_TILE, SUBLANES, LANES)),
            _const_spec((N_BLK, LANES, 2 * BLK_STATE)),
            _const_spec((N_BLK, BLK_STATE, LANES)),
            _const_spec((N_BLK, BLK_STATE, LANES)),
            _const_spec((1, SSM_WIDTH)),
            _const_spec((SSM_WIDTH, SSM_WIDTH)),
            _const_spec((1, SSM_WIDTH)),
            _const_spec((CONV_K, CONV_WIDTH)),
        ],
        out_specs=[
            pl.BlockSpec((nb, SCAN_T, D_MODEL), cur),
            pl.BlockSpec((nb, SCAN_T, D_MODEL), prev),
            pl.BlockSpec((nb, N_STATE), lambda i: (0, 0)),
            pl.BlockSpec((nb, N_STATE), lambda i: (0, 0)),
            pl.BlockSpec((nb, CONV_K - 1, CONV_WIDTH), lambda i: (0, 0, 0)),
        ],
        out_shape=[
            jax.ShapeDtypeStruct((nb, seq, D_MODEL), jnp.float32),
            jax.ShapeDtypeStruct((nb, seq, D_MODEL), jnp.bfloat16),
            jax.ShapeDtypeStruct((nb, N_STATE), jnp.float32),
            jax.ShapeDtypeStruct((nb, N_STATE), jnp.float32),
            jax.ShapeDtypeStruct((nb, CONV_K - 1, CONV_WIDTH), jnp.float32),
        ],
        scratch_shapes=[
            *[pltpu.VMEM((SLABS_PER_BLK, rows, LANES), jnp.float32) for _ in range(N_BLK)],
            pltpu.VMEM((rows, SSM_WIDTH), jnp.float32),
            pltpu.VMEM((rows, SSM_WIDTH), jnp.float32),
            pltpu.VMEM((nb, SCAN_T + SUBLANES, CONV_WIDTH), jnp.float32),
            pltpu.VMEM((nb, SCAN_T, CONV_WIDTH), jnp.float32),
            pltpu.VMEM((2 * N_TILE, SUBLANES, LANES), jnp.float32),
        ],
        compiler_params=pltpu.CompilerParams(
            dimension_semantics=("arbitrary",), vmem_limit_bytes=VMEM_LIMIT),
        name="ffn1_mixer",
    )(x, n1, wg, wu, wd, nm, win, a_re, a_im, bb, cre, cim, d, wglu, bglu, cw)


def _ffn1_kernel(x_ref, n1_ref, wg_ref, wu_ref, wd_ref, nm_ref, win_ref, h_ref, p_ref):
    h, p = _ffn1_rows(x_ref[...], n1_ref, wg_ref, wu_ref, wd_ref, nm_ref, win_ref)
    h_ref[...] = h
    p_ref[...] = p


def _ffn1(x, n1, wg, wu, wd, nm, win):
    rows = x.shape[0]
    return pl.pallas_call(
        _ffn1_kernel,
        out_shape=[
            jax.ShapeDtypeStruct((rows, D_MODEL), jnp.float32),
            jax.ShapeDtypeStruct((rows, IN_PROJ_WIDTH), jnp.float32),
        ],
        compiler_params=pltpu.CompilerParams(vmem_limit_bytes=VMEM_LIMIT),
        name="ffn1_inproj",
    )(x, n1, wg, wu, wd, nm, win)


def _ffn2_kernel(h_ref, mix_ref, wout_ref, n2_ref, wg_ref, wu_ref, wd_ref, nf_ref, y_ref):
    h = h_ref[...] + jnp.dot(mix_ref[...], wout_ref[...], preferred_element_type=jnp.float32)
    hn = _rms(h, n2_ref[...]).astype(jnp.bfloat16)
    h = h + 0.5 * _swiglu(hn, wg_ref, wu_ref, wd_ref)
    y_ref[...] = _rms(h, nf_ref[...])


def _ffn2(h, mix, wout, n2, wg, wu, wd, nf, tm):
    rows = h.shape[0]
    return pl.pallas_call(
        _ffn2_kernel,
        grid=(rows // tm,),
        in_specs=[
            pl.BlockSpec((tm, D_MODEL), lambda i: (i, 0)),
            pl.BlockSpec((tm, D_MODEL), lambda i: (i, 0)),
            _const_spec((D_MODEL, D_MODEL)),
            _const_spec((1, D_MODEL)),
            _const_spec((D_MODEL, D_FF)),
            _const_spec((D_MODEL, D_FF)),
            _const_spec((D_FF, D_MODEL)),
            _const_spec((1, D_MODEL)),
        ],
        out_specs=pl.BlockSpec((tm, D_MODEL), lambda i: (i, 0)),
        out_shape=jax.ShapeDtypeStruct((rows, D_MODEL), jnp.float32),
        compiler_params=pltpu.CompilerParams(
            dimension_semantics=("arbitrary",), vmem_limit_bytes=VMEM_LIMIT),
        name="outproj_ffn2",
    )(h, mix, wout, n2, wg, wu, wd, nf)


def _mixer_step_kernel(p_ref, hre_ref, him_ref, buf_ref, are_ref, aim_ref, bb_ref, cre_ref, cim_ref,
                       d_ref, wglu_ref, bglu_ref, cw_ref, mix_ref, sre_ref, sim_ref, nconv_ref):
    u = p_ref[:, 0:SSM_WIDTH]
    u_bf = u.astype(jnp.bfloat16)
    ys = []
    for k in range(N_BLK):
        bu = jnp.dot(u_bf[:, k * LANES:(k + 1) * LANES], bb_ref[k], preferred_element_type=jnp.float32)
        st = slice(k * BLK_STATE, (k + 1) * BLK_STATE)
        a_re = are_ref[k:k + 1, :]
        a_im = aim_ref[k:k + 1, :]
        h_re = hre_ref[:, st]
        h_im = him_ref[:, st]
        x_re = (a_re * h_re + bu[:, :BLK_STATE]) - a_im * h_im
        x_im = (a_re * h_im + bu[:, BLK_STATE:]) + a_im * h_re
        sre_ref[:, st] = x_re
        sim_ref[:, st] = x_im
        ys.append(_bdot(x_re, cre_ref[k]) - _bdot(x_im, cim_ref[k]))
    y = jnp.concatenate(ys, axis=1)
    mix_ref[:, 0:SSM_WIDTH] = _glu_tail(y, u, d_ref, wglu_ref, bglu_ref).astype(mix_ref.dtype)

    x_conv = p_ref[:, SSM_WIDTH:SSM_WIDTH + CONV_WIDTH]
    gate_b = p_ref[:, SSM_WIDTH + CONV_WIDTH:SSM_WIDTH + 2 * CONV_WIDTH]
    gate_c = p_ref[:, SSM_WIDTH + 2 * CONV_WIDTH:]
    v = gate_c * x_conv
    z = cw_ref[0:1, :] * buf_ref[0] + cw_ref[1:2, :] * buf_ref[1] + cw_ref[2:3, :] * v
    mix_ref[:, SSM_WIDTH:] = (gate_b * z).astype(mix_ref.dtype)
    nconv_ref[0] = buf_ref[1]
    nconv_ref[1] = v


def _mixer_step(p, h_re, h_im, buf, a_re, a_im, bb, cre, cim, d, wglu, bglu, cw):
    nb = p.shape[0]
    return pl.pallas_call(
        _mixer_step_kernel,
        out_shape=[
            jax.ShapeDtypeStruct((nb, D_MODEL), jnp.bfloat16),
            jax.ShapeDtypeStruct((nb, N_STATE), jnp.float32),
            jax.ShapeDtypeStruct((nb, N_STATE), jnp.float32),
            jax.ShapeDtypeStruct((CONV_K - 1, nb, CONV_WIDTH), jnp.float32),
        ],
        compiler_params=pltpu.CompilerParams(vmem_limit_bytes=VMEM_LIMIT),
        name="mixer_step",
    )(p, h_re, h_im, buf, a_re, a_im, bb, cre, cim, d, wglu, bglu, cw)


def _ssm_layout(lam_re, lam_im, log_dt, b_re, b_im, c_re, c_im):
    gpb = N_SSM_GROUPS // N_BLK
    eye = jnp.eye(gpb, dtype=jnp.float32)
    lr = lam_re.reshape(N_BLK, BLK_STATE)
    li = lam_im.reshape(N_BLK, BLK_STATE)
    ldt = jnp.broadcast_to(log_dt[:, None], (N_SSM_GROUPS, SSM_STATE)).reshape(N_BLK, BLK_STATE)

    def b_blocks(b):
        b = b.reshape(N_BLK, gpb, SSM_STATE, SSM_GROUP).transpose(0, 1, 3, 2)
        return jnp.einsum("kgcp,gh->kgchp", b, eye).reshape(N_BLK, LANES, BLK_STATE)

    def c_blocks(c):
        c = c.reshape(N_BLK, gpb, SSM_GROUP, SSM_STATE).transpose(0, 1, 3, 2)
        return jnp.einsum("kgpc,gh->kgphc", c, eye).reshape(N_BLK, BLK_STATE, LANES).astype(jnp.bfloat16)

    return lr, li, ldt, b_blocks(b_re), b_blocks(b_im), c_blocks(c_re), c_blocks(c_im)


def kernel(x_prompt, x_sample, state_ssm_re, state_ssm_im, state_conv, ffn1_norm, ffn1_w_gate, ffn1_w_up, ffn1_w_down, mix_norm, w_in, ssm_lambda_re, ssm_lambda_im, ssm_log_dt, ssm_b_re, ssm_b_im, ssm_c_re, ssm_c_im, ssm_d, ssm_w_glu, ssm_b_glu, conv_w, w_out, ffn2_norm, ffn2_w_gate, ffn2_w_up, ffn2_w_down, final_norm):
    assert ffn1_norm.shape[0] == 1, "single-layer trunk"
    bf = jnp.bfloat16
    n_prompt, seq, _ = x_prompt.shape
    n_dec, dec_seq, _ = x_sample.shape
    assert dec_seq == 1 and n_prompt == SUBLANES and seq % SCAN_T == 0 and (n_prompt * seq) % FFN_TM == 0

    n1 = ffn1_norm[0][None, :]
    nm = mix_norm[0][None, :]
    n2 = ffn2_norm[0][None, :]
    nf = final_norm[None, :]
    w1g, w1u, w1d = ffn1_w_gate[0].astype(bf), ffn1_w_up[0].astype(bf), ffn1_w_down[0].astype(bf)
    w2g, w2u, w2d = ffn2_w_gate[0].astype(bf), ffn2_w_up[0].astype(bf), ffn2_w_down[0].astype(bf)
    win = w_in[0].astype(bf)
    wout = w_out[0].astype(bf)
    wglu = ssm_w_glu[0].astype(bf)
    lr, li, ldt, bre, bim, cre, cim = _ssm_layout(
        ssm_lambda_re[0], ssm_lambda_im[0], ssm_log_dt[0], ssm_b_re[0], ssm_b_im[0], ssm_c_re[0], ssm_c_im[0])
    a_re, a_im, bb = _discretize(lr, li, ldt, bre, bim)
    a_re_t = jnp.broadcast_to(a_re.reshape(N_TILE, 1, LANES), (N_TILE, SUBLANES, LANES))
    a_im_t = jnp.broadcast_to(a_im.reshape(N_TILE, 1, LANES), (N_TILE, SUBLANES, LANES))
    d = ssm_d[0][None, :]
    bglu = ssm_b_glu[0][None, :]
    cw = conv_w[0]

    hp, mixp, p_re, p_im, p_conv = _ffn1_mixer(
        x_prompt, n1, w1g, w1u, w1d, nm, win, a_re_t, a_im_t, bb, cre, cim, d, wglu, bglu, cw)
    yp = _ffn2(hp.reshape(n_prompt * seq, D_MODEL), mixp.reshape(n_prompt * seq, D_MODEL),
               wout, n2, w2g, w2u, w2d, nf, FFN_TM)

    xs = x_sample.reshape(n_dec, D_MODEL)
    hs, ps = _ffn1(xs, n1, w1g, w1u, w1d, nm, win)
    s_re0 = state_ssm_re[0].reshape(n_dec, N_STATE)
    s_im0 = state_ssm_im[0].reshape(n_dec, N_STATE)
    buf = state_conv[0].transpose(1, 0, 2)
    mixs, s_re, s_im, s_conv = _mixer_step(ps, s_re0, s_im0, buf, a_re, a_im, bb, cre, cim, d, wglu, bglu, cw)
    ys = _ffn2(hs, mixs, wout, n2, w2g, w2u, w2d, nf, n_dec)

    gs = (N_SSM_GROUPS, SSM_STATE)
    return (yp.reshape(n_prompt, seq, D_MODEL),
            ys.reshape(n_dec, 1, D_MODEL),
            p_re.reshape(1, n_prompt, *gs),
            p_im.reshape(1, n_prompt, *gs),
            p_conv[None],
            s_re.reshape(1, n_dec, *gs),
            s_im.reshape(1, n_dec, *gs),
            s_conv.transpose(1, 0, 2)[None])
```

```python
import jax
import jax.numpy as jnp
from jax import lax
from jax.experimental import pallas as pl
from jax.experimental.pallas import tpu as pltpu

D_MODEL = 1024
D_FF = 2816
SSM_WIDTH = 512
CONV_WIDTH = 512
SSM_GROUP = 16
N_SSM_GROUPS = 32
SSM_STATE = 64
CONV_K = 3
IN_PROJ_WIDTH = SSM_WIDTH + 3 * CONV_WIDTH
EPS = 1e-6

LANES = 128
SUBLANES = 8
N_STATE = N_SSM_GROUPS * SSM_STATE
N_BLK = SSM_WIDTH // LANES
BLK_STATE = N_STATE // N_BLK
SLABS_PER_BLK = 2 * BLK_STATE // LANES
HALF = SLABS_PER_BLK // 2
N_TILE = N_BLK * HALF

FFN_TM = 1024
FFN_SUB = 512
W2_STEPS = 8
FFN_TF = 256
SCAN_T = 64
PITCH = SCAN_T + 4
VMEM_LIMIT = 58 * 1024 * 1024


def _rms(x, g):
    r = lax.rsqrt(jnp.mean(x * x, axis=-1, keepdims=True) + EPS)
    return x * r * g


def _bdot(a, b):
    return jnp.dot(a.astype(jnp.bfloat16), b.astype(jnp.bfloat16), preferred_element_type=jnp.float32)


def _swiglu_chunk(xn, wg_ref, wu_ref, wd_ref, j):
    g = jnp.dot(xn, wg_ref[:, j:j + FFN_TF], preferred_element_type=jnp.float32)
    u = jnp.dot(xn, wu_ref[:, j:j + FFN_TF], preferred_element_type=jnp.float32)
    a = (g * jax.nn.sigmoid(g)) * u
    return jnp.dot(a.astype(jnp.bfloat16), wd_ref[j:j + FFN_TF, :], preferred_element_type=jnp.float32)


def _swiglu(xn, wg_ref, wu_ref, wd_ref):
    acc = jnp.zeros((xn.shape[0], D_MODEL), jnp.float32)
    for j in range(0, D_FF, FFN_TF):
        acc = acc + _swiglu_chunk(xn, wg_ref, wu_ref, wd_ref, j)
    return acc


def _ffn1_rows(x, n1_ref, wg_ref, wu_ref, wd_ref, nm_ref, win_ref):
    xn = _rms(x, n1_ref[...]).astype(jnp.bfloat16)
    h = x + 0.5 * _swiglu(xn, wg_ref, wu_ref, wd_ref)
    hn = _rms(h, nm_ref[...]).astype(jnp.bfloat16)
    return h, jnp.dot(hn, win_ref[...], preferred_element_type=jnp.float32)


def _const_spec(shape):
    return pl.BlockSpec(shape, lambda i: (0,) * len(shape), pipeline_mode=pl.Buffered(1))


def _glu_tail(y, u, d_ref, wglu_ref, bglu_ref):
    y = y + d_ref[...] * u
    z = jax.nn.gelu(y)
    gate = jax.nn.sigmoid(_bdot(z, wglu_ref[...]) + bglu_ref[...])
    return z * gate


def _discretize_kernel(lr_ref, li_ref, ldt_ref, bre_ref, bim_ref, are_ref, aim_ref, bb_ref):
    lr, li = lr_ref[...], li_ref[...]
    dt = jnp.exp(ldt_ref[...])
    mag = jnp.exp(lr * dt)
    ab_re = mag * jnp.cos(li * dt)
    ab_im = mag * jnp.sin(li * dt)
    den = lr * lr + li * li
    nr = ab_re - 1.0
    ni = ab_im
    coef_re = (nr * lr + ni * li) / den
    coef_im = (ni * lr - nr * li) / den
    are_ref[...] = ab_re
    aim_ref[...] = ab_im
    for k in range(N_BLK):
        cr = coef_re[k:k + 1, :]
        ci = coef_im[k:k + 1, :]
        bre, bim = bre_ref[k], bim_ref[k]
        bb_ref[k, :, 0:BLK_STATE] = (cr * bre - ci * bim).astype(bb_ref.dtype)
        bb_ref[k, :, BLK_STATE:] = (cr * bim + ci * bre).astype(bb_ref.dtype)


def _discretize(lr, li, ldt, bre, bim):
    return pl.pallas_call(
        _discretize_kernel,
        out_shape=[
            jax.ShapeDtypeStruct((N_BLK, BLK_STATE), jnp.float32),
            jax.ShapeDtypeStruct((N_BLK, BLK_STATE), jnp.float32),
            jax.ShapeDtypeStruct((N_BLK, LANES, 2 * BLK_STATE), jnp.bfloat16),
        ],
        name="ssm_discretize",
    )(lr, li, ldt, bre, bim)


def _ffn1_mixer_kernel(x_ref, n1_ref, wg_ref, wu_ref, wd_ref, nm_ref, win_ref,
                       are_ref, aim_ref, bb_ref, cre_ref, cim_ref, d_ref, wglu_ref, bglu_ref, cw_ref,
                       h_ref, mix_ref, sre_ref, sim_ref, nconv_ref,
                       xs0_ref, xs1_ref, xs2_ref, xs3_ref, u_ref, y_ref, v_ref, g_ref, state_ref):
    xs_refs = (xs0_ref, xs1_ref, xs2_ref, xs3_ref)
    step = pl.program_id(0)
    nb, t_len, _ = x_ref.shape

    @pl.when(step == 0)
    def _init():
        state_ref[...] = jnp.zeros_like(state_ref)
        u_ref[...] = jnp.zeros_like(u_ref)
        v_ref[...] = jnp.zeros_like(v_ref)
        g_ref[...] = jnp.zeros_like(g_ref)

    tiles = [(k, j) for k in range(N_BLK) for j in range(HALF)]
    carry = {}

    def bbar_u(ks):
        for k in ks:
            u_bf = u_ref[:, k * LANES:(k + 1) * LANES].astype(jnp.bfloat16)
            bu = jnp.dot(u_bf, bb_ref[k], preferred_element_type=jnp.float32)
            for j in range(SLABS_PER_BLK):
                xs_refs[k][j] = bu[:, j * LANES:(j + 1) * LANES]

    def scan(t0, t1):
        if t0 == 0:
            carry["re"] = [state_ref[k * SLABS_PER_BLK + j] for k, j in tiles]
            carry["im"] = [state_ref[k * SLABS_PER_BLK + HALF + j] for k, j in tiles]
        xr, xi = carry["re"], carry["im"]
        for t in range(t0, t1):
            rows = pl.ds(t, SUBLANES, stride=PITCH)
            for n, (k, j) in enumerate(tiles):
                a_re = are_ref[k * HALF + j]
                a_im = aim_ref[k * HALF + j]
                nr = (a_re * xr[n] + xs_refs[k][j, rows, :]) - a_im * xi[n]
                ni = (a_re * xi[n] + xs_refs[k][HALF + j, rows, :]) + a_im * xr[n]
                xs_refs[k][j, rows, :] = nr
                xs_refs[k][HALF + j, rows, :] = ni
                xr[n], xi[n] = nr, ni
        if t1 == t_len:
            for n, (k, j) in enumerate(tiles):
                state_ref[k * SLABS_PER_BLK + j] = xr[n]
                state_ref[k * SLABS_PER_BLK + HALF + j] = xi[n]

    def c_proj(ks):
        for k in ks:
            x_re = jnp.concatenate([xs_refs[k][j] for j in range(HALF)], axis=1)
            x_im = jnp.concatenate([xs_refs[k][HALF + j] for j in range(HALF)], axis=1)
            y_ref[:, k * LANES:(k + 1) * LANES] = _bdot(x_re, cre_ref[k]) - _bdot(x_im, cim_ref[k])

    def glu():
        y_ref[...] = _glu_tail(y_ref[...], u_ref[...], d_ref, wglu_ref, bglu_ref)

    def emit_ssm():
        for b in range(nb):
            mix_ref[b, :, 0:SSM_WIDTH] = y_ref[pl.ds(b * PITCH, t_len), :].astype(mix_ref.dtype)

    def conv():
        v = v_ref[:, SUBLANES:SUBLANES + t_len, :]
        z = cw_ref[0:1, :] * v_ref[:, SUBLANES - 2:SUBLANES - 2 + t_len, :]
        z = z + cw_ref[1:2, :] * v_ref[:, SUBLANES - 1:SUBLANES - 1 + t_len, :]
        z = z + cw_ref[2:3, :] * v
        mix_ref[:, :, SSM_WIDTH:] = (g_ref[...] * z).astype(mix_ref.dtype)
        v_ref[:, 0:SUBLANES, :] = v_ref[:, t_len:t_len + SUBLANES, :]

    q = t_len // 4
    pieces = [
        lambda: (conv(), bbar_u((0,))), lambda: bbar_u((1,)), lambda: bbar_u((2,)), lambda: bbar_u((3,)),
        lambda: scan(0, q), lambda: scan(q, 2 * q), lambda: scan(2 * q, 3 * q), lambda: scan(3 * q, t_len),
        lambda: c_proj((0, 1)), lambda: c_proj((2, 3)),
        lambda: (glu(), emit_ssm()),
    ]

    x = x_ref[...].reshape(nb * t_len, D_MODEL)
    xn = _rms(x, n1_ref[...]).astype(jnp.bfloat16)
    acc = jnp.zeros((nb * t_len, D_MODEL), jnp.float32)
    for c, j in enumerate(range(0, D_FF, FFN_TF)):
        acc = acc + _swiglu_chunk(xn, wg_ref, wu_ref, wd_ref, j)
        if c < len(pieces):
            pieces[c]()
    assert len(pieces) <= D_FF // FFN_TF
    h = x + 0.5 * acc
    hn = _rms(h, nm_ref[...]).astype(jnp.bfloat16)
    p = jnp.dot(hn, win_ref[...], preferred_element_type=jnp.float32)
    h_ref[...] = h.reshape(nb, t_len, D_MODEL)
    v_new = p[:, SSM_WIDTH + 2 * CONV_WIDTH:] * p[:, SSM_WIDTH:SSM_WIDTH + CONV_WIDTH]
    v_ref[:, SUBLANES:SUBLANES + t_len, :] = v_new.reshape(nb, t_len, CONV_WIDTH)
    g_ref[...] = p[:, SSM_WIDTH + CONV_WIDTH:SSM_WIDTH + 2 * CONV_WIDTH].reshape(nb, t_len, CONV_WIDTH)
    for b in range(nb):
        u_ref[pl.ds(b * PITCH, t_len), :] = p[b * t_len:(b + 1) * t_len, 0:SSM_WIDTH]

    @pl.when(step == pl.num_programs(0) - 1)
    def _final():
        for n, (k, j) in enumerate(tiles):
            lanes = slice(k * BLK_STATE + j * LANES, k * BLK_STATE + (j + 1) * LANES)
            sre_ref[:, lanes] = state_ref[k * SLABS_PER_BLK + j]
            sim_ref[:, lanes] = state_ref[k * SLABS_PER_BLK + HALF + j]
        nconv_ref[...] = v_ref[:, SUBLANES - 2:SUBLANES, :]


def _ffn1_mixer(x, n1, wg, wu, wd, nm, win, a_re, a_im, bb, cre, cim, d, wglu, bglu, cw):
    nb, seq, _ = x.shape
    n_chunk = seq // SCAN_T
    rows = nb * PITCH
    cur = lambda i: (0, jnp.minimum(i, n_chunk - 1), 0)
    prev = lambda i: (0, jnp.maximum(i - 1, 0), 0)
    return pl.pallas_call(
        _ffn1_mixer_kernel,
        grid=(n_chunk + 1,),
        in_specs=[
            pl.BlockSpec((nb, SCAN_T, D_MODEL), cur),
            _const_spec((1, D_MODEL)),
            _const_spec((D_MODEL, D_FF)),
            _const_spec((D_MODEL, D_FF)),
            _const_spec((D_FF, D_MODEL)),
            _const_spec((1, D_MODEL)),
            _const_spec((D_MODEL, IN_PROJ_WIDTH)),
            _const_spec((N_TILE, SUBLANES, LANES)),
            _const_spec((N_TILE, SUBLANES, LANES)),
            _const_spec((N_BLK, LANES, 2 * BLK_STATE)),
            _const_spec((N_BLK, BLK_STATE, LANES)),
            _const_spec((N_BLK, BLK_STATE, LANES)),
            _const_spec((1, SSM_WIDTH)),
            _const_spec((SSM_WIDTH, SSM_WIDTH)),
            _const_spec((1, SSM_WIDTH)),
            _const_spec((CONV_K, CONV_WIDTH)),
        ],
        out_specs=[
            pl.BlockSpec((nb, SCAN_T, D_MODEL), cur),
            pl.BlockSpec((nb, SCAN_T, D_MODEL), prev),
            pl.BlockSpec((nb, N_STATE), lambda i: (0, 0)),
            pl.BlockSpec((nb, N_STATE), lambda i: (0, 0)),
            pl.BlockSpec((nb, CONV_K - 1, CONV_WIDTH), lambda i: (0, 0, 0)),
        ],
        out_shape=[
            jax.ShapeDtypeStruct((nb, seq, D_MODEL), jnp.float32),
            jax.ShapeDtypeStruct((nb, seq, D_MODEL), jnp.bfloat16),
            jax.ShapeDtypeStruct((nb, N_STATE), jnp.float32),
            jax.ShapeDtypeStruct((nb, N_STATE), jnp.float32),
            jax.ShapeDtypeStruct((nb, CONV_K - 1, CONV_WIDTH), jnp.float32),
        ],
        scratch_shapes=[
            *[pltpu.VMEM((SLABS_PER_BLK, rows, LANES), jnp.float32) for _ in range(N_BLK)],
            pltpu.VMEM((rows, SSM_WIDTH), jnp.float32),
            pltpu.VMEM((rows, SSM_WIDTH), jnp.float32),
            pltpu.VMEM((nb, SCAN_T + SUBLANES, CONV_WIDTH), jnp.float32),
            pltpu.VMEM((nb, SCAN_T, CONV_WIDTH), jnp.float32),
            pltpu.VMEM((2 * N_TILE, SUBLANES, LANES), jnp.float32),
        ],
        compiler_params=pltpu.CompilerParams(
            dimension_semantics=("arbitrary",), vmem_limit_bytes=VMEM_LIMIT),
        name="ffn1_mixer",
    )(x, n1, wg, wu, wd, nm, win, a_re, a_im, bb, cre, cim, d, wglu, bglu, cw)


def _ffn1_kernel(x_ref, n1_ref, wg_ref, wu_ref, wd_ref, nm_ref, win_ref, h_ref, p_ref):
    h, p = _ffn1_rows(x_ref[...], n1_ref, wg_ref, wu_ref, wd_ref, nm_ref, win_ref)
    h_ref[...] = h
    p_ref[...] = p


def _ffn1(x, n1, wg, wu, wd, nm, win):
    rows = x.shape[0]
    return pl.pallas_call(
        _ffn1_kernel,
        out_shape=[
            jax.ShapeDtypeStruct((rows, D_MODEL), jnp.float32),
            jax.ShapeDtypeStruct((rows, IN_PROJ_WIDTH), jnp.float32),
        ],
        compiler_params=pltpu.CompilerParams(vmem_limit_bytes=VMEM_LIMIT),
        name="ffn1_inproj",
    )(x, n1, wg, wu, wd, nm, win)


def _ffn2_rows(h_ref, mix_ref, y_ref, wout_ref, n2_ref, wg_ref, wu_ref, wd_ref, nf_ref):
    for r in range(0, h_ref.shape[0], FFN_SUB):
        rows = slice(r, min(r + FFN_SUB, h_ref.shape[0]))
        h = h_ref[rows, :] + jnp.dot(mix_ref[rows, :], wout_ref[...], preferred_element_type=jnp.float32)
        hn = _rms(h, n2_ref[...]).astype(jnp.bfloat16)
        h = h + 0.5 * _swiglu(hn, wg_ref, wu_ref, wd_ref)
        y_ref[rows, :] = _rms(h, nf_ref[...])


def _ffn2_kernel(h_ref, mix_ref, hs_ref, mixs_ref, wout_c, wg_c, wu_c, wd_c, n2_ref, nf_ref,
                 y_ref, ys_ref, wout_ref, wg_ref, wu_ref, wd_ref):
    step = pl.program_id(0)
    last = pl.num_programs(0) - 1

    @pl.when(step < W2_STEPS)
    def _load_weights():
        for src, dst in ((wout_c, wout_ref), (wg_c, wg_ref), (wu_c, wu_ref), (wd_c, wd_ref)):
            n = src.shape[0]
            dst[pl.ds(pl.multiple_of(step * n, n), n), :] = src[...].astype(dst.dtype)

    weights = (wout_ref, n2_ref, wg_ref, wu_ref, wd_ref, nf_ref)

    @pl.when((step >= W2_STEPS) & (step < last))
    def _prompt_tile():
        _ffn2_rows(h_ref, mix_ref, y_ref, *weights)

    @pl.when(step == last)
    def _decode_rows():
        _ffn2_rows(hs_ref, mixs_ref, ys_ref, *weights)


def _ffn2(h, mix, hs, mixs, wout, n2, wg, wu, wd, nf):
    rows = h.shape[0]
    n_tiles = rows // FFN_TM
    tile = lambda i: (jnp.clip(i - W2_STEPS, 0, n_tiles - 1), 0)
    chunk = lambda i: (jnp.minimum(i, W2_STEPS - 1), 0)
    whole = lambda shape: pl.BlockSpec(shape, lambda i: (0, 0), pipeline_mode=pl.Buffered(1))
    return pl.pallas_call(
        _ffn2_kernel,
        grid=(W2_STEPS + n_tiles + 1,),
        in_specs=[
            pl.BlockSpec((FFN_TM, D_MODEL), tile),
            pl.BlockSpec((FFN_TM, D_MODEL), tile),
            whole(hs.shape),
            whole(mixs.shape),
            pl.BlockSpec((D_MODEL // W2_STEPS, D_MODEL), chunk),
            pl.BlockSpec((D_MODEL // W2_STEPS, D_FF), chunk),
            pl.BlockSpec((D_MODEL // W2_STEPS, D_FF), chunk),
            pl.BlockSpec((D_FF // W2_STEPS, D_MODEL), chunk),
            whole((1, D_MODEL)),
            whole((1, D_MODEL)),
        ],
        out_specs=[
            pl.BlockSpec((FFN_TM, D_MODEL), tile),
            pl.BlockSpec(hs.shape, lambda i: (0, 0)),
        ],
        out_shape=[
            jax.ShapeDtypeStruct((rows, D_MODEL), jnp.float32),
            jax.ShapeDtypeStruct(hs.shape, jnp.float32),
        ],
        scratch_shapes=[
            pltpu.VMEM((D_MODEL, D_MODEL), jnp.bfloat16),
            pltpu.VMEM((D_MODEL, D_FF), jnp.bfloat16),
            pltpu.VMEM((D_MODEL, D_FF), jnp.bfloat16),
            pltpu.VMEM((D_FF, D_MODEL), jnp.bfloat16),
        ],
        compiler_params=pltpu.CompilerParams(
            dimension_semantics=("arbitrary",), vmem_limit_bytes=VMEM_LIMIT),
        name="outproj_ffn2",
    )(h, mix, hs, mixs, wout, wg, wu, wd, n2, nf)


def _mixer_step_kernel(p_ref, hre_ref, him_ref, buf_ref, are_ref, aim_ref, bb_ref, cre_ref, cim_ref,
                       d_ref, wglu_ref, bglu_ref, cw_ref, mix_ref, sre_ref, sim_ref, nconv_ref):
    u = p_ref[:, 0:SSM_WIDTH]
    u_bf = u.astype(jnp.bfloat16)
    ys = []
    for k in range(N_BLK):
        bu = jnp.dot(u_bf[:, k * LANES:(k + 1) * LANES], bb_ref[k], preferred_element_type=jnp.float32)
        st = slice(k * BLK_STATE, (k + 1) * BLK_STATE)
        a_re = are_ref[k:k + 1, :]
        a_im = aim_ref[k:k + 1, :]
        h_re = hre_ref[:, st]
        h_im = him_ref[:, st]
        x_re = (a_re * h_re + bu[:, :BLK_STATE]) - a_im * h_im
        x_im = (a_re * h_im + bu[:, BLK_STATE:]) + a_im * h_re
        sre_ref[:, st] = x_re
        sim_ref[:, st] = x_im
        ys.append(_bdot(x_re, cre_ref[k]) - _bdot(x_im, cim_ref[k]))
    y = jnp.concatenate(ys, axis=1)
    mix_ref[:, 0:SSM_WIDTH] = _glu_tail(y, u, d_ref, wglu_ref, bglu_ref).astype(mix_ref.dtype)

    x_conv = p_ref[:, SSM_WIDTH:SSM_WIDTH + CONV_WIDTH]
    gate_b = p_ref[:, SSM_WIDTH + CONV_WIDTH:SSM_WIDTH + 2 * CONV_WIDTH]
    gate_c = p_ref[:, SSM_WIDTH + 2 * CONV_WIDTH:]
    v = gate_c * x_conv
    z = cw_ref[0:1, :] * buf_ref[0] + cw_ref[1:2, :] * buf_ref[1] + cw_ref[2:3, :] * v
    mix_ref[:, SSM_WIDTH:] = (gate_b * z).astype(mix_ref.dtype)
    nconv_ref[0] = buf_ref[1]
    nconv_ref[1] = v


def _mixer_step(p, h_re, h_im, buf, a_re, a_im, bb, cre, cim, d, wglu, bglu, cw):
    nb = p.shape[0]
    return pl.pallas_call(
        _mixer_step_kernel,
        out_shape=[
            jax.ShapeDtypeStruct((nb, D_MODEL), jnp.bfloat16),
            jax.ShapeDtypeStruct((nb, N_STATE), jnp.float32),
            jax.ShapeDtypeStruct((nb, N_STATE), jnp.float32),
            jax.ShapeDtypeStruct((CONV_K - 1, nb, CONV_WIDTH), jnp.float32),
        ],
        compiler_params=pltpu.CompilerParams(vmem_limit_bytes=VMEM_LIMIT),
        name="mixer_step",
    )(p, h_re, h_im, buf, a_re, a_im, bb, cre, cim, d, wglu, bglu, cw)


def _ssm_layout(lam_re, lam_im, log_dt, b_re, b_im, c_re, c_im):
    gpb = N_SSM_GROUPS // N_BLK
    eye = jnp.eye(gpb, dtype=jnp.float32)
    lr = lam_re.reshape(N_BLK, BLK_STATE)
    li = lam_im.reshape(N_BLK, BLK_STATE)
    ldt = jnp.broadcast_to(log_dt[:, None], (N_SSM_GROUPS, SSM_STATE)).reshape(N_BLK, BLK_STATE)

    def b_blocks(b):
        b = b.reshape(N_BLK, gpb, SSM_STATE, SSM_GROUP).transpose(0, 1, 3, 2)
        return jnp.einsum("kgcp,gh->kgchp", b, eye).reshape(N_BLK, LANES, BLK_STATE)

    def c_blocks(c):
        c = c.reshape(N_BLK, gpb, SSM_GROUP, SSM_STATE).transpose(0, 1, 3, 2)
        return jnp.einsum("kgpc,gh->kgphc", c, eye).reshape(N_BLK, BLK_STATE, LANES).astype(jnp.bfloat16)

    return lr, li, ldt, b_blocks(b_re), b_blocks(b_im), c_blocks(c_re), c_blocks(c_im)


def kernel(x_prompt, x_sample, state_ssm_re, state_ssm_im, state_conv, ffn1_norm, ffn1_w_gate, ffn1_w_up, ffn1_w_down, mix_norm, w_in, ssm_lambda_re, ssm_lambda_im, ssm_log_dt, ssm_b_re, ssm_b_im, ssm_c_re, ssm_c_im, ssm_d, ssm_w_glu, ssm_b_glu, conv_w, w_out, ffn2_norm, ffn2_w_gate, ffn2_w_up, ffn2_w_down, final_norm):
    assert ffn1_norm.shape[0] == 1, "single-layer trunk"
    bf = jnp.bfloat16
    n_prompt, seq, _ = x_prompt.shape
    n_dec, dec_seq, _ = x_sample.shape
    assert dec_seq == 1 and n_prompt == SUBLANES and seq % SCAN_T == 0 and (n_prompt * seq) % FFN_TM == 0

    n1 = ffn1_norm[0][None, :]
    nm = mix_norm[0][None, :]
    n2 = ffn2_norm[0][None, :]
    nf = final_norm[None, :]
    w1g, w1u, w1d = ffn1_w_gate[0].astype(bf), ffn1_w_up[0].astype(bf), ffn1_w_down[0].astype(bf)
    win = w_in[0].astype(bf)
    wglu = ssm_w_glu[0].astype(bf)
    lr, li, ldt, bre, bim, cre, cim = _ssm_layout(
        ssm_lambda_re[0], ssm_lambda_im[0], ssm_log_dt[0], ssm_b_re[0], ssm_b_im[0], ssm_c_re[0], ssm_c_im[0])
    a_re, a_im, bb = _discretize(lr, li, ldt, bre, bim)
    a_re_t = jnp.broadcast_to(a_re.reshape(N_TILE, 1, LANES), (N_TILE, SUBLANES, LANES))
    a_im_t = jnp.broadcast_to(a_im.reshape(N_TILE, 1, LANES), (N_TILE, SUBLANES, LANES))
    d = ssm_d[0][None, :]
    bglu = ssm_b_glu[0][None, :]
    cw = conv_w[0]

    hp, mixp, p_re, p_im, p_conv = _ffn1_mixer(
        x_prompt, n1, w1g, w1u, w1d, nm, win, a_re_t, a_im_t, bb, cre, cim, d, wglu, bglu, cw)

    xs = x_sample.reshape(n_dec, D_MODEL)
    hs, ps = _ffn1(xs, n1, w1g, w1u, w1d, nm, win)
    s_re0 = state_ssm_re[0].reshape(n_dec, N_STATE)
    s_im0 = state_ssm_im[0].reshape(n_dec, N_STATE)
    buf = state_conv[0].transpose(1, 0, 2)
    mixs, s_re, s_im, s_conv = _mixer_step(ps, s_re0, s_im0, buf, a_re, a_im, bb, cre, cim, d, wglu, bglu, cw)

    yp, ys = _ffn2(hp.reshape(n_prompt * seq, D_MODEL), mixp.reshape(n_prompt * seq, D_MODEL), hs, mixs,
                   w_out[0], n2, ffn2_w_gate[0], ffn2_w_up[0], ffn2_w_down[0], nf)

    gs = (N_SSM_GROUPS, SSM_STATE)
    return (yp.reshape(n_prompt, seq, D_MODEL),
            ys.reshape(n_dec, 1, D_MODEL),
            p_re.reshape(1, n_prompt, *gs),
            p_im.reshape(1, n_prompt, *gs),
            p_conv[None],
            s_re.reshape(1, n_dec, *gs),
            s_im.reshape(1, n_dec, *gs),
            s_conv.transpose(1, 0, 2)[None])
```

```python
import jax
import jax.numpy as jnp
from jax import lax
from jax.experimental import pallas as pl
from jax.experimental.pallas import tpu as pltpu

D_MODEL = 1024
D_FF = 2816
SSM_WIDTH = 512
CONV_WIDTH = 512
SSM_GROUP = 16
N_SSM_GROUPS = 32
SSM_STATE = 64
CONV_K = 3
IN_PROJ_WIDTH = SSM_WIDTH + 3 * CONV_WIDTH
EPS = 1e-6

LANES = 128
SUBLANES = 8
N_STATE = N_SSM_GROUPS * SSM_STATE
N_BLK = SSM_WIDTH // LANES
BLK_STATE = N_STATE // N_BLK
SLABS_PER_BLK = 2 * BLK_STATE // LANES
HALF = SLABS_PER_BLK // 2
N_TILE = N_BLK * HALF

FFN_TT = 128
FFN_SUB = 512
W1_STEPS = 16
W2_STEPS = 8
FFN_TF = 256
SCAN_T = 64
PITCH = SCAN_T + 4
VMEM_LIMIT = 60 * 1024 * 1024


def _rms(x, g):
    r = lax.rsqrt(jnp.mean(x * x, axis=-1, keepdims=True) + EPS)
    return x * r * g


def _bdot(a, b):
    return jnp.dot(a.astype(jnp.bfloat16), b.astype(jnp.bfloat16), preferred_element_type=jnp.float32)


def _swiglu_chunk(xn, wg_ref, wu_ref, wd_ref, j):
    g = jnp.dot(xn, wg_ref[:, j:j + FFN_TF], preferred_element_type=jnp.float32)
    u = jnp.dot(xn, wu_ref[:, j:j + FFN_TF], preferred_element_type=jnp.float32)
    a = (g * jax.nn.sigmoid(g)) * u
    return jnp.dot(a.astype(jnp.bfloat16), wd_ref[j:j + FFN_TF, :], preferred_element_type=jnp.float32)


def _swiglu(xn, wg_ref, wu_ref, wd_ref):
    acc = jnp.zeros((xn.shape[0], D_MODEL), jnp.float32)
    for j in range(0, D_FF, FFN_TF):
        acc = acc + _swiglu_chunk(xn, wg_ref, wu_ref, wd_ref, j)
    return acc


def _cast_weight_chunks(step, pairs):
    for src, dst in pairs:
        n = src.shape[0]
        dst[pl.ds(pl.multiple_of(step * n, n), n), :] = src[...].astype(dst.dtype)


def _whole(shape):
    return pl.BlockSpec(shape, lambda i: (0,) * len(shape), pipeline_mode=pl.Buffered(1))


def _glu_tail(y, u, d_ref, wglu_ref, bglu_ref):
    y = y + d_ref[...] * u
    z = jax.nn.gelu(y)
    gate = jax.nn.sigmoid(_bdot(z, wglu_ref[...]) + bglu_ref[...])
    return z * gate


def _discretize_kernel(lr_ref, li_ref, ldt_ref, bre_ref, bim_ref, are_ref, aim_ref, bb_ref):
    lr, li = lr_ref[...], li_ref[...]
    dt = jnp.exp(ldt_ref[...])
    mag = jnp.exp(lr * dt)
    ab_re = mag * jnp.cos(li * dt)
    ab_im = mag * jnp.sin(li * dt)
    den = lr * lr + li * li
    nr = ab_re - 1.0
    ni = ab_im
    coef_re = (nr * lr + ni * li) / den
    coef_im = (ni * lr - nr * li) / den
    are_ref[...] = ab_re
    aim_ref[...] = ab_im
    for k in range(N_BLK):
        cr = coef_re[k:k + 1, :]
        ci = coef_im[k:k + 1, :]
        bre, bim = bre_ref[k], bim_ref[k]
        bb_ref[k, :, 0:BLK_STATE] = (cr * bre - ci * bim).astype(bb_ref.dtype)
        bb_ref[k, :, BLK_STATE:] = (cr * bim + ci * bre).astype(bb_ref.dtype)


def _discretize(lr, li, ldt, bre, bim):
    return pl.pallas_call(
        _discretize_kernel,
        out_shape=[
            jax.ShapeDtypeStruct((N_BLK, BLK_STATE), jnp.float32),
            jax.ShapeDtypeStruct((N_BLK, BLK_STATE), jnp.float32),
            jax.ShapeDtypeStruct((N_BLK, LANES, 2 * BLK_STATE), jnp.bfloat16),
        ],
        name="ssm_discretize",
    )(lr, li, ldt, bre, bim)


def _ffn1_mixer_kernel(x_ref, xd_ref, n1_ref, wg_c, wu_c, wd_c, nm_ref, win_c,
                       are_ref, aim_ref, bb_ref, cre_ref, cim_ref, d_ref, wglu_ref, bglu_ref, cw_ref,
                       h_ref, mix_ref, sre_ref, sim_ref, nconv_ref, ud_ref, vd_ref, gd_ref,
                       xs0_ref, xs1_ref, xs2_ref, xs3_ref, u_ref, y_ref, v_ref, g_ref, state_ref,
                       wg_ref, wu_ref, wd_ref, win_ref):
    xs_refs = (xs0_ref, xs1_ref, xs2_ref, xs3_ref)
    step = pl.program_id(0)
    last = pl.num_programs(0) - 1
    nb, t_len, _ = x_ref.shape
    n_dec = xd_ref.shape[0]
    tiles = [(k, j) for k in range(N_BLK) for j in range(HALF)]

    @pl.when(step < W1_STEPS)
    def _load_weights():
        _cast_weight_chunks(step, ((wg_c, wg_ref), (wu_c, wu_ref), (wd_c, wd_ref), (win_c, win_ref)))

    @pl.when(step == 0)
    def _init():
        state_ref[...] = jnp.zeros_like(state_ref)
        u_ref[...] = jnp.zeros_like(u_ref)
        v_ref[...] = jnp.zeros_like(v_ref)
        g_ref[...] = jnp.zeros_like(g_ref)

    @pl.when(step >= W1_STEPS)
    def _main():
        carry = {}

        def bbar_u(ks):
            for k in ks:
                u_bf = u_ref[:, k * LANES:(k + 1) * LANES].astype(jnp.bfloat16)
                bu = jnp.dot(u_bf, bb_ref[k], preferred_element_type=jnp.float32)
                for j in range(SLABS_PER_BLK):
                    xs_refs[k][j] = bu[:, j * LANES:(j + 1) * LANES]

        def scan(t0, t1):
            if t0 == 0:
                carry["re"] = [state_ref[k * SLABS_PER_BLK + j] for k, j in tiles]
                carry["im"] = [state_ref[k * SLABS_PER_BLK + HALF + j] for k, j in tiles]
            xr, xi = carry["re"], carry["im"]
            for t in range(t0, t1):
                rows = pl.ds(t, SUBLANES, stride=PITCH)
                for n, (k, j) in enumerate(tiles):
                    a_re = are_ref[k * HALF + j]
                    a_im = aim_ref[k * HALF + j]
                    nr = (a_re * xr[n] + xs_refs[k][j, rows, :]) - a_im * xi[n]
                    ni = (a_re * xi[n] + xs_refs[k][HALF + j, rows, :]) + a_im * xr[n]
                    xs_refs[k][j, rows, :] = nr
                    xs_refs[k][HALF + j, rows, :] = ni
                    xr[n], xi[n] = nr, ni
            if t1 == t_len:
                for n, (k, j) in enumerate(tiles):
                    state_ref[k * SLABS_PER_BLK + j] = xr[n]
                    state_ref[k * SLABS_PER_BLK + HALF + j] = xi[n]

        def c_proj(ks):
            for k in ks:
                x_re = jnp.concatenate([xs_refs[k][j] for j in range(HALF)], axis=1)
                x_im = jnp.concatenate([xs_refs[k][HALF + j] for j in range(HALF)], axis=1)
                y_ref[:, k * LANES:(k + 1) * LANES] = _bdot(x_re, cre_ref[k]) - _bdot(x_im, cim_ref[k])

        def glu():
            y_ref[...] = _glu_tail(y_ref[...], u_ref[...], d_ref, wglu_ref, bglu_ref)

        def emit_ssm():
            for b in range(nb):
                mix_ref[b, :, 0:SSM_WIDTH] = y_ref[pl.ds(b * PITCH, t_len), :].astype(mix_ref.dtype)

        def conv():
            v = v_ref[:, SUBLANES:SUBLANES + t_len, :]
            z = cw_ref[0:1, :] * v_ref[:, SUBLANES - 2:SUBLANES - 2 + t_len, :]
            z = z + cw_ref[1:2, :] * v_ref[:, SUBLANES - 1:SUBLANES - 1 + t_len, :]
            z = z + cw_ref[2:3, :] * v
            mix_ref[:, :, SSM_WIDTH:] = (g_ref[...] * z).astype(mix_ref.dtype)
            v_ref[:, 0:SUBLANES, :] = v_ref[:, t_len:t_len + SUBLANES, :]

        q = t_len // 4
        pieces = [
            lambda: (conv(), bbar_u((0,))), lambda: bbar_u((1,)), lambda: bbar_u((2,)), lambda: bbar_u((3,)),
            lambda: scan(0, q), lambda: scan(q, 2 * q), lambda: scan(2 * q, 3 * q), lambda: scan(3 * q, t_len),
            lambda: c_proj((0, 1)), lambda: c_proj((2, 3)),
            lambda: (glu(), emit_ssm()),
        ]
        assert len(pieces) <= D_FF // FFN_TF

        x_dec = jnp.concatenate(
            [xd_ref[...], jnp.zeros((nb * t_len - n_dec, D_MODEL), jnp.float32)], axis=0)
        x = jnp.where(step == last, x_dec, x_ref[...].reshape(nb * t_len, D_MODEL))
        xn = _rms(x, n1_ref[...]).astype(jnp.bfloat16)
        acc = jnp.zeros((nb * t_len, D_MODEL), jnp.float32)
        for c, j in enumerate(range(0, D_FF, FFN_TF)):
            acc = acc + _swiglu_chunk(xn, wg_ref, wu_ref, wd_ref, j)
            if c < len(pieces):
                pieces[c]()
        h = x + 0.5 * acc
        hn = _rms(h, nm_ref[...]).astype(jnp.bfloat16)
        p = jnp.dot(hn, win_ref[...], preferred_element_type=jnp.float32)
        h_ref[...] = h.reshape(nb, t_len, D_MODEL)
        v_new = p[:, SSM_WIDTH + 2 * CONV_WIDTH:] * p[:, SSM_WIDTH:SSM_WIDTH + CONV_WIDTH]
        v_ref[:, SUBLANES:SUBLANES + t_len, :] = v_new.reshape(nb, t_len, CONV_WIDTH)
        g_ref[...] = p[:, SSM_WIDTH + CONV_WIDTH:SSM_WIDTH + 2 * CONV_WIDTH].reshape(nb, t_len, CONV_WIDTH)
        for b in range(nb):
            u_ref[pl.ds(b * PITCH, t_len), :] = p[b * t_len:(b + 1) * t_len, 0:SSM_WIDTH]

    @pl.when(step == last)
    def _final():
        for n, (k, j) in enumerate(tiles):
            lanes = slice(k * BLK_STATE + j * LANES, k * BLK_STATE + (j + 1) * LANES)
            sre_ref[:, lanes] = state_ref[k * SLABS_PER_BLK + j]
            sim_ref[:, lanes] = state_ref[k * SLABS_PER_BLK + HALF + j]
        nconv_ref[...] = v_ref[:, SUBLANES - 2:SUBLANES, :]
        for b in range(n_dec // t_len):
            ud_ref[b * t_len:(b + 1) * t_len, :] = u_ref[pl.ds(b * PITCH, t_len), :]
            vd_ref[b * t_len:(b + 1) * t_len, :] = v_ref[b, SUBLANES:SUBLANES + t_len, :]
            gd_ref[b * t_len:(b + 1) * t_len, :] = g_ref[b]


def _ffn1_mixer(x, xd, n1, wg, wu, wd, nm, win, a_re, a_im, bb, cre, cim, d, wglu, bglu, cw):
    nb, seq, _ = x.shape
    n_dec = xd.shape[0]
    n_chunk = seq // SCAN_T
    rows = nb * PITCH
    main = lambda i: jnp.maximum(i - W1_STEPS, 0)
    chunk = lambda i: (jnp.minimum(i, W1_STEPS - 1), 0)
    return pl.pallas_call(
        _ffn1_mixer_kernel,
        grid=(W1_STEPS + n_chunk + 1,),
        in_specs=[
            pl.BlockSpec((nb, SCAN_T, D_MODEL), lambda i: (0, jnp.minimum(main(i), n_chunk - 1), 0)),
            _whole((n_dec, D_MODEL)),
            _whole((1, D_MODEL)),
            pl.BlockSpec((D_MODEL // W1_STEPS, D_FF), chunk),
            pl.BlockSpec((D_MODEL // W1_STEPS, D_FF), chunk),
            pl.BlockSpec((D_FF // W1_STEPS, D_MODEL), chunk),
            _whole((1, D_MODEL)),
            pl.BlockSpec((D_MODEL // W1_STEPS, IN_PROJ_WIDTH), chunk),
            _whole((N_TILE, SUBLANES, LANES)),
            _whole((N_TILE, SUBLANES, LANES)),
            _whole((N_BLK, LANES, 2 * BLK_STATE)),
            _whole((N_BLK, BLK_STATE, LANES)),
            _whole((N_BLK, BLK_STATE, LANES)),
            _whole((1, SSM_WIDTH)),
            _whole((SSM_WIDTH, SSM_WIDTH)),
            _whole((1, SSM_WIDTH)),
            _whole((CONV_K, CONV_WIDTH)),
        ],
        out_specs=[
            pl.BlockSpec((nb, SCAN_T, D_MODEL), lambda i: (0, main(i), 0)),
            pl.BlockSpec((nb, SCAN_T, D_MODEL), lambda i: (0, jnp.clip(main(i) - 1, 0, n_chunk - 1), 0)),
            pl.BlockSpec((nb, N_STATE), lambda i: (0, 0)),
            pl.BlockSpec((nb, N_STATE), lambda i: (0, 0)),
            pl.BlockSpec((nb, CONV_K - 1, CONV_WIDTH), lambda i: (0, 0, 0)),
            pl.BlockSpec((n_dec, SSM_WIDTH), lambda i: (0, 0)),
            pl.BlockSpec((n_dec, CONV_WIDTH), lambda i: (0, 0)),
            pl.BlockSpec((n_dec, CONV_WIDTH), lambda i: (0, 0)),
        ],
        out_shape=[
            jax.ShapeDtypeStruct((nb, seq + SCAN_T, D_MODEL), jnp.float32),
            jax.ShapeDtypeStruct((nb, seq, D_MODEL), jnp.bfloat16),
            jax.ShapeDtypeStruct((nb, N_STATE), jnp.float32),
            jax.ShapeDtypeStruct((nb, N_STATE), jnp.float32),
            jax.ShapeDtypeStruct((nb, CONV_K - 1, CONV_WIDTH), jnp.float32),
            jax.ShapeDtypeStruct((n_dec, SSM_WIDTH), jnp.float32),
            jax.ShapeDtypeStruct((n_dec, CONV_WIDTH), jnp.float32),
            jax.ShapeDtypeStruct((n_dec, CONV_WIDTH), jnp.float32),
        ],
        scratch_shapes=[
            *[pltpu.VMEM((SLABS_PER_BLK, rows, LANES), jnp.float32) for _ in range(N_BLK)],
            pltpu.VMEM((rows, SSM_WIDTH), jnp.float32),
            pltpu.VMEM((rows, SSM_WIDTH), jnp.float32),
            pltpu.VMEM((nb, SCAN_T + SUBLANES, CONV_WIDTH), jnp.float32),
            pltpu.VMEM((nb, SCAN_T, CONV_WIDTH), jnp.float32),
            pltpu.VMEM((2 * N_TILE, SUBLANES, LANES), jnp.float32),
            pltpu.VMEM((D_MODEL, D_FF), jnp.bfloat16),
            pltpu.VMEM((D_MODEL, D_FF), jnp.bfloat16),
            pltpu.VMEM((D_FF, D_MODEL), jnp.bfloat16),
            pltpu.VMEM((D_MODEL, IN_PROJ_WIDTH), jnp.bfloat16),
        ],
        compiler_params=pltpu.CompilerParams(
            dimension_semantics=("arbitrary",), vmem_limit_bytes=VMEM_LIMIT),
        name="ffn1_mixer",
    )(x, xd, n1, wg, wu, wd, nm, win, a_re, a_im, bb, cre, cim, d, wglu, bglu, cw)


def _ffn2_rows(h, mix, weights):
    wout_ref, n2_ref, wg_ref, wu_ref, wd_ref, nf_ref = weights
    h = h + jnp.dot(mix, wout_ref[...], preferred_element_type=jnp.float32)
    hn = _rms(h, n2_ref[...]).astype(jnp.bfloat16)
    h = h + 0.5 * _swiglu(hn, wg_ref, wu_ref, wd_ref)
    return _rms(h, nf_ref[...])


def _ffn2_kernel(h_ref, mix_ref, hd_ref, mixd_ref, wout_c, wg_c, wu_c, wd_c, n2_ref, nf_ref,
                 y_ref, yd_ref, wout_ref, wg_ref, wu_ref, wd_ref):
    step = pl.program_id(0)
    last = pl.num_programs(0) - 1
    nb, tt, _ = h_ref.shape
    weights = (wout_ref, n2_ref, wg_ref, wu_ref, wd_ref, nf_ref)

    @pl.when(step < W2_STEPS)
    def _load_weights():
        _cast_weight_chunks(step, ((wout_c, wout_ref), (wg_c, wg_ref), (wu_c, wu_ref), (wd_c, wd_ref)))

    @pl.when((step >= W2_STEPS) & (step < last))
    def _prompt_tile():
        sb = FFN_SUB // tt
        for b in range(0, nb, sb):
            y = _ffn2_rows(h_ref[b:b + sb].reshape(FFN_SUB, D_MODEL),
                           mix_ref[b:b + sb].reshape(FFN_SUB, D_MODEL), weights)
            y_ref[b:b + sb] = y.reshape(sb, tt, D_MODEL)

    @pl.when(step == last)
    def _decode_rows():
        yd_ref[...] = _ffn2_rows(hd_ref[...], mixd_ref[...], weights)


def _ffn2(h_all, mix, hd, mixd, wout, n2, wg, wu, wd, nf):
    nb, seq, _ = mix.shape
    n_tiles = seq // FFN_TT
    tile = lambda i: (0, jnp.clip(i - W2_STEPS, 0, n_tiles - 1), 0)
    chunk = lambda i: (jnp.minimum(i, W2_STEPS - 1), 0)
    return pl.pallas_call(
        _ffn2_kernel,
        grid=(W2_STEPS + n_tiles + 1,),
        in_specs=[
            pl.BlockSpec((nb, FFN_TT, D_MODEL), tile),
            pl.BlockSpec((nb, FFN_TT, D_MODEL), tile),
            _whole(hd.shape),
            _whole(mixd.shape),
            pl.BlockSpec((D_MODEL // W2_STEPS, D_MODEL), chunk),
            pl.BlockSpec((D_MODEL // W2_STEPS, D_FF), chunk),
            pl.BlockSpec((D_MODEL // W2_STEPS, D_FF), chunk),
            pl.BlockSpec((D_FF // W2_STEPS, D_MODEL), chunk),
            _whole((1, D_MODEL)),
            _whole((1, D_MODEL)),
        ],
        out_specs=[
            pl.BlockSpec((nb, FFN_TT, D_MODEL), tile),
            pl.BlockSpec(hd.shape, lambda i: (0, 0)),
        ],
        out_shape=[
            jax.ShapeDtypeStruct((nb, seq, D_MODEL), jnp.float32),
            jax.ShapeDtypeStruct(hd.shape, jnp.float32),
        ],
        scratch_shapes=[
            pltpu.VMEM((D_MODEL, D_MODEL), jnp.bfloat16),
            pltpu.VMEM((D_MODEL, D_FF), jnp.bfloat16),
            pltpu.VMEM((D_MODEL, D_FF), jnp.bfloat16),
            pltpu.VMEM((D_FF, D_MODEL), jnp.bfloat16),
        ],
        compiler_params=pltpu.CompilerParams(
            dimension_semantics=("arbitrary",), vmem_limit_bytes=VMEM_LIMIT),
        name="outproj_ffn2",
    )(h_all, mix, hd, mixd, wout, wg, wu, wd, n2, nf)


def _mixer_step_kernel(u_ref, v_ref, g_ref, hre_ref, him_ref, buf_ref, are_ref, aim_ref, bb_ref, cre_ref,
                       cim_ref, d_ref, wglu_ref, bglu_ref, cw_ref, mix_ref, sre_ref, sim_ref, nconv_ref):
    u = u_ref[...]
    u_bf = u.astype(jnp.bfloat16)
    ys = []
    for k in range(N_BLK):
        bu = jnp.dot(u_bf[:, k * LANES:(k + 1) * LANES], bb_ref[k], preferred_element_type=jnp.float32)
        st = slice(k * BLK_STATE, (k + 1) * BLK_STATE)
        a_re = are_ref[k:k + 1, :]
        a_im = aim_ref[k:k + 1, :]
        h_re = hre_ref[:, st]
        h_im = him_ref[:, st]
        x_re = (a_re * h_re + bu[:, :BLK_STATE]) - a_im * h_im
        x_im = (a_re * h_im + bu[:, BLK_STATE:]) + a_im * h_re
        sre_ref[:, st] = x_re
        sim_ref[:, st] = x_im
        ys.append(_bdot(x_re, cre_ref[k]) - _bdot(x_im, cim_ref[k]))
    y = jnp.concatenate(ys, axis=1)
    mix_ref[:, 0:SSM_WIDTH] = _glu_tail(y, u, d_ref, wglu_ref, bglu_ref).astype(mix_ref.dtype)

    v = v_ref[...]
    z = cw_ref[0:1, :] * buf_ref[0] + cw_ref[1:2, :] * buf_ref[1] + cw_ref[2:3, :] * v
    mix_ref[:, SSM_WIDTH:] = (g_ref[...] * z).astype(mix_ref.dtype)
    nconv_ref[0] = buf_ref[1]
    nconv_ref[1] = v


def _mixer_step(u, v, g, h_re, h_im, buf, a_re, a_im, bb, cre, cim, d, wglu, bglu, cw):
    nb = u.shape[0]
    return pl.pallas_call(
        _mixer_step_kernel,
        out_shape=[
            jax.ShapeDtypeStruct((nb, D_MODEL), jnp.bfloat16),
            jax.ShapeDtypeStruct((nb, N_STATE), jnp.float32),
            jax.ShapeDtypeStruct((nb, N_STATE), jnp.float32),
            jax.ShapeDtypeStruct((CONV_K - 1, nb, CONV_WIDTH), jnp.float32),
        ],
        compiler_params=pltpu.CompilerParams(vmem_limit_bytes=VMEM_LIMIT),
        name="mixer_step",
    )(u, v, g, h_re, h_im, buf, a_re, a_im, bb, cre, cim, d, wglu, bglu, cw)


def _ssm_layout(lam_re, lam_im, log_dt, b_re, b_im, c_re, c_im):
    gpb = N_SSM_GROUPS // N_BLK
    eye = jnp.eye(gpb, dtype=jnp.float32)
    lr = lam_re.reshape(N_BLK, BLK_STATE)
    li = lam_im.reshape(N_BLK, BLK_STATE)
    ldt = jnp.broadcast_to(log_dt[:, None], (N_SSM_GROUPS, SSM_STATE)).reshape(N_BLK, BLK_STATE)

    def b_blocks(b):
        b = b.reshape(N_BLK, gpb, SSM_STATE, SSM_GROUP).transpose(0, 1, 3, 2)
        return jnp.einsum("kgcp,gh->kgchp", b, eye).reshape(N_BLK, LANES, BLK_STATE)

    def c_blocks(c):
        c = c.reshape(N_BLK, gpb, SSM_GROUP, SSM_STATE).transpose(0, 1, 3, 2)
        return jnp.einsum("kgpc,gh->kgphc", c, eye).reshape(N_BLK, BLK_STATE, LANES).astype(jnp.bfloat16)

    return lr, li, ldt, b_blocks(b_re), b_blocks(b_im), c_blocks(c_re), c_blocks(c_im)


def kernel(x_prompt, x_sample, state_ssm_re, state_ssm_im, state_conv, ffn1_norm, ffn1_w_gate, ffn1_w_up, ffn1_w_down, mix_norm, w_in, ssm_lambda_re, ssm_lambda_im, ssm_log_dt, ssm_b_re, ssm_b_im, ssm_c_re, ssm_c_im, ssm_d, ssm_w_glu, ssm_b_glu, conv_w, w_out, ffn2_norm, ffn2_w_gate, ffn2_w_up, ffn2_w_down, final_norm):
    assert ffn1_norm.shape[0] == 1, "single-layer trunk"
    n_prompt, seq, _ = x_prompt.shape
    n_dec, dec_seq, _ = x_sample.shape
    assert dec_seq == 1 and n_prompt == SUBLANES and seq % FFN_TT == 0 and FFN_TT % SCAN_T == 0
    assert n_dec % SCAN_T == 0 and n_dec <= n_prompt * SCAN_T and (n_prompt * FFN_TT) % FFN_SUB == 0

    n1 = ffn1_norm[0][None, :]
    nm = mix_norm[0][None, :]
    n2 = ffn2_norm[0][None, :]
    nf = final_norm[None, :]
    wglu = ssm_w_glu[0].astype(jnp.bfloat16)
    lr, li, ldt, bre, bim, cre, cim = _ssm_layout(
        ssm_lambda_re[0], ssm_lambda_im[0], ssm_log_dt[0], ssm_b_re[0], ssm_b_im[0], ssm_c_re[0], ssm_c_im[0])
    a_re, a_im, bb = _discretize(lr, li, ldt, bre, bim)
    a_re_t = jnp.broadcast_to(a_re.reshape(N_TILE, 1, LANES), (N_TILE, SUBLANES, LANES))
    a_im_t = jnp.broadcast_to(a_im.reshape(N_TILE, 1, LANES), (N_TILE, SUBLANES, LANES))
    d = ssm_d[0][None, :]
    bglu = ssm_b_glu[0][None, :]
    cw = conv_w[0]

    h_all, mixp, p_re, p_im, p_conv, ud, vd, gd = _ffn1_mixer(
        x_prompt, x_sample.reshape(n_dec, D_MODEL), n1, ffn1_w_gate[0], ffn1_w_up[0], ffn1_w_down[0], nm,
        w_in[0], a_re_t, a_im_t, bb, cre, cim, d, wglu, bglu, cw)

    hd = h_all[:n_dec // SCAN_T, seq:, :].reshape(n_dec, D_MODEL)
    s_re0 = state_ssm_re[0].reshape(n_dec, N_STATE)
    s_im0 = state_ssm_im[0].reshape(n_dec, N_STATE)
    buf = state_conv[0].transpose(1, 0, 2)
    mixd, s_re, s_im, s_conv = _mixer_step(ud, vd, gd, s_re0, s_im0, buf, a_re, a_im, bb, cre, cim,
                                           d, wglu, bglu, cw)

    yp, yd = _ffn2(h_all, mixp, hd, mixd, w_out[0], n2, ffn2_w_gate[0], ffn2_w_up[0], ffn2_w_down[0], nf)

    gs = (N_SSM_GROUPS, SSM_STATE)
    return (yp,
            yd.reshape(n_dec, 1, D_MODEL),
            p_re.reshape(1, n_prompt, *gs),
            p_im.reshape(1, n_prompt, *gs),
            p_conv[None],
            s_re.reshape(1, n_dec, *gs),
            s_im.reshape(1, n_dec, *gs),
            s_conv.transpose(1, 0, 2)[None])
```

```python
import jax
import jax.numpy as jnp
from jax import lax
from jax.experimental import pallas as pl
from jax.experimental.pallas import tpu as pltpu

D_MODEL = 1024
D_FF = 2816
SSM_WIDTH = 512
CONV_WIDTH = 512
SSM_GROUP = 16
N_SSM_GROUPS = 32
SSM_STATE = 64
CONV_K = 3
IN_PROJ_WIDTH = SSM_WIDTH + 3 * CONV_WIDTH
EPS = 1e-6

LANES = 128
SUBLANES = 8
N_STATE = N_SSM_GROUPS * SSM_STATE
N_BLK = SSM_WIDTH // LANES
BLK_STATE = N_STATE // N_BLK
SLABS_PER_BLK = 2 * BLK_STATE // LANES
HALF = SLABS_PER_BLK // 2
N_TILE = N_BLK * HALF

FFN_TM = 1024
FFN_SUB = 512
W1_STEPS = 16
W2_STEPS = 8
FFN_TF = 256
SCAN_T = 64
PITCH = SCAN_T + 4
VMEM_LIMIT = 60 * 1024 * 1024


def _rms(x, g):
    r = lax.rsqrt(jnp.mean(x * x, axis=-1, keepdims=True) + EPS)
    return x * r * g


def _bdot(a, b):
    return jnp.dot(a.astype(jnp.bfloat16), b.astype(jnp.bfloat16), preferred_element_type=jnp.float32)


def _swiglu_chunk(xn, wg_ref, wu_ref, wd_ref, j):
    g = jnp.dot(xn, wg_ref[:, j:j + FFN_TF], preferred_element_type=jnp.float32)
    u = jnp.dot(xn, wu_ref[:, j:j + FFN_TF], preferred_element_type=jnp.float32)
    a = (g * jax.nn.sigmoid(g)) * u
    return jnp.dot(a.astype(jnp.bfloat16), wd_ref[j:j + FFN_TF, :], preferred_element_type=jnp.float32)


def _swiglu(xn, wg_ref, wu_ref, wd_ref):
    acc = jnp.zeros((xn.shape[0], D_MODEL), jnp.float32)
    for j in range(0, D_FF, FFN_TF):
        acc = acc + _swiglu_chunk(xn, wg_ref, wu_ref, wd_ref, j)
    return acc


def _cast_weight_chunks(step, pairs):
    for src, dst in pairs:
        n = src.shape[0]
        dst[pl.ds(pl.multiple_of(step * n, n), n), :] = src[...].astype(dst.dtype)


def _whole(shape):
    return pl.BlockSpec(shape, lambda i: (0,) * len(shape), pipeline_mode=pl.Buffered(1))


def _glu_tail(y, u, d_ref, wglu_ref, bglu_ref):
    y = y + d_ref[...] * u
    z = jax.nn.gelu(y)
    gate = jax.nn.sigmoid(_bdot(z, wglu_ref[...]) + bglu_ref[...])
    return z * gate


def _discretize_kernel(lr_ref, li_ref, ldt_ref, bre_ref, bim_ref, are_ref, aim_ref, bb_ref):
    lr, li = lr_ref[...], li_ref[...]
    dt = jnp.exp(ldt_ref[...])
    mag = jnp.exp(lr * dt)
    ab_re = mag * jnp.cos(li * dt)
    ab_im = mag * jnp.sin(li * dt)
    den = lr * lr + li * li
    nr = ab_re - 1.0
    ni = ab_im
    coef_re = (nr * lr + ni * li) / den
    coef_im = (ni * lr - nr * li) / den
    are_ref[...] = ab_re
    aim_ref[...] = ab_im
    for k in range(N_BLK):
        cr = coef_re[k:k + 1, :]
        ci = coef_im[k:k + 1, :]
        bre, bim = bre_ref[k], bim_ref[k]
        bb_ref[k, :, 0:BLK_STATE] = (cr * bre - ci * bim).astype(bb_ref.dtype)
        bb_ref[k, :, BLK_STATE:] = (cr * bim + ci * bre).astype(bb_ref.dtype)


def _discretize(lr, li, ldt, bre, bim):
    return pl.pallas_call(
        _discretize_kernel,
        out_shape=[
            jax.ShapeDtypeStruct((N_BLK, BLK_STATE), jnp.float32),
            jax.ShapeDtypeStruct((N_BLK, BLK_STATE), jnp.float32),
            jax.ShapeDtypeStruct((N_BLK, LANES, 2 * BLK_STATE), jnp.bfloat16),
        ],
        name="ssm_discretize",
    )(lr, li, ldt, bre, bim)


def _ffn1_mixer_kernel(x_ref, xd_ref, n1_ref, wg_c, wu_c, wd_c, nm_ref, win_c,
                       are_ref, aim_ref, bb_ref, cre_ref, cim_ref, d_ref, wglu_ref, bglu_ref, cw_ref,
                       h_ref, mix_ref, sre_ref, sim_ref, nconv_ref, ud_ref, vd_ref, gd_ref,
                       xs0_ref, xs1_ref, xs2_ref, xs3_ref, u_ref, y_ref, v_ref, g_ref, state_ref,
                       wg_ref, wu_ref, wd_ref, win_ref):
    xs_refs = (xs0_ref, xs1_ref, xs2_ref, xs3_ref)
    step = pl.program_id(0)
    last = pl.num_programs(0) - 1
    nb, t_len, _ = x_ref.shape
    n_dec = xd_ref.shape[0]
    tiles = [(k, j) for k in range(N_BLK) for j in range(HALF)]

    @pl.when(step < W1_STEPS)
    def _load_weights():
        _cast_weight_chunks(step, ((wg_c, wg_ref), (wu_c, wu_ref), (wd_c, wd_ref), (win_c, win_ref)))

    @pl.when(step == 0)
    def _init():
        state_ref[...] = jnp.zeros_like(state_ref)
        u_ref[...] = jnp.zeros_like(u_ref)
        v_ref[...] = jnp.zeros_like(v_ref)
        g_ref[...] = jnp.zeros_like(g_ref)

    @pl.when(step >= W1_STEPS)
    def _main():
        carry = {}

        def bbar_u(ks):
            for k in ks:
                u_bf = u_ref[:, k * LANES:(k + 1) * LANES].astype(jnp.bfloat16)
                bu = jnp.dot(u_bf, bb_ref[k], preferred_element_type=jnp.float32)
                for j in range(SLABS_PER_BLK):
                    xs_refs[k][j] = bu[:, j * LANES:(j + 1) * LANES]

        def scan(t0, t1):
            if t0 == 0:
                carry["re"] = [state_ref[k * SLABS_PER_BLK + j] for k, j in tiles]
                carry["im"] = [state_ref[k * SLABS_PER_BLK + HALF + j] for k, j in tiles]
            xr, xi = carry["re"], carry["im"]
            for t in range(t0, t1):
                rows = pl.ds(t, SUBLANES, stride=PITCH)
                for n, (k, j) in enumerate(tiles):
                    a_re = are_ref[k * HALF + j]
                    a_im = aim_ref[k * HALF + j]
                    nr = (a_re * xr[n] + xs_refs[k][j, rows, :]) - a_im * xi[n]
                    ni = (a_re * xi[n] + xs_refs[k][HALF + j, rows, :]) + a_im * xr[n]
                    xs_refs[k][j, rows, :] = nr
                    xs_refs[k][HALF + j, rows, :] = ni
                    xr[n], xi[n] = nr, ni
            if t1 == t_len:
                for n, (k, j) in enumerate(tiles):
                    state_ref[k * SLABS_PER_BLK + j] = xr[n]
                    state_ref[k * SLABS_PER_BLK + HALF + j] = xi[n]

        def c_proj(ks):
            for k in ks:
                x_re = jnp.concatenate([xs_refs[k][j] for j in range(HALF)], axis=1)
                x_im = jnp.concatenate([xs_refs[k][HALF + j] for j in range(HALF)], axis=1)
                y_ref[:, k * LANES:(k + 1) * LANES] = _bdot(x_re, cre_ref[k]) - _bdot(x_im, cim_ref[k])

        def glu():
            y_ref[...] = _glu_tail(y_ref[...], u_ref[...], d_ref, wglu_ref, bglu_ref)

        def emit_ssm():
            for b in range(nb):
                mix_ref[b, :, 0:SSM_WIDTH] = y_ref[pl.ds(b * PITCH, t_len), :].astype(mix_ref.dtype)

        def conv():
            v = v_ref[:, SUBLANES:SUBLANES + t_len, :]
            z = cw_ref[0:1, :] * v_ref[:, SUBLANES - 2:SUBLANES - 2 + t_len, :]
            z = z + cw_ref[1:2, :] * v_ref[:, SUBLANES - 1:SUBLANES - 1 + t_len, :]
            z = z + cw_ref[2:3, :] * v
            mix_ref[:, :, SSM_WIDTH:] = (g_ref[...] * z).astype(mix_ref.dtype)
            v_ref[:, 0:SUBLANES, :] = v_ref[:, t_len:t_len + SUBLANES, :]

        q = t_len // 4
        pieces = [
            lambda: (conv(), bbar_u((0,))), lambda: bbar_u((1,)), lambda: bbar_u((2,)), lambda: bbar_u((3,)),
            lambda: scan(0, q), lambda: scan(q, 2 * q), lambda: scan(2 * q, 3 * q), lambda: scan(3 * q, t_len),
            lambda: c_proj((0, 1)), lambda: c_proj((2, 3)),
            lambda: (glu(), emit_ssm()),
        ]
        assert len(pieces) <= D_FF // FFN_TF

        x_dec = jnp.concatenate(
            [xd_ref[...], jnp.zeros((nb * t_len - n_dec, D_MODEL), jnp.float32)], axis=0)
        x = jnp.where(step == last, x_dec, x_ref[...].reshape(nb * t_len, D_MODEL))
        xn = _rms(x, n1_ref[...]).astype(jnp.bfloat16)
        acc = jnp.zeros((nb * t_len, D_MODEL), jnp.float32)
        for c, j in enumerate(range(0, D_FF, FFN_TF)):
            acc = acc + _swiglu_chunk(xn, wg_ref, wu_ref, wd_ref, j)
            if c < len(pieces):
                pieces[c]()
        h = x + 0.5 * acc
        hn = _rms(h, nm_ref[...]).astype(jnp.bfloat16)
        p = jnp.dot(hn, win_ref[...], preferred_element_type=jnp.float32)
        h_ref[...] = h.reshape(nb, t_len, D_MODEL)
        v_new = p[:, SSM_WIDTH + 2 * CONV_WIDTH:] * p[:, SSM_WIDTH:SSM_WIDTH + CONV_WIDTH]
        v_ref[:, SUBLANES:SUBLANES + t_len, :] = v_new.reshape(nb, t_len, CONV_WIDTH)
        g_ref[...] = p[:, SSM_WIDTH + CONV_WIDTH:SSM_WIDTH + 2 * CONV_WIDTH].reshape(nb, t_len, CONV_WIDTH)
        for b in range(nb):
            u_ref[pl.ds(b * PITCH, t_len), :] = p[b * t_len:(b + 1) * t_len, 0:SSM_WIDTH]

    @pl.when(step == last)
    def _final():
        for n, (k, j) in enumerate(tiles):
            lanes = slice(k * BLK_STATE + j * LANES, k * BLK_STATE + (j + 1) * LANES)
            sre_ref[:, lanes] = state_ref[k * SLABS_PER_BLK + j]
            sim_ref[:, lanes] = state_ref[k * SLABS_PER_BLK + HALF + j]
        nconv_ref[...] = v_ref[:, SUBLANES - 2:SUBLANES, :]
        for b in range(n_dec // t_len):
            ud_ref[b * t_len:(b + 1) * t_len, :] = u_ref[pl.ds(b * PITCH, t_len), :]
            vd_ref[b * t_len:(b + 1) * t_len, :] = v_ref[b, SUBLANES:SUBLANES + t_len, :]
            gd_ref[b * t_len:(b + 1) * t_len, :] = g_ref[b]


def _ffn1_mixer(x, xd, n1, wg, wu, wd, nm, win, a_re, a_im, bb, cre, cim, d, wglu, bglu, cw):
    nb, seq, _ = x.shape
    n_dec = xd.shape[0]
    n_chunk = seq // SCAN_T
    rows = nb * PITCH
    main = lambda i: jnp.maximum(i - W1_STEPS, 0)
    chunk = lambda i: (jnp.minimum(i, W1_STEPS - 1), 0)
    return pl.pallas_call(
        _ffn1_mixer_kernel,
        grid=(W1_STEPS + n_chunk + 1,),
        in_specs=[
            pl.BlockSpec((nb, SCAN_T, D_MODEL), lambda i: (0, jnp.minimum(main(i), n_chunk - 1), 0)),
            _whole((n_dec, D_MODEL)),
            _whole((1, D_MODEL)),
            pl.BlockSpec((D_MODEL // W1_STEPS, D_FF), chunk),
            pl.BlockSpec((D_MODEL // W1_STEPS, D_FF), chunk),
            pl.BlockSpec((D_FF // W1_STEPS, D_MODEL), chunk),
            _whole((1, D_MODEL)),
            pl.BlockSpec((D_MODEL // W1_STEPS, IN_PROJ_WIDTH), chunk),
            _whole((N_TILE, SUBLANES, LANES)),
            _whole((N_TILE, SUBLANES, LANES)),
            _whole((N_BLK, LANES, 2 * BLK_STATE)),
            _whole((N_BLK, BLK_STATE, LANES)),
            _whole((N_BLK, BLK_STATE, LANES)),
            _whole((1, SSM_WIDTH)),
            _whole((SSM_WIDTH, SSM_WIDTH)),
            _whole((1, SSM_WIDTH)),
            _whole((CONV_K, CONV_WIDTH)),
        ],
        out_specs=[
            pl.BlockSpec((nb, SCAN_T, D_MODEL), lambda i: (0, main(i), 0)),
            pl.BlockSpec((nb, SCAN_T, D_MODEL), lambda i: (0, jnp.clip(main(i) - 1, 0, n_chunk - 1), 0)),
            pl.BlockSpec((nb, N_STATE), lambda i: (0, 0)),
            pl.BlockSpec((nb, N_STATE), lambda i: (0, 0)),
            pl.BlockSpec((nb, CONV_K - 1, CONV_WIDTH), lambda i: (0, 0, 0)),
            pl.BlockSpec((n_dec, SSM_WIDTH), lambda i: (0, 0)),
            pl.BlockSpec((n_dec, CONV_WIDTH), lambda i: (0, 0)),
            pl.BlockSpec((n_dec, CONV_WIDTH), lambda i: (0, 0)),
        ],
        out_shape=[
            jax.ShapeDtypeStruct((nb, seq + FFN_TM, D_MODEL), jnp.float32),
            jax.ShapeDtypeStruct((nb, seq, D_MODEL), jnp.bfloat16),
            jax.ShapeDtypeStruct((nb, N_STATE), jnp.float32),
            jax.ShapeDtypeStruct((nb, N_STATE), jnp.float32),
            jax.ShapeDtypeStruct((nb, CONV_K - 1, CONV_WIDTH), jnp.float32),
            jax.ShapeDtypeStruct((n_dec, SSM_WIDTH), jnp.float32),
            jax.ShapeDtypeStruct((n_dec, CONV_WIDTH), jnp.float32),
            jax.ShapeDtypeStruct((n_dec, CONV_WIDTH), jnp.float32),
        ],
        scratch_shapes=[
            *[pltpu.VMEM((SLABS_PER_BLK, rows, LANES), jnp.float32) for _ in range(N_BLK)],
            pltpu.VMEM((rows, SSM_WIDTH), jnp.float32),
            pltpu.VMEM((rows, SSM_WIDTH), jnp.float32),
            pltpu.VMEM((nb, SCAN_T + SUBLANES, CONV_WIDTH), jnp.float32),
            pltpu.VMEM((nb, SCAN_T, CONV_WIDTH), jnp.float32),
            pltpu.VMEM((2 * N_TILE, SUBLANES, LANES), jnp.float32),
            pltpu.VMEM((D_MODEL, D_FF), jnp.bfloat16),
            pltpu.VMEM((D_MODEL, D_FF), jnp.bfloat16),
            pltpu.VMEM((D_FF, D_MODEL), jnp.bfloat16),
            pltpu.VMEM((D_MODEL, IN_PROJ_WIDTH), jnp.bfloat16),
        ],
        compiler_params=pltpu.CompilerParams(
            dimension_semantics=("arbitrary",), vmem_limit_bytes=VMEM_LIMIT),
        name="ffn1_mixer",
    )(x, xd, n1, wg, wu, wd, nm, win, a_re, a_im, bb, cre, cim, d, wglu, bglu, cw)


def _ffn2_rows(h, mix, weights):
    wout_ref, n2_ref, wg_ref, wu_ref, wd_ref, nf_ref = weights
    h = h + jnp.dot(mix, wout_ref[...], preferred_element_type=jnp.float32)
    hn = _rms(h, n2_ref[...]).astype(jnp.bfloat16)
    h = h + 0.5 * _swiglu(hn, wg_ref, wu_ref, wd_ref)
    return _rms(h, nf_ref[...])


def _ffn2_kernel(h_ref, mix_ref, hd_ref, mixd_ref, wout_c, wg_c, wu_c, wd_c, n2_ref, nf_ref,
                 y_ref, yd_ref, wout_ref, wg_ref, wu_ref, wd_ref):
    step = pl.program_id(0)
    last = pl.num_programs(0) - 1
    weights = (wout_ref, n2_ref, wg_ref, wu_ref, wd_ref, nf_ref)

    @pl.when(step < W2_STEPS)
    def _load_weights():
        _cast_weight_chunks(step, ((wout_c, wout_ref), (wg_c, wg_ref), (wu_c, wu_ref), (wd_c, wd_ref)))

    @pl.when((step >= W2_STEPS) & (step < last))
    def _prompt_tile():
        for r in range(0, h_ref.shape[0], FFN_SUB):
            rows = slice(r, r + FFN_SUB)
            y_ref[rows, :] = _ffn2_rows(h_ref[rows, :], mix_ref[rows, :], weights)

    @pl.when(step == last)
    def _decode_rows():
        yd_ref[...] = _ffn2_rows(hd_ref[...], mixd_ref[...], weights)


def _ffn2(h_all, mix, hd, mixd, wout, n2, wg, wu, wd, nf):
    nb, seq, _ = mix.shape
    seq_all = h_all.shape[1]
    per_b = seq // FFN_TM
    n_tiles = nb * per_b
    h_all = h_all.reshape(nb * seq_all, D_MODEL)
    mix = mix.reshape(nb * seq, D_MODEL)
    idx = lambda i: jnp.clip(i - W2_STEPS, 0, n_tiles - 1)
    tile = lambda i: (idx(i), 0)
    assert seq_all % FFN_TM == 0
    tile_all = lambda i: ((idx(i) // per_b) * (seq_all // FFN_TM) + idx(i) % per_b, 0)
    chunk = lambda i: (jnp.minimum(i, W2_STEPS - 1), 0)
    yp, yd = pl.pallas_call(
        _ffn2_kernel,
        grid=(W2_STEPS + n_tiles + 1,),
        in_specs=[
            pl.BlockSpec((FFN_TM, D_MODEL), tile_all),
            pl.BlockSpec((FFN_TM, D_MODEL), tile),
            _whole(hd.shape),
            _whole(mixd.shape),
            pl.BlockSpec((D_MODEL // W2_STEPS, D_MODEL), chunk),
            pl.BlockSpec((D_MODEL // W2_STEPS, D_FF), chunk),
            pl.BlockSpec((D_MODEL // W2_STEPS, D_FF), chunk),
            pl.BlockSpec((D_FF // W2_STEPS, D_MODEL), chunk),
            _whole((1, D_MODEL)),
            _whole((1, D_MODEL)),
        ],
        out_specs=[
            pl.BlockSpec((FFN_TM, D_MODEL), tile),
            pl.BlockSpec(hd.shape, lambda i: (0, 0)),
        ],
        out_shape=[
            jax.ShapeDtypeStruct((nb * seq, D_MODEL), jnp.float32),
            jax.ShapeDtypeStruct(hd.shape, jnp.float32),
        ],
        scratch_shapes=[
            pltpu.VMEM((D_MODEL, D_MODEL), jnp.bfloat16),
            pltpu.VMEM((D_MODEL, D_FF), jnp.bfloat16),
            pltpu.VMEM((D_MODEL, D_FF), jnp.bfloat16),
            pltpu.VMEM((D_FF, D_MODEL), jnp.bfloat16),
        ],
        compiler_params=pltpu.CompilerParams(
            dimension_semantics=("arbitrary",), vmem_limit_bytes=VMEM_LIMIT),
        name="outproj_ffn2",
    )(h_all, mix, hd, mixd, wout, wg, wu, wd, n2, nf)
    return yp.reshape(nb, seq, D_MODEL), yd


def _mixer_step_kernel(u_ref, v_ref, g_ref, hre_ref, him_ref, buf_ref, are_ref, aim_ref, bb_ref, cre_ref,
                       cim_ref, d_ref, wglu_ref, bglu_ref, cw_ref, mix_ref, sre_ref, sim_ref, nconv_ref):
    u = u_ref[...]
    u_bf = u.astype(jnp.bfloat16)
    ys = []
    for k in range(N_BLK):
        bu = jnp.dot(u_bf[:, k * LANES:(k + 1) * LANES], bb_ref[k], preferred_element_type=jnp.float32)
        st = slice(k * BLK_STATE, (k + 1) * BLK_STATE)
        a_re = are_ref[k:k + 1, :]
        a_im = aim_ref[k:k + 1, :]
        h_re = hre_ref[:, st]
        h_im = him_ref[:, st]
        x_re = (a_re * h_re + bu[:, :BLK_STATE]) - a_im * h_im
        x_im = (a_re * h_im + bu[:, BLK_STATE:]) + a_im * h_re
        sre_ref[:, st] = x_re
        sim_ref[:, st] = x_im
        ys.append(_bdot(x_re, cre_ref[k]) - _bdot(x_im, cim_ref[k]))
    y = jnp.concatenate(ys, axis=1)
    mix_ref[:, 0:SSM_WIDTH] = _glu_tail(y, u, d_ref, wglu_ref, bglu_ref).astype(mix_ref.dtype)

    v = v_ref[...]
    z = cw_ref[0:1, :] * buf_ref[0] + cw_ref[1:2, :] * buf_ref[1] + cw_ref[2:3, :] * v
    mix_ref[:, SSM_WIDTH:] = (g_ref[...] * z).astype(mix_ref.dtype)
    nconv_ref[0] = buf_ref[1]
    nconv_ref[1] = v


def _mixer_step(u, v, g, h_re, h_im, buf, a_re, a_im, bb, cre, cim, d, wglu, bglu, cw):
    nb = u.shape[0]
    return pl.pallas_call(
        _mixer_step_kernel,
        out_shape=[
            jax.ShapeDtypeStruct((nb, D_MODEL), jnp.bfloat16),
            jax.ShapeDtypeStruct((nb, N_STATE), jnp.float32),
            jax.ShapeDtypeStruct((nb, N_STATE), jnp.float32),
            jax.ShapeDtypeStruct((CONV_K - 1, nb, CONV_WIDTH), jnp.float32),
        ],
        compiler_params=pltpu.CompilerParams(vmem_limit_bytes=VMEM_LIMIT),
        name="mixer_step",
    )(u, v, g, h_re, h_im, buf, a_re, a_im, bb, cre, cim, d, wglu, bglu, cw)


def _ssm_layout(lam_re, lam_im, log_dt, b_re, b_im, c_re, c_im):
    gpb = N_SSM_GROUPS // N_BLK
    eye = jnp.eye(gpb, dtype=jnp.float32)
    lr = lam_re.reshape(N_BLK, BLK_STATE)
    li = lam_im.reshape(N_BLK, BLK_STATE)
    ldt = jnp.broadcast_to(log_dt[:, None], (N_SSM_GROUPS, SSM_STATE)).reshape(N_BLK, BLK_STATE)

    def b_blocks(b):
        b = b.reshape(N_BLK, gpb, SSM_STATE, SSM_GROUP).transpose(0, 1, 3, 2)
        return jnp.einsum("kgcp,gh->kgchp", b, eye).reshape(N_BLK, LANES, BLK_STATE)

    def c_blocks(c):
        c = c.reshape(N_BLK, gpb, SSM_GROUP, SSM_STATE).transpose(0, 1, 3, 2)
        return jnp.einsum("kgpc,gh->kgphc", c, eye).reshape(N_BLK, BLK_STATE, LANES).astype(jnp.bfloat16)

    return lr, li, ldt, b_blocks(b_re), b_blocks(b_im), c_blocks(c_re), c_blocks(c_im)


def kernel(x_prompt, x_sample, state_ssm_re, state_ssm_im, state_conv, ffn1_norm, ffn1_w_gate, ffn1_w_up, ffn1_w_down, mix_norm, w_in, ssm_lambda_re, ssm_lambda_im, ssm_log_dt, ssm_b_re, ssm_b_im, ssm_c_re, ssm_c_im, ssm_d, ssm_w_glu, ssm_b_glu, conv_w, w_out, ffn2_norm, ffn2_w_gate, ffn2_w_up, ffn2_w_down, final_norm):
    assert ffn1_norm.shape[0] == 1, "single-layer trunk"
    n_prompt, seq, _ = x_prompt.shape
    n_dec, dec_seq, _ = x_sample.shape
    assert dec_seq == 1 and n_prompt == SUBLANES and seq % FFN_TM == 0 and seq % SCAN_T == 0
    assert n_dec % SCAN_T == 0 and n_dec <= n_prompt * SCAN_T and FFN_TM % FFN_SUB == 0

    n1 = ffn1_norm[0][None, :]
    nm = mix_norm[0][None, :]
    n2 = ffn2_norm[0][None, :]
    nf = final_norm[None, :]
    wglu = ssm_w_glu[0].astype(jnp.bfloat16)
    lr, li, ldt, bre, bim, cre, cim = _ssm_layout(
        ssm_lambda_re[0], ssm_lambda_im[0], ssm_log_dt[0], ssm_b_re[0], ssm_b_im[0], ssm_c_re[0], ssm_c_im[0])
    a_re, a_im, bb = _discretize(lr, li, ldt, bre, bim)
    a_re_t = jnp.broadcast_to(a_re.reshape(N_TILE, 1, LANES), (N_TILE, SUBLANES, LANES))
    a_im_t = jnp.broadcast_to(a_im.reshape(N_TILE, 1, LANES), (N_TILE, SUBLANES, LANES))
    d = ssm_d[0][None, :]
    bglu = ssm_b_glu[0][None, :]
    cw = conv_w[0]

    h_all, mixp, p_re, p_im, p_conv, ud, vd, gd = _ffn1_mixer(
        x_prompt, x_sample.reshape(n_dec, D_MODEL), n1, ffn1_w_gate[0], ffn1_w_up[0], ffn1_w_down[0], nm,
        w_in[0], a_re_t, a_im_t, bb, cre, cim, d, wglu, bglu, cw)

    hd = h_all[:n_dec // SCAN_T, seq:seq + SCAN_T, :].reshape(n_dec, D_MODEL)
    s_re0 = state_ssm_re[0].reshape(n_dec, N_STATE)
    s_im0 = state_ssm_im[0].reshape(n_dec, N_STATE)
    buf = state_conv[0].transpose(1, 0, 2)
    mixd, s_re, s_im, s_conv = _mixer_step(ud, vd, gd, s_re0, s_im0, buf, a_re, a_im, bb, cre, cim,
                                           d, wglu, bglu, cw)

    yp, yd = _ffn2(h_all, mixp, hd, mixd, w_out[0], n2, ffn2_w_gate[0], ffn2_w_up[0], ffn2_w_down[0], nf)

    gs = (N_SSM_GROUPS, SSM_STATE)
    return (yp,
            yd.reshape(n_dec, 1, D_MODEL),
            p_re.reshape(1, n_prompt, *gs),
            p_im.reshape(1, n_prompt, *gs),
            p_conv[None],
            s_re.reshape(1, n_dec, *gs),
            s_im.reshape(1, n_dec, *gs),
            s_conv.transpose(1, 0, 2)[None])
```

```python
import jax
import jax.numpy as jnp
from jax import lax
from jax.experimental import pallas as pl
from jax.experimental.pallas import tpu as pltpu

D_MODEL = 1024
D_FF = 2816
SSM_WIDTH = 512
CONV_WIDTH = 512
SSM_GROUP = 16
N_SSM_GROUPS = 32
SSM_STATE = 64
CONV_K = 3
IN_PROJ_WIDTH = SSM_WIDTH + 3 * CONV_WIDTH
EPS = 1e-6

LANES = 128
SUBLANES = 8
N_STATE = N_SSM_GROUPS * SSM_STATE
N_BLK = SSM_WIDTH // LANES
BLK_STATE = N_STATE // N_BLK
SLABS_PER_BLK = 2 * BLK_STATE // LANES
HALF = SLABS_PER_BLK // 2
N_TILE = N_BLK * HALF

FFN_TM = 1024
FFN_SUB = 512
W1_STEPS = 16
W2_STEPS = 8
FFN_TF = 256
SCAN_T = 64
PITCH = SCAN_T + 4
VMEM_LIMIT = 58 * 1024 * 1024


def _rms(x, g):
    r = lax.rsqrt(jnp.mean(x * x, axis=-1, keepdims=True) + EPS)
    return x * r * g


def _bdot(a, b):
    return jnp.dot(a.astype(jnp.bfloat16), b.astype(jnp.bfloat16), preferred_element_type=jnp.float32)


def _swiglu_chunk(xn, wg_ref, wu_ref, wd_ref, j):
    g = jnp.dot(xn, wg_ref[:, j:j + FFN_TF], preferred_element_type=jnp.float32)
    u = jnp.dot(xn, wu_ref[:, j:j + FFN_TF], preferred_element_type=jnp.float32)
    a = (g * jax.nn.sigmoid(g)) * u
    return jnp.dot(a.astype(jnp.bfloat16), wd_ref[j:j + FFN_TF, :], preferred_element_type=jnp.float32)


def _swiglu(xn, wg_ref, wu_ref, wd_ref):
    acc = jnp.zeros((xn.shape[0], D_MODEL), jnp.float32)
    for j in range(0, D_FF, FFN_TF):
        acc = acc + _swiglu_chunk(xn, wg_ref, wu_ref, wd_ref, j)
    return acc


def _cast_weight_chunks(step, pairs):
    for src, dst in pairs:
        n = src.shape[0]
        dst[pl.ds(pl.multiple_of(step * n, n), n), :] = src[...].astype(dst.dtype)


def _whole(shape):
    return pl.BlockSpec(shape, lambda i: (0,) * len(shape), pipeline_mode=pl.Buffered(1))


def _glu_tail(y, u, d_ref, wglu_ref, bglu_ref):
    y = y + d_ref[...] * u
    z = jax.nn.gelu(y)
    gate = jax.nn.sigmoid(_bdot(z, wglu_ref[...]) + bglu_ref[...])
    return z * gate


def _discretize_kernel(lr_ref, li_ref, ldt_ref, bre_ref, bim_ref, are_ref, aim_ref, bb_ref):
    lr, li = lr_ref[...], li_ref[...]
    dt = jnp.exp(ldt_ref[...])
    mag = jnp.exp(lr * dt)
    ab_re = mag * jnp.cos(li * dt)
    ab_im = mag * jnp.sin(li * dt)
    den = lr * lr + li * li
    nr = ab_re - 1.0
    ni = ab_im
    coef_re = (nr * lr + ni * li) / den
    coef_im = (ni * lr - nr * li) / den
    are_ref[...] = ab_re
    aim_ref[...] = ab_im
    for k in range(N_BLK):
        cr = coef_re[k:k + 1, :]
        ci = coef_im[k:k + 1, :]
        bre, bim = bre_ref[k], bim_ref[k]
        bb_ref[k, :, 0:BLK_STATE] = (cr * bre - ci * bim).astype(bb_ref.dtype)
        bb_ref[k, :, BLK_STATE:] = (cr * bim + ci * bre).astype(bb_ref.dtype)


def _discretize(lr, li, ldt, bre, bim):
    return pl.pallas_call(
        _discretize_kernel,
        out_shape=[
            jax.ShapeDtypeStruct((N_BLK, BLK_STATE), jnp.float32),
            jax.ShapeDtypeStruct((N_BLK, BLK_STATE), jnp.float32),
            jax.ShapeDtypeStruct((N_BLK, LANES, 2 * BLK_STATE), jnp.bfloat16),
        ],
        name="ssm_discretize",
    )(lr, li, ldt, bre, bim)


def _ffn1_mixer_kernel(x_ref, xd_ref, n1_ref, wg_c, wu_c, wd_c, nm_ref, win_c,
                       are_ref, aim_ref, bb_ref, cre_ref, cim_ref, d_ref, wglu_ref, bglu_ref, cw_ref,
                       h_ref, mix_ref, sre_ref, sim_ref, nconv_ref, ud_ref, vd_ref, gd_ref,
                       xs0_ref, xs1_ref, xs2_ref, xs3_ref, u_ref, y_ref, v_ref, g_ref, state_ref,
                       wg_ref, wu_ref, wd_ref, win_ref):
    xs_refs = (xs0_ref, xs1_ref, xs2_ref, xs3_ref)
    step = pl.program_id(0)
    last = pl.num_programs(0) - 1
    nb, t_len, _ = x_ref.shape
    n_dec = xd_ref.shape[0]
    tiles = [(k, j) for k in range(N_BLK) for j in range(HALF)]

    @pl.when(step < W1_STEPS)
    def _load_weights():
        _cast_weight_chunks(step, ((wg_c, wg_ref), (wu_c, wu_ref), (wd_c, wd_ref), (win_c, win_ref)))

    @pl.when(step == 0)
    def _init():
        state_ref[...] = jnp.zeros_like(state_ref)
        u_ref[...] = jnp.zeros_like(u_ref)
        v_ref[...] = jnp.zeros_like(v_ref)
        g_ref[...] = jnp.zeros_like(g_ref)

    @pl.when(step >= W1_STEPS)
    def _main():
        carry = {}

        def bbar_u(ks):
            for k in ks:
                u_bf = u_ref[:, k * LANES:(k + 1) * LANES].astype(jnp.bfloat16)
                bu = jnp.dot(u_bf, bb_ref[k], preferred_element_type=jnp.float32)
                for j in range(SLABS_PER_BLK):
                    xs_refs[k][j] = bu[:, j * LANES:(j + 1) * LANES]

        def scan(t0, t1):
            if t0 == 0:
                carry["re"] = [state_ref[k * SLABS_PER_BLK + j] for k, j in tiles]
                carry["im"] = [state_ref[k * SLABS_PER_BLK + HALF + j] for k, j in tiles]
            xr, xi = carry["re"], carry["im"]
            for t in range(t0, t1):
                rows = pl.ds(t, SUBLANES, stride=PITCH)
                for n, (k, j) in enumerate(tiles):
                    a_re = are_ref[k * HALF + j]
                    a_im = aim_ref[k * HALF + j]
                    nr = (a_re * xr[n] + xs_refs[k][j, rows, :]) - a_im * xi[n]
                    ni = (a_re * xi[n] + xs_refs[k][HALF + j, rows, :]) + a_im * xr[n]
                    xs_refs[k][j, rows, :] = nr
                    xs_refs[k][HALF + j, rows, :] = ni
                    xr[n], xi[n] = nr, ni
            if t1 == t_len:
                for n, (k, j) in enumerate(tiles):
                    state_ref[k * SLABS_PER_BLK + j] = xr[n]
                    state_ref[k * SLABS_PER_BLK + HALF + j] = xi[n]

        def c_proj(ks):
            for k in ks:
                x_re = jnp.concatenate([xs_refs[k][j] for j in range(HALF)], axis=1)
                x_im = jnp.concatenate([xs_refs[k][HALF + j] for j in range(HALF)], axis=1)
                y_ref[:, k * LANES:(k + 1) * LANES] = _bdot(x_re, cre_ref[k]) - _bdot(x_im, cim_ref[k])

        def glu():
            y_ref[...] = _glu_tail(y_ref[...], u_ref[...], d_ref, wglu_ref, bglu_ref)

        def emit_ssm():
            for b in range(nb):
                mix_ref[b, :, 0:SSM_WIDTH] = y_ref[pl.ds(b * PITCH, t_len), :].astype(mix_ref.dtype)

        def conv():
            v = v_ref[:, SUBLANES:SUBLANES + t_len, :]
            z = cw_ref[0:1, :] * v_ref[:, SUBLANES - 2:SUBLANES - 2 + t_len, :]
            z = z + cw_ref[1:2, :] * v_ref[:, SUBLANES - 1:SUBLANES - 1 + t_len, :]
            z = z + cw_ref[2:3, :] * v
            mix_ref[:, :, SSM_WIDTH:] = (g_ref[...] * z).astype(mix_ref.dtype)
            v_ref[:, 0:SUBLANES, :] = v_ref[:, t_len:t_len + SUBLANES, :]

        q = t_len // 4
        pieces = [
            lambda: (conv(), bbar_u((0,))), lambda: bbar_u((1,)), lambda: bbar_u((2,)), lambda: bbar_u((3,)),
            lambda: scan(0, q), lambda: scan(q, 2 * q), lambda: scan(2 * q, 3 * q), lambda: scan(3 * q, t_len),
            lambda: c_proj((0, 1)), lambda: c_proj((2, 3)),
            lambda: (glu(), emit_ssm()),
        ]
        assert len(pieces) <= D_FF // FFN_TF

        x_dec = jnp.concatenate(
            [xd_ref[...], jnp.zeros((nb * t_len - n_dec, D_MODEL), jnp.float32)], axis=0)
        x = jnp.where(step == last, x_dec, x_ref[...].reshape(nb * t_len, D_MODEL))
        xn = _rms(x, n1_ref[...]).astype(jnp.bfloat16)
        acc = jnp.zeros((nb * t_len, D_MODEL), jnp.float32)
        for c, j in enumerate(range(0, D_FF, FFN_TF)):
            acc = acc + _swiglu_chunk(xn, wg_ref, wu_ref, wd_ref, j)
            if c < len(pieces):
                pieces[c]()
        h = x + 0.5 * acc
        hn = _rms(h, nm_ref[...]).astype(jnp.bfloat16)
        p = jnp.dot(hn, win_ref[...], preferred_element_type=jnp.float32)
        h_ref[...] = h.reshape(nb, t_len, D_MODEL)
        v_new = p[:, SSM_WIDTH + 2 * CONV_WIDTH:] * p[:, SSM_WIDTH:SSM_WIDTH + CONV_WIDTH]
        v_ref[:, SUBLANES:SUBLANES + t_len, :] = v_new.reshape(nb, t_len, CONV_WIDTH)
        g_ref[...] = p[:, SSM_WIDTH + CONV_WIDTH:SSM_WIDTH + 2 * CONV_WIDTH].reshape(nb, t_len, CONV_WIDTH)
        for b in range(nb):
            u_ref[pl.ds(b * PITCH, t_len), :] = p[b * t_len:(b + 1) * t_len, 0:SSM_WIDTH]

    @pl.when(step == last)
    def _final():
        for n, (k, j) in enumerate(tiles):
            lanes = slice(k * BLK_STATE + j * LANES, k * BLK_STATE + (j + 1) * LANES)
            sre_ref[:, lanes] = state_ref[k * SLABS_PER_BLK + j]
            sim_ref[:, lanes] = state_ref[k * SLABS_PER_BLK + HALF + j]
        nconv_ref[...] = v_ref[:, SUBLANES - 2:SUBLANES, :]
        for b in range(n_dec // t_len):
            ud_ref[b * t_len:(b + 1) * t_len, :] = u_ref[pl.ds(b * PITCH, t_len), :]
            vd_ref[b * t_len:(b + 1) * t_len, :] = v_ref[b, SUBLANES:SUBLANES + t_len, :]
            gd_ref[b * t_len:(b + 1) * t_len, :] = g_ref[b]


def _ffn1_mixer(x, xd, n1, wg, wu, wd, nm, win, a_re, a_im, bb, cre, cim, d, wglu, bglu, cw):
    nb, seq, _ = x.shape
    n_dec = xd.shape[0]
    n_chunk = seq // SCAN_T
    rows = nb * PITCH
    main = lambda i: jnp.maximum(i - W1_STEPS, 0)
    chunk = lambda i: (jnp.minimum(i, W1_STEPS - 1), 0)
    return pl.pallas_call(
        _ffn1_mixer_kernel,
        grid=(W1_STEPS + n_chunk + 1,),
        in_specs=[
            pl.BlockSpec((nb, SCAN_T, D_MODEL), lambda i: (0, jnp.minimum(main(i), n_chunk - 1), 0)),
            _whole((n_dec, D_MODEL)),
            _whole((1, D_MODEL)),
            pl.BlockSpec((D_MODEL // W1_STEPS, D_FF), chunk),
            pl.BlockSpec((D_MODEL // W1_STEPS, D_FF), chunk),
            pl.BlockSpec((D_FF // W1_STEPS, D_MODEL), chunk),
            _whole((1, D_MODEL)),
            pl.BlockSpec((D_MODEL // W1_STEPS, IN_PROJ_WIDTH), chunk),
            _whole((N_TILE, SUBLANES, LANES)),
            _whole((N_TILE, SUBLANES, LANES)),
            _whole((N_BLK, LANES, 2 * BLK_STATE)),
            _whole((N_BLK, BLK_STATE, LANES)),
            _whole((N_BLK, BLK_STATE, LANES)),
            _whole((1, SSM_WIDTH)),
            _whole((SSM_WIDTH, SSM_WIDTH)),
            _whole((1, SSM_WIDTH)),
            _whole((CONV_K, CONV_WIDTH)),
        ],
        out_specs=[
            pl.BlockSpec((nb, SCAN_T, D_MODEL), lambda i: (0, main(i), 0)),
            pl.BlockSpec((nb, SCAN_T, D_MODEL), lambda i: (0, jnp.clip(main(i) - 1, 0, n_chunk - 1), 0)),
            pl.BlockSpec((nb, N_STATE), lambda i: (0, 0)),
            pl.BlockSpec((nb, N_STATE), lambda i: (0, 0)),
            pl.BlockSpec((nb, CONV_K - 1, CONV_WIDTH), lambda i: (0, 0, 0)),
            pl.BlockSpec((n_dec, SSM_WIDTH), lambda i: (0, 0)),
            pl.BlockSpec((n_dec, CONV_WIDTH), lambda i: (0, 0)),
            pl.BlockSpec((n_dec, CONV_WIDTH), lambda i: (0, 0)),
        ],
        out_shape=[
            jax.ShapeDtypeStruct((nb, seq + FFN_TM, D_MODEL), jnp.float32),
            jax.ShapeDtypeStruct((nb, seq, D_MODEL), jnp.bfloat16),
            jax.ShapeDtypeStruct((nb, N_STATE), jnp.float32),
            jax.ShapeDtypeStruct((nb, N_STATE), jnp.float32),
            jax.ShapeDtypeStruct((nb, CONV_K - 1, CONV_WIDTH), jnp.float32),
            jax.ShapeDtypeStruct((n_dec, SSM_WIDTH), jnp.float32),
            jax.ShapeDtypeStruct((n_dec, CONV_WIDTH), jnp.float32),
            jax.ShapeDtypeStruct((n_dec, CONV_WIDTH), jnp.float32),
        ],
        scratch_shapes=[
            *[pltpu.VMEM((SLABS_PER_BLK, rows, LANES), jnp.float32) for _ in range(N_BLK)],
            pltpu.VMEM((rows, SSM_WIDTH), jnp.float32),
            pltpu.VMEM((rows, SSM_WIDTH), jnp.float32),
            pltpu.VMEM((nb, SCAN_T + SUBLANES, CONV_WIDTH), jnp.float32),
            pltpu.VMEM((nb, SCAN_T, CONV_WIDTH), jnp.float32),
            pltpu.VMEM((2 * N_TILE, SUBLANES, LANES), jnp.float32),
            pltpu.VMEM((D_MODEL, D_FF), jnp.bfloat16),
            pltpu.VMEM((D_MODEL, D_FF), jnp.bfloat16),
            pltpu.VMEM((D_FF, D_MODEL), jnp.bfloat16),
            pltpu.VMEM((D_MODEL, IN_PROJ_WIDTH), jnp.bfloat16),
        ],
        compiler_params=pltpu.CompilerParams(
            dimension_semantics=("arbitrary",), vmem_limit_bytes=VMEM_LIMIT),
        name="ffn1_mixer",
    )(x, xd, n1, wg, wu, wd, nm, win, a_re, a_im, bb, cre, cim, d, wglu, bglu, cw)


def _ffn2_rows(h, mix, weights):
    wout_ref, n2_ref, wg_ref, wu_ref, wd_ref, nf_ref = weights
    h = h + jnp.dot(mix, wout_ref[...], preferred_element_type=jnp.float32)
    hn = _rms(h, n2_ref[...]).astype(jnp.bfloat16)
    h = h + 0.5 * _swiglu(hn, wg_ref, wu_ref, wd_ref)
    return _rms(h, nf_ref[...])


def _ffn2_kernel(h_ref, mix_ref, hd_ref, mixd_ref, wout_c, wg_c, wu_c, wd_c, n2_ref, nf_ref,
                 y_ref, yd_ref, wout_ref, wg_ref, wu_ref, wd_ref):
    step = pl.program_id(0)
    last = pl.num_programs(0) - 1
    weights = (wout_ref, n2_ref, wg_ref, wu_ref, wd_ref, nf_ref)

    @pl.when(step < W2_STEPS)
    def _load_weights():
        _cast_weight_chunks(step, ((wout_c, wout_ref), (wg_c, wg_ref), (wu_c, wu_ref), (wd_c, wd_ref)))

    @pl.when((step >= W2_STEPS) & (step < last))
    def _prompt_tile():
        for r in range(0, h_ref.shape[0], FFN_SUB):
            rows = slice(r, r + FFN_SUB)
            y_ref[rows, :] = _ffn2_rows(h_ref[rows, :], mix_ref[rows, :], weights)

    @pl.when(step == last)
    def _decode_rows():
        yd_ref[...] = _ffn2_rows(hd_ref[...], mixd_ref[...], weights)


def _ffn2(h_all, mix, hd, mixd, wout, n2, wg, wu, wd, nf):
    nb, seq, _ = mix.shape
    seq_all = h_all.shape[1]
    per_b = seq // FFN_TM
    n_tiles = nb * per_b
    h_all = h_all.reshape(nb * seq_all, D_MODEL)
    mix = mix.reshape(nb * seq, D_MODEL)
    idx = lambda i: jnp.clip(i - W2_STEPS, 0, n_tiles - 1)
    tile = lambda i: (idx(i), 0)
    assert seq_all % FFN_TM == 0
    tile_all = lambda i: ((idx(i) // per_b) * (seq_all // FFN_TM) + idx(i) % per_b, 0)
    chunk = lambda i: (jnp.minimum(i, W2_STEPS - 1), 0)
    yp, yd = pl.pallas_call(
        _ffn2_kernel,
        grid=(W2_STEPS + n_tiles + 1,),
        in_specs=[
            pl.BlockSpec((FFN_TM, D_MODEL), tile_all),
            pl.BlockSpec((FFN_TM, D_MODEL), tile),
            _whole(hd.shape),
            _whole(mixd.shape),
            pl.BlockSpec((D_MODEL // W2_STEPS, D_MODEL), chunk),
            pl.BlockSpec((D_MODEL // W2_STEPS, D_FF), chunk),
            pl.BlockSpec((D_MODEL // W2_STEPS, D_FF), chunk),
            pl.BlockSpec((D_FF // W2_STEPS, D_MODEL), chunk),
            _whole((1, D_MODEL)),
            _whole((1, D_MODEL)),
        ],
        out_specs=[
            pl.BlockSpec((FFN_TM, D_MODEL), tile),
            pl.BlockSpec(hd.shape, lambda i: (0, 0)),
        ],
        out_shape=[
            jax.ShapeDtypeStruct((nb * seq, D_MODEL), jnp.float32),
            jax.ShapeDtypeStruct(hd.shape, jnp.float32),
        ],
        scratch_shapes=[
            pltpu.VMEM((D_MODEL, D_MODEL), jnp.bfloat16),
            pltpu.VMEM((D_MODEL, D_FF), jnp.bfloat16),
            pltpu.VMEM((D_MODEL, D_FF), jnp.bfloat16),
            pltpu.VMEM((D_FF, D_MODEL), jnp.bfloat16),
        ],
        compiler_params=pltpu.CompilerParams(
            dimension_semantics=("arbitrary",), vmem_limit_bytes=VMEM_LIMIT),
        name="outproj_ffn2",
    )(h_all, mix, hd, mixd, wout, wg, wu, wd, n2, nf)
    return yp.reshape(nb, seq, D_MODEL), yd


def _mixer_step_kernel(u_ref, v_ref, g_ref, hre_ref, him_ref, buf_ref, are_ref, aim_ref, bb_ref, cre_ref,
                       cim_ref, d_ref, wglu_ref, bglu_ref, cw_ref, mix_ref, sre_ref, sim_ref, nconv_ref):
    u = u_ref[...]
    u_bf = u.astype(jnp.bfloat16)
    ys = []
    for k in range(N_BLK):
        bu = jnp.dot(u_bf[:, k * LANES:(k + 1) * LANES], bb_ref[k], preferred_element_type=jnp.float32)
        st = slice(k * BLK_STATE, (k + 1) * BLK_STATE)
        a_re = are_ref[k:k + 1, :]
        a_im = aim_ref[k:k + 1, :]
        h_re = hre_ref[:, st]
        h_im = him_ref[:, st]
        x_re = (a_re * h_re + bu[:, :BLK_STATE]) - a_im * h_im
        x_im = (a_re * h_im + bu[:, BLK_STATE:]) + a_im * h_re
        sre_ref[:, st] = x_re
        sim_ref[:, st] = x_im
        ys.append(_bdot(x_re, cre_ref[k]) - _bdot(x_im, cim_ref[k]))
    y = jnp.concatenate(ys, axis=1)
    mix_ref[:, 0:SSM_WIDTH] = _glu_tail(y, u, d_ref, wglu_ref, bglu_ref).astype(mix_ref.dtype)

    v = v_ref[...]
    z = cw_ref[0:1, :] * buf_ref[0] + cw_ref[1:2, :] * buf_ref[1] + cw_ref[2:3, :] * v
    mix_ref[:, SSM_WIDTH:] = (g_ref[...] * z).astype(mix_ref.dtype)
    nconv_ref[0] = buf_ref[1]
    nconv_ref[1] = v


def _mixer_step(u, v, g, h_re, h_im, buf, a_re, a_im, bb, cre, cim, d, wglu, bglu, cw):
    nb = u.shape[0]
    return pl.pallas_call(
        _mixer_step_kernel,
        out_shape=[
            jax.ShapeDtypeStruct((nb, D_MODEL), jnp.bfloat16),
            jax.ShapeDtypeStruct((nb, N_STATE), jnp.float32),
            jax.ShapeDtypeStruct((nb, N_STATE), jnp.float32),
            jax.ShapeDtypeStruct((CONV_K - 1, nb, CONV_WIDTH), jnp.float32),
        ],
        compiler_params=pltpu.CompilerParams(vmem_limit_bytes=VMEM_LIMIT),
        name="mixer_step",
    )(u, v, g, h_re, h_im, buf, a_re, a_im, bb, cre, cim, d, wglu, bglu, cw)


def _ssm_layout(lam_re, lam_im, log_dt, b_re, b_im, c_re, c_im):
    gpb = N_SSM_GROUPS // N_BLK
    eye = jnp.eye(gpb, dtype=jnp.float32)
    lr = lam_re.reshape(N_BLK, BLK_STATE)
    li = lam_im.reshape(N_BLK, BLK_STATE)
    ldt = jnp.broadcast_to(log_dt[:, None], (N_SSM_GROUPS, SSM_STATE)).reshape(N_BLK, BLK_STATE)

    def b_blocks(b):
        b = b.reshape(N_BLK, gpb, SSM_STATE, SSM_GROUP).transpose(0, 1, 3, 2)
        return jnp.einsum("kgcp,gh->kgchp", b, eye).reshape(N_BLK, LANES, BLK_STATE)

    def c_blocks(c):
        c = c.reshape(N_BLK, gpb, SSM_GROUP, SSM_STATE).transpose(0, 1, 3, 2)
        return jnp.einsum("kgpc,gh->kgphc", c, eye).reshape(N_BLK, BLK_STATE, LANES).astype(jnp.bfloat16)

    return lr, li, ldt, b_blocks(b_re), b_blocks(b_im), c_blocks(c_re), c_blocks(c_im)


def kernel(x_prompt, x_sample, state_ssm_re, state_ssm_im, state_conv, ffn1_norm, ffn1_w_gate, ffn1_w_up, ffn1_w_down, mix_norm, w_in, ssm_lambda_re, ssm_lambda_im, ssm_log_dt, ssm_b_re, ssm_b_im, ssm_c_re, ssm_c_im, ssm_d, ssm_w_glu, ssm_b_glu, conv_w, w_out, ffn2_norm, ffn2_w_gate, ffn2_w_up, ffn2_w_down, final_norm):
    assert ffn1_norm.shape[0] == 1, "single-layer trunk"
    n_prompt, seq, _ = x_prompt.shape
    n_dec, dec_seq, _ = x_sample.shape
    assert dec_seq == 1 and n_prompt == SUBLANES and seq % FFN_TM == 0 and seq % SCAN_T == 0
    assert n_dec % SCAN_T == 0 and n_dec <= n_prompt * SCAN_T and FFN_TM % FFN_SUB == 0

    n1 = ffn1_norm[0][None, :]
    nm = mix_norm[0][None, :]
    n2 = ffn2_norm[0][None, :]
    nf = final_norm[None, :]
    wglu = ssm_w_glu[0].astype(jnp.bfloat16)
    lr, li, ldt, bre, bim, cre, cim = _ssm_layout(
        ssm_lambda_re[0], ssm_lambda_im[0], ssm_log_dt[0], ssm_b_re[0], ssm_b_im[0], ssm_c_re[0], ssm_c_im[0])
    a_re, a_im, bb = _discretize(lr, li, ldt, bre, bim)
    a_re_t = jnp.broadcast_to(a_re.reshape(N_TILE, 1, LANES), (N_TILE, SUBLANES, LANES))
    a_im_t = jnp.broadcast_to(a_im.reshape(N_TILE, 1, LANES), (N_TILE, SUBLANES, LANES))
    d = ssm_d[0][None, :]
    bglu = ssm_b_glu[0][None, :]
    cw = conv_w[0]

    h_all, mixp, p_re, p_im, p_conv, ud, vd, gd = _ffn1_mixer(
        x_prompt, x_sample.reshape(n_dec, D_MODEL), n1, ffn1_w_gate[0], ffn1_w_up[0], ffn1_w_down[0], nm,
        w_in[0], a_re_t, a_im_t, bb, cre, cim, d, wglu, bglu, cw)

    hd = h_all[:n_dec // SCAN_T, seq:seq + SCAN_T, :].reshape(n_dec, D_MODEL)
    s_re0 = state_ssm_re[0].reshape(n_dec, N_STATE)
    s_im0 = state_ssm_im[0].reshape(n_dec, N_STATE)
    buf = state_conv[0].transpose(1, 0, 2)
    mixd, s_re, s_im, s_conv = _mixer_step(ud, vd, gd, s_re0, s_im0, buf, a_re, a_im, bb, cre, cim,
                                           d, wglu, bglu, cw)

    yp, yd = _ffn2(h_all, mixp, hd, mixd, w_out[0], n2, ffn2_w_gate[0], ffn2_w_up[0], ffn2_w_down[0], nf)

    gs = (N_SSM_GROUPS, SSM_STATE)
    return (yp,
            yd.reshape(n_dec, 1, D_MODEL),
            p_re.reshape(1, n_prompt, *gs),
            p_im.reshape(1, n_prompt, *gs),
            p_conv[None],
            s_re.reshape(1, n_dec, *gs),
            s_im.reshape(1, n_dec, *gs),
            s_conv.transpose(1, 0, 2)[None])
```

```python
import jax
import jax.numpy as jnp
from jax import lax
from jax.experimental import pallas as pl
from jax.experimental.pallas import tpu as pltpu

D_MODEL = 1024
D_FF = 2816
SSM_WIDTH = 512
CONV_WIDTH = 512
SSM_GROUP = 16
N_SSM_GROUPS = 32
SSM_STATE = 64
CONV_K = 3
IN_PROJ_WIDTH = SSM_WIDTH + 3 * CONV_WIDTH
EPS = 1e-6

LANES = 128
SUBLANES = 8
N_STATE = N_SSM_GROUPS * SSM_STATE
N_BLK = SSM_WIDTH // LANES
BLK_STATE = N_STATE // N_BLK
SLABS_PER_BLK = 2 * BLK_STATE // LANES
HALF = SLABS_PER_BLK // 2
N_TILE = N_BLK * HALF

FFN_TM = 1024
FFN_SUB = 512
W1_STEPS = 16
W2_STEPS = 8
FFN_TF = 256
SCAN_T = 64
PITCH = SCAN_T + 4
VMEM_LIMIT = 58 * 1024 * 1024


def _rms(x, g):
    r = lax.rsqrt(jnp.mean(x * x, axis=-1, keepdims=True) + EPS)
    return x * r * g


def _bdot(a, b):
    return jnp.dot(a.astype(jnp.bfloat16), b.astype(jnp.bfloat16), preferred_element_type=jnp.float32)


def _swiglu_chunk(xn, wg_ref, wu_ref, wd_ref, j):
    g = jnp.dot(xn, wg_ref[:, j:j + FFN_TF], preferred_element_type=jnp.float32)
    u = jnp.dot(xn, wu_ref[:, j:j + FFN_TF], preferred_element_type=jnp.float32)
    a = (g * jax.nn.sigmoid(g)) * u
    return jnp.dot(a.astype(jnp.bfloat16), wd_ref[j:j + FFN_TF, :], preferred_element_type=jnp.float32)


def _swiglu(xn, wg_ref, wu_ref, wd_ref):
    acc = jnp.zeros((xn.shape[0], D_MODEL), jnp.float32)
    for j in range(0, D_FF, FFN_TF):
        acc = acc + _swiglu_chunk(xn, wg_ref, wu_ref, wd_ref, j)
    return acc


def _cast_weight_chunks(step, pairs):
    for src, dst in pairs:
        n = src.shape[0]
        dst[pl.ds(pl.multiple_of(step * n, n), n), :] = src[...].astype(dst.dtype)


def _whole(shape):
    return pl.BlockSpec(shape, lambda i: (0,) * len(shape), pipeline_mode=pl.Buffered(1))


def _glu_tail(y, u, d_ref, wglu_ref, bglu_ref):
    y = y + d_ref[...] * u
    z = jax.nn.gelu(y)
    gate = jax.nn.sigmoid(_bdot(z, wglu_ref[...]) + bglu_ref[...])
    return z * gate


def _discretize_kernel(lr_ref, li_ref, ldt_ref, bre_ref, bim_ref, are_ref, aim_ref, bb_ref):
    lr, li = lr_ref[...], li_ref[...]
    dt = jnp.exp(ldt_ref[...])
    mag = jnp.exp(lr * dt)
    ab_re = mag * jnp.cos(li * dt)
    ab_im = mag * jnp.sin(li * dt)
    den = lr * lr + li * li
    nr = ab_re - 1.0
    ni = ab_im
    coef_re = (nr * lr + ni * li) / den
    coef_im = (ni * lr - nr * li) / den
    are_ref[...] = ab_re
    aim_ref[...] = ab_im
    for k in range(N_BLK):
        cr = coef_re[k:k + 1, :]
        ci = coef_im[k:k + 1, :]
        bre, bim = bre_ref[k], bim_ref[k]
        bb_ref[k, :, 0:BLK_STATE] = (cr * bre - ci * bim).astype(bb_ref.dtype)
        bb_ref[k, :, BLK_STATE:] = (cr * bim + ci * bre).astype(bb_ref.dtype)


def _discretize(lr, li, ldt, bre, bim):
    return pl.pallas_call(
        _discretize_kernel,
        out_shape=[
            jax.ShapeDtypeStruct((N_BLK, BLK_STATE), jnp.float32),
            jax.ShapeDtypeStruct((N_BLK, BLK_STATE), jnp.float32),
            jax.ShapeDtypeStruct((N_BLK, LANES, 2 * BLK_STATE), jnp.bfloat16),
        ],
        name="ssm_discretize",
    )(lr, li, ldt, bre, bim)


def _ffn1_mixer_kernel(x_ref, xd_ref, n1_ref, wg_c, wu_c, wd_c, nm_ref, win_c,
                       are_ref, aim_ref, bb_ref, cre_ref, cim_ref, d_ref, wglu_ref, bglu_ref, cw_ref,
                       h_ref, mix_ref, sre_ref, sim_ref, nconv_ref, ud_ref, vd_ref, gd_ref,
                       xs0_ref, xs1_ref, xs2_ref, xs3_ref, u_ref, y_ref, v_ref, g_ref, state_ref,
                       wg_ref, wu_ref, wd_ref, win_ref):
    xs_refs = (xs0_ref, xs1_ref, xs2_ref, xs3_ref)
    step = pl.program_id(0)
    last = pl.num_programs(0) - 1
    nb, t_len, _ = x_ref.shape
    n_dec = xd_ref.shape[0]
    tiles = [(k, j) for k in range(N_BLK) for j in range(HALF)]

    @pl.when(step < W1_STEPS)
    def _load_weights():
        _cast_weight_chunks(step, ((wg_c, wg_ref), (wu_c, wu_ref), (wd_c, wd_ref), (win_c, win_ref)))

    @pl.when(step == 0)
    def _init():
        state_ref[...] = jnp.zeros_like(state_ref)
        u_ref[...] = jnp.zeros_like(u_ref)
        v_ref[...] = jnp.zeros_like(v_ref)
        g_ref[...] = jnp.zeros_like(g_ref)

    @pl.when(step >= W1_STEPS)
    def _main():
        carry = {}

        def bbar_u(ks):
            for k in ks:
                u_bf = u_ref[:, k * LANES:(k + 1) * LANES].astype(jnp.bfloat16)
                bu = jnp.dot(u_bf, bb_ref[k], preferred_element_type=jnp.float32)
                for j in range(SLABS_PER_BLK):
                    xs_refs[k][j] = bu[:, j * LANES:(j + 1) * LANES]

        def scan(t0, t1):
            if t0 == 0:
                carry["re"] = [state_ref[k * SLABS_PER_BLK + j] for k, j in tiles]
                carry["im"] = [state_ref[k * SLABS_PER_BLK + HALF + j] for k, j in tiles]
            xr, xi = carry["re"], carry["im"]
            for t in range(t0, t1):
                rows = pl.ds(t, SUBLANES, stride=PITCH)
                for n, (k, j) in enumerate(tiles):
                    a_re = are_ref[k * HALF + j]
                    a_im = aim_ref[k * HALF + j]
                    nr = (a_re * xr[n] + xs_refs[k][j, rows, :]) - a_im * xi[n]
                    ni = (a_re * xi[n] + xs_refs[k][HALF + j, rows, :]) + a_im * xr[n]
                    xs_refs[k][j, rows, :] = nr
                    xs_refs[k][HALF + j, rows, :] = ni
                    xr[n], xi[n] = nr, ni
            if t1 == t_len:
                for n, (k, j) in enumerate(tiles):
                    state_ref[k * SLABS_PER_BLK + j] = xr[n]
                    state_ref[k * SLABS_PER_BLK + HALF + j] = xi[n]

        def c_proj(ks):
            for k in ks:
                x_re = jnp.concatenate([xs_refs[k][j] for j in range(HALF)], axis=1)
                x_im = jnp.concatenate([xs_refs[k][HALF + j] for j in range(HALF)], axis=1)
                y_ref[:, k * LANES:(k + 1) * LANES] = _bdot(x_re, cre_ref[k]) - _bdot(x_im, cim_ref[k])

        def glu():
            y_ref[...] = _glu_tail(y_ref[...], u_ref[...], d_ref, wglu_ref, bglu_ref)

        def emit_ssm():
            for b in range(nb):
                mix_ref[b, :, 0:SSM_WIDTH] = y_ref[pl.ds(b * PITCH, t_len), :].astype(mix_ref.dtype)

        def conv():
            v = v_ref[:, SUBLANES:SUBLANES + t_len, :]
            z = cw_ref[0:1, :] * v_ref[:, SUBLANES - 2:SUBLANES - 2 + t_len, :]
            z = z + cw_ref[1:2, :] * v_ref[:, SUBLANES - 1:SUBLANES - 1 + t_len, :]
            z = z + cw_ref[2:3, :] * v
            mix_ref[:, :, SSM_WIDTH:] = (g_ref[...] * z).astype(mix_ref.dtype)
            v_ref[:, 0:SUBLANES, :] = v_ref[:, t_len:t_len + SUBLANES, :]

        q = t_len // 4
        pieces = [
            lambda: (conv(), bbar_u((0,))), lambda: bbar_u((1,)), lambda: bbar_u((2,)), lambda: bbar_u((3,)),
            lambda: scan(0, q), lambda: scan(q, 2 * q), lambda: scan(2 * q, 3 * q), lambda: scan(3 * q, t_len),
            lambda: c_proj((0, 1)), lambda: c_proj((2, 3)),
            lambda: (glu(), emit_ssm()),
        ]
        assert len(pieces) <= D_FF // FFN_TF

        x_dec = jnp.concatenate(
            [xd_ref[...], jnp.zeros((nb * t_len - n_dec, D_MODEL), jnp.float32)], axis=0)
        x = jnp.where(step == last, x_dec, x_ref[...].reshape(nb * t_len, D_MODEL))
        xn = _rms(x, n1_ref[...]).astype(jnp.bfloat16)
        acc = jnp.zeros((nb * t_len, D_MODEL), jnp.float32)
        for c, j in enumerate(range(0, D_FF, FFN_TF)):
            acc = acc + _swiglu_chunk(xn, wg_ref, wu_ref, wd_ref, j)
            if c < len(pieces):
                pieces[c]()
        h = x + 0.5 * acc
        hn = _rms(h, nm_ref[...]).astype(jnp.bfloat16)
        p = jnp.dot(hn, win_ref[...], preferred_element_type=jnp.float32)
        h_ref[...] = h.reshape(nb, t_len, D_MODEL)
        v_new = p[:, SSM_WIDTH + 2 * CONV_WIDTH:] * p[:, SSM_WIDTH:SSM_WIDTH + CONV_WIDTH]
        v_ref[:, SUBLANES:SUBLANES + t_len, :] = v_new.reshape(nb, t_len, CONV_WIDTH)
        g_ref[...] = p[:, SSM_WIDTH + CONV_WIDTH:SSM_WIDTH + 2 * CONV_WIDTH].reshape(nb, t_len, CONV_WIDTH)
        for b in range(nb):
            u_ref[pl.ds(b * PITCH, t_len), :] = p[b * t_len:(b + 1) * t_len, 0:SSM_WIDTH]

    @pl.when(step == last)
    def _final():
        for n, (k, j) in enumerate(tiles):
            lanes = slice(k * BLK_STATE + j * LANES, k * BLK_STATE + (j + 1) * LANES)
            sre_ref[:, lanes] = state_ref[k * SLABS_PER_BLK + j]
            sim_ref[:, lanes] = state_ref[k * SLABS_PER_BLK + HALF + j]
        nconv_ref[...] = v_ref[:, SUBLANES - 2:SUBLANES, :]
        for b in range(n_dec // t_len):
            ud_ref[b * t_len:(b + 1) * t_len, :] = u_ref[pl.ds(b * PITCH, t_len), :]
            vd_ref[b * t_len:(b + 1) * t_len, :] = v_ref[b, SUBLANES:SUBLANES + t_len, :]
            gd_ref[b * t_len:(b + 1) * t_len, :] = g_ref[b]


def _ffn1_mixer(x, xd, n1, wg, wu, wd, nm, win, a_re, a_im, bb, cre, cim, d, wglu, bglu, cw):
    nb, seq, _ = x.shape
    n_dec = xd.shape[0]
    n_chunk = seq // SCAN_T
    rows = nb * PITCH
    main = lambda i: jnp.maximum(i - W1_STEPS, 0)
    chunk = lambda i: (jnp.minimum(i, W1_STEPS - 1), 0)
    return pl.pallas_call(
        _ffn1_mixer_kernel,
        grid=(W1_STEPS + n_chunk + 1,),
        in_specs=[
            pl.BlockSpec((nb, SCAN_T, D_MODEL), lambda i: (0, jnp.minimum(main(i), n_chunk - 1), 0)),
            _whole((n_dec, D_MODEL)),
            _whole((1, D_MODEL)),
            pl.BlockSpec((D_MODEL // W1_STEPS, D_FF), chunk),
            pl.BlockSpec((D_MODEL // W1_STEPS, D_FF), chunk),
            pl.BlockSpec((D_FF // W1_STEPS, D_MODEL), chunk),
            _whole((1, D_MODEL)),
            pl.BlockSpec((D_MODEL // W1_STEPS, IN_PROJ_WIDTH), chunk),
            _whole((N_TILE, SUBLANES, LANES)),
            _whole((N_TILE, SUBLANES, LANES)),
            _whole((N_BLK, LANES, 2 * BLK_STATE)),
            _whole((N_BLK, BLK_STATE, LANES)),
            _whole((N_BLK, BLK_STATE, LANES)),
            _whole((1, SSM_WIDTH)),
            _whole((SSM_WIDTH, SSM_WIDTH)),
            _whole((1, SSM_WIDTH)),
            _whole((CONV_K, CONV_WIDTH)),
        ],
        out_specs=[
            pl.BlockSpec((nb, SCAN_T, D_MODEL), lambda i: (0, main(i), 0)),
            pl.BlockSpec((nb, SCAN_T, D_MODEL), lambda i: (0, jnp.clip(main(i) - 1, 0, n_chunk - 1), 0)),
            pl.BlockSpec((nb, N_STATE), lambda i: (0, 0)),
            pl.BlockSpec((nb, N_STATE), lambda i: (0, 0)),
            pl.BlockSpec((nb, CONV_K - 1, CONV_WIDTH), lambda i: (0, 0, 0)),
            pl.BlockSpec((n_dec, SSM_WIDTH), lambda i: (0, 0)),
            pl.BlockSpec((n_dec, CONV_WIDTH), lambda i: (0, 0)),
            pl.BlockSpec((n_dec, CONV_WIDTH), lambda i: (0, 0)),
        ],
        out_shape=[
            jax.ShapeDtypeStruct((nb, seq + SCAN_T, D_MODEL), jnp.float32),
            jax.ShapeDtypeStruct((nb, seq, D_MODEL), jnp.bfloat16),
            jax.ShapeDtypeStruct((nb, N_STATE), jnp.float32),
            jax.ShapeDtypeStruct((nb, N_STATE), jnp.float32),
            jax.ShapeDtypeStruct((nb, CONV_K - 1, CONV_WIDTH), jnp.float32),
            jax.ShapeDtypeStruct((n_dec, SSM_WIDTH), jnp.float32),
            jax.ShapeDtypeStruct((n_dec, CONV_WIDTH), jnp.float32),
            jax.ShapeDtypeStruct((n_dec, CONV_WIDTH), jnp.float32),
        ],
        scratch_shapes=[
            *[pltpu.VMEM((SLABS_PER_BLK, rows, LANES), jnp.float32) for _ in range(N_BLK)],
            pltpu.VMEM((rows, SSM_WIDTH), jnp.float32),
            pltpu.VMEM((rows, SSM_WIDTH), jnp.float32),
            pltpu.VMEM((nb, SCAN_T + SUBLANES, CONV_WIDTH), jnp.float32),
            pltpu.VMEM((nb, SCAN_T, CONV_WIDTH), jnp.float32),
            pltpu.VMEM((2 * N_TILE, SUBLANES, LANES), jnp.float32),
            pltpu.VMEM((D_MODEL, D_FF), jnp.bfloat16),
            pltpu.VMEM((D_MODEL, D_FF), jnp.bfloat16),
            pltpu.VMEM((D_FF, D_MODEL), jnp.bfloat16),
            pltpu.VMEM((D_MODEL, IN_PROJ_WIDTH), jnp.bfloat16),
        ],
        compiler_params=pltpu.CompilerParams(
            dimension_semantics=("arbitrary",), vmem_limit_bytes=VMEM_LIMIT),
        name="ffn1_mixer",
    )(x, xd, n1, wg, wu, wd, nm, win, a_re, a_im, bb, cre, cim, d, wglu, bglu, cw)


def _ffn2_rows(h, mix, weights):
    wout_ref, n2_ref, wg_ref, wu_ref, wd_ref, nf_ref = weights
    h = h + jnp.dot(mix, wout_ref[...], preferred_element_type=jnp.float32)
    hn = _rms(h, n2_ref[...]).astype(jnp.bfloat16)
    h = h + 0.5 * _swiglu(hn, wg_ref, wu_ref, wd_ref)
    return _rms(h, nf_ref[...])


def _ffn2_kernel(h_ref, mix_ref, hd_ref, mixd_ref, wout_c, wg_c, wu_c, wd_c, n2_ref, nf_ref,
                 y_ref, yd_ref, wout_ref, wg_ref, wu_ref, wd_ref):
    step = pl.program_id(0)
    last = pl.num_programs(0) - 1
    weights = (wout_ref, n2_ref, wg_ref, wu_ref, wd_ref, nf_ref)

    @pl.when(step < W2_STEPS)
    def _load_weights():
        _cast_weight_chunks(step, ((wout_c, wout_ref), (wg_c, wg_ref), (wu_c, wu_ref), (wd_c, wd_ref)))

    @pl.when((step >= W2_STEPS) & (step < last))
    def _prompt_tile():
        for r in range(0, h_ref.shape[0], FFN_SUB):
            rows = slice(r, r + FFN_SUB)
            y_ref[rows, :] = _ffn2_rows(h_ref[rows, :], mix_ref[rows, :], weights)

    @pl.when(step == last)
    def _decode_rows():
        yd_ref[...] = _ffn2_rows(hd_ref[...], mixd_ref[...], weights)


def _ffn2(h_all, mix, hd, mixd, wout, n2, wg, wu, wd, nf):
    nb, seq, _ = mix.shape
    seq_all = h_all.shape[1]
    per_b = seq // FFN_TM
    n_tiles = nb * per_b
    h_all = h_all.reshape(nb * seq_all, D_MODEL)
    mix = mix.reshape(nb * seq, D_MODEL)
    idx = lambda i: jnp.clip(i - W2_STEPS, 0, n_tiles - 1)
    tile = lambda i: (idx(i), 0)
    assert seq_all % SUBLANES == 0
    tile_all = lambda i: (pl.multiple_of((idx(i) // per_b) * seq_all + (idx(i) % per_b) * FFN_TM, SUBLANES), 0)
    chunk = lambda i: (jnp.minimum(i, W2_STEPS - 1), 0)
    yp, yd = pl.pallas_call(
        _ffn2_kernel,
        grid=(W2_STEPS + n_tiles + 1,),
        in_specs=[
            pl.BlockSpec((pl.Element(FFN_TM), pl.Element(D_MODEL)), tile_all),
            pl.BlockSpec((FFN_TM, D_MODEL), tile),
            _whole(hd.shape),
            _whole(mixd.shape),
            pl.BlockSpec((D_MODEL // W2_STEPS, D_MODEL), chunk),
            pl.BlockSpec((D_MODEL // W2_STEPS, D_FF), chunk),
            pl.BlockSpec((D_MODEL // W2_STEPS, D_FF), chunk),
            pl.BlockSpec((D_FF // W2_STEPS, D_MODEL), chunk),
            _whole((1, D_MODEL)),
            _whole((1, D_MODEL)),
        ],
        out_specs=[
            pl.BlockSpec((FFN_TM, D_MODEL), tile),
            pl.BlockSpec(hd.shape, lambda i: (0, 0)),
        ],
        out_shape=[
            jax.ShapeDtypeStruct((nb * seq, D_MODEL), jnp.float32),
            jax.ShapeDtypeStruct(hd.shape, jnp.float32),
        ],
        scratch_shapes=[
            pltpu.VMEM((D_MODEL, D_MODEL), jnp.bfloat16),
            pltpu.VMEM((D_MODEL, D_FF), jnp.bfloat16),
            pltpu.VMEM((D_MODEL, D_FF), jnp.bfloat16),
            pltpu.VMEM((D_FF, D_MODEL), jnp.bfloat16),
        ],
        compiler_params=pltpu.CompilerParams(
            dimension_semantics=("arbitrary",), vmem_limit_bytes=VMEM_LIMIT),
        name="outproj_ffn2",
    )(h_all, mix, hd, mixd, wout, wg, wu, wd, n2, nf)
    return yp.reshape(nb, seq, D_MODEL), yd


def _mixer_step_kernel(u_ref, v_ref, g_ref, hre_ref, him_ref, buf_ref, are_ref, aim_ref, bb_ref, cre_ref,
                       cim_ref, d_ref, wglu_ref, bglu_ref, cw_ref, mix_ref, sre_ref, sim_ref, nconv_ref):
    u = u_ref[...]
    u_bf = u.astype(jnp.bfloat16)
    ys = []
    for k in range(N_BLK):
        bu = jnp.dot(u_bf[:, k * LANES:(k + 1) * LANES], bb_ref[k], preferred_element_type=jnp.float32)
        st = slice(k * BLK_STATE, (k + 1) * BLK_STATE)
        a_re = are_ref[k:k + 1, :]
        a_im = aim_ref[k:k + 1, :]
        h_re = hre_ref[:, st]
        h_im = him_ref[:, st]
        x_re = (a_re * h_re + bu[:, :BLK_STATE]) - a_im * h_im
        x_im = (a_re * h_im + bu[:, BLK_STATE:]) + a_im * h_re
        sre_ref[:, st] = x_re
        sim_ref[:, st] = x_im
        ys.append(_bdot(x_re, cre_ref[k]) - _bdot(x_im, cim_ref[k]))
    y = jnp.concatenate(ys, axis=1)
    mix_ref[:, 0:SSM_WIDTH] = _glu_tail(y, u, d_ref, wglu_ref, bglu_ref).astype(mix_ref.dtype)

    v = v_ref[...]
    z = cw_ref[0:1, :] * buf_ref[0] + cw_ref[1:2, :] * buf_ref[1] + cw_ref[2:3, :] * v
    mix_ref[:, SSM_WIDTH:] = (g_ref[...] * z).astype(mix_ref.dtype)
    nconv_ref[0] = buf_ref[1]
    nconv_ref[1] = v


def _mixer_step(u, v, g, h_re, h_im, buf, a_re, a_im, bb, cre, cim, d, wglu, bglu, cw):
    nb = u.shape[0]
    return pl.pallas_call(
        _mixer_step_kernel,
        out_shape=[
            jax.ShapeDtypeStruct((nb, D_MODEL), jnp.bfloat16),
            jax.ShapeDtypeStruct((nb, N_STATE), jnp.float32),
            jax.ShapeDtypeStruct((nb, N_STATE), jnp.float32),
            jax.ShapeDtypeStruct((CONV_K - 1, nb, CONV_WIDTH), jnp.float32),
        ],
        compiler_params=pltpu.CompilerParams(vmem_limit_bytes=VMEM_LIMIT),
        name="mixer_step",
    )(u, v, g, h_re, h_im, buf, a_re, a_im, bb, cre, cim, d, wglu, bglu, cw)


def _ssm_layout(lam_re, lam_im, log_dt, b_re, b_im, c_re, c_im):
    gpb = N_SSM_GROUPS // N_BLK
    eye = jnp.eye(gpb, dtype=jnp.float32)
    lr = lam_re.reshape(N_BLK, BLK_STATE)
    li = lam_im.reshape(N_BLK, BLK_STATE)
    ldt = jnp.broadcast_to(log_dt[:, None], (N_SSM_GROUPS, SSM_STATE)).reshape(N_BLK, BLK_STATE)

    def b_blocks(b):
        b = b.reshape(N_BLK, gpb, SSM_STATE, SSM_GROUP).transpose(0, 1, 3, 2)
        return jnp.einsum("kgcp,gh->kgchp", b, eye).reshape(N_BLK, LANES, BLK_STATE)

    def c_blocks(c):
        c = c.reshape(N_BLK, gpb, SSM_GROUP, SSM_STATE).transpose(0, 1, 3, 2)
        return jnp.einsum("kgpc,gh->kgphc", c, eye).reshape(N_BLK, BLK_STATE, LANES).astype(jnp.bfloat16)

    return lr, li, ldt, b_blocks(b_re), b_blocks(b_im), c_blocks(c_re), c_blocks(c_im)


def kernel(x_prompt, x_sample, state_ssm_re, state_ssm_im, state_conv, ffn1_norm, ffn1_w_gate, ffn1_w_up, ffn1_w_down, mix_norm, w_in, ssm_lambda_re, ssm_lambda_im, ssm_log_dt, ssm_b_re, ssm_b_im, ssm_c_re, ssm_c_im, ssm_d, ssm_w_glu, ssm_b_glu, conv_w, w_out, ffn2_norm, ffn2_w_gate, ffn2_w_up, ffn2_w_down, final_norm):
    assert ffn1_norm.shape[0] == 1, "single-layer trunk"
    n_prompt, seq, _ = x_prompt.shape
    n_dec, dec_seq, _ = x_sample.shape
    assert dec_seq == 1 and n_prompt == SUBLANES and seq % FFN_TM == 0 and seq % SCAN_T == 0
    assert n_dec % SCAN_T == 0 and n_dec <= n_prompt * SCAN_T and FFN_TM % FFN_SUB == 0

    n1 = ffn1_norm[0][None, :]
    nm = mix_norm[0][None, :]
    n2 = ffn2_norm[0][None, :]
    nf = final_norm[None, :]
    wglu = ssm_w_glu[0].astype(jnp.bfloat16)
    lr, li, ldt, bre, bim, cre, cim = _ssm_layout(
        ssm_lambda_re[0], ssm_lambda_im[0], ssm_log_dt[0], ssm_b_re[0], ssm_b_im[0], ssm_c_re[0], ssm_c_im[0])
    a_re, a_im, bb = _discretize(lr, li, ldt, bre, bim)
    a_re_t = jnp.broadcast_to(a_re.reshape(N_TILE, 1, LANES), (N_TILE, SUBLANES, LANES))
    a_im_t = jnp.broadcast_to(a_im.reshape(N_TILE, 1, LANES), (N_TILE, SUBLANES, LANES))
    d = ssm_d[0][None, :]
    bglu = ssm_b_glu[0][None, :]
    cw = conv_w[0]

    h_all, mixp, p_re, p_im, p_conv, ud, vd, gd = _ffn1_mixer(
        x_prompt, x_sample.reshape(n_dec, D_MODEL), n1, ffn1_w_gate[0], ffn1_w_up[0], ffn1_w_down[0], nm,
        w_in[0], a_re_t, a_im_t, bb, cre, cim, d, wglu, bglu, cw)

    hd = h_all[:n_dec // SCAN_T, seq:seq + SCAN_T, :].reshape(n_dec, D_MODEL)
    s_re0 = state_ssm_re[0].reshape(n_dec, N_STATE)
    s_im0 = state_ssm_im[0].reshape(n_dec, N_STATE)
    buf = state_conv[0].transpose(1, 0, 2)
    mixd, s_re, s_im, s_conv = _mixer_step(ud, vd, gd, s_re0, s_im0, buf, a_re, a_im, bb, cre, cim,
                                           d, wglu, bglu, cw)

    yp, yd = _ffn2(h_all, mixp, hd, mixd, w_out[0], n2, ffn2_w_gate[0], ffn2_w_up[0], ffn2_w_down[0], nf)

    gs = (N_SSM_GROUPS, SSM_STATE)
    return (yp,
            yd.reshape(n_dec, 1, D_MODEL),
            p_re.reshape(1, n_prompt, *gs),
            p_im.reshape(1, n_prompt, *gs),
            p_conv[None],
            s_re.reshape(1, n_dec, *gs),
            s_im.reshape(1, n_dec, *gs),
            s_conv.transpose(1, 0, 2)[None])
```

```python
import jax
import jax.numpy as jnp
from jax import lax
from jax.experimental import pallas as pl
from jax.experimental.pallas import tpu as pltpu

D_MODEL = 1024
D_FF = 2816
SSM_WIDTH = 512
CONV_WIDTH = 512
SSM_GROUP = 16
N_SSM_GROUPS = 32
SSM_STATE = 64
CONV_K = 3
IN_PROJ_WIDTH = SSM_WIDTH + 3 * CONV_WIDTH
EPS = 1e-6

LANES = 128
SUBLANES = 8
N_STATE = N_SSM_GROUPS * SSM_STATE
N_BLK = SSM_WIDTH // LANES
BLK_STATE = N_STATE // N_BLK
SLABS_PER_BLK = 2 * BLK_STATE // LANES
HALF = SLABS_PER_BLK // 2
N_TILE = N_BLK * HALF

FFN_TM = 1024
FFN_SUB = 512
W1_STEPS = 16
W2_STEPS = 8
FFN_TF = 256
SCAN_T = 64
PITCH = SCAN_T + 4
VMEM_LIMIT = 62 * 1024 * 1024


def _rms(x, g):
    r = lax.rsqrt(jnp.mean(x * x, axis=-1, keepdims=True) + EPS)
    return x * r * g


def _bdot(a, b):
    return jnp.dot(a.astype(jnp.bfloat16), b.astype(jnp.bfloat16), preferred_element_type=jnp.float32)


def _swiglu_chunk(xn, wg_ref, wu_ref, wd_ref, j):
    g = jnp.dot(xn, wg_ref[:, j:j + FFN_TF], preferred_element_type=jnp.float32)
    u = jnp.dot(xn, wu_ref[:, j:j + FFN_TF], preferred_element_type=jnp.float32)
    a = (g * jax.nn.sigmoid(g)) * u
    return jnp.dot(a.astype(jnp.bfloat16), wd_ref[j:j + FFN_TF, :], preferred_element_type=jnp.float32)


def _swiglu(xn, wg_ref, wu_ref, wd_ref):
    acc = jnp.zeros((xn.shape[0], D_MODEL), jnp.float32)
    for j in range(0, D_FF, FFN_TF):
        acc = acc + _swiglu_chunk(xn, wg_ref, wu_ref, wd_ref, j)
    return acc


def _swiglu_staged(xn, wg_ref, wu_ref, wd_ref, a_ref):
    rows = xn.shape[0]
    _swiglu_stage(xn, wg_ref, wu_ref, a_ref, 0, D_FF)
    return jnp.dot(a_ref[0:rows, :], wd_ref[...], preferred_element_type=jnp.float32)


def _swiglu_stage(xn, wg_ref, wu_ref, a_ref, lo, hi):
    rows = xn.shape[0]
    for j in range(lo, hi, FFN_TF):
        g = jnp.dot(xn, wg_ref[:, j:j + FFN_TF], preferred_element_type=jnp.float32)
        u = jnp.dot(xn, wu_ref[:, j:j + FFN_TF], preferred_element_type=jnp.float32)
        a_ref[0:rows, j:j + FFN_TF] = ((g * jax.nn.sigmoid(g)) * u).astype(a_ref.dtype)


def _cast_weight_chunks(step, pairs):
    for src, dst in pairs:
        n = src.shape[0]
        dst[pl.ds(pl.multiple_of(step * n, n), n), :] = src[...].astype(dst.dtype)


def _whole(shape):
    return pl.BlockSpec(shape, lambda i: (0,) * len(shape), pipeline_mode=pl.Buffered(1))


def _weight_chunk(shape, n_steps):
    return pl.BlockSpec(shape, lambda i: (jnp.minimum(i, n_steps - 1), 0))


def _glu_tail(y, u, d_ref, wglu_ref, bglu_ref):
    y = y + d_ref[...] * u
    z = jax.nn.gelu(y)
    gate = jax.nn.sigmoid(_bdot(z, wglu_ref[...]) + bglu_ref[...])
    return z * gate


def _discretize_kernel(lr_ref, li_ref, ldt_ref, bre_ref, bim_ref, are_ref, aim_ref, bb_ref):
    lr, li = lr_ref[...], li_ref[...]
    dt = jnp.exp(ldt_ref[...])
    mag = jnp.exp(lr * dt)
    ab_re = mag * jnp.cos(li * dt)
    ab_im = mag * jnp.sin(li * dt)
    den = lr * lr + li * li
    nr = ab_re - 1.0
    ni = ab_im
    coef_re = (nr * lr + ni * li) / den
    coef_im = (ni * lr - nr * li) / den
    are_ref[...] = ab_re
    aim_ref[...] = ab_im
    for k in range(N_BLK):
        cr = coef_re[k:k + 1, :]
        ci = coef_im[k:k + 1, :]
        bre, bim = bre_ref[k], bim_ref[k]
        bb_ref[k, :, 0:BLK_STATE] = (cr * bre - ci * bim).astype(bb_ref.dtype)
        bb_ref[k, :, BLK_STATE:] = (cr * bim + ci * bre).astype(bb_ref.dtype)


def _discretize(lr, li, ldt, bre, bim):
    return pl.pallas_call(
        _discretize_kernel,
        out_shape=[
            jax.ShapeDtypeStruct((N_BLK, BLK_STATE), jnp.float32),
            jax.ShapeDtypeStruct((N_BLK, BLK_STATE), jnp.float32),
            jax.ShapeDtypeStruct((N_BLK, LANES, 2 * BLK_STATE), jnp.bfloat16),
        ],
        name="ssm_discretize",
    )(lr, li, ldt, bre, bim)


def _ffn1_mixer_kernel(x_ref, xd_ref, n1_ref, wg_c, wu_c, wd_c, nm_ref, win_c,
                       are_ref, aim_ref, bb_ref, cre_ref, cim_ref, d_ref, wglu_ref, bglu_ref, cw_ref,
                       h_ref, mix_ref, sre_ref, sim_ref, nconv_ref, ud_ref, vd_ref, gd_ref,
                       xs0_ref, xs1_ref, xs2_ref, xs3_ref, u_ref, y_ref, v_ref, g_ref, state_ref,
                       wg_ref, wu_ref, wd_ref, win_ref, a_ref):
    xs_refs = (xs0_ref, xs1_ref, xs2_ref, xs3_ref)
    step = pl.program_id(0)
    last = pl.num_programs(0) - 1
    nb, t_len, _ = x_ref.shape
    n_dec = xd_ref.shape[0]
    tiles = [(k, j) for k in range(N_BLK) for j in range(HALF)]

    @pl.when(step < W1_STEPS)
    def _load_weights():
        _cast_weight_chunks(step, ((wg_c, wg_ref), (wu_c, wu_ref), (wd_c, wd_ref), (win_c, win_ref)))

    @pl.when(step == 0)
    def _init():
        state_ref[...] = jnp.zeros_like(state_ref)
        u_ref[...] = jnp.zeros_like(u_ref)
        v_ref[...] = jnp.zeros_like(v_ref)
        g_ref[...] = jnp.zeros_like(g_ref)

    @pl.when(step >= W1_STEPS)
    def _main():
        carry = {}

        def bbar_u(ks):
            for k in ks:
                u_bf = u_ref[:, k * LANES:(k + 1) * LANES].astype(jnp.bfloat16)
                bu = jnp.dot(u_bf, bb_ref[k], preferred_element_type=jnp.float32)
                for j in range(SLABS_PER_BLK):
                    xs_refs[k][j] = bu[:, j * LANES:(j + 1) * LANES]

        def scan(t0, t1):
            if t0 == 0:
                carry["re"] = [state_ref[k * SLABS_PER_BLK + j] for k, j in tiles]
                carry["im"] = [state_ref[k * SLABS_PER_BLK + HALF + j] for k, j in tiles]
            xr, xi = carry["re"], carry["im"]
            for t in range(t0, t1):
                rows = pl.ds(t, SUBLANES, stride=PITCH)
                for n, (k, j) in enumerate(tiles):
                    a_re = are_ref[k * HALF + j]
                    a_im = aim_ref[k * HALF + j]
                    nr = (a_re * xr[n] + xs_refs[k][j, rows, :]) - a_im * xi[n]
                    ni = (a_re * xi[n] + xs_refs[k][HALF + j, rows, :]) + a_im * xr[n]
                    xs_refs[k][j, rows, :] = nr
                    xs_refs[k][HALF + j, rows, :] = ni
                    xr[n], xi[n] = nr, ni
            if t1 == t_len:
                for n, (k, j) in enumerate(tiles):
                    state_ref[k * SLABS_PER_BLK + j] = xr[n]
                    state_ref[k * SLABS_PER_BLK + HALF + j] = xi[n]

        def c_proj(ks):
            for k in ks:
                x_re = jnp.concatenate([xs_refs[k][j] for j in range(HALF)], axis=1)
                x_im = jnp.concatenate([xs_refs[k][HALF + j] for j in range(HALF)], axis=1)
                y_ref[:, k * LANES:(k + 1) * LANES] = _bdot(x_re, cre_ref[k]) - _bdot(x_im, cim_ref[k])

        def glu():
            y_ref[...] = _glu_tail(y_ref[...], u_ref[...], d_ref, wglu_ref, bglu_ref)

        def emit_ssm():
            for b in range(nb):
                mix_ref[b, :, 0:SSM_WIDTH] = y_ref[pl.ds(b * PITCH, t_len), :].astype(mix_ref.dtype)

        def conv():
            v = v_ref[:, SUBLANES:SUBLANES + t_len, :]
            z = cw_ref[0:1, :] * v_ref[:, SUBLANES - 2:SUBLANES - 2 + t_len, :]
            z = z + cw_ref[1:2, :] * v_ref[:, SUBLANES - 1:SUBLANES - 1 + t_len, :]
            z = z + cw_ref[2:3, :] * v
            mix_ref[:, :, SSM_WIDTH:] = (g_ref[...] * z).astype(mix_ref.dtype)
            v_ref[:, 0:SUBLANES, :] = v_ref[:, t_len:t_len + SUBLANES, :]

        q = t_len // 4
        pieces = [
            lambda: (conv(), bbar_u((0,))), lambda: bbar_u((1,)), lambda: bbar_u((2,)), lambda: bbar_u((3,)),
            lambda: scan(0, q), lambda: scan(q, 2 * q), lambda: scan(2 * q, 3 * q), lambda: scan(3 * q, t_len),
            lambda: c_proj((0, 1)), lambda: c_proj((2, 3)),
            lambda: (glu(), emit_ssm()),
        ]
        assert len(pieces) <= D_FF // FFN_TF

        db = n_dec // t_len
        x_top = jnp.where(step == last, xd_ref[...], x_ref[0:db].reshape(n_dec, D_MODEL))
        x = jnp.concatenate([x_top, x_ref[db:].reshape((nb - db) * t_len, D_MODEL)], axis=0)
        xn = _rms(x, n1_ref[...]).astype(jnp.bfloat16)
        for c, j in enumerate(range(0, D_FF, FFN_TF)):
            _swiglu_stage(xn, wg_ref, wu_ref, a_ref, j, j + FFN_TF)
            if c < len(pieces):
                pieces[c]()
        h = x + 0.5 * jnp.dot(a_ref[...], wd_ref[...], preferred_element_type=jnp.float32)
        hn = _rms(h, nm_ref[...]).astype(jnp.bfloat16)
        p = jnp.dot(hn, win_ref[...], preferred_element_type=jnp.float32)
        h_ref[...] = h.reshape(nb, t_len, D_MODEL)
        v_new = p[:, SSM_WIDTH + 2 * CONV_WIDTH:] * p[:, SSM_WIDTH:SSM_WIDTH + CONV_WIDTH]
        v_ref[:, SUBLANES:SUBLANES + t_len, :] = v_new.reshape(nb, t_len, CONV_WIDTH)
        g_ref[...] = p[:, SSM_WIDTH + CONV_WIDTH:SSM_WIDTH + 2 * CONV_WIDTH].reshape(nb, t_len, CONV_WIDTH)
        for b in range(nb):
            u_ref[pl.ds(b * PITCH, t_len), :] = p[b * t_len:(b + 1) * t_len, 0:SSM_WIDTH]

    @pl.when(step == last)
    def _final():
        for n, (k, j) in enumerate(tiles):
            lanes = slice(k * BLK_STATE + j * LANES, k * BLK_STATE + (j + 1) * LANES)
            sre_ref[:, lanes] = state_ref[k * SLABS_PER_BLK + j]
            sim_ref[:, lanes] = state_ref[k * SLABS_PER_BLK + HALF + j]
        nconv_ref[...] = v_ref[:, SUBLANES - 2:SUBLANES, :]
        for b in range(n_dec // t_len):
            ud_ref[b * t_len:(b + 1) * t_len, :] = u_ref[pl.ds(b * PITCH, t_len), :]
            vd_ref[b * t_len:(b + 1) * t_len, :] = v_ref[b, SUBLANES:SUBLANES + t_len, :]
            gd_ref[b * t_len:(b + 1) * t_len, :] = g_ref[b]


def _ffn1_mixer(x, xd, n1, wg, wu, wd, nm, win, a_re, a_im, bb, cre, cim, d, wglu, bglu, cw):
    nb, seq, _ = x.shape
    n_dec = xd.shape[0]
    n_chunk = seq // SCAN_T
    rows = nb * PITCH
    main = lambda i: jnp.maximum(i - W1_STEPS, 0)
    return pl.pallas_call(
        _ffn1_mixer_kernel,
        grid=(W1_STEPS + n_chunk + 1,),
        in_specs=[
            pl.BlockSpec((nb, SCAN_T, D_MODEL), lambda i: (0, jnp.minimum(main(i), n_chunk - 1), 0)),
            _whole((n_dec, D_MODEL)),
            _whole((1, D_MODEL)),
            _weight_chunk((D_MODEL // W1_STEPS, D_FF), W1_STEPS),
            _weight_chunk((D_MODEL // W1_STEPS, D_FF), W1_STEPS),
            _weight_chunk((D_FF // W1_STEPS, D_MODEL), W1_STEPS),
            _whole((1, D_MODEL)),
            _weight_chunk((D_MODEL // W1_STEPS, IN_PROJ_WIDTH), W1_STEPS),
            _whole((N_TILE, SUBLANES, LANES)),
            _whole((N_TILE, SUBLANES, LANES)),
            _whole((N_BLK, LANES, 2 * BLK_STATE)),
            _whole((N_BLK, BLK_STATE, LANES)),
            _whole((N_BLK, BLK_STATE, LANES)),
            _whole((1, SSM_WIDTH)),
            _whole((SSM_WIDTH, SSM_WIDTH)),
            _whole((1, SSM_WIDTH)),
            _whole((CONV_K, CONV_WIDTH)),
        ],
        out_specs=[
            pl.BlockSpec((nb, SCAN_T, D_MODEL), lambda i: (0, main(i), 0)),
            pl.BlockSpec((nb, SCAN_T, D_MODEL), lambda i: (0, jnp.clip(main(i) - 1, 0, n_chunk - 1), 0)),
            pl.BlockSpec((nb, N_STATE), lambda i: (0, 0)),
            pl.BlockSpec((nb, N_STATE), lambda i: (0, 0)),
            pl.BlockSpec((nb, CONV_K - 1, CONV_WIDTH), lambda i: (0, 0, 0)),
            pl.BlockSpec((n_dec, SSM_WIDTH), lambda i: (0, 0)),
            pl.BlockSpec((n_dec, CONV_WIDTH), lambda i: (0, 0)),
            pl.BlockSpec((n_dec, CONV_WIDTH), lambda i: (0, 0)),
        ],
        out_shape=[
            jax.ShapeDtypeStruct((nb, seq + SCAN_T, D_MODEL), jnp.float32),
            jax.ShapeDtypeStruct((nb, seq, D_MODEL), jnp.bfloat16),
            jax.ShapeDtypeStruct((nb, N_STATE), jnp.float32),
            jax.ShapeDtypeStruct((nb, N_STATE), jnp.float32),
            jax.ShapeDtypeStruct((nb, CONV_K - 1, CONV_WIDTH), jnp.float32),
            jax.ShapeDtypeStruct((n_dec, SSM_WIDTH), jnp.float32),
            jax.ShapeDtypeStruct((n_dec, CONV_WIDTH), jnp.float32),
            jax.ShapeDtypeStruct((n_dec, CONV_WIDTH), jnp.float32),
        ],
        scratch_shapes=[
            *[pltpu.VMEM((SLABS_PER_BLK, rows, LANES), jnp.float32) for _ in range(N_BLK)],
            pltpu.VMEM((rows, SSM_WIDTH), jnp.float32),
            pltpu.VMEM((rows, SSM_WIDTH), jnp.float32),
            pltpu.VMEM((nb, SCAN_T + SUBLANES, CONV_WIDTH), jnp.float32),
            pltpu.VMEM((nb, SCAN_T, CONV_WIDTH), jnp.float32),
            pltpu.VMEM((2 * N_TILE, SUBLANES, LANES), jnp.float32),
            pltpu.VMEM((D_MODEL, D_FF), jnp.bfloat16),
            pltpu.VMEM((D_MODEL, D_FF), jnp.bfloat16),
            pltpu.VMEM((D_FF, D_MODEL), jnp.bfloat16),
            pltpu.VMEM((D_MODEL, IN_PROJ_WIDTH), jnp.bfloat16),
            pltpu.VMEM((nb * SCAN_T, D_FF), jnp.bfloat16),
        ],
        compiler_params=pltpu.CompilerParams(
            dimension_semantics=("arbitrary",), vmem_limit_bytes=VMEM_LIMIT),
        name="ffn1_mixer",
    )(x, xd, n1, wg, wu, wd, nm, win, a_re, a_im, bb, cre, cim, d, wglu, bglu, cw)


def _ffn2_rows(h, mix, weights, a_ref):
    wout_ref, n2_ref, wg_ref, wu_ref, wd_ref, nf_ref = weights
    h = h + jnp.dot(mix, wout_ref[...], preferred_element_type=jnp.float32)
    hn = _rms(h, n2_ref[...]).astype(jnp.bfloat16)
    h = h + 0.5 * _swiglu_staged(hn, wg_ref, wu_ref, wd_ref, a_ref)
    return _rms(h, nf_ref[...])


def _ffn2_kernel(h_ref, mix_ref, hd_ref, mixd_ref, wout_c, wg_c, wu_c, wd_c, n2_ref, nf_ref,
                 y_ref, yd_ref, wout_ref, wg_ref, wu_ref, wd_ref, a0_ref, a1_ref):
    step = pl.program_id(0)
    last = pl.num_programs(0) - 1
    weights = (wout_ref, n2_ref, wg_ref, wu_ref, wd_ref, nf_ref)

    @pl.when(step < W2_STEPS)
    def _load_weights():
        _cast_weight_chunks(step, ((wout_c, wout_ref), (wg_c, wg_ref), (wu_c, wu_ref), (wd_c, wd_ref)))

    @pl.when((step >= W2_STEPS) & (step < last))
    def _prompt_tile():
        assert h_ref.shape[0] == 2 * FFN_SUB
        ra, rb = slice(0, FFN_SUB), slice(FFN_SUB, 2 * FFN_SUB)
        h_a = h_ref[ra, :] + jnp.dot(mix_ref[ra, :], wout_ref[...], preferred_element_type=jnp.float32)
        h_b = h_ref[rb, :] + jnp.dot(mix_ref[rb, :], wout_ref[...], preferred_element_type=jnp.float32)
        hn_a = _rms(h_a, n2_ref[...]).astype(jnp.bfloat16)
        _swiglu_stage(hn_a, wg_ref, wu_ref, a0_ref, 0, FFN_TF)
        hn_b = _rms(h_b, n2_ref[...]).astype(jnp.bfloat16)
        _swiglu_stage(hn_a, wg_ref, wu_ref, a0_ref, FFN_TF, D_FF)
        f_a = jnp.dot(a0_ref[...], wd_ref[...], preferred_element_type=jnp.float32)
        _swiglu_stage(hn_b, wg_ref, wu_ref, a1_ref, 0, FFN_TF)
        y_ref[ra, :] = _rms(h_a + 0.5 * f_a, nf_ref[...])
        _swiglu_stage(hn_b, wg_ref, wu_ref, a1_ref, FFN_TF, D_FF)
        f_b = jnp.dot(a1_ref[...], wd_ref[...], preferred_element_type=jnp.float32)
        y_ref[rb, :] = _rms(h_b + 0.5 * f_b, nf_ref[...])

    @pl.when(step == last)
    def _decode_rows():
        yd_ref[...] = _ffn2_rows(hd_ref[...], mixd_ref[...], weights, a0_ref)


def _ffn2(h_all, mix, hd, mixd, wout, n2, wg, wu, wd, nf):
    nb, seq, _ = mix.shape
    seq_all = h_all.shape[1]
    per_b = seq // FFN_TM
    n_tiles = nb * per_b
    h_all = h_all.reshape(nb * seq_all, D_MODEL)
    mix = mix.reshape(nb * seq, D_MODEL)
    idx = lambda i: jnp.clip(i - W2_STEPS, 0, n_tiles - 1)
    tile = lambda i: (idx(i), 0)
    assert seq_all % SUBLANES == 0
    tile_all = lambda i: (pl.multiple_of((idx(i) // per_b) * seq_all + (idx(i) % per_b) * FFN_TM, SUBLANES), 0)
    yp, yd = pl.pallas_call(
        _ffn2_kernel,
        grid=(W2_STEPS + n_tiles + 1,),
        in_specs=[
            pl.BlockSpec((pl.Element(FFN_TM), pl.Element(D_MODEL)), tile_all),
            pl.BlockSpec((FFN_TM, D_MODEL), tile),
            _whole(hd.shape),
            _whole(mixd.shape),
            _weight_chunk((D_MODEL // W2_STEPS, D_MODEL), W2_STEPS),
            _weight_chunk((D_MODEL // W2_STEPS, D_FF), W2_STEPS),
            _weight_chunk((D_MODEL // W2_STEPS, D_FF), W2_STEPS),
            _weight_chunk((D_FF // W2_STEPS, D_MODEL), W2_STEPS),
            _whole((1, D_MODEL)),
            _whole((1, D_MODEL)),
        ],
        out_specs=[
            pl.BlockSpec((FFN_TM, D_MODEL), tile),
            pl.BlockSpec(hd.shape, lambda i: (0, 0)),
        ],
        out_shape=[
            jax.ShapeDtypeStruct((nb * seq, D_MODEL), jnp.float32),
            jax.ShapeDtypeStruct(hd.shape, jnp.float32),
        ],
        scratch_shapes=[
            pltpu.VMEM((D_MODEL, D_MODEL), jnp.bfloat16),
            pltpu.VMEM((D_MODEL, D_FF), jnp.bfloat16),
            pltpu.VMEM((D_MODEL, D_FF), jnp.bfloat16),
            pltpu.VMEM((D_FF, D_MODEL), jnp.bfloat16),
            pltpu.VMEM((FFN_SUB, D_FF), jnp.bfloat16),
            pltpu.VMEM((FFN_SUB, D_FF), jnp.bfloat16),
        ],
        compiler_params=pltpu.CompilerParams(
            dimension_semantics=("arbitrary",), vmem_limit_bytes=VMEM_LIMIT),
        name="outproj_ffn2",
    )(h_all, mix, hd, mixd, wout, wg, wu, wd, n2, nf)
    return yp.reshape(nb, seq, D_MODEL), yd


def _mixer_step_kernel(u_ref, v_ref, g_ref, hre_ref, him_ref, buf_ref, are_ref, aim_ref, bb_ref, cre_ref,
                       cim_ref, d_ref, wglu_ref, bglu_ref, cw_ref, mix_ref, sre_ref, sim_ref, nconv_ref):
    u = u_ref[...]
    u_bf = u.astype(jnp.bfloat16)
    ys = []
    for k in range(N_BLK):
        bu = jnp.dot(u_bf[:, k * LANES:(k + 1) * LANES], bb_ref[k], preferred_element_type=jnp.float32)
        st = slice(k * BLK_STATE, (k + 1) * BLK_STATE)
        a_re = are_ref[k:k + 1, :]
        a_im = aim_ref[k:k + 1, :]
        h_re = hre_ref[:, st]
        h_im = him_ref[:, st]
        x_re = (a_re * h_re + bu[:, :BLK_STATE]) - a_im * h_im
        x_im = (a_re * h_im + bu[:, BLK_STATE:]) + a_im * h_re
        sre_ref[:, st] = x_re
        sim_ref[:, st] = x_im
        ys.append(_bdot(x_re, cre_ref[k]) - _bdot(x_im, cim_ref[k]))
    y = jnp.concatenate(ys, axis=1)
    mix_ref[:, 0:SSM_WIDTH] = _glu_tail(y, u, d_ref, wglu_ref, bglu_ref).astype(mix_ref.dtype)

    v = v_ref[...]
    z = cw_ref[0:1, :] * buf_ref[0] + cw_ref[1:2, :] * buf_ref[1] + cw_ref[2:3, :] * v
    mix_ref[:, SSM_WIDTH:] = (g_ref[...] * z).astype(mix_ref.dtype)
    nconv_ref[0] = buf_ref[1]
    nconv_ref[1] = v


def _mixer_step(u, v, g, h_re, h_im, buf, a_re, a_im, bb, cre, cim, d, wglu, bglu, cw):
    nb = u.shape[0]
    return pl.pallas_call(
        _mixer_step_kernel,
        out_shape=[
            jax.ShapeDtypeStruct((nb, D_MODEL), jnp.bfloat16),
            jax.ShapeDtypeStruct((nb, N_STATE), jnp.float32),
            jax.ShapeDtypeStruct((nb, N_STATE), jnp.float32),
            jax.ShapeDtypeStruct((CONV_K - 1, nb, CONV_WIDTH), jnp.float32),
        ],
        compiler_params=pltpu.CompilerParams(vmem_limit_bytes=VMEM_LIMIT),
        name="mixer_step",
    )(u, v, g, h_re, h_im, buf, a_re, a_im, bb, cre, cim, d, wglu, bglu, cw)


def _ssm_layout(lam_re, lam_im, log_dt, b_re, b_im, c_re, c_im):
    gpb = N_SSM_GROUPS // N_BLK
    eye = jnp.eye(gpb, dtype=jnp.float32)
    lr = lam_re.reshape(N_BLK, BLK_STATE)
    li = lam_im.reshape(N_BLK, BLK_STATE)
    ldt = jnp.broadcast_to(log_dt[:, None], (N_SSM_GROUPS, SSM_STATE)).reshape(N_BLK, BLK_STATE)

    def b_blocks(b):
        b = b.reshape(N_BLK, gpb, SSM_STATE, SSM_GROUP).transpose(0, 1, 3, 2)
        return jnp.einsum("kgcp,gh->kgchp", b, eye).reshape(N_BLK, LANES, BLK_STATE)

    def c_blocks(c):
        c = c.reshape(N_BLK, gpb, SSM_GROUP, SSM_STATE).transpose(0, 1, 3, 2)
        return jnp.einsum("kgpc,gh->kgphc", c, eye).reshape(N_BLK, BLK_STATE, LANES).astype(jnp.bfloat16)

    return lr, li, ldt, b_blocks(b_re), b_blocks(b_im), c_blocks(c_re), c_blocks(c_im)


def kernel(x_prompt, x_sample, state_ssm_re, state_ssm_im, state_conv, ffn1_norm, ffn1_w_gate, ffn1_w_up, ffn1_w_down, mix_norm, w_in, ssm_lambda_re, ssm_lambda_im, ssm_log_dt, ssm_b_re, ssm_b_im, ssm_c_re, ssm_c_im, ssm_d, ssm_w_glu, ssm_b_glu, conv_w, w_out, ffn2_norm, ffn2_w_gate, ffn2_w_up, ffn2_w_down, final_norm):
    assert ffn1_norm.shape[0] == 1, "single-layer trunk"
    n_prompt, seq, _ = x_prompt.shape
    n_dec, dec_seq, _ = x_sample.shape
    assert dec_seq == 1 and n_prompt == SUBLANES and seq % FFN_TM == 0 and seq % SCAN_T == 0
    assert n_dec % SCAN_T == 0 and n_dec <= n_prompt * SCAN_T and FFN_TM % FFN_SUB == 0

    n1 = ffn1_norm[0][None, :]
    nm = mix_norm[0][None, :]
    n2 = ffn2_norm[0][None, :]
    nf = final_norm[None, :]
    wglu = ssm_w_glu[0].astype(jnp.bfloat16)
    lr, li, ldt, bre, bim, cre, cim = _ssm_layout(
        ssm_lambda_re[0], ssm_lambda_im[0], ssm_log_dt[0], ssm_b_re[0], ssm_b_im[0], ssm_c_re[0], ssm_c_im[0])
    a_re, a_im, bb = _discretize(lr, li, ldt, bre, bim)
    a_re_t = jnp.broadcast_to(a_re.reshape(N_TILE, 1, LANES), (N_TILE, SUBLANES, LANES))
    a_im_t = jnp.broadcast_to(a_im.reshape(N_TILE, 1, LANES), (N_TILE, SUBLANES, LANES))
    d = ssm_d[0][None, :]
    bglu = ssm_b_glu[0][None, :]
    cw = conv_w[0]

    h_all, mixp, p_re, p_im, p_conv, ud, vd, gd = _ffn1_mixer(
        x_prompt, x_sample.reshape(n_dec, D_MODEL), n1, ffn1_w_gate[0], ffn1_w_up[0], ffn1_w_down[0], nm,
        w_in[0], a_re_t, a_im_t, bb, cre, cim, d, wglu, bglu, cw)

    hd = h_all[:n_dec // SCAN_T, seq:seq + SCAN_T, :].reshape(n_dec, D_MODEL)
    s_re0 = state_ssm_re[0].reshape(n_dec, N_STATE)
    s_im0 = state_ssm_im[0].reshape(n_dec, N_STATE)
    buf = state_conv[0].transpose(1, 0, 2)
    mixd, s_re, s_im, s_conv = _mixer_step(ud, vd, gd, s_re0, s_im0, buf, a_re, a_im, bb, cre, cim,
                                           d, wglu, bglu, cw)

    yp, yd = _ffn2(h_all, mixp, hd, mixd, w_out[0], n2, ffn2_w_gate[0], ffn2_w_up[0], ffn2_w_down[0], nf)

    gs = (N_SSM_GROUPS, SSM_STATE)
    return (yp,
            yd.reshape(n_dec, 1, D_MODEL),
            p_re.reshape(1, n_prompt, *gs),
            p_im.reshape(1, n_prompt, *gs),
            p_conv[None],
            s_re.reshape(1, n_dec, *gs),
            s_im.reshape(1, n_dec, *gs),
            s_conv.transpose(1, 0, 2)[None])
```

```python
import jax
import jax.numpy as jnp
from jax import lax
from jax.experimental import pallas as pl
from jax.experimental.pallas import tpu as pltpu

D_MODEL = 1024
D_FF = 2816
SSM_WIDTH = 512
CONV_WIDTH = 512
SSM_GROUP = 16
N_SSM_GROUPS = 32
SSM_STATE = 64
CONV_K = 3
IN_PROJ_WIDTH = SSM_WIDTH + 3 * CONV_WIDTH
EPS = 1e-6

LANES = 128
SUBLANES = 8
N_STATE = N_SSM_GROUPS * SSM_STATE
N_BLK = SSM_WIDTH // LANES
BLK_STATE = N_STATE // N_BLK
SLABS_PER_BLK = 2 * BLK_STATE // LANES
HALF = SLABS_PER_BLK // 2
N_TILE = N_BLK * HALF

FFN_TM = 1024
FFN_SUB = 512
W1_STEPS = 16
W2_STEPS = 8
FFN_TF = 256
SCAN_T = 64
PITCH = SCAN_T + 4
VMEM_LIMIT = 62 * 1024 * 1024


def _rms(x, g):
    r = lax.rsqrt(jnp.mean(x * x, axis=-1, keepdims=True) + EPS)
    return x * r * g


def _bdot(a, b):
    return jnp.dot(a.astype(jnp.bfloat16), b.astype(jnp.bfloat16), preferred_element_type=jnp.float32)


def _swiglu_chunk(xn, wg_ref, wu_ref, wd_ref, j):
    g = jnp.dot(xn, wg_ref[:, j:j + FFN_TF], preferred_element_type=jnp.float32)
    u = jnp.dot(xn, wu_ref[:, j:j + FFN_TF], preferred_element_type=jnp.float32)
    a = (g * jax.nn.sigmoid(g)) * u
    return jnp.dot(a.astype(jnp.bfloat16), wd_ref[j:j + FFN_TF, :], preferred_element_type=jnp.float32)


def _swiglu(xn, wg_ref, wu_ref, wd_ref):
    acc = jnp.zeros((xn.shape[0], D_MODEL), jnp.float32)
    for j in range(0, D_FF, FFN_TF):
        acc = acc + _swiglu_chunk(xn, wg_ref, wu_ref, wd_ref, j)
    return acc


def _swiglu_staged(xn, wg_ref, wu_ref, wd_ref, a_ref):
    rows = xn.shape[0]
    _swiglu_stage(xn, wg_ref, wu_ref, a_ref, 0, D_FF)
    return jnp.dot(a_ref[0:rows, :], wd_ref[...], preferred_element_type=jnp.float32)


def _swiglu_stage(xn, wg_ref, wu_ref, a_ref, lo, hi):
    rows = xn.shape[0]
    for j in range(lo, hi, FFN_TF):
        g = jnp.dot(xn, wg_ref[:, j:j + FFN_TF], preferred_element_type=jnp.float32)
        u = jnp.dot(xn, wu_ref[:, j:j + FFN_TF], preferred_element_type=jnp.float32)
        a_ref[0:rows, j:j + FFN_TF] = ((g * jax.nn.sigmoid(g)) * u).astype(a_ref.dtype)


def _cast_weight_chunks(step, pairs):
    for src, dst in pairs:
        n = src.shape[0]
        dst[pl.ds(pl.multiple_of(step * n, n), n), :] = src[...].astype(dst.dtype)


def _whole(shape):
    return pl.BlockSpec(shape, lambda i: (0,) * len(shape), pipeline_mode=pl.Buffered(1))


def _weight_chunk(shape, n_steps):
    return pl.BlockSpec(shape, lambda i: (jnp.minimum(i, n_steps - 1), 0))


def _glu_tail(y, u, d_ref, wglu_ref, bglu_ref):
    y = y + d_ref[...] * u
    z = jax.nn.gelu(y)
    gate = jax.nn.sigmoid(_bdot(z, wglu_ref[...]) + bglu_ref[...])
    return z * gate


def _discretize_kernel(lr_ref, li_ref, ldt_ref, bre_ref, bim_ref, are_ref, aim_ref, bb_ref):
    lr, li = lr_ref[...], li_ref[...]
    dt = jnp.exp(ldt_ref[...])
    mag = jnp.exp(lr * dt)
    ab_re = mag * jnp.cos(li * dt)
    ab_im = mag * jnp.sin(li * dt)
    den = lr * lr + li * li
    nr = ab_re - 1.0
    ni = ab_im
    coef_re = (nr * lr + ni * li) / den
    coef_im = (ni * lr - nr * li) / den
    are_ref[...] = ab_re
    aim_ref[...] = ab_im
    for k in range(N_BLK):
        cr = coef_re[k:k + 1, :]
        ci = coef_im[k:k + 1, :]
        bre, bim = bre_ref[k], bim_ref[k]
        bb_ref[k, :, 0:BLK_STATE] = (cr * bre - ci * bim).astype(bb_ref.dtype)
        bb_ref[k, :, BLK_STATE:] = (cr * bim + ci * bre).astype(bb_ref.dtype)


def _discretize(lr, li, ldt, bre, bim):
    return pl.pallas_call(
        _discretize_kernel,
        out_shape=[
            jax.ShapeDtypeStruct((N_BLK, BLK_STATE), jnp.float32),
            jax.ShapeDtypeStruct((N_BLK, BLK_STATE), jnp.float32),
            jax.ShapeDtypeStruct((N_BLK, LANES, 2 * BLK_STATE), jnp.bfloat16),
        ],
        name="ssm_discretize",
    )(lr, li, ldt, bre, bim)


def _ffn1_mixer_kernel(x_ref, xd_ref, n1_ref, wg_c, wu_c, wd_c, nm_ref, win_c,
                       are_ref, aim_ref, bb_ref, cre_ref, cim_ref, d_ref, wglu_ref, bglu_ref, cw_ref,
                       h_ref, mix_ref, sre_ref, sim_ref, nconv_ref, ud_ref, vd_ref, gd_ref,
                       xs0_ref, xs1_ref, xs2_ref, xs3_ref, u_ref, y_ref, v_ref, g_ref, state_ref,
                       wg_ref, wu_ref, wd_ref, win_ref, a_ref):
    xs_refs = (xs0_ref, xs1_ref, xs2_ref, xs3_ref)
    step = pl.program_id(0)
    last = pl.num_programs(0) - 1
    nb, t_len, _ = x_ref.shape
    n_dec = xd_ref.shape[0]
    tiles = [(k, j) for k in range(N_BLK) for j in range(HALF)]

    @pl.when(step < W1_STEPS)
    def _load_weights():
        _cast_weight_chunks(step, ((wg_c, wg_ref), (wu_c, wu_ref), (wd_c, wd_ref), (win_c, win_ref)))

    @pl.when(step == 0)
    def _init():
        state_ref[...] = jnp.zeros_like(state_ref)
        u_ref[...] = jnp.zeros_like(u_ref)
        v_ref[...] = jnp.zeros_like(v_ref)
        g_ref[...] = jnp.zeros_like(g_ref)

    def run(decode):
        carry = {}

        def bbar_u(ks):
            for k in ks:
                u_bf = u_ref[:, k * LANES:(k + 1) * LANES].astype(jnp.bfloat16)
                bu = jnp.dot(u_bf, bb_ref[k], preferred_element_type=jnp.float32)
                for j in range(SLABS_PER_BLK):
                    xs_refs[k][j] = bu[:, j * LANES:(j + 1) * LANES]

        def scan(t0, t1):
            if t0 == 0:
                carry["re"] = [state_ref[k * SLABS_PER_BLK + j] for k, j in tiles]
                carry["im"] = [state_ref[k * SLABS_PER_BLK + HALF + j] for k, j in tiles]
            xr, xi = carry["re"], carry["im"]
            for t in range(t0, t1):
                rows = pl.ds(t, SUBLANES, stride=PITCH)
                for n, (k, j) in enumerate(tiles):
                    a_re = are_ref[k * HALF + j]
                    a_im = aim_ref[k * HALF + j]
                    nr = (a_re * xr[n] + xs_refs[k][j, rows, :]) - a_im * xi[n]
                    ni = (a_re * xi[n] + xs_refs[k][HALF + j, rows, :]) + a_im * xr[n]
                    xs_refs[k][j, rows, :] = nr
                    xs_refs[k][HALF + j, rows, :] = ni
                    xr[n], xi[n] = nr, ni
            if t1 == t_len:
                for n, (k, j) in enumerate(tiles):
                    state_ref[k * SLABS_PER_BLK + j] = xr[n]
                    state_ref[k * SLABS_PER_BLK + HALF + j] = xi[n]

        def c_proj(ks):
            for k in ks:
                x_re = jnp.concatenate([xs_refs[k][j] for j in range(HALF)], axis=1)
                x_im = jnp.concatenate([xs_refs[k][HALF + j] for j in range(HALF)], axis=1)
                y_ref[:, k * LANES:(k + 1) * LANES] = _bdot(x_re, cre_ref[k]) - _bdot(x_im, cim_ref[k])

        def glu():
            y_ref[...] = _glu_tail(y_ref[...], u_ref[...], d_ref, wglu_ref, bglu_ref)

        def emit_ssm():
            for b in range(nb):
                mix_ref[b, :, 0:SSM_WIDTH] = y_ref[pl.ds(b * PITCH, t_len), :].astype(mix_ref.dtype)

        def conv():
            v = v_ref[:, SUBLANES:SUBLANES + t_len, :]
            z = cw_ref[0:1, :] * v_ref[:, SUBLANES - 2:SUBLANES - 2 + t_len, :]
            z = z + cw_ref[1:2, :] * v_ref[:, SUBLANES - 1:SUBLANES - 1 + t_len, :]
            z = z + cw_ref[2:3, :] * v
            mix_ref[:, :, SSM_WIDTH:] = (g_ref[...] * z).astype(mix_ref.dtype)
            v_ref[:, 0:SUBLANES, :] = v_ref[:, t_len:t_len + SUBLANES, :]

        q = t_len // 4
        pieces = [
            lambda: (conv(), bbar_u((0,))), lambda: bbar_u((1,)), lambda: bbar_u((2,)), lambda: bbar_u((3,)),
            lambda: scan(0, q), lambda: scan(q, 2 * q), lambda: scan(2 * q, 3 * q), lambda: scan(3 * q, t_len),
            lambda: c_proj((0, 1)), lambda: c_proj((2, 3)),
            lambda: (glu(), emit_ssm()),
        ]
        assert len(pieces) <= D_FF // FFN_TF

        rows = n_dec if decode else nb * t_len
        x = xd_ref[...] if decode else x_ref[...].reshape(rows, D_MODEL)
        xn = _rms(x, n1_ref[...]).astype(jnp.bfloat16)
        for c, j in enumerate(range(0, D_FF, FFN_TF)):
            _swiglu_stage(xn, wg_ref, wu_ref, a_ref, j, j + FFN_TF)
            if c < len(pieces):
                pieces[c]()
        h = x + 0.5 * jnp.dot(a_ref[0:rows, :], wd_ref[...], preferred_element_type=jnp.float32)
        hn = _rms(h, nm_ref[...]).astype(jnp.bfloat16)
        p = jnp.dot(hn, win_ref[...], preferred_element_type=jnp.float32)
        u_new = p[:, 0:SSM_WIDTH]
        v_new = p[:, SSM_WIDTH + 2 * CONV_WIDTH:] * p[:, SSM_WIDTH:SSM_WIDTH + CONV_WIDTH]
        g_new = p[:, SSM_WIDTH + CONV_WIDTH:SSM_WIDTH + 2 * CONV_WIDTH]
        if decode:
            db = n_dec // t_len
            h_ref[0:db] = h.reshape(db, t_len, D_MODEL)
            h_ref[db:] = jnp.zeros((nb - db, t_len, D_MODEL), jnp.float32)
            ud_ref[...] = u_new
            vd_ref[...] = v_new
            gd_ref[...] = g_new
            for n, (k, j) in enumerate(tiles):
                lanes = slice(k * BLK_STATE + j * LANES, k * BLK_STATE + (j + 1) * LANES)
                sre_ref[:, lanes] = state_ref[k * SLABS_PER_BLK + j]
                sim_ref[:, lanes] = state_ref[k * SLABS_PER_BLK + HALF + j]
            nconv_ref[...] = v_ref[:, SUBLANES - 2:SUBLANES, :]
        else:
            h_ref[...] = h.reshape(nb, t_len, D_MODEL)
            v_ref[:, SUBLANES:SUBLANES + t_len, :] = v_new.reshape(nb, t_len, CONV_WIDTH)
            g_ref[...] = g_new.reshape(nb, t_len, CONV_WIDTH)
            for b in range(nb):
                u_ref[pl.ds(b * PITCH, t_len), :] = u_new[b * t_len:(b + 1) * t_len, :]

    @pl.when((step >= W1_STEPS) & (step < last))
    def _prompt_chunk():
        run(decode=False)

    @pl.when(step == last)
    def _decode_rows():
        run(decode=True)


def _ffn1_mixer(x, xd, n1, wg, wu, wd, nm, win, a_re, a_im, bb, cre, cim, d, wglu, bglu, cw):
    nb, seq, _ = x.shape
    n_dec = xd.shape[0]
    n_chunk = seq // SCAN_T
    rows = nb * PITCH
    main = lambda i: jnp.maximum(i - W1_STEPS, 0)
    return pl.pallas_call(
        _ffn1_mixer_kernel,
        grid=(W1_STEPS + n_chunk + 1,),
        in_specs=[
            pl.BlockSpec((nb, SCAN_T, D_MODEL), lambda i: (0, jnp.minimum(main(i), n_chunk - 1), 0)),
            _whole((n_dec, D_MODEL)),
            _whole((1, D_MODEL)),
            _weight_chunk((D_MODEL // W1_STEPS, D_FF), W1_STEPS),
            _weight_chunk((D_MODEL // W1_STEPS, D_FF), W1_STEPS),
            _weight_chunk((D_FF // W1_STEPS, D_MODEL), W1_STEPS),
            _whole((1, D_MODEL)),
            _weight_chunk((D_MODEL // W1_STEPS, IN_PROJ_WIDTH), W1_STEPS),
            _whole((N_TILE, SUBLANES, LANES)),
            _whole((N_TILE, SUBLANES, LANES)),
            _whole((N_BLK, LANES, 2 * BLK_STATE)),
            _whole((N_BLK, BLK_STATE, LANES)),
            _whole((N_BLK, BLK_STATE, LANES)),
            _whole((1, SSM_WIDTH)),
            _whole((SSM_WIDTH, SSM_WIDTH)),
            _whole((1, SSM_WIDTH)),
            _whole((CONV_K, CONV_WIDTH)),
        ],
        out_specs=[
            pl.BlockSpec((nb, SCAN_T, D_MODEL), lambda i: (0, main(i), 0)),
            pl.BlockSpec((nb, SCAN_T, D_MODEL), lambda i: (0, jnp.clip(main(i) - 1, 0, n_chunk - 1), 0)),
            pl.BlockSpec((nb, N_STATE), lambda i: (0, 0)),
            pl.BlockSpec((nb, N_STATE), lambda i: (0, 0)),
            pl.BlockSpec((nb, CONV_K - 1, CONV_WIDTH), lambda i: (0, 0, 0)),
            pl.BlockSpec((n_dec, SSM_WIDTH), lambda i: (0, 0)),
            pl.BlockSpec((n_dec, CONV_WIDTH), lambda i: (0, 0)),
            pl.BlockSpec((n_dec, CONV_WIDTH), lambda i: (0, 0)),
        ],
        out_shape=[
            jax.ShapeDtypeStruct((nb, seq + SCAN_T, D_MODEL), jnp.float32),
            jax.ShapeDtypeStruct((nb, seq, D_MODEL), jnp.bfloat16),
            jax.ShapeDtypeStruct((nb, N_STATE), jnp.float32),
            jax.ShapeDtypeStruct((nb, N_STATE), jnp.float32),
            jax.ShapeDtypeStruct((nb, CONV_K - 1, CONV_WIDTH), jnp.float32),
            jax.ShapeDtypeStruct((n_dec, SSM_WIDTH), jnp.float32),
            jax.ShapeDtypeStruct((n_dec, CONV_WIDTH), jnp.float32),
            jax.ShapeDtypeStruct((n_dec, CONV_WIDTH), jnp.float32),
        ],
        scratch_shapes=[
            *[pltpu.VMEM((SLABS_PER_BLK, rows, LANES), jnp.float32) for _ in range(N_BLK)],
            pltpu.VMEM((rows, SSM_WIDTH), jnp.float32),
            pltpu.VMEM((rows, SSM_WIDTH), jnp.float32),
            pltpu.VMEM((nb, SCAN_T + SUBLANES, CONV_WIDTH), jnp.float32),
            pltpu.VMEM((nb, SCAN_T, CONV_WIDTH), jnp.float32),
            pltpu.VMEM((2 * N_TILE, SUBLANES, LANES), jnp.float32),
            pltpu.VMEM((D_MODEL, D_FF), jnp.bfloat16),
            pltpu.VMEM((D_MODEL, D_FF), jnp.bfloat16),
            pltpu.VMEM((D_FF, D_MODEL), jnp.bfloat16),
            pltpu.VMEM((D_MODEL, IN_PROJ_WIDTH), jnp.bfloat16),
            pltpu.VMEM((nb * SCAN_T, D_FF), jnp.bfloat16),
        ],
        compiler_params=pltpu.CompilerParams(
            dimension_semantics=("arbitrary",), vmem_limit_bytes=VMEM_LIMIT),
        name="ffn1_mixer",
    )(x, xd, n1, wg, wu, wd, nm, win, a_re, a_im, bb, cre, cim, d, wglu, bglu, cw)


def _ffn2_rows(h, mix, weights, a_ref):
    wout_ref, n2_ref, wg_ref, wu_ref, wd_ref, nf_ref = weights
    h = h + jnp.dot(mix, wout_ref[...], preferred_element_type=jnp.float32)
    hn = _rms(h, n2_ref[...]).astype(jnp.bfloat16)
    h = h + 0.5 * _swiglu_staged(hn, wg_ref, wu_ref, wd_ref, a_ref)
    return _rms(h, nf_ref[...])


def _ffn2_kernel(h_ref, mix_ref, hd_ref, mixd_ref, wout_c, wg_c, wu_c, wd_c, n2_ref, nf_ref,
                 y_ref, yd_ref, wout_ref, wg_ref, wu_ref, wd_ref, a0_ref, a1_ref):
    step = pl.program_id(0)
    last = pl.num_programs(0) - 1
    weights = (wout_ref, n2_ref, wg_ref, wu_ref, wd_ref, nf_ref)

    @pl.when(step < W2_STEPS)
    def _load_weights():
        _cast_weight_chunks(step, ((wout_c, wout_ref), (wg_c, wg_ref), (wu_c, wu_ref), (wd_c, wd_ref)))

    @pl.when((step >= W2_STEPS) & (step < last))
    def _prompt_tile():
        assert h_ref.shape[0] == 2 * FFN_SUB
        ra, rb = slice(0, FFN_SUB), slice(FFN_SUB, 2 * FFN_SUB)
        h_a = h_ref[ra, :] + jnp.dot(mix_ref[ra, :], wout_ref[...], preferred_element_type=jnp.float32)
        h_b = h_ref[rb, :] + jnp.dot(mix_ref[rb, :], wout_ref[...], preferred_element_type=jnp.float32)
        hn_a = _rms(h_a, n2_ref[...]).astype(jnp.bfloat16)
        _swiglu_stage(hn_a, wg_ref, wu_ref, a0_ref, 0, FFN_TF)
        hn_b = _rms(h_b, n2_ref[...]).astype(jnp.bfloat16)
        _swiglu_stage(hn_a, wg_ref, wu_ref, a0_ref, FFN_TF, D_FF)
        f_a = jnp.dot(a0_ref[...], wd_ref[...], preferred_element_type=jnp.float32)
        _swiglu_stage(hn_b, wg_ref, wu_ref, a1_ref, 0, FFN_TF)
        y_ref[ra, :] = _rms(h_a + 0.5 * f_a, nf_ref[...])
        _swiglu_stage(hn_b, wg_ref, wu_ref, a1_ref, FFN_TF, D_FF)
        f_b = jnp.dot(a1_ref[...], wd_ref[...], preferred_element_type=jnp.float32)
        y_ref[rb, :] = _rms(h_b + 0.5 * f_b, nf_ref[...])

    @pl.when(step == last)
    def _decode_rows():
        yd_ref[...] = _ffn2_rows(hd_ref[...], mixd_ref[...], weights, a0_ref)


def _ffn2(h_all, mix, hd, mixd, wout, n2, wg, wu, wd, nf):
    nb, seq, _ = mix.shape
    seq_all = h_all.shape[1]
    per_b = seq // FFN_TM
    n_tiles = nb * per_b
    h_all = h_all.reshape(nb * seq_all, D_MODEL)
    mix = mix.reshape(nb * seq, D_MODEL)
    idx = lambda i: jnp.clip(i - W2_STEPS, 0, n_tiles - 1)
    tile = lambda i: (idx(i), 0)
    assert seq_all % SUBLANES == 0
    tile_all = lambda i: (pl.multiple_of((idx(i) // per_b) * seq_all + (idx(i) % per_b) * FFN_TM, SUBLANES), 0)
    yp, yd = pl.pallas_call(
        _ffn2_kernel,
        grid=(W2_STEPS + n_tiles + 1,),
        in_specs=[
            pl.BlockSpec((pl.Element(FFN_TM), pl.Element(D_MODEL)), tile_all),
            pl.BlockSpec((FFN_TM, D_MODEL), tile),
            _whole(hd.shape),
            _whole(mixd.shape),
            _weight_chunk((D_MODEL // W2_STEPS, D_MODEL), W2_STEPS),
            _weight_chunk((D_MODEL // W2_STEPS, D_FF), W2_STEPS),
            _weight_chunk((D_MODEL // W2_STEPS, D_FF), W2_STEPS),
            _weight_chunk((D_FF // W2_STEPS, D_MODEL), W2_STEPS),
            _whole((1, D_MODEL)),
            _whole((1, D_MODEL)),
        ],
        out_specs=[
            pl.BlockSpec((FFN_TM, D_MODEL), tile),
            pl.BlockSpec(hd.shape, lambda i: (0, 0)),
        ],
        out_shape=[
            jax.ShapeDtypeStruct((nb * seq, D_MODEL), jnp.float32),
            jax.ShapeDtypeStruct(hd.shape, jnp.float32),
        ],
        scratch_shapes=[
            pltpu.VMEM((D_MODEL, D_MODEL), jnp.bfloat16),
            pltpu.VMEM((D_MODEL, D_FF), jnp.bfloat16),
            pltpu.VMEM((D_MODEL, D_FF), jnp.bfloat16),
            pltpu.VMEM((D_FF, D_MODEL), jnp.bfloat16),
            pltpu.VMEM((FFN_SUB, D_FF), jnp.bfloat16),
            pltpu.VMEM((FFN_SUB, D_FF), jnp.bfloat16),
        ],
        compiler_params=pltpu.CompilerParams(
            dimension_semantics=("arbitrary",), vmem_limit_bytes=VMEM_LIMIT),
        name="outproj_ffn2",
    )(h_all, mix, hd, mixd, wout, wg, wu, wd, n2, nf)
    return yp.reshape(nb, seq, D_MODEL), yd


def _mixer_step_kernel(u_ref, v_ref, g_ref, hre_ref, him_ref, buf_ref, are_ref, aim_ref, bb_ref, cre_ref,
                       cim_ref, d_ref, wglu_ref, bglu_ref, cw_ref, mix_ref, sre_ref, sim_ref, nconv_ref):
    u = u_ref[...]
    u_bf = u.astype(jnp.bfloat16)
    ys = []
    for k in range(N_BLK):
        bu = jnp.dot(u_bf[:, k * LANES:(k + 1) * LANES], bb_ref[k], preferred_element_type=jnp.float32)
        st = slice(k * BLK_STATE, (k + 1) * BLK_STATE)
        a_re = are_ref[k:k + 1, :]
        a_im = aim_ref[k:k + 1, :]
        h_re = hre_ref[:, st]
        h_im = him_ref[:, st]
        x_re = (a_re * h_re + bu[:, :BLK_STATE]) - a_im * h_im
        x_im = (a_re * h_im + bu[:, BLK_STATE:]) + a_im * h_re
        sre_ref[:, st] = x_re
        sim_ref[:, st] = x_im
        ys.append(_bdot(x_re, cre_ref[k]) - _bdot(x_im, cim_ref[k]))
    y = jnp.concatenate(ys, axis=1)
    mix_ref[:, 0:SSM_WIDTH] = _glu_tail(y, u, d_ref, wglu_ref, bglu_ref).astype(mix_ref.dtype)

    v = v_ref[...]
    z = cw_ref[0:1, :] * buf_ref[0] + cw_ref[1:2, :] * buf_ref[1] + cw_ref[2:3, :] * v
    mix_ref[:, SSM_WIDTH:] = (g_ref[...] * z).astype(mix_ref.dtype)
    nconv_ref[0] = buf_ref[1]
    nconv_ref[1] = v


def _mixer_step(u, v, g, h_re, h_im, buf, a_re, a_im, bb, cre, cim, d, wglu, bglu, cw):
    nb = u.shape[0]
    return pl.pallas_call(
        _mixer_step_kernel,
        out_shape=[
            jax.ShapeDtypeStruct((nb, D_MODEL), jnp.bfloat16),
            jax.ShapeDtypeStruct((nb, N_STATE), jnp.float32),
            jax.ShapeDtypeStruct((nb, N_STATE), jnp.float32),
            jax.ShapeDtypeStruct((CONV_K - 1, nb, CONV_WIDTH), jnp.float32),
        ],
        compiler_params=pltpu.CompilerParams(vmem_limit_bytes=VMEM_LIMIT),
        name="mixer_step",
    )(u, v, g, h_re, h_im, buf, a_re, a_im, bb, cre, cim, d, wglu, bglu, cw)


def _ssm_layout(lam_re, lam_im, log_dt, b_re, b_im, c_re, c_im):
    gpb = N_SSM_GROUPS // N_BLK
    eye = jnp.eye(gpb, dtype=jnp.float32)
    lr = lam_re.reshape(N_BLK, BLK_STATE)
    li = lam_im.reshape(N_BLK, BLK_STATE)
    ldt = jnp.broadcast_to(log_dt[:, None], (N_SSM_GROUPS, SSM_STATE)).reshape(N_BLK, BLK_STATE)

    def b_blocks(b):
        b = b.reshape(N_BLK, gpb, SSM_STATE, SSM_GROUP).transpose(0, 1, 3, 2)
        return jnp.einsum("kgcp,gh->kgchp", b, eye).reshape(N_BLK, LANES, BLK_STATE)

    def c_blocks(c):
        c = c.reshape(N_BLK, gpb, SSM_GROUP, SSM_STATE).transpose(0, 1, 3, 2)
        return jnp.einsum("kgpc,gh->kgphc", c, eye).reshape(N_BLK, BLK_STATE, LANES).astype(jnp.bfloat16)

    return lr, li, ldt, b_blocks(b_re), b_blocks(b_im), c_blocks(c_re), c_blocks(c_im)


def kernel(x_prompt, x_sample, state_ssm_re, state_ssm_im, state_conv, ffn1_norm, ffn1_w_gate, ffn1_w_up, ffn1_w_down, mix_norm, w_in, ssm_lambda_re, ssm_lambda_im, ssm_log_dt, ssm_b_re, ssm_b_im, ssm_c_re, ssm_c_im, ssm_d, ssm_w_glu, ssm_b_glu, conv_w, w_out, ffn2_norm, ffn2_w_gate, ffn2_w_up, ffn2_w_down, final_norm):
    assert ffn1_norm.shape[0] == 1, "single-layer trunk"
    n_prompt, seq, _ = x_prompt.shape
    n_dec, dec_seq, _ = x_sample.shape
    assert dec_seq == 1 and n_prompt == SUBLANES and seq % FFN_TM == 0 and seq % SCAN_T == 0
    assert n_dec % SCAN_T == 0 and n_dec <= n_prompt * SCAN_T and FFN_TM % FFN_SUB == 0

    n1 = ffn1_norm[0][None, :]
    nm = mix_norm[0][None, :]
    n2 = ffn2_norm[0][None, :]
    nf = final_norm[None, :]
    wglu = ssm_w_glu[0].astype(jnp.bfloat16)
    lr, li, ldt, bre, bim, cre, cim = _ssm_layout(
        ssm_lambda_re[0], ssm_lambda_im[0], ssm_log_dt[0], ssm_b_re[0], ssm_b_im[0], ssm_c_re[0], ssm_c_im[0])
    a_re, a_im, bb = _discretize(lr, li, ldt, bre, bim)
    a_re_t = jnp.broadcast_to(a_re.reshape(N_TILE, 1, LANES), (N_TILE, SUBLANES, LANES))
    a_im_t = jnp.broadcast_to(a_im.reshape(N_TILE, 1, LANES), (N_TILE, SUBLANES, LANES))
    d = ssm_d[0][None, :]
    bglu = ssm_b_glu[0][None, :]
    cw = conv_w[0]

    h_all, mixp, p_re, p_im, p_conv, ud, vd, gd = _ffn1_mixer(
        x_prompt, x_sample.reshape(n_dec, D_MODEL), n1, ffn1_w_gate[0], ffn1_w_up[0], ffn1_w_down[0], nm,
        w_in[0], a_re_t, a_im_t, bb, cre, cim, d, wglu, bglu, cw)

    hd = h_all[:n_dec // SCAN_T, seq:seq + SCAN_T, :].reshape(n_dec, D_MODEL)
    s_re0 = state_ssm_re[0].reshape(n_dec, N_STATE)
    s_im0 = state_ssm_im[0].reshape(n_dec, N_STATE)
    buf = state_conv[0].transpose(1, 0, 2)
    mixd, s_re, s_im, s_conv = _mixer_step(ud, vd, gd, s_re0, s_im0, buf, a_re, a_im, bb, cre, cim,
                                           d, wglu, bglu, cw)

    yp, yd = _ffn2(h_all, mixp, hd, mixd, w_out[0], n2, ffn2_w_gate[0], ffn2_w_up[0], ffn2_w_down[0], nf)

    gs = (N_SSM_GROUPS, SSM_STATE)
    return (yp,
            yd.reshape(n_dec, 1, D_MODEL),
            p_re.reshape(1, n_prompt, *gs),
            p_im.reshape(1, n_prompt, *gs),
            p_conv[None],
            s_re.reshape(1, n_dec, *gs),
            s_im.reshape(1, n_dec, *gs),
            s_conv.transpose(1, 0, 2)[None])
```

```python
import jax
import jax.numpy as jnp
from jax import lax
from jax.experimental import pallas as pl
from jax.experimental.pallas import tpu as pltpu

D_MODEL = 1024
D_FF = 2816
SSM_WIDTH = 512
CONV_WIDTH = 512
SSM_GROUP = 16
N_SSM_GROUPS = 32
SSM_STATE = 64
CONV_K = 3
IN_PROJ_WIDTH = SSM_WIDTH + 3 * CONV_WIDTH
EPS = 1e-6

LANES = 128
SUBLANES = 8
N_STATE = N_SSM_GROUPS * SSM_STATE
N_BLK = SSM_WIDTH // LANES
BLK_STATE = N_STATE // N_BLK
SLABS_PER_BLK = 2 * BLK_STATE // LANES
HALF = SLABS_PER_BLK // 2
N_TILE = N_BLK * HALF

FFN_TM = 1024
FFN_SUB = 512
W1_STEPS = 16
W2_STEPS = 8
FFN_TF = 256
SCAN_T = 64
N_PAIR = SCAN_T // 2
PAIR_PITCH = N_PAIR + 4
VMEM_LIMIT = 62 * 1024 * 1024

_NT = (((1,), (1,)), ((), ()))


def _rms(x, g):
    r = lax.rsqrt(jnp.mean(x * x, axis=-1, keepdims=True) + EPS)
    return x * r * g


def _bdot(a, b):
    return jnp.dot(a.astype(jnp.bfloat16), b.astype(jnp.bfloat16), preferred_element_type=jnp.float32)


def _swiglu_stage(xn, wg_ref, wu_ref, a_ref, lo, hi):
    rows = xn.shape[0]
    for j in range(lo, hi, FFN_TF):
        g = jnp.dot(xn, wg_ref[:, j:j + FFN_TF], preferred_element_type=jnp.float32)
        u = jnp.dot(xn, wu_ref[:, j:j + FFN_TF], preferred_element_type=jnp.float32)
        a_ref[0:rows, j:j + FFN_TF] = ((g * jax.nn.sigmoid(g)) * u).astype(a_ref.dtype)


def _swiglu_staged(xn, wg_ref, wu_ref, wd_ref, a_ref):
    rows = xn.shape[0]
    _swiglu_stage(xn, wg_ref, wu_ref, a_ref, 0, D_FF)
    return jnp.dot(a_ref[0:rows, :], wd_ref[...], preferred_element_type=jnp.float32)


def _cast_weight_chunks(step, pairs):
    for src, dst in pairs:
        n = src.shape[0]
        dst[pl.ds(pl.multiple_of(step * n, n), n), :] = src[...].astype(dst.dtype)


def _whole(shape):
    return pl.BlockSpec(shape, lambda i: (0,) * len(shape), pipeline_mode=pl.Buffered(1))


def _weight_chunk(shape, n_steps):
    return pl.BlockSpec(shape, lambda i: (jnp.minimum(i, n_steps - 1), 0))


def _glu_tail(y, u, d_ref, wglu_ref, bglu_ref):
    y = y + d_ref[...] * u
    z = jax.nn.gelu(y)
    gate = jax.nn.sigmoid(_bdot(z, wglu_ref[...]) + bglu_ref[...])
    return z * gate


def _discretize_kernel(lr_ref, li_ref, ldt_ref, bre_ref, bim_ref, ctre_ref, ctim_ref,
                       are_ref, aim_ref, a2re_ref, a2im_ref, bb_ref, wb_ref, wct_ref, wdm_ref):
    lr, li = lr_ref[...], li_ref[...]
    dt = jnp.exp(ldt_ref[...])
    mag = jnp.exp(lr * dt)
    ab_re = mag * jnp.cos(li * dt)
    ab_im = mag * jnp.sin(li * dt)
    den = lr * lr + li * li
    nr = ab_re - 1.0
    ni = ab_im
    coef_re = (nr * lr + ni * li) / den
    coef_im = (ni * lr - nr * li) / den
    are_ref[...] = ab_re
    aim_ref[...] = ab_im
    a2_re = ab_re * ab_re - ab_im * ab_im
    a2_im = 2.0 * (ab_re * ab_im)
    bf = jnp.bfloat16
    for k in range(N_BLK):
        ar, ai = ab_re[k:k + 1, :], ab_im[k:k + 1, :]
        for j in range(HALF):
            lanes = slice(j * LANES, (j + 1) * LANES)
            a2re_ref[k * HALF + j] = jnp.broadcast_to(a2_re[k:k + 1, lanes], (SUBLANES, LANES))
            a2im_ref[k * HALF + j] = jnp.broadcast_to(a2_im[k:k + 1, lanes], (SUBLANES, LANES))
        cr, ci = coef_re[k:k + 1, :], coef_im[k:k + 1, :]
        bre, bim = bre_ref[k], bim_ref[k]
        bb_re = cr * bre - ci * bim
        bb_im = cr * bim + ci * bre
        bb_ref[k, :, 0:BLK_STATE] = bb_re.astype(bf)
        bb_ref[k, :, BLK_STATE:] = bb_im.astype(bf)
        wb_ref[k, 0:LANES, 0:BLK_STATE] = (ar * bb_re - ai * bb_im).astype(bf)
        wb_ref[k, 0:LANES, BLK_STATE:] = (ar * bb_im + ai * bb_re).astype(bf)
        wb_ref[k, LANES:, 0:BLK_STATE] = bb_re.astype(bf)
        wb_ref[k, LANES:, BLK_STATE:] = bb_im.astype(bf)
        ctre, ctim = ctre_ref[k], ctim_ref[k]
        wct_ref[k, 0:LANES, 0:BLK_STATE] = ctre.astype(bf)
        wct_ref[k, 0:LANES, BLK_STATE:] = (-ctim).astype(bf)
        wct_ref[k, LANES:, 0:BLK_STATE] = (ctre * ar - ctim * ai).astype(bf)
        wct_ref[k, LANES:, BLK_STATE:] = (-(ctre * ai + ctim * ar)).astype(bf)
        wdm = (lax.dot_general(bb_re, ctre, _NT, precision=lax.Precision.HIGHEST,
                               preferred_element_type=jnp.float32)
               - lax.dot_general(bb_im, ctim, _NT, precision=lax.Precision.HIGHEST,
                                 preferred_element_type=jnp.float32))
        wdm_ref[k] = wdm.astype(bf)


def _discretize(lr, li, ldt, bre, bim, ctre, ctim):
    f32, bf = jnp.float32, jnp.bfloat16
    return pl.pallas_call(
        _discretize_kernel,
        out_shape=[
            jax.ShapeDtypeStruct((N_BLK, BLK_STATE), f32),
            jax.ShapeDtypeStruct((N_BLK, BLK_STATE), f32),
            jax.ShapeDtypeStruct((N_TILE, SUBLANES, LANES), f32),
            jax.ShapeDtypeStruct((N_TILE, SUBLANES, LANES), f32),
            jax.ShapeDtypeStruct((N_BLK, LANES, 2 * BLK_STATE), bf),
            jax.ShapeDtypeStruct((N_BLK, 2 * LANES, 2 * BLK_STATE), bf),
            jax.ShapeDtypeStruct((N_BLK, 2 * LANES, 2 * BLK_STATE), bf),
            jax.ShapeDtypeStruct((N_BLK, LANES, LANES), bf),
        ],
        name="ssm_discretize",
    )(lr, li, ldt, bre, bim, ctre, ctim)


def _ffn1_mixer_kernel(x_ref, xd_ref, n1_ref, wg_c, wu_c, wd_c, nm_ref, win_c,
                       a2re_ref, a2im_ref, wb_ref, wct_ref, wdm_ref, d_ref, wglu_ref, bglu_ref, cw_ref,
                       h_ref, mix_ref, sre_ref, sim_ref, nconv_ref, ud_ref, vd_ref, gd_ref,
                       xs0_ref, xs1_ref, xs2_ref, xs3_ref, us_ref, ys_ref, l_ref, yo_ref, ye_ref, e_ref,
                       ecarry_ref, v_ref, g_ref, state_ref, wg_ref, wu_ref, wd_ref, win_ref, a_ref):
    xs_refs = (xs0_ref, xs1_ref, xs2_ref, xs3_ref)
    step = pl.program_id(0)
    last = pl.num_programs(0) - 1
    nb, t_len, _ = x_ref.shape
    n_dec = xd_ref.shape[0]
    prow = nb * PAIR_PITCH
    tiles = [(k, j) for k in range(N_BLK) for j in range(HALF)]

    @pl.when(step < W1_STEPS)
    def _load_weights():
        _cast_weight_chunks(step, ((wg_c, wg_ref), (wu_c, wu_ref), (wd_c, wd_ref), (win_c, win_ref)))

    @pl.when(step == 0)
    def _init():
        for ref in (state_ref, us_ref, l_ref, e_ref, ecarry_ref, v_ref, g_ref):
            ref[...] = jnp.zeros_like(ref)

    @pl.when(step >= W1_STEPS)
    def _main():
        carry = {}

        def build_lhs():
            for b in range(nb):
                for k in range(N_BLK):
                    for par in range(2):
                        col = (2 * k + par) * LANES
                        l_ref[pl.ds(b * PAIR_PITCH, N_PAIR), col:col + LANES] = (
                            us_ref[k, pl.ds(b * t_len + par, N_PAIR, stride=2), :])

        def b_proj(ks):
            for k in ks:
                lhs = l_ref[:, 2 * k * LANES:2 * (k + 1) * LANES].astype(jnp.bfloat16)
                v = jnp.dot(lhs, wb_ref[k], preferred_element_type=jnp.float32)
                for j in range(SLABS_PER_BLK):
                    xs_refs[k][j] = v[:, j * LANES:(j + 1) * LANES]

        def scan(r0, r1):
            if r0 == 0:
                carry["re"] = [state_ref[k * SLABS_PER_BLK + j] for k, j in tiles]
                carry["im"] = [state_ref[k * SLABS_PER_BLK + HALF + j] for k, j in tiles]
            xr, xi = carry["re"], carry["im"]
            for r in range(r0, r1):
                rows = pl.ds(r, SUBLANES, stride=PAIR_PITCH)
                for n, (k, j) in enumerate(tiles):
                    a_re = a2re_ref[k * HALF + j]
                    a_im = a2im_ref[k * HALF + j]
                    nr = (a_re * xr[n] + xs_refs[k][j, rows, :]) - a_im * xi[n]
                    ni = (a_re * xi[n] + xs_refs[k][HALF + j, rows, :]) + a_im * xr[n]
                    xs_refs[k][j, rows, :] = nr
                    xs_refs[k][HALF + j, rows, :] = ni
                    xr[n], xi[n] = nr, ni
            if r1 == N_PAIR:
                for n, (k, j) in enumerate(tiles):
                    state_ref[k * SLABS_PER_BLK + j] = xr[n]
                    state_ref[k * SLABS_PER_BLK + HALF + j] = xi[n]

        def c_proj(ks):
            for k in ks:
                lanes = slice(k * LANES, (k + 1) * LANES)
                x = jnp.concatenate([xs_refs[k][j] for j in range(SLABS_PER_BLK)], axis=1)
                ce = lax.dot_general(x.astype(jnp.bfloat16), wct_ref[k], _NT,
                                     preferred_element_type=jnp.float32)
                yo_ref[:, lanes] = ce[:, 0:LANES]
                e_ref[k, SUBLANES:SUBLANES + prow, :] = ce[:, LANES:]
                nxt = e_ref[k, pl.ds(SUBLANES + N_PAIR - 1, nb, stride=PAIR_PITCH), :]
                e_ref[k, pl.ds(SUBLANES - 1, nb, stride=PAIR_PITCH), :] = ecarry_ref[k]
                ecarry_ref[k] = nxt
                u_even = l_ref[:, 2 * k * LANES:(2 * k + 1) * LANES].astype(jnp.bfloat16)
                ye_ref[:, lanes] = (e_ref[k, SUBLANES - 1:SUBLANES - 1 + prow, :]
                                    + jnp.dot(u_even, wdm_ref[k], preferred_element_type=jnp.float32))

        def glu_emit():
            for b in range(nb):
                for k in range(N_BLK):
                    lanes = slice(k * LANES, (k + 1) * LANES)
                    for par, src in ((0, ye_ref), (1, yo_ref)):
                        ys_ref[k, pl.ds(b * t_len + par, N_PAIR, stride=2), :] = (
                            src[pl.ds(b * PAIR_PITCH, N_PAIR), lanes])
            y = jnp.concatenate([ys_ref[k] for k in range(N_BLK)], axis=1)
            u = jnp.concatenate([us_ref[k] for k in range(N_BLK)], axis=1)
            s_out = _glu_tail(y, u, d_ref, wglu_ref, bglu_ref)
            mix_ref[:, :, 0:SSM_WIDTH] = s_out.reshape(nb, t_len, SSM_WIDTH).astype(mix_ref.dtype)

        def conv():
            v = v_ref[:, SUBLANES:SUBLANES + t_len, :]
            z = cw_ref[0:1, :] * v_ref[:, SUBLANES - 2:SUBLANES - 2 + t_len, :]
            z = z + cw_ref[1:2, :] * v_ref[:, SUBLANES - 1:SUBLANES - 1 + t_len, :]
            z = z + cw_ref[2:3, :] * v
            mix_ref[:, :, SSM_WIDTH:] = (g_ref[...] * z).astype(mix_ref.dtype)
            v_ref[:, 0:SUBLANES, :] = v_ref[:, t_len:t_len + SUBLANES, :]

        q = N_PAIR // 4
        pieces = [
            lambda: (conv(), build_lhs(), b_proj((0,))), lambda: b_proj((1,)), lambda: b_proj((2,)),
            lambda: b_proj((3,)),
            lambda: scan(0, q), lambda: scan(q, 2 * q), lambda: scan(2 * q, 3 * q), lambda: scan(3 * q, N_PAIR),
            lambda: c_proj((0, 1)), lambda: c_proj((2, 3)),
            glu_emit,
        ]
        assert len(pieces) <= D_FF // FFN_TF

        db = n_dec // t_len
        x_top = jnp.where(step == last, xd_ref[...], x_ref[0:db].reshape(n_dec, D_MODEL))
        x = jnp.concatenate([x_top, x_ref[db:].reshape((nb - db) * t_len, D_MODEL)], axis=0)
        xn = _rms(x, n1_ref[...]).astype(jnp.bfloat16)
        for c, j in enumerate(range(0, D_FF, FFN_TF)):
            _swiglu_stage(xn, wg_ref, wu_ref, a_ref, j, j + FFN_TF)
            if c < len(pieces):
                pieces[c]()
        h = x + 0.5 * jnp.dot(a_ref[...], wd_ref[...], preferred_element_type=jnp.float32)
        hn = _rms(h, nm_ref[...]).astype(jnp.bfloat16)
        p = jnp.dot(hn, win_ref[...], preferred_element_type=jnp.float32)
        h_ref[...] = h.reshape(nb, t_len, D_MODEL)
        v_new = p[:, SSM_WIDTH + 2 * CONV_WIDTH:] * p[:, SSM_WIDTH:SSM_WIDTH + CONV_WIDTH]
        v_ref[:, SUBLANES:SUBLANES + t_len, :] = v_new.reshape(nb, t_len, CONV_WIDTH)
        g_ref[...] = p[:, SSM_WIDTH + CONV_WIDTH:SSM_WIDTH + 2 * CONV_WIDTH].reshape(nb, t_len, CONV_WIDTH)
        for k in range(N_BLK):
            us_ref[k] = p[:, k * LANES:(k + 1) * LANES]

    @pl.when(step == last)
    def _final():
        for n, (k, j) in enumerate(tiles):
            lanes = slice(k * BLK_STATE + j * LANES, k * BLK_STATE + (j + 1) * LANES)
            sre_ref[:, lanes] = state_ref[k * SLABS_PER_BLK + j]
            sim_ref[:, lanes] = state_ref[k * SLABS_PER_BLK + HALF + j]
        nconv_ref[...] = v_ref[:, SUBLANES - 2:SUBLANES, :]
        for k in range(N_BLK):
            ud_ref[:, k * LANES:(k + 1) * LANES] = us_ref[k, 0:n_dec, :]
        for b in range(n_dec // t_len):
            vd_ref[b * t_len:(b + 1) * t_len, :] = v_ref[b, SUBLANES:SUBLANES + t_len, :]
            gd_ref[b * t_len:(b + 1) * t_len, :] = g_ref[b]


def _ffn1_mixer(x, xd, n1, wg, wu, wd, nm, win, a2_re, a2_im, wb, wct, wdm, d, wglu, bglu, cw):
    nb, seq, _ = x.shape
    n_dec = xd.shape[0]
    n_chunk = seq // SCAN_T
    prow = nb * PAIR_PITCH
    f32, bf = jnp.float32, jnp.bfloat16
    main = lambda i: jnp.maximum(i - W1_STEPS, 0)
    return pl.pallas_call(
        _ffn1_mixer_kernel,
        grid=(W1_STEPS + n_chunk + 1,),
        in_specs=[
            pl.BlockSpec((nb, SCAN_T, D_MODEL), lambda i: (0, jnp.minimum(main(i), n_chunk - 1), 0)),
            _whole((n_dec, D_MODEL)),
            _whole((1, D_MODEL)),
            _weight_chunk((D_MODEL // W1_STEPS, D_FF), W1_STEPS),
            _weight_chunk((D_MODEL // W1_STEPS, D_FF), W1_STEPS),
            _weight_chunk((D_FF // W1_STEPS, D_MODEL), W1_STEPS),
            _whole((1, D_MODEL)),
            _weight_chunk((D_MODEL // W1_STEPS, IN_PROJ_WIDTH), W1_STEPS),
            _whole((N_TILE, SUBLANES, LANES)),
            _whole((N_TILE, SUBLANES, LANES)),
            _whole((N_BLK, 2 * LANES, 2 * BLK_STATE)),
            _whole((N_BLK, 2 * LANES, 2 * BLK_STATE)),
            _whole((N_BLK, LANES, LANES)),
            _whole((1, SSM_WIDTH)),
            _whole((SSM_WIDTH, SSM_WIDTH)),
            _whole((1, SSM_WIDTH)),
            _whole((CONV_K, CONV_WIDTH)),
        ],
        out_specs=[
            pl.BlockSpec((nb, SCAN_T, D_MODEL), lambda i: (0, main(i), 0)),
            pl.BlockSpec((nb, SCAN_T, D_MODEL), lambda i: (0, jnp.clip(main(i) - 1, 0, n_chunk - 1), 0)),
            pl.BlockSpec((nb, N_STATE), lambda i: (0, 0)),
            pl.BlockSpec((nb, N_STATE), lambda i: (0, 0)),
            pl.BlockSpec((nb, CONV_K - 1, CONV_WIDTH), lambda i: (0, 0, 0)),
            pl.BlockSpec((n_dec, SSM_WIDTH), lambda i: (0, 0)),
            pl.BlockSpec((n_dec, CONV_WIDTH), lambda i: (0, 0)),
            pl.BlockSpec((n_dec, CONV_WIDTH), lambda i: (0, 0)),
        ],
        out_shape=[
            jax.ShapeDtypeStruct((nb, seq + SCAN_T, D_MODEL), f32),
            jax.ShapeDtypeStruct((nb, seq, D_MODEL), bf),
            jax.ShapeDtypeStruct((nb, N_STATE), f32),
            jax.ShapeDtypeStruct((nb, N_STATE), f32),
            jax.ShapeDtypeStruct((nb, CONV_K - 1, CONV_WIDTH), f32),
            jax.ShapeDtypeStruct((n_dec, SSM_WIDTH), f32),
            jax.ShapeDtypeStruct((n_dec, CONV_WIDTH), f32),
            jax.ShapeDtypeStruct((n_dec, CONV_WIDTH), f32),
        ],
        scratch_shapes=[
            *[pltpu.VMEM((SLABS_PER_BLK, prow, LANES), f32) for _ in range(N_BLK)],
            pltpu.VMEM((N_BLK, nb * SCAN_T, LANES), f32),
            pltpu.VMEM((N_BLK, nb * SCAN_T, LANES), f32),
            pltpu.VMEM((prow, 2 * SSM_WIDTH), f32),
            pltpu.VMEM((prow, SSM_WIDTH), f32),
            pltpu.VMEM((prow, SSM_WIDTH), f32),
            pltpu.VMEM((N_BLK, SUBLANES + prow, LANES), f32),
            pltpu.VMEM((N_BLK, SUBLANES, LANES), f32),
            pltpu.VMEM((nb, SCAN_T + SUBLANES, CONV_WIDTH), f32),
            pltpu.VMEM((nb, SCAN_T, CONV_WIDTH), f32),
            pltpu.VMEM((2 * N_TILE, SUBLANES, LANES), f32),
            pltpu.VMEM((D_MODEL, D_FF), bf),
            pltpu.VMEM((D_MODEL, D_FF), bf),
            pltpu.VMEM((D_FF, D_MODEL), bf),
            pltpu.VMEM((D_MODEL, IN_PROJ_WIDTH), bf),
            pltpu.VMEM((nb * SCAN_T, D_FF), bf),
        ],
        compiler_params=pltpu.CompilerParams(
            dimension_semantics=("arbitrary",), vmem_limit_bytes=VMEM_LIMIT),
        name="ffn1_mixer",
    )(x, xd, n1, wg, wu, wd, nm, win, a2_re, a2_im, wb, wct, wdm, d, wglu, bglu, cw)


def _ffn2_rows(h, mix, weights, a_ref):
    wout_ref, n2_ref, wg_ref, wu_ref, wd_ref, nf_ref = weights
    h = h + jnp.dot(mix, wout_ref[...], preferred_element_type=jnp.float32)
    hn = _rms(h, n2_ref[...]).astype(jnp.bfloat16)
    h = h + 0.5 * _swiglu_staged(hn, wg_ref, wu_ref, wd_ref, a_ref)
    return _rms(h, nf_ref[...])


def _ffn2_kernel(h_ref, mix_ref, hd_ref, mixd_ref, wout_c, wg_c, wu_c, wd_c, n2_ref, nf_ref,
                 y_ref, yd_ref, wout_ref, wg_ref, wu_ref, wd_ref, a0_ref, a1_ref):
    step = pl.program_id(0)
    last = pl.num_programs(0) - 1
    weights = (wout_ref, n2_ref, wg_ref, wu_ref, wd_ref, nf_ref)

    @pl.when(step < W2_STEPS)
    def _load_weights():
        _cast_weight_chunks(step, ((wout_c, wout_ref), (wg_c, wg_ref), (wu_c, wu_ref), (wd_c, wd_ref)))

    @pl.when((step >= W2_STEPS) & (step < last))
    def _prompt_tile():
        assert h_ref.shape[0] == 2 * FFN_SUB
        ra, rb = slice(0, FFN_SUB), slice(FFN_SUB, 2 * FFN_SUB)
        h_a = h_ref[ra, :] + jnp.dot(mix_ref[ra, :], wout_ref[...], preferred_element_type=jnp.float32)
        h_b = h_ref[rb, :] + jnp.dot(mix_ref[rb, :], wout_ref[...], preferred_element_type=jnp.float32)
        hn_a = _rms(h_a, n2_ref[...]).astype(jnp.bfloat16)
        _swiglu_stage(hn_a, wg_ref, wu_ref, a0_ref, 0, FFN_TF)
        hn_b = _rms(h_b, n2_ref[...]).astype(jnp.bfloat16)
        _swiglu_stage(hn_a, wg_ref, wu_ref, a0_ref, FFN_TF, D_FF)
        f_a = jnp.dot(a0_ref[...], wd_ref[...], preferred_element_type=jnp.float32)
        _swiglu_stage(hn_b, wg_ref, wu_ref, a1_ref, 0, FFN_TF)
        y_ref[ra, :] = _rms(h_a + 0.5 * f_a, nf_ref[...])
        _swiglu_stage(hn_b, wg_ref, wu_ref, a1_ref, FFN_TF, D_FF)
        f_b = jnp.dot(a1_ref[...], wd_ref[...], preferred_element_type=jnp.float32)
        y_ref[rb, :] = _rms(h_b + 0.5 * f_b, nf_ref[...])

    @pl.when(step == last)
    def _decode_rows():
        yd_ref[...] = _ffn2_rows(hd_ref[...], mixd_ref[...], weights, a0_ref)


def _ffn2(h_all, mix, hd, mixd, wout, n2, wg, wu, wd, nf):
    nb, seq, _ = mix.shape
    seq_all = h_all.shape[1]
    per_b = seq // FFN_TM
    n_tiles = nb * per_b
    h_all = h_all.reshape(nb * seq_all, D_MODEL)
    mix = mix.reshape(nb * seq, D_MODEL)
    idx = lambda i: jnp.clip(i - W2_STEPS, 0, n_tiles - 1)
    tile = lambda i: (idx(i), 0)
    assert seq_all % SUBLANES == 0
    tile_all = lambda i: (pl.multiple_of((idx(i) // per_b) * seq_all + (idx(i) % per_b) * FFN_TM, SUBLANES), 0)
    yp, yd = pl.pallas_call(
        _ffn2_kernel,
        grid=(W2_STEPS + n_tiles + 1,),
        in_specs=[
            pl.BlockSpec((pl.Element(FFN_TM), pl.Element(D_MODEL)), tile_all),
            pl.BlockSpec((FFN_TM, D_MODEL), tile),
            _whole(hd.shape),
            _whole(mixd.shape),
            _weight_chunk((D_MODEL // W2_STEPS, D_MODEL), W2_STEPS),
            _weight_chunk((D_MODEL // W2_STEPS, D_FF), W2_STEPS),
            _weight_chunk((D_MODEL // W2_STEPS, D_FF), W2_STEPS),
            _weight_chunk((D_FF // W2_STEPS, D_MODEL), W2_STEPS),
            _whole((1, D_MODEL)),
            _whole((1, D_MODEL)),
        ],
        out_specs=[
            pl.BlockSpec((FFN_TM, D_MODEL), tile),
            pl.BlockSpec(hd.shape, lambda i: (0, 0)),
        ],
        out_shape=[
            jax.ShapeDtypeStruct((nb * seq, D_MODEL), jnp.float32),
            jax.ShapeDtypeStruct(hd.shape, jnp.float32),
        ],
        scratch_shapes=[
            pltpu.VMEM((D_MODEL, D_MODEL), jnp.bfloat16),
            pltpu.VMEM((D_MODEL, D_FF), jnp.bfloat16),
            pltpu.VMEM((D_MODEL, D_FF), jnp.bfloat16),
            pltpu.VMEM((D_FF, D_MODEL), jnp.bfloat16),
            pltpu.VMEM((FFN_SUB, D_FF), jnp.bfloat16),
            pltpu.VMEM((FFN_SUB, D_FF), jnp.bfloat16),
        ],
        compiler_params=pltpu.CompilerParams(
            dimension_semantics=("arbitrary",), vmem_limit_bytes=VMEM_LIMIT),
        name="outproj_ffn2",
    )(h_all, mix, hd, mixd, wout, wg, wu, wd, n2, nf)
    return yp.reshape(nb, seq, D_MODEL), yd


def _mixer_step_kernel(u_ref, v_ref, g_ref, hre_ref, him_ref, buf_ref, are_ref, aim_ref, bb_ref, wct_ref,
                       d_ref, wglu_ref, bglu_ref, cw_ref, mix_ref, sre_ref, sim_ref, nconv_ref):
    u = u_ref[...]
    u_bf = u.astype(jnp.bfloat16)
    ys = []
    for k in range(N_BLK):
        bu = jnp.dot(u_bf[:, k * LANES:(k + 1) * LANES], bb_ref[k], preferred_element_type=jnp.float32)
        st = slice(k * BLK_STATE, (k + 1) * BLK_STATE)
        a_re = are_ref[k:k + 1, :]
        a_im = aim_ref[k:k + 1, :]
        h_re = hre_ref[:, st]
        h_im = him_ref[:, st]
        x_re = (a_re * h_re + bu[:, :BLK_STATE]) - a_im * h_im
        x_im = (a_re * h_im + bu[:, BLK_STATE:]) + a_im * h_re
        sre_ref[:, st] = x_re
        sim_ref[:, st] = x_im
        x = jnp.concatenate([x_re, x_im], axis=1).astype(jnp.bfloat16)
        ys.append(lax.dot_general(x, wct_ref[k, 0:LANES, :], _NT, preferred_element_type=jnp.float32))
    y = jnp.concatenate(ys, axis=1)
    mix_ref[:, 0:SSM_WIDTH] = _glu_tail(y, u, d_ref, wglu_ref, bglu_ref).astype(mix_ref.dtype)

    v = v_ref[...]
    z = cw_ref[0:1, :] * buf_ref[0] + cw_ref[1:2, :] * buf_ref[1] + cw_ref[2:3, :] * v
    mix_ref[:, SSM_WIDTH:] = (g_ref[...] * z).astype(mix_ref.dtype)
    nconv_ref[0] = buf_ref[1]
    nconv_ref[1] = v


def _mixer_step(u, v, g, h_re, h_im, buf, a_re, a_im, bb, wct, d, wglu, bglu, cw):
    nb = u.shape[0]
    return pl.pallas_call(
        _mixer_step_kernel,
        out_shape=[
            jax.ShapeDtypeStruct((nb, D_MODEL), jnp.bfloat16),
            jax.ShapeDtypeStruct((nb, N_STATE), jnp.float32),
            jax.ShapeDtypeStruct((nb, N_STATE), jnp.float32),
            jax.ShapeDtypeStruct((CONV_K - 1, nb, CONV_WIDTH), jnp.float32),
        ],
        compiler_params=pltpu.CompilerParams(vmem_limit_bytes=VMEM_LIMIT),
        name="mixer_step",
    )(u, v, g, h_re, h_im, buf, a_re, a_im, bb, wct, d, wglu, bglu, cw)


def _ssm_layout(lam_re, lam_im, log_dt, b_re, b_im, c_re, c_im):
    gpb = N_SSM_GROUPS // N_BLK
    eye = jnp.eye(gpb, dtype=jnp.float32)
    lr = lam_re.reshape(N_BLK, BLK_STATE)
    li = lam_im.reshape(N_BLK, BLK_STATE)
    ldt = jnp.broadcast_to(log_dt[:, None], (N_SSM_GROUPS, SSM_STATE)).reshape(N_BLK, BLK_STATE)

    def blocks(m):
        return jnp.einsum("kgcp,gh->kgchp", m, eye).reshape(N_BLK, LANES, BLK_STATE)

    def b_blocks(b):
        return blocks(b.reshape(N_BLK, gpb, SSM_STATE, SSM_GROUP).transpose(0, 1, 3, 2))

    def c_blocks(c):
        return blocks(c.reshape(N_BLK, gpb, SSM_GROUP, SSM_STATE))

    return lr, li, ldt, b_blocks(b_re), b_blocks(b_im), c_blocks(c_re), c_blocks(c_im)


def kernel(x_prompt, x_sample, state_ssm_re, state_ssm_im, state_conv, ffn1_norm, ffn1_w_gate, ffn1_w_up, ffn1_w_down, mix_norm, w_in, ssm_lambda_re, ssm_lambda_im, ssm_log_dt, ssm_b_re, ssm_b_im, ssm_c_re, ssm_c_im, ssm_d, ssm_w_glu, ssm_b_glu, conv_w, w_out, ffn2_norm, ffn2_w_gate, ffn2_w_up, ffn2_w_down, final_norm):
    assert ffn1_norm.shape[0] == 1, "single-layer trunk"
    n_prompt, seq, _ = x_prompt.shape
    n_dec, dec_seq, _ = x_sample.shape
    assert dec_seq == 1 and n_prompt == SUBLANES and seq % FFN_TM == 0 and seq % SCAN_T == 0
    assert n_dec % SCAN_T == 0 and n_dec <= n_prompt * SCAN_T and FFN_TM % FFN_SUB == 0

    n1 = ffn1_norm[0][None, :]
    nm = mix_norm[0][None, :]
    n2 = ffn2_norm[0][None, :]
    nf = final_norm[None, :]
    wglu = ssm_w_glu[0].astype(jnp.bfloat16)
    layout = _ssm_layout(ssm_lambda_re[0], ssm_lambda_im[0], ssm_log_dt[0], ssm_b_re[0], ssm_b_im[0],
                         ssm_c_re[0], ssm_c_im[0])
    a_re, a_im, a2_re, a2_im, bb, wb, wct, wdm = _discretize(*layout)
    d = ssm_d[0][None, :]
    bglu = ssm_b_glu[0][None, :]
    cw = conv_w[0]

    h_all, mixp, p_re, p_im, p_conv, ud, vd, gd = _ffn1_mixer(
        x_prompt, x_sample.reshape(n_dec, D_MODEL), n1, ffn1_w_gate[0], ffn1_w_up[0], ffn1_w_down[0], nm,
        w_in[0], a2_re, a2_im, wb, wct, wdm, d, wglu, bglu, cw)

    hd = h_all[:n_dec // SCAN_T, seq:seq + SCAN_T, :].reshape(n_dec, D_MODEL)
    s_re0 = state_ssm_re[0].reshape(n_dec, N_STATE)
    s_im0 = state_ssm_im[0].reshape(n_dec, N_STATE)
    buf = state_conv[0].transpose(1, 0, 2)
    mixd, s_re, s_im, s_conv = _mixer_step(ud, vd, gd, s_re0, s_im0, buf, a_re, a_im, bb, wct,
                                           d, wglu, bglu, cw)

    yp, yd = _ffn2(h_all, mixp, hd, mixd, w_out[0], n2, ffn2_w_gate[0], ffn2_w_up[0], ffn2_w_down[0], nf)

    gs = (N_SSM_GROUPS, SSM_STATE)
    return (yp,
            yd.reshape(n_dec, 1, D_MODEL),
            p_re.reshape(1, n_prompt, *gs),
            p_im.reshape(1, n_prompt, *gs),
            p_conv[None],
            s_re.reshape(1, n_dec, *gs),
            s_im.reshape(1, n_dec, *gs),
            s_conv.transpose(1, 0, 2)[None])
```

```python
import jax
import jax.numpy as jnp
from jax import lax
from jax.experimental import pallas as pl
from jax.experimental.pallas import tpu as pltpu

D_MODEL = 1024
D_FF = 2816
SSM_WIDTH = 512
CONV_WIDTH = 512
SSM_GROUP = 16
N_SSM_GROUPS = 32
SSM_STATE = 64
CONV_K = 3
IN_PROJ_WIDTH = SSM_WIDTH + 3 * CONV_WIDTH
EPS = 1e-6

LANES = 128
SUBLANES = 8
N_STATE = N_SSM_GROUPS * SSM_STATE
N_BLK = SSM_WIDTH // LANES
BLK_STATE = N_STATE // N_BLK
SLABS_PER_BLK = 2 * BLK_STATE // LANES
HALF = SLABS_PER_BLK // 2
N_TILE = N_BLK * HALF

FFN_TM = 1024
FFN_SUB = 512
W1_STEPS = 16
W2_STEPS = 8
FFN_TF = 256
SCAN_T = 64
N_PAIR = SCAN_T // 2
PAIR_PITCH = N_PAIR + 4
VMEM_LIMIT = 62 * 1024 * 1024

_NT = (((1,), (1,)), ((), ()))


def _rms(x, g):
    r = lax.rsqrt(jnp.mean(x * x, axis=-1, keepdims=True) + EPS)
    return x * r * g


def _bdot(a, b):
    return jnp.dot(a.astype(jnp.bfloat16), b.astype(jnp.bfloat16), preferred_element_type=jnp.float32)


def _swiglu_stage(xn, wg_ref, wu_ref, a_ref, lo, hi):
    rows = xn.shape[0]
    for j in range(lo, hi, FFN_TF):
        g = jnp.dot(xn, wg_ref[:, j:j + FFN_TF], preferred_element_type=jnp.float32)
        u = jnp.dot(xn, wu_ref[:, j:j + FFN_TF], preferred_element_type=jnp.float32)
        a_ref[0:rows, j:j + FFN_TF] = ((g * jax.nn.sigmoid(g)) * u).astype(a_ref.dtype)


def _swiglu_staged(xn, wg_ref, wu_ref, wd_ref, a_ref):
    rows = xn.shape[0]
    _swiglu_stage(xn, wg_ref, wu_ref, a_ref, 0, D_FF)
    return jnp.dot(a_ref[0:rows, :], wd_ref[...], preferred_element_type=jnp.float32)


def _cast_weight_chunks(step, pairs):
    for src, dst in pairs:
        n = src.shape[0]
        dst[pl.ds(pl.multiple_of(step * n, n), n), :] = src[...].astype(dst.dtype)


def _whole(shape):
    return pl.BlockSpec(shape, lambda i: (0,) * len(shape), pipeline_mode=pl.Buffered(1))


def _weight_chunk(shape, n_steps):
    return pl.BlockSpec(shape, lambda i: (jnp.minimum(i, n_steps - 1), 0))


def _glu_tail(y, u, d_ref, wglu_ref, bglu_ref):
    y = y + d_ref[...] * u
    z = jax.nn.gelu(y)
    gate = jax.nn.sigmoid(_bdot(z, wglu_ref[...]) + bglu_ref[...])
    return z * gate


def _discretize_kernel(lr_ref, li_ref, ldt_ref, bre_ref, bim_ref, ctre_ref, ctim_ref,
                       are_ref, aim_ref, a2re_ref, a2im_ref, bb_ref, wb_ref, wct_ref, wdm_ref):
    row_group = lax.broadcasted_iota(jnp.int32, (LANES, BLK_STATE), 0) // SSM_GROUP
    col_group = lax.broadcasted_iota(jnp.int32, (LANES, BLK_STATE), 1) // SSM_STATE
    on_diagonal = row_group == col_group

    def blocks(m):
        pair = jnp.concatenate([m, m], axis=1)
        return jnp.where(on_diagonal, jnp.concatenate([pair] * (BLK_STATE // LANES), axis=1), 0.0)

    lr, li = lr_ref[...], li_ref[...]
    dt = jnp.exp(ldt_ref[...])
    mag = jnp.exp(lr * dt)
    ab_re = mag * jnp.cos(li * dt)
    ab_im = mag * jnp.sin(li * dt)
    den = lr * lr + li * li
    nr = ab_re - 1.0
    ni = ab_im
    coef_re = (nr * lr + ni * li) / den
    coef_im = (ni * lr - nr * li) / den
    are_ref[...] = ab_re
    aim_ref[...] = ab_im
    a2_re = ab_re * ab_re - ab_im * ab_im
    a2_im = 2.0 * (ab_re * ab_im)
    bf = jnp.bfloat16
    for k in range(N_BLK):
        ar, ai = ab_re[k:k + 1, :], ab_im[k:k + 1, :]
        for j in range(HALF):
            lanes = slice(j * LANES, (j + 1) * LANES)
            a2re_ref[k * HALF + j] = jnp.broadcast_to(a2_re[k:k + 1, lanes], (SUBLANES, LANES))
            a2im_ref[k * HALF + j] = jnp.broadcast_to(a2_im[k:k + 1, lanes], (SUBLANES, LANES))
        cr, ci = coef_re[k:k + 1, :], coef_im[k:k + 1, :]
        bre, bim = blocks(bre_ref[k]), blocks(bim_ref[k])
        bb_re = cr * bre - ci * bim
        bb_im = cr * bim + ci * bre
        bb_ref[k, :, 0:BLK_STATE] = bb_re.astype(bf)
        bb_ref[k, :, BLK_STATE:] = bb_im.astype(bf)
        wb_ref[k, 0:LANES, 0:BLK_STATE] = (ar * bb_re - ai * bb_im).astype(bf)
        wb_ref[k, 0:LANES, BLK_STATE:] = (ar * bb_im + ai * bb_re).astype(bf)
        wb_ref[k, LANES:, 0:BLK_STATE] = bb_re.astype(bf)
        wb_ref[k, LANES:, BLK_STATE:] = bb_im.astype(bf)
        ctre, ctim = blocks(ctre_ref[k]), blocks(ctim_ref[k])
        wct_ref[k, 0:LANES, 0:BLK_STATE] = ctre.astype(bf)
        wct_ref[k, 0:LANES, BLK_STATE:] = (-ctim).astype(bf)
        wct_ref[k, LANES:, 0:BLK_STATE] = (ctre * ar - ctim * ai).astype(bf)
        wct_ref[k, LANES:, BLK_STATE:] = (-(ctre * ai + ctim * ar)).astype(bf)
        wdm = (lax.dot_general(bb_re, ctre, _NT, precision=lax.Precision.HIGHEST,
                               preferred_element_type=jnp.float32)
               - lax.dot_general(bb_im, ctim, _NT, precision=lax.Precision.HIGHEST,
                                 preferred_element_type=jnp.float32))
        wdm_ref[k] = wdm.astype(bf)


def _discretize(lr, li, ldt, bre, bim, ctre, ctim):
    f32, bf = jnp.float32, jnp.bfloat16
    return pl.pallas_call(
        _discretize_kernel,
        out_shape=[
            jax.ShapeDtypeStruct((N_BLK, BLK_STATE), f32),
            jax.ShapeDtypeStruct((N_BLK, BLK_STATE), f32),
            jax.ShapeDtypeStruct((N_TILE, SUBLANES, LANES), f32),
            jax.ShapeDtypeStruct((N_TILE, SUBLANES, LANES), f32),
            jax.ShapeDtypeStruct((N_BLK, LANES, 2 * BLK_STATE), bf),
            jax.ShapeDtypeStruct((N_BLK, 2 * LANES, 2 * BLK_STATE), bf),
            jax.ShapeDtypeStruct((N_BLK, 2 * LANES, 2 * BLK_STATE), bf),
            jax.ShapeDtypeStruct((N_BLK, LANES, LANES), bf),
        ],
        name="ssm_discretize",
    )(lr, li, ldt, bre, bim, ctre, ctim)


def _ffn1_mixer_kernel(x_ref, xd_ref, n1_ref, wg_c, wu_c, wd_c, nm_ref, win_c,
                       a2re_ref, a2im_ref, wb_ref, wct_ref, wdm_ref, d_ref, wglu_ref, bglu_ref, cw_ref,
                       h_ref, mix_ref, sre_ref, sim_ref, nconv_ref, ud_ref, vd_ref, gd_ref,
                       xs0_ref, xs1_ref, xs2_ref, xs3_ref, us_ref, ys_ref, l_ref, yo_ref, ye_ref, e_ref,
                       ecarry_ref, v_ref, g_ref, state_ref, wg_ref, wu_ref, wd_ref, win_ref, a_ref):
    xs_refs = (xs0_ref, xs1_ref, xs2_ref, xs3_ref)
    step = pl.program_id(0)
    last = pl.num_programs(0) - 1
    nb, t_len, _ = x_ref.shape
    n_dec = xd_ref.shape[0]
    prow = nb * PAIR_PITCH
    tiles = [(k, j) for k in range(N_BLK) for j in range(HALF)]

    @pl.when(step < W1_STEPS)
    def _load_weights():
        _cast_weight_chunks(step, ((wg_c, wg_ref), (wu_c, wu_ref), (wd_c, wd_ref), (win_c, win_ref)))

    @pl.when(step == 0)
    def _init():
        for ref in (state_ref, us_ref, l_ref, e_ref, ecarry_ref, v_ref, g_ref):
            ref[...] = jnp.zeros_like(ref)

    @pl.when(step >= W1_STEPS)
    def _main():
        carry = {}

        def build_lhs():
            for b in range(nb):
                for k in range(N_BLK):
                    for par in range(2):
                        col = (2 * k + par) * LANES
                        l_ref[pl.ds(b * PAIR_PITCH, N_PAIR), col:col + LANES] = (
                            us_ref[k, pl.ds(b * t_len + par, N_PAIR, stride=2), :])

        def b_proj(ks):
            for k in ks:
                lhs = l_ref[:, 2 * k * LANES:2 * (k + 1) * LANES].astype(jnp.bfloat16)
                v = jnp.dot(lhs, wb_ref[k], preferred_element_type=jnp.float32)
                for j in range(SLABS_PER_BLK):
                    xs_refs[k][j] = v[:, j * LANES:(j + 1) * LANES]

        def scan(r0, r1):
            if r0 == 0:
                carry["re"] = [state_ref[k * SLABS_PER_BLK + j] for k, j in tiles]
                carry["im"] = [state_ref[k * SLABS_PER_BLK + HALF + j] for k, j in tiles]
            xr, xi = carry["re"], carry["im"]
            for r in range(r0, r1):
                rows = pl.ds(r, SUBLANES, stride=PAIR_PITCH)
                for n, (k, j) in enumerate(tiles):
                    a_re = a2re_ref[k * HALF + j]
                    a_im = a2im_ref[k * HALF + j]
                    nr = (a_re * xr[n] + xs_refs[k][j, rows, :]) - a_im * xi[n]
                    ni = (a_re * xi[n] + xs_refs[k][HALF + j, rows, :]) + a_im * xr[n]
                    xs_refs[k][j, rows, :] = nr
                    xs_refs[k][HALF + j, rows, :] = ni
                    xr[n], xi[n] = nr, ni
            if r1 == N_PAIR:
                for n, (k, j) in enumerate(tiles):
                    state_ref[k * SLABS_PER_BLK + j] = xr[n]
                    state_ref[k * SLABS_PER_BLK + HALF + j] = xi[n]

        def c_proj(ks):
            for k in ks:
                lanes = slice(k * LANES, (k + 1) * LANES)
                x = jnp.concatenate([xs_refs[k][j] for j in range(SLABS_PER_BLK)], axis=1)
                ce = lax.dot_general(x.astype(jnp.bfloat16), wct_ref[k], _NT,
                                     preferred_element_type=jnp.float32)
                yo_ref[:, lanes] = ce[:, 0:LANES]
                e_ref[k, SUBLANES:SUBLANES + prow, :] = ce[:, LANES:]
                nxt = e_ref[k, pl.ds(SUBLANES + N_PAIR - 1, nb, stride=PAIR_PITCH), :]
                e_ref[k, pl.ds(SUBLANES - 1, nb, stride=PAIR_PITCH), :] = ecarry_ref[k]
                ecarry_ref[k] = nxt
                u_even = l_ref[:, 2 * k * LANES:(2 * k + 1) * LANES].astype(jnp.bfloat16)
                ye_ref[:, lanes] = (e_ref[k, SUBLANES - 1:SUBLANES - 1 + prow, :]
                                    + jnp.dot(u_even, wdm_ref[k], preferred_element_type=jnp.float32))

        def glu_emit():
            for b in range(nb):
                for k in range(N_BLK):
                    lanes = slice(k * LANES, (k + 1) * LANES)
                    for par, src in ((0, ye_ref), (1, yo_ref)):
                        ys_ref[k, pl.ds(b * t_len + par, N_PAIR, stride=2), :] = (
                            src[pl.ds(b * PAIR_PITCH, N_PAIR), lanes])
            y = jnp.concatenate([ys_ref[k] for k in range(N_BLK)], axis=1)
            u = jnp.concatenate([us_ref[k] for k in range(N_BLK)], axis=1)
            s_out = _glu_tail(y, u, d_ref, wglu_ref, bglu_ref)
            mix_ref[:, :, 0:SSM_WIDTH] = s_out.reshape(nb, t_len, SSM_WIDTH).astype(mix_ref.dtype)

        def conv():
            v = v_ref[:, SUBLANES:SUBLANES + t_len, :]
            z = cw_ref[0:1, :] * v_ref[:, SUBLANES - 2:SUBLANES - 2 + t_len, :]
            z = z + cw_ref[1:2, :] * v_ref[:, SUBLANES - 1:SUBLANES - 1 + t_len, :]
            z = z + cw_ref[2:3, :] * v
            mix_ref[:, :, SSM_WIDTH:] = (g_ref[...] * z).astype(mix_ref.dtype)
            v_ref[:, 0:SUBLANES, :] = v_ref[:, t_len:t_len + SUBLANES, :]

        q = N_PAIR // 4
        pieces = [
            lambda: (conv(), build_lhs(), b_proj((0,))), lambda: b_proj((1,)), lambda: b_proj((2,)),
            lambda: b_proj((3,)),
            lambda: scan(0, q), lambda: scan(q, 2 * q), lambda: scan(2 * q, 3 * q), lambda: scan(3 * q, N_PAIR),
            lambda: c_proj((0, 1)), lambda: c_proj((2, 3)),
            glu_emit,
        ]
        assert len(pieces) <= D_FF // FFN_TF

        db = n_dec // t_len
        x_top = jnp.where(step == last, xd_ref[...], x_ref[0:db].reshape(n_dec, D_MODEL))
        x = jnp.concatenate([x_top, x_ref[db:].reshape((nb - db) * t_len, D_MODEL)], axis=0)
        xn = _rms(x, n1_ref[...]).astype(jnp.bfloat16)
        for c, j in enumerate(range(0, D_FF, FFN_TF)):
            _swiglu_stage(xn, wg_ref, wu_ref, a_ref, j, j + FFN_TF)
            if c < len(pieces):
                pieces[c]()
        h = x + 0.5 * jnp.dot(a_ref[...], wd_ref[...], preferred_element_type=jnp.float32)
        hn = _rms(h, nm_ref[...]).astype(jnp.bfloat16)
        p = jnp.dot(hn, win_ref[...], preferred_element_type=jnp.float32)
        h_ref[...] = h.reshape(nb, t_len, D_MODEL)
        v_new = p[:, SSM_WIDTH + 2 * CONV_WIDTH:] * p[:, SSM_WIDTH:SSM_WIDTH + CONV_WIDTH]
        v_ref[:, SUBLANES:SUBLANES + t_len, :] = v_new.reshape(nb, t_len, CONV_WIDTH)
        g_ref[...] = p[:, SSM_WIDTH + CONV_WIDTH:SSM_WIDTH + 2 * CONV_WIDTH].reshape(nb, t_len, CONV_WIDTH)
        for k in range(N_BLK):
            us_ref[k] = p[:, k * LANES:(k + 1) * LANES]

    @pl.when(step == last)
    def _final():
        for n, (k, j) in enumerate(tiles):
            lanes = slice(k * BLK_STATE + j * LANES, k * BLK_STATE + (j + 1) * LANES)
            sre_ref[:, lanes] = state_ref[k * SLABS_PER_BLK + j]
            sim_ref[:, lanes] = state_ref[k * SLABS_PER_BLK + HALF + j]
        nconv_ref[...] = v_ref[:, SUBLANES - 2:SUBLANES, :]
        for k in range(N_BLK):
            ud_ref[:, k * LANES:(k + 1) * LANES] = us_ref[k, 0:n_dec, :]
        for b in range(n_dec // t_len):
            vd_ref[b * t_len:(b + 1) * t_len, :] = v_ref[b, SUBLANES:SUBLANES + t_len, :]
            gd_ref[b * t_len:(b + 1) * t_len, :] = g_ref[b]


def _ffn1_mixer(x, xd, n1, wg, wu, wd, nm, win, a2_re, a2_im, wb, wct, wdm, d, wglu, bglu, cw):
    nb, seq, _ = x.shape
    n_dec = xd.shape[0]
    n_chunk = seq // SCAN_T
    prow = nb * PAIR_PITCH
    f32, bf = jnp.float32, jnp.bfloat16
    main = lambda i: jnp.maximum(i - W1_STEPS, 0)
    return pl.pallas_call(
        _ffn1_mixer_kernel,
        grid=(W1_STEPS + n_chunk + 1,),
        in_specs=[
            pl.BlockSpec((nb, SCAN_T, D_MODEL), lambda i: (0, jnp.minimum(main(i), n_chunk - 1), 0)),
            _whole((n_dec, D_MODEL)),
            _whole((1, D_MODEL)),
            _weight_chunk((D_MODEL // W1_STEPS, D_FF), W1_STEPS),
            _weight_chunk((D_MODEL // W1_STEPS, D_FF), W1_STEPS),
            _weight_chunk((D_FF // W1_STEPS, D_MODEL), W1_STEPS),
            _whole((1, D_MODEL)),
            _weight_chunk((D_MODEL // W1_STEPS, IN_PROJ_WIDTH), W1_STEPS),
            _whole((N_TILE, SUBLANES, LANES)),
            _whole((N_TILE, SUBLANES, LANES)),
            _whole((N_BLK, 2 * LANES, 2 * BLK_STATE)),
            _whole((N_BLK, 2 * LANES, 2 * BLK_STATE)),
            _whole((N_BLK, LANES, LANES)),
            _whole((1, SSM_WIDTH)),
            _whole((SSM_WIDTH, SSM_WIDTH)),
            _whole((1, SSM_WIDTH)),
            _whole((CONV_K, CONV_WIDTH)),
        ],
        out_specs=[
            pl.BlockSpec((nb, SCAN_T, D_MODEL), lambda i: (0, main(i), 0)),
            pl.BlockSpec((nb, SCAN_T, D_MODEL), lambda i: (0, jnp.clip(main(i) - 1, 0, n_chunk - 1), 0)),
            pl.BlockSpec((nb, N_STATE), lambda i: (0, 0)),
            pl.BlockSpec((nb, N_STATE), lambda i: (0, 0)),
            pl.BlockSpec((nb, CONV_K - 1, CONV_WIDTH), lambda i: (0, 0, 0)),
            pl.BlockSpec((n_dec, SSM_WIDTH), lambda i: (0, 0)),
            pl.BlockSpec((n_dec, CONV_WIDTH), lambda i: (0, 0)),
            pl.BlockSpec((n_dec, CONV_WIDTH), lambda i: (0, 0)),
        ],
        out_shape=[
            jax.ShapeDtypeStruct((nb, seq + SCAN_T, D_MODEL), f32),
            jax.ShapeDtypeStruct((nb, seq, D_MODEL), bf),
            jax.ShapeDtypeStruct((nb, N_STATE), f32),
            jax.ShapeDtypeStruct((nb, N_STATE), f32),
            jax.ShapeDtypeStruct((nb, CONV_K - 1, CONV_WIDTH), f32),
            jax.ShapeDtypeStruct((n_dec, SSM_WIDTH), f32),
            jax.ShapeDtypeStruct((n_dec, CONV_WIDTH), f32),
            jax.ShapeDtypeStruct((n_dec, CONV_WIDTH), f32),
        ],
        scratch_shapes=[
            *[pltpu.VMEM((SLABS_PER_BLK, prow, LANES), f32) for _ in range(N_BLK)],
            pltpu.VMEM((N_BLK, nb * SCAN_T, LANES), f32),
            pltpu.VMEM((N_BLK, nb * SCAN_T, LANES), f32),
            pltpu.VMEM((prow, 2 * SSM_WIDTH), f32),
            pltpu.VMEM((prow, SSM_WIDTH), f32),
            pltpu.VMEM((prow, SSM_WIDTH), f32),
            pltpu.VMEM((N_BLK, SUBLANES + prow, LANES), f32),
            pltpu.VMEM((N_BLK, SUBLANES, LANES), f32),
            pltpu.VMEM((nb, SCAN_T + SUBLANES, CONV_WIDTH), f32),
            pltpu.VMEM((nb, SCAN_T, CONV_WIDTH), f32),
            pltpu.VMEM((2 * N_TILE, SUBLANES, LANES), f32),
            pltpu.VMEM((D_MODEL, D_FF), bf),
            pltpu.VMEM((D_MODEL, D_FF), bf),
            pltpu.VMEM((D_FF, D_MODEL), bf),
            pltpu.VMEM((D_MODEL, IN_PROJ_WIDTH), bf),
            pltpu.VMEM((nb * SCAN_T, D_FF), bf),
        ],
        compiler_params=pltpu.CompilerParams(
            dimension_semantics=("arbitrary",), vmem_limit_bytes=VMEM_LIMIT),
        name="ffn1_mixer",
    )(x, xd, n1, wg, wu, wd, nm, win, a2_re, a2_im, wb, wct, wdm, d, wglu, bglu, cw)


def _ffn2_rows(h, mix, weights, a_ref):
    wout_ref, n2_ref, wg_ref, wu_ref, wd_ref, nf_ref = weights
    h = h + jnp.dot(mix, wout_ref[...], preferred_element_type=jnp.float32)
    hn = _rms(h, n2_ref[...]).astype(jnp.bfloat16)
    h = h + 0.5 * _swiglu_staged(hn, wg_ref, wu_ref, wd_ref, a_ref)
    return _rms(h, nf_ref[...])


def _ffn2_kernel(h_ref, mix_ref, hd_ref, mixd_ref, wout_c, wg_c, wu_c, wd_c, n2_ref, nf_ref,
                 y_ref, yd_ref, wout_ref, wg_ref, wu_ref, wd_ref, a0_ref, a1_ref):
    step = pl.program_id(0)
    last = pl.num_programs(0) - 1
    weights = (wout_ref, n2_ref, wg_ref, wu_ref, wd_ref, nf_ref)

    @pl.when(step < W2_STEPS)
    def _load_weights():
        _cast_weight_chunks(step, ((wout_c, wout_ref), (wg_c, wg_ref), (wu_c, wu_ref), (wd_c, wd_ref)))

    @pl.when((step >= W2_STEPS) & (step < last))
    def _prompt_tile():
        assert h_ref.shape[0] == 2 * FFN_SUB
        ra, rb = slice(0, FFN_SUB), slice(FFN_SUB, 2 * FFN_SUB)
        h_a = h_ref[ra, :] + jnp.dot(mix_ref[ra, :], wout_ref[...], preferred_element_type=jnp.float32)
        h_b = h_ref[rb, :] + jnp.dot(mix_ref[rb, :], wout_ref[...], preferred_element_type=jnp.float32)
        hn_a = _rms(h_a, n2_ref[...]).astype(jnp.bfloat16)
        _swiglu_stage(hn_a, wg_ref, wu_ref, a0_ref, 0, FFN_TF)
        hn_b = _rms(h_b, n2_ref[...]).astype(jnp.bfloat16)
        _swiglu_stage(hn_a, wg_ref, wu_ref, a0_ref, FFN_TF, D_FF)
        f_a = jnp.dot(a0_ref[...], wd_ref[...], preferred_element_type=jnp.float32)
        _swiglu_stage(hn_b, wg_ref, wu_ref, a1_ref, 0, FFN_TF)
        y_ref[ra, :] = _rms(h_a + 0.5 * f_a, nf_ref[...])
        _swiglu_stage(hn_b, wg_ref, wu_ref, a1_ref, FFN_TF, D_FF)
        f_b = jnp.dot(a1_ref[...], wd_ref[...], preferred_element_type=jnp.float32)
        y_ref[rb, :] = _rms(h_b + 0.5 * f_b, nf_ref[...])

    @pl.when(step == last)
    def _decode_rows():
        yd_ref[...] = _ffn2_rows(hd_ref[...], mixd_ref[...], weights, a0_ref)


def _ffn2(h_all, mix, hd, mixd, wout, n2, wg, wu, wd, nf):
    nb, seq, _ = mix.shape
    seq_all = h_all.shape[1]
    per_b = seq // FFN_TM
    n_tiles = nb * per_b
    h_all = h_all.reshape(nb * seq_all, D_MODEL)
    mix = mix.reshape(nb * seq, D_MODEL)
    idx = lambda i: jnp.clip(i - W2_STEPS, 0, n_tiles - 1)
    tile = lambda i: (idx(i), 0)
    assert seq_all % SUBLANES == 0
    tile_all = lambda i: (pl.multiple_of((idx(i) // per_b) * seq_all + (idx(i) % per_b) * FFN_TM, SUBLANES), 0)
    yp, yd = pl.pallas_call(
        _ffn2_kernel,
        grid=(W2_STEPS + n_tiles + 1,),
        in_specs=[
            pl.BlockSpec((pl.Element(FFN_TM), pl.Element(D_MODEL)), tile_all),
            pl.BlockSpec((FFN_TM, D_MODEL), tile),
            _whole(hd.shape),
            _whole(mixd.shape),
            _weight_chunk((D_MODEL // W2_STEPS, D_MODEL), W2_STEPS),
            _weight_chunk((D_MODEL // W2_STEPS, D_FF), W2_STEPS),
            _weight_chunk((D_MODEL // W2_STEPS, D_FF), W2_STEPS),
            _weight_chunk((D_FF // W2_STEPS, D_MODEL), W2_STEPS),
            _whole((1, D_MODEL)),
            _whole((1, D_MODEL)),
        ],
        out_specs=[
            pl.BlockSpec((FFN_TM, D_MODEL), tile),
            pl.BlockSpec(hd.shape, lambda i: (0, 0)),
        ],
        out_shape=[
            jax.ShapeDtypeStruct((nb * seq, D_MODEL), jnp.float32),
            jax.ShapeDtypeStruct(hd.shape, jnp.float32),
        ],
        scratch_shapes=[
            pltpu.VMEM((D_MODEL, D_MODEL), jnp.bfloat16),
            pltpu.VMEM((D_MODEL, D_FF), jnp.bfloat16),
            pltpu.VMEM((D_MODEL, D_FF), jnp.bfloat16),
            pltpu.VMEM((D_FF, D_MODEL), jnp.bfloat16),
            pltpu.VMEM((FFN_SUB, D_FF), jnp.bfloat16),
            pltpu.VMEM((FFN_SUB, D_FF), jnp.bfloat16),
        ],
        compiler_params=pltpu.CompilerParams(
            dimension_semantics=("arbitrary",), vmem_limit_bytes=VMEM_LIMIT),
        name="outproj_ffn2",
    )(h_all, mix, hd, mixd, wout, wg, wu, wd, n2, nf)
    return yp.reshape(nb, seq, D_MODEL), yd


def _mixer_step_kernel(u_ref, v_ref, g_ref, hre_ref, him_ref, buf_ref, are_ref, aim_ref, bb_ref, wct_ref,
                       d_ref, wglu_ref, bglu_ref, cw_ref, mix_ref, sre_ref, sim_ref, nconv_ref):
    u = u_ref[...]
    u_bf = u.astype(jnp.bfloat16)
    ys = []
    for k in range(N_BLK):
        bu = jnp.dot(u_bf[:, k * LANES:(k + 1) * LANES], bb_ref[k], preferred_element_type=jnp.float32)
        st = slice(k * BLK_STATE, (k + 1) * BLK_STATE)
        a_re = are_ref[k:k + 1, :]
        a_im = aim_ref[k:k + 1, :]
        h_re = hre_ref[:, st]
        h_im = him_ref[:, st]
        x_re = (a_re * h_re + bu[:, :BLK_STATE]) - a_im * h_im
        x_im = (a_re * h_im + bu[:, BLK_STATE:]) + a_im * h_re
        sre_ref[:, st] = x_re
        sim_ref[:, st] = x_im
        x = jnp.concatenate([x_re, x_im], axis=1).astype(jnp.bfloat16)
        ys.append(lax.dot_general(x, wct_ref[k, 0:LANES, :], _NT, preferred_element_type=jnp.float32))
    y = jnp.concatenate(ys, axis=1)
    mix_ref[:, 0:SSM_WIDTH] = _glu_tail(y, u, d_ref, wglu_ref, bglu_ref).astype(mix_ref.dtype)

    v = v_ref[...]
    z = cw_ref[0:1, :] * buf_ref[0] + cw_ref[1:2, :] * buf_ref[1] + cw_ref[2:3, :] * v
    mix_ref[:, SSM_WIDTH:] = (g_ref[...] * z).astype(mix_ref.dtype)
    nconv_ref[0] = buf_ref[1]
    nconv_ref[1] = v


def _mixer_step(u, v, g, h_re, h_im, buf, a_re, a_im, bb, wct, d, wglu, bglu, cw):
    nb = u.shape[0]
    return pl.pallas_call(
        _mixer_step_kernel,
        out_shape=[
            jax.ShapeDtypeStruct((nb, D_MODEL), jnp.bfloat16),
            jax.ShapeDtypeStruct((nb, N_STATE), jnp.float32),
            jax.ShapeDtypeStruct((nb, N_STATE), jnp.float32),
            jax.ShapeDtypeStruct((CONV_K - 1, nb, CONV_WIDTH), jnp.float32),
        ],
        compiler_params=pltpu.CompilerParams(vmem_limit_bytes=VMEM_LIMIT),
        name="mixer_step",
    )(u, v, g, h_re, h_im, buf, a_re, a_im, bb, wct, d, wglu, bglu, cw)


def _ssm_layout(lam_re, lam_im, log_dt, b_re, b_im, c_re, c_im):
    gpb = N_SSM_GROUPS // N_BLK
    lr = lam_re.reshape(N_BLK, BLK_STATE)
    li = lam_im.reshape(N_BLK, BLK_STATE)
    ldt = jnp.broadcast_to(log_dt[:, None], (N_SSM_GROUPS, SSM_STATE)).reshape(N_BLK, BLK_STATE)

    def b_rows(b):
        return b.reshape(N_BLK, gpb, SSM_STATE, SSM_GROUP).transpose(0, 1, 3, 2).reshape(N_BLK, LANES, SSM_STATE)

    def c_rows(c):
        return c.reshape(N_BLK, LANES, SSM_STATE)

    return lr, li, ldt, b_rows(b_re), b_rows(b_im), c_rows(c_re), c_rows(c_im)


def kernel(x_prompt, x_sample, state_ssm_re, state_ssm_im, state_conv, ffn1_norm, ffn1_w_gate, ffn1_w_up, ffn1_w_down, mix_norm, w_in, ssm_lambda_re, ssm_lambda_im, ssm_log_dt, ssm_b_re, ssm_b_im, ssm_c_re, ssm_c_im, ssm_d, ssm_w_glu, ssm_b_glu, conv_w, w_out, ffn2_norm, ffn2_w_gate, ffn2_w_up, ffn2_w_down, final_norm):
    assert ffn1_norm.shape[0] == 1, "single-layer trunk"
    n_prompt, seq, _ = x_prompt.shape
    n_dec, dec_seq, _ = x_sample.shape
    assert dec_seq == 1 and n_prompt == SUBLANES and seq % FFN_TM == 0 and seq % SCAN_T == 0
    assert n_dec % SCAN_T == 0 and n_dec <= n_prompt * SCAN_T and FFN_TM % FFN_SUB == 0

    n1 = ffn1_norm[0][None, :]
    nm = mix_norm[0][None, :]
    n2 = ffn2_norm[0][None, :]
    nf = final_norm[None, :]
    wglu = ssm_w_glu[0].astype(jnp.bfloat16)
    layout = _ssm_layout(ssm_lambda_re[0], ssm_lambda_im[0], ssm_log_dt[0], ssm_b_re[0], ssm_b_im[0],
                         ssm_c_re[0], ssm_c_im[0])
    a_re, a_im, a2_re, a2_im, bb, wb, wct, wdm = _discretize(*layout)
    d = ssm_d[0][None, :]
    bglu = ssm_b_glu[0][None, :]
    cw = conv_w[0]

    h_all, mixp, p_re, p_im, p_conv, ud, vd, gd = _ffn1_mixer(
        x_prompt, x_sample.reshape(n_dec, D_MODEL), n1, ffn1_w_gate[0], ffn1_w_up[0], ffn1_w_down[0], nm,
        w_in[0], a2_re, a2_im, wb, wct, wdm, d, wglu, bglu, cw)

    hd = h_all[:n_dec // SCAN_T, seq:seq + SCAN_T, :].reshape(n_dec, D_MODEL)
    s_re0 = state_ssm_re[0].reshape(n_dec, N_STATE)
    s_im0 = state_ssm_im[0].reshape(n_dec, N_STATE)
    buf = state_conv[0].transpose(1, 0, 2)
    mixd, s_re, s_im, s_conv = _mixer_step(ud, vd, gd, s_re0, s_im0, buf, a_re, a_im, bb, wct,
                                           d, wglu, bglu, cw)

    yp, yd = _ffn2(h_all, mixp, hd, mixd, w_out[0], n2, ffn2_w_gate[0], ffn2_w_up[0], ffn2_w_down[0], nf)

    gs = (N_SSM_GROUPS, SSM_STATE)
    return (yp,
            yd.reshape(n_dec, 1, D_MODEL),
            p_re.reshape(1, n_prompt, *gs),
            p_im.reshape(1, n_prompt, *gs),
            p_conv[None],
            s_re.reshape(1, n_dec, *gs),
            s_im.reshape(1, n_dec, *gs),
            s_conv.transpose(1, 0, 2)[None])
```

```python
import jax
import jax.numpy as jnp
from jax import lax
from jax.experimental import pallas as pl
from jax.experimental.pallas import tpu as pltpu

D_MODEL = 1024
D_FF = 2816
SSM_WIDTH = 512
CONV_WIDTH = 512
SSM_GROUP = 16
N_SSM_GROUPS = 32
SSM_STATE = 64
CONV_K = 3
IN_PROJ_WIDTH = SSM_WIDTH + 3 * CONV_WIDTH
EPS = 1e-6

LANES = 128
SUBLANES = 8
N_STATE = N_SSM_GROUPS * SSM_STATE
N_BLK = SSM_WIDTH // LANES
BLK_STATE = N_STATE // N_BLK
SLABS_PER_BLK = 2 * BLK_STATE // LANES
HALF = SLABS_PER_BLK // 2
N_TILE = N_BLK * HALF

FFN_TM = 1024
FFN_SUB = 512
W1_STEPS = 16
W2_STEPS = 8
FFN_TF = 256
SCAN_T = 64
N_PAIR = SCAN_T // 2
PAIR_PITCH = N_PAIR + 4
VMEM_LIMIT = 62 * 1024 * 1024

_NT = (((1,), (1,)), ((), ()))


def _rms(x, g):
    r = lax.rsqrt(jnp.mean(x * x, axis=-1, keepdims=True) + EPS)
    return x * r * g


def _bdot(a, b):
    return jnp.dot(a.astype(jnp.bfloat16), b.astype(jnp.bfloat16), preferred_element_type=jnp.float32)


def _swiglu_stage(xn, wg_ref, wu_ref, a_ref, lo, hi):
    rows = xn.shape[0]
    for j in range(lo, hi, FFN_TF):
        g = jnp.dot(xn, wg_ref[:, j:j + FFN_TF], preferred_element_type=jnp.float32)
        u = jnp.dot(xn, wu_ref[:, j:j + FFN_TF], preferred_element_type=jnp.float32)
        a_ref[0:rows, j:j + FFN_TF] = ((g * jax.nn.sigmoid(g)) * u).astype(a_ref.dtype)


def _swiglu_staged(xn, wg_ref, wu_ref, wd_ref, a_ref):
    rows = xn.shape[0]
    _swiglu_stage(xn, wg_ref, wu_ref, a_ref, 0, D_FF)
    return jnp.dot(a_ref[0:rows, :], wd_ref[...], preferred_element_type=jnp.float32)


def _cast_weight_chunks(step, pairs):
    for src, dst in pairs:
        n = src.shape[0]
        dst[pl.ds(pl.multiple_of(step * n, n), n), :] = src[...].astype(dst.dtype)


def _whole(shape):
    return pl.BlockSpec(shape, lambda i: (0,) * len(shape), pipeline_mode=pl.Buffered(1))


def _weight_chunk(shape, n_steps):
    return pl.BlockSpec(shape, lambda i: (jnp.minimum(i, n_steps - 1), 0))


def _glu_tail(y, u, d_ref, wglu_ref, bglu_ref):
    y = y + d_ref[...] * u
    z = jax.nn.gelu(y)
    gate = jax.nn.sigmoid(_bdot(z, wglu_ref[...]) + bglu_ref[...])
    return z * gate


def _discretize_kernel(lr_ref, li_ref, ldt_ref, bre_ref, bim_ref, ctre_ref, ctim_ref,
                       are_ref, aim_ref, a2re_ref, a2im_ref, bb_ref, wb_ref, wct_ref, wdm_ref):
    row_group = lax.broadcasted_iota(jnp.int32, (LANES, BLK_STATE), 0) // SSM_GROUP
    col_group = lax.broadcasted_iota(jnp.int32, (LANES, BLK_STATE), 1) // SSM_STATE
    on_diagonal = row_group == col_group

    def blocks(m):
        pair = jnp.concatenate([m, m], axis=1)
        return jnp.where(on_diagonal, jnp.concatenate([pair] * (BLK_STATE // LANES), axis=1), 0.0)

    lr, li = lr_ref[...], li_ref[...]
    dt = jnp.exp(ldt_ref[...])
    mag = jnp.exp(lr * dt)
    ab_re = mag * jnp.cos(li * dt)
    ab_im = mag * jnp.sin(li * dt)
    den = lr * lr + li * li
    nr = ab_re - 1.0
    ni = ab_im
    coef_re = (nr * lr + ni * li) / den
    coef_im = (ni * lr - nr * li) / den
    are_ref[...] = ab_re
    aim_ref[...] = ab_im
    a2_re = ab_re * ab_re - ab_im * ab_im
    a2_im = 2.0 * (ab_re * ab_im)
    bf = jnp.bfloat16
    for k in range(N_BLK):
        ar, ai = ab_re[k:k + 1, :], ab_im[k:k + 1, :]
        for j in range(HALF):
            lanes = slice(j * LANES, (j + 1) * LANES)
            a2re_ref[k * HALF + j] = jnp.broadcast_to(a2_re[k:k + 1, lanes], (SUBLANES, LANES))
            a2im_ref[k * HALF + j] = jnp.broadcast_to(a2_im[k:k + 1, lanes], (SUBLANES, LANES))
        cr, ci = coef_re[k:k + 1, :], coef_im[k:k + 1, :]
        bre, bim = blocks(bre_ref[k]), blocks(bim_ref[k])
        bb_re = cr * bre - ci * bim
        bb_im = cr * bim + ci * bre
        bb_ref[k, :, 0:BLK_STATE] = bb_re.astype(bf)
        bb_ref[k, :, BLK_STATE:] = bb_im.astype(bf)
        wb_ref[k, 0:LANES, 0:BLK_STATE] = (ar * bb_re - ai * bb_im).astype(bf)
        wb_ref[k, 0:LANES, BLK_STATE:] = (ar * bb_im + ai * bb_re).astype(bf)
        wb_ref[k, LANES:, 0:BLK_STATE] = bb_re.astype(bf)
        wb_ref[k, LANES:, BLK_STATE:] = bb_im.astype(bf)
        ctre, ctim = blocks(ctre_ref[k]), blocks(ctim_ref[k])
        wct_ref[k, 0:LANES, 0:BLK_STATE] = ctre.astype(bf)
        wct_ref[k, 0:LANES, BLK_STATE:] = (-ctim).astype(bf)
        wct_ref[k, LANES:, 0:BLK_STATE] = (ctre * ar - ctim * ai).astype(bf)
        wct_ref[k, LANES:, BLK_STATE:] = (-(ctre * ai + ctim * ar)).astype(bf)
        wdm = (lax.dot_general(bb_re, ctre, _NT, precision=lax.Precision.HIGHEST,
                               preferred_element_type=jnp.float32)
               - lax.dot_general(bb_im, ctim, _NT, precision=lax.Precision.HIGHEST,
                                 preferred_element_type=jnp.float32))
        wdm_ref[k] = wdm.astype(bf)


def _discretize(lr, li, ldt, bre, bim, ctre, ctim):
    f32, bf = jnp.float32, jnp.bfloat16
    return pl.pallas_call(
        _discretize_kernel,
        out_shape=[
            jax.ShapeDtypeStruct((N_BLK, BLK_STATE), f32),
            jax.ShapeDtypeStruct((N_BLK, BLK_STATE), f32),
            jax.ShapeDtypeStruct((N_TILE, SUBLANES, LANES), f32),
            jax.ShapeDtypeStruct((N_TILE, SUBLANES, LANES), f32),
            jax.ShapeDtypeStruct((N_BLK, LANES, 2 * BLK_STATE), bf),
            jax.ShapeDtypeStruct((N_BLK, 2 * LANES, 2 * BLK_STATE), bf),
            jax.ShapeDtypeStruct((N_BLK, 2 * LANES, 2 * BLK_STATE), bf),
            jax.ShapeDtypeStruct((N_BLK, LANES, LANES), bf),
        ],
        name="ssm_discretize",
    )(lr, li, ldt, bre, bim, ctre, ctim)


def _ffn1_mixer_kernel(x_ref, xd_ref, n1_ref, wg_c, wu_c, wd_c, nm_ref, win_c,
                       a2re_ref, a2im_ref, wb_ref, wct_ref, wdm_ref, d_ref, wglu_ref, bglu_ref, cw_ref,
                       h_ref, mix_ref, sre_ref, sim_ref, nconv_ref, ud_ref, vd_ref, gd_ref,
                       xs0_ref, xs1_ref, xs2_ref, xs3_ref, us_ref, ys_ref, l_ref, yo_ref, ye_ref, e_ref,
                       ecarry_ref, v_ref, g_ref, state_ref, wg_ref, wu_ref, wd_ref, win_ref, a_ref):
    xs_refs = (xs0_ref, xs1_ref, xs2_ref, xs3_ref)
    step = pl.program_id(0)
    last = pl.num_programs(0) - 1
    nb, t_len, _ = x_ref.shape
    n_dec = xd_ref.shape[0]
    prow = nb * PAIR_PITCH
    tiles = [(k, j) for k in range(N_BLK) for j in range(HALF)]

    @pl.when(step < W1_STEPS)
    def _load_weights():
        _cast_weight_chunks(step, ((wg_c, wg_ref), (wu_c, wu_ref), (wd_c, wd_ref), (win_c, win_ref)))

    @pl.when(step == 0)
    def _init():
        for ref in (state_ref, us_ref, l_ref, e_ref, ecarry_ref, v_ref, g_ref):
            ref[...] = jnp.zeros_like(ref)

    @pl.when(step >= W1_STEPS)
    def _main():
        carry = {}

        def build_lhs():
            for b in range(nb):
                for k in range(N_BLK):
                    for par in range(2):
                        col = (2 * k + par) * LANES
                        l_ref[pl.ds(b * PAIR_PITCH, N_PAIR), col:col + LANES] = (
                            us_ref[k, pl.ds(b * t_len + par, N_PAIR, stride=2), :])

        def b_proj(ks):
            for k in ks:
                lhs = l_ref[:, 2 * k * LANES:2 * (k + 1) * LANES].astype(jnp.bfloat16)
                v = jnp.dot(lhs, wb_ref[k], preferred_element_type=jnp.float32)
                for j in range(SLABS_PER_BLK):
                    xs_refs[k][j] = v[:, j * LANES:(j + 1) * LANES]

        def scan(r0, r1):
            if r0 == 0:
                carry["re"] = [state_ref[k * SLABS_PER_BLK + j] for k, j in tiles]
                carry["im"] = [state_ref[k * SLABS_PER_BLK + HALF + j] for k, j in tiles]
            xr, xi = carry["re"], carry["im"]
            for r in range(r0, r1):
                rows = pl.ds(r, SUBLANES, stride=PAIR_PITCH)
                for n, (k, j) in enumerate(tiles):
                    a_re = a2re_ref[k * HALF + j]
                    a_im = a2im_ref[k * HALF + j]
                    nr = (a_re * xr[n] + xs_refs[k][j, rows, :]) - a_im * xi[n]
                    ni = (a_re * xi[n] + xs_refs[k][HALF + j, rows, :]) + a_im * xr[n]
                    xs_refs[k][j, rows, :] = nr
                    xs_refs[k][HALF + j, rows, :] = ni
                    xr[n], xi[n] = nr, ni
            if r1 == N_PAIR:
                for n, (k, j) in enumerate(tiles):
                    state_ref[k * SLABS_PER_BLK + j] = xr[n]
                    state_ref[k * SLABS_PER_BLK + HALF + j] = xi[n]

        def c_proj(ks):
            for k in ks:
                lanes = slice(k * LANES, (k + 1) * LANES)
                x = jnp.concatenate([xs_refs[k][j] for j in range(SLABS_PER_BLK)], axis=1)
                ce = lax.dot_general(x.astype(jnp.bfloat16), wct_ref[k], _NT,
                                     preferred_element_type=jnp.float32)
                yo_ref[:, lanes] = ce[:, 0:LANES]
                e_ref[k, SUBLANES:SUBLANES + prow, :] = ce[:, LANES:]
                nxt = e_ref[k, pl.ds(SUBLANES + N_PAIR - 1, nb, stride=PAIR_PITCH), :]
                e_ref[k, pl.ds(SUBLANES - 1, nb, stride=PAIR_PITCH), :] = ecarry_ref[k]
                ecarry_ref[k] = nxt
                u_even = l_ref[:, 2 * k * LANES:(2 * k + 1) * LANES].astype(jnp.bfloat16)
                ye_ref[:, lanes] = (e_ref[k, SUBLANES - 1:SUBLANES - 1 + prow, :]
                                    + jnp.dot(u_even, wdm_ref[k], preferred_element_type=jnp.float32))

        def glu_emit():
            for b in range(nb):
                for k in range(N_BLK):
                    lanes = slice(k * LANES, (k + 1) * LANES)
                    for par, src in ((0, ye_ref), (1, yo_ref)):
                        ys_ref[k, pl.ds(b * t_len + par, N_PAIR, stride=2), :] = (
                            src[pl.ds(b * PAIR_PITCH, N_PAIR), lanes])
            y = jnp.concatenate([ys_ref[k] for k in range(N_BLK)], axis=1)
            u = jnp.concatenate([us_ref[k] for k in range(N_BLK)], axis=1)
            s_out = _glu_tail(y, u, d_ref, wglu_ref, bglu_ref)
            mix_ref[:, :, 0:SSM_WIDTH] = s_out.reshape(nb, t_len, SSM_WIDTH).astype(mix_ref.dtype)

        def conv():
            v = v_ref[:, SUBLANES:SUBLANES + t_len, :]
            z = cw_ref[0:1, :] * v_ref[:, SUBLANES - 2:SUBLANES - 2 + t_len, :]
            z = z + cw_ref[1:2, :] * v_ref[:, SUBLANES - 1:SUBLANES - 1 + t_len, :]
            z = z + cw_ref[2:3, :] * v
            mix_ref[:, :, SSM_WIDTH:] = (g_ref[...] * z).astype(mix_ref.dtype)
            v_ref[:, 0:SUBLANES, :] = v_ref[:, t_len:t_len + SUBLANES, :]

        q = N_PAIR // 4
        pieces = [
            lambda: (conv(), build_lhs(), b_proj((0,))), lambda: b_proj((1,)), lambda: b_proj((2,)),
            lambda: b_proj((3,)),
            lambda: scan(0, q), lambda: scan(q, 2 * q), lambda: scan(2 * q, 3 * q), lambda: scan(3 * q, N_PAIR),
            lambda: c_proj((0, 1)), lambda: c_proj((2, 3)),
            glu_emit,
        ]
        assert len(pieces) <= D_FF // FFN_TF

        db = n_dec // t_len
        x_top = jnp.where(step == last, xd_ref[...], x_ref[0:db].reshape(n_dec, D_MODEL))
        x = jnp.concatenate([x_top, x_ref[db:].reshape((nb - db) * t_len, D_MODEL)], axis=0)
        xn = _rms(x, n1_ref[...]).astype(jnp.bfloat16)
        for c, j in enumerate(range(0, D_FF, FFN_TF)):
            _swiglu_stage(xn, wg_ref, wu_ref, a_ref, j, j + FFN_TF)
            if c < len(pieces):
                pieces[c]()
        h = x + 0.5 * jnp.dot(a_ref[...], wd_ref[...], preferred_element_type=jnp.float32)
        hn = _rms(h, nm_ref[...]).astype(jnp.bfloat16)
        p = jnp.dot(hn, win_ref[...], preferred_element_type=jnp.float32)
        h_ref[...] = h.reshape(nb, t_len, D_MODEL)
        v_new = p[:, SSM_WIDTH + 2 * CONV_WIDTH:] * p[:, SSM_WIDTH:SSM_WIDTH + CONV_WIDTH]
        v_ref[:, SUBLANES:SUBLANES + t_len, :] = v_new.reshape(nb, t_len, CONV_WIDTH)
        g_ref[...] = p[:, SSM_WIDTH + CONV_WIDTH:SSM_WIDTH + 2 * CONV_WIDTH].reshape(nb, t_len, CONV_WIDTH)
        for k in range(N_BLK):
            us_ref[k] = p[:, k * LANES:(k + 1) * LANES]

    @pl.when(step == last)
    def _final():
        for n, (k, j) in enumerate(tiles):
            for out_ref, slab in ((sre_ref, k * SLABS_PER_BLK + j), (sim_ref, k * SLABS_PER_BLK + HALF + j)):
                tile = state_ref[slab]
                for half in range(LANES // SSM_STATE):
                    group = (k * BLK_STATE + j * LANES) // SSM_STATE + half
                    out_ref[:, group, :] = tile[:, half * SSM_STATE:(half + 1) * SSM_STATE]
        nconv_ref[...] = v_ref[:, SUBLANES - 2:SUBLANES, :]
        for k in range(N_BLK):
            ud_ref[:, k * LANES:(k + 1) * LANES] = us_ref[k, 0:n_dec, :]
        for b in range(n_dec // t_len):
            vd_ref[b * t_len:(b + 1) * t_len, :] = v_ref[b, SUBLANES:SUBLANES + t_len, :]
            gd_ref[b * t_len:(b + 1) * t_len, :] = g_ref[b]


def _ffn1_mixer(x, xd, n1, wg, wu, wd, nm, win, a2_re, a2_im, wb, wct, wdm, d, wglu, bglu, cw):
    nb, seq, _ = x.shape
    n_dec = xd.shape[0]
    n_chunk = seq // SCAN_T
    prow = nb * PAIR_PITCH
    f32, bf = jnp.float32, jnp.bfloat16
    main = lambda i: jnp.maximum(i - W1_STEPS, 0)
    return pl.pallas_call(
        _ffn1_mixer_kernel,
        grid=(W1_STEPS + n_chunk + 1,),
        in_specs=[
            pl.BlockSpec((nb, SCAN_T, D_MODEL), lambda i: (0, jnp.minimum(main(i), n_chunk - 1), 0)),
            _whole((n_dec, D_MODEL)),
            _whole((1, D_MODEL)),
            _weight_chunk((D_MODEL // W1_STEPS, D_FF), W1_STEPS),
            _weight_chunk((D_MODEL // W1_STEPS, D_FF), W1_STEPS),
            _weight_chunk((D_FF // W1_STEPS, D_MODEL), W1_STEPS),
            _whole((1, D_MODEL)),
            _weight_chunk((D_MODEL // W1_STEPS, IN_PROJ_WIDTH), W1_STEPS),
            _whole((N_TILE, SUBLANES, LANES)),
            _whole((N_TILE, SUBLANES, LANES)),
            _whole((N_BLK, 2 * LANES, 2 * BLK_STATE)),
            _whole((N_BLK, 2 * LANES, 2 * BLK_STATE)),
            _whole((N_BLK, LANES, LANES)),
            _whole((1, SSM_WIDTH)),
            _whole((SSM_WIDTH, SSM_WIDTH)),
            _whole((1, SSM_WIDTH)),
            _whole((CONV_K, CONV_WIDTH)),
        ],
        out_specs=[
            pl.BlockSpec((nb, SCAN_T, D_MODEL), lambda i: (0, main(i), 0)),
            pl.BlockSpec((nb, SCAN_T, D_MODEL), lambda i: (0, jnp.clip(main(i) - 1, 0, n_chunk - 1), 0)),
            pl.BlockSpec((nb, N_SSM_GROUPS, SSM_STATE), lambda i: (0, 0, 0)),
            pl.BlockSpec((nb, N_SSM_GROUPS, SSM_STATE), lambda i: (0, 0, 0)),
            pl.BlockSpec((nb, CONV_K - 1, CONV_WIDTH), lambda i: (0, 0, 0)),
            pl.BlockSpec((n_dec, SSM_WIDTH), lambda i: (0, 0)),
            pl.BlockSpec((n_dec, CONV_WIDTH), lambda i: (0, 0)),
            pl.BlockSpec((n_dec, CONV_WIDTH), lambda i: (0, 0)),
        ],
        out_shape=[
            jax.ShapeDtypeStruct((nb, seq + SCAN_T, D_MODEL), f32),
            jax.ShapeDtypeStruct((nb, seq, D_MODEL), bf),
            jax.ShapeDtypeStruct((nb, N_SSM_GROUPS, SSM_STATE), f32),
            jax.ShapeDtypeStruct((nb, N_SSM_GROUPS, SSM_STATE), f32),
            jax.ShapeDtypeStruct((nb, CONV_K - 1, CONV_WIDTH), f32),
            jax.ShapeDtypeStruct((n_dec, SSM_WIDTH), f32),
            jax.ShapeDtypeStruct((n_dec, CONV_WIDTH), f32),
            jax.ShapeDtypeStruct((n_dec, CONV_WIDTH), f32),
        ],
        scratch_shapes=[
            *[pltpu.VMEM((SLABS_PER_BLK, prow, LANES), f32) for _ in range(N_BLK)],
            pltpu.VMEM((N_BLK, nb * SCAN_T, LANES), f32),
            pltpu.VMEM((N_BLK, nb * SCAN_T, LANES), f32),
            pltpu.VMEM((prow, 2 * SSM_WIDTH), f32),
            pltpu.VMEM((prow, SSM_WIDTH), f32),
            pltpu.VMEM((prow, SSM_WIDTH), f32),
            pltpu.VMEM((N_BLK, SUBLANES + prow, LANES), f32),
            pltpu.VMEM((N_BLK, SUBLANES, LANES), f32),
            pltpu.VMEM((nb, SCAN_T + SUBLANES, CONV_WIDTH), f32),
            pltpu.VMEM((nb, SCAN_T, CONV_WIDTH), f32),
            pltpu.VMEM((2 * N_TILE, SUBLANES, LANES), f32),
            pltpu.VMEM((D_MODEL, D_FF), bf),
            pltpu.VMEM((D_MODEL, D_FF), bf),
            pltpu.VMEM((D_FF, D_MODEL), bf),
            pltpu.VMEM((D_MODEL, IN_PROJ_WIDTH), bf),
            pltpu.VMEM((nb * SCAN_T, D_FF), bf),
        ],
        compiler_params=pltpu.CompilerParams(
            dimension_semantics=("arbitrary",), vmem_limit_bytes=VMEM_LIMIT),
        name="ffn1_mixer",
    )(x, xd, n1, wg, wu, wd, nm, win, a2_re, a2_im, wb, wct, wdm, d, wglu, bglu, cw)


def _ffn2_rows(h, mix, weights, a_ref):
    wout_ref, n2_ref, wg_ref, wu_ref, wd_ref, nf_ref = weights
    h = h + jnp.dot(mix, wout_ref[...], preferred_element_type=jnp.float32)
    hn = _rms(h, n2_ref[...]).astype(jnp.bfloat16)
    h = h + 0.5 * _swiglu_staged(hn, wg_ref, wu_ref, wd_ref, a_ref)
    return _rms(h, nf_ref[...])


def _ffn2_kernel(h_ref, mix_ref, hd0_ref, hd1_ref, mixd_ref, wout_c, wg_c, wu_c, wd_c, n2_ref, nf_ref,
                 y_ref, yd_ref, wout_ref, wg_ref, wu_ref, wd_ref, a0_ref, a1_ref):
    step = pl.program_id(0)
    last = pl.num_programs(0) - 1
    weights = (wout_ref, n2_ref, wg_ref, wu_ref, wd_ref, nf_ref)

    @pl.when(step < W2_STEPS)
    def _load_weights():
        _cast_weight_chunks(step, ((wout_c, wout_ref), (wg_c, wg_ref), (wu_c, wu_ref), (wd_c, wd_ref)))

    @pl.when((step >= W2_STEPS) & (step < last))
    def _prompt_tile():
        assert h_ref.shape[0] == 2 * FFN_SUB
        ra, rb = slice(0, FFN_SUB), slice(FFN_SUB, 2 * FFN_SUB)
        h_a = h_ref[ra, :] + jnp.dot(mix_ref[ra, :], wout_ref[...], preferred_element_type=jnp.float32)
        h_b = h_ref[rb, :] + jnp.dot(mix_ref[rb, :], wout_ref[...], preferred_element_type=jnp.float32)
        hn_a = _rms(h_a, n2_ref[...]).astype(jnp.bfloat16)
        _swiglu_stage(hn_a, wg_ref, wu_ref, a0_ref, 0, FFN_TF)
        hn_b = _rms(h_b, n2_ref[...]).astype(jnp.bfloat16)
        _swiglu_stage(hn_a, wg_ref, wu_ref, a0_ref, FFN_TF, D_FF)
        f_a = jnp.dot(a0_ref[...], wd_ref[...], preferred_element_type=jnp.float32)
        _swiglu_stage(hn_b, wg_ref, wu_ref, a1_ref, 0, FFN_TF)
        y_ref[ra, :] = _rms(h_a + 0.5 * f_a, nf_ref[...])
        _swiglu_stage(hn_b, wg_ref, wu_ref, a1_ref, FFN_TF, D_FF)
        f_b = jnp.dot(a1_ref[...], wd_ref[...], preferred_element_type=jnp.float32)
        y_ref[rb, :] = _rms(h_b + 0.5 * f_b, nf_ref[...])

    @pl.when(step == last)
    def _decode_rows():
        h_d = jnp.concatenate([hd0_ref[...], hd1_ref[...]], axis=0)
        yd_ref[...] = _ffn2_rows(h_d, mixd_ref[...], weights, a0_ref)


def _ffn2(h_all, mix, mixd, wout, n2, wg, wu, wd, nf):
    nb, seq, _ = mix.shape
    seq_all = h_all.shape[1]
    n_dec = mixd.shape[0]
    tail = seq_all - seq
    assert n_dec == 2 * tail
    per_b = seq // FFN_TM
    n_tiles = nb * per_b
    h_all = h_all.reshape(nb * seq_all, D_MODEL)
    mix = mix.reshape(nb * seq, D_MODEL)
    idx = lambda i: jnp.clip(i - W2_STEPS, 0, n_tiles - 1)
    tile = lambda i: (idx(i), 0)
    assert seq_all % SUBLANES == 0
    tile_all = lambda i: (pl.multiple_of((idx(i) // per_b) * seq_all + (idx(i) % per_b) * FFN_TM, SUBLANES), 0)
    yp, yd = pl.pallas_call(
        _ffn2_kernel,
        grid=(W2_STEPS + n_tiles + 1,),
        in_specs=[
            pl.BlockSpec((pl.Element(FFN_TM), pl.Element(D_MODEL)), tile_all),
            pl.BlockSpec((FFN_TM, D_MODEL), tile),
            pl.BlockSpec((pl.Element(tail), pl.Element(D_MODEL)), lambda i: (seq, 0),
                         pipeline_mode=pl.Buffered(1)),
            pl.BlockSpec((pl.Element(tail), pl.Element(D_MODEL)), lambda i: (seq_all + seq, 0),
                         pipeline_mode=pl.Buffered(1)),
            _whole(mixd.shape),
            _weight_chunk((D_MODEL // W2_STEPS, D_MODEL), W2_STEPS),
            _weight_chunk((D_MODEL // W2_STEPS, D_FF), W2_STEPS),
            _weight_chunk((D_MODEL // W2_STEPS, D_FF), W2_STEPS),
            _weight_chunk((D_FF // W2_STEPS, D_MODEL), W2_STEPS),
            _whole((1, D_MODEL)),
            _whole((1, D_MODEL)),
        ],
        out_specs=[
            pl.BlockSpec((FFN_TM, D_MODEL), tile),
            pl.BlockSpec((n_dec, D_MODEL), lambda i: (0, 0)),
        ],
        out_shape=[
            jax.ShapeDtypeStruct((nb * seq, D_MODEL), jnp.float32),
            jax.ShapeDtypeStruct((n_dec, D_MODEL), jnp.float32),
        ],
        scratch_shapes=[
            pltpu.VMEM((D_MODEL, D_MODEL), jnp.bfloat16),
            pltpu.VMEM((D_MODEL, D_FF), jnp.bfloat16),
            pltpu.VMEM((D_MODEL, D_FF), jnp.bfloat16),
            pltpu.VMEM((D_FF, D_MODEL), jnp.bfloat16),
            pltpu.VMEM((FFN_SUB, D_FF), jnp.bfloat16),
            pltpu.VMEM((FFN_SUB, D_FF), jnp.bfloat16),
        ],
        compiler_params=pltpu.CompilerParams(
            dimension_semantics=("arbitrary",), vmem_limit_bytes=VMEM_LIMIT),
        name="outproj_ffn2",
    )(h_all, mix, h_all, h_all, mixd, wout, wg, wu, wd, n2, nf)
    return yp.reshape(nb, seq, D_MODEL), yd


def _mixer_step_kernel(u_ref, v_ref, g_ref, hre_ref, him_ref, buf_ref, are_ref, aim_ref, bb_ref, wct_ref,
                       d_ref, wglu_ref, bglu_ref, cw_ref, mix_ref, sre_ref, sim_ref, nconv_ref):
    u = u_ref[...]
    u_bf = u.astype(jnp.bfloat16)
    ys = []
    for k in range(N_BLK):
        bu = jnp.dot(u_bf[:, k * LANES:(k + 1) * LANES], bb_ref[k], preferred_element_type=jnp.float32)
        st = slice(k * BLK_STATE, (k + 1) * BLK_STATE)
        a_re = are_ref[k:k + 1, :]
        a_im = aim_ref[k:k + 1, :]
        h_re = hre_ref[:, st]
        h_im = him_ref[:, st]
        x_re = (a_re * h_re + bu[:, :BLK_STATE]) - a_im * h_im
        x_im = (a_re * h_im + bu[:, BLK_STATE:]) + a_im * h_re
        sre_ref[:, st] = x_re
        sim_ref[:, st] = x_im
        x = jnp.concatenate([x_re, x_im], axis=1).astype(jnp.bfloat16)
        ys.append(lax.dot_general(x, wct_ref[k, 0:LANES, :], _NT, preferred_element_type=jnp.float32))
    y = jnp.concatenate(ys, axis=1)
    mix_ref[:, 0:SSM_WIDTH] = _glu_tail(y, u, d_ref, wglu_ref, bglu_ref).astype(mix_ref.dtype)

    v = v_ref[...]
    z = cw_ref[0:1, :] * buf_ref[0] + cw_ref[1:2, :] * buf_ref[1] + cw_ref[2:3, :] * v
    mix_ref[:, SSM_WIDTH:] = (g_ref[...] * z).astype(mix_ref.dtype)
    nconv_ref[0] = buf_ref[1]
    nconv_ref[1] = v


def _mixer_step(u, v, g, h_re, h_im, buf, a_re, a_im, bb, wct, d, wglu, bglu, cw):
    nb = u.shape[0]
    return pl.pallas_call(
        _mixer_step_kernel,
        out_shape=[
            jax.ShapeDtypeStruct((nb, D_MODEL), jnp.bfloat16),
            jax.ShapeDtypeStruct((nb, N_STATE), jnp.float32),
            jax.ShapeDtypeStruct((nb, N_STATE), jnp.float32),
            jax.ShapeDtypeStruct((CONV_K - 1, nb, CONV_WIDTH), jnp.float32),
        ],
        compiler_params=pltpu.CompilerParams(vmem_limit_bytes=VMEM_LIMIT),
        name="mixer_step",
    )(u, v, g, h_re, h_im, buf, a_re, a_im, bb, wct, d, wglu, bglu, cw)


def _ssm_layout(lam_re, lam_im, log_dt, b_re, b_im, c_re, c_im):
    gpb = N_SSM_GROUPS // N_BLK
    lr = lam_re.reshape(N_BLK, BLK_STATE)
    li = lam_im.reshape(N_BLK, BLK_STATE)
    ldt = jnp.broadcast_to(log_dt[:, None], (N_SSM_GROUPS, SSM_STATE)).reshape(N_BLK, BLK_STATE)

    def b_rows(b):
        return b.reshape(N_BLK, gpb, SSM_STATE, SSM_GROUP).transpose(0, 1, 3, 2).reshape(N_BLK, LANES, SSM_STATE)

    def c_rows(c):
        return c.reshape(N_BLK, LANES, SSM_STATE)

    return lr, li, ldt, b_rows(b_re), b_rows(b_im), c_rows(c_re), c_rows(c_im)


def kernel(x_prompt, x_sample, state_ssm_re, state_ssm_im, state_conv, ffn1_norm, ffn1_w_gate, ffn1_w_up, ffn1_w_down, mix_norm, w_in, ssm_lambda_re, ssm_lambda_im, ssm_log_dt, ssm_b_re, ssm_b_im, ssm_c_re, ssm_c_im, ssm_d, ssm_w_glu, ssm_b_glu, conv_w, w_out, ffn2_norm, ffn2_w_gate, ffn2_w_up, ffn2_w_down, final_norm):
    assert ffn1_norm.shape[0] == 1, "single-layer trunk"
    n_prompt, seq, _ = x_prompt.shape
    n_dec, dec_seq, _ = x_sample.shape
    assert dec_seq == 1 and n_prompt == SUBLANES and seq % FFN_TM == 0 and seq % SCAN_T == 0
    assert n_dec % SCAN_T == 0 and n_dec <= n_prompt * SCAN_T and FFN_TM % FFN_SUB == 0

    n1 = ffn1_norm[0][None, :]
    nm = mix_norm[0][None, :]
    n2 = ffn2_norm[0][None, :]
    nf = final_norm[None, :]
    wglu = ssm_w_glu[0].astype(jnp.bfloat16)
    layout = _ssm_layout(ssm_lambda_re[0], ssm_lambda_im[0], ssm_log_dt[0], ssm_b_re[0], ssm_b_im[0],
                         ssm_c_re[0], ssm_c_im[0])
    a_re, a_im, a2_re, a2_im, bb, wb, wct, wdm = _discretize(*layout)
    d = ssm_d[0][None, :]
    bglu = ssm_b_glu[0][None, :]
    cw = conv_w[0]

    h_all, mixp, p_re, p_im, p_conv, ud, vd, gd = _ffn1_mixer(
        x_prompt, x_sample.reshape(n_dec, D_MODEL), n1, ffn1_w_gate[0], ffn1_w_up[0], ffn1_w_down[0], nm,
        w_in[0], a2_re, a2_im, wb, wct, wdm, d, wglu, bglu, cw)

    s_re0 = state_ssm_re[0].reshape(n_dec, N_STATE)
    s_im0 = state_ssm_im[0].reshape(n_dec, N_STATE)
    buf = state_conv[0].transpose(1, 0, 2)
    mixd, s_re, s_im, s_conv = _mixer_step(ud, vd, gd, s_re0, s_im0, buf, a_re, a_im, bb, wct,
                                           d, wglu, bglu, cw)

    yp, yd = _ffn2(h_all, mixp, mixd, w_out[0], n2, ffn2_w_gate[0], ffn2_w_up[0], ffn2_w_down[0], nf)

    gs = (N_SSM_GROUPS, SSM_STATE)
    return (yp,
            yd.reshape(n_dec, 1, D_MODEL),
            p_re[None],
            p_im[None],
            p_conv[None],
            s_re.reshape(1, n_dec, *gs),
            s_im.reshape(1, n_dec, *gs),
            s_conv.transpose(1, 0, 2)[None])
```

```python
import jax
import jax.numpy as jnp
from jax import lax
from jax.experimental import pallas as pl
from jax.experimental.pallas import tpu as pltpu

D_MODEL = 1024
D_FF = 2816
SSM_WIDTH = 512
CONV_WIDTH = 512
SSM_GROUP = 16
N_SSM_GROUPS = 32
SSM_STATE = 64
CONV_K = 3
IN_PROJ_WIDTH = SSM_WIDTH + 3 * CONV_WIDTH
EPS = 1e-6

LANES = 128
SUBLANES = 8
N_STATE = N_SSM_GROUPS * SSM_STATE
N_BLK = SSM_WIDTH // LANES
BLK_STATE = N_STATE // N_BLK
SLABS_PER_BLK = 2 * BLK_STATE // LANES
HALF = SLABS_PER_BLK // 2
N_TILE = N_BLK * HALF

FFN_TM = 1024
FFN_SUB = 512
W1_STEPS = 16
W2_STEPS = 8
FFN_TF = 256
SCAN_T = 64
N_PAIR = SCAN_T // 2
PAIR_PITCH = N_PAIR + 4
VMEM_LIMIT = 62 * 1024 * 1024

_NT = (((1,), (1,)), ((), ()))

NORM_FFN1, NORM_MIX, NORM_FFN2, NORM_FINAL = range(4)
VEC_D, VEC_BGLU, VEC_CW = 0, 1, 2


def _rms(x, g):
    r = lax.rsqrt(jnp.mean(x * x, axis=-1, keepdims=True) + EPS)
    return x * r * g


def _bdot(a, b):
    return jnp.dot(a.astype(jnp.bfloat16), b.astype(jnp.bfloat16), preferred_element_type=jnp.float32)


def _swiglu_stage(xn, wg_ref, wu_ref, a_ref, lo, hi):
    rows = xn.shape[0]
    for j in range(lo, hi, FFN_TF):
        g = jnp.dot(xn, wg_ref[:, j:j + FFN_TF], preferred_element_type=jnp.float32)
        u = jnp.dot(xn, wu_ref[:, j:j + FFN_TF], preferred_element_type=jnp.float32)
        a_ref[0:rows, j:j + FFN_TF] = ((g * jax.nn.sigmoid(g)) * u).astype(a_ref.dtype)


def _swiglu_staged(xn, wg_ref, wu_ref, wd_ref, a_ref):
    rows = xn.shape[0]
    _swiglu_stage(xn, wg_ref, wu_ref, a_ref, 0, D_FF)
    return jnp.dot(a_ref[0:rows, :], wd_ref[...], preferred_element_type=jnp.float32)


def _cast_weight_chunks(step, pairs):
    for src, dst in pairs:
        n = src.shape[0]
        dst[pl.ds(pl.multiple_of(step * n, n), n), :] = src[...].astype(dst.dtype)


def _whole(shape):
    return pl.BlockSpec(shape, lambda i: (0,) * len(shape), pipeline_mode=pl.Buffered(1))


def _weight_chunk(shape, n_steps):
    return pl.BlockSpec(shape, lambda i: (jnp.minimum(i, n_steps - 1), 0))


def _ssm_vecs(vecs_ref):
    return (vecs_ref.at[VEC_D:VEC_D + 1], vecs_ref.at[VEC_BGLU:VEC_BGLU + 1],
            vecs_ref.at[VEC_CW:VEC_CW + CONV_K])


def _glu_tail(y, u, d_ref, wglu_ref, bglu_ref):
    y = y + d_ref[...] * u
    z = jax.nn.gelu(y)
    gate = jax.nn.sigmoid(_bdot(z, wglu_ref[...]) + bglu_ref[...])
    return z * gate


def _discretize_kernel(lr_ref, li_ref, ldt_ref, bre_ref, bim_ref, ctre_ref, ctim_ref,
                       are_ref, aim_ref, a2re_ref, a2im_ref, bb_ref, wb_ref, wct_ref, wdm_ref):
    row_group = lax.broadcasted_iota(jnp.int32, (LANES, BLK_STATE), 0) // SSM_GROUP
    col_group = lax.broadcasted_iota(jnp.int32, (LANES, BLK_STATE), 1) // SSM_STATE
    on_diagonal = row_group == col_group

    def blocks(m):
        pair = jnp.concatenate([m, m], axis=1)
        return jnp.where(on_diagonal, jnp.concatenate([pair] * (BLK_STATE // LANES), axis=1), 0.0)

    lr, li = lr_ref[...], li_ref[...]
    dt = jnp.exp(ldt_ref[...])
    mag = jnp.exp(lr * dt)
    ab_re = mag * jnp.cos(li * dt)
    ab_im = mag * jnp.sin(li * dt)
    den = lr * lr + li * li
    nr = ab_re - 1.0
    ni = ab_im
    coef_re = (nr * lr + ni * li) / den
    coef_im = (ni * lr - nr * li) / den
    are_ref[...] = ab_re
    aim_ref[...] = ab_im
    a2_re = ab_re * ab_re - ab_im * ab_im
    a2_im = 2.0 * (ab_re * ab_im)
    bf = jnp.bfloat16
    for k in range(N_BLK):
        ar, ai = ab_re[k:k + 1, :], ab_im[k:k + 1, :]
        for j in range(HALF):
            lanes = slice(j * LANES, (j + 1) * LANES)
            a2re_ref[k * HALF + j] = jnp.broadcast_to(a2_re[k:k + 1, lanes], (SUBLANES, LANES))
            a2im_ref[k * HALF + j] = jnp.broadcast_to(a2_im[k:k + 1, lanes], (SUBLANES, LANES))
        cr, ci = coef_re[k:k + 1, :], coef_im[k:k + 1, :]
        bre, bim = blocks(bre_ref[k]), blocks(bim_ref[k])
        bb_re = cr * bre - ci * bim
        bb_im = cr * bim + ci * bre
        bb_ref[k, :, 0:BLK_STATE] = bb_re.astype(bf)
        bb_ref[k, :, BLK_STATE:] = bb_im.astype(bf)
        wb_ref[k, 0:LANES, 0:BLK_STATE] = (ar * bb_re - ai * bb_im).astype(bf)
        wb_ref[k, 0:LANES, BLK_STATE:] = (ar * bb_im + ai * bb_re).astype(bf)
        wb_ref[k, LANES:, 0:BLK_STATE] = bb_re.astype(bf)
        wb_ref[k, LANES:, BLK_STATE:] = bb_im.astype(bf)
        ctre, ctim = blocks(ctre_ref[k]), blocks(ctim_ref[k])
        wct_ref[k, 0:LANES, 0:BLK_STATE] = ctre.astype(bf)
        wct_ref[k, 0:LANES, BLK_STATE:] = (-ctim).astype(bf)
        wct_ref[k, LANES:, 0:BLK_STATE] = (ctre * ar - ctim * ai).astype(bf)
        wct_ref[k, LANES:, BLK_STATE:] = (-(ctre * ai + ctim * ar)).astype(bf)
        wdm = (lax.dot_general(bb_re, ctre, _NT, precision=lax.Precision.HIGHEST,
                               preferred_element_type=jnp.float32)
               - lax.dot_general(bb_im, ctim, _NT, precision=lax.Precision.HIGHEST,
                                 preferred_element_type=jnp.float32))
        wdm_ref[k] = wdm.astype(bf)


def _discretize(lr, li, ldt, bre, bim, ctre, ctim):
    f32, bf = jnp.float32, jnp.bfloat16
    return pl.pallas_call(
        _discretize_kernel,
        out_shape=[
            jax.ShapeDtypeStruct((N_BLK, BLK_STATE), f32),
            jax.ShapeDtypeStruct((N_BLK, BLK_STATE), f32),
            jax.ShapeDtypeStruct((N_TILE, SUBLANES, LANES), f32),
            jax.ShapeDtypeStruct((N_TILE, SUBLANES, LANES), f32),
            jax.ShapeDtypeStruct((N_BLK, LANES, 2 * BLK_STATE), bf),
            jax.ShapeDtypeStruct((N_BLK, 2 * LANES, 2 * BLK_STATE), bf),
            jax.ShapeDtypeStruct((N_BLK, 2 * LANES, 2 * BLK_STATE), bf),
            jax.ShapeDtypeStruct((N_BLK, LANES, LANES), bf),
        ],
        name="ssm_discretize",
    )(lr, li, ldt, bre, bim, ctre, ctim)


def _ffn1_mixer_kernel(x_ref, xd_ref, norms_ref, wg_c, wu_c, wd_c, win_c,
                       a2re_ref, a2im_ref, wb_ref, wct_ref, wdm_ref, vecs_ref, wglu_ref,
                       h_ref, mix_ref, sre_ref, sim_ref, nconv_ref, ud_ref, vd_ref, gd_ref,
                       xs0_ref, xs1_ref, xs2_ref, xs3_ref, us_ref, ys_ref, l_ref, yo_ref, ye_ref, e_ref,
                       ecarry_ref, v_ref, g_ref, state_ref, wg_ref, wu_ref, wd_ref, win_ref, a_ref):
    xs_refs = (xs0_ref, xs1_ref, xs2_ref, xs3_ref)
    n1_ref, nm_ref = norms_ref.at[NORM_FFN1:NORM_FFN1 + 1], norms_ref.at[NORM_MIX:NORM_MIX + 1]
    d_ref, bglu_ref, cw_ref = _ssm_vecs(vecs_ref)
    step = pl.program_id(0)
    last = pl.num_programs(0) - 1
    nb, t_len, _ = x_ref.shape
    n_dec = xd_ref.shape[0]
    prow = nb * PAIR_PITCH
    tiles = [(k, j) for k in range(N_BLK) for j in range(HALF)]

    @pl.when(step < W1_STEPS)
    def _load_weights():
        _cast_weight_chunks(step, ((wg_c, wg_ref), (wu_c, wu_ref), (wd_c, wd_ref), (win_c, win_ref)))

    @pl.when(step == 0)
    def _init():
        for ref in (state_ref, us_ref, l_ref, e_ref, ecarry_ref, v_ref, g_ref):
            ref[...] = jnp.zeros_like(ref)

    @pl.when(step >= W1_STEPS)
    def _main():
        carry = {}

        def build_lhs():
            for b in range(nb):
                for k in range(N_BLK):
                    for par in range(2):
                        col = (2 * k + par) * LANES
                        l_ref[pl.ds(b * PAIR_PITCH, N_PAIR), col:col + LANES] = (
                            us_ref[k, pl.ds(b * t_len + par, N_PAIR, stride=2), :])

        def b_proj(ks):
            for k in ks:
                lhs = l_ref[:, 2 * k * LANES:2 * (k + 1) * LANES].astype(jnp.bfloat16)
                v = jnp.dot(lhs, wb_ref[k], preferred_element_type=jnp.float32)
                for j in range(SLABS_PER_BLK):
                    xs_refs[k][j] = v[:, j * LANES:(j + 1) * LANES]

        def scan(r0, r1):
            if r0 == 0:
                carry["re"] = [state_ref[k * SLABS_PER_BLK + j] for k, j in tiles]
                carry["im"] = [state_ref[k * SLABS_PER_BLK + HALF + j] for k, j in tiles]
            xr, xi = carry["re"], carry["im"]
            for r in range(r0, r1):
                rows = pl.ds(r, SUBLANES, stride=PAIR_PITCH)
                for n, (k, j) in enumerate(tiles):
                    a_re = a2re_ref[k * HALF + j]
                    a_im = a2im_ref[k * HALF + j]
                    nr = (a_re * xr[n] + xs_refs[k][j, rows, :]) - a_im * xi[n]
                    ni = (a_re * xi[n] + xs_refs[k][HALF + j, rows, :]) + a_im * xr[n]
                    xs_refs[k][j, rows, :] = nr
                    xs_refs[k][HALF + j, rows, :] = ni
                    xr[n], xi[n] = nr, ni
            if r1 == N_PAIR:
                for n, (k, j) in enumerate(tiles):
                    state_ref[k * SLABS_PER_BLK + j] = xr[n]
                    state_ref[k * SLABS_PER_BLK + HALF + j] = xi[n]

        def c_proj(ks):
            for k in ks:
                lanes = slice(k * LANES, (k + 1) * LANES)
                x = jnp.concatenate([xs_refs[k][j] for j in range(SLABS_PER_BLK)], axis=1)
                ce = lax.dot_general(x.astype(jnp.bfloat16), wct_ref[k], _NT,
                                     preferred_element_type=jnp.float32)
                yo_ref[:, lanes] = ce[:, 0:LANES]
                e_ref[k, SUBLANES:SUBLANES + prow, :] = ce[:, LANES:]
                nxt = e_ref[k, pl.ds(SUBLANES + N_PAIR - 1, nb, stride=PAIR_PITCH), :]
                e_ref[k, pl.ds(SUBLANES - 1, nb, stride=PAIR_PITCH), :] = ecarry_ref[k]
                ecarry_ref[k] = nxt
                u_even = l_ref[:, 2 * k * LANES:(2 * k + 1) * LANES].astype(jnp.bfloat16)
                ye_ref[:, lanes] = (e_ref[k, SUBLANES - 1:SUBLANES - 1 + prow, :]
                                    + jnp.dot(u_even, wdm_ref[k], preferred_element_type=jnp.float32))

        def glu_emit():
            for b in range(nb):
                for k in range(N_BLK):
                    lanes = slice(k * LANES, (k + 1) * LANES)
                    for par, src in ((0, ye_ref), (1, yo_ref)):
                        ys_ref[k, pl.ds(b * t_len + par, N_PAIR, stride=2), :] = (
                            src[pl.ds(b * PAIR_PITCH, N_PAIR), lanes])
            y = jnp.concatenate([ys_ref[k] for k in range(N_BLK)], axis=1)
            u = jnp.concatenate([us_ref[k] for k in range(N_BLK)], axis=1)
            s_out = _glu_tail(y, u, d_ref, wglu_ref, bglu_ref)
            mix_ref[:, :, 0:SSM_WIDTH] = s_out.reshape(nb, t_len, SSM_WIDTH).astype(mix_ref.dtype)

        def conv():
            v = v_ref[:, SUBLANES:SUBLANES + t_len, :]
            z = cw_ref[0:1, :] * v_ref[:, SUBLANES - 2:SUBLANES - 2 + t_len, :]
            z = z + cw_ref[1:2, :] * v_ref[:, SUBLANES - 1:SUBLANES - 1 + t_len, :]
            z = z + cw_ref[2:3, :] * v
            mix_ref[:, :, SSM_WIDTH:] = (g_ref[...] * z).astype(mix_ref.dtype)
            v_ref[:, 0:SUBLANES, :] = v_ref[:, t_len:t_len + SUBLANES, :]

        q = N_PAIR // 4
        pieces = [
            lambda: (conv(), build_lhs(), b_proj((0,))), lambda: b_proj((1,)), lambda: b_proj((2,)),
            lambda: b_proj((3,)),
            lambda: scan(0, q), lambda: scan(q, 2 * q), lambda: scan(2 * q, 3 * q), lambda: scan(3 * q, N_PAIR),
            lambda: c_proj((0, 1)), lambda: c_proj((2, 3)),
            glu_emit,
        ]
        assert len(pieces) <= D_FF // FFN_TF

        db = n_dec // t_len
        x_top = jnp.where(step == last, xd_ref[...], x_ref[0:db].reshape(n_dec, D_MODEL))
        x = jnp.concatenate([x_top, x_ref[db:].reshape((nb - db) * t_len, D_MODEL)], axis=0)
        xn = _rms(x, n1_ref[...]).astype(jnp.bfloat16)
        for c, j in enumerate(range(0, D_FF, FFN_TF)):
            _swiglu_stage(xn, wg_ref, wu_ref, a_ref, j, j + FFN_TF)
            if c < len(pieces):
                pieces[c]()
        h = x + 0.5 * jnp.dot(a_ref[...], wd_ref[...], preferred_element_type=jnp.float32)
        hn = _rms(h, nm_ref[...]).astype(jnp.bfloat16)
        p = jnp.dot(hn, win_ref[...], preferred_element_type=jnp.float32)
        h_ref[...] = h.reshape(nb, t_len, D_MODEL)
        v_new = p[:, SSM_WIDTH + 2 * CONV_WIDTH:] * p[:, SSM_WIDTH:SSM_WIDTH + CONV_WIDTH]
        v_ref[:, SUBLANES:SUBLANES + t_len, :] = v_new.reshape(nb, t_len, CONV_WIDTH)
        g_ref[...] = p[:, SSM_WIDTH + CONV_WIDTH:SSM_WIDTH + 2 * CONV_WIDTH].reshape(nb, t_len, CONV_WIDTH)
        for k in range(N_BLK):
            us_ref[k] = p[:, k * LANES:(k + 1) * LANES]

    @pl.when(step == last)
    def _final():
        for n, (k, j) in enumerate(tiles):
            for out_ref, slab in ((sre_ref, k * SLABS_PER_BLK + j), (sim_ref, k * SLABS_PER_BLK + HALF + j)):
                tile = state_ref[slab]
                for half in range(LANES // SSM_STATE):
                    group = (k * BLK_STATE + j * LANES) // SSM_STATE + half
                    out_ref[:, group, :] = tile[:, half * SSM_STATE:(half + 1) * SSM_STATE]
        nconv_ref[...] = v_ref[:, SUBLANES - 2:SUBLANES, :]
        for k in range(N_BLK):
            ud_ref[:, k * LANES:(k + 1) * LANES] = us_ref[k, 0:n_dec, :]
        for b in range(n_dec // t_len):
            vd_ref[b * t_len:(b + 1) * t_len, :] = v_ref[b, SUBLANES:SUBLANES + t_len, :]
            gd_ref[b * t_len:(b + 1) * t_len, :] = g_ref[b]


def _ffn1_mixer(x, xd, norms, wg, wu, wd, win, a2_re, a2_im, wb, wct, wdm, vecs, wglu):
    nb, seq, _ = x.shape
    n_dec = xd.shape[0]
    n_chunk = seq // SCAN_T
    prow = nb * PAIR_PITCH
    f32, bf = jnp.float32, jnp.bfloat16
    main = lambda i: jnp.maximum(i - W1_STEPS, 0)
    return pl.pallas_call(
        _ffn1_mixer_kernel,
        grid=(W1_STEPS + n_chunk + 1,),
        in_specs=[
            pl.BlockSpec((nb, SCAN_T, D_MODEL), lambda i: (0, jnp.minimum(main(i), n_chunk - 1), 0)),
            _whole((n_dec, D_MODEL)),
            _whole(norms.shape),
            _weight_chunk((D_MODEL // W1_STEPS, D_FF), W1_STEPS),
            _weight_chunk((D_MODEL // W1_STEPS, D_FF), W1_STEPS),
            _weight_chunk((D_FF // W1_STEPS, D_MODEL), W1_STEPS),
            _weight_chunk((D_MODEL // W1_STEPS, IN_PROJ_WIDTH), W1_STEPS),
            _whole((N_TILE, SUBLANES, LANES)),
            _whole((N_TILE, SUBLANES, LANES)),
            _whole((N_BLK, 2 * LANES, 2 * BLK_STATE)),
            _whole((N_BLK, 2 * LANES, 2 * BLK_STATE)),
            _whole((N_BLK, LANES, LANES)),
            _whole(vecs.shape),
            _whole((SSM_WIDTH, SSM_WIDTH)),
        ],
        out_specs=[
            pl.BlockSpec((nb, SCAN_T, D_MODEL), lambda i: (0, main(i), 0)),
            pl.BlockSpec((nb, SCAN_T, D_MODEL), lambda i: (0, jnp.clip(main(i) - 1, 0, n_chunk - 1), 0)),
            pl.BlockSpec((nb, N_SSM_GROUPS, SSM_STATE), lambda i: (0, 0, 0)),
            pl.BlockSpec((nb, N_SSM_GROUPS, SSM_STATE), lambda i: (0, 0, 0)),
            pl.BlockSpec((nb, CONV_K - 1, CONV_WIDTH), lambda i: (0, 0, 0)),
            pl.BlockSpec((n_dec, SSM_WIDTH), lambda i: (0, 0)),
            pl.BlockSpec((n_dec, CONV_WIDTH), lambda i: (0, 0)),
            pl.BlockSpec((n_dec, CONV_WIDTH), lambda i: (0, 0)),
        ],
        out_shape=[
            jax.ShapeDtypeStruct((nb, seq + SCAN_T, D_MODEL), f32),
            jax.ShapeDtypeStruct((nb, seq, D_MODEL), bf),
            jax.ShapeDtypeStruct((nb, N_SSM_GROUPS, SSM_STATE), f32),
            jax.ShapeDtypeStruct((nb, N_SSM_GROUPS, SSM_STATE), f32),
            jax.ShapeDtypeStruct((nb, CONV_K - 1, CONV_WIDTH), f32),
            jax.ShapeDtypeStruct((n_dec, SSM_WIDTH), f32),
            jax.ShapeDtypeStruct((n_dec, CONV_WIDTH), f32),
            jax.ShapeDtypeStruct((n_dec, CONV_WIDTH), f32),
        ],
        scratch_shapes=[
            *[pltpu.VMEM((SLABS_PER_BLK, prow, LANES), f32) for _ in range(N_BLK)],
            pltpu.VMEM((N_BLK, nb * SCAN_T, LANES), f32),
            pltpu.VMEM((N_BLK, nb * SCAN_T, LANES), f32),
            pltpu.VMEM((prow, 2 * SSM_WIDTH), f32),
            pltpu.VMEM((prow, SSM_WIDTH), f32),
            pltpu.VMEM((prow, SSM_WIDTH), f32),
            pltpu.VMEM((N_BLK, SUBLANES + prow, LANES), f32),
            pltpu.VMEM((N_BLK, SUBLANES, LANES), f32),
            pltpu.VMEM((nb, SCAN_T + SUBLANES, CONV_WIDTH), f32),
            pltpu.VMEM((nb, SCAN_T, CONV_WIDTH), f32),
            pltpu.VMEM((2 * N_TILE, SUBLANES, LANES), f32),
            pltpu.VMEM((D_MODEL, D_FF), bf),
            pltpu.VMEM((D_MODEL, D_FF), bf),
            pltpu.VMEM((D_FF, D_MODEL), bf),
            pltpu.VMEM((D_MODEL, IN_PROJ_WIDTH), bf),
            pltpu.VMEM((nb * SCAN_T, D_FF), bf),
        ],
        compiler_params=pltpu.CompilerParams(
            dimension_semantics=("arbitrary",), vmem_limit_bytes=VMEM_LIMIT),
        name="ffn1_mixer",
    )(x, xd, norms, wg, wu, wd, win, a2_re, a2_im, wb, wct, wdm, vecs, wglu)


def _ffn2_rows(h, mix, weights, a_ref):
    wout_ref, n2_ref, wg_ref, wu_ref, wd_ref, nf_ref = weights
    h = h + jnp.dot(mix, wout_ref[...], preferred_element_type=jnp.float32)
    hn = _rms(h, n2_ref[...]).astype(jnp.bfloat16)
    h = h + 0.5 * _swiglu_staged(hn, wg_ref, wu_ref, wd_ref, a_ref)
    return _rms(h, nf_ref[...])


def _ffn2_kernel(h_ref, mix_ref, hd0_ref, hd1_ref, mixd_ref, wout_c, wg_c, wu_c, wd_c, norms_ref,
                 y_ref, yd_ref, wout_ref, wg_ref, wu_ref, wd_ref, a0_ref, a1_ref):
    n2_ref, nf_ref = norms_ref.at[NORM_FFN2:NORM_FFN2 + 1], norms_ref.at[NORM_FINAL:NORM_FINAL + 1]
    step = pl.program_id(0)
    last = pl.num_programs(0) - 1
    weights = (wout_ref, n2_ref, wg_ref, wu_ref, wd_ref, nf_ref)

    @pl.when(step < W2_STEPS)
    def _load_weights():
        _cast_weight_chunks(step, ((wout_c, wout_ref), (wg_c, wg_ref), (wu_c, wu_ref), (wd_c, wd_ref)))

    @pl.when((step >= W2_STEPS) & (step < last))
    def _prompt_tile():
        assert h_ref.shape[0] == 2 * FFN_SUB
        ra, rb = slice(0, FFN_SUB), slice(FFN_SUB, 2 * FFN_SUB)
        h_a = h_ref[ra, :] + jnp.dot(mix_ref[ra, :], wout_ref[...], preferred_element_type=jnp.float32)
        h_b = h_ref[rb, :] + jnp.dot(mix_ref[rb, :], wout_ref[...], preferred_element_type=jnp.float32)
        hn_a = _rms(h_a, n2_ref[...]).astype(jnp.bfloat16)
        _swiglu_stage(hn_a, wg_ref, wu_ref, a0_ref, 0, FFN_TF)
        hn_b = _rms(h_b, n2_ref[...]).astype(jnp.bfloat16)
        _swiglu_stage(hn_a, wg_ref, wu_ref, a0_ref, FFN_TF, D_FF)
        f_a = jnp.dot(a0_ref[...], wd_ref[...], preferred_element_type=jnp.float32)
        _swiglu_stage(hn_b, wg_ref, wu_ref, a1_ref, 0, FFN_TF)
        y_ref[ra, :] = _rms(h_a + 0.5 * f_a, nf_ref[...])
        _swiglu_stage(hn_b, wg_ref, wu_ref, a1_ref, FFN_TF, D_FF)
        f_b = jnp.dot(a1_ref[...], wd_ref[...], preferred_element_type=jnp.float32)
        y_ref[rb, :] = _rms(h_b + 0.5 * f_b, nf_ref[...])

    @pl.when(step == last)
    def _decode_rows():
        h_d = jnp.concatenate([hd0_ref[...], hd1_ref[...]], axis=0)
        yd_ref[...] = _ffn2_rows(h_d, mixd_ref[...], weights, a0_ref)


def _ffn2(h_all, mix, mixd, wout, wg, wu, wd, norms):
    nb, seq, _ = mix.shape
    seq_all = h_all.shape[1]
    n_dec = mixd.shape[0]
    tail = seq_all - seq
    assert n_dec == 2 * tail
    per_b = seq // FFN_TM
    n_tiles = nb * per_b
    h_all = h_all.reshape(nb * seq_all, D_MODEL)
    mix = mix.reshape(nb * seq, D_MODEL)
    idx = lambda i: jnp.clip(i - W2_STEPS, 0, n_tiles - 1)
    tile = lambda i: (idx(i), 0)
    assert seq_all % SUBLANES == 0
    tile_all = lambda i: (pl.multiple_of((idx(i) // per_b) * seq_all + (idx(i) % per_b) * FFN_TM, SUBLANES), 0)
    yp, yd = pl.pallas_call(
        _ffn2_kernel,
        grid=(W2_STEPS + n_tiles + 1,),
        in_specs=[
            pl.BlockSpec((pl.Element(FFN_TM), pl.Element(D_MODEL)), tile_all),
            pl.BlockSpec((FFN_TM, D_MODEL), tile),
            pl.BlockSpec((pl.Element(tail), pl.Element(D_MODEL)), lambda i: (seq, 0),
                         pipeline_mode=pl.Buffered(1)),
            pl.BlockSpec((pl.Element(tail), pl.Element(D_MODEL)), lambda i: (seq_all + seq, 0),
                         pipeline_mode=pl.Buffered(1)),
            _whole(mixd.shape),
            _weight_chunk((D_MODEL // W2_STEPS, D_MODEL), W2_STEPS),
            _weight_chunk((D_MODEL // W2_STEPS, D_FF), W2_STEPS),
            _weight_chunk((D_MODEL // W2_STEPS, D_FF), W2_STEPS),
            _weight_chunk((D_FF // W2_STEPS, D_MODEL), W2_STEPS),
            _whole(norms.shape),
        ],
        out_specs=[
            pl.BlockSpec((FFN_TM, D_MODEL), tile),
            pl.BlockSpec((n_dec, D_MODEL), lambda i: (0, 0)),
        ],
        out_shape=[
            jax.ShapeDtypeStruct((nb * seq, D_MODEL), jnp.float32),
            jax.ShapeDtypeStruct((n_dec, D_MODEL), jnp.float32),
        ],
        scratch_shapes=[
            pltpu.VMEM((D_MODEL, D_MODEL), jnp.bfloat16),
            pltpu.VMEM((D_MODEL, D_FF), jnp.bfloat16),
            pltpu.VMEM((D_MODEL, D_FF), jnp.bfloat16),
            pltpu.VMEM((D_FF, D_MODEL), jnp.bfloat16),
            pltpu.VMEM((FFN_SUB, D_FF), jnp.bfloat16),
            pltpu.VMEM((FFN_SUB, D_FF), jnp.bfloat16),
        ],
        compiler_params=pltpu.CompilerParams(
            dimension_semantics=("arbitrary",), vmem_limit_bytes=VMEM_LIMIT),
        name="outproj_ffn2",
    )(h_all, mix, h_all, h_all, mixd, wout, wg, wu, wd, norms)
    return yp.reshape(nb, seq, D_MODEL), yd


def _mixer_step_kernel(u_ref, v_ref, g_ref, hre_ref, him_ref, buf_ref, are_ref, aim_ref, bb_ref, wct_ref,
                       vecs_ref, wglu_ref, mix_ref, sre_ref, sim_ref, nconv_ref):
    d_ref, bglu_ref, cw_ref = _ssm_vecs(vecs_ref)
    u = u_ref[...]
    u_bf = u.astype(jnp.bfloat16)
    ys = []
    for k in range(N_BLK):
        bu = jnp.dot(u_bf[:, k * LANES:(k + 1) * LANES], bb_ref[k], preferred_element_type=jnp.float32)
        st = slice(k * BLK_STATE, (k + 1) * BLK_STATE)
        a_re = are_ref[k:k + 1, :]
        a_im = aim_ref[k:k + 1, :]
        h_re = hre_ref[:, st]
        h_im = him_ref[:, st]
        x_re = (a_re * h_re + bu[:, :BLK_STATE]) - a_im * h_im
        x_im = (a_re * h_im + bu[:, BLK_STATE:]) + a_im * h_re
        sre_ref[:, st] = x_re
        sim_ref[:, st] = x_im
        x = jnp.concatenate([x_re, x_im], axis=1).astype(jnp.bfloat16)
        ys.append(lax.dot_general(x, wct_ref[k, 0:LANES, :], _NT, preferred_element_type=jnp.float32))
    y = jnp.concatenate(ys, axis=1)
    mix_ref[:, 0:SSM_WIDTH] = _glu_tail(y, u, d_ref, wglu_ref, bglu_ref).astype(mix_ref.dtype)

    v = v_ref[...]
    z = cw_ref[0:1, :] * buf_ref[0] + cw_ref[1:2, :] * buf_ref[1] + cw_ref[2:3, :] * v
    mix_ref[:, SSM_WIDTH:] = (g_ref[...] * z).astype(mix_ref.dtype)
    nconv_ref[0] = buf_ref[1]
    nconv_ref[1] = v


def _mixer_step(u, v, g, h_re, h_im, buf, a_re, a_im, bb, wct, vecs, wglu):
    nb = u.shape[0]
    return pl.pallas_call(
        _mixer_step_kernel,
        out_shape=[
            jax.ShapeDtypeStruct((nb, D_MODEL), jnp.bfloat16),
            jax.ShapeDtypeStruct((nb, N_STATE), jnp.float32),
            jax.ShapeDtypeStruct((nb, N_STATE), jnp.float32),
            jax.ShapeDtypeStruct((CONV_K - 1, nb, CONV_WIDTH), jnp.float32),
        ],
        compiler_params=pltpu.CompilerParams(vmem_limit_bytes=VMEM_LIMIT),
        name="mixer_step",
    )(u, v, g, h_re, h_im, buf, a_re, a_im, bb, wct, vecs, wglu)


def _ssm_layout(lam_re, lam_im, log_dt, b_re, b_im, c_re, c_im):
    gpb = N_SSM_GROUPS // N_BLK
    lr = lam_re.reshape(N_BLK, BLK_STATE)
    li = lam_im.reshape(N_BLK, BLK_STATE)
    ldt = jnp.broadcast_to(log_dt[:, None], (N_SSM_GROUPS, SSM_STATE)).reshape(N_BLK, BLK_STATE)

    def b_rows(b):
        return b.reshape(N_BLK, gpb, SSM_STATE, SSM_GROUP).transpose(0, 1, 3, 2).reshape(N_BLK, LANES, SSM_STATE)

    def c_rows(c):
        return c.reshape(N_BLK, LANES, SSM_STATE)

    return lr, li, ldt, b_rows(b_re), b_rows(b_im), c_rows(c_re), c_rows(c_im)


def kernel(x_prompt, x_sample, state_ssm_re, state_ssm_im, state_conv, ffn1_norm, ffn1_w_gate, ffn1_w_up, ffn1_w_down, mix_norm, w_in, ssm_lambda_re, ssm_lambda_im, ssm_log_dt, ssm_b_re, ssm_b_im, ssm_c_re, ssm_c_im, ssm_d, ssm_w_glu, ssm_b_glu, conv_w, w_out, ffn2_norm, ffn2_w_gate, ffn2_w_up, ffn2_w_down, final_norm):
    assert ffn1_norm.shape[0] == 1, "single-layer trunk"
    n_prompt, seq, _ = x_prompt.shape
    n_dec, dec_seq, _ = x_sample.shape
    assert dec_seq == 1 and n_prompt == SUBLANES and seq % FFN_TM == 0 and seq % SCAN_T == 0
    assert n_dec % SCAN_T == 0 and n_dec <= n_prompt * SCAN_T and FFN_TM % FFN_SUB == 0

    norms = jnp.concatenate([ffn1_norm, mix_norm, ffn2_norm, final_norm[None, :]], axis=0)
    vecs = jnp.concatenate([ssm_d, ssm_b_glu, conv_w[0],
                            jnp.zeros((SUBLANES - 2 - CONV_K, SSM_WIDTH), jnp.float32)], axis=0)
    wglu = ssm_w_glu[0].astype(jnp.bfloat16)
    layout = _ssm_layout(ssm_lambda_re[0], ssm_lambda_im[0], ssm_log_dt[0], ssm_b_re[0], ssm_b_im[0],
                         ssm_c_re[0], ssm_c_im[0])
    a_re, a_im, a2_re, a2_im, bb, wb, wct, wdm = _discretize(*layout)

    h_all, mixp, p_re, p_im, p_conv, ud, vd, gd = _ffn1_mixer(
        x_prompt, x_sample.reshape(n_dec, D_MODEL), norms, ffn1_w_gate[0], ffn1_w_up[0], ffn1_w_down[0],
        w_in[0], a2_re, a2_im, wb, wct, wdm, vecs, wglu)

    s_re0 = state_ssm_re[0].reshape(n_dec, N_STATE)
    s_im0 = state_ssm_im[0].reshape(n_dec, N_STATE)
    buf = state_conv[0].transpose(1, 0, 2)
    mixd, s_re, s_im, s_conv = _mixer_step(ud, vd, gd, s_re0, s_im0, buf, a_re, a_im, bb, wct, vecs, wglu)

    yp, yd = _ffn2(h_all, mixp, mixd, w_out[0], ffn2_w_gate[0], ffn2_w_up[0], ffn2_w_down[0], norms)

    gs = (N_SSM_GROUPS, SSM_STATE)
    return (yp,
            yd.reshape(n_dec, 1, D_MODEL),
            p_re[None],
            p_im[None],
            p_conv[None],
            s_re.reshape(1, n_dec, *gs),
            s_im.reshape(1, n_dec, *gs),
            s_conv.transpose(1, 0, 2)[None])
```

```python
import jax
import jax.numpy as jnp
from jax import lax
from jax.experimental import pallas as pl
from jax.experimental.pallas import tpu as pltpu

D_MODEL = 1024
D_FF = 2816
SSM_WIDTH = 512
CONV_WIDTH = 512
SSM_GROUP = 16
N_SSM_GROUPS = 32
SSM_STATE = 64
CONV_K = 3
IN_PROJ_WIDTH = SSM_WIDTH + 3 * CONV_WIDTH
EPS = 1e-6

LANES = 128
SUBLANES = 8
N_STATE = N_SSM_GROUPS * SSM_STATE
N_BLK = SSM_WIDTH // LANES
BLK_STATE = N_STATE // N_BLK
SLABS_PER_BLK = 2 * BLK_STATE // LANES
HALF = SLABS_PER_BLK // 2
N_TILE = N_BLK * HALF

FFN_TM = 1024
FFN_SUB = 512
W1_STEPS = 16
W2_STEPS = 8
FFN_TF = 256
SCAN_T = 64
N_PAIR = SCAN_T // 2
PAIR_PITCH = N_PAIR + 4
VMEM_LIMIT = 62 * 1024 * 1024

_NT = (((1,), (1,)), ((), ()))

NORM_FFN1, NORM_MIX, NORM_FFN2, NORM_FINAL = range(4)
VEC_D, VEC_BGLU, VEC_CW = 0, 1, 2


def _rms(x, g):
    r = lax.rsqrt(jnp.mean(x * x, axis=-1, keepdims=True) + EPS)
    return x * r * g


def _bdot(a, b):
    return jnp.dot(a.astype(jnp.bfloat16), b.astype(jnp.bfloat16), preferred_element_type=jnp.float32)


def _swiglu_stage(xn, wg_ref, wu_ref, a_ref, lo, hi):
    rows = xn.shape[0]
    for j in range(lo, hi, FFN_TF):
        g = jnp.dot(xn, wg_ref[:, j:j + FFN_TF], preferred_element_type=jnp.float32)
        u = jnp.dot(xn, wu_ref[:, j:j + FFN_TF], preferred_element_type=jnp.float32)
        a_ref[0:rows, j:j + FFN_TF] = ((g * jax.nn.sigmoid(g)) * u).astype(a_ref.dtype)


def _swiglu_staged(xn, wg_ref, wu_ref, wd_ref, a_ref):
    rows = xn.shape[0]
    _swiglu_stage(xn, wg_ref, wu_ref, a_ref, 0, D_FF)
    return jnp.dot(a_ref[0:rows, :], wd_ref[...], preferred_element_type=jnp.float32)


def _cast_weight_chunks(step, pairs):
    for src, dst in pairs:
        n = src.shape[0]
        dst[pl.ds(pl.multiple_of(step * n, n), n), :] = src[...].astype(dst.dtype)


def _whole(shape):
    return pl.BlockSpec(shape, lambda i: (0,) * len(shape), pipeline_mode=pl.Buffered(1))


def _weight_chunk(shape, n_steps):
    return pl.BlockSpec(shape, lambda i: (jnp.minimum(i, n_steps - 1), 0))


def _ssm_vecs(vecs_ref):
    return (vecs_ref.at[VEC_D:VEC_D + 1], vecs_ref.at[VEC_BGLU:VEC_BGLU + 1],
            vecs_ref.at[VEC_CW:VEC_CW + CONV_K])


def _glu_tail(y, u, d_ref, wglu_ref, bglu_ref):
    y = y + d_ref[...] * u
    z = jax.nn.gelu(y)
    gate = jax.nn.sigmoid(_bdot(z, wglu_ref[...]) + bglu_ref[...])
    return z * gate


def _discretize_kernel(lr_ref, li_ref, ldt_ref, bre_ref, bim_ref, ctre_ref, ctim_ref,
                       are_ref, aim_ref, a2re_ref, a2im_ref, bb_ref, wb_ref, wct_ref, wdm_ref):
    row_group = lax.broadcasted_iota(jnp.int32, (LANES, BLK_STATE), 0) // SSM_GROUP
    col_group = lax.broadcasted_iota(jnp.int32, (LANES, BLK_STATE), 1) // SSM_STATE
    on_diagonal = row_group == col_group

    def blocks(m):
        pair = jnp.concatenate([m, m], axis=1)
        return jnp.where(on_diagonal, jnp.concatenate([pair] * (BLK_STATE // LANES), axis=1), 0.0)

    lr, li = lr_ref[...], li_ref[...]
    dt = jnp.exp(ldt_ref[...])
    mag = jnp.exp(lr * dt)
    ab_re = mag * jnp.cos(li * dt)
    ab_im = mag * jnp.sin(li * dt)
    den = lr * lr + li * li
    nr = ab_re - 1.0
    ni = ab_im
    coef_re = (nr * lr + ni * li) / den
    coef_im = (ni * lr - nr * li) / den
    are_ref[...] = ab_re
    aim_ref[...] = ab_im
    a2_re = ab_re * ab_re - ab_im * ab_im
    a2_im = 2.0 * (ab_re * ab_im)
    bf = jnp.bfloat16
    for k in range(N_BLK):
        ar, ai = ab_re[k:k + 1, :], ab_im[k:k + 1, :]
        for j in range(HALF):
            lanes = slice(j * LANES, (j + 1) * LANES)
            a2re_ref[k * HALF + j] = jnp.broadcast_to(a2_re[k:k + 1, lanes], (SUBLANES, LANES))
            a2im_ref[k * HALF + j] = jnp.broadcast_to(a2_im[k:k + 1, lanes], (SUBLANES, LANES))
        cr, ci = coef_re[k:k + 1, :], coef_im[k:k + 1, :]
        bre, bim = blocks(bre_ref[k]), blocks(bim_ref[k])
        bb_re = cr * bre - ci * bim
        bb_im = cr * bim + ci * bre
        bb_ref[k, :, 0:BLK_STATE] = bb_re.astype(bf)
        bb_ref[k, :, BLK_STATE:] = bb_im.astype(bf)
        wb_ref[k, 0:LANES, 0:BLK_STATE] = (ar * bb_re - ai * bb_im).astype(bf)
        wb_ref[k, 0:LANES, BLK_STATE:] = (ar * bb_im + ai * bb_re).astype(bf)
        wb_ref[k, LANES:, 0:BLK_STATE] = bb_re.astype(bf)
        wb_ref[k, LANES:, BLK_STATE:] = bb_im.astype(bf)
        ctre, ctim = blocks(ctre_ref[k]), blocks(ctim_ref[k])
        wct_ref[k, 0:LANES, 0:BLK_STATE] = ctre.astype(bf)
        wct_ref[k, 0:LANES, BLK_STATE:] = (-ctim).astype(bf)
        wct_ref[k, LANES:, 0:BLK_STATE] = (ctre * ar - ctim * ai).astype(bf)
        wct_ref[k, LANES:, BLK_STATE:] = (-(ctre * ai + ctim * ar)).astype(bf)
        wdm = (lax.dot_general(bb_re, ctre, _NT, precision=lax.Precision.HIGHEST,
                               preferred_element_type=jnp.float32)
               - lax.dot_general(bb_im, ctim, _NT, precision=lax.Precision.HIGHEST,
                                 preferred_element_type=jnp.float32))
        wdm_ref[k] = wdm.astype(bf)


def _discretize(lr, li, ldt, bre, bim, ctre, ctim):
    f32, bf = jnp.float32, jnp.bfloat16
    return pl.pallas_call(
        _discretize_kernel,
        out_shape=[
            jax.ShapeDtypeStruct((N_BLK, BLK_STATE), f32),
            jax.ShapeDtypeStruct((N_BLK, BLK_STATE), f32),
            jax.ShapeDtypeStruct((N_TILE, SUBLANES, LANES), f32),
            jax.ShapeDtypeStruct((N_TILE, SUBLANES, LANES), f32),
            jax.ShapeDtypeStruct((N_BLK, LANES, 2 * BLK_STATE), bf),
            jax.ShapeDtypeStruct((N_BLK, 2 * LANES, 2 * BLK_STATE), bf),
            jax.ShapeDtypeStruct((N_BLK, 2 * LANES, 2 * BLK_STATE), bf),
            jax.ShapeDtypeStruct((N_BLK, LANES, LANES), bf),
        ],
        name="ssm_discretize",
    )(lr, li, ldt, bre, bim, ctre, ctim)


def _ffn1_mixer_kernel(x_ref, xd_ref, norms_ref, wg_c, wu_c, wd_c, win_c,
                       a2re_ref, a2im_ref, wb_ref, wct_ref, wdm_ref, vecs_ref, wglu_ref,
                       h_ref, mix_ref, sre_ref, sim_ref, nconv_ref, ud_ref, vd_ref, gd_ref,
                       xs0_ref, xs1_ref, xs2_ref, xs3_ref, us_ref, ys_ref, l_ref, yo_ref, ye_ref, e_ref,
                       ecarry_ref, v_ref, g_ref, state_ref, wg_ref, wu_ref, wd_ref, win_ref, a_ref):
    xs_refs = (xs0_ref, xs1_ref, xs2_ref, xs3_ref)
    n1_ref, nm_ref = norms_ref.at[NORM_FFN1:NORM_FFN1 + 1], norms_ref.at[NORM_MIX:NORM_MIX + 1]
    d_ref, bglu_ref, cw_ref = _ssm_vecs(vecs_ref)
    step = pl.program_id(0)
    last = pl.num_programs(0) - 1
    nb, t_len, _ = x_ref.shape
    n_dec = xd_ref.shape[0]
    prow = nb * PAIR_PITCH
    tiles = [(k, j) for k in range(N_BLK) for j in range(HALF)]

    @pl.when(step < W1_STEPS)
    def _load_weights():
        _cast_weight_chunks(step, ((wg_c, wg_ref), (wu_c, wu_ref), (wd_c, wd_ref), (win_c, win_ref)))

    @pl.when(step == 0)
    def _init():
        for ref in (state_ref, us_ref, l_ref, e_ref, ecarry_ref, v_ref, g_ref):
            ref[...] = jnp.zeros_like(ref)

    def run(decode):
        carry = {}

        def build_lhs():
            for b in range(nb):
                for k in range(N_BLK):
                    for par in range(2):
                        col = (2 * k + par) * LANES
                        l_ref[pl.ds(b * PAIR_PITCH, N_PAIR), col:col + LANES] = (
                            us_ref[k, pl.ds(b * t_len + par, N_PAIR, stride=2), :])

        def b_proj(ks):
            for k in ks:
                lhs = l_ref[:, 2 * k * LANES:2 * (k + 1) * LANES].astype(jnp.bfloat16)
                v = jnp.dot(lhs, wb_ref[k], preferred_element_type=jnp.float32)
                for j in range(SLABS_PER_BLK):
                    xs_refs[k][j] = v[:, j * LANES:(j + 1) * LANES]

        def scan(r0, r1):
            if r0 == 0:
                carry["re"] = [state_ref[k * SLABS_PER_BLK + j] for k, j in tiles]
                carry["im"] = [state_ref[k * SLABS_PER_BLK + HALF + j] for k, j in tiles]
            xr, xi = carry["re"], carry["im"]
            for r in range(r0, r1):
                rows = pl.ds(r, SUBLANES, stride=PAIR_PITCH)
                for n, (k, j) in enumerate(tiles):
                    a_re = a2re_ref[k * HALF + j]
                    a_im = a2im_ref[k * HALF + j]
                    nr = (a_re * xr[n] + xs_refs[k][j, rows, :]) - a_im * xi[n]
                    ni = (a_re * xi[n] + xs_refs[k][HALF + j, rows, :]) + a_im * xr[n]
                    xs_refs[k][j, rows, :] = nr
                    xs_refs[k][HALF + j, rows, :] = ni
                    xr[n], xi[n] = nr, ni
            if r1 == N_PAIR:
                for n, (k, j) in enumerate(tiles):
                    state_ref[k * SLABS_PER_BLK + j] = xr[n]
                    state_ref[k * SLABS_PER_BLK + HALF + j] = xi[n]

        def c_proj(ks):
            for k in ks:
                lanes = slice(k * LANES, (k + 1) * LANES)
                x = jnp.concatenate([xs_refs[k][j] for j in range(SLABS_PER_BLK)], axis=1)
                ce = lax.dot_general(x.astype(jnp.bfloat16), wct_ref[k], _NT,
                                     preferred_element_type=jnp.float32)
                yo_ref[:, lanes] = ce[:, 0:LANES]
                e_ref[k, SUBLANES:SUBLANES + prow, :] = ce[:, LANES:]
                nxt = e_ref[k, pl.ds(SUBLANES + N_PAIR - 1, nb, stride=PAIR_PITCH), :]
                e_ref[k, pl.ds(SUBLANES - 1, nb, stride=PAIR_PITCH), :] = ecarry_ref[k]
                ecarry_ref[k] = nxt
                u_even = l_ref[:, 2 * k * LANES:(2 * k + 1) * LANES].astype(jnp.bfloat16)
                ye_ref[:, lanes] = (e_ref[k, SUBLANES - 1:SUBLANES - 1 + prow, :]
                                    + jnp.dot(u_even, wdm_ref[k], preferred_element_type=jnp.float32))

        def glu_emit():
            for b in range(nb):
                for k in range(N_BLK):
                    lanes = slice(k * LANES, (k + 1) * LANES)
                    for par, src in ((0, ye_ref), (1, yo_ref)):
                        ys_ref[k, pl.ds(b * t_len + par, N_PAIR, stride=2), :] = (
                            src[pl.ds(b * PAIR_PITCH, N_PAIR), lanes])
            y = jnp.concatenate([ys_ref[k] for k in range(N_BLK)], axis=1)
            u = jnp.concatenate([us_ref[k] for k in range(N_BLK)], axis=1)
            s_out = _glu_tail(y, u, d_ref, wglu_ref, bglu_ref)
            mix_ref[:, :, 0:SSM_WIDTH] = s_out.reshape(nb, t_len, SSM_WIDTH).astype(mix_ref.dtype)

        def conv():
            v = v_ref[:, SUBLANES:SUBLANES + t_len, :]
            z = cw_ref[0:1, :] * v_ref[:, SUBLANES - 2:SUBLANES - 2 + t_len, :]
            z = z + cw_ref[1:2, :] * v_ref[:, SUBLANES - 1:SUBLANES - 1 + t_len, :]
            z = z + cw_ref[2:3, :] * v
            mix_ref[:, :, SSM_WIDTH:] = (g_ref[...] * z).astype(mix_ref.dtype)
            v_ref[:, 0:SUBLANES, :] = v_ref[:, t_len:t_len + SUBLANES, :]

        q = N_PAIR // 4
        pieces = [
            lambda: (conv(), build_lhs(), b_proj((0,))), lambda: b_proj((1,)), lambda: b_proj((2,)),
            lambda: b_proj((3,)),
            lambda: scan(0, q), lambda: scan(q, 2 * q), lambda: scan(2 * q, 3 * q), lambda: scan(3 * q, N_PAIR),
            lambda: c_proj((0, 1)), lambda: c_proj((2, 3)),
            glu_emit,
        ]
        assert len(pieces) <= D_FF // FFN_TF

        rows = n_dec if decode else nb * t_len
        x = xd_ref[...] if decode else x_ref[...].reshape(rows, D_MODEL)
        xn = _rms(x, n1_ref[...]).astype(jnp.bfloat16)
        for c, j in enumerate(range(0, D_FF, FFN_TF)):
            _swiglu_stage(xn, wg_ref, wu_ref, a_ref, j, j + FFN_TF)
            if c < len(pieces):
                pieces[c]()
        h = x + 0.5 * jnp.dot(a_ref[0:rows, :], wd_ref[...], preferred_element_type=jnp.float32)
        hn = _rms(h, nm_ref[...]).astype(jnp.bfloat16)
        p = jnp.dot(hn, win_ref[...], preferred_element_type=jnp.float32)
        v_new = p[:, SSM_WIDTH + 2 * CONV_WIDTH:] * p[:, SSM_WIDTH:SSM_WIDTH + CONV_WIDTH]
        g_new = p[:, SSM_WIDTH + CONV_WIDTH:SSM_WIDTH + 2 * CONV_WIDTH]
        if decode:
            db = n_dec // t_len
            h_ref[0:db] = h.reshape(db, t_len, D_MODEL)
            h_ref[db:] = jnp.zeros((nb - db, t_len, D_MODEL), jnp.float32)
            ud_ref[...] = p[:, 0:SSM_WIDTH]
            vd_ref[...] = v_new
            gd_ref[...] = g_new
            for n, (k, j) in enumerate(tiles):
                for out_ref, slab in ((sre_ref, k * SLABS_PER_BLK + j), (sim_ref, k * SLABS_PER_BLK + HALF + j)):
                    tile = state_ref[slab]
                    for half in range(LANES // SSM_STATE):
                        group = (k * BLK_STATE + j * LANES) // SSM_STATE + half
                        out_ref[:, group, :] = tile[:, half * SSM_STATE:(half + 1) * SSM_STATE]
            nconv_ref[...] = v_ref[:, SUBLANES - 2:SUBLANES, :]
        else:
            h_ref[...] = h.reshape(nb, t_len, D_MODEL)
            v_ref[:, SUBLANES:SUBLANES + t_len, :] = v_new.reshape(nb, t_len, CONV_WIDTH)
            g_ref[...] = g_new.reshape(nb, t_len, CONV_WIDTH)
            for k in range(N_BLK):
                us_ref[k] = p[:, k * LANES:(k + 1) * LANES]

    @pl.when((step >= W1_STEPS) & (step < last))
    def _prompt_chunk():
        run(decode=False)

    @pl.when(step == last)
    def _decode_rows():
        run(decode=True)


def _ffn1_mixer(x, xd, norms, wg, wu, wd, win, a2_re, a2_im, wb, wct, wdm, vecs, wglu):
    nb, seq, _ = x.shape
    n_dec = xd.shape[0]
    n_chunk = seq // SCAN_T
    prow = nb * PAIR_PITCH
    f32, bf = jnp.float32, jnp.bfloat16
    main = lambda i: jnp.maximum(i - W1_STEPS, 0)
    return pl.pallas_call(
        _ffn1_mixer_kernel,
        grid=(W1_STEPS + n_chunk + 1,),
        in_specs=[
            pl.BlockSpec((nb, SCAN_T, D_MODEL), lambda i: (0, jnp.minimum(main(i), n_chunk - 1), 0)),
            _whole((n_dec, D_MODEL)),
            _whole(norms.shape),
            _weight_chunk((D_MODEL // W1_STEPS, D_FF), W1_STEPS),
            _weight_chunk((D_MODEL // W1_STEPS, D_FF), W1_STEPS),
            _weight_chunk((D_FF // W1_STEPS, D_MODEL), W1_STEPS),
            _weight_chunk((D_MODEL // W1_STEPS, IN_PROJ_WIDTH), W1_STEPS),
            _whole((N_TILE, SUBLANES, LANES)),
            _whole((N_TILE, SUBLANES, LANES)),
            _whole((N_BLK, 2 * LANES, 2 * BLK_STATE)),
            _whole((N_BLK, 2 * LANES, 2 * BLK_STATE)),
            _whole((N_BLK, LANES, LANES)),
            _whole(vecs.shape),
            _whole((SSM_WIDTH, SSM_WIDTH)),
        ],
        out_specs=[
            pl.BlockSpec((nb, SCAN_T, D_MODEL), lambda i: (0, main(i), 0)),
            pl.BlockSpec((nb, SCAN_T, D_MODEL), lambda i: (0, jnp.clip(main(i) - 1, 0, n_chunk - 1), 0)),
            pl.BlockSpec((nb, N_SSM_GROUPS, SSM_STATE), lambda i: (0, 0, 0)),
            pl.BlockSpec((nb, N_SSM_GROUPS, SSM_STATE), lambda i: (0, 0, 0)),
            pl.BlockSpec((nb, CONV_K - 1, CONV_WIDTH), lambda i: (0, 0, 0)),
            pl.BlockSpec((n_dec, SSM_WIDTH), lambda i: (0, 0)),
            pl.BlockSpec((n_dec, CONV_WIDTH), lambda i: (0, 0)),
            pl.BlockSpec((n_dec, CONV_WIDTH), lambda i: (0, 0)),
        ],
        out_shape=[
            jax.ShapeDtypeStruct((nb, seq + SCAN_T, D_MODEL), f32),
            jax.ShapeDtypeStruct((nb, seq, D_MODEL), bf),
            jax.ShapeDtypeStruct((nb, N_SSM_GROUPS, SSM_STATE), f32),
            jax.ShapeDtypeStruct((nb, N_SSM_GROUPS, SSM_STATE), f32),
            jax.ShapeDtypeStruct((nb, CONV_K - 1, CONV_WIDTH), f32),
            jax.ShapeDtypeStruct((n_dec, SSM_WIDTH), f32),
            jax.ShapeDtypeStruct((n_dec, CONV_WIDTH), f32),
            jax.ShapeDtypeStruct((n_dec, CONV_WIDTH), f32),
        ],
        scratch_shapes=[
            *[pltpu.VMEM((SLABS_PER_BLK, prow, LANES), f32) for _ in range(N_BLK)],
            pltpu.VMEM((N_BLK, nb * SCAN_T, LANES), f32),
            pltpu.VMEM((N_BLK, nb * SCAN_T, LANES), f32),
            pltpu.VMEM((prow, 2 * SSM_WIDTH), f32),
            pltpu.VMEM((prow, SSM_WIDTH), f32),
            pltpu.VMEM((prow, SSM_WIDTH), f32),
            pltpu.VMEM((N_BLK, SUBLANES + prow, LANES), f32),
            pltpu.VMEM((N_BLK, SUBLANES, LANES), f32),
            pltpu.VMEM((nb, SCAN_T + SUBLANES, CONV_WIDTH), f32),
            pltpu.VMEM((nb, SCAN_T, CONV_WIDTH), f32),
            pltpu.VMEM((2 * N_TILE, SUBLANES, LANES), f32),
            pltpu.VMEM((D_MODEL, D_FF), bf),
            pltpu.VMEM((D_MODEL, D_FF), bf),
            pltpu.VMEM((D_FF, D_MODEL), bf),
            pltpu.VMEM((D_MODEL, IN_PROJ_WIDTH), bf),
            pltpu.VMEM((nb * SCAN_T, D_FF), bf),
        ],
        compiler_params=pltpu.CompilerParams(
            dimension_semantics=("arbitrary",), vmem_limit_bytes=VMEM_LIMIT),
        name="ffn1_mixer",
    )(x, xd, norms, wg, wu, wd, win, a2_re, a2_im, wb, wct, wdm, vecs, wglu)


def _ffn2_rows(h, mix, weights, a_ref):
    wout_ref, n2_ref, wg_ref, wu_ref, wd_ref, nf_ref = weights
    h = h + jnp.dot(mix, wout_ref[...], preferred_element_type=jnp.float32)
    hn = _rms(h, n2_ref[...]).astype(jnp.bfloat16)
    h = h + 0.5 * _swiglu_staged(hn, wg_ref, wu_ref, wd_ref, a_ref)
    return _rms(h, nf_ref[...])


def _ffn2_kernel(h_ref, mix_ref, hd0_ref, hd1_ref, mixd_ref, wout_c, wg_c, wu_c, wd_c, norms_ref,
                 y_ref, yd_ref, wout_ref, wg_ref, wu_ref, wd_ref, a0_ref, a1_ref):
    n2_ref, nf_ref = norms_ref.at[NORM_FFN2:NORM_FFN2 + 1], norms_ref.at[NORM_FINAL:NORM_FINAL + 1]
    step = pl.program_id(0)
    last = pl.num_programs(0) - 1
    weights = (wout_ref, n2_ref, wg_ref, wu_ref, wd_ref, nf_ref)

    @pl.when(step < W2_STEPS)
    def _load_weights():
        _cast_weight_chunks(step, ((wout_c, wout_ref), (wg_c, wg_ref), (wu_c, wu_ref), (wd_c, wd_ref)))

    @pl.when((step >= W2_STEPS) & (step < last))
    def _prompt_tile():
        assert h_ref.shape[0] == 2 * FFN_SUB
        ra, rb = slice(0, FFN_SUB), slice(FFN_SUB, 2 * FFN_SUB)
        h_a = h_ref[ra, :] + jnp.dot(mix_ref[ra, :], wout_ref[...], preferred_element_type=jnp.float32)
        h_b = h_ref[rb, :] + jnp.dot(mix_ref[rb, :], wout_ref[...], preferred_element_type=jnp.float32)
        hn_a = _rms(h_a, n2_ref[...]).astype(jnp.bfloat16)
        _swiglu_stage(hn_a, wg_ref, wu_ref, a0_ref, 0, FFN_TF)
        hn_b = _rms(h_b, n2_ref[...]).astype(jnp.bfloat16)
        _swiglu_stage(hn_a, wg_ref, wu_ref, a0_ref, FFN_TF, D_FF)
        f_a = jnp.dot(a0_ref[...], wd_ref[...], preferred_element_type=jnp.float32)
        _swiglu_stage(hn_b, wg_ref, wu_ref, a1_ref, 0, FFN_TF)
        y_ref[ra, :] = _rms(h_a + 0.5 * f_a, nf_ref[...])
        _swiglu_stage(hn_b, wg_ref, wu_ref, a1_ref, FFN_TF, D_FF)
        f_b = jnp.dot(a1_ref[...], wd_ref[...], preferred_element_type=jnp.float32)
        y_ref[rb, :] = _rms(h_b + 0.5 * f_b, nf_ref[...])

    @pl.when(step == last)
    def _decode_rows():
        h_d = jnp.concatenate([hd0_ref[...], hd1_ref[...]], axis=0)
        yd_ref[...] = _ffn2_rows(h_d, mixd_ref[...], weights, a0_ref)


def _ffn2(h_all, mix, mixd, wout, wg, wu, wd, norms):
    nb, seq, _ = mix.shape
    seq_all = h_all.shape[1]
    n_dec = mixd.shape[0]
    tail = seq_all - seq
    assert n_dec == 2 * tail
    per_b = seq // FFN_TM
    n_tiles = nb * per_b
    h_all = h_all.reshape(nb * seq_all, D_MODEL)
    mix = mix.reshape(nb * seq, D_MODEL)
    idx = lambda i: jnp.clip(i - W2_STEPS, 0, n_tiles - 1)
    tile = lambda i: (idx(i), 0)
    assert seq_all % SUBLANES == 0
    tile_all = lambda i: (pl.multiple_of((idx(i) // per_b) * seq_all + (idx(i) % per_b) * FFN_TM, SUBLANES), 0)
    yp, yd = pl.pallas_call(
        _ffn2_kernel,
        grid=(W2_STEPS + n_tiles + 1,),
        in_specs=[
            pl.BlockSpec((pl.Element(FFN_TM), pl.Element(D_MODEL)), tile_all),
            pl.BlockSpec((FFN_TM, D_MODEL), tile),
            pl.BlockSpec((pl.Element(tail), pl.Element(D_MODEL)), lambda i: (seq, 0),
                         pipeline_mode=pl.Buffered(1)),
            pl.BlockSpec((pl.Element(tail), pl.Element(D_MODEL)), lambda i: (seq_all + seq, 0),
                         pipeline_mode=pl.Buffered(1)),
            _whole(mixd.shape),
            _weight_chunk((D_MODEL // W2_STEPS, D_MODEL), W2_STEPS),
            _weight_chunk((D_MODEL // W2_STEPS, D_FF), W2_STEPS),
            _weight_chunk((D_MODEL // W2_STEPS, D_FF), W2_STEPS),
            _weight_chunk((D_FF // W2_STEPS, D_MODEL), W2_STEPS),
            _whole(norms.shape),
        ],
        out_specs=[
            pl.BlockSpec((FFN_TM, D_MODEL), tile),
            pl.BlockSpec((n_dec, D_MODEL), lambda i: (0, 0)),
        ],
        out_shape=[
            jax.ShapeDtypeStruct((nb * seq, D_MODEL), jnp.float32),
            jax.ShapeDtypeStruct((n_dec, D_MODEL), jnp.float32),
        ],
        scratch_shapes=[
            pltpu.VMEM((D_MODEL, D_MODEL), jnp.bfloat16),
            pltpu.VMEM((D_MODEL, D_FF), jnp.bfloat16),
            pltpu.VMEM((D_MODEL, D_FF), jnp.bfloat16),
            pltpu.VMEM((D_FF, D_MODEL), jnp.bfloat16),
            pltpu.VMEM((FFN_SUB, D_FF), jnp.bfloat16),
            pltpu.VMEM((FFN_SUB, D_FF), jnp.bfloat16),
        ],
        compiler_params=pltpu.CompilerParams(
            dimension_semantics=("arbitrary",), vmem_limit_bytes=VMEM_LIMIT),
        name="outproj_ffn2",
    )(h_all, mix, h_all, h_all, mixd, wout, wg, wu, wd, norms)
    return yp.reshape(nb, seq, D_MODEL), yd


def _mixer_step_kernel(u_ref, v_ref, g_ref, hre_ref, him_ref, buf_ref, are_ref, aim_ref, bb_ref, wct_ref,
                       vecs_ref, wglu_ref, mix_ref, sre_ref, sim_ref, nconv_ref):
    d_ref, bglu_ref, cw_ref = _ssm_vecs(vecs_ref)
    u = u_ref[...]
    u_bf = u.astype(jnp.bfloat16)
    ys = []
    for k in range(N_BLK):
        bu = jnp.dot(u_bf[:, k * LANES:(k + 1) * LANES], bb_ref[k], preferred_element_type=jnp.float32)
        st = slice(k * BLK_STATE, (k + 1) * BLK_STATE)
        a_re = are_ref[k:k + 1, :]
        a_im = aim_ref[k:k + 1, :]
        h_re = hre_ref[:, st]
        h_im = him_ref[:, st]
        x_re = (a_re * h_re + bu[:, :BLK_STATE]) - a_im * h_im
        x_im = (a_re * h_im + bu[:, BLK_STATE:]) + a_im * h_re
        sre_ref[:, st] = x_re
        sim_ref[:, st] = x_im
        x = jnp.concatenate([x_re, x_im], axis=1).astype(jnp.bfloat16)
        ys.append(lax.dot_general(x, wct_ref[k, 0:LANES, :], _NT, preferred_element_type=jnp.float32))
    y = jnp.concatenate(ys, axis=1)
    mix_ref[:, 0:SSM_WIDTH] = _glu_tail(y, u, d_ref, wglu_ref, bglu_ref).astype(mix_ref.dtype)

    v = v_ref[...]
    z = cw_ref[0:1, :] * buf_ref[0] + cw_ref[1:2, :] * buf_ref[1] + cw_ref[2:3, :] * v
    mix_ref[:, SSM_WIDTH:] = (g_ref[...] * z).astype(mix_ref.dtype)
    nconv_ref[0] = buf_ref[1]
    nconv_ref[1] = v


def _mixer_step(u, v, g, h_re, h_im, buf, a_re, a_im, bb, wct, vecs, wglu):
    nb = u.shape[0]
    return pl.pallas_call(
        _mixer_step_kernel,
        out_shape=[
            jax.ShapeDtypeStruct((nb, D_MODEL), jnp.bfloat16),
            jax.ShapeDtypeStruct((nb, N_STATE), jnp.float32),
            jax.ShapeDtypeStruct((nb, N_STATE), jnp.float32),
            jax.ShapeDtypeStruct((CONV_K - 1, nb, CONV_WIDTH), jnp.float32),
        ],
        compiler_params=pltpu.CompilerParams(vmem_limit_bytes=VMEM_LIMIT),
        name="mixer_step",
    )(u, v, g, h_re, h_im, buf, a_re, a_im, bb, wct, vecs, wglu)


def _ssm_layout(lam_re, lam_im, log_dt, b_re, b_im, c_re, c_im):
    gpb = N_SSM_GROUPS // N_BLK
    lr = lam_re.reshape(N_BLK, BLK_STATE)
    li = lam_im.reshape(N_BLK, BLK_STATE)
    ldt = jnp.broadcast_to(log_dt[:, None], (N_SSM_GROUPS, SSM_STATE)).reshape(N_BLK, BLK_STATE)

    def b_rows(b):
        return b.reshape(N_BLK, gpb, SSM_STATE, SSM_GROUP).transpose(0, 1, 3, 2).reshape(N_BLK, LANES, SSM_STATE)

    def c_rows(c):
        return c.reshape(N_BLK, LANES, SSM_STATE)

    return lr, li, ldt, b_rows(b_re), b_rows(b_im), c_rows(c_re), c_rows(c_im)


def kernel(x_prompt, x_sample, state_ssm_re, state_ssm_im, state_conv, ffn1_norm, ffn1_w_gate, ffn1_w_up, ffn1_w_down, mix_norm, w_in, ssm_lambda_re, ssm_lambda_im, ssm_log_dt, ssm_b_re, ssm_b_im, ssm_c_re, ssm_c_im, ssm_d, ssm_w_glu, ssm_b_glu, conv_w, w_out, ffn2_norm, ffn2_w_gate, ffn2_w_up, ffn2_w_down, final_norm):
    assert ffn1_norm.shape[0] == 1, "single-layer trunk"
    n_prompt, seq, _ = x_prompt.shape
    n_dec, dec_seq, _ = x_sample.shape
    assert dec_seq == 1 and n_prompt == SUBLANES and seq % FFN_TM == 0 and seq % SCAN_T == 0
    assert n_dec % SCAN_T == 0 and n_dec <= n_prompt * SCAN_T and FFN_TM % FFN_SUB == 0

    norms = jnp.concatenate([ffn1_norm, mix_norm, ffn2_norm, final_norm[None, :]], axis=0)
    vecs = jnp.concatenate([ssm_d, ssm_b_glu, conv_w[0],
                            jnp.zeros((SUBLANES - 2 - CONV_K, SSM_WIDTH), jnp.float32)], axis=0)
    wglu = ssm_w_glu[0].astype(jnp.bfloat16)
    layout = _ssm_layout(ssm_lambda_re[0], ssm_lambda_im[0], ssm_log_dt[0], ssm_b_re[0], ssm_b_im[0],
                         ssm_c_re[0], ssm_c_im[0])
    a_re, a_im, a2_re, a2_im, bb, wb, wct, wdm = _discretize(*layout)

    h_all, mixp, p_re, p_im, p_conv, ud, vd, gd = _ffn1_mixer(
        x_prompt, x_sample.reshape(n_dec, D_MODEL), norms, ffn1_w_gate[0], ffn1_w_up[0], ffn1_w_down[0],
        w_in[0], a2_re, a2_im, wb, wct, wdm, vecs, wglu)

    s_re0 = state_ssm_re[0].reshape(n_dec, N_STATE)
    s_im0 = state_ssm_im[0].reshape(n_dec, N_STATE)
    buf = state_conv[0].transpose(1, 0, 2)
    mixd, s_re, s_im, s_conv = _mixer_step(ud, vd, gd, s_re0, s_im0, buf, a_re, a_im, bb, wct, vecs, wglu)

    yp, yd = _ffn2(h_all, mixp, mixd, w_out[0], ffn2_w_gate[0], ffn2_w_up[0], ffn2_w_down[0], norms)

    gs = (N_SSM_GROUPS, SSM_STATE)
    return (yp,
            yd.reshape(n_dec, 1, D_MODEL),
            p_re[None],
            p_im[None],
            p_conv[None],
            s_re.reshape(1, n_dec, *gs),
            s_im.reshape(1, n_dec, *gs),
            s_conv.transpose(1, 0, 2)[None])
```

```python
import jax
import jax.numpy as jnp
from jax import lax
from jax.experimental import pallas as pl
from jax.experimental.pallas import tpu as pltpu

D_MODEL = 1024
D_FF = 2816
SSM_WIDTH = 512
CONV_WIDTH = 512
SSM_GROUP = 16
N_SSM_GROUPS = 32
SSM_STATE = 64
CONV_K = 3
IN_PROJ_WIDTH = SSM_WIDTH + 3 * CONV_WIDTH
EPS = 1e-6

LANES = 128
SUBLANES = 8
N_STATE = N_SSM_GROUPS * SSM_STATE
N_BLK = SSM_WIDTH // LANES
BLK_STATE = N_STATE // N_BLK
SLABS_PER_BLK = 2 * BLK_STATE // LANES
HALF = SLABS_PER_BLK // 2
N_TILE = N_BLK * HALF

FFN_TM = 1024
FFN_SUB = 512
W1_STEPS = 16
W2_STEPS = 8
FFN_TF = 256
SCAN_T = 64
N_PAIR = SCAN_T // 2
PAIR_PITCH = N_PAIR + 4
VMEM_LIMIT = 62 * 1024 * 1024

_NT = (((1,), (1,)), ((), ()))

NORM_FFN1, NORM_MIX, NORM_FFN2, NORM_FINAL = range(4)
VEC_D, VEC_BGLU, VEC_CW = 0, 1, 2


def _rms(x, g):
    r = lax.rsqrt(jnp.mean(x * x, axis=-1, keepdims=True) + EPS)
    return x * r * g


def _bdot(a, b):
    return jnp.dot(a.astype(jnp.bfloat16), b.astype(jnp.bfloat16), preferred_element_type=jnp.float32)


def _swiglu_stage(xn, wg_ref, wu_ref, a_ref, lo, hi):
    rows = xn.shape[0]
    for j in range(lo, hi, FFN_TF):
        g = jnp.dot(xn, wg_ref[:, j:j + FFN_TF], preferred_element_type=jnp.float32)
        u = jnp.dot(xn, wu_ref[:, j:j + FFN_TF], preferred_element_type=jnp.float32)
        a_ref[0:rows, j:j + FFN_TF] = ((g * jax.nn.sigmoid(g)) * u).astype(a_ref.dtype)


def _swiglu_staged(xn, wg_ref, wu_ref, wd_ref, a_ref):
    rows = xn.shape[0]
    _swiglu_stage(xn, wg_ref, wu_ref, a_ref, 0, D_FF)
    return jnp.dot(a_ref[0:rows, :], wd_ref[...], preferred_element_type=jnp.float32)


def _cast_weight_chunks(step, pairs):
    for src, dst in pairs:
        n = src.shape[0]
        dst[pl.ds(pl.multiple_of(step * n, n), n), :] = src[...].astype(dst.dtype)


def _whole(shape):
    return pl.BlockSpec(shape, lambda i: (0,) * len(shape), pipeline_mode=pl.Buffered(1))


def _weight_chunk(shape, n_steps):
    return pl.BlockSpec(shape, lambda i: (jnp.minimum(i, n_steps - 1), 0))


def _ssm_vecs(vecs_ref):
    return (vecs_ref.at[VEC_D:VEC_D + 1], vecs_ref.at[VEC_BGLU:VEC_BGLU + 1],
            vecs_ref.at[VEC_CW:VEC_CW + CONV_K])


def _glu_tail(y, u, d_ref, wglu_ref, bglu_ref):
    y = y + d_ref[...] * u
    z = jax.nn.gelu(y)
    gate = jax.nn.sigmoid(_bdot(z, wglu_ref[...]) + bglu_ref[...])
    return z * gate


def _discretize_kernel(lr_ref, li_ref, ldt_ref, bre_ref, bim_ref, ctre_ref, ctim_ref,
                       are_ref, aim_ref, a2re_ref, a2im_ref, bb_ref, wb_ref, wct_ref, wdm_ref):
    row_group = lax.broadcasted_iota(jnp.int32, (LANES, BLK_STATE), 0) // SSM_GROUP
    col_group = lax.broadcasted_iota(jnp.int32, (LANES, BLK_STATE), 1) // SSM_STATE
    on_diagonal = row_group == col_group

    def blocks(m):
        pair = jnp.concatenate([m, m], axis=1)
        return jnp.where(on_diagonal, jnp.concatenate([pair] * (BLK_STATE // LANES), axis=1), 0.0)

    lr, li = lr_ref[...], li_ref[...]
    dt = jnp.exp(ldt_ref[...])
    mag = jnp.exp(lr * dt)
    ab_re = mag * jnp.cos(li * dt)
    ab_im = mag * jnp.sin(li * dt)
    den = lr * lr + li * li
    nr = ab_re - 1.0
    ni = ab_im
    coef_re = (nr * lr + ni * li) / den
    coef_im = (ni * lr - nr * li) / den
    are_ref[...] = ab_re
    aim_ref[...] = ab_im
    a2_re = ab_re * ab_re - ab_im * ab_im
    a2_im = 2.0 * (ab_re * ab_im)
    bf = jnp.bfloat16
    for k in range(N_BLK):
        ar, ai = ab_re[k:k + 1, :], ab_im[k:k + 1, :]
        for j in range(HALF):
            lanes = slice(j * LANES, (j + 1) * LANES)
            a2re_ref[k * HALF + j] = jnp.broadcast_to(a2_re[k:k + 1, lanes], (SUBLANES, LANES))
            a2im_ref[k * HALF + j] = jnp.broadcast_to(a2_im[k:k + 1, lanes], (SUBLANES, LANES))
        cr, ci = coef_re[k:k + 1, :], coef_im[k:k + 1, :]
        bre, bim = blocks(bre_ref[k]), blocks(bim_ref[k])
        bb_re = cr * bre - ci * bim
        bb_im = cr * bim + ci * bre
        bb_ref[k, :, 0:BLK_STATE] = bb_re.astype(bf)
        bb_ref[k, :, BLK_STATE:] = bb_im.astype(bf)
        wb_ref[k, 0:LANES, 0:BLK_STATE] = (ar * bb_re - ai * bb_im).astype(bf)
        wb_ref[k, 0:LANES, BLK_STATE:] = (ar * bb_im + ai * bb_re).astype(bf)
        wb_ref[k, LANES:, 0:BLK_STATE] = bb_re.astype(bf)
        wb_ref[k, LANES:, BLK_STATE:] = bb_im.astype(bf)
        ctre, ctim = blocks(ctre_ref[k]), blocks(ctim_ref[k])
        wct_ref[k, 0:LANES, 0:BLK_STATE] = ctre.astype(bf)
        wct_ref[k, 0:LANES, BLK_STATE:] = (-ctim).astype(bf)
        wct_ref[k, LANES:, 0:BLK_STATE] = (ctre * ar - ctim * ai).astype(bf)
        wct_ref[k, LANES:, BLK_STATE:] = (-(ctre * ai + ctim * ar)).astype(bf)
        wdm = (lax.dot_general(bb_re, ctre, _NT, precision=lax.Precision.HIGHEST,
                               preferred_element_type=jnp.float32)
               - lax.dot_general(bb_im, ctim, _NT, precision=lax.Precision.HIGHEST,
                                 preferred_element_type=jnp.float32))
        wdm_ref[k] = wdm.astype(bf)


def _discretize(lr, li, ldt, bre, bim, ctre, ctim):
    f32, bf = jnp.float32, jnp.bfloat16
    return pl.pallas_call(
        _discretize_kernel,
        out_shape=[
            jax.ShapeDtypeStruct((N_BLK, BLK_STATE), f32),
            jax.ShapeDtypeStruct((N_BLK, BLK_STATE), f32),
            jax.ShapeDtypeStruct((N_TILE, SUBLANES, LANES), f32),
            jax.ShapeDtypeStruct((N_TILE, SUBLANES, LANES), f32),
            jax.ShapeDtypeStruct((N_BLK, LANES, 2 * BLK_STATE), bf),
            jax.ShapeDtypeStruct((N_BLK, 2 * LANES, 2 * BLK_STATE), bf),
            jax.ShapeDtypeStruct((N_BLK, 2 * LANES, 2 * BLK_STATE), bf),
            jax.ShapeDtypeStruct((N_BLK, LANES, LANES), bf),
        ],
        name="ssm_discretize",
    )(lr, li, ldt, bre, bim, ctre, ctim)


def _ffn1_mixer_kernel(x_ref, xd_ref, norms_ref, wg_c, wu_c, wd_c, win_c,
                       a2re_ref, a2im_ref, wb_ref, wct_ref, wdm_ref, vecs_ref, wglu_ref,
                       h_ref, mix_ref, sre_ref, sim_ref, nconv_ref, hd_ref, ud_ref, vd_ref, gd_ref,
                       xs0_ref, xs1_ref, xs2_ref, xs3_ref, us_ref, ys_ref, l_ref, yo_ref, ye_ref, e_ref,
                       ecarry_ref, v_ref, g_ref, state_ref, wg_ref, wu_ref, wd_ref, win_ref, a_ref):
    xs_refs = (xs0_ref, xs1_ref, xs2_ref, xs3_ref)
    n1_ref, nm_ref = norms_ref.at[NORM_FFN1:NORM_FFN1 + 1], norms_ref.at[NORM_MIX:NORM_MIX + 1]
    d_ref, bglu_ref, cw_ref = _ssm_vecs(vecs_ref)
    step = pl.program_id(0)
    last = pl.num_programs(0) - 1
    nb, t_len, _ = x_ref.shape
    n_dec = xd_ref.shape[0]
    prow = nb * PAIR_PITCH
    tiles = [(k, j) for k in range(N_BLK) for j in range(HALF)]

    @pl.when(step < W1_STEPS)
    def _load_weights():
        _cast_weight_chunks(step, ((wg_c, wg_ref), (wu_c, wu_ref), (wd_c, wd_ref), (win_c, win_ref)))

    @pl.when(step == 0)
    def _init():
        for ref in (state_ref, us_ref, l_ref, e_ref, ecarry_ref, v_ref, g_ref):
            ref[...] = jnp.zeros_like(ref)

    def run(decode):
        carry = {}

        def build_lhs():
            for b in range(nb):
                for k in range(N_BLK):
                    for par in range(2):
                        col = (2 * k + par) * LANES
                        l_ref[pl.ds(b * PAIR_PITCH, N_PAIR), col:col + LANES] = (
                            us_ref[k, pl.ds(b * t_len + par, N_PAIR, stride=2), :])

        def b_proj(ks):
            for k in ks:
                lhs = l_ref[:, 2 * k * LANES:2 * (k + 1) * LANES].astype(jnp.bfloat16)
                v = jnp.dot(lhs, wb_ref[k], preferred_element_type=jnp.float32)
                for j in range(SLABS_PER_BLK):
                    xs_refs[k][j] = v[:, j * LANES:(j + 1) * LANES]

        def scan(r0, r1):
            if r0 == 0:
                carry["re"] = [state_ref[k * SLABS_PER_BLK + j] for k, j in tiles]
                carry["im"] = [state_ref[k * SLABS_PER_BLK + HALF + j] for k, j in tiles]
            xr, xi = carry["re"], carry["im"]
            for r in range(r0, r1):
                rows = pl.ds(r, SUBLANES, stride=PAIR_PITCH)
                for n, (k, j) in enumerate(tiles):
                    a_re = a2re_ref[k * HALF + j]
                    a_im = a2im_ref[k * HALF + j]
                    nr = (a_re * xr[n] + xs_refs[k][j, rows, :]) - a_im * xi[n]
                    ni = (a_re * xi[n] + xs_refs[k][HALF + j, rows, :]) + a_im * xr[n]
                    xs_refs[k][j, rows, :] = nr
                    xs_refs[k][HALF + j, rows, :] = ni
                    xr[n], xi[n] = nr, ni
            if r1 == N_PAIR:
                for n, (k, j) in enumerate(tiles):
                    state_ref[k * SLABS_PER_BLK + j] = xr[n]
                    state_ref[k * SLABS_PER_BLK + HALF + j] = xi[n]

        def c_proj(ks):
            for k in ks:
                lanes = slice(k * LANES, (k + 1) * LANES)
                x = jnp.concatenate([xs_refs[k][j] for j in range(SLABS_PER_BLK)], axis=1)
                ce = lax.dot_general(x.astype(jnp.bfloat16), wct_ref[k], _NT,
                                     preferred_element_type=jnp.float32)
                yo_ref[:, lanes] = ce[:, 0:LANES]
                e_ref[k, SUBLANES:SUBLANES + prow, :] = ce[:, LANES:]
                nxt = e_ref[k, pl.ds(SUBLANES + N_PAIR - 1, nb, stride=PAIR_PITCH), :]
                e_ref[k, pl.ds(SUBLANES - 1, nb, stride=PAIR_PITCH), :] = ecarry_ref[k]
                ecarry_ref[k] = nxt
                u_even = l_ref[:, 2 * k * LANES:(2 * k + 1) * LANES].astype(jnp.bfloat16)
                ye_ref[:, lanes] = (e_ref[k, SUBLANES - 1:SUBLANES - 1 + prow, :]
                                    + jnp.dot(u_even, wdm_ref[k], preferred_element_type=jnp.float32))

        def glu_emit():
            for b in range(nb):
                for k in range(N_BLK):
                    lanes = slice(k * LANES, (k + 1) * LANES)
                    for par, src in ((0, ye_ref), (1, yo_ref)):
                        ys_ref[k, pl.ds(b * t_len + par, N_PAIR, stride=2), :] = (
                            src[pl.ds(b * PAIR_PITCH, N_PAIR), lanes])
            y = jnp.concatenate([ys_ref[k] for k in range(N_BLK)], axis=1)
            u = jnp.concatenate([us_ref[k] for k in range(N_BLK)], axis=1)
            s_out = _glu_tail(y, u, d_ref, wglu_ref, bglu_ref)
            mix_ref[:, :, 0:SSM_WIDTH] = s_out.reshape(nb, t_len, SSM_WIDTH).astype(mix_ref.dtype)

        def conv():
            v = v_ref[:, SUBLANES:SUBLANES + t_len, :]
            z = cw_ref[0:1, :] * v_ref[:, SUBLANES - 2:SUBLANES - 2 + t_len, :]
            z = z + cw_ref[1:2, :] * v_ref[:, SUBLANES - 1:SUBLANES - 1 + t_len, :]
            z = z + cw_ref[2:3, :] * v
            mix_ref[:, :, SSM_WIDTH:] = (g_ref[...] * z).astype(mix_ref.dtype)
            v_ref[:, 0:SUBLANES, :] = v_ref[:, t_len:t_len + SUBLANES, :]

        q = N_PAIR // 4
        pieces = [
            lambda: (conv(), build_lhs(), b_proj((0,))), lambda: b_proj((1,)), lambda: b_proj((2,)),
            lambda: b_proj((3,)),
            lambda: scan(0, q), lambda: scan(q, 2 * q), lambda: scan(2 * q, 3 * q), lambda: scan(3 * q, N_PAIR),
            lambda: c_proj((0, 1)), lambda: c_proj((2, 3)),
            glu_emit,
        ]
        assert len(pieces) <= D_FF // FFN_TF

        rows = n_dec if decode else nb * t_len
        x = xd_ref[...] if decode else x_ref[...].reshape(rows, D_MODEL)
        xn = _rms(x, n1_ref[...]).astype(jnp.bfloat16)
        for c, j in enumerate(range(0, D_FF, FFN_TF)):
            _swiglu_stage(xn, wg_ref, wu_ref, a_ref, j, j + FFN_TF)
            if c < len(pieces):
                pieces[c]()
        h = x + 0.5 * jnp.dot(a_ref[0:rows, :], wd_ref[...], preferred_element_type=jnp.float32)
        hn = _rms(h, nm_ref[...]).astype(jnp.bfloat16)
        p = jnp.dot(hn, win_ref[...], preferred_element_type=jnp.float32)
        v_new = p[:, SSM_WIDTH + 2 * CONV_WIDTH:] * p[:, SSM_WIDTH:SSM_WIDTH + CONV_WIDTH]
        g_new = p[:, SSM_WIDTH + CONV_WIDTH:SSM_WIDTH + 2 * CONV_WIDTH]
        if decode:
            hd_ref[...] = h
            ud_ref[...] = p[:, 0:SSM_WIDTH]
            vd_ref[...] = v_new
            gd_ref[...] = g_new
            for n, (k, j) in enumerate(tiles):
                for out_ref, slab in ((sre_ref, k * SLABS_PER_BLK + j), (sim_ref, k * SLABS_PER_BLK + HALF + j)):
                    tile = state_ref[slab]
                    for half in range(LANES // SSM_STATE):
                        group = (k * BLK_STATE + j * LANES) // SSM_STATE + half
                        out_ref[:, group, :] = tile[:, half * SSM_STATE:(half + 1) * SSM_STATE]
            nconv_ref[...] = v_ref[:, SUBLANES - 2:SUBLANES, :]
        else:
            h_ref[...] = h.reshape(nb, t_len, D_MODEL)
            v_ref[:, SUBLANES:SUBLANES + t_len, :] = v_new.reshape(nb, t_len, CONV_WIDTH)
            g_ref[...] = g_new.reshape(nb, t_len, CONV_WIDTH)
            for k in range(N_BLK):
                us_ref[k] = p[:, k * LANES:(k + 1) * LANES]

    @pl.when((step >= W1_STEPS) & (step < last))
    def _prompt_chunk():
        run(decode=False)

    @pl.when(step == last)
    def _decode_rows():
        run(decode=True)


def _ffn1_mixer(x, xd, norms, wg, wu, wd, win, a2_re, a2_im, wb, wct, wdm, vecs, wglu):
    nb, seq, _ = x.shape
    n_dec = xd.shape[0]
    n_chunk = seq // SCAN_T
    prow = nb * PAIR_PITCH
    f32, bf = jnp.float32, jnp.bfloat16
    main = lambda i: jnp.maximum(i - W1_STEPS, 0)
    return pl.pallas_call(
        _ffn1_mixer_kernel,
        grid=(W1_STEPS + n_chunk + 1,),
        in_specs=[
            pl.BlockSpec((nb, SCAN_T, D_MODEL), lambda i: (0, jnp.minimum(main(i), n_chunk - 1), 0)),
            _whole((n_dec, D_MODEL)),
            _whole(norms.shape),
            _weight_chunk((D_MODEL // W1_STEPS, D_FF), W1_STEPS),
            _weight_chunk((D_MODEL // W1_STEPS, D_FF), W1_STEPS),
            _weight_chunk((D_FF // W1_STEPS, D_MODEL), W1_STEPS),
            _weight_chunk((D_MODEL // W1_STEPS, IN_PROJ_WIDTH), W1_STEPS),
            _whole((N_TILE, SUBLANES, LANES)),
            _whole((N_TILE, SUBLANES, LANES)),
            _whole((N_BLK, 2 * LANES, 2 * BLK_STATE)),
            _whole((N_BLK, 2 * LANES, 2 * BLK_STATE)),
            _whole((N_BLK, LANES, LANES)),
            _whole(vecs.shape),
            _whole((SSM_WIDTH, SSM_WIDTH)),
        ],
        out_specs=[
            pl.BlockSpec((nb, SCAN_T, D_MODEL), lambda i: (0, jnp.minimum(main(i), n_chunk - 1), 0)),
            pl.BlockSpec((nb, SCAN_T, D_MODEL), lambda i: (0, jnp.clip(main(i) - 1, 0, n_chunk - 1), 0)),
            pl.BlockSpec((nb, N_SSM_GROUPS, SSM_STATE), lambda i: (0, 0, 0)),
            pl.BlockSpec((nb, N_SSM_GROUPS, SSM_STATE), lambda i: (0, 0, 0)),
            pl.BlockSpec((nb, CONV_K - 1, CONV_WIDTH), lambda i: (0, 0, 0)),
            pl.BlockSpec((n_dec, D_MODEL), lambda i: (0, 0)),
            pl.BlockSpec((n_dec, SSM_WIDTH), lambda i: (0, 0)),
            pl.BlockSpec((n_dec, CONV_WIDTH), lambda i: (0, 0)),
            pl.BlockSpec((n_dec, CONV_WIDTH), lambda i: (0, 0)),
        ],
        out_shape=[
            jax.ShapeDtypeStruct((nb, seq, D_MODEL), f32),
            jax.ShapeDtypeStruct((nb, seq, D_MODEL), bf),
            jax.ShapeDtypeStruct((nb, N_SSM_GROUPS, SSM_STATE), f32),
            jax.ShapeDtypeStruct((nb, N_SSM_GROUPS, SSM_STATE), f32),
            jax.ShapeDtypeStruct((nb, CONV_K - 1, CONV_WIDTH), f32),
            jax.ShapeDtypeStruct((n_dec, D_MODEL), f32),
            jax.ShapeDtypeStruct((n_dec, SSM_WIDTH), f32),
            jax.ShapeDtypeStruct((n_dec, CONV_WIDTH), f32),
            jax.ShapeDtypeStruct((n_dec, CONV_WIDTH), f32),
        ],
        scratch_shapes=[
            *[pltpu.VMEM((SLABS_PER_BLK, prow, LANES), f32) for _ in range(N_BLK)],
            pltpu.VMEM((N_BLK, nb * SCAN_T, LANES), f32),
            pltpu.VMEM((N_BLK, nb * SCAN_T, LANES), f32),
            pltpu.VMEM((prow, 2 * SSM_WIDTH), f32),
            pltpu.VMEM((prow, SSM_WIDTH), f32),
            pltpu.VMEM((prow, SSM_WIDTH), f32),
            pltpu.VMEM((N_BLK, SUBLANES + prow, LANES), f32),
            pltpu.VMEM((N_BLK, SUBLANES, LANES), f32),
            pltpu.VMEM((nb, SCAN_T + SUBLANES, CONV_WIDTH), f32),
            pltpu.VMEM((nb, SCAN_T, CONV_WIDTH), f32),
            pltpu.VMEM((2 * N_TILE, SUBLANES, LANES), f32),
            pltpu.VMEM((D_MODEL, D_FF), bf),
            pltpu.VMEM((D_MODEL, D_FF), bf),
            pltpu.VMEM((D_FF, D_MODEL), bf),
            pltpu.VMEM((D_MODEL, IN_PROJ_WIDTH), bf),
            pltpu.VMEM((nb * SCAN_T, D_FF), bf),
        ],
        compiler_params=pltpu.CompilerParams(
            dimension_semantics=("arbitrary",), vmem_limit_bytes=VMEM_LIMIT),
        name="ffn1_mixer",
    )(x, xd, norms, wg, wu, wd, win, a2_re, a2_im, wb, wct, wdm, vecs, wglu)


def _ffn2_rows(h, mix, weights, a_ref):
    wout_ref, n2_ref, wg_ref, wu_ref, wd_ref, nf_ref = weights
    h = h + jnp.dot(mix, wout_ref[...], preferred_element_type=jnp.float32)
    hn = _rms(h, n2_ref[...]).astype(jnp.bfloat16)
    h = h + 0.5 * _swiglu_staged(hn, wg_ref, wu_ref, wd_ref, a_ref)
    return _rms(h, nf_ref[...])


def _ffn2_kernel(h_ref, mix_ref, hd_ref, mixd_ref, wout_c, wg_c, wu_c, wd_c, norms_ref,
                 y_ref, yd_ref, wout_ref, wg_ref, wu_ref, wd_ref, a0_ref, a1_ref):
    n2_ref, nf_ref = norms_ref.at[NORM_FFN2:NORM_FFN2 + 1], norms_ref.at[NORM_FINAL:NORM_FINAL + 1]
    step = pl.program_id(0)
    last = pl.num_programs(0) - 1
    weights = (wout_ref, n2_ref, wg_ref, wu_ref, wd_ref, nf_ref)

    @pl.when(step < W2_STEPS)
    def _load_weights():
        _cast_weight_chunks(step, ((wout_c, wout_ref), (wg_c, wg_ref), (wu_c, wu_ref), (wd_c, wd_ref)))

    @pl.when((step >= W2_STEPS) & (step < last))
    def _prompt_tile():
        assert h_ref.shape[0] == 2 * FFN_SUB
        ra, rb = slice(0, FFN_SUB), slice(FFN_SUB, 2 * FFN_SUB)
        h_a = h_ref[ra, :] + jnp.dot(mix_ref[ra, :], wout_ref[...], preferred_element_type=jnp.float32)
        h_b = h_ref[rb, :] + jnp.dot(mix_ref[rb, :], wout_ref[...], preferred_element_type=jnp.float32)
        hn_a = _rms(h_a, n2_ref[...]).astype(jnp.bfloat16)
        _swiglu_stage(hn_a, wg_ref, wu_ref, a0_ref, 0, FFN_TF)
        hn_b = _rms(h_b, n2_ref[...]).astype(jnp.bfloat16)
        _swiglu_stage(hn_a, wg_ref, wu_ref, a0_ref, FFN_TF, D_FF)
        f_a = jnp.dot(a0_ref[...], wd_ref[...], preferred_element_type=jnp.float32)
        _swiglu_stage(hn_b, wg_ref, wu_ref, a1_ref, 0, FFN_TF)
        y_ref[ra, :] = _rms(h_a + 0.5 * f_a, nf_ref[...])
        _swiglu_stage(hn_b, wg_ref, wu_ref, a1_ref, FFN_TF, D_FF)
        f_b = jnp.dot(a1_ref[...], wd_ref[...], preferred_element_type=jnp.float32)
        y_ref[rb, :] = _rms(h_b + 0.5 * f_b, nf_ref[...])

    @pl.when(step == last)
    def _decode_rows():
        yd_ref[...] = _ffn2_rows(hd_ref[...], mixd_ref[...], weights, a0_ref)


def _ffn2(h, mix, hd, mixd, wout, wg, wu, wd, norms):
    nb, seq, _ = mix.shape
    n_dec = mixd.shape[0]
    n_tiles = nb * seq // FFN_TM
    h = h.reshape(nb * seq, D_MODEL)
    mix = mix.reshape(nb * seq, D_MODEL)
    tile = lambda i: (jnp.clip(i - W2_STEPS, 0, n_tiles - 1), 0)
    yp, yd = pl.pallas_call(
        _ffn2_kernel,
        grid=(W2_STEPS + n_tiles + 1,),
        in_specs=[
            pl.BlockSpec((FFN_TM, D_MODEL), tile),
            pl.BlockSpec((FFN_TM, D_MODEL), tile),
            _whole(hd.shape),
            _whole(mixd.shape),
            _weight_chunk((D_MODEL // W2_STEPS, D_MODEL), W2_STEPS),
            _weight_chunk((D_MODEL // W2_STEPS, D_FF), W2_STEPS),
            _weight_chunk((D_MODEL // W2_STEPS, D_FF), W2_STEPS),
            _weight_chunk((D_FF // W2_STEPS, D_MODEL), W2_STEPS),
            _whole(norms.shape),
        ],
        out_specs=[
            pl.BlockSpec((FFN_TM, D_MODEL), tile),
            pl.BlockSpec((n_dec, D_MODEL), lambda i: (0, 0)),
        ],
        out_shape=[
            jax.ShapeDtypeStruct((nb * seq, D_MODEL), jnp.float32),
            jax.ShapeDtypeStruct((n_dec, D_MODEL), jnp.float32),
        ],
        scratch_shapes=[
            pltpu.VMEM((D_MODEL, D_MODEL), jnp.bfloat16),
            pltpu.VMEM((D_MODEL, D_FF), jnp.bfloat16),
            pltpu.VMEM((D_MODEL, D_FF), jnp.bfloat16),
            pltpu.VMEM((D_FF, D_MODEL), jnp.bfloat16),
            pltpu.VMEM((FFN_SUB, D_FF), jnp.bfloat16),
            pltpu.VMEM((FFN_SUB, D_FF), jnp.bfloat16),
        ],
        input_output_aliases={0: 0},
        compiler_params=pltpu.CompilerParams(
            dimension_semantics=("arbitrary",), vmem_limit_bytes=VMEM_LIMIT),
        name="outproj_ffn2",
    )(h, mix, hd, mixd, wout, wg, wu, wd, norms)
    return yp.reshape(nb, seq, D_MODEL), yd


def _mixer_step_kernel(u_ref, v_ref, g_ref, hre_ref, him_ref, buf_ref, are_ref, aim_ref, bb_ref, wct_ref,
                       vecs_ref, wglu_ref, mix_ref, sre_ref, sim_ref, nconv_ref):
    d_ref, bglu_ref, cw_ref = _ssm_vecs(vecs_ref)
    u = u_ref[...]
    u_bf = u.astype(jnp.bfloat16)
    ys = []
    for k in range(N_BLK):
        bu = jnp.dot(u_bf[:, k * LANES:(k + 1) * LANES], bb_ref[k], preferred_element_type=jnp.float32)
        st = slice(k * BLK_STATE, (k + 1) * BLK_STATE)
        a_re = are_ref[k:k + 1, :]
        a_im = aim_ref[k:k + 1, :]
        h_re = hre_ref[:, st]
        h_im = him_ref[:, st]
        x_re = (a_re * h_re + bu[:, :BLK_STATE]) - a_im * h_im
        x_im = (a_re * h_im + bu[:, BLK_STATE:]) + a_im * h_re
        sre_ref[:, st] = x_re
        sim_ref[:, st] = x_im
        x = jnp.concatenate([x_re, x_im], axis=1).astype(jnp.bfloat16)
        ys.append(lax.dot_general(x, wct_ref[k, 0:LANES, :], _NT, preferred_element_type=jnp.float32))
    y = jnp.concatenate(ys, axis=1)
    mix_ref[:, 0:SSM_WIDTH] = _glu_tail(y, u, d_ref, wglu_ref, bglu_ref).astype(mix_ref.dtype)

    v = v_ref[...]
    z = cw_ref[0:1, :] * buf_ref[0] + cw_ref[1:2, :] * buf_ref[1] + cw_ref[2:3, :] * v
    mix_ref[:, SSM_WIDTH:] = (g_ref[...] * z).astype(mix_ref.dtype)
    nconv_ref[0] = buf_ref[1]
    nconv_ref[1] = v


def _mixer_step(u, v, g, h_re, h_im, buf, a_re, a_im, bb, wct, vecs, wglu):
    nb = u.shape[0]
    return pl.pallas_call(
        _mixer_step_kernel,
        out_shape=[
            jax.ShapeDtypeStruct((nb, D_MODEL), jnp.bfloat16),
            jax.ShapeDtypeStruct((nb, N_STATE), jnp.float32),
            jax.ShapeDtypeStruct((nb, N_STATE), jnp.float32),
            jax.ShapeDtypeStruct((CONV_K - 1, nb, CONV_WIDTH), jnp.float32),
        ],
        compiler_params=pltpu.CompilerParams(vmem_limit_bytes=VMEM_LIMIT),
        name="mixer_step",
    )(u, v, g, h_re, h_im, buf, a_re, a_im, bb, wct, vecs, wglu)


def _ssm_layout(lam_re, lam_im, log_dt, b_re, b_im, c_re, c_im):
    gpb = N_SSM_GROUPS // N_BLK
    lr = lam_re.reshape(N_BLK, BLK_STATE)
    li = lam_im.reshape(N_BLK, BLK_STATE)
    ldt = jnp.broadcast_to(log_dt[:, None], (N_SSM_GROUPS, SSM_STATE)).reshape(N_BLK, BLK_STATE)

    def b_rows(b):
        return b.reshape(N_BLK, gpb, SSM_STATE, SSM_GROUP).transpose(0, 1, 3, 2).reshape(N_BLK, LANES, SSM_STATE)

    def c_rows(c):
        return c.reshape(N_BLK, LANES, SSM_STATE)

    return lr, li, ldt, b_rows(b_re), b_rows(b_im), c_rows(c_re), c_rows(c_im)


def kernel(x_prompt, x_sample, state_ssm_re, state_ssm_im, state_conv, ffn1_norm, ffn1_w_gate, ffn1_w_up, ffn1_w_down, mix_norm, w_in, ssm_lambda_re, ssm_lambda_im, ssm_log_dt, ssm_b_re, ssm_b_im, ssm_c_re, ssm_c_im, ssm_d, ssm_w_glu, ssm_b_glu, conv_w, w_out, ffn2_norm, ffn2_w_gate, ffn2_w_up, ffn2_w_down, final_norm):
    assert ffn1_norm.shape[0] == 1, "single-layer trunk"
    n_prompt, seq, _ = x_prompt.shape
    n_dec, dec_seq, _ = x_sample.shape
    assert dec_seq == 1 and n_prompt == SUBLANES and seq % FFN_TM == 0 and seq % SCAN_T == 0
    assert n_dec % SCAN_T == 0 and n_dec <= n_prompt * SCAN_T and FFN_TM % FFN_SUB == 0

    norms = jnp.concatenate([ffn1_norm, mix_norm, ffn2_norm, final_norm[None, :]], axis=0)
    vecs = jnp.concatenate([ssm_d, ssm_b_glu, conv_w[0],
                            jnp.zeros((SUBLANES - 2 - CONV_K, SSM_WIDTH), jnp.float32)], axis=0)
    wglu = ssm_w_glu[0].astype(jnp.bfloat16)
    layout = _ssm_layout(ssm_lambda_re[0], ssm_lambda_im[0], ssm_log_dt[0], ssm_b_re[0], ssm_b_im[0],
                         ssm_c_re[0], ssm_c_im[0])
    a_re, a_im, a2_re, a2_im, bb, wb, wct, wdm = _discretize(*layout)

    hp, mixp, p_re, p_im, p_conv, hd, ud, vd, gd = _ffn1_mixer(
        x_prompt, x_sample.reshape(n_dec, D_MODEL), norms, ffn1_w_gate[0], ffn1_w_up[0], ffn1_w_down[0],
        w_in[0], a2_re, a2_im, wb, wct, wdm, vecs, wglu)

    s_re0 = state_ssm_re[0].reshape(n_dec, N_STATE)
    s_im0 = state_ssm_im[0].reshape(n_dec, N_STATE)
    buf = state_conv[0].transpose(1, 0, 2)
    mixd, s_re, s_im, s_conv = _mixer_step(ud, vd, gd, s_re0, s_im0, buf, a_re, a_im, bb, wct, vecs, wglu)

    yp, yd = _ffn2(hp, mixp, hd, mixd, w_out[0], ffn2_w_gate[0], ffn2_w_up[0], ffn2_w_down[0], norms)

    gs = (N_SSM_GROUPS, SSM_STATE)
    return (yp,
            yd.reshape(n_dec, 1, D_MODEL),
            p_re[None],
            p_im[None],
            p_conv[None],
            s_re.reshape(1, n_dec, *gs),
            s_im.reshape(1, n_dec, *gs),
            s_conv.transpose(1, 0, 2)[None])
```

```python
import jax
import jax.numpy as jnp
from jax import lax
from jax.experimental import pallas as pl
from jax.experimental.pallas import tpu as pltpu

D_MODEL = 1024
D_FF = 2816
SSM_WIDTH = 512
CONV_WIDTH = 512
SSM_GROUP = 16
N_SSM_GROUPS = 32
SSM_STATE = 64
CONV_K = 3
IN_PROJ_WIDTH = SSM_WIDTH + 3 * CONV_WIDTH
EPS = 1e-6

LANES = 128
SUBLANES = 8
N_STATE = N_SSM_GROUPS * SSM_STATE
N_BLK = SSM_WIDTH // LANES
BLK_STATE = N_STATE // N_BLK
SLABS_PER_BLK = 2 * BLK_STATE // LANES
HALF = SLABS_PER_BLK // 2
N_TILE = N_BLK * HALF

FFN_TM = 1024
FFN_SUB = 512
W1_STEPS = 16
W2_STEPS = 8
FFN_TF = 256
SCAN_T = 64
N_PAIR = SCAN_T // 2
PAIR_PITCH = N_PAIR + 4
VMEM_LIMIT = 62 * 1024 * 1024

_NT = (((1,), (1,)), ((), ()))


def _rms(x, g):
    r = lax.rsqrt(jnp.mean(x * x, axis=-1, keepdims=True) + EPS)
    return x * r * g


def _bdot(a, b):
    return jnp.dot(a.astype(jnp.bfloat16), b.astype(jnp.bfloat16), preferred_element_type=jnp.float32)


def _swiglu_stage(xn, wg_ref, wu_ref, a_ref, lo, hi):
    rows = xn.shape[0]
    for j in range(lo, hi, FFN_TF):
        g = jnp.dot(xn, wg_ref[:, j:j + FFN_TF], preferred_element_type=jnp.float32)
        u = jnp.dot(xn, wu_ref[:, j:j + FFN_TF], preferred_element_type=jnp.float32)
        a_ref[0:rows, j:j + FFN_TF] = ((g * jax.nn.sigmoid(g)) * u).astype(a_ref.dtype)


def _swiglu_staged(xn, wg_ref, wu_ref, wd_ref, a_ref):
    rows = xn.shape[0]
    _swiglu_stage(xn, wg_ref, wu_ref, a_ref, 0, D_FF)
    return jnp.dot(a_ref[0:rows, :], wd_ref[...], preferred_element_type=jnp.float32)


def _cast_weight_chunks(step, pairs):
    for src, dst in pairs:
        n = src.shape[0]
        dst[pl.ds(pl.multiple_of(step * n, n), n), :] = src[...].astype(dst.dtype)


def _whole(shape):
    return pl.BlockSpec(shape, lambda i: (0,) * len(shape), pipeline_mode=pl.Buffered(1))


def _weight_chunk(shape, n_steps):
    return pl.BlockSpec(shape, lambda i: (jnp.minimum(i, n_steps - 1), 0))


def _glu_tail(y, u, d_ref, wglu_ref, bglu_ref):
    y = y + d_ref[...] * u
    z = jax.nn.gelu(y)
    gate = jax.nn.sigmoid(_bdot(z, wglu_ref[...]) + bglu_ref[...])
    return z * gate


def _discretize_kernel(lr_ref, li_ref, ldt_ref, bre_ref, bim_ref, ctre_ref, ctim_ref,
                       are_ref, aim_ref, a2re_ref, a2im_ref, bb_ref, wb_ref, wct_ref, wdm_ref):
    row_group = lax.broadcasted_iota(jnp.int32, (LANES, BLK_STATE), 0) // SSM_GROUP
    col_group = lax.broadcasted_iota(jnp.int32, (LANES, BLK_STATE), 1) // SSM_STATE
    on_diagonal = row_group == col_group

    def blocks(m):
        pair = jnp.concatenate([m, m], axis=1)
        return jnp.where(on_diagonal, jnp.concatenate([pair] * (BLK_STATE // LANES), axis=1), 0.0)

    lr, li = lr_ref[...], li_ref[...]
    dt = jnp.exp(ldt_ref[...])
    mag = jnp.exp(lr * dt)
    ab_re = mag * jnp.cos(li * dt)
    ab_im = mag * jnp.sin(li * dt)
    den = lr * lr + li * li
    nr = ab_re - 1.0
    ni = ab_im
    coef_re = (nr * lr + ni * li) / den
    coef_im = (ni * lr - nr * li) / den
    are_ref[...] = ab_re
    aim_ref[...] = ab_im
    a2_re = ab_re * ab_re - ab_im * ab_im
    a2_im = 2.0 * (ab_re * ab_im)
    bf = jnp.bfloat16
    for k in range(N_BLK):
        ar, ai = ab_re[k:k + 1, :], ab_im[k:k + 1, :]
        for j in range(HALF):
            lanes = slice(j * LANES, (j + 1) * LANES)
            a2re_ref[k * HALF + j] = jnp.broadcast_to(a2_re[k:k + 1, lanes], (SUBLANES, LANES))
            a2im_ref[k * HALF + j] = jnp.broadcast_to(a2_im[k:k + 1, lanes], (SUBLANES, LANES))
        cr, ci = coef_re[k:k + 1, :], coef_im[k:k + 1, :]
        bre, bim = blocks(bre_ref[k]), blocks(bim_ref[k])
        bb_re = cr * bre - ci * bim
        bb_im = cr * bim + ci * bre
        bb_ref[k, :, 0:BLK_STATE] = bb_re.astype(bf)
        bb_ref[k, :, BLK_STATE:] = bb_im.astype(bf)
        wb_ref[k, 0:LANES, 0:BLK_STATE] = (ar * bb_re - ai * bb_im).astype(bf)
        wb_ref[k, 0:LANES, BLK_STATE:] = (ar * bb_im + ai * bb_re).astype(bf)
        wb_ref[k, LANES:, 0:BLK_STATE] = bb_re.astype(bf)
        wb_ref[k, LANES:, BLK_STATE:] = bb_im.astype(bf)
        ctre, ctim = blocks(ctre_ref[k]), blocks(ctim_ref[k])
        wct_ref[k, 0:LANES, 0:BLK_STATE] = ctre.astype(bf)
        wct_ref[k, 0:LANES, BLK_STATE:] = (-ctim).astype(bf)
        wct_ref[k, LANES:, 0:BLK_STATE] = (ctre * ar - ctim * ai).astype(bf)
        wct_ref[k, LANES:, BLK_STATE:] = (-(ctre * ai + ctim * ar)).astype(bf)
        wdm = (lax.dot_general(bb_re, ctre, _NT, precision=lax.Precision.HIGHEST,
                               preferred_element_type=jnp.float32)
               - lax.dot_general(bb_im, ctim, _NT, precision=lax.Precision.HIGHEST,
                                 preferred_element_type=jnp.float32))
        wdm_ref[k] = wdm.astype(bf)


def _discretize(lr, li, ldt, bre, bim, ctre, ctim):
    f32, bf = jnp.float32, jnp.bfloat16
    return pl.pallas_call(
        _discretize_kernel,
        out_shape=[
            jax.ShapeDtypeStruct((N_BLK, BLK_STATE), f32),
            jax.ShapeDtypeStruct((N_BLK, BLK_STATE), f32),
            jax.ShapeDtypeStruct((N_TILE, SUBLANES, LANES), f32),
            jax.ShapeDtypeStruct((N_TILE, SUBLANES, LANES), f32),
            jax.ShapeDtypeStruct((N_BLK, LANES, 2 * BLK_STATE), bf),
            jax.ShapeDtypeStruct((N_BLK, 2 * LANES, 2 * BLK_STATE), bf),
            jax.ShapeDtypeStruct((N_BLK, 2 * LANES, 2 * BLK_STATE), bf),
            jax.ShapeDtypeStruct((N_BLK, LANES, LANES), bf),
        ],
        name="ssm_discretize",
    )(lr, li, ldt, bre, bim, ctre, ctim)


def _ffn1_mixer_kernel(x_ref, xd_ref, n1_ref, wg_c, wu_c, wd_c, nm_ref, win_c,
                       a2re_ref, a2im_ref, wb_ref, wct_ref, wdm_ref, d_ref, wglu_ref, bglu_ref, cw_ref,
                       h_ref, mix_ref, sre_ref, sim_ref, nconv_ref, ud_ref, vd_ref, gd_ref,
                       xs0_ref, xs1_ref, xs2_ref, xs3_ref, us_ref, ys_ref, l_ref, yo_ref, ye_ref, e_ref,
                       ecarry_ref, v_ref, g_ref, state_ref, wg_ref, wu_ref, wd_ref, win_ref, a_ref):
    xs_refs = (xs0_ref, xs1_ref, xs2_ref, xs3_ref)
    step = pl.program_id(0)
    last = pl.num_programs(0) - 1
    nb, t_len, _ = x_ref.shape
    n_dec = xd_ref.shape[0]
    prow = nb * PAIR_PITCH
    tiles = [(k, j) for k in range(N_BLK) for j in range(HALF)]

    @pl.when(step < W1_STEPS)
    def _load_weights():
        _cast_weight_chunks(step, ((wg_c, wg_ref), (wu_c, wu_ref), (wd_c, wd_ref), (win_c, win_ref)))

    @pl.when(step == 0)
    def _init():
        for ref in (state_ref, us_ref, l_ref, e_ref, ecarry_ref, v_ref, g_ref):
            ref[...] = jnp.zeros_like(ref)

    @pl.when(step >= W1_STEPS)
    def _main():
        carry = {}

        def build_lhs():
            for b in range(nb):
                for k in range(N_BLK):
                    for par in range(2):
                        col = (2 * k + par) * LANES
                        l_ref[pl.ds(b * PAIR_PITCH, N_PAIR), col:col + LANES] = (
                            us_ref[k, pl.ds(b * t_len + par, N_PAIR, stride=2), :])

        def b_proj(ks):
            for k in ks:
                lhs = l_ref[:, 2 * k * LANES:2 * (k + 1) * LANES].astype(jnp.bfloat16)
                v = jnp.dot(lhs, wb_ref[k], preferred_element_type=jnp.float32)
                for j in range(SLABS_PER_BLK):
                    xs_refs[k][j] = v[:, j * LANES:(j + 1) * LANES]

        def scan(r0, r1):
            if r0 == 0:
                carry["re"] = [state_ref[k * SLABS_PER_BLK + j] for k, j in tiles]
                carry["im"] = [state_ref[k * SLABS_PER_BLK + HALF + j] for k, j in tiles]
            xr, xi = carry["re"], carry["im"]
            for r in range(r0, r1):
                rows = pl.ds(r, SUBLANES, stride=PAIR_PITCH)
                for n, (k, j) in enumerate(tiles):
                    a_re = a2re_ref[k * HALF + j]
                    a_im = a2im_ref[k * HALF + j]
                    nr = (a_re * xr[n] + xs_refs[k][j, rows, :]) - a_im * xi[n]
                    ni = (a_re * xi[n] + xs_refs[k][HALF + j, rows, :]) + a_im * xr[n]
                    xs_refs[k][j, rows, :] = nr
                    xs_refs[k][HALF + j, rows, :] = ni
                    xr[n], xi[n] = nr, ni
            if r1 == N_PAIR:
                for n, (k, j) in enumerate(tiles):
                    state_ref[k * SLABS_PER_BLK + j] = xr[n]
                    state_ref[k * SLABS_PER_BLK + HALF + j] = xi[n]

        def c_proj(ks):
            for k in ks:
                lanes = slice(k * LANES, (k + 1) * LANES)
                x = jnp.concatenate([xs_refs[k][j] for j in range(SLABS_PER_BLK)], axis=1)
                ce = lax.dot_general(x.astype(jnp.bfloat16), wct_ref[k], _NT,
                                     preferred_element_type=jnp.float32)
                yo_ref[:, lanes] = ce[:, 0:LANES]
                e_ref[k, SUBLANES:SUBLANES + prow, :] = ce[:, LANES:]
                nxt = e_ref[k, pl.ds(SUBLANES + N_PAIR - 1, nb, stride=PAIR_PITCH), :]
                e_ref[k, pl.ds(SUBLANES - 1, nb, stride=PAIR_PITCH), :] = ecarry_ref[k]
                ecarry_ref[k] = nxt
                u_even = l_ref[:, 2 * k * LANES:(2 * k + 1) * LANES].astype(jnp.bfloat16)
                ye_ref[:, lanes] = (e_ref[k, SUBLANES - 1:SUBLANES - 1 + prow, :]
                                    + jnp.dot(u_even, wdm_ref[k], preferred_element_type=jnp.float32))

        def glu_emit():
            for b in range(nb):
                for k in range(N_BLK):
                    lanes = slice(k * LANES, (k + 1) * LANES)
                    for par, src in ((0, ye_ref), (1, yo_ref)):
                        ys_ref[k, pl.ds(b * t_len + par, N_PAIR, stride=2), :] = (
                            src[pl.ds(b * PAIR_PITCH, N_PAIR), lanes])
            y = jnp.concatenate([ys_ref[k] for k in range(N_BLK)], axis=1)
            u = jnp.concatenate([us_ref[k] for k in range(N_BLK)], axis=1)
            s_out = _glu_tail(y, u, d_ref, wglu_ref, bglu_ref)
            mix_ref[:, :, 0:SSM_WIDTH] = s_out.reshape(nb, t_len, SSM_WIDTH).astype(mix_ref.dtype)

        def conv():
            v = v_ref[:, SUBLANES:SUBLANES + t_len, :]
            z = cw_ref[0:1, :] * v_ref[:, SUBLANES - 2:SUBLANES - 2 + t_len, :]
            z = z + cw_ref[1:2, :] * v_ref[:, SUBLANES - 1:SUBLANES - 1 + t_len, :]
            z = z + cw_ref[2:3, :] * v
            mix_ref[:, :, SSM_WIDTH:] = (g_ref[...] * z).astype(mix_ref.dtype)
            v_ref[:, 0:SUBLANES, :] = v_ref[:, t_len:t_len + SUBLANES, :]

        q = N_PAIR // 4
        pieces = [
            lambda: (conv(), build_lhs(), b_proj((0,))), lambda: b_proj((1,)), lambda: b_proj((2,)),
            lambda: b_proj((3,)),
            lambda: scan(0, q), lambda: scan(q, 2 * q), lambda: scan(2 * q, 3 * q), lambda: scan(3 * q, N_PAIR),
            lambda: c_proj((0, 1)), lambda: c_proj((2, 3)),
            glu_emit,
        ]
        assert len(pieces) <= D_FF // FFN_TF

        db = n_dec // t_len
        x_top = jnp.where(step == last, xd_ref[...], x_ref[0:db].reshape(n_dec, D_MODEL))
        x = jnp.concatenate([x_top, x_ref[db:].reshape((nb - db) * t_len, D_MODEL)], axis=0)
        xn = _rms(x, n1_ref[...]).astype(jnp.bfloat16)
        for c, j in enumerate(range(0, D_FF, FFN_TF)):
            _swiglu_stage(xn, wg_ref, wu_ref, a_ref, j, j + FFN_TF)
            if c < len(pieces):
                pieces[c]()
        h = x + 0.5 * jnp.dot(a_ref[...], wd_ref[...], preferred_element_type=jnp.float32)
        hn = _rms(h, nm_ref[...]).astype(jnp.bfloat16)
        p = jnp.dot(hn, win_ref[...], preferred_element_type=jnp.float32)
        h_ref[...] = h.reshape(nb, t_len, D_MODEL)
        v_new = p[:, SSM_WIDTH + 2 * CONV_WIDTH:] * p[:, SSM_WIDTH:SSM_WIDTH + CONV_WIDTH]
        v_ref[:, SUBLANES:SUBLANES + t_len, :] = v_new.reshape(nb, t_len, CONV_WIDTH)
        g_ref[...] = p[:, SSM_WIDTH + CONV_WIDTH:SSM_WIDTH + 2 * CONV_WIDTH].reshape(nb, t_len, CONV_WIDTH)
        for k in range(N_BLK):
            us_ref[k] = p[:, k * LANES:(k + 1) * LANES]

    @pl.when(step == last)
    def _final():
        for n, (k, j) in enumerate(tiles):
            for out_ref, slab in ((sre_ref, k * SLABS_PER_BLK + j), (sim_ref, k * SLABS_PER_BLK + HALF + j)):
                tile = state_ref[slab]
                for half in range(LANES // SSM_STATE):
                    group = (k * BLK_STATE + j * LANES) // SSM_STATE + half
                    out_ref[:, group, :] = tile[:, half * SSM_STATE:(half + 1) * SSM_STATE]
        nconv_ref[...] = v_ref[:, SUBLANES - 2:SUBLANES, :]
        for k in range(N_BLK):
            ud_ref[:, k * LANES:(k + 1) * LANES] = us_ref[k, 0:n_dec, :]
        for b in range(n_dec // t_len):
            vd_ref[b * t_len:(b + 1) * t_len, :] = v_ref[b, SUBLANES:SUBLANES + t_len, :]
            gd_ref[b * t_len:(b + 1) * t_len, :] = g_ref[b]


def _ffn1_mixer(x, xd, n1, wg, wu, wd, nm, win, a2_re, a2_im, wb, wct, wdm, d, wglu, bglu, cw):
    nb, seq, _ = x.shape
    n_dec = xd.shape[0]
    n_chunk = seq // SCAN_T
    prow = nb * PAIR_PITCH
    f32, bf = jnp.float32, jnp.bfloat16
    main = lambda i: jnp.maximum(i - W1_STEPS, 0)
    return pl.pallas_call(
        _ffn1_mixer_kernel,
        grid=(W1_STEPS + n_chunk + 1,),
        in_specs=[
            pl.BlockSpec((nb, SCAN_T, D_MODEL), lambda i: (0, jnp.minimum(main(i), n_chunk - 1), 0)),
            _whole((n_dec, D_MODEL)),
            _whole((1, D_MODEL)),
            _weight_chunk((D_MODEL // W1_STEPS, D_FF), W1_STEPS),
            _weight_chunk((D_MODEL // W1_STEPS, D_FF), W1_STEPS),
            _weight_chunk((D_FF // W1_STEPS, D_MODEL), W1_STEPS),
            _whole((1, D_MODEL)),
            _weight_chunk((D_MODEL // W1_STEPS, IN_PROJ_WIDTH), W1_STEPS),
            _whole((N_TILE, SUBLANES, LANES)),
            _whole((N_TILE, SUBLANES, LANES)),
            _whole((N_BLK, 2 * LANES, 2 * BLK_STATE)),
            _whole((N_BLK, 2 * LANES, 2 * BLK_STATE)),
            _whole((N_BLK, LANES, LANES)),
            _whole((1, SSM_WIDTH)),
            _whole((SSM_WIDTH, SSM_WIDTH)),
            _whole((1, SSM_WIDTH)),
            _whole((CONV_K, CONV_WIDTH)),
        ],
        out_specs=[
            pl.BlockSpec((nb, SCAN_T, D_MODEL), lambda i: (0, main(i), 0)),
            pl.BlockSpec((nb, SCAN_T, D_MODEL), lambda i: (0, jnp.clip(main(i) - 1, 0, n_chunk - 1), 0)),
            pl.BlockSpec((nb, N_SSM_GROUPS, SSM_STATE), lambda i: (0, 0, 0)),
            pl.BlockSpec((nb, N_SSM_GROUPS, SSM_STATE), lambda i: (0, 0, 0)),
            pl.BlockSpec((nb, CONV_K - 1, CONV_WIDTH), lambda i: (0, 0, 0)),
            pl.BlockSpec((n_dec, SSM_WIDTH), lambda i: (0, 0)),
            pl.BlockSpec((n_dec, CONV_WIDTH), lambda i: (0, 0)),
            pl.BlockSpec((n_dec, CONV_WIDTH), lambda i: (0, 0)),
        ],
        out_shape=[
            jax.ShapeDtypeStruct((nb, seq + SCAN_T, D_MODEL), f32),
            jax.ShapeDtypeStruct((nb, seq, D_MODEL), bf),
            jax.ShapeDtypeStruct((nb, N_SSM_GROUPS, SSM_STATE), f32),
            jax.ShapeDtypeStruct((nb, N_SSM_GROUPS, SSM_STATE), f32),
            jax.ShapeDtypeStruct((nb, CONV_K - 1, CONV_WIDTH), f32),
            jax.ShapeDtypeStruct((n_dec, SSM_WIDTH), f32),
            jax.ShapeDtypeStruct((n_dec, CONV_WIDTH), f32),
            jax.ShapeDtypeStruct((n_dec, CONV_WIDTH), f32),
        ],
        scratch_shapes=[
            *[pltpu.VMEM((SLABS_PER_BLK, prow, LANES), f32) for _ in range(N_BLK)],
            pltpu.VMEM((N_BLK, nb * SCAN_T, LANES), f32),
            pltpu.VMEM((N_BLK, nb * SCAN_T, LANES), f32),
            pltpu.VMEM((prow, 2 * SSM_WIDTH), f32),
            pltpu.VMEM((prow, SSM_WIDTH), f32),
            pltpu.VMEM((prow, SSM_WIDTH), f32),
            pltpu.VMEM((N_BLK, SUBLANES + prow, LANES), f32),
            pltpu.VMEM((N_BLK, SUBLANES, LANES), f32),
            pltpu.VMEM((nb, SCAN_T + SUBLANES, CONV_WIDTH), f32),
            pltpu.VMEM((nb, SCAN_T, CONV_WIDTH), f32),
            pltpu.VMEM((2 * N_TILE, SUBLANES, LANES), f32),
            pltpu.VMEM((D_MODEL, D_FF), bf),
            pltpu.VMEM((D_MODEL, D_FF), bf),
            pltpu.VMEM((D_FF, D_MODEL), bf),
            pltpu.VMEM((D_MODEL, IN_PROJ_WIDTH), bf),
            pltpu.VMEM((nb * SCAN_T, D_FF), bf),
        ],
        compiler_params=pltpu.CompilerParams(
            dimension_semantics=("arbitrary",), vmem_limit_bytes=VMEM_LIMIT),
        name="ffn1_mixer",
    )(x, xd, n1, wg, wu, wd, nm, win, a2_re, a2_im, wb, wct, wdm, d, wglu, bglu, cw)


def _ffn2_rows(h, mix, weights, a_ref):
    wout_ref, n2_ref, wg_ref, wu_ref, wd_ref, nf_ref = weights
    h = h + jnp.dot(mix, wout_ref[...], preferred_element_type=jnp.float32)
    hn = _rms(h, n2_ref[...]).astype(jnp.bfloat16)
    h = h + 0.5 * _swiglu_staged(hn, wg_ref, wu_ref, wd_ref, a_ref)
    return _rms(h, nf_ref[...])


def _ffn2_kernel(h_ref, mix_ref, hd0_ref, hd1_ref, mixd_ref, wout_c, wg_c, wu_c, wd_c, n2_ref, nf_ref,
                 y_ref, yd_ref, wout_ref, wg_ref, wu_ref, wd_ref, a0_ref, a1_ref):
    step = pl.program_id(0)
    last = pl.num_programs(0) - 1
    weights = (wout_ref, n2_ref, wg_ref, wu_ref, wd_ref, nf_ref)

    @pl.when(step < W2_STEPS)
    def _load_weights():
        _cast_weight_chunks(step, ((wout_c, wout_ref), (wg_c, wg_ref), (wu_c, wu_ref), (wd_c, wd_ref)))

    @pl.when((step >= W2_STEPS) & (step < last))
    def _prompt_tile():
        assert h_ref.shape[0] == 2 * FFN_SUB
        ra, rb = slice(0, FFN_SUB), slice(FFN_SUB, 2 * FFN_SUB)
        h_a = h_ref[ra, :] + jnp.dot(mix_ref[ra, :], wout_ref[...], preferred_element_type=jnp.float32)
        h_b = h_ref[rb, :] + jnp.dot(mix_ref[rb, :], wout_ref[...], preferred_element_type=jnp.float32)
        hn_a = _rms(h_a, n2_ref[...]).astype(jnp.bfloat16)
        _swiglu_stage(hn_a, wg_ref, wu_ref, a0_ref, 0, FFN_TF)
        hn_b = _rms(h_b, n2_ref[...]).astype(jnp.bfloat16)
        _swiglu_stage(hn_a, wg_ref, wu_ref, a0_ref, FFN_TF, D_FF)
        f_a = jnp.dot(a0_ref[...], wd_ref[...], preferred_element_type=jnp.float32)
        _swiglu_stage(hn_b, wg_ref, wu_ref, a1_ref, 0, FFN_TF)
        y_ref[ra, :] = _rms(h_a + 0.5 * f_a, nf_ref[...])
        _swiglu_stage(hn_b, wg_ref, wu_ref, a1_ref, FFN_TF, D_FF)
        f_b = jnp.dot(a1_ref[...], wd_ref[...], preferred_element_type=jnp.float32)
        y_ref[rb, :] = _rms(h_b + 0.5 * f_b, nf_ref[...])

    @pl.when(step == last)
    def _decode_rows():
        h_d = jnp.concatenate([hd0_ref[...], hd1_ref[...]], axis=0)
        yd_ref[...] = _ffn2_rows(h_d, mixd_ref[...], weights, a0_ref)


def _ffn2(h_all, mix, mixd, wout, n2, wg, wu, wd, nf):
    nb, seq, _ = mix.shape
    seq_all = h_all.shape[1]
    n_dec = mixd.shape[0]
    tail = seq_all - seq
    assert n_dec == 2 * tail
    per_b = seq // FFN_TM
    n_tiles = nb * per_b
    h_all = h_all.reshape(nb * seq_all, D_MODEL)
    mix = mix.reshape(nb * seq, D_MODEL)
    idx = lambda i: jnp.clip(i - W2_STEPS, 0, n_tiles - 1)
    tile = lambda i: ((idx(i) % nb) * per_b + idx(i) // nb, 0)
    assert seq_all % SUBLANES == 0
    tile_all = lambda i: (pl.multiple_of((idx(i) % nb) * seq_all + (idx(i) // nb) * FFN_TM, SUBLANES), 0)
    yp, yd = pl.pallas_call(
        _ffn2_kernel,
        grid=(W2_STEPS + n_tiles + 1,),
        in_specs=[
            pl.BlockSpec((pl.Element(FFN_TM), pl.Element(D_MODEL)), tile_all),
            pl.BlockSpec((FFN_TM, D_MODEL), tile),
            pl.BlockSpec((pl.Element(tail), pl.Element(D_MODEL)), lambda i: (seq, 0),
                         pipeline_mode=pl.Buffered(1)),
            pl.BlockSpec((pl.Element(tail), pl.Element(D_MODEL)), lambda i: (seq_all + seq, 0),
                         pipeline_mode=pl.Buffered(1)),
            _whole(mixd.shape),
            _weight_chunk((D_MODEL // W2_STEPS, D_MODEL), W2_STEPS),
            _weight_chunk((D_MODEL // W2_STEPS, D_FF), W2_STEPS),
            _weight_chunk((D_MODEL // W2_STEPS, D_FF), W2_STEPS),
            _weight_chunk((D_FF // W2_STEPS, D_MODEL), W2_STEPS),
            _whole((1, D_MODEL)),
            _whole((1, D_MODEL)),
        ],
        out_specs=[
            pl.BlockSpec((FFN_TM, D_MODEL), tile),
            pl.BlockSpec((n_dec, D_MODEL), lambda i: (0, 0)),
        ],
        out_shape=[
            jax.ShapeDtypeStruct((nb * seq, D_MODEL), jnp.float32),
            jax.ShapeDtypeStruct((n_dec, D_MODEL), jnp.float32),
        ],
        scratch_shapes=[
            pltpu.VMEM((D_MODEL, D_MODEL), jnp.bfloat16),
            pltpu.VMEM((D_MODEL, D_FF), jnp.bfloat16),
            pltpu.VMEM((D_MODEL, D_FF), jnp.bfloat16),
            pltpu.VMEM((D_FF, D_MODEL), jnp.bfloat16),
            pltpu.VMEM((FFN_SUB, D_FF), jnp.bfloat16),
            pltpu.VMEM((FFN_SUB, D_FF), jnp.bfloat16),
        ],
        compiler_params=pltpu.CompilerParams(
            dimension_semantics=("arbitrary",), vmem_limit_bytes=VMEM_LIMIT),
        name="outproj_ffn2",
    )(h_all, mix, h_all, h_all, mixd, wout, wg, wu, wd, n2, nf)
    return yp.reshape(nb, seq, D_MODEL), yd


def _mixer_step_kernel(u_ref, v_ref, g_ref, hre_ref, him_ref, buf_ref, are_ref, aim_ref, bb_ref, wct_ref,
                       d_ref, wglu_ref, bglu_ref, cw_ref, mix_ref, sre_ref, sim_ref, nconv_ref):
    u = u_ref[...]
    u_bf = u.astype(jnp.bfloat16)
    ys = []
    for k in range(N_BLK):
        bu = jnp.dot(u_bf[:, k * LANES:(k + 1) * LANES], bb_ref[k], preferred_element_type=jnp.float32)
        st = slice(k * BLK_STATE, (k + 1) * BLK_STATE)
        a_re = are_ref[k:k + 1, :]
        a_im = aim_ref[k:k + 1, :]
        h_re = hre_ref[:, st]
        h_im = him_ref[:, st]
        x_re = (a_re * h_re + bu[:, :BLK_STATE]) - a_im * h_im
        x_im = (a_re * h_im + bu[:, BLK_STATE:]) + a_im * h_re
        sre_ref[:, st] = x_re
        sim_ref[:, st] = x_im
        x = jnp.concatenate([x_re, x_im], axis=1).astype(jnp.bfloat16)
        ys.append(lax.dot_general(x, wct_ref[k, 0:LANES, :], _NT, preferred_element_type=jnp.float32))
    y = jnp.concatenate(ys, axis=1)
    mix_ref[:, 0:SSM_WIDTH] = _glu_tail(y, u, d_ref, wglu_ref, bglu_ref).astype(mix_ref.dtype)

    v = v_ref[...]
    z = cw_ref[0:1, :] * buf_ref[0] + cw_ref[1:2, :] * buf_ref[1] + cw_ref[2:3, :] * v
    mix_ref[:, SSM_WIDTH:] = (g_ref[...] * z).astype(mix_ref.dtype)
    nconv_ref[0] = buf_ref[1]
    nconv_ref[1] = v


def _mixer_step(u, v, g, h_re, h_im, buf, a_re, a_im, bb, wct, d, wglu, bglu, cw):
    nb = u.shape[0]
    return pl.pallas_call(
        _mixer_step_kernel,
        out_shape=[
            jax.ShapeDtypeStruct((nb, D_MODEL), jnp.bfloat16),
            jax.ShapeDtypeStruct((nb, N_STATE), jnp.float32),
            jax.ShapeDtypeStruct((nb, N_STATE), jnp.float32),
            jax.ShapeDtypeStruct((CONV_K - 1, nb, CONV_WIDTH), jnp.float32),
        ],
        compiler_params=pltpu.CompilerParams(vmem_limit_bytes=VMEM_LIMIT),
        name="mixer_step",
    )(u, v, g, h_re, h_im, buf, a_re, a_im, bb, wct, d, wglu, bglu, cw)


def _ssm_layout(lam_re, lam_im, log_dt, b_re, b_im, c_re, c_im):
    gpb = N_SSM_GROUPS // N_BLK
    lr = lam_re.reshape(N_BLK, BLK_STATE)
    li = lam_im.reshape(N_BLK, BLK_STATE)
    ldt = jnp.broadcast_to(log_dt[:, None], (N_SSM_GROUPS, SSM_STATE)).reshape(N_BLK, BLK_STATE)

    def b_rows(b):
        return b.reshape(N_BLK, gpb, SSM_STATE, SSM_GROUP).transpose(0, 1, 3, 2).reshape(N_BLK, LANES, SSM_STATE)

    def c_rows(c):
        return c.reshape(N_BLK, LANES, SSM_STATE)

    return lr, li, ldt, b_rows(b_re), b_rows(b_im), c_rows(c_re), c_rows(c_im)


def kernel(x_prompt, x_sample, state_ssm_re, state_ssm_im, state_conv, ffn1_norm, ffn1_w_gate, ffn1_w_up, ffn1_w_down, mix_norm, w_in, ssm_lambda_re, ssm_lambda_im, ssm_log_dt, ssm_b_re, ssm_b_im, ssm_c_re, ssm_c_im, ssm_d, ssm_w_glu, ssm_b_glu, conv_w, w_out, ffn2_norm, ffn2_w_gate, ffn2_w_up, ffn2_w_down, final_norm):
    assert ffn1_norm.shape[0] == 1, "single-layer trunk"
    n_prompt, seq, _ = x_prompt.shape
    n_dec, dec_seq, _ = x_sample.shape
    assert dec_seq == 1 and n_prompt == SUBLANES and seq % FFN_TM == 0 and seq % SCAN_T == 0
    assert n_dec % SCAN_T == 0 and n_dec <= n_prompt * SCAN_T and FFN_TM % FFN_SUB == 0

    n1 = ffn1_norm[0][None, :]
    nm = mix_norm[0][None, :]
    n2 = ffn2_norm[0][None, :]
    nf = final_norm[None, :]
    wglu = ssm_w_glu[0].astype(jnp.bfloat16)
    layout = _ssm_layout(ssm_lambda_re[0], ssm_lambda_im[0], ssm_log_dt[0], ssm_b_re[0], ssm_b_im[0],
                         ssm_c_re[0], ssm_c_im[0])
    a_re, a_im, a2_re, a2_im, bb, wb, wct, wdm = _discretize(*layout)
    d = ssm_d[0][None, :]
    bglu = ssm_b_glu[0][None, :]
    cw = conv_w[0]

    h_all, mixp, p_re, p_im, p_conv, ud, vd, gd = _ffn1_mixer(
        x_prompt, x_sample.reshape(n_dec, D_MODEL), n1, ffn1_w_gate[0], ffn1_w_up[0], ffn1_w_down[0], nm,
        w_in[0], a2_re, a2_im, wb, wct, wdm, d, wglu, bglu, cw)

    s_re0 = state_ssm_re[0].reshape(n_dec, N_STATE)
    s_im0 = state_ssm_im[0].reshape(n_dec, N_STATE)
    buf = state_conv[0].transpose(1, 0, 2)
    mixd, s_re, s_im, s_conv = _mixer_step(ud, vd, gd, s_re0, s_im0, buf, a_re, a_im, bb, wct,
                                           d, wglu, bglu, cw)

    yp, yd = _ffn2(h_all, mixp, mixd, w_out[0], n2, ffn2_w_gate[0], ffn2_w_up[0], ffn2_w_down[0], nf)

    gs = (N_SSM_GROUPS, SSM_STATE)
    return (yp,
            yd.reshape(n_dec, 1, D_MODEL),
            p_re[None],
            p_im[None],
            p_conv[None],
            s_re.reshape(1, n_dec, *gs),
            s_im.reshape(1, n_dec, *gs),
            s_conv.transpose(1, 0, 2)[None])
```

```python
import jax
import jax.numpy as jnp
from jax import lax
from jax.experimental import pallas as pl
from jax.experimental.pallas import tpu as pltpu

D_MODEL = 1024
D_FF = 2816
SSM_WIDTH = 512
CONV_WIDTH = 512
SSM_GROUP = 16
N_SSM_GROUPS = 32
SSM_STATE = 64
CONV_K = 3
IN_PROJ_WIDTH = SSM_WIDTH + 3 * CONV_WIDTH
EPS = 1e-6

LANES = 128
SUBLANES = 8
N_STATE = N_SSM_GROUPS * SSM_STATE
N_BLK = SSM_WIDTH // LANES
BLK_STATE = N_STATE // N_BLK
SLABS_PER_BLK = 2 * BLK_STATE // LANES
HALF = SLABS_PER_BLK // 2
N_TILE = N_BLK * HALF

FFN_TM = 1024
FFN_SUB = 512
W1_STEPS = 16
W2_STEPS = 8
FFN_TF = 256
SCAN_T = 64
N_PAIR = SCAN_T // 2
PAIR_PITCH = N_PAIR + 4
VMEM_LIMIT = 62 * 1024 * 1024

_NT = (((1,), (1,)), ((), ()))


def _rms(x, g):
    r = lax.rsqrt(jnp.mean(x * x, axis=-1, keepdims=True) + EPS)
    return x * r * g


def _bdot(a, b):
    return jnp.dot(a.astype(jnp.bfloat16), b.astype(jnp.bfloat16), preferred_element_type=jnp.float32)


def _swiglu_stage(xn, wg_ref, wu_ref, a_ref, lo, hi):
    rows = xn.shape[0]
    for j in range(lo, hi, FFN_TF):
        g = jnp.dot(xn, wg_ref[:, j:j + FFN_TF], preferred_element_type=jnp.float32)
        u = jnp.dot(xn, wu_ref[:, j:j + FFN_TF], preferred_element_type=jnp.float32)
        a_ref[0:rows, j:j + FFN_TF] = ((g * jax.nn.sigmoid(g)) * u).astype(a_ref.dtype)


def _swiglu_staged(xn, wg_ref, wu_ref, wd_ref, a_ref):
    rows = xn.shape[0]
    _swiglu_stage(xn, wg_ref, wu_ref, a_ref, 0, D_FF)
    return jnp.dot(a_ref[0:rows, :], wd_ref[...], preferred_element_type=jnp.float32)


def _cast_weight_chunks(step, pairs):
    for src, dst in pairs:
        n = src.shape[0]
        dst[pl.ds(pl.multiple_of(step * n, n), n), :] = src[...].astype(dst.dtype)


def _whole(shape):
    return pl.BlockSpec(shape, lambda i: (0,) * len(shape), pipeline_mode=pl.Buffered(1))


def _weight_chunk(shape, n_steps):
    return pl.BlockSpec(shape, lambda i: (jnp.minimum(i, n_steps - 1), 0))


def _glu_tail(y, u, d_ref, wglu_ref, bglu_ref):
    y = y + d_ref[...] * u
    z = jax.nn.gelu(y)
    gate = jax.nn.sigmoid(_bdot(z, wglu_ref[...]) + bglu_ref[...])
    return z * gate


def _discretize_kernel(lr_ref, li_ref, ldt_ref, bre_ref, bim_ref, ctre_ref, ctim_ref,
                       are_ref, aim_ref, a2re_ref, a2im_ref, bb_ref, wb_ref, wct_ref, wdm_ref):
    row_group = lax.broadcasted_iota(jnp.int32, (LANES, BLK_STATE), 0) // SSM_GROUP
    col_group = lax.broadcasted_iota(jnp.int32, (LANES, BLK_STATE), 1) // SSM_STATE
    on_diagonal = row_group == col_group

    def blocks(m):
        pair = jnp.concatenate([m, m], axis=1)
        return jnp.where(on_diagonal, jnp.concatenate([pair] * (BLK_STATE // LANES), axis=1), 0.0)

    gpb = N_SSM_GROUPS // N_BLK

    def state_rows(ref):
        return jnp.concatenate(
            [jnp.concatenate([ref[k * gpb + g:k * gpb + g + 1, :] for g in range(gpb)], axis=1)
             for k in range(N_BLK)], axis=0)

    def per_state(row_ref):
        vals = jnp.broadcast_to(row_ref[...], (SUBLANES, N_SSM_GROUPS))
        group = lax.broadcasted_iota(jnp.int32, (N_SSM_GROUPS, BLK_STATE), 0)
        col_group = lax.broadcasted_iota(jnp.int32, (N_SSM_GROUPS, BLK_STATE), 1) // SSM_STATE
        rows = [jnp.dot(vals, (group == k * gpb + col_group).astype(jnp.float32),
                        precision=lax.Precision.HIGHEST, preferred_element_type=jnp.float32)[0:1, :]
                for k in range(N_BLK)]
        return jnp.concatenate(rows, axis=0)

    lr, li = state_rows(lr_ref), state_rows(li_ref)
    dt = jnp.exp(per_state(ldt_ref))
    mag = jnp.exp(lr * dt)
    ab_re = mag * jnp.cos(li * dt)
    ab_im = mag * jnp.sin(li * dt)
    den = lr * lr + li * li
    nr = ab_re - 1.0
    ni = ab_im
    coef_re = (nr * lr + ni * li) / den
    coef_im = (ni * lr - nr * li) / den
    are_ref[...] = ab_re
    aim_ref[...] = ab_im
    a2_re = ab_re * ab_re - ab_im * ab_im
    a2_im = 2.0 * (ab_re * ab_im)
    bf = jnp.bfloat16
    for k in range(N_BLK):
        ar, ai = ab_re[k:k + 1, :], ab_im[k:k + 1, :]
        for j in range(HALF):
            lanes = slice(j * LANES, (j + 1) * LANES)
            a2re_ref[k * HALF + j] = jnp.broadcast_to(a2_re[k:k + 1, lanes], (SUBLANES, LANES))
            a2im_ref[k * HALF + j] = jnp.broadcast_to(a2_im[k:k + 1, lanes], (SUBLANES, LANES))
        cr, ci = coef_re[k:k + 1, :], coef_im[k:k + 1, :]
        bre, bim = blocks(bre_ref[k]), blocks(bim_ref[k])
        bb_re = cr * bre - ci * bim
        bb_im = cr * bim + ci * bre
        bb_ref[k, :, 0:BLK_STATE] = bb_re.astype(bf)
        bb_ref[k, :, BLK_STATE:] = bb_im.astype(bf)
        wb_ref[k, 0:LANES, 0:BLK_STATE] = (ar * bb_re - ai * bb_im).astype(bf)
        wb_ref[k, 0:LANES, BLK_STATE:] = (ar * bb_im + ai * bb_re).astype(bf)
        wb_ref[k, LANES:, 0:BLK_STATE] = bb_re.astype(bf)
        wb_ref[k, LANES:, BLK_STATE:] = bb_im.astype(bf)
        ctre, ctim = blocks(ctre_ref[k]), blocks(ctim_ref[k])
        wct_ref[k, 0:LANES, 0:BLK_STATE] = ctre.astype(bf)
        wct_ref[k, 0:LANES, BLK_STATE:] = (-ctim).astype(bf)
        wct_ref[k, LANES:, 0:BLK_STATE] = (ctre * ar - ctim * ai).astype(bf)
        wct_ref[k, LANES:, BLK_STATE:] = (-(ctre * ai + ctim * ar)).astype(bf)
        wdm = (lax.dot_general(bb_re, ctre, _NT, precision=lax.Precision.HIGHEST,
                               preferred_element_type=jnp.float32)
               - lax.dot_general(bb_im, ctim, _NT, precision=lax.Precision.HIGHEST,
                                 preferred_element_type=jnp.float32))
        wdm_ref[k] = wdm.astype(bf)


def _discretize(lr, li, ldt, bre, bim, ctre, ctim):
    f32, bf = jnp.float32, jnp.bfloat16
    return pl.pallas_call(
        _discretize_kernel,
        out_shape=[
            jax.ShapeDtypeStruct((N_BLK, BLK_STATE), f32),
            jax.ShapeDtypeStruct((N_BLK, BLK_STATE), f32),
            jax.ShapeDtypeStruct((N_TILE, SUBLANES, LANES), f32),
            jax.ShapeDtypeStruct((N_TILE, SUBLANES, LANES), f32),
            jax.ShapeDtypeStruct((N_BLK, LANES, 2 * BLK_STATE), bf),
            jax.ShapeDtypeStruct((N_BLK, 2 * LANES, 2 * BLK_STATE), bf),
            jax.ShapeDtypeStruct((N_BLK, 2 * LANES, 2 * BLK_STATE), bf),
            jax.ShapeDtypeStruct((N_BLK, LANES, LANES), bf),
        ],
        name="ssm_discretize",
    )(lr, li, ldt, bre, bim, ctre, ctim)


def _ffn1_mixer_kernel(x_ref, xd_ref, n1_ref, wg_c, wu_c, wd_c, nm_ref, win_c,
                       a2re_ref, a2im_ref, wb_ref, wct_ref, wdm_ref, d_ref, wglu_ref, bglu_ref, cw_ref,
                       h_ref, mix_ref, sre_ref, sim_ref, nconv_ref, ud_ref, vd_ref, gd_ref,
                       xs0_ref, xs1_ref, xs2_ref, xs3_ref, us_ref, ys_ref, l_ref, yo_ref, ye_ref, e_ref,
                       ecarry_ref, v_ref, g_ref, state_ref, wg_ref, wu_ref, wd_ref, win_ref, a_ref):
    xs_refs = (xs0_ref, xs1_ref, xs2_ref, xs3_ref)
    step = pl.program_id(0)
    last = pl.num_programs(0) - 1
    nb, t_len, _ = x_ref.shape
    n_dec = xd_ref.shape[0]
    prow = nb * PAIR_PITCH
    tiles = [(k, j) for k in range(N_BLK) for j in range(HALF)]

    @pl.when(step < W1_STEPS)
    def _load_weights():
        _cast_weight_chunks(step, ((wg_c, wg_ref), (wu_c, wu_ref), (wd_c, wd_ref), (win_c, win_ref)))

    @pl.when(step == 0)
    def _init():
        for ref in (state_ref, us_ref, l_ref, e_ref, ecarry_ref, v_ref, g_ref):
            ref[...] = jnp.zeros_like(ref)

    @pl.when(step >= W1_STEPS)
    def _main():
        carry = {}

        def build_lhs():
            for b in range(nb):
                for k in range(N_BLK):
                    for par in range(2):
                        col = (2 * k + par) * LANES
                        l_ref[pl.ds(b * PAIR_PITCH, N_PAIR), col:col + LANES] = (
                            us_ref[k, pl.ds(b * t_len + par, N_PAIR, stride=2), :])

        def b_proj(ks):
            for k in ks:
                lhs = l_ref[:, 2 * k * LANES:2 * (k + 1) * LANES].astype(jnp.bfloat16)
                v = jnp.dot(lhs, wb_ref[k], preferred_element_type=jnp.float32)
                for j in range(SLABS_PER_BLK):
                    xs_refs[k][j] = v[:, j * LANES:(j + 1) * LANES]

        def scan(r0, r1):
            if r0 == 0:
                carry["re"] = [state_ref[k * SLABS_PER_BLK + j] for k, j in tiles]
                carry["im"] = [state_ref[k * SLABS_PER_BLK + HALF + j] for k, j in tiles]
            xr, xi = carry["re"], carry["im"]
            for r in range(r0, r1):
                rows = pl.ds(r, SUBLANES, stride=PAIR_PITCH)
                for n, (k, j) in enumerate(tiles):
                    a_re = a2re_ref[k * HALF + j]
                    a_im = a2im_ref[k * HALF + j]
                    nr = (a_re * xr[n] + xs_refs[k][j, rows, :]) - a_im * xi[n]
                    ni = (a_re * xi[n] + xs_refs[k][HALF + j, rows, :]) + a_im * xr[n]
                    xs_refs[k][j, rows, :] = nr
                    xs_refs[k][HALF + j, rows, :] = ni
                    xr[n], xi[n] = nr, ni
            if r1 == N_PAIR:
                for n, (k, j) in enumerate(tiles):
                    state_ref[k * SLABS_PER_BLK + j] = xr[n]
                    state_ref[k * SLABS_PER_BLK + HALF + j] = xi[n]

        def c_proj(ks):
            for k in ks:
                lanes = slice(k * LANES, (k + 1) * LANES)
                x = jnp.concatenate([xs_refs[k][j] for j in range(SLABS_PER_BLK)], axis=1)
                ce = lax.dot_general(x.astype(jnp.bfloat16), wct_ref[k], _NT,
                                     preferred_element_type=jnp.float32)
                yo_ref[:, lanes] = ce[:, 0:LANES]
                e_ref[k, SUBLANES:SUBLANES + prow, :] = ce[:, LANES:]
                nxt = e_ref[k, pl.ds(SUBLANES + N_PAIR - 1, nb, stride=PAIR_PITCH), :]
                e_ref[k, pl.ds(SUBLANES - 1, nb, stride=PAIR_PITCH), :] = ecarry_ref[k]
                ecarry_ref[k] = nxt
                u_even = l_ref[:, 2 * k * LANES:(2 * k + 1) * LANES].astype(jnp.bfloat16)
                ye_ref[:, lanes] = (e_ref[k, SUBLANES - 1:SUBLANES - 1 + prow, :]
                                    + jnp.dot(u_even, wdm_ref[k], preferred_element_type=jnp.float32))

        def glu_emit():
            for b in range(nb):
                for k in range(N_BLK):
                    lanes = slice(k * LANES, (k + 1) * LANES)
                    for par, src in ((0, ye_ref), (1, yo_ref)):
                        ys_ref[k, pl.ds(b * t_len + par, N_PAIR, stride=2), :] = (
                            src[pl.ds(b * PAIR_PITCH, N_PAIR), lanes])
            y = jnp.concatenate([ys_ref[k] for k in range(N_BLK)], axis=1)
            u = jnp.concatenate([us_ref[k] for k in range(N_BLK)], axis=1)
            s_out = _glu_tail(y, u, d_ref, wglu_ref, bglu_ref)
            mix_ref[:, :, 0:SSM_WIDTH] = s_out.reshape(nb, t_len, SSM_WIDTH).astype(mix_ref.dtype)

        def conv():
            v = v_ref[:, SUBLANES:SUBLANES + t_len, :]
            z = cw_ref[0, 0:1, :] * v_ref[:, SUBLANES - 2:SUBLANES - 2 + t_len, :]
            z = z + cw_ref[0, 1:2, :] * v_ref[:, SUBLANES - 1:SUBLANES - 1 + t_len, :]
            z = z + cw_ref[0, 2:3, :] * v
            mix_ref[:, :, SSM_WIDTH:] = (g_ref[...] * z).astype(mix_ref.dtype)
            v_ref[:, 0:SUBLANES, :] = v_ref[:, t_len:t_len + SUBLANES, :]

        q = N_PAIR // 4
        pieces = [
            lambda: (conv(), build_lhs(), b_proj((0,))), lambda: b_proj((1,)), lambda: b_proj((2,)),
            lambda: b_proj((3,)),
            lambda: scan(0, q), lambda: scan(q, 2 * q), lambda: scan(2 * q, 3 * q), lambda: scan(3 * q, N_PAIR),
            lambda: c_proj((0, 1)), lambda: c_proj((2, 3)),
            glu_emit,
        ]
        assert len(pieces) <= D_FF // FFN_TF

        db = n_dec // t_len
        x_top = jnp.where(step == last, xd_ref[...], x_ref[0:db].reshape(n_dec, D_MODEL))
        x = jnp.concatenate([x_top, x_ref[db:].reshape((nb - db) * t_len, D_MODEL)], axis=0)
        xn = _rms(x, n1_ref[...]).astype(jnp.bfloat16)
        for c, j in enumerate(range(0, D_FF, FFN_TF)):
            _swiglu_stage(xn, wg_ref, wu_ref, a_ref, j, j + FFN_TF)
            if c < len(pieces):
                pieces[c]()
        h = x + 0.5 * jnp.dot(a_ref[...], wd_ref[...], preferred_element_type=jnp.float32)
        hn = _rms(h, nm_ref[...]).astype(jnp.bfloat16)
        p = jnp.dot(hn, win_ref[...], preferred_element_type=jnp.float32)
        h_ref[...] = h.reshape(nb, t_len, D_MODEL)
        v_new = p[:, SSM_WIDTH + 2 * CONV_WIDTH:] * p[:, SSM_WIDTH:SSM_WIDTH + CONV_WIDTH]
        v_ref[:, SUBLANES:SUBLANES + t_len, :] = v_new.reshape(nb, t_len, CONV_WIDTH)
        g_ref[...] = p[:, SSM_WIDTH + CONV_WIDTH:SSM_WIDTH + 2 * CONV_WIDTH].reshape(nb, t_len, CONV_WIDTH)
        for k in range(N_BLK):
            us_ref[k] = p[:, k * LANES:(k + 1) * LANES]

    @pl.when(step == last)
    def _final():
        for n, (k, j) in enumerate(tiles):
            for out_ref, slab in ((sre_ref, k * SLABS_PER_BLK + j), (sim_ref, k * SLABS_PER_BLK + HALF + j)):
                tile = state_ref[slab]
                for half in range(LANES // SSM_STATE):
                    group = (k * BLK_STATE + j * LANES) // SSM_STATE + half
                    out_ref[:, group, :] = tile[:, half * SSM_STATE:(half + 1) * SSM_STATE]
        nconv_ref[...] = v_ref[:, SUBLANES - 2:SUBLANES, :]
        for k in range(N_BLK):
            ud_ref[:, k * LANES:(k + 1) * LANES] = us_ref[k, 0:n_dec, :]
        for b in range(n_dec // t_len):
            vd_ref[b * t_len:(b + 1) * t_len, :] = v_ref[b, SUBLANES:SUBLANES + t_len, :]
            gd_ref[b * t_len:(b + 1) * t_len, :] = g_ref[b]


def _ffn1_mixer(x, xd, n1, wg, wu, wd, nm, win, a2_re, a2_im, wb, wct, wdm, d, wglu, bglu, cw):
    nb, seq, _ = x.shape
    n_dec = xd.shape[0]
    n_chunk = seq // SCAN_T
    prow = nb * PAIR_PITCH
    f32, bf = jnp.float32, jnp.bfloat16
    main = lambda i: jnp.maximum(i - W1_STEPS, 0)
    return pl.pallas_call(
        _ffn1_mixer_kernel,
        grid=(W1_STEPS + n_chunk + 1,),
        in_specs=[
            pl.BlockSpec((nb, SCAN_T, D_MODEL), lambda i: (0, jnp.minimum(main(i), n_chunk - 1), 0)),
            _whole((n_dec, D_MODEL)),
            _whole((1, D_MODEL)),
            _weight_chunk((D_MODEL // W1_STEPS, D_FF), W1_STEPS),
            _weight_chunk((D_MODEL // W1_STEPS, D_FF), W1_STEPS),
            _weight_chunk((D_FF // W1_STEPS, D_MODEL), W1_STEPS),
            _whole((1, D_MODEL)),
            _weight_chunk((D_MODEL // W1_STEPS, IN_PROJ_WIDTH), W1_STEPS),
            _whole((N_TILE, SUBLANES, LANES)),
            _whole((N_TILE, SUBLANES, LANES)),
            _whole((N_BLK, 2 * LANES, 2 * BLK_STATE)),
            _whole((N_BLK, 2 * LANES, 2 * BLK_STATE)),
            _whole((N_BLK, LANES, LANES)),
            _whole((1, SSM_WIDTH)),
            _whole((SSM_WIDTH, SSM_WIDTH)),
            _whole((1, SSM_WIDTH)),
            _whole((1, CONV_K, CONV_WIDTH)),
        ],
        out_specs=[
            pl.BlockSpec((nb, SCAN_T, D_MODEL), lambda i: (0, main(i), 0)),
            pl.BlockSpec((nb, SCAN_T, D_MODEL), lambda i: (0, jnp.clip(main(i) - 1, 0, n_chunk - 1), 0)),
            pl.BlockSpec((nb, N_SSM_GROUPS, SSM_STATE), lambda i: (0, 0, 0)),
            pl.BlockSpec((nb, N_SSM_GROUPS, SSM_STATE), lambda i: (0, 0, 0)),
            pl.BlockSpec((nb, CONV_K - 1, CONV_WIDTH), lambda i: (0, 0, 0)),
            pl.BlockSpec((n_dec, SSM_WIDTH), lambda i: (0, 0)),
            pl.BlockSpec((n_dec, CONV_WIDTH), lambda i: (0, 0)),
            pl.BlockSpec((n_dec, CONV_WIDTH), lambda i: (0, 0)),
        ],
        out_shape=[
            jax.ShapeDtypeStruct((nb, seq + SCAN_T, D_MODEL), f32),
            jax.ShapeDtypeStruct((nb, seq, D_MODEL), bf),
            jax.ShapeDtypeStruct((nb, N_SSM_GROUPS, SSM_STATE), f32),
            jax.ShapeDtypeStruct((nb, N_SSM_GROUPS, SSM_STATE), f32),
            jax.ShapeDtypeStruct((nb, CONV_K - 1, CONV_WIDTH), f32),
            jax.ShapeDtypeStruct((n_dec, SSM_WIDTH), f32),
            jax.ShapeDtypeStruct((n_dec, CONV_WIDTH), f32),
            jax.ShapeDtypeStruct((n_dec, CONV_WIDTH), f32),
        ],
        scratch_shapes=[
            *[pltpu.VMEM((SLABS_PER_BLK, prow, LANES), f32) for _ in range(N_BLK)],
            pltpu.VMEM((N_BLK, nb * SCAN_T, LANES), f32),
            pltpu.VMEM((N_BLK, nb * SCAN_T, LANES), f32),
            pltpu.VMEM((prow, 2 * SSM_WIDTH), f32),
            pltpu.VMEM((prow, SSM_WIDTH), f32),
            pltpu.VMEM((prow, SSM_WIDTH), f32),
            pltpu.VMEM((N_BLK, SUBLANES + prow, LANES), f32),
            pltpu.VMEM((N_BLK, SUBLANES, LANES), f32),
            pltpu.VMEM((nb, SCAN_T + SUBLANES, CONV_WIDTH), f32),
            pltpu.VMEM((nb, SCAN_T, CONV_WIDTH), f32),
            pltpu.VMEM((2 * N_TILE, SUBLANES, LANES), f32),
            pltpu.VMEM((D_MODEL, D_FF), bf),
            pltpu.VMEM((D_MODEL, D_FF), bf),
            pltpu.VMEM((D_FF, D_MODEL), bf),
            pltpu.VMEM((D_MODEL, IN_PROJ_WIDTH), bf),
            pltpu.VMEM((nb * SCAN_T, D_FF), bf),
        ],
        compiler_params=pltpu.CompilerParams(
            dimension_semantics=("arbitrary",), vmem_limit_bytes=VMEM_LIMIT),
        name="ffn1_mixer",
    )(x, xd, n1, wg, wu, wd, nm, win, a2_re, a2_im, wb, wct, wdm, d, wglu, bglu, cw)


def _ffn2_rows(h, mix, weights, a_ref):
    wout_ref, n2_ref, wg_ref, wu_ref, wd_ref, nf_ref = weights
    h = h + jnp.dot(mix, wout_ref[...], preferred_element_type=jnp.float32)
    hn = _rms(h, n2_ref[...]).astype(jnp.bfloat16)
    h = h + 0.5 * _swiglu_staged(hn, wg_ref, wu_ref, wd_ref, a_ref)
    return _rms(h, nf_ref[...])


def _ffn2_kernel(h_ref, mix_ref, hd0_ref, hd1_ref, mixd_ref, wout_c, wg_c, wu_c, wd_c, n2_ref, nf_ref,
                 y_ref, yd_ref, wout_ref, wg_ref, wu_ref, wd_ref, a0_ref, a1_ref):
    step = pl.program_id(0)
    last = pl.num_programs(0) - 1
    weights = (wout_ref, n2_ref, wg_ref, wu_ref, wd_ref, nf_ref)

    @pl.when(step < W2_STEPS)
    def _load_weights():
        _cast_weight_chunks(step, ((wout_c, wout_ref), (wg_c, wg_ref), (wu_c, wu_ref), (wd_c, wd_ref)))

    @pl.when((step >= W2_STEPS) & (step < last))
    def _prompt_tile():
        assert h_ref.shape[0] == 2 * FFN_SUB
        ra, rb = slice(0, FFN_SUB), slice(FFN_SUB, 2 * FFN_SUB)
        h_a = h_ref[ra, :] + jnp.dot(mix_ref[ra, :], wout_ref[...], preferred_element_type=jnp.float32)
        h_b = h_ref[rb, :] + jnp.dot(mix_ref[rb, :], wout_ref[...], preferred_element_type=jnp.float32)
        hn_a = _rms(h_a, n2_ref[...]).astype(jnp.bfloat16)
        _swiglu_stage(hn_a, wg_ref, wu_ref, a0_ref, 0, FFN_TF)
        hn_b = _rms(h_b, n2_ref[...]).astype(jnp.bfloat16)
        _swiglu_stage(hn_a, wg_ref, wu_ref, a0_ref, FFN_TF, D_FF)
        f_a = jnp.dot(a0_ref[...], wd_ref[...], preferred_element_type=jnp.float32)
        _swiglu_stage(hn_b, wg_ref, wu_ref, a1_ref, 0, FFN_TF)
        y_ref[ra, :] = _rms(h_a + 0.5 * f_a, nf_ref[...])
        _swiglu_stage(hn_b, wg_ref, wu_ref, a1_ref, FFN_TF, D_FF)
        f_b = jnp.dot(a1_ref[...], wd_ref[...], preferred_element_type=jnp.float32)
        y_ref[rb, :] = _rms(h_b + 0.5 * f_b, nf_ref[...])

    @pl.when(step == last)
    def _decode_rows():
        h_d = jnp.concatenate([hd0_ref[...], hd1_ref[...]], axis=0)
        yd_ref[...] = _ffn2_rows(h_d, mixd_ref[...], weights, a0_ref)


def _ffn2(h_all, mix, mixd, wout, n2, wg, wu, wd, nf):
    nb, seq, _ = mix.shape
    seq_all = h_all.shape[1]
    n_dec = mixd.shape[0]
    tail = seq_all - seq
    assert n_dec == 2 * tail
    per_b = seq // FFN_TM
    n_tiles = nb * per_b
    h_all = h_all.reshape(nb * seq_all, D_MODEL)
    mix = mix.reshape(nb * seq, D_MODEL)
    idx = lambda i: jnp.clip(i - W2_STEPS, 0, n_tiles - 1)
    tile = lambda i: (idx(i), 0)
    assert seq_all % SUBLANES == 0
    tile_all = lambda i: (pl.multiple_of((idx(i) // per_b) * seq_all + (idx(i) % per_b) * FFN_TM, SUBLANES), 0)
    yp, yd = pl.pallas_call(
        _ffn2_kernel,
        grid=(W2_STEPS + n_tiles + 1,),
        in_specs=[
            pl.BlockSpec((pl.Element(FFN_TM), pl.Element(D_MODEL)), tile_all),
            pl.BlockSpec((FFN_TM, D_MODEL), tile),
            pl.BlockSpec((pl.Element(tail), pl.Element(D_MODEL)), lambda i: (seq, 0),
                         pipeline_mode=pl.Buffered(1)),
            pl.BlockSpec((pl.Element(tail), pl.Element(D_MODEL)), lambda i: (seq_all + seq, 0),
                         pipeline_mode=pl.Buffered(1)),
            _whole(mixd.shape),
            _weight_chunk((D_MODEL // W2_STEPS, D_MODEL), W2_STEPS),
            _weight_chunk((D_MODEL // W2_STEPS, D_FF), W2_STEPS),
            _weight_chunk((D_MODEL // W2_STEPS, D_FF), W2_STEPS),
            _weight_chunk((D_FF // W2_STEPS, D_MODEL), W2_STEPS),
            _whole((1, D_MODEL)),
            _whole((1, D_MODEL)),
        ],
        out_specs=[
            pl.BlockSpec((FFN_TM, D_MODEL), tile),
            pl.BlockSpec((n_dec, D_MODEL), lambda i: (0, 0)),
        ],
        out_shape=[
            jax.ShapeDtypeStruct((nb * seq, D_MODEL), jnp.float32),
            jax.ShapeDtypeStruct((n_dec, D_MODEL), jnp.float32),
        ],
        scratch_shapes=[
            pltpu.VMEM((D_MODEL, D_MODEL), jnp.bfloat16),
            pltpu.VMEM((D_MODEL, D_FF), jnp.bfloat16),
            pltpu.VMEM((D_MODEL, D_FF), jnp.bfloat16),
            pltpu.VMEM((D_FF, D_MODEL), jnp.bfloat16),
            pltpu.VMEM((FFN_SUB, D_FF), jnp.bfloat16),
            pltpu.VMEM((FFN_SUB, D_FF), jnp.bfloat16),
        ],
        compiler_params=pltpu.CompilerParams(
            dimension_semantics=("arbitrary",), vmem_limit_bytes=VMEM_LIMIT),
        name="outproj_ffn2",
    )(h_all, mix, h_all, h_all, mixd, wout, wg, wu, wd, n2, nf)
    return yp.reshape(nb, seq, D_MODEL), yd


def _mixer_step_kernel(u_ref, v_ref, g_ref, hre_ref, him_ref, buf_ref, are_ref, aim_ref, bb_ref, wct_ref,
                       d_ref, wglu_ref, bglu_ref, cw_ref, mix_ref, sre_ref, sim_ref, nconv_ref):
    u = u_ref[...]
    u_bf = u.astype(jnp.bfloat16)
    ys = []
    for k in range(N_BLK):
        bu = jnp.dot(u_bf[:, k * LANES:(k + 1) * LANES], bb_ref[k], preferred_element_type=jnp.float32)
        st = slice(k * BLK_STATE, (k + 1) * BLK_STATE)
        a_re = are_ref[k:k + 1, :]
        a_im = aim_ref[k:k + 1, :]
        h_re = hre_ref[:, st]
        h_im = him_ref[:, st]
        x_re = (a_re * h_re + bu[:, :BLK_STATE]) - a_im * h_im
        x_im = (a_re * h_im + bu[:, BLK_STATE:]) + a_im * h_re
        sre_ref[:, st] = x_re
        sim_ref[:, st] = x_im
        x = jnp.concatenate([x_re, x_im], axis=1).astype(jnp.bfloat16)
        ys.append(lax.dot_general(x, wct_ref[k, 0:LANES, :], _NT, preferred_element_type=jnp.float32))
    y = jnp.concatenate(ys, axis=1)
    mix_ref[:, 0:SSM_WIDTH] = _glu_tail(y, u, d_ref, wglu_ref, bglu_ref).astype(mix_ref.dtype)

    v = v_ref[...]
    z = cw_ref[0, 0:1, :] * buf_ref[0] + cw_ref[0, 1:2, :] * buf_ref[1] + cw_ref[0, 2:3, :] * v
    mix_ref[:, SSM_WIDTH:] = (g_ref[...] * z).astype(mix_ref.dtype)
    nconv_ref[0] = buf_ref[1]
    nconv_ref[1] = v


def _mixer_step(u, v, g, h_re, h_im, buf, a_re, a_im, bb, wct, d, wglu, bglu, cw):
    nb = u.shape[0]
    return pl.pallas_call(
        _mixer_step_kernel,
        out_shape=[
            jax.ShapeDtypeStruct((nb, D_MODEL), jnp.bfloat16),
            jax.ShapeDtypeStruct((nb, N_STATE), jnp.float32),
            jax.ShapeDtypeStruct((nb, N_STATE), jnp.float32),
            jax.ShapeDtypeStruct((CONV_K - 1, nb, CONV_WIDTH), jnp.float32),
        ],
        compiler_params=pltpu.CompilerParams(vmem_limit_bytes=VMEM_LIMIT),
        name="mixer_step",
    )(u, v, g, h_re, h_im, buf, a_re, a_im, bb, wct, d, wglu, bglu, cw)


def _ssm_layout(lam_re, lam_im, log_dt, b_re, b_im, c_re, c_im):
    gpb = N_SSM_GROUPS // N_BLK

    def b_rows(b):
        return b.reshape(N_BLK, gpb, SSM_STATE, SSM_GROUP).transpose(0, 1, 3, 2).reshape(N_BLK, LANES, SSM_STATE)

    def c_rows(c):
        return c.reshape(N_BLK, LANES, SSM_STATE)

    return lam_re, lam_im, log_dt, b_rows(b_re), b_rows(b_im), c_rows(c_re), c_rows(c_im)


def kernel(x_prompt, x_sample, state_ssm_re, state_ssm_im, state_conv, ffn1_norm, ffn1_w_gate, ffn1_w_up, ffn1_w_down, mix_norm, w_in, ssm_lambda_re, ssm_lambda_im, ssm_log_dt, ssm_b_re, ssm_b_im, ssm_c_re, ssm_c_im, ssm_d, ssm_w_glu, ssm_b_glu, conv_w, w_out, ffn2_norm, ffn2_w_gate, ffn2_w_up, ffn2_w_down, final_norm):
    assert ffn1_norm.shape[0] == 1, "single-layer trunk"
    n_prompt, seq, _ = x_prompt.shape
    n_dec, dec_seq, _ = x_sample.shape
    assert dec_seq == 1 and n_prompt == SUBLANES and seq % FFN_TM == 0 and seq % SCAN_T == 0
    assert n_dec % SCAN_T == 0 and n_dec <= n_prompt * SCAN_T and FFN_TM % FFN_SUB == 0

    n1 = ffn1_norm[0][None, :]
    nm = mix_norm[0][None, :]
    n2 = ffn2_norm[0][None, :]
    nf = final_norm[None, :]
    wglu = ssm_w_glu[0]
    layout = _ssm_layout(ssm_lambda_re[0], ssm_lambda_im[0], ssm_log_dt, ssm_b_re[0], ssm_b_im[0],
                         ssm_c_re[0], ssm_c_im[0])
    a_re, a_im, a2_re, a2_im, bb, wb, wct, wdm = _discretize(*layout)
    d = ssm_d[0][None, :]
    bglu = ssm_b_glu[0][None, :]
    cw = conv_w

    h_all, mixp, p_re, p_im, p_conv, ud, vd, gd = _ffn1_mixer(
        x_prompt, x_sample.reshape(n_dec, D_MODEL), n1, ffn1_w_gate[0], ffn1_w_up[0], ffn1_w_down[0], nm,
        w_in[0], a2_re, a2_im, wb, wct, wdm, d, wglu, bglu, cw)

    s_re0 = state_ssm_re[0].reshape(n_dec, N_STATE)
    s_im0 = state_ssm_im[0].reshape(n_dec, N_STATE)
    buf = state_conv[0].transpose(1, 0, 2)
    mixd, s_re, s_im, s_conv = _mixer_step(ud, vd, gd, s_re0, s_im0, buf, a_re, a_im, bb, wct,
                                           d, wglu, bglu, cw)

    yp, yd = _ffn2(h_all, mixp, mixd, w_out[0], n2, ffn2_w_gate[0], ffn2_w_up[0], ffn2_w_down[0], nf)

    gs = (N_SSM_GROUPS, SSM_STATE)
    return (yp,
            yd.reshape(n_dec, 1, D_MODEL),
            p_re[None],
            p_im[None],
            p_conv[None],
            s_re.reshape(1, n_dec, *gs),
            s_im.reshape(1, n_dec, *gs),
            s_conv.transpose(1, 0, 2)[None])
```

```python
import jax
import jax.numpy as jnp
from jax import lax
from jax.experimental import pallas as pl
from jax.experimental.pallas import tpu as pltpu

D_MODEL = 1024
D_FF = 2816
SSM_WIDTH = 512
CONV_WIDTH = 512
SSM_GROUP = 16
N_SSM_GROUPS = 32
SSM_STATE = 64
CONV_K = 3
IN_PROJ_WIDTH = SSM_WIDTH + 3 * CONV_WIDTH
EPS = 1e-6

LANES = 128
SUBLANES = 8
N_STATE = N_SSM_GROUPS * SSM_STATE
N_BLK = SSM_WIDTH // LANES
BLK_STATE = N_STATE // N_BLK
SLABS_PER_BLK = 2 * BLK_STATE // LANES
HALF = SLABS_PER_BLK // 2
N_TILE = N_BLK * HALF

FFN_TM = 1024
FFN_SUB = 512
W1_STEPS = 16
W2_STEPS = 8
FFN_TF = 256
SCAN_T = 64
N_PAIR = SCAN_T // 2
PAIR_PITCH = N_PAIR + 4
VMEM_LIMIT = 62 * 1024 * 1024

_NT = (((1,), (1,)), ((), ()))


def _rms(x, g):
    r = lax.rsqrt(jnp.mean(x * x, axis=-1, keepdims=True) + EPS)
    return x * r * g


def _bdot(a, b):
    return jnp.dot(a.astype(jnp.bfloat16), b.astype(jnp.bfloat16), preferred_element_type=jnp.float32)


def _swiglu_stage(xn, wg_ref, wu_ref, a_ref, lo, hi):
    rows = xn.shape[0]
    for j in range(lo, hi, FFN_TF):
        g = jnp.dot(xn, wg_ref[:, j:j + FFN_TF], preferred_element_type=jnp.float32)
        u = jnp.dot(xn, wu_ref[:, j:j + FFN_TF], preferred_element_type=jnp.float32)
        a_ref[0:rows, j:j + FFN_TF] = ((g * jax.nn.sigmoid(g)) * u).astype(a_ref.dtype)


def _swiglu_staged(xn, wg_ref, wu_ref, wd_ref, a_ref):
    rows = xn.shape[0]
    _swiglu_stage(xn, wg_ref, wu_ref, a_ref, 0, D_FF)
    return jnp.dot(a_ref[0:rows, :], wd_ref[...], preferred_element_type=jnp.float32)


def _cast_weight_chunks(step, pairs):
    for src, dst in pairs:
        n = src.shape[0]
        dst[pl.ds(pl.multiple_of(step * n, n), n), :] = src[...].astype(dst.dtype)


def _whole(shape):
    return pl.BlockSpec(shape, lambda i: (0,) * len(shape), pipeline_mode=pl.Buffered(1))


def _weight_chunk(shape, n_steps):
    return pl.BlockSpec(shape, lambda i: (jnp.minimum(i, n_steps - 1), 0))


def _glu_tail(y, u, d_ref, wglu_ref, bglu_ref):
    y = y + d_ref[...] * u
    z = jax.nn.gelu(y)
    gate = jax.nn.sigmoid(_bdot(z, wglu_ref[...]) + bglu_ref[...])
    return z * gate


def _discretize_kernel(lr_ref, li_ref, ldt_ref, bre_ref, bim_ref, ctre_ref, ctim_ref,
                       are_ref, aim_ref, a2re_ref, a2im_ref, bb_ref, wb_ref, wct_ref, wdm_ref):
    row_group = lax.broadcasted_iota(jnp.int32, (LANES, BLK_STATE), 0) // SSM_GROUP
    col_group = lax.broadcasted_iota(jnp.int32, (LANES, BLK_STATE), 1) // SSM_STATE
    on_diagonal = row_group == col_group

    def blocks(m):
        pair = jnp.concatenate([m, m], axis=1)
        return jnp.where(on_diagonal, jnp.concatenate([pair] * (BLK_STATE // LANES), axis=1), 0.0)

    gpb = N_SSM_GROUPS // N_BLK

    def state_rows(ref):
        return jnp.concatenate(
            [jnp.concatenate([ref[k * gpb + g:k * gpb + g + 1, :] for g in range(gpb)], axis=1)
             for k in range(N_BLK)], axis=0)

    def per_state(row_ref):
        vals = jnp.broadcast_to(row_ref[...], (SUBLANES, N_SSM_GROUPS))
        group = lax.broadcasted_iota(jnp.int32, (N_SSM_GROUPS, BLK_STATE), 0)
        col_group = lax.broadcasted_iota(jnp.int32, (N_SSM_GROUPS, BLK_STATE), 1) // SSM_STATE
        rows = [jnp.dot(vals, (group == k * gpb + col_group).astype(jnp.float32),
                        precision=lax.Precision.HIGHEST, preferred_element_type=jnp.float32)[0:1, :]
                for k in range(N_BLK)]
        return jnp.concatenate(rows, axis=0)

    lr, li = state_rows(lr_ref), state_rows(li_ref)
    dt = jnp.exp(per_state(ldt_ref))
    mag = jnp.exp(lr * dt)
    ab_re = mag * jnp.cos(li * dt)
    ab_im = mag * jnp.sin(li * dt)
    den = lr * lr + li * li
    nr = ab_re - 1.0
    ni = ab_im
    coef_re = (nr * lr + ni * li) / den
    coef_im = (ni * lr - nr * li) / den
    are_ref[...] = ab_re
    aim_ref[...] = ab_im
    a2_re = ab_re * ab_re - ab_im * ab_im
    a2_im = 2.0 * (ab_re * ab_im)
    bf = jnp.bfloat16
    for k in range(N_BLK):
        ar, ai = ab_re[k:k + 1, :], ab_im[k:k + 1, :]
        for j in range(HALF):
            lanes = slice(j * LANES, (j + 1) * LANES)
            a2re_ref[k * HALF + j] = jnp.broadcast_to(a2_re[k:k + 1, lanes], (SUBLANES, LANES))
            a2im_ref[k * HALF + j] = jnp.broadcast_to(a2_im[k:k + 1, lanes], (SUBLANES, LANES))
        cr, ci = coef_re[k:k + 1, :], coef_im[k:k + 1, :]
        bre, bim = blocks(bre_ref[k]), blocks(bim_ref[k])
        bb_re = cr * bre - ci * bim
        bb_im = cr * bim + ci * bre
        bb_ref[k, :, 0:BLK_STATE] = bb_re.astype(bf)
        bb_ref[k, :, BLK_STATE:] = bb_im.astype(bf)
        wb_ref[k, 0:LANES, 0:BLK_STATE] = (ar * bb_re - ai * bb_im).astype(bf)
        wb_ref[k, 0:LANES, BLK_STATE:] = (ar * bb_im + ai * bb_re).astype(bf)
        wb_ref[k, LANES:, 0:BLK_STATE] = bb_re.astype(bf)
        wb_ref[k, LANES:, BLK_STATE:] = bb_im.astype(bf)
        ctre, ctim = blocks(ctre_ref[k]), blocks(ctim_ref[k])
        wct_ref[k, 0:LANES, 0:BLK_STATE] = ctre.astype(bf)
        wct_ref[k, 0:LANES, BLK_STATE:] = (-ctim).astype(bf)
        wct_ref[k, LANES:, 0:BLK_STATE] = (ctre * ar - ctim * ai).astype(bf)
        wct_ref[k, LANES:, BLK_STATE:] = (-(ctre * ai + ctim * ar)).astype(bf)
        wdm = (lax.dot_general(bb_re, ctre, _NT, precision=lax.Precision.HIGHEST,
                               preferred_element_type=jnp.float32)
               - lax.dot_general(bb_im, ctim, _NT, precision=lax.Precision.HIGHEST,
                                 preferred_element_type=jnp.float32))
        wdm_ref[k] = wdm.astype(bf)


def _discretize(lr, li, ldt, bre, bim, ctre, ctim):
    f32, bf = jnp.float32, jnp.bfloat16
    return pl.pallas_call(
        _discretize_kernel,
        out_shape=[
            jax.ShapeDtypeStruct((N_BLK, BLK_STATE), f32),
            jax.ShapeDtypeStruct((N_BLK, BLK_STATE), f32),
            jax.ShapeDtypeStruct((N_TILE, SUBLANES, LANES), f32),
            jax.ShapeDtypeStruct((N_TILE, SUBLANES, LANES), f32),
            jax.ShapeDtypeStruct((N_BLK, LANES, 2 * BLK_STATE), bf),
            jax.ShapeDtypeStruct((N_BLK, 2 * LANES, 2 * BLK_STATE), bf),
            jax.ShapeDtypeStruct((N_BLK, 2 * LANES, 2 * BLK_STATE), bf),
            jax.ShapeDtypeStruct((N_BLK, LANES, LANES), bf),
        ],
        name="ssm_discretize",
    )(lr, li, ldt, bre, bim, ctre, ctim)


def _ffn1_mixer_kernel(x_ref, xd_ref, n1_ref, wg_c, wu_c, wd_c, nm_ref, win_c,
                       a2re_ref, a2im_ref, wb_ref, wct_ref, wdm_ref, d_ref, wglu_ref, bglu_ref, cw_ref,
                       h_ref, mix_ref, sre_ref, sim_ref, nconv_ref, ud_ref, vd_ref, gd_ref,
                       xs0_ref, xs1_ref, xs2_ref, xs3_ref, us_ref, ys_ref, l_ref, yo_ref, ye_ref, e_ref,
                       ecarry_ref, v_ref, g_ref, state_ref, wg_ref, wu_ref, wd_ref, win_ref, a_ref):
    xs_refs = (xs0_ref, xs1_ref, xs2_ref, xs3_ref)
    step = pl.program_id(0)
    last = pl.num_programs(0) - 1
    nb, t_len, _ = x_ref.shape
    n_dec = xd_ref.shape[0]
    prow = nb * PAIR_PITCH
    tiles = [(k, j) for k in range(N_BLK) for j in range(HALF)]

    @pl.when(step < W1_STEPS)
    def _load_weights():
        _cast_weight_chunks(step, ((wg_c, wg_ref), (wu_c, wu_ref), (wd_c, wd_ref), (win_c, win_ref)))

    @pl.when(step == 0)
    def _init():
        for ref in (state_ref, us_ref, l_ref, e_ref, ecarry_ref, v_ref, g_ref):
            ref[...] = jnp.zeros_like(ref)

    @pl.when(step >= W1_STEPS)
    def _main():
        carry = {}

        def build_lhs():
            for b in range(nb):
                for k in range(N_BLK):
                    for par in range(2):
                        col = (2 * k + par) * LANES
                        l_ref[pl.ds(b * PAIR_PITCH, N_PAIR), col:col + LANES] = (
                            us_ref[k, pl.ds(b * t_len + par, N_PAIR, stride=2), :])

        def b_proj(ks):
            for k in ks:
                lhs = l_ref[:, 2 * k * LANES:2 * (k + 1) * LANES].astype(jnp.bfloat16)
                v = jnp.dot(lhs, wb_ref[k], preferred_element_type=jnp.float32)
                for j in range(SLABS_PER_BLK):
                    xs_refs[k][j] = v[:, j * LANES:(j + 1) * LANES]

        def scan(r0, r1):
            if r0 == 0:
                carry["re"] = [state_ref[k * SLABS_PER_BLK + j] for k, j in tiles]
                carry["im"] = [state_ref[k * SLABS_PER_BLK + HALF + j] for k, j in tiles]
            xr, xi = carry["re"], carry["im"]
            for r in range(r0, r1):
                rows = pl.ds(r, SUBLANES, stride=PAIR_PITCH)
                for n, (k, j) in enumerate(tiles):
                    a_re = a2re_ref[k * HALF + j]
                    a_im = a2im_ref[k * HALF + j]
                    nr = (a_re * xr[n] + xs_refs[k][j, rows, :]) - a_im * xi[n]
                    ni = (a_re * xi[n] + xs_refs[k][HALF + j, rows, :]) + a_im * xr[n]
                    xs_refs[k][j, rows, :] = nr
                    xs_refs[k][HALF + j, rows, :] = ni
                    xr[n], xi[n] = nr, ni
            if r1 == N_PAIR:
                for n, (k, j) in enumerate(tiles):
                    state_ref[k * SLABS_PER_BLK + j] = xr[n]
                    state_ref[k * SLABS_PER_BLK + HALF + j] = xi[n]

        def c_proj(ks):
            for k in ks:
                lanes = slice(k * LANES, (k + 1) * LANES)
                x = jnp.concatenate([xs_refs[k][j] for j in range(SLABS_PER_BLK)], axis=1)
                ce = lax.dot_general(x.astype(jnp.bfloat16), wct_ref[k], _NT,
                                     preferred_element_type=jnp.float32)
                yo_ref[:, lanes] = ce[:, 0:LANES]
                e_ref[k, SUBLANES:SUBLANES + prow, :] = ce[:, LANES:]
                nxt = e_ref[k, pl.ds(SUBLANES + N_PAIR - 1, nb, stride=PAIR_PITCH), :]
                e_ref[k, pl.ds(SUBLANES - 1, nb, stride=PAIR_PITCH), :] = ecarry_ref[k]
                ecarry_ref[k] = nxt
                u_even = l_ref[:, 2 * k * LANES:(2 * k + 1) * LANES].astype(jnp.bfloat16)
                ye_ref[:, lanes] = (e_ref[k, SUBLANES - 1:SUBLANES - 1 + prow, :]
                                    + jnp.dot(u_even, wdm_ref[k], preferred_element_type=jnp.float32))

        def glu_emit():
            for b in range(nb):
                for k in range(N_BLK):
                    lanes = slice(k * LANES, (k + 1) * LANES)
                    for par, src in ((0, ye_ref), (1, yo_ref)):
                        ys_ref[k, pl.ds(b * t_len + par, N_PAIR, stride=2), :] = (
                            src[pl.ds(b * PAIR_PITCH, N_PAIR), lanes])
            y = jnp.concatenate([ys_ref[k] for k in range(N_BLK)], axis=1)
            u = jnp.concatenate([us_ref[k] for k in range(N_BLK)], axis=1)
            s_out = _glu_tail(y, u, d_ref, wglu_ref, bglu_ref)
            mix_ref[:, :, 0:SSM_WIDTH] = s_out.reshape(nb, t_len, SSM_WIDTH).astype(mix_ref.dtype)

        def conv():
            v = v_ref[:, SUBLANES:SUBLANES + t_len, :]
            z = cw_ref[0, 0:1, :] * v_ref[:, SUBLANES - 2:SUBLANES - 2 + t_len, :]
            z = z + cw_ref[0, 1:2, :] * v_ref[:, SUBLANES - 1:SUBLANES - 1 + t_len, :]
            z = z + cw_ref[0, 2:3, :] * v
            mix_ref[:, :, SSM_WIDTH:] = (g_ref[...] * z).astype(mix_ref.dtype)
            v_ref[:, 0:SUBLANES, :] = v_ref[:, t_len:t_len + SUBLANES, :]

        q = N_PAIR // 4
        pieces = [
            lambda: (conv(), build_lhs(), b_proj((0,))), lambda: b_proj((1,)), lambda: b_proj((2,)),
            lambda: b_proj((3,)),
            lambda: scan(0, q), lambda: scan(q, 2 * q), lambda: scan(2 * q, 3 * q), lambda: scan(3 * q, N_PAIR),
            lambda: c_proj((0, 1)), lambda: c_proj((2, 3)),
            glu_emit,
        ]
        assert len(pieces) <= D_FF // FFN_TF

        db = n_dec // t_len
        x_top = jnp.where(step == last, xd_ref[...], x_ref[0:db].reshape(n_dec, D_MODEL))
        x = jnp.concatenate([x_top, x_ref[db:].reshape((nb - db) * t_len, D_MODEL)], axis=0)
        xn = _rms(x, n1_ref[...]).astype(jnp.bfloat16)
        for c, j in enumerate(range(0, D_FF, FFN_TF)):
            _swiglu_stage(xn, wg_ref, wu_ref, a_ref, j, j + FFN_TF)
            if c < len(pieces):
                pieces[c]()
        h = x + 0.5 * jnp.dot(a_ref[...], wd_ref[...], preferred_element_type=jnp.float32)
        hn = _rms(h, nm_ref[...]).astype(jnp.bfloat16)
        p = jnp.dot(hn, win_ref[...], preferred_element_type=jnp.float32)
        h_ref[...] = h.reshape(nb, t_len, D_MODEL)
        v_new = p[:, SSM_WIDTH + 2 * CONV_WIDTH:] * p[:, SSM_WIDTH:SSM_WIDTH + CONV_WIDTH]
        v_ref[:, SUBLANES:SUBLANES + t_len, :] = v_new.reshape(nb, t_len, CONV_WIDTH)
        g_ref[...] = p[:, SSM_WIDTH + CONV_WIDTH:SSM_WIDTH + 2 * CONV_WIDTH].reshape(nb, t_len, CONV_WIDTH)
        for k in range(N_BLK):
            us_ref[k] = p[:, k * LANES:(k + 1) * LANES]

    @pl.when(step == last)
    def _final():
        for n, (k, j) in enumerate(tiles):
            for out_ref, slab in ((sre_ref, k * SLABS_PER_BLK + j), (sim_ref, k * SLABS_PER_BLK + HALF + j)):
                tile = state_ref[slab]
                for half in range(LANES // SSM_STATE):
                    group = (k * BLK_STATE + j * LANES) // SSM_STATE + half
                    out_ref[:, group, :] = tile[:, half * SSM_STATE:(half + 1) * SSM_STATE]
        nconv_ref[...] = v_ref[:, SUBLANES - 2:SUBLANES, :]
        for k in range(N_BLK):
            ud_ref[:, k * LANES:(k + 1) * LANES] = us_ref[k, 0:n_dec, :]
        for b in range(n_dec // t_len):
            vd_ref[b * t_len:(b + 1) * t_len, :] = v_ref[b, SUBLANES:SUBLANES + t_len, :]
            gd_ref[b * t_len:(b + 1) * t_len, :] = g_ref[b]


def _ffn1_mixer(x, xd, n1, wg, wu, wd, nm, win, a2_re, a2_im, wb, wct, wdm, d, wglu, bglu, cw):
    nb, seq, _ = x.shape
    n_dec = xd.shape[0]
    n_chunk = seq // SCAN_T
    prow = nb * PAIR_PITCH
    f32, bf = jnp.float32, jnp.bfloat16
    main = lambda i: jnp.maximum(i - W1_STEPS, 0)
    return pl.pallas_call(
        _ffn1_mixer_kernel,
        grid=(W1_STEPS + n_chunk + 1,),
        in_specs=[
            pl.BlockSpec((nb, SCAN_T, D_MODEL), lambda i: (0, jnp.minimum(main(i), n_chunk - 1), 0)),
            pl.BlockSpec((n_dec, None, D_MODEL), lambda i: (0, 0, 0), pipeline_mode=pl.Buffered(1)),
            _whole((1, D_MODEL)),
            _weight_chunk((D_MODEL // W1_STEPS, D_FF), W1_STEPS),
            _weight_chunk((D_MODEL // W1_STEPS, D_FF), W1_STEPS),
            _weight_chunk((D_FF // W1_STEPS, D_MODEL), W1_STEPS),
            _whole((1, D_MODEL)),
            _weight_chunk((D_MODEL // W1_STEPS, IN_PROJ_WIDTH), W1_STEPS),
            _whole((N_TILE, SUBLANES, LANES)),
            _whole((N_TILE, SUBLANES, LANES)),
            _whole((N_BLK, 2 * LANES, 2 * BLK_STATE)),
            _whole((N_BLK, 2 * LANES, 2 * BLK_STATE)),
            _whole((N_BLK, LANES, LANES)),
            _whole((1, SSM_WIDTH)),
            _whole((SSM_WIDTH, SSM_WIDTH)),
            _whole((1, SSM_WIDTH)),
            _whole((1, CONV_K, CONV_WIDTH)),
        ],
        out_specs=[
            pl.BlockSpec((nb, SCAN_T, D_MODEL), lambda i: (0, main(i), 0)),
            pl.BlockSpec((nb, SCAN_T, D_MODEL), lambda i: (0, jnp.clip(main(i) - 1, 0, n_chunk - 1), 0)),
            pl.BlockSpec((nb, N_SSM_GROUPS, SSM_STATE), lambda i: (0, 0, 0)),
            pl.BlockSpec((nb, N_SSM_GROUPS, SSM_STATE), lambda i: (0, 0, 0)),
            pl.BlockSpec((nb, CONV_K - 1, CONV_WIDTH), lambda i: (0, 0, 0)),
            pl.BlockSpec((n_dec, SSM_WIDTH), lambda i: (0, 0)),
            pl.BlockSpec((n_dec, CONV_WIDTH), lambda i: (0, 0)),
            pl.BlockSpec((n_dec, CONV_WIDTH), lambda i: (0, 0)),
        ],
        out_shape=[
            jax.ShapeDtypeStruct((nb, seq + SCAN_T, D_MODEL), f32),
            jax.ShapeDtypeStruct((nb, seq, D_MODEL), bf),
            jax.ShapeDtypeStruct((nb, N_SSM_GROUPS, SSM_STATE), f32),
            jax.ShapeDtypeStruct((nb, N_SSM_GROUPS, SSM_STATE), f32),
            jax.ShapeDtypeStruct((nb, CONV_K - 1, CONV_WIDTH), f32),
            jax.ShapeDtypeStruct((n_dec, SSM_WIDTH), f32),
            jax.ShapeDtypeStruct((n_dec, CONV_WIDTH), f32),
            jax.ShapeDtypeStruct((n_dec, CONV_WIDTH), f32),
        ],
        scratch_shapes=[
            *[pltpu.VMEM((SLABS_PER_BLK, prow, LANES), f32) for _ in range(N_BLK)],
            pltpu.VMEM((N_BLK, nb * SCAN_T, LANES), f32),
            pltpu.VMEM((N_BLK, nb * SCAN_T, LANES), f32),
            pltpu.VMEM((prow, 2 * SSM_WIDTH), f32),
            pltpu.VMEM((prow, SSM_WIDTH), f32),
            pltpu.VMEM((prow, SSM_WIDTH), f32),
            pltpu.VMEM((N_BLK, SUBLANES + prow, LANES), f32),
            pltpu.VMEM((N_BLK, SUBLANES, LANES), f32),
            pltpu.VMEM((nb, SCAN_T + SUBLANES, CONV_WIDTH), f32),
            pltpu.VMEM((nb, SCAN_T, CONV_WIDTH), f32),
            pltpu.VMEM((2 * N_TILE, SUBLANES, LANES), f32),
            pltpu.VMEM((D_MODEL, D_FF), bf),
            pltpu.VMEM((D_MODEL, D_FF), bf),
            pltpu.VMEM((D_FF, D_MODEL), bf),
            pltpu.VMEM((D_MODEL, IN_PROJ_WIDTH), bf),
            pltpu.VMEM((nb * SCAN_T, D_FF), bf),
        ],
        compiler_params=pltpu.CompilerParams(
            dimension_semantics=("arbitrary",), vmem_limit_bytes=VMEM_LIMIT),
        name="ffn1_mixer",
    )(x, xd, n1, wg, wu, wd, nm, win, a2_re, a2_im, wb, wct, wdm, d, wglu, bglu, cw)


def _ffn2_rows(h, mix, weights, a_ref):
    wout_ref, n2_ref, wg_ref, wu_ref, wd_ref, nf_ref = weights
    h = h + jnp.dot(mix, wout_ref[...], preferred_element_type=jnp.float32)
    hn = _rms(h, n2_ref[...]).astype(jnp.bfloat16)
    h = h + 0.5 * _swiglu_staged(hn, wg_ref, wu_ref, wd_ref, a_ref)
    return _rms(h, nf_ref[...])


def _ffn2_kernel(h_ref, mix_ref, hd0_ref, hd1_ref, mixd_ref, wout_c, wg_c, wu_c, wd_c, n2_ref, nf_ref,
                 y_ref, yd_ref, wout_ref, wg_ref, wu_ref, wd_ref, a0_ref, a1_ref):
    step = pl.program_id(0)
    last = pl.num_programs(0) - 1
    weights = (wout_ref, n2_ref, wg_ref, wu_ref, wd_ref, nf_ref)

    @pl.when(step < W2_STEPS)
    def _load_weights():
        _cast_weight_chunks(step, ((wout_c, wout_ref), (wg_c, wg_ref), (wu_c, wu_ref), (wd_c, wd_ref)))

    @pl.when((step >= W2_STEPS) & (step < last))
    def _prompt_tile():
        assert h_ref.shape[0] == 2 * FFN_SUB
        ra, rb = slice(0, FFN_SUB), slice(FFN_SUB, 2 * FFN_SUB)
        h_a = h_ref[ra, :] + jnp.dot(mix_ref[ra, :], wout_ref[...], preferred_element_type=jnp.float32)
        h_b = h_ref[rb, :] + jnp.dot(mix_ref[rb, :], wout_ref[...], preferred_element_type=jnp.float32)
        hn_a = _rms(h_a, n2_ref[...]).astype(jnp.bfloat16)
        _swiglu_stage(hn_a, wg_ref, wu_ref, a0_ref, 0, FFN_TF)
        hn_b = _rms(h_b, n2_ref[...]).astype(jnp.bfloat16)
        _swiglu_stage(hn_a, wg_ref, wu_ref, a0_ref, FFN_TF, D_FF)
        f_a = jnp.dot(a0_ref[...], wd_ref[...], preferred_element_type=jnp.float32)
        _swiglu_stage(hn_b, wg_ref, wu_ref, a1_ref, 0, FFN_TF)
        y_ref[ra, :] = _rms(h_a + 0.5 * f_a, nf_ref[...])
        _swiglu_stage(hn_b, wg_ref, wu_ref, a1_ref, FFN_TF, D_FF)
        f_b = jnp.dot(a1_ref[...], wd_ref[...], preferred_element_type=jnp.float32)
        y_ref[rb, :] = _rms(h_b + 0.5 * f_b, nf_ref[...])

    @pl.when(step == last)
    def _decode_rows():
        h_d = jnp.concatenate([hd0_ref[...], hd1_ref[...]], axis=0)
        yd_ref[...] = _ffn2_rows(h_d, mixd_ref[...], weights, a0_ref)


def _ffn2(h_all, mix, mixd, wout, n2, wg, wu, wd, nf):
    nb, seq, _ = mix.shape
    seq_all = h_all.shape[1]
    n_dec = mixd.shape[0]
    tail = seq_all - seq
    assert n_dec == 2 * tail
    per_b = seq // FFN_TM
    n_tiles = nb * per_b
    h_all = h_all.reshape(nb * seq_all, D_MODEL)
    mix = mix.reshape(nb * seq, D_MODEL)
    idx = lambda i: jnp.clip(i - W2_STEPS, 0, n_tiles - 1)
    tile = lambda i: (idx(i), 0)
    assert seq_all % SUBLANES == 0
    tile_all = lambda i: (pl.multiple_of((idx(i) // per_b) * seq_all + (idx(i) % per_b) * FFN_TM, SUBLANES), 0)
    yp, yd = pl.pallas_call(
        _ffn2_kernel,
        grid=(W2_STEPS + n_tiles + 1,),
        in_specs=[
            pl.BlockSpec((pl.Element(FFN_TM), pl.Element(D_MODEL)), tile_all),
            pl.BlockSpec((FFN_TM, D_MODEL), tile),
            pl.BlockSpec((pl.Element(tail), pl.Element(D_MODEL)), lambda i: (seq, 0),
                         pipeline_mode=pl.Buffered(1)),
            pl.BlockSpec((pl.Element(tail), pl.Element(D_MODEL)), lambda i: (seq_all + seq, 0),
                         pipeline_mode=pl.Buffered(1)),
            _whole(mixd.shape),
            _weight_chunk((D_MODEL // W2_STEPS, D_MODEL), W2_STEPS),
            _weight_chunk((D_MODEL // W2_STEPS, D_FF), W2_STEPS),
            _weight_chunk((D_MODEL // W2_STEPS, D_FF), W2_STEPS),
            _weight_chunk((D_FF // W2_STEPS, D_MODEL), W2_STEPS),
            _whole((1, D_MODEL)),
            _whole((1, D_MODEL)),
        ],
        out_specs=[
            pl.BlockSpec((FFN_TM, D_MODEL), tile),
            pl.BlockSpec((n_dec, None, D_MODEL), lambda i: (0, 0, 0)),
        ],
        out_shape=[
            jax.ShapeDtypeStruct((nb * seq, D_MODEL), jnp.float32),
            jax.ShapeDtypeStruct((n_dec, 1, D_MODEL), jnp.float32),
        ],
        scratch_shapes=[
            pltpu.VMEM((D_MODEL, D_MODEL), jnp.bfloat16),
            pltpu.VMEM((D_MODEL, D_FF), jnp.bfloat16),
            pltpu.VMEM((D_MODEL, D_FF), jnp.bfloat16),
            pltpu.VMEM((D_FF, D_MODEL), jnp.bfloat16),
            pltpu.VMEM((FFN_SUB, D_FF), jnp.bfloat16),
            pltpu.VMEM((FFN_SUB, D_FF), jnp.bfloat16),
        ],
        compiler_params=pltpu.CompilerParams(
            dimension_semantics=("arbitrary",), vmem_limit_bytes=VMEM_LIMIT),
        name="outproj_ffn2",
    )(h_all, mix, h_all, h_all, mixd, wout, wg, wu, wd, n2, nf)
    return yp.reshape(nb, seq, D_MODEL), yd


def _mixer_step_kernel(u_ref, v_ref, g_ref, hre_ref, him_ref, buf_ref, are_ref, aim_ref, bb_ref, wct_ref,
                       d_ref, wglu_ref, bglu_ref, cw_ref, mix_ref, sre_ref, sim_ref, nconv_ref):
    u = u_ref[...]
    u_bf = u.astype(jnp.bfloat16)
    ys = []
    for k in range(N_BLK):
        bu = jnp.dot(u_bf[:, k * LANES:(k + 1) * LANES], bb_ref[k], preferred_element_type=jnp.float32)
        st = slice(k * BLK_STATE, (k + 1) * BLK_STATE)
        a_re = are_ref[k:k + 1, :]
        a_im = aim_ref[k:k + 1, :]
        h_re = hre_ref[:, st]
        h_im = him_ref[:, st]
        x_re = (a_re * h_re + bu[:, :BLK_STATE]) - a_im * h_im
        x_im = (a_re * h_im + bu[:, BLK_STATE:]) + a_im * h_re
        sre_ref[:, st] = x_re
        sim_ref[:, st] = x_im
        x = jnp.concatenate([x_re, x_im], axis=1).astype(jnp.bfloat16)
        ys.append(lax.dot_general(x, wct_ref[k, 0:LANES, :], _NT, preferred_element_type=jnp.float32))
    y = jnp.concatenate(ys, axis=1)
    mix_ref[:, 0:SSM_WIDTH] = _glu_tail(y, u, d_ref, wglu_ref, bglu_ref).astype(mix_ref.dtype)

    v = v_ref[...]
    z = cw_ref[0, 0:1, :] * buf_ref[0] + cw_ref[0, 1:2, :] * buf_ref[1] + cw_ref[0, 2:3, :] * v
    mix_ref[:, SSM_WIDTH:] = (g_ref[...] * z).astype(mix_ref.dtype)
    nconv_ref[0] = buf_ref[1]
    nconv_ref[1] = v


def _mixer_step(u, v, g, h_re, h_im, buf, a_re, a_im, bb, wct, d, wglu, bglu, cw):
    nb = u.shape[0]
    return pl.pallas_call(
        _mixer_step_kernel,
        out_shape=[
            jax.ShapeDtypeStruct((nb, D_MODEL), jnp.bfloat16),
            jax.ShapeDtypeStruct((nb, N_STATE), jnp.float32),
            jax.ShapeDtypeStruct((nb, N_STATE), jnp.float32),
            jax.ShapeDtypeStruct((CONV_K - 1, nb, CONV_WIDTH), jnp.float32),
        ],
        compiler_params=pltpu.CompilerParams(vmem_limit_bytes=VMEM_LIMIT),
        name="mixer_step",
    )(u, v, g, h_re, h_im, buf, a_re, a_im, bb, wct, d, wglu, bglu, cw)


def _ssm_layout(lam_re, lam_im, log_dt, b_re, b_im, c_re, c_im):
    gpb = N_SSM_GROUPS // N_BLK

    def b_rows(b):
        return b.reshape(N_BLK, gpb, SSM_STATE, SSM_GROUP).transpose(0, 1, 3, 2).reshape(N_BLK, LANES, SSM_STATE)

    def c_rows(c):
        return c.reshape(N_BLK, LANES, SSM_STATE)

    return lam_re, lam_im, log_dt, b_rows(b_re), b_rows(b_im), c_rows(c_re), c_rows(c_im)


def kernel(x_prompt, x_sample, state_ssm_re, state_ssm_im, state_conv, ffn1_norm, ffn1_w_gate, ffn1_w_up, ffn1_w_down, mix_norm, w_in, ssm_lambda_re, ssm_lambda_im, ssm_log_dt, ssm_b_re, ssm_b_im, ssm_c_re, ssm_c_im, ssm_d, ssm_w_glu, ssm_b_glu, conv_w, w_out, ffn2_norm, ffn2_w_gate, ffn2_w_up, ffn2_w_down, final_norm):
    assert ffn1_norm.shape[0] == 1, "single-layer trunk"
    n_prompt, seq, _ = x_prompt.shape
    n_dec, dec_seq, _ = x_sample.shape
    assert dec_seq == 1 and n_prompt == SUBLANES and seq % FFN_TM == 0 and seq % SCAN_T == 0
    assert n_dec % SCAN_T == 0 and n_dec <= n_prompt * SCAN_T and FFN_TM % FFN_SUB == 0

    n1 = ffn1_norm[0][None, :]
    nm = mix_norm[0][None, :]
    n2 = ffn2_norm[0][None, :]
    nf = final_norm[None, :]
    wglu = ssm_w_glu[0]
    layout = _ssm_layout(ssm_lambda_re[0], ssm_lambda_im[0], ssm_log_dt, ssm_b_re[0], ssm_b_im[0],
                         ssm_c_re[0], ssm_c_im[0])
    a_re, a_im, a2_re, a2_im, bb, wb, wct, wdm = _discretize(*layout)
    d = ssm_d[0][None, :]
    bglu = ssm_b_glu[0][None, :]
    cw = conv_w

    h_all, mixp, p_re, p_im, p_conv, ud, vd, gd = _ffn1_mixer(
        x_prompt, x_sample, n1, ffn1_w_gate[0], ffn1_w_up[0], ffn1_w_down[0], nm,
        w_in[0], a2_re, a2_im, wb, wct, wdm, d, wglu, bglu, cw)

    s_re0 = state_ssm_re[0].reshape(n_dec, N_STATE)
    s_im0 = state_ssm_im[0].reshape(n_dec, N_STATE)
    buf = state_conv[0].transpose(1, 0, 2)
    mixd, s_re, s_im, s_conv = _mixer_step(ud, vd, gd, s_re0, s_im0, buf, a_re, a_im, bb, wct,
                                           d, wglu, bglu, cw)

    yp, yd = _ffn2(h_all, mixp, mixd, w_out[0], n2, ffn2_w_gate[0], ffn2_w_up[0], ffn2_w_down[0], nf)

    gs = (N_SSM_GROUPS, SSM_STATE)
    return (yp,
            yd,
            p_re[None],
            p_im[None],
            p_conv[None],
            s_re.reshape(1, n_dec, *gs),
            s_im.reshape(1, n_dec, *gs),
            s_conv.transpose(1, 0, 2)[None])
```

```python
import jax
import jax.numpy as jnp
from jax import lax
from jax.experimental import pallas as pl
from jax.experimental.pallas import tpu as pltpu

D_MODEL = 1024
D_FF = 2816
SSM_WIDTH = 512
CONV_WIDTH = 512
SSM_GROUP = 16
N_SSM_GROUPS = 32
SSM_STATE = 64
CONV_K = 3
IN_PROJ_WIDTH = SSM_WIDTH + 3 * CONV_WIDTH
EPS = 1e-6

LANES = 128
SUBLANES = 8
N_STATE = N_SSM_GROUPS * SSM_STATE
N_BLK = SSM_WIDTH // LANES
BLK_STATE = N_STATE // N_BLK
SLABS_PER_BLK = 2 * BLK_STATE // LANES
HALF = SLABS_PER_BLK // 2
N_TILE = N_BLK * HALF

FFN_TM = 1024
FFN_SUB = 512
W1_STEPS = 16
W2_STEPS = 8
FFN_TF = 256
SCAN_T = 64
N_PAIR = SCAN_T // 2
PAIR_PITCH = N_PAIR + 4
VMEM_LIMIT = 62 * 1024 * 1024

_NT = (((1,), (1,)), ((), ()))


def _rms(x, g):
    r = lax.rsqrt(jnp.mean(x * x, axis=-1, keepdims=True) + EPS)
    return x * r * g


def _bdot(a, b):
    return jnp.dot(a.astype(jnp.bfloat16), b.astype(jnp.bfloat16), preferred_element_type=jnp.float32)


def _swiglu_stage(xn, wg_ref, wu_ref, a_ref, lo, hi):
    rows = xn.shape[0]
    for j in range(lo, hi, FFN_TF):
        g = jnp.dot(xn, wg_ref[:, j:j + FFN_TF], preferred_element_type=jnp.float32)
        u = jnp.dot(xn, wu_ref[:, j:j + FFN_TF], preferred_element_type=jnp.float32)
        a_ref[0:rows, j:j + FFN_TF] = ((g * jax.nn.sigmoid(g)) * u).astype(a_ref.dtype)


def _swiglu_staged(xn, wg_ref, wu_ref, wd_ref, a_ref):
    rows = xn.shape[0]
    _swiglu_stage(xn, wg_ref, wu_ref, a_ref, 0, D_FF)
    return jnp.dot(a_ref[0:rows, :], wd_ref[...], preferred_element_type=jnp.float32)


def _cast_weight_chunks(step, pairs):
    for src, dst in pairs:
        n = src.shape[0]
        dst[pl.ds(pl.multiple_of(step * n, n), n), :] = src[...].astype(dst.dtype)


def _whole(shape):
    return pl.BlockSpec(shape, lambda i: (0,) * len(shape), pipeline_mode=pl.Buffered(1))


def _weight_chunk(shape, n_steps):
    return pl.BlockSpec(shape, lambda i: (jnp.minimum(i, n_steps - 1), 0))


def _glu_tail(y, u, d_ref, wglu_ref, bglu_ref):
    y = y + d_ref[...] * u
    z = jax.nn.gelu(y)
    gate = jax.nn.sigmoid(_bdot(z, wglu_ref[...]) + bglu_ref[...])
    return z * gate


def _discretize_kernel(lr_ref, li_ref, ldt_ref, bre_ref, bim_ref, ctre_ref, ctim_ref,
                       are_ref, aim_ref, a2re_ref, a2im_ref, bb_ref, wb_ref, wct_ref, wdm_ref):
    row_group = lax.broadcasted_iota(jnp.int32, (LANES, BLK_STATE), 0) // SSM_GROUP
    col_group = lax.broadcasted_iota(jnp.int32, (LANES, BLK_STATE), 1) // SSM_STATE
    on_diagonal = row_group == col_group

    def blocks(m):
        pair = jnp.concatenate([m, m], axis=1)
        return jnp.where(on_diagonal, jnp.concatenate([pair] * (BLK_STATE // LANES), axis=1), 0.0)

    gpb = N_SSM_GROUPS // N_BLK

    def state_rows(ref):
        return jnp.concatenate(
            [jnp.concatenate([ref[k * gpb + g:k * gpb + g + 1, :] for g in range(gpb)], axis=1)
             for k in range(N_BLK)], axis=0)

    def per_state(row_ref):
        vals = jnp.broadcast_to(row_ref[...], (SUBLANES, N_SSM_GROUPS))
        group = lax.broadcasted_iota(jnp.int32, (N_SSM_GROUPS, BLK_STATE), 0)
        col_group = lax.broadcasted_iota(jnp.int32, (N_SSM_GROUPS, BLK_STATE), 1) // SSM_STATE
        rows = [jnp.dot(vals, (group == k * gpb + col_group).astype(jnp.float32),
                        precision=lax.Precision.HIGHEST, preferred_element_type=jnp.float32)[0:1, :]
                for k in range(N_BLK)]
        return jnp.concatenate(rows, axis=0)

    lr, li = state_rows(lr_ref), state_rows(li_ref)
    dt = jnp.exp(per_state(ldt_ref))
    mag = jnp.exp(lr * dt)
    ab_re = mag * jnp.cos(li * dt)
    ab_im = mag * jnp.sin(li * dt)
    den = lr * lr + li * li
    nr = ab_re - 1.0
    ni = ab_im
    coef_re = (nr * lr + ni * li) / den
    coef_im = (ni * lr - nr * li) / den
    are_ref[...] = ab_re
    aim_ref[...] = ab_im
    a2_re = ab_re * ab_re - ab_im * ab_im
    a2_im = 2.0 * (ab_re * ab_im)
    bf = jnp.bfloat16
    for k in range(N_BLK):
        ar, ai = ab_re[k:k + 1, :], ab_im[k:k + 1, :]
        for j in range(HALF):
            lanes = slice(j * LANES, (j + 1) * LANES)
            a2re_ref[k * HALF + j] = jnp.broadcast_to(a2_re[k:k + 1, lanes], (SUBLANES, LANES))
            a2im_ref[k * HALF + j] = jnp.broadcast_to(a2_im[k:k + 1, lanes], (SUBLANES, LANES))
        cr, ci = coef_re[k:k + 1, :], coef_im[k:k + 1, :]
        bre, bim = blocks(bre_ref[k]), blocks(bim_ref[k])
        bb_re = cr * bre - ci * bim
        bb_im = cr * bim + ci * bre
        bb_ref[k, :, 0:BLK_STATE] = bb_re.astype(bf)
        bb_ref[k, :, BLK_STATE:] = bb_im.astype(bf)
        wb_ref[k, 0:LANES, 0:BLK_STATE] = (ar * bb_re - ai * bb_im).astype(bf)
        wb_ref[k, 0:LANES, BLK_STATE:] = (ar * bb_im + ai * bb_re).astype(bf)
        wb_ref[k, LANES:, 0:BLK_STATE] = bb_re.astype(bf)
        wb_ref[k, LANES:, BLK_STATE:] = bb_im.astype(bf)
        ctre, ctim = blocks(ctre_ref[k]), blocks(ctim_ref[k])
        wct_ref[k, 0:LANES, 0:BLK_STATE] = ctre.astype(bf)
        wct_ref[k, 0:LANES, BLK_STATE:] = (-ctim).astype(bf)
        wct_ref[k, LANES:, 0:BLK_STATE] = (ctre * ar - ctim * ai).astype(bf)
        wct_ref[k, LANES:, BLK_STATE:] = (-(ctre * ai + ctim * ar)).astype(bf)
        wdm = (lax.dot_general(bb_re, ctre, _NT, precision=lax.Precision.HIGHEST,
                               preferred_element_type=jnp.float32)
               - lax.dot_general(bb_im, ctim, _NT, precision=lax.Precision.HIGHEST,
                                 preferred_element_type=jnp.float32))
        wdm_ref[k] = wdm.astype(bf)


def _discretize(lr, li, ldt, bre, bim, ctre, ctim):
    f32, bf = jnp.float32, jnp.bfloat16
    return pl.pallas_call(
        _discretize_kernel,
        out_shape=[
            jax.ShapeDtypeStruct((N_BLK, BLK_STATE), f32),
            jax.ShapeDtypeStruct((N_BLK, BLK_STATE), f32),
            jax.ShapeDtypeStruct((N_TILE, SUBLANES, LANES), f32),
            jax.ShapeDtypeStruct((N_TILE, SUBLANES, LANES), f32),
            jax.ShapeDtypeStruct((N_BLK, LANES, 2 * BLK_STATE), bf),
            jax.ShapeDtypeStruct((N_BLK, 2 * LANES, 2 * BLK_STATE), bf),
            jax.ShapeDtypeStruct((N_BLK, 2 * LANES, 2 * BLK_STATE), bf),
            jax.ShapeDtypeStruct((N_BLK, LANES, LANES), bf),
        ],
        name="ssm_discretize",
    )(lr, li, ldt, bre, bim, ctre, ctim)


def _ffn1_mixer_kernel(x_ref, xd_ref, n1_ref, wg_c, wu_c, wd_c, nm_ref, win_c,
                       a2re_ref, a2im_ref, wb_ref, wct_ref, wdm_ref, d_ref, wglu_ref, bglu_ref, cw_ref,
                       h_ref, mix_ref, sre_ref, sim_ref, nconv_ref, ud_ref, vd_ref, gd_ref,
                       xs0_ref, xs1_ref, xs2_ref, xs3_ref, us_ref, ys_ref, l_ref, yo_ref, ye_ref, e_ref,
                       ecarry_ref, v_ref, g_ref, state_ref, wg_ref, wu_ref, wd_ref, win_ref, a_ref, xdd_ref):
    xs_refs = (xs0_ref, xs1_ref, xs2_ref, xs3_ref)
    step = pl.program_id(0)
    last = pl.num_programs(0) - 1
    nb, t_len, _ = x_ref.shape
    n_dec = xd_ref.shape[0]
    prow = nb * PAIR_PITCH
    tiles = [(k, j) for k in range(N_BLK) for j in range(HALF)]

    @pl.when(step < W1_STEPS)
    def _load_weights():
        _cast_weight_chunks(step, ((wg_c, wg_ref), (wu_c, wu_ref), (wd_c, wd_ref), (win_c, win_ref)))

    @pl.when(step == 0)
    def _init():
        for ref in (state_ref, us_ref, l_ref, e_ref, ecarry_ref, v_ref, g_ref):
            ref[...] = jnp.zeros_like(ref)
        xdd_ref[...] = xd_ref[...]

    @pl.when(step >= W1_STEPS)
    def _main():
        carry = {}

        def build_lhs():
            for b in range(nb):
                for k in range(N_BLK):
                    for par in range(2):
                        col = (2 * k + par) * LANES
                        l_ref[pl.ds(b * PAIR_PITCH, N_PAIR), col:col + LANES] = (
                            us_ref[k, pl.ds(b * t_len + par, N_PAIR, stride=2), :])

        def b_proj(ks):
            for k in ks:
                lhs = l_ref[:, 2 * k * LANES:2 * (k + 1) * LANES].astype(jnp.bfloat16)
                v = jnp.dot(lhs, wb_ref[k], preferred_element_type=jnp.float32)
                for j in range(SLABS_PER_BLK):
                    xs_refs[k][j] = v[:, j * LANES:(j + 1) * LANES]

        def scan(r0, r1):
            if r0 == 0:
                carry["re"] = [state_ref[k * SLABS_PER_BLK + j] for k, j in tiles]
                carry["im"] = [state_ref[k * SLABS_PER_BLK + HALF + j] for k, j in tiles]
            xr, xi = carry["re"], carry["im"]
            for r in range(r0, r1):
                rows = pl.ds(r, SUBLANES, stride=PAIR_PITCH)
                for n, (k, j) in enumerate(tiles):
                    a_re = a2re_ref[k * HALF + j]
                    a_im = a2im_ref[k * HALF + j]
                    nr = (a_re * xr[n] + xs_refs[k][j, rows, :]) - a_im * xi[n]
                    ni = (a_re * xi[n] + xs_refs[k][HALF + j, rows, :]) + a_im * xr[n]
                    xs_refs[k][j, rows, :] = nr
                    xs_refs[k][HALF + j, rows, :] = ni
                    xr[n], xi[n] = nr, ni
            if r1 == N_PAIR:
                for n, (k, j) in enumerate(tiles):
                    state_ref[k * SLABS_PER_BLK + j] = xr[n]
                    state_ref[k * SLABS_PER_BLK + HALF + j] = xi[n]

        def c_proj(ks):
            for k in ks:
                lanes = slice(k * LANES, (k + 1) * LANES)
                x = jnp.concatenate([xs_refs[k][j] for j in range(SLABS_PER_BLK)], axis=1)
                ce = lax.dot_general(x.astype(jnp.bfloat16), wct_ref[k], _NT,
                                     preferred_element_type=jnp.float32)
                yo_ref[:, lanes] = ce[:, 0:LANES]
                e_ref[k, SUBLANES:SUBLANES + prow, :] = ce[:, LANES:]
                nxt = e_ref[k, pl.ds(SUBLANES + N_PAIR - 1, nb, stride=PAIR_PITCH), :]
                e_ref[k, pl.ds(SUBLANES - 1, nb, stride=PAIR_PITCH), :] = ecarry_ref[k]
                ecarry_ref[k] = nxt
                u_even = l_ref[:, 2 * k * LANES:(2 * k + 1) * LANES].astype(jnp.bfloat16)
                ye_ref[:, lanes] = (e_ref[k, SUBLANES - 1:SUBLANES - 1 + prow, :]
                                    + jnp.dot(u_even, wdm_ref[k], preferred_element_type=jnp.float32))

        def glu_emit():
            for b in range(nb):
                for k in range(N_BLK):
                    lanes = slice(k * LANES, (k + 1) * LANES)
                    for par, src in ((0, ye_ref), (1, yo_ref)):
                        ys_ref[k, pl.ds(b * t_len + par, N_PAIR, stride=2), :] = (
                            src[pl.ds(b * PAIR_PITCH, N_PAIR), lanes])
            y = jnp.concatenate([ys_ref[k] for k in range(N_BLK)], axis=1)
            u = jnp.concatenate([us_ref[k] for k in range(N_BLK)], axis=1)
            s_out = _glu_tail(y, u, d_ref, wglu_ref, bglu_ref)
            mix_ref[:, :, 0:SSM_WIDTH] = s_out.reshape(nb, t_len, SSM_WIDTH).astype(mix_ref.dtype)

        def conv():
            v = v_ref[:, SUBLANES:SUBLANES + t_len, :]
            z = cw_ref[0, 0:1, :] * v_ref[:, SUBLANES - 2:SUBLANES - 2 + t_len, :]
            z = z + cw_ref[0, 1:2, :] * v_ref[:, SUBLANES - 1:SUBLANES - 1 + t_len, :]
            z = z + cw_ref[0, 2:3, :] * v
            mix_ref[:, :, SSM_WIDTH:] = (g_ref[...] * z).astype(mix_ref.dtype)
            v_ref[:, 0:SUBLANES, :] = v_ref[:, t_len:t_len + SUBLANES, :]

        q = N_PAIR // 4
        pieces = [
            lambda: (conv(), build_lhs(), b_proj((0,))), lambda: b_proj((1,)), lambda: b_proj((2,)),
            lambda: b_proj((3,)),
            lambda: scan(0, q), lambda: scan(q, 2 * q), lambda: scan(2 * q, 3 * q), lambda: scan(3 * q, N_PAIR),
            lambda: c_proj((0, 1)), lambda: c_proj((2, 3)),
            glu_emit,
        ]
        assert len(pieces) <= D_FF // FFN_TF

        db = n_dec // t_len
        x_top = jnp.where(step == last, xdd_ref[...], x_ref[0:db].reshape(n_dec, D_MODEL))
        x = jnp.concatenate([x_top, x_ref[db:].reshape((nb - db) * t_len, D_MODEL)], axis=0)
        xn = _rms(x, n1_ref[...]).astype(jnp.bfloat16)
        for c, j in enumerate(range(0, D_FF, FFN_TF)):
            _swiglu_stage(xn, wg_ref, wu_ref, a_ref, j, j + FFN_TF)
            if c < len(pieces):
                pieces[c]()
        h = x + 0.5 * jnp.dot(a_ref[...], wd_ref[...], preferred_element_type=jnp.float32)
        hn = _rms(h, nm_ref[...]).astype(jnp.bfloat16)
        p = jnp.dot(hn, win_ref[...], preferred_element_type=jnp.float32)
        h_ref[...] = h.reshape(nb, t_len, D_MODEL)
        v_new = p[:, SSM_WIDTH + 2 * CONV_WIDTH:] * p[:, SSM_WIDTH:SSM_WIDTH + CONV_WIDTH]
        v_ref[:, SUBLANES:SUBLANES + t_len, :] = v_new.reshape(nb, t_len, CONV_WIDTH)
        g_ref[...] = p[:, SSM_WIDTH + CONV_WIDTH:SSM_WIDTH + 2 * CONV_WIDTH].reshape(nb, t_len, CONV_WIDTH)
        for k in range(N_BLK):
            us_ref[k] = p[:, k * LANES:(k + 1) * LANES]

    @pl.when(step == last)
    def _final():
        for n, (k, j) in enumerate(tiles):
            for out_ref, slab in ((sre_ref, k * SLABS_PER_BLK + j), (sim_ref, k * SLABS_PER_BLK + HALF + j)):
                tile = state_ref[slab]
                for half in range(LANES // SSM_STATE):
                    group = (k * BLK_STATE + j * LANES) // SSM_STATE + half
                    out_ref[:, group, :] = tile[:, half * SSM_STATE:(half + 1) * SSM_STATE]
        nconv_ref[...] = v_ref[:, SUBLANES - 2:SUBLANES, :]
        for k in range(N_BLK):
            ud_ref[:, k * LANES:(k + 1) * LANES] = us_ref[k, 0:n_dec, :]
        for b in range(n_dec // t_len):
            vd_ref[b * t_len:(b + 1) * t_len, :] = v_ref[b, SUBLANES:SUBLANES + t_len, :]
            gd_ref[b * t_len:(b + 1) * t_len, :] = g_ref[b]


def _ffn1_mixer(x, xd, n1, wg, wu, wd, nm, win, a2_re, a2_im, wb, wct, wdm, d, wglu, bglu, cw):
    nb, seq, _ = x.shape
    n_dec = xd.shape[0]
    n_chunk = seq // SCAN_T
    prow = nb * PAIR_PITCH
    f32, bf = jnp.float32, jnp.bfloat16
    main = lambda i: jnp.maximum(i - W1_STEPS, 0)
    return pl.pallas_call(
        _ffn1_mixer_kernel,
        grid=(W1_STEPS + n_chunk + 1,),
        in_specs=[
            pl.BlockSpec((nb, SCAN_T, D_MODEL), lambda i: (0, jnp.minimum(main(i), n_chunk - 1), 0)),
            pl.BlockSpec((n_dec, None, D_MODEL), lambda i: (0, 0, 0), pipeline_mode=pl.Buffered(1)),
            _whole((1, D_MODEL)),
            _weight_chunk((D_MODEL // W1_STEPS, D_FF), W1_STEPS),
            _weight_chunk((D_MODEL // W1_STEPS, D_FF), W1_STEPS),
            _weight_chunk((D_FF // W1_STEPS, D_MODEL), W1_STEPS),
            _whole((1, D_MODEL)),
            _weight_chunk((D_MODEL // W1_STEPS, IN_PROJ_WIDTH), W1_STEPS),
            _whole((N_TILE, SUBLANES, LANES)),
            _whole((N_TILE, SUBLANES, LANES)),
            _whole((N_BLK, 2 * LANES, 2 * BLK_STATE)),
            _whole((N_BLK, 2 * LANES, 2 * BLK_STATE)),
            _whole((N_BLK, LANES, LANES)),
            _whole((1, SSM_WIDTH)),
            _whole((SSM_WIDTH, SSM_WIDTH)),
            _whole((1, SSM_WIDTH)),
            _whole((1, CONV_K, CONV_WIDTH)),
        ],
        out_specs=[
            pl.BlockSpec((nb, SCAN_T, D_MODEL), lambda i: (0, main(i), 0)),
            pl.BlockSpec((nb, SCAN_T, D_MODEL), lambda i: (0, jnp.clip(main(i) - 1, 0, n_chunk - 1), 0)),
            pl.BlockSpec((nb, N_SSM_GROUPS, SSM_STATE), lambda i: (0, 0, 0)),
            pl.BlockSpec((nb, N_SSM_GROUPS, SSM_STATE), lambda i: (0, 0, 0)),
            pl.BlockSpec((nb, CONV_K - 1, CONV_WIDTH), lambda i: (0, 0, 0)),
            pl.BlockSpec((n_dec, SSM_WIDTH), lambda i: (0, 0)),
            pl.BlockSpec((n_dec, CONV_WIDTH), lambda i: (0, 0)),
            pl.BlockSpec((n_dec, CONV_WIDTH), lambda i: (0, 0)),
        ],
        out_shape=[
            jax.ShapeDtypeStruct((nb, seq + SCAN_T, D_MODEL), f32),
            jax.ShapeDtypeStruct((nb, seq, D_MODEL), bf),
            jax.ShapeDtypeStruct((nb, N_SSM_GROUPS, SSM_STATE), f32),
            jax.ShapeDtypeStruct((nb, N_SSM_GROUPS, SSM_STATE), f32),
            jax.ShapeDtypeStruct((nb, CONV_K - 1, CONV_WIDTH), f32),
            jax.ShapeDtypeStruct((n_dec, SSM_WIDTH), f32),
            jax.ShapeDtypeStruct((n_dec, CONV_WIDTH), f32),
            jax.ShapeDtypeStruct((n_dec, CONV_WIDTH), f32),
        ],
        scratch_shapes=[
            *[pltpu.VMEM((SLABS_PER_BLK, prow, LANES), f32) for _ in range(N_BLK)],
            pltpu.VMEM((N_BLK, nb * SCAN_T, LANES), f32),
            pltpu.VMEM((N_BLK, nb * SCAN_T, LANES), f32),
            pltpu.VMEM((prow, 2 * SSM_WIDTH), f32),
            pltpu.VMEM((prow, SSM_WIDTH), f32),
            pltpu.VMEM((prow, SSM_WIDTH), f32),
            pltpu.VMEM((N_BLK, SUBLANES + prow, LANES), f32),
            pltpu.VMEM((N_BLK, SUBLANES, LANES), f32),
            pltpu.VMEM((nb, SCAN_T + SUBLANES, CONV_WIDTH), f32),
            pltpu.VMEM((nb, SCAN_T, CONV_WIDTH), f32),
            pltpu.VMEM((2 * N_TILE, SUBLANES, LANES), f32),
            pltpu.VMEM((D_MODEL, D_FF), bf),
            pltpu.VMEM((D_MODEL, D_FF), bf),
            pltpu.VMEM((D_FF, D_MODEL), bf),
            pltpu.VMEM((D_MODEL, IN_PROJ_WIDTH), bf),
            pltpu.VMEM((nb * SCAN_T, D_FF), bf),
            pltpu.VMEM((n_dec, D_MODEL), f32),
        ],
        compiler_params=pltpu.CompilerParams(
            dimension_semantics=("arbitrary",), vmem_limit_bytes=VMEM_LIMIT),
        name="ffn1_mixer",
    )(x, xd, n1, wg, wu, wd, nm, win, a2_re, a2_im, wb, wct, wdm, d, wglu, bglu, cw)


def _ffn2_rows(h, mix, weights, a_ref):
    wout_ref, n2_ref, wg_ref, wu_ref, wd_ref, nf_ref = weights
    h = h + jnp.dot(mix, wout_ref[...], preferred_element_type=jnp.float32)
    hn = _rms(h, n2_ref[...]).astype(jnp.bfloat16)
    h = h + 0.5 * _swiglu_staged(hn, wg_ref, wu_ref, wd_ref, a_ref)
    return _rms(h, nf_ref[...])


def _ffn2_kernel(h_ref, mix_ref, hd0_ref, hd1_ref, mixd_ref, wout_c, wg_c, wu_c, wd_c, n2_ref, nf_ref,
                 y_ref, yd_ref, wout_ref, wg_ref, wu_ref, wd_ref, a0_ref, a1_ref):
    step = pl.program_id(0)
    last = pl.num_programs(0) - 1
    weights = (wout_ref, n2_ref, wg_ref, wu_ref, wd_ref, nf_ref)

    @pl.when(step < W2_STEPS)
    def _load_weights():
        _cast_weight_chunks(step, ((wout_c, wout_ref), (wg_c, wg_ref), (wu_c, wu_ref), (wd_c, wd_ref)))

    @pl.when((step >= W2_STEPS) & (step < last))
    def _prompt_tile():
        assert h_ref.shape[0] == 2 * FFN_SUB
        ra, rb = slice(0, FFN_SUB), slice(FFN_SUB, 2 * FFN_SUB)
        h_a = h_ref[ra, :] + jnp.dot(mix_ref[ra, :], wout_ref[...], preferred_element_type=jnp.float32)
        h_b = h_ref[rb, :] + jnp.dot(mix_ref[rb, :], wout_ref[...], preferred_element_type=jnp.float32)
        hn_a = _rms(h_a, n2_ref[...]).astype(jnp.bfloat16)
        _swiglu_stage(hn_a, wg_ref, wu_ref, a0_ref, 0, FFN_TF)
        hn_b = _rms(h_b, n2_ref[...]).astype(jnp.bfloat16)
        _swiglu_stage(hn_a, wg_ref, wu_ref, a0_ref, FFN_TF, D_FF)
        f_a = jnp.dot(a0_ref[...], wd_ref[...], preferred_element_type=jnp.float32)
        _swiglu_stage(hn_b, wg_ref, wu_ref, a1_ref, 0, FFN_TF)
        y_ref[ra, :] = _rms(h_a + 0.5 * f_a, nf_ref[...])
        _swiglu_stage(hn_b, wg_ref, wu_ref, a1_ref, FFN_TF, D_FF)
        f_b = jnp.dot(a1_ref[...], wd_ref[...], preferred_element_type=jnp.float32)
        y_ref[rb, :] = _rms(h_b + 0.5 * f_b, nf_ref[...])

    @pl.when(step == last)
    def _decode_rows():
        h_d = jnp.concatenate([hd0_ref[...], hd1_ref[...]], axis=0)
        yd_ref[...] = _ffn2_rows(h_d, mixd_ref[...], weights, a0_ref)


def _ffn2(h_all, mix, mixd, wout, n2, wg, wu, wd, nf):
    nb, seq, _ = mix.shape
    seq_all = h_all.shape[1]
    n_dec = mixd.shape[0]
    tail = seq_all - seq
    assert n_dec == 2 * tail
    per_b = seq // FFN_TM
    n_tiles = nb * per_b
    h_all = h_all.reshape(nb * seq_all, D_MODEL)
    mix = mix.reshape(nb * seq, D_MODEL)
    idx = lambda i: jnp.clip(i - W2_STEPS, 0, n_tiles - 1)
    tile = lambda i: (idx(i), 0)
    assert seq_all % SUBLANES == 0
    tile_all = lambda i: (pl.multiple_of((idx(i) // per_b) * seq_all + (idx(i) % per_b) * FFN_TM, SUBLANES), 0)
    yp, yd = pl.pallas_call(
        _ffn2_kernel,
        grid=(W2_STEPS + n_tiles + 1,),
        in_specs=[
            pl.BlockSpec((pl.Element(FFN_TM), pl.Element(D_MODEL)), tile_all),
            pl.BlockSpec((FFN_TM, D_MODEL), tile),
            pl.BlockSpec((pl.Element(tail), pl.Element(D_MODEL)), lambda i: (seq, 0),
                         pipeline_mode=pl.Buffered(1)),
            pl.BlockSpec((pl.Element(tail), pl.Element(D_MODEL)), lambda i: (seq_all + seq, 0),
                         pipeline_mode=pl.Buffered(1)),
            _whole(mixd.shape),
            _weight_chunk((D_MODEL // W2_STEPS, D_MODEL), W2_STEPS),
            _weight_chunk((D_MODEL // W2_STEPS, D_FF), W2_STEPS),
            _weight_chunk((D_MODEL // W2_STEPS, D_FF), W2_STEPS),
            _weight_chunk((D_FF // W2_STEPS, D_MODEL), W2_STEPS),
            _whole((1, D_MODEL)),
            _whole((1, D_MODEL)),
        ],
        out_specs=[
            pl.BlockSpec((FFN_TM, D_MODEL), tile),
            pl.BlockSpec((n_dec, None, D_MODEL), lambda i: (0, 0, 0)),
        ],
        out_shape=[
            jax.ShapeDtypeStruct((nb * seq, D_MODEL), jnp.float32),
            jax.ShapeDtypeStruct((n_dec, 1, D_MODEL), jnp.float32),
        ],
        scratch_shapes=[
            pltpu.VMEM((D_MODEL, D_MODEL), jnp.bfloat16),
            pltpu.VMEM((D_MODEL, D_FF), jnp.bfloat16),
            pltpu.VMEM((D_MODEL, D_FF), jnp.bfloat16),
            pltpu.VMEM((D_FF, D_MODEL), jnp.bfloat16),
            pltpu.VMEM((FFN_SUB, D_FF), jnp.bfloat16),
            pltpu.VMEM((FFN_SUB, D_FF), jnp.bfloat16),
        ],
        compiler_params=pltpu.CompilerParams(
            dimension_semantics=("arbitrary",), vmem_limit_bytes=VMEM_LIMIT),
        name="outproj_ffn2",
    )(h_all, mix, h_all, h_all, mixd, wout, wg, wu, wd, n2, nf)
    return yp.reshape(nb, seq, D_MODEL), yd


def _mixer_step_kernel(u_ref, v_ref, g_ref, hre_ref, him_ref, buf_ref, are_ref, aim_ref, bb_ref, wct_ref,
                       d_ref, wglu_ref, bglu_ref, cw_ref, mix_ref, sre_ref, sim_ref, nconv_ref):
    u = u_ref[...]
    u_bf = u.astype(jnp.bfloat16)
    ys = []
    for k in range(N_BLK):
        bu = jnp.dot(u_bf[:, k * LANES:(k + 1) * LANES], bb_ref[k], preferred_element_type=jnp.float32)
        st = slice(k * BLK_STATE, (k + 1) * BLK_STATE)
        a_re = are_ref[k:k + 1, :]
        a_im = aim_ref[k:k + 1, :]
        h_re = hre_ref[:, st]
        h_im = him_ref[:, st]
        x_re = (a_re * h_re + bu[:, :BLK_STATE]) - a_im * h_im
        x_im = (a_re * h_im + bu[:, BLK_STATE:]) + a_im * h_re
        sre_ref[:, st] = x_re
        sim_ref[:, st] = x_im
        x = jnp.concatenate([x_re, x_im], axis=1).astype(jnp.bfloat16)
        ys.append(lax.dot_general(x, wct_ref[k, 0:LANES, :], _NT, preferred_element_type=jnp.float32))
    y = jnp.concatenate(ys, axis=1)
    mix_ref[:, 0:SSM_WIDTH] = _glu_tail(y, u, d_ref, wglu_ref, bglu_ref).astype(mix_ref.dtype)

    v = v_ref[...]
    z = cw_ref[0, 0:1, :] * buf_ref[0] + cw_ref[0, 1:2, :] * buf_ref[1] + cw_ref[0, 2:3, :] * v
    mix_ref[:, SSM_WIDTH:] = (g_ref[...] * z).astype(mix_ref.dtype)
    nconv_ref[0] = buf_ref[1]
    nconv_ref[1] = v


def _mixer_step(u, v, g, h_re, h_im, buf, a_re, a_im, bb, wct, d, wglu, bglu, cw):
    nb = u.shape[0]
    return pl.pallas_call(
        _mixer_step_kernel,
        out_shape=[
            jax.ShapeDtypeStruct((nb, D_MODEL), jnp.bfloat16),
            jax.ShapeDtypeStruct((nb, N_STATE), jnp.float32),
            jax.ShapeDtypeStruct((nb, N_STATE), jnp.float32),
            jax.ShapeDtypeStruct((CONV_K - 1, nb, CONV_WIDTH), jnp.float32),
        ],
        compiler_params=pltpu.CompilerParams(vmem_limit_bytes=VMEM_LIMIT),
        name="mixer_step",
    )(u, v, g, h_re, h_im, buf, a_re, a_im, bb, wct, d, wglu, bglu, cw)


def _ssm_layout(lam_re, lam_im, log_dt, b_re, b_im, c_re, c_im):
    gpb = N_SSM_GROUPS // N_BLK

    def b_rows(b):
        return b.reshape(N_BLK, gpb, SSM_STATE, SSM_GROUP).transpose(0, 1, 3, 2).reshape(N_BLK, LANES, SSM_STATE)

    def c_rows(c):
        return c.reshape(N_BLK, LANES, SSM_STATE)

    return lam_re, lam_im, log_dt, b_rows(b_re), b_rows(b_im), c_rows(c_re), c_rows(c_im)


def kernel(x_prompt, x_sample, state_ssm_re, state_ssm_im, state_conv, ffn1_norm, ffn1_w_gate, ffn1_w_up, ffn1_w_down, mix_norm, w_in, ssm_lambda_re, ssm_lambda_im, ssm_log_dt, ssm_b_re, ssm_b_im, ssm_c_re, ssm_c_im, ssm_d, ssm_w_glu, ssm_b_glu, conv_w, w_out, ffn2_norm, ffn2_w_gate, ffn2_w_up, ffn2_w_down, final_norm):
    assert ffn1_norm.shape[0] == 1, "single-layer trunk"
    n_prompt, seq, _ = x_prompt.shape
    n_dec, dec_seq, _ = x_sample.shape
    assert dec_seq == 1 and n_prompt == SUBLANES and seq % FFN_TM == 0 and seq % SCAN_T == 0
    assert n_dec % SCAN_T == 0 and n_dec <= n_prompt * SCAN_T and FFN_TM % FFN_SUB == 0

    n1 = ffn1_norm[0][None, :]
    nm = mix_norm[0][None, :]
    n2 = ffn2_norm[0][None, :]
    nf = final_norm[None, :]
    wglu = ssm_w_glu[0]
    layout = _ssm_layout(ssm_lambda_re[0], ssm_lambda_im[0], ssm_log_dt, ssm_b_re[0], ssm_b_im[0],
                         ssm_c_re[0], ssm_c_im[0])
    a_re, a_im, a2_re, a2_im, bb, wb, wct, wdm = _discretize(*layout)
    d = ssm_d[0][None, :]
    bglu = ssm_b_glu[0][None, :]
    cw = conv_w

    h_all, mixp, p_re, p_im, p_conv, ud, vd, gd = _ffn1_mixer(
        x_prompt, x_sample, n1, ffn1_w_gate[0], ffn1_w_up[0], ffn1_w_down[0], nm,
        w_in[0], a2_re, a2_im, wb, wct, wdm, d, wglu, bglu, cw)

    s_re0 = state_ssm_re[0].reshape(n_dec, N_STATE)
    s_im0 = state_ssm_im[0].reshape(n_dec, N_STATE)
    buf = state_conv[0].transpose(1, 0, 2)
    mixd, s_re, s_im, s_conv = _mixer_step(ud, vd, gd, s_re0, s_im0, buf, a_re, a_im, bb, wct,
                                           d, wglu, bglu, cw)

    yp, yd = _ffn2(h_all, mixp, mixd, w_out[0], n2, ffn2_w_gate[0], ffn2_w_up[0], ffn2_w_down[0], nf)

    gs = (N_SSM_GROUPS, SSM_STATE)
    return (yp,
            yd,
            p_re[None],
            p_im[None],
            p_conv[None],
            s_re.reshape(1, n_dec, *gs),
            s_im.reshape(1, n_dec, *gs),
            s_conv.transpose(1, 0, 2)[None])
```

```python
import jax
import jax.numpy as jnp
from jax import lax
from jax.experimental import pallas as pl
from jax.experimental.pallas import tpu as pltpu

D_MODEL = 1024
D_FF = 2816
SSM_WIDTH = 512
CONV_WIDTH = 512
SSM_GROUP = 16
N_SSM_GROUPS = 32
SSM_STATE = 64
CONV_K = 3
IN_PROJ_WIDTH = SSM_WIDTH + 3 * CONV_WIDTH
EPS = 1e-6

LANES = 128
SUBLANES = 8
N_STATE = N_SSM_GROUPS * SSM_STATE
N_BLK = SSM_WIDTH // LANES
BLK_STATE = N_STATE // N_BLK
SLABS_PER_BLK = 2 * BLK_STATE // LANES
HALF = SLABS_PER_BLK // 2
N_TILE = N_BLK * HALF

FFN_TM = 1024
FFN_SUB = 512
W1_STEPS = 16
W2_STEPS = 8
FFN_TF = 256
SCAN_T = 64
N_PAIR = SCAN_T // 2
PAIR_PITCH = N_PAIR + 4
VMEM_LIMIT = 62 * 1024 * 1024

_NT = (((1,), (1,)), ((), ()))


def _rms(x, g):
    r = lax.rsqrt(jnp.mean(x * x, axis=-1, keepdims=True) + EPS)
    return x * r * g


def _bdot(a, b):
    return jnp.dot(a.astype(jnp.bfloat16), b.astype(jnp.bfloat16), preferred_element_type=jnp.float32)


def _swiglu_stage(xn, wg_ref, wu_ref, a_ref, lo, hi):
    rows = xn.shape[0]
    for j in range(lo, hi, FFN_TF):
        g = jnp.dot(xn, wg_ref[:, j:j + FFN_TF], preferred_element_type=jnp.float32)
        u = jnp.dot(xn, wu_ref[:, j:j + FFN_TF], preferred_element_type=jnp.float32)
        a_ref[0:rows, j:j + FFN_TF] = ((g * jax.nn.sigmoid(g)) * u).astype(a_ref.dtype)


def _swiglu_staged(xn, wg_ref, wu_ref, wd_ref, a_ref):
    rows = xn.shape[0]
    _swiglu_stage(xn, wg_ref, wu_ref, a_ref, 0, D_FF)
    return jnp.dot(a_ref[0:rows, :], wd_ref[...], preferred_element_type=jnp.float32)


def _cast_weight_chunks(step, pairs):
    for src, dst in pairs:
        n = src.shape[0]
        dst[pl.ds(pl.multiple_of(step * n, n), n), :] = src[...].astype(dst.dtype)


def _whole(shape):
    return pl.BlockSpec(shape, lambda i: (0,) * len(shape), pipeline_mode=pl.Buffered(1))


def _weight_chunk(shape, n_steps):
    return pl.BlockSpec(shape, lambda i: (jnp.minimum(i, n_steps - 1), 0))


def _glu_tail(y, u, d_ref, wglu_ref, bglu_ref):
    y = y + d_ref[...] * u
    z = jax.nn.gelu(y)
    gate = jax.nn.sigmoid(_bdot(z, wglu_ref[...]) + bglu_ref[...])
    return z * gate


def _discretize_kernel(lr_ref, li_ref, ldt_ref, bre_ref, bim_ref, ctre_ref, ctim_ref,
                       are_ref, aim_ref, a2re_ref, a2im_ref, bb_ref, wb_ref, wct_ref, wdm_ref):
    row_group = lax.broadcasted_iota(jnp.int32, (LANES, BLK_STATE), 0) // SSM_GROUP
    col_group = lax.broadcasted_iota(jnp.int32, (LANES, BLK_STATE), 1) // SSM_STATE
    on_diagonal = row_group == col_group

    def blocks(m):
        pair = jnp.concatenate([m, m], axis=1)
        return jnp.where(on_diagonal, jnp.concatenate([pair] * (BLK_STATE // LANES), axis=1), 0.0)

    gpb = N_SSM_GROUPS // N_BLK

    def state_rows(ref):
        return jnp.concatenate(
            [jnp.concatenate([ref[k * gpb + g:k * gpb + g + 1, :] for g in range(gpb)], axis=1)
             for k in range(N_BLK)], axis=0)

    def per_state(row_ref):
        vals = jnp.broadcast_to(row_ref[...], (SUBLANES, N_SSM_GROUPS))
        group = lax.broadcasted_iota(jnp.int32, (N_SSM_GROUPS, BLK_STATE), 0)
        col_group = lax.broadcasted_iota(jnp.int32, (N_SSM_GROUPS, BLK_STATE), 1) // SSM_STATE
        rows = [jnp.dot(vals, (group == k * gpb + col_group).astype(jnp.float32),
                        precision=lax.Precision.HIGHEST, preferred_element_type=jnp.float32)[0:1, :]
                for k in range(N_BLK)]
        return jnp.concatenate(rows, axis=0)

    lr, li = state_rows(lr_ref), state_rows(li_ref)
    dt = jnp.exp(per_state(ldt_ref))
    mag = jnp.exp(lr * dt)
    ab_re = mag * jnp.cos(li * dt)
    ab_im = mag * jnp.sin(li * dt)
    den = lr * lr + li * li
    nr = ab_re - 1.0
    ni = ab_im
    coef_re = (nr * lr + ni * li) / den
    coef_im = (ni * lr - nr * li) / den
    are_ref[...] = ab_re
    aim_ref[...] = ab_im
    a2_re = ab_re * ab_re - ab_im * ab_im
    a2_im = 2.0 * (ab_re * ab_im)
    bf = jnp.bfloat16
    for k in range(N_BLK):
        ar, ai = ab_re[k:k + 1, :], ab_im[k:k + 1, :]
        for j in range(HALF):
            lanes = slice(j * LANES, (j + 1) * LANES)
            a2re_ref[k * HALF + j] = jnp.broadcast_to(a2_re[k:k + 1, lanes], (SUBLANES, LANES))
            a2im_ref[k * HALF + j] = jnp.broadcast_to(a2_im[k:k + 1, lanes], (SUBLANES, LANES))
        cr, ci = coef_re[k:k + 1, :], coef_im[k:k + 1, :]
        bre, bim = blocks(bre_ref[k]), blocks(bim_ref[k])
        bb_re = cr * bre - ci * bim
        bb_im = cr * bim + ci * bre
        bb_ref[k, :, 0:BLK_STATE] = bb_re.astype(bf)
        bb_ref[k, :, BLK_STATE:] = bb_im.astype(bf)
        wb_ref[k, 0:LANES, 0:BLK_STATE] = (ar * bb_re - ai * bb_im).astype(bf)
        wb_ref[k, 0:LANES, BLK_STATE:] = (ar * bb_im + ai * bb_re).astype(bf)
        wb_ref[k, LANES:, 0:BLK_STATE] = bb_re.astype(bf)
        wb_ref[k, LANES:, BLK_STATE:] = bb_im.astype(bf)
        ctre, ctim = blocks(ctre_ref[k]), blocks(ctim_ref[k])
        wct_ref[k, 0:LANES, 0:BLK_STATE] = ctre.astype(bf)
        wct_ref[k, 0:LANES, BLK_STATE:] = (-ctim).astype(bf)
        wct_ref[k, LANES:, 0:BLK_STATE] = (ctre * ar - ctim * ai).astype(bf)
        wct_ref[k, LANES:, BLK_STATE:] = (-(ctre * ai + ctim * ar)).astype(bf)
        wdm = (lax.dot_general(bb_re, ctre, _NT, precision=lax.Precision.HIGHEST,
                               preferred_element_type=jnp.float32)
               - lax.dot_general(bb_im, ctim, _NT, precision=lax.Precision.HIGHEST,
                                 preferred_element_type=jnp.float32))
        wdm_ref[k] = wdm.astype(bf)


def _discretize(lr, li, ldt, bre, bim, ctre, ctim):
    f32, bf = jnp.float32, jnp.bfloat16
    return pl.pallas_call(
        _discretize_kernel,
        out_shape=[
            jax.ShapeDtypeStruct((N_BLK, BLK_STATE), f32),
            jax.ShapeDtypeStruct((N_BLK, BLK_STATE), f32),
            jax.ShapeDtypeStruct((N_TILE, SUBLANES, LANES), f32),
            jax.ShapeDtypeStruct((N_TILE, SUBLANES, LANES), f32),
            jax.ShapeDtypeStruct((N_BLK, LANES, 2 * BLK_STATE), bf),
            jax.ShapeDtypeStruct((N_BLK, 2 * LANES, 2 * BLK_STATE), bf),
            jax.ShapeDtypeStruct((N_BLK, 2 * LANES, 2 * BLK_STATE), bf),
            jax.ShapeDtypeStruct((N_BLK, LANES, LANES), bf),
        ],
        name="ssm_discretize",
    )(lr, li, ldt, bre, bim, ctre, ctim)


def _ffn1_mixer_kernel(x_ref, xd_ref, n1_ref, wg_c, wu_c, wd_c, nm_ref, win_c,
                       a2re_ref, a2im_ref, wb_ref, wct_ref, wdm_ref, d_ref, wglu_ref, bglu_ref, cw_ref,
                       h_ref, mix_ref, sre_ref, sim_ref, nconv_ref, ud_ref, vd_ref, gd_ref,
                       xs0_ref, xs1_ref, xs2_ref, xs3_ref, us_ref, ys_ref, l_ref, yo_ref, ye_ref, e_ref,
                       ecarry_ref, v_ref, g_ref, state_ref, wg_ref, wu_ref, wd_ref, win_ref, a_ref, xdd_ref):
    xs_refs = (xs0_ref, xs1_ref, xs2_ref, xs3_ref)
    step = pl.program_id(0)
    last = pl.num_programs(0) - 1
    nb, t_len, _ = x_ref.shape
    n_dec = xd_ref.shape[0]
    prow = nb * PAIR_PITCH
    tiles = [(k, j) for k in range(N_BLK) for j in range(HALF)]

    @pl.when(step < W1_STEPS)
    def _load_weights():
        _cast_weight_chunks(step, ((wg_c, wg_ref), (wu_c, wu_ref), (wd_c, wd_ref), (win_c, win_ref)))

    @pl.when(step == 0)
    def _init():
        for ref in (state_ref, us_ref, l_ref, e_ref, ecarry_ref, v_ref, g_ref):
            ref[...] = jnp.zeros_like(ref)
        xdd_ref[...] = xd_ref[...]

    @pl.when(step >= W1_STEPS)
    def _main():
        carry = {}

        def build_lhs():
            for b in range(nb):
                for k in range(N_BLK):
                    for par in range(2):
                        col = (2 * k + par) * LANES
                        l_ref[pl.ds(b * PAIR_PITCH, N_PAIR), col:col + LANES] = (
                            us_ref[k, pl.ds(b * t_len + par, N_PAIR, stride=2), :])

        def b_proj(ks):
            for k in ks:
                lhs = l_ref[:, 2 * k * LANES:2 * (k + 1) * LANES].astype(jnp.bfloat16)
                v = jnp.dot(lhs, wb_ref[k], preferred_element_type=jnp.float32)
                for j in range(SLABS_PER_BLK):
                    xs_refs[k][j] = v[:, j * LANES:(j + 1) * LANES]

        def scan(r0, r1):
            if r0 == 0:
                carry["re"] = [state_ref[k * SLABS_PER_BLK + j] for k, j in tiles]
                carry["im"] = [state_ref[k * SLABS_PER_BLK + HALF + j] for k, j in tiles]
            xr, xi = carry["re"], carry["im"]
            for r in range(r0, r1):
                rows = pl.ds(r, SUBLANES, stride=PAIR_PITCH)
                for n, (k, j) in enumerate(tiles):
                    a_re = a2re_ref[k * HALF + j]
                    a_im = a2im_ref[k * HALF + j]
                    nr = (a_re * xr[n] + xs_refs[k][j, rows, :]) - a_im * xi[n]
                    ni = (a_re * xi[n] + xs_refs[k][HALF + j, rows, :]) + a_im * xr[n]
                    xs_refs[k][j, rows, :] = nr
                    xs_refs[k][HALF + j, rows, :] = ni
                    xr[n], xi[n] = nr, ni
            if r1 == N_PAIR:
                for n, (k, j) in enumerate(tiles):
                    state_ref[k * SLABS_PER_BLK + j] = xr[n]
                    state_ref[k * SLABS_PER_BLK + HALF + j] = xi[n]

        def c_proj(ks):
            for k in ks:
                lanes = slice(k * LANES, (k + 1) * LANES)
                x = jnp.concatenate([xs_refs[k][j] for j in range(SLABS_PER_BLK)], axis=1)
                ce = lax.dot_general(x.astype(jnp.bfloat16), wct_ref[k], _NT,
                                     preferred_element_type=jnp.float32)
                yo_ref[:, lanes] = ce[:, 0:LANES]
                e_ref[k, SUBLANES:SUBLANES + prow, :] = ce[:, LANES:]
                nxt = e_ref[k, pl.ds(SUBLANES + N_PAIR - 1, nb, stride=PAIR_PITCH), :]
                e_ref[k, pl.ds(SUBLANES - 1, nb, stride=PAIR_PITCH), :] = ecarry_ref[k]
                ecarry_ref[k] = nxt
                u_even = l_ref[:, 2 * k * LANES:(2 * k + 1) * LANES].astype(jnp.bfloat16)
                ye_ref[:, lanes] = (e_ref[k, SUBLANES - 1:SUBLANES - 1 + prow, :]
                                    + jnp.dot(u_even, wdm_ref[k], preferred_element_type=jnp.float32))

        def glu_emit():
            for b in range(nb):
                for k in range(N_BLK):
                    lanes = slice(k * LANES, (k + 1) * LANES)
                    for par, src in ((0, ye_ref), (1, yo_ref)):
                        ys_ref[k, pl.ds(b * t_len + par, N_PAIR, stride=2), :] = (
                            src[pl.ds(b * PAIR_PITCH, N_PAIR), lanes])
            y = jnp.concatenate([ys_ref[k] for k in range(N_BLK)], axis=1)
            u = jnp.concatenate([us_ref[k] for k in range(N_BLK)], axis=1)
            s_out = _glu_tail(y, u, d_ref, wglu_ref, bglu_ref)
            mix_ref[:, :, 0:SSM_WIDTH] = s_out.reshape(nb, t_len, SSM_WIDTH).astype(mix_ref.dtype)

        def conv():
            v = v_ref[:, SUBLANES:SUBLANES + t_len, :]
            z = cw_ref[0, 0:1, :] * v_ref[:, SUBLANES - 2:SUBLANES - 2 + t_len, :]
            z = z + cw_ref[0, 1:2, :] * v_ref[:, SUBLANES - 1:SUBLANES - 1 + t_len, :]
            z = z + cw_ref[0, 2:3, :] * v
            mix_ref[:, :, SSM_WIDTH:] = (g_ref[...] * z).astype(mix_ref.dtype)
            v_ref[:, 0:SUBLANES, :] = v_ref[:, t_len:t_len + SUBLANES, :]

        q = N_PAIR // 4
        pieces = [
            lambda: (conv(), build_lhs(), b_proj((0,))), lambda: b_proj((1,)), lambda: b_proj((2,)),
            lambda: b_proj((3,)),
            lambda: scan(0, q), lambda: scan(q, 2 * q), lambda: scan(2 * q, 3 * q), lambda: scan(3 * q, N_PAIR),
            lambda: c_proj((0, 1)), lambda: c_proj((2, 3)),
            glu_emit,
        ]
        assert len(pieces) <= D_FF // FFN_TF

        db = n_dec // t_len
        x_top = jnp.where(step == last, xdd_ref[...], x_ref[0:db].reshape(n_dec, D_MODEL))
        x = jnp.concatenate([x_top, x_ref[db:].reshape((nb - db) * t_len, D_MODEL)], axis=0)
        xn = _rms(x, n1_ref[...]).astype(jnp.bfloat16)
        for c, j in enumerate(range(0, D_FF, FFN_TF)):
            _swiglu_stage(xn, wg_ref, wu_ref, a_ref, j, j + FFN_TF)
            if c < len(pieces):
                pieces[c]()
        h = x + 0.5 * jnp.dot(a_ref[...], wd_ref[...], preferred_element_type=jnp.float32)
        hn = _rms(h, nm_ref[...]).astype(jnp.bfloat16)
        p = jnp.dot(hn, win_ref[...], preferred_element_type=jnp.float32)
        h_ref[...] = h.reshape(nb, t_len, D_MODEL)
        v_new = p[:, SSM_WIDTH + 2 * CONV_WIDTH:] * p[:, SSM_WIDTH:SSM_WIDTH + CONV_WIDTH]
        v_ref[:, SUBLANES:SUBLANES + t_len, :] = v_new.reshape(nb, t_len, CONV_WIDTH)
        g_ref[...] = p[:, SSM_WIDTH + CONV_WIDTH:SSM_WIDTH + 2 * CONV_WIDTH].reshape(nb, t_len, CONV_WIDTH)
        for k in range(N_BLK):
            us_ref[k] = p[:, k * LANES:(k + 1) * LANES]

    @pl.when(step == last)
    def _final():
        for n, (k, j) in enumerate(tiles):
            for out_ref, slab in ((sre_ref, k * SLABS_PER_BLK + j), (sim_ref, k * SLABS_PER_BLK + HALF + j)):
                tile = state_ref[slab]
                for half in range(LANES // SSM_STATE):
                    group = (k * BLK_STATE + j * LANES) // SSM_STATE + half
                    out_ref[:, group, :] = tile[:, half * SSM_STATE:(half + 1) * SSM_STATE]
        nconv_ref[...] = v_ref[:, SUBLANES - 2:SUBLANES, :]
        for k in range(N_BLK):
            ud_ref[:, k * LANES:(k + 1) * LANES] = us_ref[k, 0:n_dec, :]
        for b in range(n_dec // t_len):
            vd_ref[b * t_len:(b + 1) * t_len, :] = v_ref[b, SUBLANES:SUBLANES + t_len, :]
            gd_ref[b * t_len:(b + 1) * t_len, :] = g_ref[b]


def _ffn1_mixer(x, xd, n1, wg, wu, wd, nm, win, a2_re, a2_im, wb, wct, wdm, d, wglu, bglu, cw):
    nb, seq, _ = x.shape
    n_dec = xd.shape[0]
    n_chunk = seq // SCAN_T
    prow = nb * PAIR_PITCH
    f32, bf = jnp.float32, jnp.bfloat16
    main = lambda i: jnp.maximum(i - W1_STEPS, 0)
    return pl.pallas_call(
        _ffn1_mixer_kernel,
        grid=(W1_STEPS + n_chunk + 1,),
        in_specs=[
            pl.BlockSpec((nb, SCAN_T, D_MODEL), lambda i: (0, jnp.minimum(main(i), n_chunk - 1), 0)),
            pl.BlockSpec((n_dec, None, D_MODEL), lambda i: (0, 0, 0), pipeline_mode=pl.Buffered(1)),
            _whole((1, D_MODEL)),
            _weight_chunk((D_MODEL // W1_STEPS, D_FF), W1_STEPS),
            _weight_chunk((D_MODEL // W1_STEPS, D_FF), W1_STEPS),
            _weight_chunk((D_FF // W1_STEPS, D_MODEL), W1_STEPS),
            _whole((1, D_MODEL)),
            _weight_chunk((D_MODEL // W1_STEPS, IN_PROJ_WIDTH), W1_STEPS),
            _whole((N_TILE, SUBLANES, LANES)),
            _whole((N_TILE, SUBLANES, LANES)),
            _whole((N_BLK, 2 * LANES, 2 * BLK_STATE)),
            _whole((N_BLK, 2 * LANES, 2 * BLK_STATE)),
            _whole((N_BLK, LANES, LANES)),
            _whole((1, SSM_WIDTH)),
            _whole((SSM_WIDTH, SSM_WIDTH)),
            _whole((1, SSM_WIDTH)),
            _whole((1, CONV_K, CONV_WIDTH)),
        ],
        out_specs=[
            pl.BlockSpec((nb, SCAN_T, D_MODEL), lambda i: (0, main(i), 0)),
            pl.BlockSpec((nb, SCAN_T, D_MODEL), lambda i: (0, jnp.clip(main(i) - 1, 0, n_chunk - 1), 0)),
            pl.BlockSpec((nb, N_SSM_GROUPS, SSM_STATE), lambda i: (0, 0, 0)),
            pl.BlockSpec((nb, N_SSM_GROUPS, SSM_STATE), lambda i: (0, 0, 0)),
            pl.BlockSpec((nb, CONV_K - 1, CONV_WIDTH), lambda i: (0, 0, 0)),
            pl.BlockSpec((n_dec, SSM_WIDTH), lambda i: (0, 0)),
            pl.BlockSpec((n_dec, CONV_WIDTH), lambda i: (0, 0)),
            pl.BlockSpec((n_dec, CONV_WIDTH), lambda i: (0, 0)),
        ],
        out_shape=[
            jax.ShapeDtypeStruct((nb, seq + SCAN_T, D_MODEL), f32),
            jax.ShapeDtypeStruct((nb, seq, D_MODEL), bf),
            jax.ShapeDtypeStruct((nb, N_SSM_GROUPS, SSM_STATE), f32),
            jax.ShapeDtypeStruct((nb, N_SSM_GROUPS, SSM_STATE), f32),
            jax.ShapeDtypeStruct((nb, CONV_K - 1, CONV_WIDTH), f32),
            jax.ShapeDtypeStruct((n_dec, SSM_WIDTH), f32),
            jax.ShapeDtypeStruct((n_dec, CONV_WIDTH), f32),
            jax.ShapeDtypeStruct((n_dec, CONV_WIDTH), f32),
        ],
        scratch_shapes=[
            *[pltpu.VMEM((SLABS_PER_BLK, prow, LANES), f32) for _ in range(N_BLK)],
            pltpu.VMEM((N_BLK, nb * SCAN_T, LANES), f32),
            pltpu.VMEM((N_BLK, nb * SCAN_T, LANES), f32),
            pltpu.VMEM((prow, 2 * SSM_WIDTH), f32),
            pltpu.VMEM((prow, SSM_WIDTH), f32),
            pltpu.VMEM((prow, SSM_WIDTH), f32),
            pltpu.VMEM((N_BLK, SUBLANES + prow, LANES), f32),
            pltpu.VMEM((N_BLK, SUBLANES, LANES), f32),
            pltpu.VMEM((nb, SCAN_T + SUBLANES, CONV_WIDTH), f32),
            pltpu.VMEM((nb, SCAN_T, CONV_WIDTH), f32),
            pltpu.VMEM((2 * N_TILE, SUBLANES, LANES), f32),
            pltpu.VMEM((D_MODEL, D_FF), bf),
            pltpu.VMEM((D_MODEL, D_FF), bf),
            pltpu.VMEM((D_FF, D_MODEL), bf),
            pltpu.VMEM((D_MODEL, IN_PROJ_WIDTH), bf),
            pltpu.VMEM((nb * SCAN_T, D_FF), bf),
            pltpu.VMEM((n_dec, D_MODEL), f32),
        ],
        compiler_params=pltpu.CompilerParams(
            dimension_semantics=("arbitrary",), vmem_limit_bytes=VMEM_LIMIT),
        name="ffn1_mixer",
    )(x, xd, n1, wg, wu, wd, nm, win, a2_re, a2_im, wb, wct, wdm, d, wglu, bglu, cw)


def _ffn2_rows(h, mix, weights, a_ref):
    wout_ref, n2_ref, wg_ref, wu_ref, wd_ref, nf_ref = weights
    h = h + jnp.dot(mix, wout_ref[...], preferred_element_type=jnp.float32)
    hn = _rms(h, n2_ref[...]).astype(jnp.bfloat16)
    h = h + 0.5 * _swiglu_staged(hn, wg_ref, wu_ref, wd_ref, a_ref)
    return _rms(h, nf_ref[...])


def _ffn2_kernel(h_ref, mix_ref, hd0_ref, hd1_ref, mixd_ref, wout_c, wg_c, wu_c, wd_c, n2_ref, nf_ref,
                 y_ref, yd_ref, wout_ref, wg_ref, wu_ref, wd_ref, a0_ref, a1_ref):
    step = pl.program_id(0)
    last = pl.num_programs(0) - 1
    weights = (wout_ref, n2_ref, wg_ref, wu_ref, wd_ref, nf_ref)

    @pl.when(step < W2_STEPS)
    def _load_weights():
        _cast_weight_chunks(step, ((wout_c, wout_ref), (wg_c, wg_ref), (wu_c, wu_ref), (wd_c, wd_ref)))

    @pl.when((step >= W2_STEPS) & (step < last))
    def _prompt_tile():
        assert h_ref.shape[0] == 2 * FFN_SUB
        ra, rb = slice(0, FFN_SUB), slice(FFN_SUB, 2 * FFN_SUB)
        h_a = h_ref[ra, :] + jnp.dot(mix_ref[ra, :], wout_ref[...], preferred_element_type=jnp.float32)
        h_b = h_ref[rb, :] + jnp.dot(mix_ref[rb, :], wout_ref[...], preferred_element_type=jnp.float32)
        hn_a = _rms(h_a, n2_ref[...]).astype(jnp.bfloat16)
        _swiglu_stage(hn_a, wg_ref, wu_ref, a0_ref, 0, FFN_TF)
        hn_b = _rms(h_b, n2_ref[...]).astype(jnp.bfloat16)
        _swiglu_stage(hn_a, wg_ref, wu_ref, a0_ref, FFN_TF, D_FF)
        f_a = jnp.dot(a0_ref[...], wd_ref[...], preferred_element_type=jnp.float32)
        _swiglu_stage(hn_b, wg_ref, wu_ref, a1_ref, 0, FFN_TF)
        y_ref[ra, :] = _rms(h_a + 0.5 * f_a, nf_ref[...])
        _swiglu_stage(hn_b, wg_ref, wu_ref, a1_ref, FFN_TF, D_FF)
        f_b = jnp.dot(a1_ref[...], wd_ref[...], preferred_element_type=jnp.float32)
        y_ref[rb, :] = _rms(h_b + 0.5 * f_b, nf_ref[...])

    @pl.when(step == last)
    def _decode_rows():
        h_d = jnp.concatenate([hd0_ref[...], hd1_ref[...]], axis=0)
        yd_ref[...] = _ffn2_rows(h_d, mixd_ref[...], weights, a0_ref)


def _ffn2(h_all, mix, mixd, wout, n2, wg, wu, wd, nf):
    nb, seq, _ = mix.shape
    seq_all = h_all.shape[1]
    n_dec = mixd.shape[0]
    tail = seq_all - seq
    assert n_dec == 2 * tail
    per_b = seq // FFN_TM
    n_tiles = nb * per_b
    h_all = h_all.reshape(nb * seq_all, D_MODEL)
    mix = mix.reshape(nb * seq, D_MODEL)
    idx = lambda i: jnp.clip(i - W2_STEPS, 0, n_tiles - 1)
    tile = lambda i: (idx(i), 0)
    assert seq_all % SUBLANES == 0
    tile_all = lambda i: (pl.multiple_of((idx(i) // per_b) * seq_all + (idx(i) % per_b) * FFN_TM, SUBLANES), 0)
    yp, yd = pl.pallas_call(
        _ffn2_kernel,
        grid=(W2_STEPS + n_tiles + 1,),
        in_specs=[
            pl.BlockSpec((pl.Element(FFN_TM), pl.Element(D_MODEL)), tile_all),
            pl.BlockSpec((FFN_TM, D_MODEL), tile),
            pl.BlockSpec((pl.Element(tail), pl.Element(D_MODEL)), lambda i: (seq, 0),
                         pipeline_mode=pl.Buffered(1)),
            pl.BlockSpec((pl.Element(tail), pl.Element(D_MODEL)), lambda i: (seq_all + seq, 0),
                         pipeline_mode=pl.Buffered(1)),
            _whole(mixd.shape),
            _weight_chunk((D_MODEL // W2_STEPS, D_MODEL), W2_STEPS),
            _weight_chunk((D_MODEL // W2_STEPS, D_FF), W2_STEPS),
            _weight_chunk((D_MODEL // W2_STEPS, D_FF), W2_STEPS),
            _weight_chunk((D_FF // W2_STEPS, D_MODEL), W2_STEPS),
            _whole((1, D_MODEL)),
            _whole((1, D_MODEL)),
        ],
        out_specs=[
            pl.BlockSpec((FFN_TM, D_MODEL), tile),
            pl.BlockSpec((n_dec, None, D_MODEL), lambda i: (0, 0, 0)),
        ],
        out_shape=[
            jax.ShapeDtypeStruct((nb * seq, D_MODEL), jnp.float32),
            jax.ShapeDtypeStruct((n_dec, 1, D_MODEL), jnp.float32),
        ],
        scratch_shapes=[
            pltpu.VMEM((D_MODEL, D_MODEL), jnp.bfloat16),
            pltpu.VMEM((D_MODEL, D_FF), jnp.bfloat16),
            pltpu.VMEM((D_MODEL, D_FF), jnp.bfloat16),
            pltpu.VMEM((D_FF, D_MODEL), jnp.bfloat16),
            pltpu.VMEM((FFN_SUB, D_FF), jnp.bfloat16),
            pltpu.VMEM((FFN_SUB, D_FF), jnp.bfloat16),
        ],
        compiler_params=pltpu.CompilerParams(
            dimension_semantics=("arbitrary",), vmem_limit_bytes=VMEM_LIMIT),
        name="outproj_ffn2",
    )(h_all, mix, h_all, h_all, mixd, wout, wg, wu, wd, n2, nf)
    return yp.reshape(nb, seq, D_MODEL), yd


def _mixer_step_kernel(u_ref, v_ref, g_ref, hre_ref, him_ref, buf_ref, are_ref, aim_ref, bb_ref, wct_ref,
                       d_ref, wglu_ref, bglu_ref, cw_ref, mix_ref, sre_ref, sim_ref, nconv_ref):
    u = u_ref[...]
    u_bf = u.astype(jnp.bfloat16)
    ys = []
    for k in range(N_BLK):
        bu = jnp.dot(u_bf[:, k * LANES:(k + 1) * LANES], bb_ref[k], preferred_element_type=jnp.float32)
        st = slice(k * BLK_STATE, (k + 1) * BLK_STATE)
        a_re = are_ref[k:k + 1, :]
        a_im = aim_ref[k:k + 1, :]
        h_re = hre_ref[:, st]
        h_im = him_ref[:, st]
        x_re = (a_re * h_re + bu[:, :BLK_STATE]) - a_im * h_im
        x_im = (a_re * h_im + bu[:, BLK_STATE:]) + a_im * h_re
        sre_ref[:, st] = x_re
        sim_ref[:, st] = x_im
        x = jnp.concatenate([x_re, x_im], axis=1).astype(jnp.bfloat16)
        ys.append(lax.dot_general(x, wct_ref[k, 0:LANES, :], _NT, preferred_element_type=jnp.float32))
    y = jnp.concatenate(ys, axis=1)
    mix_ref[:, 0:SSM_WIDTH] = _glu_tail(y, u, d_ref, wglu_ref, bglu_ref).astype(mix_ref.dtype)

    v = v_ref[...]
    buf1 = buf_ref[:, 1, :]
    z = cw_ref[0, 0:1, :] * buf_ref[:, 0, :] + cw_ref[0, 1:2, :] * buf1 + cw_ref[0, 2:3, :] * v
    mix_ref[:, SSM_WIDTH:] = (g_ref[...] * z).astype(mix_ref.dtype)
    nconv_ref[:, 0, :] = buf1
    nconv_ref[:, 1, :] = v


def _mixer_step(u, v, g, h_re, h_im, conv_state, a_re, a_im, bb, wct, d, wglu, bglu, cw):
    nb = u.shape[0]
    conv_spec = pl.BlockSpec((None, nb, CONV_K - 1, CONV_WIDTH), lambda i: (0, 0, 0, 0))
    rows = (u, v, g, h_re, h_im)
    consts = (a_re, a_im, bb, wct, d, wglu, bglu, cw)
    return pl.pallas_call(
        _mixer_step_kernel,
        grid=(1,),
        in_specs=[_whole(a.shape) for a in rows] + [conv_spec] + [_whole(a.shape) for a in consts],
        out_specs=[
            pl.BlockSpec((nb, D_MODEL), lambda i: (0, 0)),
            pl.BlockSpec((nb, N_STATE), lambda i: (0, 0)),
            pl.BlockSpec((nb, N_STATE), lambda i: (0, 0)),
            conv_spec,
        ],
        out_shape=[
            jax.ShapeDtypeStruct((nb, D_MODEL), jnp.bfloat16),
            jax.ShapeDtypeStruct((nb, N_STATE), jnp.float32),
            jax.ShapeDtypeStruct((nb, N_STATE), jnp.float32),
            jax.ShapeDtypeStruct(conv_state.shape, jnp.float32),
        ],
        compiler_params=pltpu.CompilerParams(
            dimension_semantics=("arbitrary",), vmem_limit_bytes=VMEM_LIMIT),
        name="mixer_step",
    )(*rows, conv_state, *consts)


def _ssm_layout(lam_re, lam_im, log_dt, b_re, b_im, c_re, c_im):
    gpb = N_SSM_GROUPS // N_BLK

    def b_rows(b):
        return b.reshape(N_BLK, gpb, SSM_STATE, SSM_GROUP).transpose(0, 1, 3, 2).reshape(N_BLK, LANES, SSM_STATE)

    def c_rows(c):
        return c.reshape(N_BLK, LANES, SSM_STATE)

    return lam_re, lam_im, log_dt, b_rows(b_re), b_rows(b_im), c_rows(c_re), c_rows(c_im)


def kernel(x_prompt, x_sample, state_ssm_re, state_ssm_im, state_conv, ffn1_norm, ffn1_w_gate, ffn1_w_up, ffn1_w_down, mix_norm, w_in, ssm_lambda_re, ssm_lambda_im, ssm_log_dt, ssm_b_re, ssm_b_im, ssm_c_re, ssm_c_im, ssm_d, ssm_w_glu, ssm_b_glu, conv_w, w_out, ffn2_norm, ffn2_w_gate, ffn2_w_up, ffn2_w_down, final_norm):
    assert ffn1_norm.shape[0] == 1, "single-layer trunk"
    n_prompt, seq, _ = x_prompt.shape
    n_dec, dec_seq, _ = x_sample.shape
    assert dec_seq == 1 and n_prompt == SUBLANES and seq % FFN_TM == 0 and seq % SCAN_T == 0
    assert n_dec % SCAN_T == 0 and n_dec <= n_prompt * SCAN_T and FFN_TM % FFN_SUB == 0

    n1 = ffn1_norm[0][None, :]
    nm = mix_norm[0][None, :]
    n2 = ffn2_norm[0][None, :]
    nf = final_norm[None, :]
    wglu = ssm_w_glu[0]
    layout = _ssm_layout(ssm_lambda_re[0], ssm_lambda_im[0], ssm_log_dt, ssm_b_re[0], ssm_b_im[0],
                         ssm_c_re[0], ssm_c_im[0])
    a_re, a_im, a2_re, a2_im, bb, wb, wct, wdm = _discretize(*layout)
    d = ssm_d[0][None, :]
    bglu = ssm_b_glu[0][None, :]
    cw = conv_w

    h_all, mixp, p_re, p_im, p_conv, ud, vd, gd = _ffn1_mixer(
        x_prompt, x_sample, n1, ffn1_w_gate[0], ffn1_w_up[0], ffn1_w_down[0], nm,
        w_in[0], a2_re, a2_im, wb, wct, wdm, d, wglu, bglu, cw)

    s_re0 = state_ssm_re[0].reshape(n_dec, N_STATE)
    s_im0 = state_ssm_im[0].reshape(n_dec, N_STATE)
    mixd, s_re, s_im, s_conv = _mixer_step(ud, vd, gd, s_re0, s_im0, state_conv, a_re, a_im, bb, wct,
                                           d, wglu, bglu, cw)

    yp, yd = _ffn2(h_all, mixp, mixd, w_out[0], n2, ffn2_w_gate[0], ffn2_w_up[0], ffn2_w_down[0], nf)

    gs = (N_SSM_GROUPS, SSM_STATE)
    return (yp,
            yd,
            p_re[None],
            p_im[None],
            p_conv[None],
            s_re.reshape(1, n_dec, *gs),
            s_im.reshape(1, n_dec, *gs),
            s_conv)
```

```python
import jax
import jax.numpy as jnp
from jax import lax
from jax.experimental import pallas as pl
from jax.experimental.pallas import tpu as pltpu

D_MODEL = 1024
D_FF = 2816
SSM_WIDTH = 512
CONV_WIDTH = 512
SSM_GROUP = 16
N_SSM_GROUPS = 32
SSM_STATE = 64
CONV_K = 3
IN_PROJ_WIDTH = SSM_WIDTH + 3 * CONV_WIDTH
EPS = 1e-6

LANES = 128
SUBLANES = 8
N_STATE = N_SSM_GROUPS * SSM_STATE
N_BLK = SSM_WIDTH // LANES
BLK_STATE = N_STATE // N_BLK
SLABS_PER_BLK = 2 * BLK_STATE // LANES
HALF = SLABS_PER_BLK // 2
N_TILE = N_BLK * HALF

FFN_TM = 1024
FFN_SUB = 512
W1_STEPS = 16
W2_STEPS = 8
FFN_TF = 256
SCAN_T = 64
N_PAIR = SCAN_T // 2
PAIR_PITCH = N_PAIR + 4
VMEM_LIMIT = 62 * 1024 * 1024

_NT = (((1,), (1,)), ((), ()))


def _rms(x, g):
    r = lax.rsqrt(jnp.mean(x * x, axis=-1, keepdims=True) + EPS)
    return x * r * g


def _bdot(a, b):
    return jnp.dot(a.astype(jnp.bfloat16), b.astype(jnp.bfloat16), preferred_element_type=jnp.float32)


def _swiglu_stage(xn, wg_ref, wu_ref, a_ref, lo, hi):
    rows = xn.shape[0]
    for j in range(lo, hi, FFN_TF):
        g = jnp.dot(xn, wg_ref[:, j:j + FFN_TF], preferred_element_type=jnp.float32)
        u = jnp.dot(xn, wu_ref[:, j:j + FFN_TF], preferred_element_type=jnp.float32)
        a_ref[0:rows, j:j + FFN_TF] = ((g * jax.nn.sigmoid(g)) * u).astype(a_ref.dtype)


def _swiglu_staged(xn, wg_ref, wu_ref, wd_ref, a_ref):
    rows = xn.shape[0]
    _swiglu_stage(xn, wg_ref, wu_ref, a_ref, 0, D_FF)
    return jnp.dot(a_ref[0:rows, :], wd_ref[...], preferred_element_type=jnp.float32)


def _cast_weight_chunks(step, pairs):
    for src, dst in pairs:
        n = src.shape[0]
        dst[pl.ds(pl.multiple_of(step * n, n), n), :] = src[...].astype(dst.dtype)


def _whole(shape):
    return pl.BlockSpec(shape, lambda i: (0,) * len(shape), pipeline_mode=pl.Buffered(1))


def _weight_chunk(shape, n_steps):
    return pl.BlockSpec(shape, lambda i: (jnp.minimum(i, n_steps - 1), 0))


def _glu_tail(y, u, d_ref, wglu_ref, bglu_ref):
    y = y + d_ref[...] * u
    z = jax.nn.gelu(y)
    gate = jax.nn.sigmoid(_bdot(z, wglu_ref[...]) + bglu_ref[...])
    return z * gate


def _discretize_kernel(lr_ref, li_ref, ldt_ref, bre_ref, bim_ref, ctre_ref, ctim_ref,
                       are_ref, aim_ref, a2re_ref, a2im_ref, wb_ref, wct_ref, wdm_ref):
    row_group = lax.broadcasted_iota(jnp.int32, (LANES, BLK_STATE), 0) // SSM_GROUP
    col_group = lax.broadcasted_iota(jnp.int32, (LANES, BLK_STATE), 1) // SSM_STATE
    on_diagonal = row_group == col_group

    def blocks(m):
        pair = jnp.concatenate([m, m], axis=1)
        return jnp.where(on_diagonal, jnp.concatenate([pair] * (BLK_STATE // LANES), axis=1), 0.0)

    gpb = N_SSM_GROUPS // N_BLK

    def state_rows(ref):
        return jnp.concatenate(
            [jnp.concatenate([ref[k * gpb + g:k * gpb + g + 1, :] for g in range(gpb)], axis=1)
             for k in range(N_BLK)], axis=0)

    def per_state(row_ref):
        vals = row_ref[...]
        col_group = lax.broadcasted_iota(jnp.int32, (1, BLK_STATE), 1) // SSM_STATE
        rows = []
        for k in range(N_BLK):
            row = jnp.zeros((1, BLK_STATE), jnp.float32)
            for g in range(gpb):
                row = jnp.where(col_group == g, vals[:, k * gpb + g:k * gpb + g + 1], row)
            rows.append(row)
        return jnp.concatenate(rows, axis=0)

    lr, li = state_rows(lr_ref), state_rows(li_ref)
    dt = jnp.exp(per_state(ldt_ref))
    mag = jnp.exp(lr * dt)
    ab_re = mag * jnp.cos(li * dt)
    ab_im = mag * jnp.sin(li * dt)
    den = lr * lr + li * li
    nr = ab_re - 1.0
    ni = ab_im
    coef_re = (nr * lr + ni * li) / den
    coef_im = (ni * lr - nr * li) / den
    are_ref[...] = ab_re
    aim_ref[...] = ab_im
    a2_re = ab_re * ab_re - ab_im * ab_im
    a2_im = 2.0 * (ab_re * ab_im)
    bf = jnp.bfloat16
    for k in range(N_BLK):
        ar, ai = ab_re[k:k + 1, :], ab_im[k:k + 1, :]
        for j in range(HALF):
            lanes = slice(j * LANES, (j + 1) * LANES)
            a2re_ref[k * HALF + j] = jnp.broadcast_to(a2_re[k:k + 1, lanes], (SUBLANES, LANES))
            a2im_ref[k * HALF + j] = jnp.broadcast_to(a2_im[k:k + 1, lanes], (SUBLANES, LANES))
        cr, ci = coef_re[k:k + 1, :], coef_im[k:k + 1, :]
        bre, bim = blocks(bre_ref[k]), blocks(bim_ref[k])
        bb_re = cr * bre - ci * bim
        bb_im = cr * bim + ci * bre
        wb_ref[k, 0:LANES, 0:BLK_STATE] = (ar * bb_re - ai * bb_im).astype(bf)
        wb_ref[k, 0:LANES, BLK_STATE:] = (ar * bb_im + ai * bb_re).astype(bf)
        wb_ref[k, LANES:, 0:BLK_STATE] = bb_re.astype(bf)
        wb_ref[k, LANES:, BLK_STATE:] = bb_im.astype(bf)
        ctre, ctim = blocks(ctre_ref[k]), blocks(ctim_ref[k])
        wct_ref[k, 0:LANES, 0:BLK_STATE] = ctre.astype(bf)
        wct_ref[k, 0:LANES, BLK_STATE:] = (-ctim).astype(bf)
        wct_ref[k, LANES:, 0:BLK_STATE] = (ctre * ar - ctim * ai).astype(bf)
        wct_ref[k, LANES:, BLK_STATE:] = (-(ctre * ai + ctim * ar)).astype(bf)
        wdm = (lax.dot_general(bb_re, ctre, _NT, precision=lax.Precision.HIGHEST,
                               preferred_element_type=jnp.float32)
               - lax.dot_general(bb_im, ctim, _NT, precision=lax.Precision.HIGHEST,
                                 preferred_element_type=jnp.float32))
        wdm_ref[k] = wdm.astype(bf)


def _discretize(lr, li, ldt, bre, bim, ctre, ctim):
    f32, bf = jnp.float32, jnp.bfloat16
    return pl.pallas_call(
        _discretize_kernel,
        out_shape=[
            jax.ShapeDtypeStruct((N_BLK, BLK_STATE), f32),
            jax.ShapeDtypeStruct((N_BLK, BLK_STATE), f32),
            jax.ShapeDtypeStruct((N_TILE, SUBLANES, LANES), f32),
            jax.ShapeDtypeStruct((N_TILE, SUBLANES, LANES), f32),
            jax.ShapeDtypeStruct((N_BLK, 2 * LANES, 2 * BLK_STATE), bf),
            jax.ShapeDtypeStruct((N_BLK, 2 * LANES, 2 * BLK_STATE), bf),
            jax.ShapeDtypeStruct((N_BLK, LANES, LANES), bf),
        ],
        name="ssm_discretize",
    )(lr, li, ldt, bre, bim, ctre, ctim)


def _ffn1_mixer_kernel(x_ref, xd_ref, n1_ref, wg_c, wu_c, wd_c, nm_ref, win_c,
                       a2re_ref, a2im_ref, wb_ref, wct_ref, wdm_ref, d_ref, wglu_ref, bglu_ref, cw_ref,
                       h_ref, mix_ref, sre_ref, sim_ref, nconv_ref, ud_ref, vd_ref, gd_ref,
                       xs0_ref, xs1_ref, xs2_ref, xs3_ref, us_ref, ys_ref, l_ref, yo_ref, ye_ref, e_ref,
                       ecarry_ref, v_ref, g_ref, state_ref, wg_ref, wu_ref, wd_ref, win_ref, a_ref, xdd_ref):
    xs_refs = (xs0_ref, xs1_ref, xs2_ref, xs3_ref)
    step = pl.program_id(0)
    last = pl.num_programs(0) - 1
    nb, t_len, _ = x_ref.shape
    n_dec = xd_ref.shape[0]
    prow = nb * PAIR_PITCH
    tiles = [(k, j) for k in range(N_BLK) for j in range(HALF)]

    @pl.when(step < W1_STEPS)
    def _load_weights():
        _cast_weight_chunks(step, ((wg_c, wg_ref), (wu_c, wu_ref), (wd_c, wd_ref), (win_c, win_ref)))

    @pl.when(step == 0)
    def _init():
        for ref in (state_ref, us_ref, l_ref, e_ref, ecarry_ref, v_ref, g_ref):
            ref[...] = jnp.zeros_like(ref)
        xdd_ref[...] = xd_ref[...]

    @pl.when(step >= W1_STEPS)
    def _main():
        carry = {}

        def build_lhs():
            for b in range(nb):
                for k in range(N_BLK):
                    for par in range(2):
                        col = (2 * k + par) * LANES
                        l_ref[pl.ds(b * PAIR_PITCH, N_PAIR), col:col + LANES] = (
                            us_ref[k, pl.ds(b * t_len + par, N_PAIR, stride=2), :])

        def b_proj(ks):
            for k in ks:
                lhs = l_ref[:, 2 * k * LANES:2 * (k + 1) * LANES].astype(jnp.bfloat16)
                v = jnp.dot(lhs, wb_ref[k], preferred_element_type=jnp.float32)
                for j in range(SLABS_PER_BLK):
                    xs_refs[k][j] = v[:, j * LANES:(j + 1) * LANES]

        def scan(r0, r1):
            if r0 == 0:
                carry["re"] = [state_ref[k * SLABS_PER_BLK + j] for k, j in tiles]
                carry["im"] = [state_ref[k * SLABS_PER_BLK + HALF + j] for k, j in tiles]
            xr, xi = carry["re"], carry["im"]
            for r in range(r0, r1):
                rows = pl.ds(r, SUBLANES, stride=PAIR_PITCH)
                for n, (k, j) in enumerate(tiles):
                    a_re = a2re_ref[k * HALF + j]
                    a_im = a2im_ref[k * HALF + j]
                    nr = (a_re * xr[n] + xs_refs[k][j, rows, :]) - a_im * xi[n]
                    ni = (a_re * xi[n] + xs_refs[k][HALF + j, rows, :]) + a_im * xr[n]
                    xs_refs[k][j, rows, :] = nr
                    xs_refs[k][HALF + j, rows, :] = ni
                    xr[n], xi[n] = nr, ni
            if r1 == N_PAIR:
                for n, (k, j) in enumerate(tiles):
                    state_ref[k * SLABS_PER_BLK + j] = xr[n]
                    state_ref[k * SLABS_PER_BLK + HALF + j] = xi[n]

        def c_proj(ks):
            for k in ks:
                lanes = slice(k * LANES, (k + 1) * LANES)
                x = jnp.concatenate([xs_refs[k][j] for j in range(SLABS_PER_BLK)], axis=1)
                ce = lax.dot_general(x.astype(jnp.bfloat16), wct_ref[k], _NT,
                                     preferred_element_type=jnp.float32)
                yo_ref[:, lanes] = ce[:, 0:LANES]
                e_ref[k, SUBLANES:SUBLANES + prow, :] = ce[:, LANES:]
                nxt = e_ref[k, pl.ds(SUBLANES + N_PAIR - 1, nb, stride=PAIR_PITCH), :]
                e_ref[k, pl.ds(SUBLANES - 1, nb, stride=PAIR_PITCH), :] = ecarry_ref[k]
                ecarry_ref[k] = nxt
                u_even = l_ref[:, 2 * k * LANES:(2 * k + 1) * LANES].astype(jnp.bfloat16)
                ye_ref[:, lanes] = (e_ref[k, SUBLANES - 1:SUBLANES - 1 + prow, :]
                                    + jnp.dot(u_even, wdm_ref[k], preferred_element_type=jnp.float32))

        def glu_emit():
            for b in range(nb):
                for k in range(N_BLK):
                    lanes = slice(k * LANES, (k + 1) * LANES)
                    for par, src in ((0, ye_ref), (1, yo_ref)):
                        ys_ref[k, pl.ds(b * t_len + par, N_PAIR, stride=2), :] = (
                            src[pl.ds(b * PAIR_PITCH, N_PAIR), lanes])
            y = jnp.concatenate([ys_ref[k] for k in range(N_BLK)], axis=1)
            u = jnp.concatenate([us_ref[k] for k in range(N_BLK)], axis=1)
            s_out = _glu_tail(y, u, d_ref, wglu_ref, bglu_ref)
            mix_ref[:, :, 0:SSM_WIDTH] = s_out.reshape(nb, t_len, SSM_WIDTH).astype(mix_ref.dtype)

        def conv():
            v = v_ref[:, SUBLANES:SUBLANES + t_len, :]
            z = cw_ref[0, 0:1, :] * v_ref[:, SUBLANES - 2:SUBLANES - 2 + t_len, :]
            z = z + cw_ref[0, 1:2, :] * v_ref[:, SUBLANES - 1:SUBLANES - 1 + t_len, :]
            z = z + cw_ref[0, 2:3, :] * v
            mix_ref[:, :, SSM_WIDTH:] = (g_ref[...] * z).astype(mix_ref.dtype)
            v_ref[:, 0:SUBLANES, :] = v_ref[:, t_len:t_len + SUBLANES, :]

        q = N_PAIR // 4
        pieces = [
            lambda: (conv(), build_lhs(), b_proj((0,))), lambda: b_proj((1,)), lambda: b_proj((2,)),
            lambda: b_proj((3,)),
            lambda: scan(0, q), lambda: scan(q, 2 * q), lambda: scan(2 * q, 3 * q), lambda: scan(3 * q, N_PAIR),
            lambda: c_proj((0, 1)), lambda: c_proj((2, 3)),
            glu_emit,
        ]
        assert len(pieces) <= D_FF // FFN_TF

        db = n_dec // t_len
        x_top = jnp.where(step == last, xdd_ref[...], x_ref[0:db].reshape(n_dec, D_MODEL))
        x = jnp.concatenate([x_top, x_ref[db:].reshape((nb - db) * t_len, D_MODEL)], axis=0)
        xn = _rms(x, n1_ref[...]).astype(jnp.bfloat16)
        for c, j in enumerate(range(0, D_FF, FFN_TF)):
            _swiglu_stage(xn, wg_ref, wu_ref, a_ref, j, j + FFN_TF)
            if c < len(pieces):
                pieces[c]()
        h = x + 0.5 * jnp.dot(a_ref[...], wd_ref[...], preferred_element_type=jnp.float32)
        hn = _rms(h, nm_ref[...]).astype(jnp.bfloat16)
        p = jnp.dot(hn, win_ref[...], preferred_element_type=jnp.float32)
        h_ref[...] = h.reshape(nb, t_len, D_MODEL)
        v_new = p[:, SSM_WIDTH + 2 * CONV_WIDTH:] * p[:, SSM_WIDTH:SSM_WIDTH + CONV_WIDTH]
        v_ref[:, SUBLANES:SUBLANES + t_len, :] = v_new.reshape(nb, t_len, CONV_WIDTH)
        g_ref[...] = p[:, SSM_WIDTH + CONV_WIDTH:SSM_WIDTH + 2 * CONV_WIDTH].reshape(nb, t_len, CONV_WIDTH)
        for k in range(N_BLK):
            us_ref[k] = p[:, k * LANES:(k + 1) * LANES]

    @pl.when(step == last)
    def _final():
        for n, (k, j) in enumerate(tiles):
            for out_ref, slab in ((sre_ref, k * SLABS_PER_BLK + j), (sim_ref, k * SLABS_PER_BLK + HALF + j)):
                tile = state_ref[slab]
                for half in range(LANES // SSM_STATE):
                    group = (k * BLK_STATE + j * LANES) // SSM_STATE + half
                    out_ref[:, group, :] = tile[:, half * SSM_STATE:(half + 1) * SSM_STATE]
        nconv_ref[...] = v_ref[:, SUBLANES - 2:SUBLANES, :]
        for k in range(N_BLK):
            ud_ref[:, k * LANES:(k + 1) * LANES] = us_ref[k, 0:n_dec, :]
        for b in range(n_dec // t_len):
            vd_ref[b * t_len:(b + 1) * t_len, :] = v_ref[b, SUBLANES:SUBLANES + t_len, :]
            gd_ref[b * t_len:(b + 1) * t_len, :] = g_ref[b]


def _ffn1_mixer(x, xd, n1, wg, wu, wd, nm, win, a2_re, a2_im, wb, wct, wdm, d, wglu, bglu, cw):
    nb, seq, _ = x.shape
    n_dec = xd.shape[0]
    n_chunk = seq // SCAN_T
    prow = nb * PAIR_PITCH
    f32, bf = jnp.float32, jnp.bfloat16
    main = lambda i: jnp.maximum(i - W1_STEPS, 0)
    return pl.pallas_call(
        _ffn1_mixer_kernel,
        grid=(W1_STEPS + n_chunk + 1,),
        in_specs=[
            pl.BlockSpec((nb, SCAN_T, D_MODEL), lambda i: (0, jnp.minimum(main(i), n_chunk - 1), 0)),
            pl.BlockSpec((n_dec, None, D_MODEL), lambda i: (0, 0, 0), pipeline_mode=pl.Buffered(1)),
            _whole((1, D_MODEL)),
            _weight_chunk((D_MODEL // W1_STEPS, D_FF), W1_STEPS),
            _weight_chunk((D_MODEL // W1_STEPS, D_FF), W1_STEPS),
            _weight_chunk((D_FF // W1_STEPS, D_MODEL), W1_STEPS),
            _whole((1, D_MODEL)),
            _weight_chunk((D_MODEL // W1_STEPS, IN_PROJ_WIDTH), W1_STEPS),
            _whole((N_TILE, SUBLANES, LANES)),
            _whole((N_TILE, SUBLANES, LANES)),
            _whole((N_BLK, 2 * LANES, 2 * BLK_STATE)),
            _whole((N_BLK, 2 * LANES, 2 * BLK_STATE)),
            _whole((N_BLK, LANES, LANES)),
            _whole((1, SSM_WIDTH)),
            _whole((SSM_WIDTH, SSM_WIDTH)),
            _whole((1, SSM_WIDTH)),
            _whole((1, CONV_K, CONV_WIDTH)),
        ],
        out_specs=[
            pl.BlockSpec((nb, SCAN_T, D_MODEL), lambda i: (0, main(i), 0)),
            pl.BlockSpec((nb, SCAN_T, D_MODEL), lambda i: (0, jnp.clip(main(i) - 1, 0, n_chunk - 1), 0)),
            pl.BlockSpec((nb, N_SSM_GROUPS, SSM_STATE), lambda i: (0, 0, 0)),
            pl.BlockSpec((nb, N_SSM_GROUPS, SSM_STATE), lambda i: (0, 0, 0)),
            pl.BlockSpec((nb, CONV_K - 1, CONV_WIDTH), lambda i: (0, 0, 0)),
            pl.BlockSpec((n_dec, SSM_WIDTH), lambda i: (0, 0)),
            pl.BlockSpec((n_dec, CONV_WIDTH), lambda i: (0, 0)),
            pl.BlockSpec((n_dec, CONV_WIDTH), lambda i: (0, 0)),
        ],
        out_shape=[
            jax.ShapeDtypeStruct((nb, seq + SCAN_T, D_MODEL), f32),
            jax.ShapeDtypeStruct((nb, seq, D_MODEL), bf),
            jax.ShapeDtypeStruct((nb, N_SSM_GROUPS, SSM_STATE), f32),
            jax.ShapeDtypeStruct((nb, N_SSM_GROUPS, SSM_STATE), f32),
            jax.ShapeDtypeStruct((nb, CONV_K - 1, CONV_WIDTH), f32),
            jax.ShapeDtypeStruct((n_dec, SSM_WIDTH), f32),
            jax.ShapeDtypeStruct((n_dec, CONV_WIDTH), f32),
            jax.ShapeDtypeStruct((n_dec, CONV_WIDTH), f32),
        ],
        scratch_shapes=[
            *[pltpu.VMEM((SLABS_PER_BLK, prow, LANES), f32) for _ in range(N_BLK)],
            pltpu.VMEM((N_BLK, nb * SCAN_T, LANES), f32),
            pltpu.VMEM((N_BLK, nb * SCAN_T, LANES), f32),
            pltpu.VMEM((prow, 2 * SSM_WIDTH), f32),
            pltpu.VMEM((prow, SSM_WIDTH), f32),
            pltpu.VMEM((prow, SSM_WIDTH), f32),
            pltpu.VMEM((N_BLK, SUBLANES + prow, LANES), f32),
            pltpu.VMEM((N_BLK, SUBLANES, LANES), f32),
            pltpu.VMEM((nb, SCAN_T + SUBLANES, CONV_WIDTH), f32),
            pltpu.VMEM((nb, SCAN_T, CONV_WIDTH), f32),
            pltpu.VMEM((2 * N_TILE, SUBLANES, LANES), f32),
            pltpu.VMEM((D_MODEL, D_FF), bf),
            pltpu.VMEM((D_MODEL, D_FF), bf),
            pltpu.VMEM((D_FF, D_MODEL), bf),
            pltpu.VMEM((D_MODEL, IN_PROJ_WIDTH), bf),
            pltpu.VMEM((nb * SCAN_T, D_FF), bf),
            pltpu.VMEM((n_dec, D_MODEL), f32),
        ],
        compiler_params=pltpu.CompilerParams(
            dimension_semantics=("arbitrary",), vmem_limit_bytes=VMEM_LIMIT),
        name="ffn1_mixer",
    )(x, xd, n1, wg, wu, wd, nm, win, a2_re, a2_im, wb, wct, wdm, d, wglu, bglu, cw)


def _ffn2_rows(h, mix, weights, a_ref):
    wout_ref, n2_ref, wg_ref, wu_ref, wd_ref, nf_ref = weights
    h = h + jnp.dot(mix, wout_ref[...], preferred_element_type=jnp.float32)
    hn = _rms(h, n2_ref[...]).astype(jnp.bfloat16)
    h = h + 0.5 * _swiglu_staged(hn, wg_ref, wu_ref, wd_ref, a_ref)
    return _rms(h, nf_ref[...])


def _ffn2_kernel(h_ref, mix_ref, hd0_ref, hd1_ref, mixd_ref, wout_c, wg_c, wu_c, wd_c, n2_ref, nf_ref,
                 y_ref, yd_ref, wout_ref, wg_ref, wu_ref, wd_ref, a0_ref, a1_ref):
    step = pl.program_id(0)
    last = pl.num_programs(0) - 1
    weights = (wout_ref, n2_ref, wg_ref, wu_ref, wd_ref, nf_ref)

    @pl.when(step < W2_STEPS)
    def _load_weights():
        _cast_weight_chunks(step, ((wout_c, wout_ref), (wg_c, wg_ref), (wu_c, wu_ref), (wd_c, wd_ref)))

    @pl.when((step >= W2_STEPS) & (step < last))
    def _prompt_tile():
        assert h_ref.shape[0] == 2 * FFN_SUB
        ra, rb = slice(0, FFN_SUB), slice(FFN_SUB, 2 * FFN_SUB)
        h_a = h_ref[ra, :] + jnp.dot(mix_ref[ra, :], wout_ref[...], preferred_element_type=jnp.float32)
        h_b = h_ref[rb, :] + jnp.dot(mix_ref[rb, :], wout_ref[...], preferred_element_type=jnp.float32)
        hn_a = _rms(h_a, n2_ref[...]).astype(jnp.bfloat16)
        _swiglu_stage(hn_a, wg_ref, wu_ref, a0_ref, 0, FFN_TF)
        hn_b = _rms(h_b, n2_ref[...]).astype(jnp.bfloat16)
        _swiglu_stage(hn_a, wg_ref, wu_ref, a0_ref, FFN_TF, D_FF)
        f_a = jnp.dot(a0_ref[...], wd_ref[...], preferred_element_type=jnp.float32)
        _swiglu_stage(hn_b, wg_ref, wu_ref, a1_ref, 0, FFN_TF)
        y_ref[ra, :] = _rms(h_a + 0.5 * f_a, nf_ref[...])
        _swiglu_stage(hn_b, wg_ref, wu_ref, a1_ref, FFN_TF, D_FF)
        f_b = jnp.dot(a1_ref[...], wd_ref[...], preferred_element_type=jnp.float32)
        y_ref[rb, :] = _rms(h_b + 0.5 * f_b, nf_ref[...])

    @pl.when(step == last)
    def _decode_rows():
        h_d = jnp.concatenate([hd0_ref[...], hd1_ref[...]], axis=0)
        yd_ref[...] = _ffn2_rows(h_d, mixd_ref[...], weights, a0_ref)


def _ffn2(h_all, mix, mixd, wout, n2, wg, wu, wd, nf):
    nb, seq, _ = mix.shape
    seq_all = h_all.shape[1]
    n_dec = mixd.shape[0]
    tail = seq_all - seq
    assert n_dec == 2 * tail
    per_b = seq // FFN_TM
    n_tiles = nb * per_b
    h_all = h_all.reshape(nb * seq_all, D_MODEL)
    mix = mix.reshape(nb * seq, D_MODEL)
    idx = lambda i: jnp.clip(i - W2_STEPS, 0, n_tiles - 1)
    tile = lambda i: (idx(i), 0)
    assert seq_all % SUBLANES == 0
    tile_all = lambda i: (pl.multiple_of((idx(i) // per_b) * seq_all + (idx(i) % per_b) * FFN_TM, SUBLANES), 0)
    yp, yd = pl.pallas_call(
        _ffn2_kernel,
        grid=(W2_STEPS + n_tiles + 1,),
        in_specs=[
            pl.BlockSpec((pl.Element(FFN_TM), pl.Element(D_MODEL)), tile_all),
            pl.BlockSpec((FFN_TM, D_MODEL), tile),
            pl.BlockSpec((pl.Element(tail), pl.Element(D_MODEL)), lambda i: (seq, 0),
                         pipeline_mode=pl.Buffered(1)),
            pl.BlockSpec((pl.Element(tail), pl.Element(D_MODEL)), lambda i: (seq_all + seq, 0),
                         pipeline_mode=pl.Buffered(1)),
            _whole(mixd.shape),
            _weight_chunk((D_MODEL // W2_STEPS, D_MODEL), W2_STEPS),
            _weight_chunk((D_MODEL // W2_STEPS, D_FF), W2_STEPS),
            _weight_chunk((D_MODEL // W2_STEPS, D_FF), W2_STEPS),
            _weight_chunk((D_FF // W2_STEPS, D_MODEL), W2_STEPS),
            _whole((1, D_MODEL)),
            _whole((1, D_MODEL)),
        ],
        out_specs=[
            pl.BlockSpec((FFN_TM, D_MODEL), tile),
            pl.BlockSpec((n_dec, None, D_MODEL), lambda i: (0, 0, 0)),
        ],
        out_shape=[
            jax.ShapeDtypeStruct((nb * seq, D_MODEL), jnp.float32),
            jax.ShapeDtypeStruct((n_dec, 1, D_MODEL), jnp.float32),
        ],
        scratch_shapes=[
            pltpu.VMEM((D_MODEL, D_MODEL), jnp.bfloat16),
            pltpu.VMEM((D_MODEL, D_FF), jnp.bfloat16),
            pltpu.VMEM((D_MODEL, D_FF), jnp.bfloat16),
            pltpu.VMEM((D_FF, D_MODEL), jnp.bfloat16),
            pltpu.VMEM((FFN_SUB, D_FF), jnp.bfloat16),
            pltpu.VMEM((FFN_SUB, D_FF), jnp.bfloat16),
        ],
        compiler_params=pltpu.CompilerParams(
            dimension_semantics=("arbitrary",), vmem_limit_bytes=VMEM_LIMIT),
        name="outproj_ffn2",
    )(h_all, mix, h_all, h_all, mixd, wout, wg, wu, wd, n2, nf)
    return yp.reshape(nb, seq, D_MODEL), yd


def _mixer_step_kernel(u_ref, v_ref, g_ref, hre_ref, him_ref, buf_ref, are_ref, aim_ref, bb_ref, wct_ref,
                       d_ref, wglu_ref, bglu_ref, cw_ref, mix_ref, sre_ref, sim_ref, nconv_ref):
    u = u_ref[...]
    u_bf = u.astype(jnp.bfloat16)
    ys = []
    for k in range(N_BLK):
        bu = jnp.dot(u_bf[:, k * LANES:(k + 1) * LANES], bb_ref[k], preferred_element_type=jnp.float32)
        st = slice(k * BLK_STATE, (k + 1) * BLK_STATE)
        a_re = are_ref[k:k + 1, :]
        a_im = aim_ref[k:k + 1, :]
        h_re = hre_ref[:, st]
        h_im = him_ref[:, st]
        x_re = (a_re * h_re + bu[:, :BLK_STATE]) - a_im * h_im
        x_im = (a_re * h_im + bu[:, BLK_STATE:]) + a_im * h_re
        sre_ref[:, st] = x_re
        sim_ref[:, st] = x_im
        x = jnp.concatenate([x_re, x_im], axis=1).astype(jnp.bfloat16)
        ys.append(lax.dot_general(x, wct_ref[k], _NT, preferred_element_type=jnp.float32))
    y = jnp.concatenate(ys, axis=1)
    mix_ref[:, 0:SSM_WIDTH] = _glu_tail(y, u, d_ref, wglu_ref, bglu_ref).astype(mix_ref.dtype)

    v = v_ref[...]
    buf1 = buf_ref[:, 1, :]
    z = cw_ref[0, 0:1, :] * buf_ref[:, 0, :] + cw_ref[0, 1:2, :] * buf1 + cw_ref[0, 2:3, :] * v
    mix_ref[:, SSM_WIDTH:] = (g_ref[...] * z).astype(mix_ref.dtype)
    nconv_ref[:, 0, :] = buf1
    nconv_ref[:, 1, :] = v


def _mixer_step(u, v, g, h_re, h_im, conv_state, a_re, a_im, wb, wct, d, wglu, bglu, cw):
    nb = u.shape[0]
    conv_spec = pl.BlockSpec((None, nb, CONV_K - 1, CONV_WIDTH), lambda i: (0, 0, 0, 0))
    half = (N_BLK, LANES, 2 * BLK_STATE)
    bb_spec = pl.BlockSpec(half, lambda i: (0, 1, 0), pipeline_mode=pl.Buffered(1))
    c_spec = pl.BlockSpec(half, lambda i: (0, 0, 0), pipeline_mode=pl.Buffered(1))
    rows = (u, v, g, h_re, h_im)
    tail = (d, wglu, bglu, cw)
    return pl.pallas_call(
        _mixer_step_kernel,
        grid=(1,),
        in_specs=([_whole(a.shape) for a in rows] + [conv_spec] + [_whole(a_re.shape), _whole(a_im.shape)]
                  + [bb_spec, c_spec] + [_whole(a.shape) for a in tail]),
        out_specs=[
            pl.BlockSpec((nb, D_MODEL), lambda i: (0, 0)),
            pl.BlockSpec((nb, N_STATE), lambda i: (0, 0)),
            pl.BlockSpec((nb, N_STATE), lambda i: (0, 0)),
            conv_spec,
        ],
        out_shape=[
            jax.ShapeDtypeStruct((nb, D_MODEL), jnp.bfloat16),
            jax.ShapeDtypeStruct((nb, N_STATE), jnp.float32),
            jax.ShapeDtypeStruct((nb, N_STATE), jnp.float32),
            jax.ShapeDtypeStruct(conv_state.shape, jnp.float32),
        ],
        compiler_params=pltpu.CompilerParams(
            dimension_semantics=("arbitrary",), vmem_limit_bytes=VMEM_LIMIT),
        name="mixer_step",
    )(*rows, conv_state, a_re, a_im, wb, wct, *tail)


def _ssm_layout(lam_re, lam_im, log_dt, b_re, b_im, c_re, c_im):
    gpb = N_SSM_GROUPS // N_BLK

    def b_rows(b):
        return b.reshape(N_BLK, gpb, SSM_STATE, SSM_GROUP).transpose(0, 1, 3, 2).reshape(N_BLK, LANES, SSM_STATE)

    def c_rows(c):
        return c.reshape(N_BLK, LANES, SSM_STATE)

    return lam_re, lam_im, log_dt, b_rows(b_re), b_rows(b_im), c_rows(c_re), c_rows(c_im)


def kernel(x_prompt, x_sample, state_ssm_re, state_ssm_im, state_conv, ffn1_norm, ffn1_w_gate, ffn1_w_up, ffn1_w_down, mix_norm, w_in, ssm_lambda_re, ssm_lambda_im, ssm_log_dt, ssm_b_re, ssm_b_im, ssm_c_re, ssm_c_im, ssm_d, ssm_w_glu, ssm_b_glu, conv_w, w_out, ffn2_norm, ffn2_w_gate, ffn2_w_up, ffn2_w_down, final_norm):
    assert ffn1_norm.shape[0] == 1, "single-layer trunk"
    n_prompt, seq, _ = x_prompt.shape
    n_dec, dec_seq, _ = x_sample.shape
    assert dec_seq == 1 and n_prompt == SUBLANES and seq % FFN_TM == 0 and seq % SCAN_T == 0
    assert n_dec % SCAN_T == 0 and n_dec <= n_prompt * SCAN_T and FFN_TM % FFN_SUB == 0

    n1 = ffn1_norm[0][None, :]
    nm = mix_norm[0][None, :]
    n2 = ffn2_norm[0][None, :]
    nf = final_norm[None, :]
    wglu = ssm_w_glu[0]
    layout = _ssm_layout(ssm_lambda_re[0], ssm_lambda_im[0], ssm_log_dt, ssm_b_re[0], ssm_b_im[0],
                         ssm_c_re[0], ssm_c_im[0])
    a_re, a_im, a2_re, a2_im, wb, wct, wdm = _discretize(*layout)
    d = ssm_d[0][None, :]
    bglu = ssm_b_glu[0][None, :]
    cw = conv_w

    h_all, mixp, p_re, p_im, p_conv, ud, vd, gd = _ffn1_mixer(
        x_prompt, x_sample, n1, ffn1_w_gate[0], ffn1_w_up[0], ffn1_w_down[0], nm,
        w_in[0], a2_re, a2_im, wb, wct, wdm, d, wglu, bglu, cw)

    s_re0 = state_ssm_re[0].reshape(n_dec, N_STATE)
    s_im0 = state_ssm_im[0].reshape(n_dec, N_STATE)
    mixd, s_re, s_im, s_conv = _mixer_step(ud, vd, gd, s_re0, s_im0, state_conv, a_re, a_im, wb, wct,
                                           d, wglu, bglu, cw)

    yp, yd = _ffn2(h_all, mixp, mixd, w_out[0], n2, ffn2_w_gate[0], ffn2_w_up[0], ffn2_w_down[0], nf)

    gs = (N_SSM_GROUPS, SSM_STATE)
    return (yp,
            yd,
            p_re[None],
            p_im[None],
            p_conv[None],
            s_re.reshape(1, n_dec, *gs),
            s_im.reshape(1, n_dec, *gs),
            s_conv)
```

```python
import jax
import jax.numpy as jnp
from jax import lax
from jax.experimental import pallas as pl
from jax.experimental.pallas import tpu as pltpu

D_MODEL = 1024
D_FF = 2816
SSM_WIDTH = 512
CONV_WIDTH = 512
SSM_GROUP = 16
N_SSM_GROUPS = 32
SSM_STATE = 64
CONV_K = 3
IN_PROJ_WIDTH = SSM_WIDTH + 3 * CONV_WIDTH
EPS = 1e-6

LANES = 128
SUBLANES = 8
N_STATE = N_SSM_GROUPS * SSM_STATE
N_BLK = SSM_WIDTH // LANES
BLK_STATE = N_STATE // N_BLK
SLABS_PER_BLK = 2 * BLK_STATE // LANES
HALF = SLABS_PER_BLK // 2
N_TILE = N_BLK * HALF

FFN_TM = 1024
FFN_SUB = 512
W1_STEPS = 16
W2_STEPS = 8
FFN_TF = 256
SCAN_T = 64
N_PAIR = SCAN_T // 2
PAIR_PITCH = N_PAIR + 4
VMEM_LIMIT = 62 * 1024 * 1024

_NT = (((1,), (1,)), ((), ()))


def _rms(x, g):
    r = lax.rsqrt(jnp.mean(x * x, axis=-1, keepdims=True) + EPS)
    return x * r * g


def _bdot(a, b):
    return jnp.dot(a.astype(jnp.bfloat16), b.astype(jnp.bfloat16), preferred_element_type=jnp.float32)


def _swiglu_stage(xn, wg_ref, wu_ref, a_ref, lo, hi):
    rows = xn.shape[0]
    for j in range(lo, hi, FFN_TF):
        g = jnp.dot(xn, wg_ref[:, j:j + FFN_TF], preferred_element_type=jnp.float32)
        u = jnp.dot(xn, wu_ref[:, j:j + FFN_TF], preferred_element_type=jnp.float32)
        a_ref[0:rows, j:j + FFN_TF] = ((g * jax.nn.sigmoid(g)) * u).astype(a_ref.dtype)


def _swiglu_staged(xn, wg_ref, wu_ref, wd_ref, a_ref):
    rows = xn.shape[0]
    _swiglu_stage(xn, wg_ref, wu_ref, a_ref, 0, D_FF)
    return jnp.dot(a_ref[0:rows, :], wd_ref[...], preferred_element_type=jnp.float32)


def _cast_weight_chunks(step, pairs):
    for src, dst in pairs:
        n = src.shape[0]
        dst[pl.ds(pl.multiple_of(step * n, n), n), :] = src[...].astype(dst.dtype)


def _whole(shape):
    return pl.BlockSpec(shape, lambda i: (0,) * len(shape), pipeline_mode=pl.Buffered(1))


def _weight_chunk(shape, n_steps):
    return pl.BlockSpec(shape, lambda i: (jnp.minimum(i, n_steps - 1), 0))


def _glu_tail(y, u, d_ref, wglu_ref, bglu_ref):
    y = y + d_ref[...] * u
    z = jax.nn.gelu(y)
    gate = jax.nn.sigmoid(_bdot(z, wglu_ref[...]) + bglu_ref[...])
    return z * gate


def _discretize_kernel(lr_ref, li_ref, ldt_ref, bre_ref, bim_ref, ctre_ref, ctim_ref,
                       are_ref, aim_ref, a2re_ref, a2im_ref, wb_ref, wct_ref, wdm_ref):
    row_group = lax.broadcasted_iota(jnp.int32, (LANES, BLK_STATE), 0) // SSM_GROUP
    col_group = lax.broadcasted_iota(jnp.int32, (LANES, BLK_STATE), 1) // SSM_STATE
    on_diagonal = row_group == col_group

    def blocks(m):
        pair = jnp.concatenate([m, m], axis=1)
        return jnp.where(on_diagonal, jnp.concatenate([pair] * (BLK_STATE // LANES), axis=1), 0.0)

    gpb = N_SSM_GROUPS // N_BLK

    def state_rows(ref):
        return jnp.concatenate(
            [jnp.concatenate([ref[k * gpb + g:k * gpb + g + 1, :] for g in range(gpb)], axis=1)
             for k in range(N_BLK)], axis=0)

    def per_state(row_ref):
        vals = row_ref[...]
        col_group = lax.broadcasted_iota(jnp.int32, (1, BLK_STATE), 1) // SSM_STATE
        rows = []
        for k in range(N_BLK):
            row = jnp.zeros((1, BLK_STATE), jnp.float32)
            for g in range(gpb):
                row = jnp.where(col_group == g, vals[:, k * gpb + g:k * gpb + g + 1], row)
            rows.append(row)
        return jnp.concatenate(rows, axis=0)

    lr, li = state_rows(lr_ref), state_rows(li_ref)
    dt = jnp.exp(per_state(ldt_ref))
    mag = jnp.exp(lr * dt)
    ab_re = mag * jnp.cos(li * dt)
    ab_im = mag * jnp.sin(li * dt)
    den = lr * lr + li * li
    nr = ab_re - 1.0
    ni = ab_im
    coef_re = (nr * lr + ni * li) / den
    coef_im = (ni * lr - nr * li) / den
    are_ref[...] = ab_re
    aim_ref[...] = ab_im
    a2_re = ab_re * ab_re - ab_im * ab_im
    a2_im = 2.0 * (ab_re * ab_im)
    bf = jnp.bfloat16
    for k in range(N_BLK):
        ar, ai = ab_re[k:k + 1, :], ab_im[k:k + 1, :]
        for j in range(HALF):
            lanes = slice(j * LANES, (j + 1) * LANES)
            a2re_ref[k * HALF + j] = jnp.broadcast_to(a2_re[k:k + 1, lanes], (SUBLANES, LANES))
            a2im_ref[k * HALF + j] = jnp.broadcast_to(a2_im[k:k + 1, lanes], (SUBLANES, LANES))
        cr, ci = coef_re[k:k + 1, :], coef_im[k:k + 1, :]
        bre, bim = blocks(bre_ref[k]), blocks(bim_ref[k])
        bb_re = cr * bre - ci * bim
        bb_im = cr * bim + ci * bre
        wb_ref[k, 0:LANES, 0:BLK_STATE] = (ar * bb_re - ai * bb_im).astype(bf)
        wb_ref[k, 0:LANES, BLK_STATE:] = (ar * bb_im + ai * bb_re).astype(bf)
        wb_ref[k, LANES:, 0:BLK_STATE] = bb_re.astype(bf)
        wb_ref[k, LANES:, BLK_STATE:] = bb_im.astype(bf)
        ctre, ctim = blocks(ctre_ref[k]), blocks(ctim_ref[k])
        wct_ref[k, 0:LANES, 0:BLK_STATE] = ctre.astype(bf)
        wct_ref[k, 0:LANES, BLK_STATE:] = (-ctim).astype(bf)
        wct_ref[k, LANES:, 0:BLK_STATE] = (ctre * ar - ctim * ai).astype(bf)
        wct_ref[k, LANES:, BLK_STATE:] = (-(ctre * ai + ctim * ar)).astype(bf)
        wdm = (lax.dot_general(bb_re, ctre, _NT, precision=lax.Precision.HIGHEST,
                               preferred_element_type=jnp.float32)
               - lax.dot_general(bb_im, ctim, _NT, precision=lax.Precision.HIGHEST,
                                 preferred_element_type=jnp.float32))
        wdm_ref[k] = wdm.astype(bf)


def _discretize(lr, li, ldt, bre, bim, ctre, ctim):
    f32, bf = jnp.float32, jnp.bfloat16
    return pl.pallas_call(
        _discretize_kernel,
        out_shape=[
            jax.ShapeDtypeStruct((N_BLK, BLK_STATE), f32),
            jax.ShapeDtypeStruct((N_BLK, BLK_STATE), f32),
            jax.ShapeDtypeStruct((N_TILE, SUBLANES, LANES), f32),
            jax.ShapeDtypeStruct((N_TILE, SUBLANES, LANES), f32),
            jax.ShapeDtypeStruct((N_BLK, 2 * LANES, 2 * BLK_STATE), bf),
            jax.ShapeDtypeStruct((N_BLK, 2 * LANES, 2 * BLK_STATE), bf),
            jax.ShapeDtypeStruct((N_BLK, LANES, LANES), bf),
        ],
        name="ssm_discretize",
    )(lr, li, ldt, bre, bim, ctre, ctim)


def _ffn1_mixer_kernel(x_ref, xd_ref, n1_ref, wg_c, wu_c, wd_c, nm_ref, win_c,
                       a2re_ref, a2im_ref, wb_ref, wct_ref, wdm_ref, d_ref, wglu_ref, bglu_ref, cw_ref,
                       h_ref, mix_ref, sre_ref, sim_ref, nconv_ref, ud_ref, vd_ref, gd_ref,
                       xs0_ref, xs1_ref, xs2_ref, xs3_ref, us_ref, ys_ref, l_ref, yo_ref, ye_ref, e_ref,
                       ecarry_ref, v_ref, g_ref, state_ref, wg_ref, wu_ref, wd_ref, win_ref, a_ref, xdd_ref):
    xs_refs = (xs0_ref, xs1_ref, xs2_ref, xs3_ref)
    step = pl.program_id(0)
    last = pl.num_programs(0) - 1
    nb, t_len, _ = x_ref.shape
    n_dec = xd_ref.shape[0]
    prow = nb * PAIR_PITCH
    tiles = [(k, j) for k in range(N_BLK) for j in range(HALF)]

    @pl.when(step < W1_STEPS)
    def _load_weights():
        _cast_weight_chunks(step, ((wg_c, wg_ref), (wu_c, wu_ref), (wd_c, wd_ref), (win_c, win_ref)))

    @pl.when(step == 0)
    def _init():
        for ref in (state_ref, us_ref, l_ref, e_ref, ecarry_ref, v_ref, g_ref):
            ref[...] = jnp.zeros_like(ref)
        xdd_ref[...] = xd_ref[...]

    @pl.when(step >= W1_STEPS)
    def _main():
        carry = {}

        def build_lhs():
            for b in range(nb):
                for k in range(N_BLK):
                    for par in range(2):
                        col = (2 * k + par) * LANES
                        l_ref[pl.ds(b * PAIR_PITCH, N_PAIR), col:col + LANES] = (
                            us_ref[k, pl.ds(b * t_len + par, N_PAIR, stride=2), :])

        def b_proj(ks):
            for k in ks:
                lhs = l_ref[:, 2 * k * LANES:2 * (k + 1) * LANES].astype(jnp.bfloat16)
                v = jnp.dot(lhs, wb_ref[k], preferred_element_type=jnp.float32)
                for j in range(SLABS_PER_BLK):
                    xs_refs[k][j] = v[:, j * LANES:(j + 1) * LANES]

        def scan(r0, r1):
            if r0 == 0:
                carry["re"] = [state_ref[k * SLABS_PER_BLK + j] for k, j in tiles]
                carry["im"] = [state_ref[k * SLABS_PER_BLK + HALF + j] for k, j in tiles]
            xr, xi = carry["re"], carry["im"]
            for r in range(r0, r1):
                rows = pl.ds(r, SUBLANES, stride=PAIR_PITCH)
                for n, (k, j) in enumerate(tiles):
                    a_re = a2re_ref[k * HALF + j]
                    a_im = a2im_ref[k * HALF + j]
                    nr = (a_re * xr[n] + xs_refs[k][j, rows, :]) - a_im * xi[n]
                    ni = (a_re * xi[n] + xs_refs[k][HALF + j, rows, :]) + a_im * xr[n]
                    xs_refs[k][j, rows, :] = nr
                    xs_refs[k][HALF + j, rows, :] = ni
                    xr[n], xi[n] = nr, ni
            if r1 == N_PAIR:
                for n, (k, j) in enumerate(tiles):
                    state_ref[k * SLABS_PER_BLK + j] = xr[n]
                    state_ref[k * SLABS_PER_BLK + HALF + j] = xi[n]

        def c_proj(ks):
            for k in ks:
                lanes = slice(k * LANES, (k + 1) * LANES)
                x = jnp.concatenate([xs_refs[k][j] for j in range(SLABS_PER_BLK)], axis=1)
                ce = lax.dot_general(x.astype(jnp.bfloat16), wct_ref[k], _NT,
                                     preferred_element_type=jnp.float32)
                yo_ref[:, lanes] = ce[:, 0:LANES]
                e_ref[k, SUBLANES:SUBLANES + prow, :] = ce[:, LANES:]
                nxt = e_ref[k, pl.ds(SUBLANES + N_PAIR - 1, nb, stride=PAIR_PITCH), :]
                e_ref[k, pl.ds(SUBLANES - 1, nb, stride=PAIR_PITCH), :] = ecarry_ref[k]
                ecarry_ref[k] = nxt
                u_even = l_ref[:, 2 * k * LANES:(2 * k + 1) * LANES].astype(jnp.bfloat16)
                ye_ref[:, lanes] = (e_ref[k, SUBLANES - 1:SUBLANES - 1 + prow, :]
                                    + jnp.dot(u_even, wdm_ref[k], preferred_element_type=jnp.float32))

        def glu_emit():
            for b in range(nb):
                for k in range(N_BLK):
                    lanes = slice(k * LANES, (k + 1) * LANES)
                    for par, src in ((0, ye_ref), (1, yo_ref)):
                        ys_ref[k, pl.ds(b * t_len + par, N_PAIR, stride=2), :] = (
                            src[pl.ds(b * PAIR_PITCH, N_PAIR), lanes])
            y = jnp.concatenate([ys_ref[k] for k in range(N_BLK)], axis=1)
            u = jnp.concatenate([us_ref[k] for k in range(N_BLK)], axis=1)
            s_out = _glu_tail(y, u, d_ref, wglu_ref, bglu_ref)
            mix_ref[:, :, 0:SSM_WIDTH] = s_out.reshape(nb, t_len, SSM_WIDTH).astype(mix_ref.dtype)

        def conv():
            v = v_ref[:, SUBLANES:SUBLANES + t_len, :]
            z = cw_ref[0] * v_ref[:, SUBLANES - 2:SUBLANES - 2 + t_len, :]
            z = z + cw_ref[1] * v_ref[:, SUBLANES - 1:SUBLANES - 1 + t_len, :]
            z = z + cw_ref[2] * v
            mix_ref[:, :, SSM_WIDTH:] = (g_ref[...] * z).astype(mix_ref.dtype)
            v_ref[:, 0:SUBLANES, :] = v_ref[:, t_len:t_len + SUBLANES, :]

        q = N_PAIR // 4
        pieces = [
            lambda: (conv(), build_lhs(), b_proj((0,))), lambda: b_proj((1,)), lambda: b_proj((2,)),
            lambda: b_proj((3,)),
            lambda: scan(0, q), lambda: scan(q, 2 * q), lambda: scan(2 * q, 3 * q), lambda: scan(3 * q, N_PAIR),
            lambda: c_proj((0, 1)), lambda: c_proj((2, 3)),
            glu_emit,
        ]
        assert len(pieces) <= D_FF // FFN_TF

        db = n_dec // t_len
        x_top = jnp.where(step == last, xdd_ref[...], x_ref[0:db].reshape(n_dec, D_MODEL))
        x = jnp.concatenate([x_top, x_ref[db:].reshape((nb - db) * t_len, D_MODEL)], axis=0)
        xn = _rms(x, n1_ref[...]).astype(jnp.bfloat16)
        for c, j in enumerate(range(0, D_FF, FFN_TF)):
            _swiglu_stage(xn, wg_ref, wu_ref, a_ref, j, j + FFN_TF)
            if c < len(pieces):
                pieces[c]()
        h = x + 0.5 * jnp.dot(a_ref[...], wd_ref[...], preferred_element_type=jnp.float32)
        hn = _rms(h, nm_ref[...]).astype(jnp.bfloat16)
        p = jnp.dot(hn, win_ref[...], preferred_element_type=jnp.float32)
        h_ref[...] = h.reshape(nb, t_len, D_MODEL)
        v_new = p[:, SSM_WIDTH + 2 * CONV_WIDTH:] * p[:, SSM_WIDTH:SSM_WIDTH + CONV_WIDTH]
        v_ref[:, SUBLANES:SUBLANES + t_len, :] = v_new.reshape(nb, t_len, CONV_WIDTH)
        g_ref[...] = p[:, SSM_WIDTH + CONV_WIDTH:SSM_WIDTH + 2 * CONV_WIDTH].reshape(nb, t_len, CONV_WIDTH)
        for k in range(N_BLK):
            us_ref[k] = p[:, k * LANES:(k + 1) * LANES]

    @pl.when(step == last)
    def _final():
        for n, (k, j) in enumerate(tiles):
            for out_ref, slab in ((sre_ref, k * SLABS_PER_BLK + j), (sim_ref, k * SLABS_PER_BLK + HALF + j)):
                tile = state_ref[slab]
                for half in range(LANES // SSM_STATE):
                    group = (k * BLK_STATE + j * LANES) // SSM_STATE + half
                    out_ref[:, group, :] = tile[:, half * SSM_STATE:(half + 1) * SSM_STATE]
        nconv_ref[...] = v_ref[:, SUBLANES - 2:SUBLANES, :]
        for k in range(N_BLK):
            ud_ref[:, k * LANES:(k + 1) * LANES] = us_ref[k, 0:n_dec, :]
        for b in range(n_dec // t_len):
            vd_ref[b * t_len:(b + 1) * t_len, :] = v_ref[b, SUBLANES:SUBLANES + t_len, :]
            gd_ref[b * t_len:(b + 1) * t_len, :] = g_ref[b]


def _ffn1_mixer(x, xd, n1, wg, wu, wd, nm, win, a2_re, a2_im, wb, wct, wdm, d, wglu, bglu, cw):
    nb, seq, _ = x.shape
    n_dec = xd.shape[0]
    n_chunk = seq // SCAN_T
    prow = nb * PAIR_PITCH
    f32, bf = jnp.float32, jnp.bfloat16
    main = lambda i: jnp.maximum(i - W1_STEPS, 0)
    return pl.pallas_call(
        _ffn1_mixer_kernel,
        grid=(W1_STEPS + n_chunk + 1,),
        in_specs=[
            pl.BlockSpec((nb, SCAN_T, D_MODEL), lambda i: (0, jnp.minimum(main(i), n_chunk - 1), 0)),
            pl.BlockSpec((n_dec, None, D_MODEL), lambda i: (0, 0, 0), pipeline_mode=pl.Buffered(1)),
            _whole((1, D_MODEL)),
            _weight_chunk((D_MODEL // W1_STEPS, D_FF), W1_STEPS),
            _weight_chunk((D_MODEL // W1_STEPS, D_FF), W1_STEPS),
            _weight_chunk((D_FF // W1_STEPS, D_MODEL), W1_STEPS),
            _whole((1, D_MODEL)),
            _weight_chunk((D_MODEL // W1_STEPS, IN_PROJ_WIDTH), W1_STEPS),
            _whole((N_TILE, SUBLANES, LANES)),
            _whole((N_TILE, SUBLANES, LANES)),
            _whole((N_BLK, 2 * LANES, 2 * BLK_STATE)),
            _whole((N_BLK, 2 * LANES, 2 * BLK_STATE)),
            _whole((N_BLK, LANES, LANES)),
            _whole((1, SSM_WIDTH)),
            _whole((SSM_WIDTH, SSM_WIDTH)),
            _whole((1, SSM_WIDTH)),
            _whole((CONV_K, 1, CONV_WIDTH)),
        ],
        out_specs=[
            pl.BlockSpec((nb, SCAN_T, D_MODEL), lambda i: (0, main(i), 0)),
            pl.BlockSpec((nb, SCAN_T, D_MODEL), lambda i: (0, jnp.clip(main(i) - 1, 0, n_chunk - 1), 0)),
            pl.BlockSpec((nb, N_SSM_GROUPS, SSM_STATE), lambda i: (0, 0, 0)),
            pl.BlockSpec((nb, N_SSM_GROUPS, SSM_STATE), lambda i: (0, 0, 0)),
            pl.BlockSpec((nb, CONV_K - 1, CONV_WIDTH), lambda i: (0, 0, 0)),
            pl.BlockSpec((n_dec, SSM_WIDTH), lambda i: (0, 0)),
            pl.BlockSpec((n_dec, CONV_WIDTH), lambda i: (0, 0)),
            pl.BlockSpec((n_dec, CONV_WIDTH), lambda i: (0, 0)),
        ],
        out_shape=[
            jax.ShapeDtypeStruct((nb, seq + SCAN_T, D_MODEL), f32),
            jax.ShapeDtypeStruct((nb, seq, D_MODEL), bf),
            jax.ShapeDtypeStruct((nb, N_SSM_GROUPS, SSM_STATE), f32),
            jax.ShapeDtypeStruct((nb, N_SSM_GROUPS, SSM_STATE), f32),
            jax.ShapeDtypeStruct((nb, CONV_K - 1, CONV_WIDTH), f32),
            jax.ShapeDtypeStruct((n_dec, SSM_WIDTH), f32),
            jax.ShapeDtypeStruct((n_dec, CONV_WIDTH), f32),
            jax.ShapeDtypeStruct((n_dec, CONV_WIDTH), f32),
        ],
        scratch_shapes=[
            *[pltpu.VMEM((SLABS_PER_BLK, prow, LANES), f32) for _ in range(N_BLK)],
            pltpu.VMEM((N_BLK, nb * SCAN_T, LANES), f32),
            pltpu.VMEM((N_BLK, nb * SCAN_T, LANES), f32),
            pltpu.VMEM((prow, 2 * SSM_WIDTH), f32),
            pltpu.VMEM((prow, SSM_WIDTH), f32),
            pltpu.VMEM((prow, SSM_WIDTH), f32),
            pltpu.VMEM((N_BLK, SUBLANES + prow, LANES), f32),
            pltpu.VMEM((N_BLK, SUBLANES, LANES), f32),
            pltpu.VMEM((nb, SCAN_T + SUBLANES, CONV_WIDTH), f32),
            pltpu.VMEM((nb, SCAN_T, CONV_WIDTH), f32),
            pltpu.VMEM((2 * N_TILE, SUBLANES, LANES), f32),
            pltpu.VMEM((D_MODEL, D_FF), bf),
            pltpu.VMEM((D_MODEL, D_FF), bf),
            pltpu.VMEM((D_FF, D_MODEL), bf),
            pltpu.VMEM((D_MODEL, IN_PROJ_WIDTH), bf),
            pltpu.VMEM((nb * SCAN_T, D_FF), bf),
            pltpu.VMEM((n_dec, D_MODEL), f32),
        ],
        compiler_params=pltpu.CompilerParams(
            dimension_semantics=("arbitrary",), vmem_limit_bytes=VMEM_LIMIT),
        name="ffn1_mixer",
    )(x, xd, n1, wg, wu, wd, nm, win, a2_re, a2_im, wb, wct, wdm, d, wglu, bglu, cw)


def _ffn2_rows(h, mix, weights, a_ref):
    wout_ref, n2_ref, wg_ref, wu_ref, wd_ref, nf_ref = weights
    h = h + jnp.dot(mix, wout_ref[...], preferred_element_type=jnp.float32)
    hn = _rms(h, n2_ref[...]).astype(jnp.bfloat16)
    h = h + 0.5 * _swiglu_staged(hn, wg_ref, wu_ref, wd_ref, a_ref)
    return _rms(h, nf_ref[...])


def _ffn2_kernel(h_ref, mix_ref, hd0_ref, hd1_ref, mixd_ref, wout_c, wg_c, wu_c, wd_c, n2_ref, nf_ref,
                 y_ref, yd_ref, wout_ref, wg_ref, wu_ref, wd_ref, a0_ref, a1_ref):
    step = pl.program_id(0)
    last = pl.num_programs(0) - 1
    weights = (wout_ref, n2_ref, wg_ref, wu_ref, wd_ref, nf_ref)

    @pl.when(step < W2_STEPS)
    def _load_weights():
        _cast_weight_chunks(step, ((wout_c, wout_ref), (wg_c, wg_ref), (wu_c, wu_ref), (wd_c, wd_ref)))

    @pl.when((step >= W2_STEPS) & (step < last))
    def _prompt_tile():
        assert h_ref.shape[0] == 2 * FFN_SUB
        ra, rb = slice(0, FFN_SUB), slice(FFN_SUB, 2 * FFN_SUB)
        h_a = h_ref[ra, :] + jnp.dot(mix_ref[ra, :], wout_ref[...], preferred_element_type=jnp.float32)
        h_b = h_ref[rb, :] + jnp.dot(mix_ref[rb, :], wout_ref[...], preferred_element_type=jnp.float32)
        hn_a = _rms(h_a, n2_ref[...]).astype(jnp.bfloat16)
        _swiglu_stage(hn_a, wg_ref, wu_ref, a0_ref, 0, FFN_TF)
        hn_b = _rms(h_b, n2_ref[...]).astype(jnp.bfloat16)
        _swiglu_stage(hn_a, wg_ref, wu_ref, a0_ref, FFN_TF, D_FF)
        f_a = jnp.dot(a0_ref[...], wd_ref[...], preferred_element_type=jnp.float32)
        _swiglu_stage(hn_b, wg_ref, wu_ref, a1_ref, 0, FFN_TF)
        y_ref[ra, :] = _rms(h_a + 0.5 * f_a, nf_ref[...])
        _swiglu_stage(hn_b, wg_ref, wu_ref, a1_ref, FFN_TF, D_FF)
        f_b = jnp.dot(a1_ref[...], wd_ref[...], preferred_element_type=jnp.float32)
        y_ref[rb, :] = _rms(h_b + 0.5 * f_b, nf_ref[...])

    @pl.when(step == last)
    def _decode_rows():
        h_d = jnp.concatenate([hd0_ref[...], hd1_ref[...]], axis=0)
        yd_ref[...] = _ffn2_rows(h_d, mixd_ref[...], weights, a0_ref)


def _ffn2(h_all, mix, mixd, wout, n2, wg, wu, wd, nf):
    nb, seq, _ = mix.shape
    seq_all = h_all.shape[1]
    n_dec = mixd.shape[0]
    tail = seq_all - seq
    assert n_dec == 2 * tail
    per_b = seq // FFN_TM
    n_tiles = nb * per_b
    h_all = h_all.reshape(nb * seq_all, D_MODEL)
    mix = mix.reshape(nb * seq, D_MODEL)
    idx = lambda i: jnp.clip(i - W2_STEPS, 0, n_tiles - 1)
    tile = lambda i: (idx(i), 0)
    assert seq_all % SUBLANES == 0
    tile_all = lambda i: (pl.multiple_of((idx(i) // per_b) * seq_all + (idx(i) % per_b) * FFN_TM, SUBLANES), 0)
    yp, yd = pl.pallas_call(
        _ffn2_kernel,
        grid=(W2_STEPS + n_tiles + 1,),
        in_specs=[
            pl.BlockSpec((pl.Element(FFN_TM), pl.Element(D_MODEL)), tile_all),
            pl.BlockSpec((FFN_TM, D_MODEL), tile),
            pl.BlockSpec((pl.Element(tail), pl.Element(D_MODEL)), lambda i: (seq, 0),
                         pipeline_mode=pl.Buffered(1)),
            pl.BlockSpec((pl.Element(tail), pl.Element(D_MODEL)), lambda i: (seq_all + seq, 0),
                         pipeline_mode=pl.Buffered(1)),
            _whole(mixd.shape),
            _weight_chunk((D_MODEL // W2_STEPS, D_MODEL), W2_STEPS),
            _weight_chunk((D_MODEL // W2_STEPS, D_FF), W2_STEPS),
            _weight_chunk((D_MODEL // W2_STEPS, D_FF), W2_STEPS),
            _weight_chunk((D_FF // W2_STEPS, D_MODEL), W2_STEPS),
            _whole((1, D_MODEL)),
            _whole((1, D_MODEL)),
        ],
        out_specs=[
            pl.BlockSpec((FFN_TM, D_MODEL), tile),
            pl.BlockSpec((n_dec, None, D_MODEL), lambda i: (0, 0, 0)),
        ],
        out_shape=[
            jax.ShapeDtypeStruct((nb * seq, D_MODEL), jnp.float32),
            jax.ShapeDtypeStruct((n_dec, 1, D_MODEL), jnp.float32),
        ],
        scratch_shapes=[
            pltpu.VMEM((D_MODEL, D_MODEL), jnp.bfloat16),
            pltpu.VMEM((D_MODEL, D_FF), jnp.bfloat16),
            pltpu.VMEM((D_MODEL, D_FF), jnp.bfloat16),
            pltpu.VMEM((D_FF, D_MODEL), jnp.bfloat16),
            pltpu.VMEM((FFN_SUB, D_FF), jnp.bfloat16),
            pltpu.VMEM((FFN_SUB, D_FF), jnp.bfloat16),
        ],
        compiler_params=pltpu.CompilerParams(
            dimension_semantics=("arbitrary",), vmem_limit_bytes=VMEM_LIMIT),
        name="outproj_ffn2",
    )(h_all, mix, h_all, h_all, mixd, wout, wg, wu, wd, n2, nf)
    return yp.reshape(nb, seq, D_MODEL), yd


def _mixer_step_kernel(u_ref, v_ref, g_ref, hre_ref, him_ref, buf_ref, are_ref, aim_ref, bb_ref, wct_ref,
                       d_ref, wglu_ref, bglu_ref, cw_ref, mix_ref, sre_ref, sim_ref, nconv_ref):
    u = u_ref[...]
    u_bf = u.astype(jnp.bfloat16)
    ys = []
    for k in range(N_BLK):
        bu = jnp.dot(u_bf[:, k * LANES:(k + 1) * LANES], bb_ref[k], preferred_element_type=jnp.float32)
        st = slice(k * BLK_STATE, (k + 1) * BLK_STATE)
        a_re = are_ref[k:k + 1, :]
        a_im = aim_ref[k:k + 1, :]
        h_re = hre_ref[st, :].T
        h_im = him_ref[st, :].T
        x_re = (a_re * h_re + bu[:, :BLK_STATE]) - a_im * h_im
        x_im = (a_re * h_im + bu[:, BLK_STATE:]) + a_im * h_re
        sre_ref[st, :] = x_re.T
        sim_ref[st, :] = x_im.T
        x = jnp.concatenate([x_re, x_im], axis=1).astype(jnp.bfloat16)
        ys.append(lax.dot_general(x, wct_ref[k], _NT, preferred_element_type=jnp.float32))
    y = jnp.concatenate(ys, axis=1)
    mix_ref[:, 0:SSM_WIDTH] = _glu_tail(y, u, d_ref, wglu_ref, bglu_ref).astype(mix_ref.dtype)

    v = v_ref[...]
    buf1 = buf_ref[:, 1, :]
    z = cw_ref[0] * buf_ref[:, 0, :] + cw_ref[1] * buf1 + cw_ref[2] * v
    mix_ref[:, SSM_WIDTH:] = (g_ref[...] * z).astype(mix_ref.dtype)
    nconv_ref[:, 0, :] = buf1
    nconv_ref[:, 1, :] = v


def _mixer_step(u, v, g, h_re, h_im, conv_state, a_re, a_im, wb, wct, d, wglu, bglu, cw):
    nb = u.shape[0]
    conv_spec = pl.BlockSpec((None, nb, CONV_K - 1, CONV_WIDTH), lambda i: (0, 0, 0, 0))
    half = (N_BLK, LANES, 2 * BLK_STATE)
    bb_spec = pl.BlockSpec(half, lambda i: (0, 1, 0), pipeline_mode=pl.Buffered(1))
    c_spec = pl.BlockSpec(half, lambda i: (0, 0, 0), pipeline_mode=pl.Buffered(1))
    rows = (u, v, g, h_re, h_im)
    tail = (d, wglu, bglu, cw)
    return pl.pallas_call(
        _mixer_step_kernel,
        grid=(1,),
        in_specs=([_whole(a.shape) for a in rows] + [conv_spec] + [_whole(a_re.shape), _whole(a_im.shape)]
                  + [bb_spec, c_spec] + [_whole(a.shape) for a in tail]),
        out_specs=[
            pl.BlockSpec((nb, D_MODEL), lambda i: (0, 0)),
            pl.BlockSpec((N_STATE, nb), lambda i: (0, 0)),
            pl.BlockSpec((N_STATE, nb), lambda i: (0, 0)),
            conv_spec,
        ],
        out_shape=[
            jax.ShapeDtypeStruct((nb, D_MODEL), jnp.bfloat16),
            jax.ShapeDtypeStruct((N_STATE, nb), jnp.float32),
            jax.ShapeDtypeStruct((N_STATE, nb), jnp.float32),
            jax.ShapeDtypeStruct(conv_state.shape, jnp.float32),
        ],
        compiler_params=pltpu.CompilerParams(
            dimension_semantics=("arbitrary",), vmem_limit_bytes=VMEM_LIMIT),
        name="mixer_step",
    )(*rows, conv_state, a_re, a_im, wb, wct, *tail)


def _ssm_layout(lam_re, lam_im, log_dt, b_re, b_im, c_re, c_im):
    gpb = N_SSM_GROUPS // N_BLK

    def b_rows(b):
        return b.reshape(N_BLK, gpb, SSM_STATE, SSM_GROUP).transpose(0, 1, 3, 2).reshape(N_BLK, LANES, SSM_STATE)

    def c_rows(c):
        return c.reshape(N_BLK, LANES, SSM_STATE)

    return lam_re, lam_im, log_dt, b_rows(b_re), b_rows(b_im), c_rows(c_re), c_rows(c_im)


def kernel(x_prompt, x_sample, state_ssm_re, state_ssm_im, state_conv, ffn1_norm, ffn1_w_gate, ffn1_w_up, ffn1_w_down, mix_norm, w_in, ssm_lambda_re, ssm_lambda_im, ssm_log_dt, ssm_b_re, ssm_b_im, ssm_c_re, ssm_c_im, ssm_d, ssm_w_glu, ssm_b_glu, conv_w, w_out, ffn2_norm, ffn2_w_gate, ffn2_w_up, ffn2_w_down, final_norm):
    assert ffn1_norm.shape[0] == 1, "single-layer trunk"
    n_prompt, seq, _ = x_prompt.shape
    n_dec, dec_seq, _ = x_sample.shape
    assert dec_seq == 1 and n_prompt == SUBLANES and seq % FFN_TM == 0 and seq % SCAN_T == 0
    assert n_dec % SCAN_T == 0 and n_dec <= n_prompt * SCAN_T and FFN_TM % FFN_SUB == 0

    n1 = ffn1_norm[0][None, :]
    nm = mix_norm[0][None, :]
    n2 = ffn2_norm[0][None, :]
    nf = final_norm[None, :]
    wglu = ssm_w_glu[0]
    layout = _ssm_layout(ssm_lambda_re[0], ssm_lambda_im[0], ssm_log_dt, ssm_b_re[0], ssm_b_im[0],
                         ssm_c_re[0], ssm_c_im[0])
    a_re, a_im, a2_re, a2_im, wb, wct, wdm = _discretize(*layout)
    d = ssm_d[0][None, :]
    bglu = ssm_b_glu[0][None, :]
    cw = conv_w.transpose(1, 0, 2)

    h_all, mixp, p_re, p_im, p_conv, ud, vd, gd = _ffn1_mixer(
        x_prompt, x_sample, n1, ffn1_w_gate[0], ffn1_w_up[0], ffn1_w_down[0], nm,
        w_in[0], a2_re, a2_im, wb, wct, wdm, d, wglu, bglu, cw)

    s_re0 = state_ssm_re[0].transpose(1, 2, 0).reshape(N_STATE, n_dec)
    s_im0 = state_ssm_im[0].transpose(1, 2, 0).reshape(N_STATE, n_dec)
    mixd, s_re, s_im, s_conv = _mixer_step(ud, vd, gd, s_re0, s_im0, state_conv, a_re, a_im, wb, wct,
                                           d, wglu, bglu, cw)

    yp, yd = _ffn2(h_all, mixp, mixd, w_out[0], n2, ffn2_w_gate[0], ffn2_w_up[0], ffn2_w_down[0], nf)

    gs = (N_SSM_GROUPS, SSM_STATE)
    return (yp,
            yd,
            p_re[None],
            p_im[None],
            p_conv[None],
            s_re.reshape(*gs, n_dec).transpose(2, 0, 1)[None],
            s_im.reshape(*gs, n_dec).transpose(2, 0, 1)[None],
            s_conv)
```

```python
import jax
import jax.numpy as jnp
from jax import lax
from jax.experimental import pallas as pl
from jax.experimental.pallas import tpu as pltpu

D_MODEL = 1024
D_FF = 2816
SSM_WIDTH = 512
CONV_WIDTH = 512
SSM_GROUP = 16
N_SSM_GROUPS = 32
SSM_STATE = 64
CONV_K = 3
IN_PROJ_WIDTH = SSM_WIDTH + 3 * CONV_WIDTH
EPS = 1e-6

LANES = 128
SUBLANES = 8
N_STATE = N_SSM_GROUPS * SSM_STATE
N_BLK = SSM_WIDTH // LANES
BLK_STATE = N_STATE // N_BLK
SLABS_PER_BLK = 2 * BLK_STATE // LANES
HALF = SLABS_PER_BLK // 2
N_TILE = N_BLK * HALF

FFN_TM = 1024
FFN_SUB = 512
W1_STEPS = 16
W2_STEPS = 8
FFN_TF = 256
SCAN_T = 64
N_PAIR = SCAN_T // 2
PAIR_PITCH = N_PAIR + 4
VMEM_LIMIT = 62 * 1024 * 1024

_NT = (((1,), (1,)), ((), ()))


def _rms(x, g):
    r = lax.rsqrt(jnp.mean(x * x, axis=-1, keepdims=True) + EPS)
    return x * r * g


def _bdot(a, b):
    return jnp.dot(a.astype(jnp.bfloat16), b.astype(jnp.bfloat16), preferred_element_type=jnp.float32)


def _swiglu_stage(xn, wg_ref, wu_ref, a_ref, lo, hi):
    rows = xn.shape[0]
    for j in range(lo, hi, FFN_TF):
        g = jnp.dot(xn, wg_ref[:, j:j + FFN_TF], preferred_element_type=jnp.float32)
        u = jnp.dot(xn, wu_ref[:, j:j + FFN_TF], preferred_element_type=jnp.float32)
        a_ref[0:rows, j:j + FFN_TF] = ((g * jax.nn.sigmoid(g)) * u).astype(a_ref.dtype)


def _swiglu_staged(xn, wg_ref, wu_ref, wd_ref, a_ref):
    rows = xn.shape[0]
    _swiglu_stage(xn, wg_ref, wu_ref, a_ref, 0, D_FF)
    return jnp.dot(a_ref[0:rows, :], wd_ref[...], preferred_element_type=jnp.float32)


def _cast_weight_chunks(step, pairs):
    for src, dst in pairs:
        n = src.shape[0]
        dst[pl.ds(pl.multiple_of(step * n, n), n), :] = src[...].astype(dst.dtype)


def _whole(shape):
    return pl.BlockSpec(shape, lambda i: (0,) * len(shape), pipeline_mode=pl.Buffered(1))


def _weight_chunk(shape, n_steps):
    return pl.BlockSpec(shape, lambda i: (jnp.minimum(i, n_steps - 1), 0))


def _glu_tail(y, u, d_ref, wglu_ref, bglu_ref):
    y = y + d_ref[...] * u
    z = jax.nn.gelu(y)
    gate = jax.nn.sigmoid(_bdot(z, wglu_ref[...]) + bglu_ref[...])
    return z * gate


def _discretize_kernel(lr_ref, li_ref, ldt_ref, bre_ref, bim_ref, ctre_ref, ctim_ref,
                       are_ref, aim_ref, a2re_ref, a2im_ref, wb_ref, wct_ref, wdm_ref):
    row_group = lax.broadcasted_iota(jnp.int32, (LANES, BLK_STATE), 0) // SSM_GROUP
    col_group = lax.broadcasted_iota(jnp.int32, (LANES, BLK_STATE), 1) // SSM_STATE
    on_diagonal = row_group == col_group

    def blocks(m):
        pair = jnp.concatenate([m, m], axis=1)
        return jnp.where(on_diagonal, jnp.concatenate([pair] * (BLK_STATE // LANES), axis=1), 0.0)

    gpb = N_SSM_GROUPS // N_BLK

    def state_rows(ref):
        return jnp.concatenate(
            [jnp.concatenate([ref[k * gpb + g:k * gpb + g + 1, :] for g in range(gpb)], axis=1)
             for k in range(N_BLK)], axis=0)

    def per_state(row_ref):
        vals = row_ref[...]
        col_group = lax.broadcasted_iota(jnp.int32, (1, BLK_STATE), 1) // SSM_STATE
        rows = []
        for k in range(N_BLK):
            row = jnp.zeros((1, BLK_STATE), jnp.float32)
            for g in range(gpb):
                row = jnp.where(col_group == g, vals[:, k * gpb + g:k * gpb + g + 1], row)
            rows.append(row)
        return jnp.concatenate(rows, axis=0)

    lr, li = state_rows(lr_ref), state_rows(li_ref)
    dt = jnp.exp(per_state(ldt_ref))
    mag = jnp.exp(lr * dt)
    ab_re = mag * jnp.cos(li * dt)
    ab_im = mag * jnp.sin(li * dt)
    den = lr * lr + li * li
    nr = ab_re - 1.0
    ni = ab_im
    coef_re = (nr * lr + ni * li) / den
    coef_im = (ni * lr - nr * li) / den
    are_ref[...] = ab_re
    aim_ref[...] = ab_im
    a2_re = ab_re * ab_re - ab_im * ab_im
    a2_im = 2.0 * (ab_re * ab_im)
    bf = jnp.bfloat16
    for k in range(N_BLK):
        ar, ai = ab_re[k:k + 1, :], ab_im[k:k + 1, :]
        for j in range(HALF):
            lanes = slice(j * LANES, (j + 1) * LANES)
            a2re_ref[k * HALF + j] = jnp.broadcast_to(a2_re[k:k + 1, lanes], (SUBLANES, LANES))
            a2im_ref[k * HALF + j] = jnp.broadcast_to(a2_im[k:k + 1, lanes], (SUBLANES, LANES))
        cr, ci = coef_re[k:k + 1, :], coef_im[k:k + 1, :]
        bre, bim = blocks(bre_ref[k]), blocks(bim_ref[k])
        bb_re = cr * bre - ci * bim
        bb_im = cr * bim + ci * bre
        wb_ref[k, 0:LANES, 0:BLK_STATE] = (ar * bb_re - ai * bb_im).astype(bf)
        wb_ref[k, 0:LANES, BLK_STATE:] = (ar * bb_im + ai * bb_re).astype(bf)
        wb_ref[k, LANES:, 0:BLK_STATE] = bb_re.astype(bf)
        wb_ref[k, LANES:, BLK_STATE:] = bb_im.astype(bf)
        ctre, ctim = blocks(ctre_ref[k]), blocks(ctim_ref[k])
        wct_ref[k, 0:LANES, 0:BLK_STATE] = ctre.astype(bf)
        wct_ref[k, 0:LANES, BLK_STATE:] = (-ctim).astype(bf)
        wct_ref[k, LANES:, 0:BLK_STATE] = (ctre * ar - ctim * ai).astype(bf)
        wct_ref[k, LANES:, BLK_STATE:] = (-(ctre * ai + ctim * ar)).astype(bf)
        wdm = (lax.dot_general(bb_re, ctre, _NT, precision=lax.Precision.HIGHEST,
                               preferred_element_type=jnp.float32)
               - lax.dot_general(bb_im, ctim, _NT, precision=lax.Precision.HIGHEST,
                                 preferred_element_type=jnp.float32))
        wdm_ref[k] = wdm.astype(bf)


def _discretize(lr, li, ldt, bre, bim, ctre, ctim):
    f32, bf = jnp.float32, jnp.bfloat16
    return pl.pallas_call(
        _discretize_kernel,
        out_shape=[
            jax.ShapeDtypeStruct((N_BLK, BLK_STATE), f32),
            jax.ShapeDtypeStruct((N_BLK, BLK_STATE), f32),
            jax.ShapeDtypeStruct((N_TILE, SUBLANES, LANES), f32),
            jax.ShapeDtypeStruct((N_TILE, SUBLANES, LANES), f32),
            jax.ShapeDtypeStruct((N_BLK, 2 * LANES, 2 * BLK_STATE), bf),
            jax.ShapeDtypeStruct((N_BLK, 2 * LANES, 2 * BLK_STATE), bf),
            jax.ShapeDtypeStruct((N_BLK, LANES, LANES), bf),
        ],
        name="ssm_discretize",
    )(lr, li, ldt, bre, bim, ctre, ctim)


def _ffn1_mixer_kernel(x_ref, xd_ref, n1_ref, wg_c, wu_c, wd_c, nm_ref, win_c,
                       a2re_ref, a2im_ref, wb_ref, wct_ref, wdm_ref, d_ref, wglu_ref, bglu_ref, cw_ref,
                       h_ref, mix_ref, sre_ref, sim_ref, nconv_ref, ud_ref, vd_ref, gd_ref,
                       xs0_ref, xs1_ref, xs2_ref, xs3_ref, us_ref, ys_ref, l_ref, yo_ref, ye_ref, e_ref,
                       ecarry_ref, v_ref, g_ref, state_ref, wg_ref, wu_ref, wd_ref, win_ref, a_ref):
    xs_refs = (xs0_ref, xs1_ref, xs2_ref, xs3_ref)
    step = pl.program_id(0)
    last = pl.num_programs(0) - 1
    nb, t_len, _ = x_ref.shape
    n_dec = xd_ref.shape[0]
    prow = nb * PAIR_PITCH
    tiles = [(k, j) for k in range(N_BLK) for j in range(HALF)]

    @pl.when(step < W1_STEPS)
    def _load_weights():
        _cast_weight_chunks(step, ((wg_c, wg_ref), (wu_c, wu_ref), (wd_c, wd_ref), (win_c, win_ref)))

    @pl.when(step == 0)
    def _init():
        for ref in (state_ref, us_ref, l_ref, e_ref, ecarry_ref, v_ref, g_ref):
            ref[...] = jnp.zeros_like(ref)

    def run(decode):
        carry = {}

        def build_lhs():
            for b in range(nb):
                for k in range(N_BLK):
                    for par in range(2):
                        col = (2 * k + par) * LANES
                        l_ref[pl.ds(b * PAIR_PITCH, N_PAIR), col:col + LANES] = (
                            us_ref[k, pl.ds(b * t_len + par, N_PAIR, stride=2), :])

        def b_proj(ks):
            for k in ks:
                lhs = l_ref[:, 2 * k * LANES:2 * (k + 1) * LANES].astype(jnp.bfloat16)
                v = jnp.dot(lhs, wb_ref[k], preferred_element_type=jnp.float32)
                for j in range(SLABS_PER_BLK):
                    xs_refs[k][j] = v[:, j * LANES:(j + 1) * LANES]

        def scan(r0, r1):
            if r0 == 0:
                carry["re"] = [state_ref[k * SLABS_PER_BLK + j] for k, j in tiles]
                carry["im"] = [state_ref[k * SLABS_PER_BLK + HALF + j] for k, j in tiles]
            xr, xi = carry["re"], carry["im"]
            for r in range(r0, r1):
                rows = pl.ds(r, SUBLANES, stride=PAIR_PITCH)
                for n, (k, j) in enumerate(tiles):
                    a_re = a2re_ref[k * HALF + j]
                    a_im = a2im_ref[k * HALF + j]
                    nr = (a_re * xr[n] + xs_refs[k][j, rows, :]) - a_im * xi[n]
                    ni = (a_re * xi[n] + xs_refs[k][HALF + j, rows, :]) + a_im * xr[n]
                    xs_refs[k][j, rows, :] = nr
                    xs_refs[k][HALF + j, rows, :] = ni
                    xr[n], xi[n] = nr, ni
            if r1 == N_PAIR:
                for n, (k, j) in enumerate(tiles):
                    state_ref[k * SLABS_PER_BLK + j] = xr[n]
                    state_ref[k * SLABS_PER_BLK + HALF + j] = xi[n]

        def c_proj(ks):
            for k in ks:
                lanes = slice(k * LANES, (k + 1) * LANES)
                x = jnp.concatenate([xs_refs[k][j] for j in range(SLABS_PER_BLK)], axis=1)
                ce = lax.dot_general(x.astype(jnp.bfloat16), wct_ref[k], _NT,
                                     preferred_element_type=jnp.float32)
                yo_ref[:, lanes] = ce[:, 0:LANES]
                e_ref[k, SUBLANES:SUBLANES + prow, :] = ce[:, LANES:]
                nxt = e_ref[k, pl.ds(SUBLANES + N_PAIR - 1, nb, stride=PAIR_PITCH), :]
                e_ref[k, pl.ds(SUBLANES - 1, nb, stride=PAIR_PITCH), :] = ecarry_ref[k]
                ecarry_ref[k] = nxt
                u_even = l_ref[:, 2 * k * LANES:(2 * k + 1) * LANES].astype(jnp.bfloat16)
                ye_ref[:, lanes] = (e_ref[k, SUBLANES - 1:SUBLANES - 1 + prow, :]
                                    + jnp.dot(u_even, wdm_ref[k], preferred_element_type=jnp.float32))

        def glu_emit():
            for b in range(nb):
                for k in range(N_BLK):
                    lanes = slice(k * LANES, (k + 1) * LANES)
                    for par, src in ((0, ye_ref), (1, yo_ref)):
                        ys_ref[k, pl.ds(b * t_len + par, N_PAIR, stride=2), :] = (
                            src[pl.ds(b * PAIR_PITCH, N_PAIR), lanes])
            y = jnp.concatenate([ys_ref[k] for k in range(N_BLK)], axis=1)
            u = jnp.concatenate([us_ref[k] for k in range(N_BLK)], axis=1)
            s_out = _glu_tail(y, u, d_ref, wglu_ref, bglu_ref)
            mix_ref[:, :, 0:SSM_WIDTH] = s_out.reshape(nb, t_len, SSM_WIDTH).astype(mix_ref.dtype)

        def conv():
            v = v_ref[:, SUBLANES:SUBLANES + t_len, :]
            z = cw_ref[0] * v_ref[:, SUBLANES - 2:SUBLANES - 2 + t_len, :]
            z = z + cw_ref[1] * v_ref[:, SUBLANES - 1:SUBLANES - 1 + t_len, :]
            z = z + cw_ref[2] * v
            mix_ref[:, :, SSM_WIDTH:] = (g_ref[...] * z).astype(mix_ref.dtype)
            v_ref[:, 0:SUBLANES, :] = v_ref[:, t_len:t_len + SUBLANES, :]

        q = N_PAIR // 4
        pieces = [
            lambda: (conv(), build_lhs(), b_proj((0,))), lambda: b_proj((1,)), lambda: b_proj((2,)),
            lambda: b_proj((3,)),
            lambda: scan(0, q), lambda: scan(q, 2 * q), lambda: scan(2 * q, 3 * q), lambda: scan(3 * q, N_PAIR),
            lambda: c_proj((0, 1)), lambda: c_proj((2, 3)),
            glu_emit,
        ]
        assert len(pieces) <= D_FF // FFN_TF

        rows = n_dec if decode else nb * t_len
        x = xd_ref[...] if decode else x_ref[...].reshape(rows, D_MODEL)
        xn = _rms(x, n1_ref[...]).astype(jnp.bfloat16)
        for c, j in enumerate(range(0, D_FF, FFN_TF)):
            _swiglu_stage(xn, wg_ref, wu_ref, a_ref, j, j + FFN_TF)
            if c < len(pieces):
                pieces[c]()
        h = x + 0.5 * jnp.dot(a_ref[0:rows, :], wd_ref[...], preferred_element_type=jnp.float32)
        hn = _rms(h, nm_ref[...]).astype(jnp.bfloat16)
        p = jnp.dot(hn, win_ref[...], preferred_element_type=jnp.float32)
        v_new = p[:, SSM_WIDTH + 2 * CONV_WIDTH:] * p[:, SSM_WIDTH:SSM_WIDTH + CONV_WIDTH]
        g_new = p[:, SSM_WIDTH + CONV_WIDTH:SSM_WIDTH + 2 * CONV_WIDTH]
        if decode:
            db = n_dec // t_len
            h_ref[0:db] = h.reshape(db, t_len, D_MODEL)
            h_ref[db:] = jnp.zeros((nb - db, t_len, D_MODEL), jnp.float32)
            ud_ref[...] = p[:, 0:SSM_WIDTH]
            vd_ref[...] = v_new
            gd_ref[...] = g_new
            for n, (k, j) in enumerate(tiles):
                for out_ref, slab in ((sre_ref, k * SLABS_PER_BLK + j), (sim_ref, k * SLABS_PER_BLK + HALF + j)):
                    tile = state_ref[slab]
                    for half in range(LANES // SSM_STATE):
                        group = (k * BLK_STATE + j * LANES) // SSM_STATE + half
                        out_ref[:, group, :] = tile[:, half * SSM_STATE:(half + 1) * SSM_STATE]
            nconv_ref[...] = v_ref[:, SUBLANES - 2:SUBLANES, :]
        else:
            h_ref[...] = h.reshape(nb, t_len, D_MODEL)
            v_ref[:, SUBLANES:SUBLANES + t_len, :] = v_new.reshape(nb, t_len, CONV_WIDTH)
            g_ref[...] = g_new.reshape(nb, t_len, CONV_WIDTH)
            for k in range(N_BLK):
                us_ref[k] = p[:, k * LANES:(k + 1) * LANES]

    @pl.when((step >= W1_STEPS) & (step < last))
    def _prompt_chunk():
        run(decode=False)

    @pl.when(step == last)
    def _decode_rows():
        run(decode=True)


def _ffn1_mixer(x, xd, n1, wg, wu, wd, nm, win, a2_re, a2_im, wb, wct, wdm, d, wglu, bglu, cw):
    nb, seq, _ = x.shape
    n_dec = xd.shape[0]
    n_chunk = seq // SCAN_T
    prow = nb * PAIR_PITCH
    f32, bf = jnp.float32, jnp.bfloat16
    main = lambda i: jnp.maximum(i - W1_STEPS, 0)
    return pl.pallas_call(
        _ffn1_mixer_kernel,
        grid=(W1_STEPS + n_chunk + 1,),
        in_specs=[
            pl.BlockSpec((nb, SCAN_T, D_MODEL), lambda i: (0, jnp.minimum(main(i), n_chunk - 1), 0)),
            pl.BlockSpec((n_dec, None, D_MODEL), lambda i: (0, 0, 0), pipeline_mode=pl.Buffered(1)),
            _whole((1, D_MODEL)),
            _weight_chunk((D_MODEL // W1_STEPS, D_FF), W1_STEPS),
            _weight_chunk((D_MODEL // W1_STEPS, D_FF), W1_STEPS),
            _weight_chunk((D_FF // W1_STEPS, D_MODEL), W1_STEPS),
            _whole((1, D_MODEL)),
            _weight_chunk((D_MODEL // W1_STEPS, IN_PROJ_WIDTH), W1_STEPS),
            _whole((N_TILE, SUBLANES, LANES)),
            _whole((N_TILE, SUBLANES, LANES)),
            _whole((N_BLK, 2 * LANES, 2 * BLK_STATE)),
            _whole((N_BLK, 2 * LANES, 2 * BLK_STATE)),
            _whole((N_BLK, LANES, LANES)),
            _whole((1, SSM_WIDTH)),
            _whole((SSM_WIDTH, SSM_WIDTH)),
            _whole((1, SSM_WIDTH)),
            _whole((CONV_K, 1, CONV_WIDTH)),
        ],
        out_specs=[
            pl.BlockSpec((nb, SCAN_T, D_MODEL), lambda i: (0, main(i), 0)),
            pl.BlockSpec((nb, SCAN_T, D_MODEL), lambda i: (0, jnp.clip(main(i) - 1, 0, n_chunk - 1), 0)),
            pl.BlockSpec((nb, N_SSM_GROUPS, SSM_STATE), lambda i: (0, 0, 0)),
            pl.BlockSpec((nb, N_SSM_GROUPS, SSM_STATE), lambda i: (0, 0, 0)),
            pl.BlockSpec((nb, CONV_K - 1, CONV_WIDTH), lambda i: (0, 0, 0)),
            pl.BlockSpec((n_dec, SSM_WIDTH), lambda i: (0, 0)),
            pl.BlockSpec((n_dec, CONV_WIDTH), lambda i: (0, 0)),
            pl.BlockSpec((n_dec, CONV_WIDTH), lambda i: (0, 0)),
        ],
        out_shape=[
            jax.ShapeDtypeStruct((nb, seq + SCAN_T, D_MODEL), f32),
            jax.ShapeDtypeStruct((nb, seq, D_MODEL), bf),
            jax.ShapeDtypeStruct((nb, N_SSM_GROUPS, SSM_STATE), f32),
            jax.ShapeDtypeStruct((nb, N_SSM_GROUPS, SSM_STATE), f32),
            jax.ShapeDtypeStruct((nb, CONV_K - 1, CONV_WIDTH), f32),
            jax.ShapeDtypeStruct((n_dec, SSM_WIDTH), f32),
            jax.ShapeDtypeStruct((n_dec, CONV_WIDTH), f32),
            jax.ShapeDtypeStruct((n_dec, CONV_WIDTH), f32),
        ],
        scratch_shapes=[
            *[pltpu.VMEM((SLABS_PER_BLK, prow, LANES), f32) for _ in range(N_BLK)],
            pltpu.VMEM((N_BLK, nb * SCAN_T, LANES), f32),
            pltpu.VMEM((N_BLK, nb * SCAN_T, LANES), f32),
            pltpu.VMEM((prow, 2 * SSM_WIDTH), f32),
            pltpu.VMEM((prow, SSM_WIDTH), f32),
            pltpu.VMEM((prow, SSM_WIDTH), f32),
            pltpu.VMEM((N_BLK, SUBLANES + prow, LANES), f32),
            pltpu.VMEM((N_BLK, SUBLANES, LANES), f32),
            pltpu.VMEM((nb, SCAN_T + SUBLANES, CONV_WIDTH), f32),
            pltpu.VMEM((nb, SCAN_T, CONV_WIDTH), f32),
            pltpu.VMEM((2 * N_TILE, SUBLANES, LANES), f32),
            pltpu.VMEM((D_MODEL, D_FF), bf),
            pltpu.VMEM((D_MODEL, D_FF), bf),
            pltpu.VMEM((D_FF, D_MODEL), bf),
            pltpu.VMEM((D_MODEL, IN_PROJ_WIDTH), bf),
            pltpu.VMEM((nb * SCAN_T, D_FF), bf),
        ],
        compiler_params=pltpu.CompilerParams(
            dimension_semantics=("arbitrary",), vmem_limit_bytes=VMEM_LIMIT),
        name="ffn1_mixer",
    )(x, xd, n1, wg, wu, wd, nm, win, a2_re, a2_im, wb, wct, wdm, d, wglu, bglu, cw)


def _ffn2_rows(h, mix, weights, a_ref):
    wout_ref, n2_ref, wg_ref, wu_ref, wd_ref, nf_ref = weights
    h = h + jnp.dot(mix, wout_ref[...], preferred_element_type=jnp.float32)
    hn = _rms(h, n2_ref[...]).astype(jnp.bfloat16)
    h = h + 0.5 * _swiglu_staged(hn, wg_ref, wu_ref, wd_ref, a_ref)
    return _rms(h, nf_ref[...])


def _ffn2_kernel(h_ref, mix_ref, hd0_ref, hd1_ref, mixd_ref, wout_c, wg_c, wu_c, wd_c, n2_ref, nf_ref,
                 y_ref, yd_ref, wout_ref, wg_ref, wu_ref, wd_ref, a0_ref, a1_ref):
    step = pl.program_id(0)
    last = pl.num_programs(0) - 1
    weights = (wout_ref, n2_ref, wg_ref, wu_ref, wd_ref, nf_ref)

    @pl.when(step < W2_STEPS)
    def _load_weights():
        _cast_weight_chunks(step, ((wout_c, wout_ref), (wg_c, wg_ref), (wu_c, wu_ref), (wd_c, wd_ref)))

    @pl.when((step >= W2_STEPS) & (step < last))
    def _prompt_tile():
        assert h_ref.shape[0] == 2 * FFN_SUB
        ra, rb = slice(0, FFN_SUB), slice(FFN_SUB, 2 * FFN_SUB)
        h_a = h_ref[ra, :] + jnp.dot(mix_ref[ra, :], wout_ref[...], preferred_element_type=jnp.float32)
        h_b = h_ref[rb, :] + jnp.dot(mix_ref[rb, :], wout_ref[...], preferred_element_type=jnp.float32)
        hn_a = _rms(h_a, n2_ref[...]).astype(jnp.bfloat16)
        _swiglu_stage(hn_a, wg_ref, wu_ref, a0_ref, 0, FFN_TF)
        hn_b = _rms(h_b, n2_ref[...]).astype(jnp.bfloat16)
        _swiglu_stage(hn_a, wg_ref, wu_ref, a0_ref, FFN_TF, D_FF)
        f_a = jnp.dot(a0_ref[...], wd_ref[...], preferred_element_type=jnp.float32)
        _swiglu_stage(hn_b, wg_ref, wu_ref, a1_ref, 0, FFN_TF)
        y_ref[ra, :] = _rms(h_a + 0.5 * f_a, nf_ref[...])
        _swiglu_stage(hn_b, wg_ref, wu_ref, a1_ref, FFN_TF, D_FF)
        f_b = jnp.dot(a1_ref[...], wd_ref[...], preferred_element_type=jnp.float32)
        y_ref[rb, :] = _rms(h_b + 0.5 * f_b, nf_ref[...])

    @pl.when(step == last)
    def _decode_rows():
        h_d = jnp.concatenate([hd0_ref[...], hd1_ref[...]], axis=0)
        yd_ref[...] = _ffn2_rows(h_d, mixd_ref[...], weights, a0_ref)


def _ffn2(h_all, mix, mixd, wout, n2, wg, wu, wd, nf):
    nb, seq, _ = mix.shape
    seq_all = h_all.shape[1]
    n_dec = mixd.shape[0]
    tail = seq_all - seq
    assert n_dec == 2 * tail
    per_b = seq // FFN_TM
    n_tiles = nb * per_b
    h_all = h_all.reshape(nb * seq_all, D_MODEL)
    mix = mix.reshape(nb * seq, D_MODEL)
    idx = lambda i: jnp.clip(i - W2_STEPS, 0, n_tiles - 1)
    tile = lambda i: (idx(i), 0)
    assert seq_all % SUBLANES == 0
    tile_all = lambda i: (pl.multiple_of((idx(i) // per_b) * seq_all + (idx(i) % per_b) * FFN_TM, SUBLANES), 0)
    yp, yd = pl.pallas_call(
        _ffn2_kernel,
        grid=(W2_STEPS + n_tiles + 1,),
        in_specs=[
            pl.BlockSpec((pl.Element(FFN_TM), pl.Element(D_MODEL)), tile_all),
            pl.BlockSpec((FFN_TM, D_MODEL), tile),
            pl.BlockSpec((pl.Element(tail), pl.Element(D_MODEL)), lambda i: (seq, 0),
                         pipeline_mode=pl.Buffered(1)),
            pl.BlockSpec((pl.Element(tail), pl.Element(D_MODEL)), lambda i: (seq_all + seq, 0),
                         pipeline_mode=pl.Buffered(1)),
            _whole(mixd.shape),
            _weight_chunk((D_MODEL // W2_STEPS, D_MODEL), W2_STEPS),
            _weight_chunk((D_MODEL // W2_STEPS, D_FF), W2_STEPS),
            _weight_chunk((D_MODEL // W2_STEPS, D_FF), W2_STEPS),
            _weight_chunk((D_FF // W2_STEPS, D_MODEL), W2_STEPS),
            _whole((1, D_MODEL)),
            _whole((1, D_MODEL)),
        ],
        out_specs=[
            pl.BlockSpec((FFN_TM, D_MODEL), tile),
            pl.BlockSpec((n_dec, None, D_MODEL), lambda i: (0, 0, 0)),
        ],
        out_shape=[
            jax.ShapeDtypeStruct((nb * seq, D_MODEL), jnp.float32),
            jax.ShapeDtypeStruct((n_dec, 1, D_MODEL), jnp.float32),
        ],
        scratch_shapes=[
            pltpu.VMEM((D_MODEL, D_MODEL), jnp.bfloat16),
            pltpu.VMEM((D_MODEL, D_FF), jnp.bfloat16),
            pltpu.VMEM((D_MODEL, D_FF), jnp.bfloat16),
            pltpu.VMEM((D_FF, D_MODEL), jnp.bfloat16),
            pltpu.VMEM((FFN_SUB, D_FF), jnp.bfloat16),
            pltpu.VMEM((FFN_SUB, D_FF), jnp.bfloat16),
        ],
        compiler_params=pltpu.CompilerParams(
            dimension_semantics=("arbitrary",), vmem_limit_bytes=VMEM_LIMIT),
        name="outproj_ffn2",
    )(h_all, mix, h_all, h_all, mixd, wout, wg, wu, wd, n2, nf)
    return yp.reshape(nb, seq, D_MODEL), yd


def _mixer_step_kernel(u_ref, v_ref, g_ref, hre_ref, him_ref, buf_ref, are_ref, aim_ref, bb_ref, wct_ref,
                       d_ref, wglu_ref, bglu_ref, cw_ref, mix_ref, sre_ref, sim_ref, nconv_ref):
    u = u_ref[...]
    u_bf = u.astype(jnp.bfloat16)
    ys = []
    for k in range(N_BLK):
        bu = jnp.dot(u_bf[:, k * LANES:(k + 1) * LANES], bb_ref[k], preferred_element_type=jnp.float32)
        st = slice(k * BLK_STATE, (k + 1) * BLK_STATE)
        a_re = are_ref[k:k + 1, :]
        a_im = aim_ref[k:k + 1, :]
        h_re = hre_ref[st, :].T
        h_im = him_ref[st, :].T
        x_re = (a_re * h_re + bu[:, :BLK_STATE]) - a_im * h_im
        x_im = (a_re * h_im + bu[:, BLK_STATE:]) + a_im * h_re
        sre_ref[st, :] = x_re.T
        sim_ref[st, :] = x_im.T
        x = jnp.concatenate([x_re, x_im], axis=1).astype(jnp.bfloat16)
        ys.append(lax.dot_general(x, wct_ref[k], _NT, preferred_element_type=jnp.float32))
    y = jnp.concatenate(ys, axis=1)
    mix_ref[:, 0:SSM_WIDTH] = _glu_tail(y, u, d_ref, wglu_ref, bglu_ref).astype(mix_ref.dtype)

    v = v_ref[...]
    buf1 = buf_ref[:, 1, :]
    z = cw_ref[0] * buf_ref[:, 0, :] + cw_ref[1] * buf1 + cw_ref[2] * v
    mix_ref[:, SSM_WIDTH:] = (g_ref[...] * z).astype(mix_ref.dtype)
    nconv_ref[:, 0, :] = buf1
    nconv_ref[:, 1, :] = v


def _mixer_step(u, v, g, h_re, h_im, conv_state, a_re, a_im, wb, wct, d, wglu, bglu, cw):
    nb = u.shape[0]
    conv_spec = pl.BlockSpec((None, nb, CONV_K - 1, CONV_WIDTH), lambda i: (0, 0, 0, 0))
    half = (N_BLK, LANES, 2 * BLK_STATE)
    bb_spec = pl.BlockSpec(half, lambda i: (0, 1, 0), pipeline_mode=pl.Buffered(1))
    c_spec = pl.BlockSpec(half, lambda i: (0, 0, 0), pipeline_mode=pl.Buffered(1))
    rows = (u, v, g, h_re, h_im)
    tail = (d, wglu, bglu, cw)
    return pl.pallas_call(
        _mixer_step_kernel,
        grid=(1,),
        in_specs=([_whole(a.shape) for a in rows] + [conv_spec] + [_whole(a_re.shape), _whole(a_im.shape)]
                  + [bb_spec, c_spec] + [_whole(a.shape) for a in tail]),
        out_specs=[
            pl.BlockSpec((nb, D_MODEL), lambda i: (0, 0)),
            pl.BlockSpec((N_STATE, nb), lambda i: (0, 0)),
            pl.BlockSpec((N_STATE, nb), lambda i: (0, 0)),
            conv_spec,
        ],
        out_shape=[
            jax.ShapeDtypeStruct((nb, D_MODEL), jnp.bfloat16),
            jax.ShapeDtypeStruct((N_STATE, nb), jnp.float32),
            jax.ShapeDtypeStruct((N_STATE, nb), jnp.float32),
            jax.ShapeDtypeStruct(conv_state.shape, jnp.float32),
        ],
        compiler_params=pltpu.CompilerParams(
            dimension_semantics=("arbitrary",), vmem_limit_bytes=VMEM_LIMIT),
        name="mixer_step",
    )(*rows, conv_state, a_re, a_im, wb, wct, *tail)


def _ssm_layout(lam_re, lam_im, log_dt, b_re, b_im, c_re, c_im):
    gpb = N_SSM_GROUPS // N_BLK

    def b_rows(b):
        return b.reshape(N_BLK, gpb, SSM_STATE, SSM_GROUP).transpose(0, 1, 3, 2).reshape(N_BLK, LANES, SSM_STATE)

    def c_rows(c):
        return c.reshape(N_BLK, LANES, SSM_STATE)

    return lam_re, lam_im, log_dt, b_rows(b_re), b_rows(b_im), c_rows(c_re), c_rows(c_im)


def kernel(x_prompt, x_sample, state_ssm_re, state_ssm_im, state_conv, ffn1_norm, ffn1_w_gate, ffn1_w_up, ffn1_w_down, mix_norm, w_in, ssm_lambda_re, ssm_lambda_im, ssm_log_dt, ssm_b_re, ssm_b_im, ssm_c_re, ssm_c_im, ssm_d, ssm_w_glu, ssm_b_glu, conv_w, w_out, ffn2_norm, ffn2_w_gate, ffn2_w_up, ffn2_w_down, final_norm):
    assert ffn1_norm.shape[0] == 1, "single-layer trunk"
    n_prompt, seq, _ = x_prompt.shape
    n_dec, dec_seq, _ = x_sample.shape
    assert dec_seq == 1 and n_prompt == SUBLANES and seq % FFN_TM == 0 and seq % SCAN_T == 0
    assert n_dec % SCAN_T == 0 and n_dec <= n_prompt * SCAN_T and FFN_TM % FFN_SUB == 0

    n1 = ffn1_norm[0][None, :]
    nm = mix_norm[0][None, :]
    n2 = ffn2_norm[0][None, :]
    nf = final_norm[None, :]
    wglu = ssm_w_glu[0]
    layout = _ssm_layout(ssm_lambda_re[0], ssm_lambda_im[0], ssm_log_dt, ssm_b_re[0], ssm_b_im[0],
                         ssm_c_re[0], ssm_c_im[0])
    a_re, a_im, a2_re, a2_im, wb, wct, wdm = _discretize(*layout)
    d = ssm_d[0][None, :]
    bglu = ssm_b_glu[0][None, :]
    cw = conv_w.transpose(1, 0, 2)

    h_all, mixp, p_re, p_im, p_conv, ud, vd, gd = _ffn1_mixer(
        x_prompt, x_sample, n1, ffn1_w_gate[0], ffn1_w_up[0], ffn1_w_down[0], nm,
        w_in[0], a2_re, a2_im, wb, wct, wdm, d, wglu, bglu, cw)

    s_re0 = state_ssm_re[0].transpose(1, 2, 0).reshape(N_STATE, n_dec)
    s_im0 = state_ssm_im[0].transpose(1, 2, 0).reshape(N_STATE, n_dec)
    mixd, s_re, s_im, s_conv = _mixer_step(ud, vd, gd, s_re0, s_im0, state_conv, a_re, a_im, wb, wct,
                                           d, wglu, bglu, cw)

    yp, yd = _ffn2(h_all, mixp, mixd, w_out[0], n2, ffn2_w_gate[0], ffn2_w_up[0], ffn2_w_down[0], nf)

    gs = (N_SSM_GROUPS, SSM_STATE)
    return (yp,
            yd,
            p_re[None],
            p_im[None],
            p_conv[None],
            s_re.reshape(*gs, n_dec).transpose(2, 0, 1)[None],
            s_im.reshape(*gs, n_dec).transpose(2, 0, 1)[None],
            s_conv)
```

```python
import jax
import jax.numpy as jnp
from jax import lax
from jax.experimental import pallas as pl
from jax.experimental.pallas import tpu as pltpu

D_MODEL = 1024
D_FF = 2816
SSM_WIDTH = 512
CONV_WIDTH = 512
SSM_GROUP = 16
N_SSM_GROUPS = 32
SSM_STATE = 64
CONV_K = 3
IN_PROJ_WIDTH = SSM_WIDTH + 3 * CONV_WIDTH
EPS = 1e-6

LANES = 128
SUBLANES = 8
N_STATE = N_SSM_GROUPS * SSM_STATE
N_BLK = SSM_WIDTH // LANES
BLK_STATE = N_STATE // N_BLK
SLABS_PER_BLK = 2 * BLK_STATE // LANES
HALF = SLABS_PER_BLK // 2
N_TILE = N_BLK * HALF

FFN_TM = 1024
FFN_SUB = 512
W1_STEPS = 1
W2_STEPS = 1
W1_CHUNKS = 16
W2_CHUNKS = 8
FFN_TF = 256
SCAN_T = 64
N_PAIR = SCAN_T // 2
PAIR_PITCH = N_PAIR + 4
VMEM_LIMIT = 62 * 1024 * 1024

_NT = (((1,), (1,)), ((), ()))


def _rms(x, g):
    r = lax.rsqrt(jnp.mean(x * x, axis=-1, keepdims=True) + EPS)
    return x * r * g


def _bdot(a, b):
    return jnp.dot(a.astype(jnp.bfloat16), b.astype(jnp.bfloat16), preferred_element_type=jnp.float32)


def _swiglu_stage(xn, wg_ref, wu_ref, a_ref, lo, hi):
    rows = xn.shape[0]
    for j in range(lo, hi, FFN_TF):
        g = jnp.dot(xn, wg_ref[:, j:j + FFN_TF], preferred_element_type=jnp.float32)
        u = jnp.dot(xn, wu_ref[:, j:j + FFN_TF], preferred_element_type=jnp.float32)
        a_ref[0:rows, j:j + FFN_TF] = ((g * jax.nn.sigmoid(g)) * u).astype(a_ref.dtype)


def _swiglu_staged(xn, wg_ref, wu_ref, wd_ref, a_ref):
    rows = xn.shape[0]
    _swiglu_stage(xn, wg_ref, wu_ref, a_ref, 0, D_FF)
    return jnp.dot(a_ref[0:rows, :], wd_ref[...], preferred_element_type=jnp.float32)


def _stream_cast(triples, sem, n_chunks):
    def copy(w, c, slot):
        src, buf, _ = triples[w]
        n = buf.shape[1]
        return pltpu.make_async_copy(src.at[pl.ds(pl.multiple_of(c * n, SUBLANES), n), :], buf.at[slot],
                                     sem.at[2 * w + slot])

    for w in range(len(triples)):
        copy(w, 0, 0).start()

    def body(c, carry):
        slot = lax.rem(c, 2)

        @pl.when(c + 1 < n_chunks)
        def _prefetch():
            for w in range(len(triples)):
                copy(w, c + 1, 1 - slot).start()

        for w, (_, buf, dst) in enumerate(triples):
            copy(w, c, slot).wait()
            n = buf.shape[1]
            dst[pl.ds(pl.multiple_of(c * n, SUBLANES), n), :] = buf[slot].astype(dst.dtype)
        return carry

    lax.fori_loop(0, n_chunks, body, 0)


def _staging(rows, cols):
    return pltpu.VMEM((2, rows, cols), jnp.float32)


def _whole(shape):
    return pl.BlockSpec(shape, lambda i: (0,) * len(shape), pipeline_mode=pl.Buffered(1))


_IN_HBM = pl.BlockSpec(memory_space=pl.ANY)


def _glu_tail(y, u, d_ref, wglu_ref, bglu_ref):
    y = y + d_ref[...] * u
    z = jax.nn.gelu(y)
    gate = jax.nn.sigmoid(_bdot(z, wglu_ref[...]) + bglu_ref[...])
    return z * gate


def _discretize_kernel(lr_ref, li_ref, ldt_ref, bre_ref, bim_ref, ctre_ref, ctim_ref,
                       are_ref, aim_ref, a2re_ref, a2im_ref, wb_ref, wct_ref, wdm_ref):
    row_group = lax.broadcasted_iota(jnp.int32, (LANES, BLK_STATE), 0) // SSM_GROUP
    col_group = lax.broadcasted_iota(jnp.int32, (LANES, BLK_STATE), 1) // SSM_STATE
    on_diagonal = row_group == col_group

    def blocks(m):
        pair = jnp.concatenate([m, m], axis=1)
        return jnp.where(on_diagonal, jnp.concatenate([pair] * (BLK_STATE // LANES), axis=1), 0.0)

    gpb = N_SSM_GROUPS // N_BLK

    def state_rows(ref):
        return jnp.concatenate(
            [jnp.concatenate([ref[k * gpb + g:k * gpb + g + 1, :] for g in range(gpb)], axis=1)
             for k in range(N_BLK)], axis=0)

    def per_state(row_ref):
        vals = row_ref[...]
        col_group = lax.broadcasted_iota(jnp.int32, (1, BLK_STATE), 1) // SSM_STATE
        rows = []
        for k in range(N_BLK):
            row = jnp.zeros((1, BLK_STATE), jnp.float32)
            for g in range(gpb):
                row = jnp.where(col_group == g, vals[:, k * gpb + g:k * gpb + g + 1], row)
            rows.append(row)
        return jnp.concatenate(rows, axis=0)

    lr, li = state_rows(lr_ref), state_rows(li_ref)
    dt = jnp.exp(per_state(ldt_ref))
    mag = jnp.exp(lr * dt)
    ab_re = mag * jnp.cos(li * dt)
    ab_im = mag * jnp.sin(li * dt)
    den = lr * lr + li * li
    nr = ab_re - 1.0
    ni = ab_im
    coef_re = (nr * lr + ni * li) / den
    coef_im = (ni * lr - nr * li) / den
    are_ref[...] = ab_re
    aim_ref[...] = ab_im
    a2_re = ab_re * ab_re - ab_im * ab_im
    a2_im = 2.0 * (ab_re * ab_im)
    bf = jnp.bfloat16
    for k in range(N_BLK):
        ar, ai = ab_re[k:k + 1, :], ab_im[k:k + 1, :]
        for j in range(HALF):
            lanes = slice(j * LANES, (j + 1) * LANES)
            a2re_ref[k * HALF + j] = jnp.broadcast_to(a2_re[k:k + 1, lanes], (SUBLANES, LANES))
            a2im_ref[k * HALF + j] = jnp.broadcast_to(a2_im[k:k + 1, lanes], (SUBLANES, LANES))
        cr, ci = coef_re[k:k + 1, :], coef_im[k:k + 1, :]
        bre, bim = blocks(bre_ref[k]), blocks(bim_ref[k])
        bb_re = cr * bre - ci * bim
        bb_im = cr * bim + ci * bre
        wb_ref[k, 0:LANES, 0:BLK_STATE] = (ar * bb_re - ai * bb_im).astype(bf)
        wb_ref[k, 0:LANES, BLK_STATE:] = (ar * bb_im + ai * bb_re).astype(bf)
        wb_ref[k, LANES:, 0:BLK_STATE] = bb_re.astype(bf)
        wb_ref[k, LANES:, BLK_STATE:] = bb_im.astype(bf)
        ctre, ctim = blocks(ctre_ref[k]), blocks(ctim_ref[k])
        wct_ref[k, 0:LANES, 0:BLK_STATE] = ctre.astype(bf)
        wct_ref[k, 0:LANES, BLK_STATE:] = (-ctim).astype(bf)
        wct_ref[k, LANES:, 0:BLK_STATE] = (ctre * ar - ctim * ai).astype(bf)
        wct_ref[k, LANES:, BLK_STATE:] = (-(ctre * ai + ctim * ar)).astype(bf)
        wdm = (lax.dot_general(bb_re, ctre, _NT, precision=lax.Precision.HIGHEST,
                               preferred_element_type=jnp.float32)
               - lax.dot_general(bb_im, ctim, _NT, precision=lax.Precision.HIGHEST,
                                 preferred_element_type=jnp.float32))
        wdm_ref[k] = wdm.astype(bf)


def _discretize(lr, li, ldt, bre, bim, ctre, ctim):
    f32, bf = jnp.float32, jnp.bfloat16
    return pl.pallas_call(
        _discretize_kernel,
        out_shape=[
            jax.ShapeDtypeStruct((N_BLK, BLK_STATE), f32),
            jax.ShapeDtypeStruct((N_BLK, BLK_STATE), f32),
            jax.ShapeDtypeStruct((N_TILE, SUBLANES, LANES), f32),
            jax.ShapeDtypeStruct((N_TILE, SUBLANES, LANES), f32),
            jax.ShapeDtypeStruct((N_BLK, 2 * LANES, 2 * BLK_STATE), bf),
            jax.ShapeDtypeStruct((N_BLK, 2 * LANES, 2 * BLK_STATE), bf),
            jax.ShapeDtypeStruct((N_BLK, LANES, LANES), bf),
        ],
        name="ssm_discretize",
    )(lr, li, ldt, bre, bim, ctre, ctim)


def _ffn1_mixer_kernel(x_ref, xd_ref, n1_ref, wg_c, wu_c, wd_c, nm_ref, win_c,
                       a2re_ref, a2im_ref, wb_ref, wct_ref, wdm_ref, d_ref, wglu_ref, bglu_ref, cw_ref,
                       h_ref, mix_ref, sre_ref, sim_ref, nconv_ref, ud_ref, vd_ref, gd_ref,
                       xs0_ref, xs1_ref, xs2_ref, xs3_ref, us_ref, ys_ref, l_ref, yo_ref, ye_ref, e_ref,
                       ecarry_ref, v_ref, g_ref, state_ref, wg_ref, wu_ref, wd_ref, win_ref, a_ref, xdd_ref,
                       wg_s, wu_s, wd_s, win_s, cast_sem):
    xs_refs = (xs0_ref, xs1_ref, xs2_ref, xs3_ref)
    step = pl.program_id(0)
    last = pl.num_programs(0) - 1
    nb, t_len, _ = x_ref.shape
    n_dec = xd_ref.shape[0]
    prow = nb * PAIR_PITCH
    tiles = [(k, j) for k in range(N_BLK) for j in range(HALF)]

    @pl.when(step < W1_STEPS)
    def _load_weights():
        _stream_cast(((wg_c, wg_s, wg_ref), (wu_c, wu_s, wu_ref), (wd_c, wd_s, wd_ref),
                      (win_c, win_s, win_ref)), cast_sem, W1_CHUNKS)

    @pl.when(step == 0)
    def _init():
        for ref in (state_ref, us_ref, l_ref, e_ref, ecarry_ref, v_ref, g_ref):
            ref[...] = jnp.zeros_like(ref)
        xdd_ref[...] = xd_ref[...]

    @pl.when(step >= W1_STEPS)
    def _main():
        carry = {}

        def build_lhs():
            for b in range(nb):
                for k in range(N_BLK):
                    for par in range(2):
                        col = (2 * k + par) * LANES
                        l_ref[pl.ds(b * PAIR_PITCH, N_PAIR), col:col + LANES] = (
                            us_ref[k, pl.ds(b * t_len + par, N_PAIR, stride=2), :])

        def b_proj(ks):
            for k in ks:
                lhs = l_ref[:, 2 * k * LANES:2 * (k + 1) * LANES].astype(jnp.bfloat16)
                v = jnp.dot(lhs, wb_ref[k], preferred_element_type=jnp.float32)
                for j in range(SLABS_PER_BLK):
                    xs_refs[k][j] = v[:, j * LANES:(j + 1) * LANES]

        def scan(r0, r1):
            if r0 == 0:
                carry["re"] = [state_ref[k * SLABS_PER_BLK + j] for k, j in tiles]
                carry["im"] = [state_ref[k * SLABS_PER_BLK + HALF + j] for k, j in tiles]
            xr, xi = carry["re"], carry["im"]
            for r in range(r0, r1):
                rows = pl.ds(r, SUBLANES, stride=PAIR_PITCH)
                for n, (k, j) in enumerate(tiles):
                    a_re = a2re_ref[k * HALF + j]
                    a_im = a2im_ref[k * HALF + j]
                    nr = (a_re * xr[n] + xs_refs[k][j, rows, :]) - a_im * xi[n]
                    ni = (a_re * xi[n] + xs_refs[k][HALF + j, rows, :]) + a_im * xr[n]
                    xs_refs[k][j, rows, :] = nr
                    xs_refs[k][HALF + j, rows, :] = ni
                    xr[n], xi[n] = nr, ni
            if r1 == N_PAIR:
                for n, (k, j) in enumerate(tiles):
                    state_ref[k * SLABS_PER_BLK + j] = xr[n]
                    state_ref[k * SLABS_PER_BLK + HALF + j] = xi[n]

        def c_proj(ks):
            for k in ks:
                lanes = slice(k * LANES, (k + 1) * LANES)
                x = jnp.concatenate([xs_refs[k][j] for j in range(SLABS_PER_BLK)], axis=1)
                ce = lax.dot_general(x.astype(jnp.bfloat16), wct_ref[k], _NT,
                                     preferred_element_type=jnp.float32)
                yo_ref[:, lanes] = ce[:, 0:LANES]
                e_ref[k, SUBLANES:SUBLANES + prow, :] = ce[:, LANES:]
                nxt = e_ref[k, pl.ds(SUBLANES + N_PAIR - 1, nb, stride=PAIR_PITCH), :]
                e_ref[k, pl.ds(SUBLANES - 1, nb, stride=PAIR_PITCH), :] = ecarry_ref[k]
                ecarry_ref[k] = nxt
                u_even = l_ref[:, 2 * k * LANES:(2 * k + 1) * LANES].astype(jnp.bfloat16)
                ye_ref[:, lanes] = (e_ref[k, SUBLANES - 1:SUBLANES - 1 + prow, :]
                                    + jnp.dot(u_even, wdm_ref[k], preferred_element_type=jnp.float32))

        def glu_emit():
            for b in range(nb):
                for k in range(N_BLK):
                    lanes = slice(k * LANES, (k + 1) * LANES)
                    for par, src in ((0, ye_ref), (1, yo_ref)):
                        ys_ref[k, pl.ds(b * t_len + par, N_PAIR, stride=2), :] = (
                            src[pl.ds(b * PAIR_PITCH, N_PAIR), lanes])
            y = jnp.concatenate([ys_ref[k] for k in range(N_BLK)], axis=1)
            u = jnp.concatenate([us_ref[k] for k in range(N_BLK)], axis=1)
            s_out = _glu_tail(y, u, d_ref, wglu_ref, bglu_ref)
            mix_ref[:, :, 0:SSM_WIDTH] = s_out.reshape(nb, t_len, SSM_WIDTH).astype(mix_ref.dtype)

        def conv():
            v = v_ref[:, SUBLANES:SUBLANES + t_len, :]
            z = cw_ref[0] * v_ref[:, SUBLANES - 2:SUBLANES - 2 + t_len, :]
            z = z + cw_ref[1] * v_ref[:, SUBLANES - 1:SUBLANES - 1 + t_len, :]
            z = z + cw_ref[2] * v
            mix_ref[:, :, SSM_WIDTH:] = (g_ref[...] * z).astype(mix_ref.dtype)
            v_ref[:, 0:SUBLANES, :] = v_ref[:, t_len:t_len + SUBLANES, :]

        q = N_PAIR // 4
        pieces = [
            lambda: (conv(), build_lhs(), b_proj((0,))), lambda: b_proj((1,)), lambda: b_proj((2,)),
            lambda: b_proj((3,)),
            lambda: scan(0, q), lambda: scan(q, 2 * q), lambda: scan(2 * q, 3 * q), lambda: scan(3 * q, N_PAIR),
            lambda: c_proj((0, 1)), lambda: c_proj((2, 3)),
            glu_emit,
        ]
        assert len(pieces) <= D_FF // FFN_TF

        db = n_dec // t_len
        x_top = jnp.where(step == last, xdd_ref[...], x_ref[0:db].reshape(n_dec, D_MODEL))
        x = jnp.concatenate([x_top, x_ref[db:].reshape((nb - db) * t_len, D_MODEL)], axis=0)
        xn = _rms(x, n1_ref[...]).astype(jnp.bfloat16)
        for c, j in enumerate(range(0, D_FF, FFN_TF)):
            _swiglu_stage(xn, wg_ref, wu_ref, a_ref, j, j + FFN_TF)
            if c < len(pieces):
                pieces[c]()
        h = x + 0.5 * jnp.dot(a_ref[...], wd_ref[...], preferred_element_type=jnp.float32)
        hn = _rms(h, nm_ref[...]).astype(jnp.bfloat16)
        p = jnp.dot(hn, win_ref[...], preferred_element_type=jnp.float32)
        h_ref[...] = h.reshape(nb, t_len, D_MODEL)
        v_new = p[:, SSM_WIDTH + 2 * CONV_WIDTH:] * p[:, SSM_WIDTH:SSM_WIDTH + CONV_WIDTH]
        v_ref[:, SUBLANES:SUBLANES + t_len, :] = v_new.reshape(nb, t_len, CONV_WIDTH)
        g_ref[...] = p[:, SSM_WIDTH + CONV_WIDTH:SSM_WIDTH + 2 * CONV_WIDTH].reshape(nb, t_len, CONV_WIDTH)
        for k in range(N_BLK):
            us_ref[k] = p[:, k * LANES:(k + 1) * LANES]

    @pl.when(step == last)
    def _final():
        for n, (k, j) in enumerate(tiles):
            for out_ref, slab in ((sre_ref, k * SLABS_PER_BLK + j), (sim_ref, k * SLABS_PER_BLK + HALF + j)):
                tile = state_ref[slab]
                for half in range(LANES // SSM_STATE):
                    group = (k * BLK_STATE + j * LANES) // SSM_STATE + half
                    out_ref[:, group, :] = tile[:, half * SSM_STATE:(half + 1) * SSM_STATE]
        nconv_ref[...] = v_ref[:, SUBLANES - 2:SUBLANES, :]
        for k in range(N_BLK):
            ud_ref[:, k * LANES:(k + 1) * LANES] = us_ref[k, 0:n_dec, :]
        for b in range(n_dec // t_len):
            vd_ref[b * t_len:(b + 1) * t_len, :] = v_ref[b, SUBLANES:SUBLANES + t_len, :]
            gd_ref[b * t_len:(b + 1) * t_len, :] = g_ref[b]


def _ffn1_mixer(x, xd, n1, wg, wu, wd, nm, win, a2_re, a2_im, wb, wct, wdm, d, wglu, bglu, cw):
    nb, seq, _ = x.shape
    n_dec = xd.shape[0]
    n_chunk = seq // SCAN_T
    prow = nb * PAIR_PITCH
    f32, bf = jnp.float32, jnp.bfloat16
    main = lambda i: jnp.maximum(i - W1_STEPS, 0)
    return pl.pallas_call(
        _ffn1_mixer_kernel,
        grid=(W1_STEPS + n_chunk + 1,),
        in_specs=[
            pl.BlockSpec((nb, SCAN_T, D_MODEL), lambda i: (0, jnp.minimum(main(i), n_chunk - 1), 0)),
            pl.BlockSpec((n_dec, None, D_MODEL), lambda i: (0, 0, 0), pipeline_mode=pl.Buffered(1)),
            _whole((1, D_MODEL)),
            _IN_HBM,
            _IN_HBM,
            _IN_HBM,
            _whole((1, D_MODEL)),
            _IN_HBM,
            _whole((N_TILE, SUBLANES, LANES)),
            _whole((N_TILE, SUBLANES, LANES)),
            _whole((N_BLK, 2 * LANES, 2 * BLK_STATE)),
            _whole((N_BLK, 2 * LANES, 2 * BLK_STATE)),
            _whole((N_BLK, LANES, LANES)),
            _whole((1, SSM_WIDTH)),
            _whole((SSM_WIDTH, SSM_WIDTH)),
            _whole((1, SSM_WIDTH)),
            _whole((CONV_K, 1, CONV_WIDTH)),
        ],
        out_specs=[
            pl.BlockSpec((nb, SCAN_T, D_MODEL), lambda i: (0, main(i), 0)),
            pl.BlockSpec((nb, SCAN_T, D_MODEL), lambda i: (0, jnp.clip(main(i) - 1, 0, n_chunk - 1), 0)),
            pl.BlockSpec((nb, N_SSM_GROUPS, SSM_STATE), lambda i: (0, 0, 0)),
            pl.BlockSpec((nb, N_SSM_GROUPS, SSM_STATE), lambda i: (0, 0, 0)),
            pl.BlockSpec((nb, CONV_K - 1, CONV_WIDTH), lambda i: (0, 0, 0)),
            pl.BlockSpec((n_dec, SSM_WIDTH), lambda i: (0, 0)),
            pl.BlockSpec((n_dec, CONV_WIDTH), lambda i: (0, 0)),
            pl.BlockSpec((n_dec, CONV_WIDTH), lambda i: (0, 0)),
        ],
        out_shape=[
            jax.ShapeDtypeStruct((nb, seq + SCAN_T, D_MODEL), f32),
            jax.ShapeDtypeStruct((nb, seq, D_MODEL), bf),
            jax.ShapeDtypeStruct((nb, N_SSM_GROUPS, SSM_STATE), f32),
            jax.ShapeDtypeStruct((nb, N_SSM_GROUPS, SSM_STATE), f32),
            jax.ShapeDtypeStruct((nb, CONV_K - 1, CONV_WIDTH), f32),
            jax.ShapeDtypeStruct((n_dec, SSM_WIDTH), f32),
            jax.ShapeDtypeStruct((n_dec, CONV_WIDTH), f32),
            jax.ShapeDtypeStruct((n_dec, CONV_WIDTH), f32),
        ],
        scratch_shapes=[
            *[pltpu.VMEM((SLABS_PER_BLK, prow, LANES), f32) for _ in range(N_BLK)],
            pltpu.VMEM((N_BLK, nb * SCAN_T, LANES), f32),
            pltpu.VMEM((N_BLK, nb * SCAN_T, LANES), f32),
            pltpu.VMEM((prow, 2 * SSM_WIDTH), f32),
            pltpu.VMEM((prow, SSM_WIDTH), f32),
            pltpu.VMEM((prow, SSM_WIDTH), f32),
            pltpu.VMEM((N_BLK, SUBLANES + prow, LANES), f32),
            pltpu.VMEM((N_BLK, SUBLANES, LANES), f32),
            pltpu.VMEM((nb, SCAN_T + SUBLANES, CONV_WIDTH), f32),
            pltpu.VMEM((nb, SCAN_T, CONV_WIDTH), f32),
            pltpu.VMEM((2 * N_TILE, SUBLANES, LANES), f32),
            pltpu.VMEM((D_MODEL, D_FF), bf),
            pltpu.VMEM((D_MODEL, D_FF), bf),
            pltpu.VMEM((D_FF, D_MODEL), bf),
            pltpu.VMEM((D_MODEL, IN_PROJ_WIDTH), bf),
            pltpu.VMEM((nb * SCAN_T, D_FF), bf),
            pltpu.VMEM((n_dec, D_MODEL), f32),
            _staging(D_MODEL // W1_CHUNKS, D_FF),
            _staging(D_MODEL // W1_CHUNKS, D_FF),
            _staging(D_FF // W1_CHUNKS, D_MODEL),
            _staging(D_MODEL // W1_CHUNKS, IN_PROJ_WIDTH),
            pltpu.SemaphoreType.DMA((8,)),
        ],
        compiler_params=pltpu.CompilerParams(
            dimension_semantics=("arbitrary",), vmem_limit_bytes=VMEM_LIMIT),
        name="ffn1_mixer",
    )(x, xd, n1, wg, wu, wd, nm, win, a2_re, a2_im, wb, wct, wdm, d, wglu, bglu, cw)


def _ffn2_rows(h, mix, weights, a_ref):
    wout_ref, n2_ref, wg_ref, wu_ref, wd_ref, nf_ref = weights
    h = h + jnp.dot(mix, wout_ref[...], preferred_element_type=jnp.float32)
    hn = _rms(h, n2_ref[...]).astype(jnp.bfloat16)
    h = h + 0.5 * _swiglu_staged(hn, wg_ref, wu_ref, wd_ref, a_ref)
    return _rms(h, nf_ref[...])


def _ffn2_kernel(h_ref, mix_ref, hd0_ref, hd1_ref, mixd_ref, wout_c, wg_c, wu_c, wd_c, n2_ref, nf_ref,
                 y_ref, yd_ref, wout_ref, wg_ref, wu_ref, wd_ref, a0_ref, a1_ref,
                 wout_s, wg_s, wu_s, wd_s, cast_sem):
    step = pl.program_id(0)
    last = pl.num_programs(0) - 1
    weights = (wout_ref, n2_ref, wg_ref, wu_ref, wd_ref, nf_ref)

    @pl.when(step < W2_STEPS)
    def _load_weights():
        _stream_cast(((wout_c, wout_s, wout_ref), (wg_c, wg_s, wg_ref), (wu_c, wu_s, wu_ref),
                      (wd_c, wd_s, wd_ref)), cast_sem, W2_CHUNKS)

    @pl.when((step >= W2_STEPS) & (step < last))
    def _prompt_tile():
        assert h_ref.shape[0] == 2 * FFN_SUB
        ra, rb = slice(0, FFN_SUB), slice(FFN_SUB, 2 * FFN_SUB)
        h_a = h_ref[ra, :] + jnp.dot(mix_ref[ra, :], wout_ref[...], preferred_element_type=jnp.float32)
        h_b = h_ref[rb, :] + jnp.dot(mix_ref[rb, :], wout_ref[...], preferred_element_type=jnp.float32)
        hn_a = _rms(h_a, n2_ref[...]).astype(jnp.bfloat16)
        _swiglu_stage(hn_a, wg_ref, wu_ref, a0_ref, 0, FFN_TF)
        hn_b = _rms(h_b, n2_ref[...]).astype(jnp.bfloat16)
        _swiglu_stage(hn_a, wg_ref, wu_ref, a0_ref, FFN_TF, D_FF)
        f_a = jnp.dot(a0_ref[...], wd_ref[...], preferred_element_type=jnp.float32)
        _swiglu_stage(hn_b, wg_ref, wu_ref, a1_ref, 0, FFN_TF)
        y_ref[ra, :] = _rms(h_a + 0.5 * f_a, nf_ref[...])
        _swiglu_stage(hn_b, wg_ref, wu_ref, a1_ref, FFN_TF, D_FF)
        f_b = jnp.dot(a1_ref[...], wd_ref[...], preferred_element_type=jnp.float32)
        y_ref[rb, :] = _rms(h_b + 0.5 * f_b, nf_ref[...])

    @pl.when(step == last)
    def _decode_rows():
        h_d = jnp.concatenate([hd0_ref[...], hd1_ref[...]], axis=0)
        yd_ref[...] = _ffn2_rows(h_d, mixd_ref[...], weights, a0_ref)


def _ffn2(h_all, mix, mixd, wout, n2, wg, wu, wd, nf):
    nb, seq, _ = mix.shape
    seq_all = h_all.shape[1]
    n_dec = mixd.shape[0]
    tail = seq_all - seq
    assert n_dec == 2 * tail
    per_b = seq // FFN_TM
    n_tiles = nb * per_b
    h_all = h_all.reshape(nb * seq_all, D_MODEL)
    mix = mix.reshape(nb * seq, D_MODEL)
    idx = lambda i: jnp.clip(i - W2_STEPS, 0, n_tiles - 1)
    tile = lambda i: (idx(i), 0)
    assert seq_all % SUBLANES == 0
    tile_all = lambda i: (pl.multiple_of((idx(i) // per_b) * seq_all + (idx(i) % per_b) * FFN_TM, SUBLANES), 0)
    yp, yd = pl.pallas_call(
        _ffn2_kernel,
        grid=(W2_STEPS + n_tiles + 1,),
        in_specs=[
            pl.BlockSpec((pl.Element(FFN_TM), pl.Element(D_MODEL)), tile_all),
            pl.BlockSpec((FFN_TM, D_MODEL), tile),
            pl.BlockSpec((pl.Element(tail), pl.Element(D_MODEL)), lambda i: (seq, 0),
                         pipeline_mode=pl.Buffered(1)),
            pl.BlockSpec((pl.Element(tail), pl.Element(D_MODEL)), lambda i: (seq_all + seq, 0),
                         pipeline_mode=pl.Buffered(1)),
            _whole(mixd.shape),
            _IN_HBM,
            _IN_HBM,
            _IN_HBM,
            _IN_HBM,
            _whole((1, D_MODEL)),
            _whole((1, D_MODEL)),
        ],
        out_specs=[
            pl.BlockSpec((FFN_TM, D_MODEL), tile),
            pl.BlockSpec((n_dec, None, D_MODEL), lambda i: (0, 0, 0)),
        ],
        out_shape=[
            jax.ShapeDtypeStruct((nb * seq, D_MODEL), jnp.float32),
            jax.ShapeDtypeStruct((n_dec, 1, D_MODEL), jnp.float32),
        ],
        scratch_shapes=[
            pltpu.VMEM((D_MODEL, D_MODEL), jnp.bfloat16),
            pltpu.VMEM((D_MODEL, D_FF), jnp.bfloat16),
            pltpu.VMEM((D_MODEL, D_FF), jnp.bfloat16),
            pltpu.VMEM((D_FF, D_MODEL), jnp.bfloat16),
            pltpu.VMEM((FFN_SUB, D_FF), jnp.bfloat16),
            pltpu.VMEM((FFN_SUB, D_FF), jnp.bfloat16),
            _staging(D_MODEL // W2_CHUNKS, D_MODEL),
            _staging(D_MODEL // W2_CHUNKS, D_FF),
            _staging(D_MODEL // W2_CHUNKS, D_FF),
            _staging(D_FF // W2_CHUNKS, D_MODEL),
            pltpu.SemaphoreType.DMA((8,)),
        ],
        compiler_params=pltpu.CompilerParams(
            dimension_semantics=("arbitrary",), vmem_limit_bytes=VMEM_LIMIT),
        name="outproj_ffn2",
    )(h_all, mix, h_all, h_all, mixd, wout, wg, wu, wd, n2, nf)
    return yp.reshape(nb, seq, D_MODEL), yd


def _mixer_step_kernel(u_ref, v_ref, g_ref, hre_ref, him_ref, buf_ref, are_ref, aim_ref, bb_ref, wct_ref,
                       d_ref, wglu_ref, bglu_ref, cw_ref, mix_ref, sre_ref, sim_ref, nconv_ref):
    u = u_ref[...]
    u_bf = u.astype(jnp.bfloat16)
    ys = []
    for k in range(N_BLK):
        bu = jnp.dot(u_bf[:, k * LANES:(k + 1) * LANES], bb_ref[k], preferred_element_type=jnp.float32)
        st = slice(k * BLK_STATE, (k + 1) * BLK_STATE)
        a_re = are_ref[k:k + 1, :]
        a_im = aim_ref[k:k + 1, :]
        h_re = hre_ref[st, :].T
        h_im = him_ref[st, :].T
        x_re = (a_re * h_re + bu[:, :BLK_STATE]) - a_im * h_im
        x_im = (a_re * h_im + bu[:, BLK_STATE:]) + a_im * h_re
        sre_ref[st, :] = x_re.T
        sim_ref[st, :] = x_im.T
        x = jnp.concatenate([x_re, x_im], axis=1).astype(jnp.bfloat16)
        ys.append(lax.dot_general(x, wct_ref[k], _NT, preferred_element_type=jnp.float32))
    y = jnp.concatenate(ys, axis=1)
    mix_ref[:, 0:SSM_WIDTH] = _glu_tail(y, u, d_ref, wglu_ref, bglu_ref).astype(mix_ref.dtype)

    v = v_ref[...]
    buf1 = buf_ref[:, 1, :]
    z = cw_ref[0] * buf_ref[:, 0, :] + cw_ref[1] * buf1 + cw_ref[2] * v
    mix_ref[:, SSM_WIDTH:] = (g_ref[...] * z).astype(mix_ref.dtype)
    nconv_ref[:, 0, :] = buf1
    nconv_ref[:, 1, :] = v


def _mixer_step(u, v, g, h_re, h_im, conv_state, a_re, a_im, wb, wct, d, wglu, bglu, cw):
    nb = u.shape[0]
    conv_spec = pl.BlockSpec((None, nb, CONV_K - 1, CONV_WIDTH), lambda i: (0, 0, 0, 0))
    half = (N_BLK, LANES, 2 * BLK_STATE)
    bb_spec = pl.BlockSpec(half, lambda i: (0, 1, 0), pipeline_mode=pl.Buffered(1))
    c_spec = pl.BlockSpec(half, lambda i: (0, 0, 0), pipeline_mode=pl.Buffered(1))
    rows = (u, v, g, h_re, h_im)
    tail = (d, wglu, bglu, cw)
    return pl.pallas_call(
        _mixer_step_kernel,
        grid=(1,),
        in_specs=([_whole(a.shape) for a in rows] + [conv_spec] + [_whole(a_re.shape), _whole(a_im.shape)]
                  + [bb_spec, c_spec] + [_whole(a.shape) for a in tail]),
        out_specs=[
            pl.BlockSpec((nb, D_MODEL), lambda i: (0, 0)),
            pl.BlockSpec((N_STATE, nb), lambda i: (0, 0)),
            pl.BlockSpec((N_STATE, nb), lambda i: (0, 0)),
            conv_spec,
        ],
        out_shape=[
            jax.ShapeDtypeStruct((nb, D_MODEL), jnp.bfloat16),
            jax.ShapeDtypeStruct((N_STATE, nb), jnp.float32),
            jax.ShapeDtypeStruct((N_STATE, nb), jnp.float32),
            jax.ShapeDtypeStruct(conv_state.shape, jnp.float32),
        ],
        compiler_params=pltpu.CompilerParams(
            dimension_semantics=("arbitrary",), vmem_limit_bytes=VMEM_LIMIT),
        name="mixer_step",
    )(*rows, conv_state, a_re, a_im, wb, wct, *tail)


def _ssm_layout(lam_re, lam_im, log_dt, b_re, b_im, c_re, c_im):
    gpb = N_SSM_GROUPS // N_BLK

    def b_rows(b):
        return b.reshape(N_BLK, gpb, SSM_STATE, SSM_GROUP).transpose(0, 1, 3, 2).reshape(N_BLK, LANES, SSM_STATE)

    def c_rows(c):
        return c.reshape(N_BLK, LANES, SSM_STATE)

    return lam_re, lam_im, log_dt, b_rows(b_re), b_rows(b_im), c_rows(c_re), c_rows(c_im)


def kernel(x_prompt, x_sample, state_ssm_re, state_ssm_im, state_conv, ffn1_norm, ffn1_w_gate, ffn1_w_up, ffn1_w_down, mix_norm, w_in, ssm_lambda_re, ssm_lambda_im, ssm_log_dt, ssm_b_re, ssm_b_im, ssm_c_re, ssm_c_im, ssm_d, ssm_w_glu, ssm_b_glu, conv_w, w_out, ffn2_norm, ffn2_w_gate, ffn2_w_up, ffn2_w_down, final_norm):
    assert ffn1_norm.shape[0] == 1, "single-layer trunk"
    n_prompt, seq, _ = x_prompt.shape
    n_dec, dec_seq, _ = x_sample.shape
    assert dec_seq == 1 and n_prompt == SUBLANES and seq % FFN_TM == 0 and seq % SCAN_T == 0
    assert n_dec % SCAN_T == 0 and n_dec <= n_prompt * SCAN_T and FFN_TM % FFN_SUB == 0

    n1 = ffn1_norm[0][None, :]
    nm = mix_norm[0][None, :]
    n2 = ffn2_norm[0][None, :]
    nf = final_norm[None, :]
    wglu = ssm_w_glu[0]
    layout = _ssm_layout(ssm_lambda_re[0], ssm_lambda_im[0], ssm_log_dt, ssm_b_re[0], ssm_b_im[0],
                         ssm_c_re[0], ssm_c_im[0])
    a_re, a_im, a2_re, a2_im, wb, wct, wdm = _discretize(*layout)
    d = ssm_d[0][None, :]
    bglu = ssm_b_glu[0][None, :]
    cw = conv_w.transpose(1, 0, 2)

    h_all, mixp, p_re, p_im, p_conv, ud, vd, gd = _ffn1_mixer(
        x_prompt, x_sample, n1, ffn1_w_gate[0], ffn1_w_up[0], ffn1_w_down[0], nm,
        w_in[0], a2_re, a2_im, wb, wct, wdm, d, wglu, bglu, cw)

    s_re0 = state_ssm_re[0].transpose(1, 2, 0).reshape(N_STATE, n_dec)
    s_im0 = state_ssm_im[0].transpose(1, 2, 0).reshape(N_STATE, n_dec)
    mixd, s_re, s_im, s_conv = _mixer_step(ud, vd, gd, s_re0, s_im0, state_conv, a_re, a_im, wb, wct,
                                           d, wglu, bglu, cw)

    yp, yd = _ffn2(h_all, mixp, mixd, w_out[0], n2, ffn2_w_gate[0], ffn2_w_up[0], ffn2_w_down[0], nf)

    gs = (N_SSM_GROUPS, SSM_STATE)
    return (yp,
            yd,
            p_re[None],
            p_im[None],
            p_conv[None],
            s_re.reshape(*gs, n_dec).transpose(2, 0, 1)[None],
            s_im.reshape(*gs, n_dec).transpose(2, 0, 1)[None],
            s_conv)
```

```python
import jax
import jax.numpy as jnp
from jax import lax
from jax.experimental import pallas as pl
from jax.experimental.pallas import tpu as pltpu

D_MODEL = 1024
D_FF = 2816
SSM_WIDTH = 512
CONV_WIDTH = 512
SSM_GROUP = 16
N_SSM_GROUPS = 32
SSM_STATE = 64
CONV_K = 3
IN_PROJ_WIDTH = SSM_WIDTH + 3 * CONV_WIDTH
EPS = 1e-6

LANES = 128
SUBLANES = 8
N_STATE = N_SSM_GROUPS * SSM_STATE
N_BLK = SSM_WIDTH // LANES
BLK_STATE = N_STATE // N_BLK
SLABS_PER_BLK = 2 * BLK_STATE // LANES
HALF = SLABS_PER_BLK // 2
N_TILE = N_BLK * HALF

FFN_TM = 1024
FFN_SUB = 512
W1_STEPS = 1
W2_STEPS = 1
W1_CHUNKS = 16
W2_CHUNKS = 8
FFN_TF = 256
SCAN_T = 64
N_PAIR = SCAN_T // 2
PAIR_PITCH = N_PAIR + 4
VMEM_LIMIT = 62 * 1024 * 1024

_NT = (((1,), (1,)), ((), ()))


def _rms(x, g):
    r = lax.rsqrt(jnp.mean(x * x, axis=-1, keepdims=True) + EPS)
    return x * r * g


def _bdot(a, b):
    return jnp.dot(a.astype(jnp.bfloat16), b.astype(jnp.bfloat16), preferred_element_type=jnp.float32)


def _swiglu_stage(xn, wg_ref, wu_ref, a_ref, lo, hi):
    rows = xn.shape[0]
    for j in range(lo, hi, FFN_TF):
        g = jnp.dot(xn, wg_ref[:, j:j + FFN_TF], preferred_element_type=jnp.float32)
        u = jnp.dot(xn, wu_ref[:, j:j + FFN_TF], preferred_element_type=jnp.float32)
        a_ref[0:rows, j:j + FFN_TF] = ((g * jax.nn.sigmoid(g)) * u).astype(a_ref.dtype)


def _swiglu_staged(xn, wg_ref, wu_ref, wd_ref, a_ref):
    rows = xn.shape[0]
    _swiglu_stage(xn, wg_ref, wu_ref, a_ref, 0, D_FF)
    return jnp.dot(a_ref[0:rows, :], wd_ref[...], preferred_element_type=jnp.float32)


def _stream_cast(triples, sem, n_chunks):
    def copy(w, c, slot):
        src, buf, _ = triples[w]
        n = buf.shape[1]
        return pltpu.make_async_copy(src.at[pl.ds(pl.multiple_of(c * n, SUBLANES), n), :], buf.at[slot],
                                     sem.at[2 * w + slot])

    for w in range(len(triples)):
        copy(w, 0, 0).start()

    def body(c, carry):
        slot = lax.rem(c, 2)

        @pl.when(c + 1 < n_chunks)
        def _prefetch():
            for w in range(len(triples)):
                copy(w, c + 1, 1 - slot).start()

        for w, (_, buf, dst) in enumerate(triples):
            copy(w, c, slot).wait()
            n = buf.shape[1]
            dst[pl.ds(pl.multiple_of(c * n, SUBLANES), n), :] = buf[slot].astype(dst.dtype)
        return carry

    lax.fori_loop(0, n_chunks, body, 0)


def _staging(rows, cols):
    return pltpu.VMEM((2, rows, cols), jnp.float32)


def _whole(shape):
    return pl.BlockSpec(shape, lambda i: (0,) * len(shape), pipeline_mode=pl.Buffered(1))


_IN_HBM = pl.BlockSpec(memory_space=pl.ANY)


def _glu_tail(y, u, d_ref, wglu_ref, bglu_ref):
    y = y + d_ref[...] * u
    z = jax.nn.gelu(y)
    gate = jax.nn.sigmoid(_bdot(z, wglu_ref[...]) + bglu_ref[...])
    return z * gate


def _dot3_nt(a, b):
    bf = jnp.bfloat16
    a_hi, b_hi = a.astype(bf), b.astype(bf)
    a_lo = (a - a_hi.astype(jnp.float32)).astype(bf)
    b_lo = (b - b_hi.astype(jnp.float32)).astype(bf)
    dot = lambda p, q: lax.dot_general(p, q, _NT, preferred_element_type=jnp.float32)
    return dot(a_hi, b_hi) + (dot(a_hi, b_lo) + dot(a_lo, b_hi))


def _discretize_kernel(lr_ref, li_ref, ldt_ref, bre_ref, bim_ref, ctre_ref, ctim_ref,
                       are_ref, aim_ref, a2re_ref, a2im_ref, wb_ref, wct_ref, wdm_ref):
    row_group = lax.broadcasted_iota(jnp.int32, (LANES, BLK_STATE), 0) // SSM_GROUP
    col_group = lax.broadcasted_iota(jnp.int32, (LANES, BLK_STATE), 1) // SSM_STATE
    on_diagonal = row_group == col_group

    def blocks(m):
        pair = jnp.concatenate([m, m], axis=1)
        return jnp.where(on_diagonal, jnp.concatenate([pair] * (BLK_STATE // LANES), axis=1), 0.0)

    gpb = N_SSM_GROUPS // N_BLK

    def state_rows(ref):
        return jnp.concatenate(
            [jnp.concatenate([ref[k * gpb + g:k * gpb + g + 1, :] for g in range(gpb)], axis=1)
             for k in range(N_BLK)], axis=0)

    def per_state(row_ref):
        vals = row_ref[...]
        col_group = lax.broadcasted_iota(jnp.int32, (1, BLK_STATE), 1) // SSM_STATE
        rows = []
        for k in range(N_BLK):
            row = jnp.zeros((1, BLK_STATE), jnp.float32)
            for g in range(gpb):
                row = jnp.where(col_group == g, vals[:, k * gpb + g:k * gpb + g + 1], row)
            rows.append(row)
        return jnp.concatenate(rows, axis=0)

    lr, li = state_rows(lr_ref), state_rows(li_ref)
    dt = jnp.exp(per_state(ldt_ref))
    mag = jnp.exp(lr * dt)
    ab_re = mag * jnp.cos(li * dt)
    ab_im = mag * jnp.sin(li * dt)
    den = lr * lr + li * li
    nr = ab_re - 1.0
    ni = ab_im
    coef_re = (nr * lr + ni * li) / den
    coef_im = (ni * lr - nr * li) / den
    are_ref[...] = ab_re
    aim_ref[...] = ab_im
    a2_re = ab_re * ab_re - ab_im * ab_im
    a2_im = 2.0 * (ab_re * ab_im)
    bf = jnp.bfloat16
    for k in range(N_BLK):
        ar, ai = ab_re[k:k + 1, :], ab_im[k:k + 1, :]
        for j in range(HALF):
            lanes = slice(j * LANES, (j + 1) * LANES)
            a2re_ref[k * HALF + j] = jnp.broadcast_to(a2_re[k:k + 1, lanes], (SUBLANES, LANES))
            a2im_ref[k * HALF + j] = jnp.broadcast_to(a2_im[k:k + 1, lanes], (SUBLANES, LANES))
        cr, ci = coef_re[k:k + 1, :], coef_im[k:k + 1, :]
        bre, bim = blocks(bre_ref[k]), blocks(bim_ref[k])
        bb_re = cr * bre - ci * bim
        bb_im = cr * bim + ci * bre
        wb_ref[k, 0:LANES, 0:BLK_STATE] = (ar * bb_re - ai * bb_im).astype(bf)
        wb_ref[k, 0:LANES, BLK_STATE:] = (ar * bb_im + ai * bb_re).astype(bf)
        wb_ref[k, LANES:, 0:BLK_STATE] = bb_re.astype(bf)
        wb_ref[k, LANES:, BLK_STATE:] = bb_im.astype(bf)
        ctre, ctim = blocks(ctre_ref[k]), blocks(ctim_ref[k])
        wct_ref[k, 0:LANES, 0:BLK_STATE] = ctre.astype(bf)
        wct_ref[k, 0:LANES, BLK_STATE:] = (-ctim).astype(bf)
        wct_ref[k, LANES:, 0:BLK_STATE] = (ctre * ar - ctim * ai).astype(bf)
        wct_ref[k, LANES:, BLK_STATE:] = (-(ctre * ai + ctim * ar)).astype(bf)
        wdm = _dot3_nt(bb_re, ctre) - _dot3_nt(bb_im, ctim)
        wdm_ref[k] = wdm.astype(bf)


def _discretize(lr, li, ldt, bre, bim, ctre, ctim):
    f32, bf = jnp.float32, jnp.bfloat16
    return pl.pallas_call(
        _discretize_kernel,
        out_shape=[
            jax.ShapeDtypeStruct((N_BLK, BLK_STATE), f32),
            jax.ShapeDtypeStruct((N_BLK, BLK_STATE), f32),
            jax.ShapeDtypeStruct((N_TILE, SUBLANES, LANES), f32),
            jax.ShapeDtypeStruct((N_TILE, SUBLANES, LANES), f32),
            jax.ShapeDtypeStruct((N_BLK, 2 * LANES, 2 * BLK_STATE), bf),
            jax.ShapeDtypeStruct((N_BLK, 2 * LANES, 2 * BLK_STATE), bf),
            jax.ShapeDtypeStruct((N_BLK, LANES, LANES), bf),
        ],
        name="ssm_discretize",
    )(lr, li, ldt, bre, bim, ctre, ctim)


def _ffn1_mixer_kernel(x_ref, xd_ref, n1_ref, wg_c, wu_c, wd_c, nm_ref, win_c,
                       a2re_ref, a2im_ref, wb_ref, wct_ref, wdm_ref, d_ref, wglu_ref, bglu_ref, cw_ref,
                       h_ref, mix_ref, sre_ref, sim_ref, nconv_ref, ud_ref, vd_ref, gd_ref,
                       xs0_ref, xs1_ref, xs2_ref, xs3_ref, us_ref, ys_ref, l_ref, yo_ref, ye_ref, e_ref,
                       ecarry_ref, v_ref, g_ref, state_ref, wg_ref, wu_ref, wd_ref, win_ref, a_ref, xdd_ref,
                       wg_s, wu_s, wd_s, win_s, cast_sem):
    xs_refs = (xs0_ref, xs1_ref, xs2_ref, xs3_ref)
    step = pl.program_id(0)
    last = pl.num_programs(0) - 1
    nb, t_len, _ = x_ref.shape
    n_dec = xd_ref.shape[0]
    prow = nb * PAIR_PITCH
    tiles = [(k, j) for k in range(N_BLK) for j in range(HALF)]

    @pl.when(step < W1_STEPS)
    def _load_weights():
        _stream_cast(((wg_c, wg_s, wg_ref), (wu_c, wu_s, wu_ref), (wd_c, wd_s, wd_ref),
                      (win_c, win_s, win_ref)), cast_sem, W1_CHUNKS)

    @pl.when(step == 0)
    def _init():
        for ref in (state_ref, us_ref, l_ref, e_ref, ecarry_ref, v_ref, g_ref):
            ref[...] = jnp.zeros_like(ref)
        xdd_ref[...] = xd_ref[...]

    @pl.when(step >= W1_STEPS)
    def _main():
        carry = {}

        def build_lhs():
            for b in range(nb):
                for k in range(N_BLK):
                    for par in range(2):
                        col = (2 * k + par) * LANES
                        l_ref[pl.ds(b * PAIR_PITCH, N_PAIR), col:col + LANES] = (
                            us_ref[k, pl.ds(b * t_len + par, N_PAIR, stride=2), :])

        def b_proj(ks):
            for k in ks:
                lhs = l_ref[:, 2 * k * LANES:2 * (k + 1) * LANES].astype(jnp.bfloat16)
                v = jnp.dot(lhs, wb_ref[k], preferred_element_type=jnp.float32)
                for j in range(SLABS_PER_BLK):
                    xs_refs[k][j] = v[:, j * LANES:(j + 1) * LANES]

        def scan(r0, r1):
            if r0 == 0:
                carry["re"] = [state_ref[k * SLABS_PER_BLK + j] for k, j in tiles]
                carry["im"] = [state_ref[k * SLABS_PER_BLK + HALF + j] for k, j in tiles]
            xr, xi = carry["re"], carry["im"]
            for r in range(r0, r1):
                rows = pl.ds(r, SUBLANES, stride=PAIR_PITCH)
                for n, (k, j) in enumerate(tiles):
                    a_re = a2re_ref[k * HALF + j]
                    a_im = a2im_ref[k * HALF + j]
                    nr = (a_re * xr[n] + xs_refs[k][j, rows, :]) - a_im * xi[n]
                    ni = (a_re * xi[n] + xs_refs[k][HALF + j, rows, :]) + a_im * xr[n]
                    xs_refs[k][j, rows, :] = nr
                    xs_refs[k][HALF + j, rows, :] = ni
                    xr[n], xi[n] = nr, ni
            if r1 == N_PAIR:
                for n, (k, j) in enumerate(tiles):
                    state_ref[k * SLABS_PER_BLK + j] = xr[n]
                    state_ref[k * SLABS_PER_BLK + HALF + j] = xi[n]

        def c_proj(ks):
            for k in ks:
                lanes = slice(k * LANES, (k + 1) * LANES)
                x = jnp.concatenate([xs_refs[k][j] for j in range(SLABS_PER_BLK)], axis=1)
                ce = lax.dot_general(x.astype(jnp.bfloat16), wct_ref[k], _NT,
                                     preferred_element_type=jnp.float32)
                yo_ref[:, lanes] = ce[:, 0:LANES]
                e_ref[k, SUBLANES:SUBLANES + prow, :] = ce[:, LANES:]
                nxt = e_ref[k, pl.ds(SUBLANES + N_PAIR - 1, nb, stride=PAIR_PITCH), :]
                e_ref[k, pl.ds(SUBLANES - 1, nb, stride=PAIR_PITCH), :] = ecarry_ref[k]
                ecarry_ref[k] = nxt
                u_even = l_ref[:, 2 * k * LANES:(2 * k + 1) * LANES].astype(jnp.bfloat16)
                ye_ref[:, lanes] = (e_ref[k, SUBLANES - 1:SUBLANES - 1 + prow, :]
                                    + jnp.dot(u_even, wdm_ref[k], preferred_element_type=jnp.float32))

        def glu_emit():
            for b in range(nb):
                for k in range(N_BLK):
                    lanes = slice(k * LANES, (k + 1) * LANES)
                    for par, src in ((0, ye_ref), (1, yo_ref)):
                        ys_ref[k, pl.ds(b * t_len + par, N_PAIR, stride=2), :] = (
                            src[pl.ds(b * PAIR_PITCH, N_PAIR), lanes])
            y = jnp.concatenate([ys_ref[k] for k in range(N_BLK)], axis=1)
            u = jnp.concatenate([us_ref[k] for k in range(N_BLK)], axis=1)
            s_out = _glu_tail(y, u, d_ref, wglu_ref, bglu_ref)
            mix_ref[:, :, 0:SSM_WIDTH] = s_out.reshape(nb, t_len, SSM_WIDTH).astype(mix_ref.dtype)

        def conv():
            v = v_ref[:, SUBLANES:SUBLANES + t_len, :]
            z = cw_ref[0] * v_ref[:, SUBLANES - 2:SUBLANES - 2 + t_len, :]
            z = z + cw_ref[1] * v_ref[:, SUBLANES - 1:SUBLANES - 1 + t_len, :]
            z = z + cw_ref[2] * v
            mix_ref[:, :, SSM_WIDTH:] = (g_ref[...] * z).astype(mix_ref.dtype)
            v_ref[:, 0:SUBLANES, :] = v_ref[:, t_len:t_len + SUBLANES, :]

        q = N_PAIR // 4
        pieces = [
            lambda: (conv(), build_lhs(), b_proj((0,))), lambda: b_proj((1,)), lambda: b_proj((2,)),
            lambda: b_proj((3,)),
            lambda: scan(0, q), lambda: scan(q, 2 * q), lambda: scan(2 * q, 3 * q), lambda: scan(3 * q, N_PAIR),
            lambda: c_proj((0, 1)), lambda: c_proj((2, 3)),
            glu_emit,
        ]
        assert len(pieces) <= D_FF // FFN_TF

        db = n_dec // t_len
        x_top = jnp.where(step == last, xdd_ref[...], x_ref[0:db].reshape(n_dec, D_MODEL))
        x = jnp.concatenate([x_top, x_ref[db:].reshape((nb - db) * t_len, D_MODEL)], axis=0)
        xn = _rms(x, n1_ref[...]).astype(jnp.bfloat16)
        for c, j in enumerate(range(0, D_FF, FFN_TF)):
            _swiglu_stage(xn, wg_ref, wu_ref, a_ref, j, j + FFN_TF)
            if c < len(pieces):
                pieces[c]()
        h = x + 0.5 * jnp.dot(a_ref[...], wd_ref[...], preferred_element_type=jnp.float32)
        hn = _rms(h, nm_ref[...]).astype(jnp.bfloat16)
        p = jnp.dot(hn, win_ref[...], preferred_element_type=jnp.float32)
        h_ref[...] = h.reshape(nb, t_len, D_MODEL)
        v_new = p[:, SSM_WIDTH + 2 * CONV_WIDTH:] * p[:, SSM_WIDTH:SSM_WIDTH + CONV_WIDTH]
        v_ref[:, SUBLANES:SUBLANES + t_len, :] = v_new.reshape(nb, t_len, CONV_WIDTH)
        g_ref[...] = p[:, SSM_WIDTH + CONV_WIDTH:SSM_WIDTH + 2 * CONV_WIDTH].reshape(nb, t_len, CONV_WIDTH)
        for k in range(N_BLK):
            us_ref[k] = p[:, k * LANES:(k + 1) * LANES]

    @pl.when(step == last)
    def _final():
        for n, (k, j) in enumerate(tiles):
            for out_ref, slab in ((sre_ref, k * SLABS_PER_BLK + j), (sim_ref, k * SLABS_PER_BLK + HALF + j)):
                tile = state_ref[slab]
                for half in range(LANES // SSM_STATE):
                    group = (k * BLK_STATE + j * LANES) // SSM_STATE + half
                    out_ref[:, group, :] = tile[:, half * SSM_STATE:(half + 1) * SSM_STATE]
        nconv_ref[...] = v_ref[:, SUBLANES - 2:SUBLANES, :]
        for k in range(N_BLK):
            ud_ref[:, k * LANES:(k + 1) * LANES] = us_ref[k, 0:n_dec, :]
        for b in range(n_dec // t_len):
            vd_ref[b * t_len:(b + 1) * t_len, :] = v_ref[b, SUBLANES:SUBLANES + t_len, :]
            gd_ref[b * t_len:(b + 1) * t_len, :] = g_ref[b]


def _ffn1_mixer(x, xd, n1, wg, wu, wd, nm, win, a2_re, a2_im, wb, wct, wdm, d, wglu, bglu, cw):
    nb, seq, _ = x.shape
    n_dec = xd.shape[0]
    n_chunk = seq // SCAN_T
    prow = nb * PAIR_PITCH
    f32, bf = jnp.float32, jnp.bfloat16
    main = lambda i: jnp.maximum(i - W1_STEPS, 0)
    return pl.pallas_call(
        _ffn1_mixer_kernel,
        grid=(W1_STEPS + n_chunk + 1,),
        in_specs=[
            pl.BlockSpec((nb, SCAN_T, D_MODEL), lambda i: (0, jnp.minimum(main(i), n_chunk - 1), 0)),
            pl.BlockSpec((n_dec, None, D_MODEL), lambda i: (0, 0, 0), pipeline_mode=pl.Buffered(1)),
            _whole((1, D_MODEL)),
            _IN_HBM,
            _IN_HBM,
            _IN_HBM,
            _whole((1, D_MODEL)),
            _IN_HBM,
            _whole((N_TILE, SUBLANES, LANES)),
            _whole((N_TILE, SUBLANES, LANES)),
            _whole((N_BLK, 2 * LANES, 2 * BLK_STATE)),
            _whole((N_BLK, 2 * LANES, 2 * BLK_STATE)),
            _whole((N_BLK, LANES, LANES)),
            _whole((1, SSM_WIDTH)),
            _whole((SSM_WIDTH, SSM_WIDTH)),
            _whole((1, SSM_WIDTH)),
            _whole((CONV_K, 1, CONV_WIDTH)),
        ],
        out_specs=[
            pl.BlockSpec((nb, SCAN_T, D_MODEL), lambda i: (0, main(i), 0)),
            pl.BlockSpec((nb, SCAN_T, D_MODEL), lambda i: (0, jnp.clip(main(i) - 1, 0, n_chunk - 1), 0)),
            pl.BlockSpec((nb, N_SSM_GROUPS, SSM_STATE), lambda i: (0, 0, 0)),
            pl.BlockSpec((nb, N_SSM_GROUPS, SSM_STATE), lambda i: (0, 0, 0)),
            pl.BlockSpec((nb, CONV_K - 1, CONV_WIDTH), lambda i: (0, 0, 0)),
            pl.BlockSpec((n_dec, SSM_WIDTH), lambda i: (0, 0)),
            pl.BlockSpec((n_dec, CONV_WIDTH), lambda i: (0, 0)),
            pl.BlockSpec((n_dec, CONV_WIDTH), lambda i: (0, 0)),
        ],
        out_shape=[
            jax.ShapeDtypeStruct((nb, seq + SCAN_T, D_MODEL), f32),
            jax.ShapeDtypeStruct((nb, seq, D_MODEL), bf),
            jax.ShapeDtypeStruct((nb, N_SSM_GROUPS, SSM_STATE), f32),
            jax.ShapeDtypeStruct((nb, N_SSM_GROUPS, SSM_STATE), f32),
            jax.ShapeDtypeStruct((nb, CONV_K - 1, CONV_WIDTH), f32),
            jax.ShapeDtypeStruct((n_dec, SSM_WIDTH), f32),
            jax.ShapeDtypeStruct((n_dec, CONV_WIDTH), f32),
            jax.ShapeDtypeStruct((n_dec, CONV_WIDTH), f32),
        ],
        scratch_shapes=[
            *[pltpu.VMEM((SLABS_PER_BLK, prow, LANES), f32) for _ in range(N_BLK)],
            pltpu.VMEM((N_BLK, nb * SCAN_T, LANES), f32),
            pltpu.VMEM((N_BLK, nb * SCAN_T, LANES), f32),
            pltpu.VMEM((prow, 2 * SSM_WIDTH), f32),
            pltpu.VMEM((prow, SSM_WIDTH), f32),
            pltpu.VMEM((prow, SSM_WIDTH), f32),
            pltpu.VMEM((N_BLK, SUBLANES + prow, LANES), f32),
            pltpu.VMEM((N_BLK, SUBLANES, LANES), f32),
            pltpu.VMEM((nb, SCAN_T + SUBLANES, CONV_WIDTH), f32),
            pltpu.VMEM((nb, SCAN_T, CONV_WIDTH), f32),
            pltpu.VMEM((2 * N_TILE, SUBLANES, LANES), f32),
            pltpu.VMEM((D_MODEL, D_FF), bf),
            pltpu.VMEM((D_MODEL, D_FF), bf),
            pltpu.VMEM((D_FF, D_MODEL), bf),
            pltpu.VMEM((D_MODEL, IN_PROJ_WIDTH), bf),
            pltpu.VMEM((nb * SCAN_T, D_FF), bf),
            pltpu.VMEM((n_dec, D_MODEL), f32),
            _staging(D_MODEL // W1_CHUNKS, D_FF),
            _staging(D_MODEL // W1_CHUNKS, D_FF),
            _staging(D_FF // W1_CHUNKS, D_MODEL),
            _staging(D_MODEL // W1_CHUNKS, IN_PROJ_WIDTH),
            pltpu.SemaphoreType.DMA((8,)),
        ],
        compiler_params=pltpu.CompilerParams(
            dimension_semantics=("arbitrary",), vmem_limit_bytes=VMEM_LIMIT),
        name="ffn1_mixer",
    )(x, xd, n1, wg, wu, wd, nm, win, a2_re, a2_im, wb, wct, wdm, d, wglu, bglu, cw)


def _ffn2_rows(h, mix, weights, a_ref):
    wout_ref, n2_ref, wg_ref, wu_ref, wd_ref, nf_ref = weights
    h = h + jnp.dot(mix, wout_ref[...], preferred_element_type=jnp.float32)
    hn = _rms(h, n2_ref[...]).astype(jnp.bfloat16)
    h = h + 0.5 * _swiglu_staged(hn, wg_ref, wu_ref, wd_ref, a_ref)
    return _rms(h, nf_ref[...])


def _ffn2_kernel(h_ref, mix_ref, hd0_ref, hd1_ref, mixd_ref, wout_c, wg_c, wu_c, wd_c, n2_ref, nf_ref,
                 y_ref, yd_ref, wout_ref, wg_ref, wu_ref, wd_ref, a0_ref, a1_ref,
                 wout_s, wg_s, wu_s, wd_s, cast_sem):
    step = pl.program_id(0)
    last = pl.num_programs(0) - 1
    weights = (wout_ref, n2_ref, wg_ref, wu_ref, wd_ref, nf_ref)

    @pl.when(step < W2_STEPS)
    def _load_weights():
        _stream_cast(((wout_c, wout_s, wout_ref), (wg_c, wg_s, wg_ref), (wu_c, wu_s, wu_ref),
                      (wd_c, wd_s, wd_ref)), cast_sem, W2_CHUNKS)

    @pl.when((step >= W2_STEPS) & (step < last))
    def _prompt_tile():
        assert h_ref.shape[0] == 2 * FFN_SUB
        ra, rb = slice(0, FFN_SUB), slice(FFN_SUB, 2 * FFN_SUB)
        h_a = h_ref[ra, :] + jnp.dot(mix_ref[ra, :], wout_ref[...], preferred_element_type=jnp.float32)
        h_b = h_ref[rb, :] + jnp.dot(mix_ref[rb, :], wout_ref[...], preferred_element_type=jnp.float32)
        hn_a = _rms(h_a, n2_ref[...]).astype(jnp.bfloat16)
        _swiglu_stage(hn_a, wg_ref, wu_ref, a0_ref, 0, FFN_TF)
        hn_b = _rms(h_b, n2_ref[...]).astype(jnp.bfloat16)
        _swiglu_stage(hn_a, wg_ref, wu_ref, a0_ref, FFN_TF, D_FF)
        f_a = jnp.dot(a0_ref[...], wd_ref[...], preferred_element_type=jnp.float32)
        _swiglu_stage(hn_b, wg_ref, wu_ref, a1_ref, 0, FFN_TF)
        y_ref[ra, :] = _rms(h_a + 0.5 * f_a, nf_ref[...])
        _swiglu_stage(hn_b, wg_ref, wu_ref, a1_ref, FFN_TF, D_FF)
        f_b = jnp.dot(a1_ref[...], wd_ref[...], preferred_element_type=jnp.float32)
        y_ref[rb, :] = _rms(h_b + 0.5 * f_b, nf_ref[...])

    @pl.when(step == last)
    def _decode_rows():
        h_d = jnp.concatenate([hd0_ref[...], hd1_ref[...]], axis=0)
        yd_ref[...] = _ffn2_rows(h_d, mixd_ref[...], weights, a0_ref)


def _ffn2(h_all, mix, mixd, wout, n2, wg, wu, wd, nf):
    nb, seq, _ = mix.shape
    seq_all = h_all.shape[1]
    n_dec = mixd.shape[0]
    tail = seq_all - seq
    assert n_dec == 2 * tail
    per_b = seq // FFN_TM
    n_tiles = nb * per_b
    h_all = h_all.reshape(nb * seq_all, D_MODEL)
    mix = mix.reshape(nb * seq, D_MODEL)
    idx = lambda i: jnp.clip(i - W2_STEPS, 0, n_tiles - 1)
    tile = lambda i: (idx(i), 0)
    assert seq_all % SUBLANES == 0
    tile_all = lambda i: (pl.multiple_of((idx(i) // per_b) * seq_all + (idx(i) % per_b) * FFN_TM, SUBLANES), 0)
    yp, yd = pl.pallas_call(
        _ffn2_kernel,
        grid=(W2_STEPS + n_tiles + 1,),
        in_specs=[
            pl.BlockSpec((pl.Element(FFN_TM), pl.Element(D_MODEL)), tile_all),
            pl.BlockSpec((FFN_TM, D_MODEL), tile),
            pl.BlockSpec((pl.Element(tail), pl.Element(D_MODEL)), lambda i: (seq, 0),
                         pipeline_mode=pl.Buffered(1)),
            pl.BlockSpec((pl.Element(tail), pl.Element(D_MODEL)), lambda i: (seq_all + seq, 0),
                         pipeline_mode=pl.Buffered(1)),
            _whole(mixd.shape),
            _IN_HBM,
            _IN_HBM,
            _IN_HBM,
            _IN_HBM,
            _whole((1, D_MODEL)),
            _whole((1, D_MODEL)),
        ],
        out_specs=[
            pl.BlockSpec((FFN_TM, D_MODEL), tile),
            pl.BlockSpec((n_dec, None, D_MODEL), lambda i: (0, 0, 0)),
        ],
        out_shape=[
            jax.ShapeDtypeStruct((nb * seq, D_MODEL), jnp.float32),
            jax.ShapeDtypeStruct((n_dec, 1, D_MODEL), jnp.float32),
        ],
        scratch_shapes=[
            pltpu.VMEM((D_MODEL, D_MODEL), jnp.bfloat16),
            pltpu.VMEM((D_MODEL, D_FF), jnp.bfloat16),
            pltpu.VMEM((D_MODEL, D_FF), jnp.bfloat16),
            pltpu.VMEM((D_FF, D_MODEL), jnp.bfloat16),
            pltpu.VMEM((FFN_SUB, D_FF), jnp.bfloat16),
            pltpu.VMEM((FFN_SUB, D_FF), jnp.bfloat16),
            _staging(D_MODEL // W2_CHUNKS, D_MODEL),
            _staging(D_MODEL // W2_CHUNKS, D_FF),
            _staging(D_MODEL // W2_CHUNKS, D_FF),
            _staging(D_FF // W2_CHUNKS, D_MODEL),
            pltpu.SemaphoreType.DMA((8,)),
        ],
        compiler_params=pltpu.CompilerParams(
            dimension_semantics=("arbitrary",), vmem_limit_bytes=VMEM_LIMIT),
        name="outproj_ffn2",
    )(h_all, mix, h_all, h_all, mixd, wout, wg, wu, wd, n2, nf)
    return yp.reshape(nb, seq, D_MODEL), yd


def _mixer_step_kernel(u_ref, v_ref, g_ref, hre_ref, him_ref, buf_ref, are_ref, aim_ref, bb_ref, wct_ref,
                       d_ref, wglu_ref, bglu_ref, cw_ref, mix_ref, sre_ref, sim_ref, nconv_ref):
    u = u_ref[...]
    u_bf = u.astype(jnp.bfloat16)
    ys = []
    for k in range(N_BLK):
        bu = jnp.dot(u_bf[:, k * LANES:(k + 1) * LANES], bb_ref[k], preferred_element_type=jnp.float32)
        st = slice(k * BLK_STATE, (k + 1) * BLK_STATE)
        a_re = are_ref[k:k + 1, :]
        a_im = aim_ref[k:k + 1, :]
        h_re = hre_ref[st, :].T
        h_im = him_ref[st, :].T
        x_re = (a_re * h_re + bu[:, :BLK_STATE]) - a_im * h_im
        x_im = (a_re * h_im + bu[:, BLK_STATE:]) + a_im * h_re
        sre_ref[st, :] = x_re.T
        sim_ref[st, :] = x_im.T
        x = jnp.concatenate([x_re, x_im], axis=1).astype(jnp.bfloat16)
        ys.append(lax.dot_general(x, wct_ref[k], _NT, preferred_element_type=jnp.float32))
    y = jnp.concatenate(ys, axis=1)
    mix_ref[:, 0:SSM_WIDTH] = _glu_tail(y, u, d_ref, wglu_ref, bglu_ref).astype(mix_ref.dtype)

    v = v_ref[...]
    buf1 = buf_ref[:, 1, :]
    z = cw_ref[0] * buf_ref[:, 0, :] + cw_ref[1] * buf1 + cw_ref[2] * v
    mix_ref[:, SSM_WIDTH:] = (g_ref[...] * z).astype(mix_ref.dtype)
    nconv_ref[:, 0, :] = buf1
    nconv_ref[:, 1, :] = v


def _mixer_step(u, v, g, h_re, h_im, conv_state, a_re, a_im, wb, wct, d, wglu, bglu, cw):
    nb = u.shape[0]
    conv_spec = pl.BlockSpec((None, nb, CONV_K - 1, CONV_WIDTH), lambda i: (0, 0, 0, 0))
    half = (N_BLK, LANES, 2 * BLK_STATE)
    bb_spec = pl.BlockSpec(half, lambda i: (0, 1, 0), pipeline_mode=pl.Buffered(1))
    c_spec = pl.BlockSpec(half, lambda i: (0, 0, 0), pipeline_mode=pl.Buffered(1))
    rows = (u, v, g, h_re, h_im)
    tail = (d, wglu, bglu, cw)
    return pl.pallas_call(
        _mixer_step_kernel,
        grid=(1,),
        in_specs=([_whole(a.shape) for a in rows] + [conv_spec] + [_whole(a_re.shape), _whole(a_im.shape)]
                  + [bb_spec, c_spec] + [_whole(a.shape) for a in tail]),
        out_specs=[
            pl.BlockSpec((nb, D_MODEL), lambda i: (0, 0)),
            pl.BlockSpec((N_STATE, nb), lambda i: (0, 0)),
            pl.BlockSpec((N_STATE, nb), lambda i: (0, 0)),
            conv_spec,
        ],
        out_shape=[
            jax.ShapeDtypeStruct((nb, D_MODEL), jnp.bfloat16),
            jax.ShapeDtypeStruct((N_STATE, nb), jnp.float32),
            jax.ShapeDtypeStruct((N_STATE, nb), jnp.float32),
            jax.ShapeDtypeStruct(conv_state.shape, jnp.float32),
        ],
        compiler_params=pltpu.CompilerParams(
            dimension_semantics=("arbitrary",), vmem_limit_bytes=VMEM_LIMIT),
        name="mixer_step",
    )(*rows, conv_state, a_re, a_im, wb, wct, *tail)


def _ssm_layout(lam_re, lam_im, log_dt, b_re, b_im, c_re, c_im):
    gpb = N_SSM_GROUPS // N_BLK

    def b_rows(b):
        return b.reshape(N_BLK, gpb, SSM_STATE, SSM_GROUP).transpose(0, 1, 3, 2).reshape(N_BLK, LANES, SSM_STATE)

    def c_rows(c):
        return c.reshape(N_BLK, LANES, SSM_STATE)

    return lam_re, lam_im, log_dt, b_rows(b_re), b_rows(b_im), c_rows(c_re), c_rows(c_im)


def kernel(x_prompt, x_sample, state_ssm_re, state_ssm_im, state_conv, ffn1_norm, ffn1_w_gate, ffn1_w_up, ffn1_w_down, mix_norm, w_in, ssm_lambda_re, ssm_lambda_im, ssm_log_dt, ssm_b_re, ssm_b_im, ssm_c_re, ssm_c_im, ssm_d, ssm_w_glu, ssm_b_glu, conv_w, w_out, ffn2_norm, ffn2_w_gate, ffn2_w_up, ffn2_w_down, final_norm):
    assert ffn1_norm.shape[0] == 1, "single-layer trunk"
    n_prompt, seq, _ = x_prompt.shape
    n_dec, dec_seq, _ = x_sample.shape
    assert dec_seq == 1 and n_prompt == SUBLANES and seq % FFN_TM == 0 and seq % SCAN_T == 0
    assert n_dec % SCAN_T == 0 and n_dec <= n_prompt * SCAN_T and FFN_TM % FFN_SUB == 0

    n1 = ffn1_norm[0][None, :]
    nm = mix_norm[0][None, :]
    n2 = ffn2_norm[0][None, :]
    nf = final_norm[None, :]
    wglu = ssm_w_glu[0]
    layout = _ssm_layout(ssm_lambda_re[0], ssm_lambda_im[0], ssm_log_dt, ssm_b_re[0], ssm_b_im[0],
                         ssm_c_re[0], ssm_c_im[0])
    a_re, a_im, a2_re, a2_im, wb, wct, wdm = _discretize(*layout)
    d = ssm_d[0][None, :]
    bglu = ssm_b_glu[0][None, :]
    cw = conv_w.transpose(1, 0, 2)

    h_all, mixp, p_re, p_im, p_conv, ud, vd, gd = _ffn1_mixer(
        x_prompt, x_sample, n1, ffn1_w_gate[0], ffn1_w_up[0], ffn1_w_down[0], nm,
        w_in[0], a2_re, a2_im, wb, wct, wdm, d, wglu, bglu, cw)

    s_re0 = state_ssm_re[0].transpose(1, 2, 0).reshape(N_STATE, n_dec)
    s_im0 = state_ssm_im[0].transpose(1, 2, 0).reshape(N_STATE, n_dec)
    mixd, s_re, s_im, s_conv = _mixer_step(ud, vd, gd, s_re0, s_im0, state_conv, a_re, a_im, wb, wct,
                                           d, wglu, bglu, cw)

    yp, yd = _ffn2(h_all, mixp, mixd, w_out[0], n2, ffn2_w_gate[0], ffn2_w_up[0], ffn2_w_down[0], nf)

    gs = (N_SSM_GROUPS, SSM_STATE)
    return (yp,
            yd,
            p_re[None],
            p_im[None],
            p_conv[None],
            s_re.reshape(*gs, n_dec).transpose(2, 0, 1)[None],
            s_im.reshape(*gs, n_dec).transpose(2, 0, 1)[None],
            s_conv)
```

```python
import jax
import jax.numpy as jnp
from jax import lax
from jax.experimental import pallas as pl
from jax.experimental.pallas import tpu as pltpu

D_MODEL = 1024
D_FF = 2816
SSM_WIDTH = 512
CONV_WIDTH = 512
SSM_GROUP = 16
N_SSM_GROUPS = 32
SSM_STATE = 64
CONV_K = 3
IN_PROJ_WIDTH = SSM_WIDTH + 3 * CONV_WIDTH
EPS = 1e-6

LANES = 128
SUBLANES = 8
N_STATE = N_SSM_GROUPS * SSM_STATE
N_BLK = SSM_WIDTH // LANES
BLK_STATE = N_STATE // N_BLK
SLABS_PER_BLK = 2 * BLK_STATE // LANES
HALF = SLABS_PER_BLK // 2
N_TILE = N_BLK * HALF

FFN_TM = 1024
FFN_SUB = 512
W1_STEPS = 1
W2_STEPS = 1
W1_CHUNKS = 16
W2_CHUNKS = 8
FFN_TF = 256
SCAN_T = 64
N_PAIR = SCAN_T // 2
PAIR_PITCH = N_PAIR + 4
VMEM_LIMIT = 62 * 1024 * 1024

_NT = (((1,), (1,)), ((), ()))


def _rms(x, g):
    r = lax.rsqrt(jnp.mean(x * x, axis=-1, keepdims=True) + EPS)
    return x * r * g


def _bdot(a, b):
    return jnp.dot(a.astype(jnp.bfloat16), b.astype(jnp.bfloat16), preferred_element_type=jnp.float32)


def _swiglu_stage(xn, wg_ref, wu_ref, a_ref, lo, hi):
    rows = xn.shape[0]
    for j in range(lo, hi, FFN_TF):
        g = jnp.dot(xn, wg_ref[:, j:j + FFN_TF], preferred_element_type=jnp.float32)
        u = jnp.dot(xn, wu_ref[:, j:j + FFN_TF], preferred_element_type=jnp.float32)
        a_ref[0:rows, j:j + FFN_TF] = ((g * jax.nn.sigmoid(g)) * u).astype(a_ref.dtype)


def _swiglu_staged(xn, wg_ref, wu_ref, wd_ref, a_ref):
    rows = xn.shape[0]
    _swiglu_stage(xn, wg_ref, wu_ref, a_ref, 0, D_FF)
    return jnp.dot(a_ref[0:rows, :], wd_ref[...], preferred_element_type=jnp.float32)


def _stream_cast(triples, sem, n_chunks):
    def copy(w, c, slot):
        src, buf, _ = triples[w]
        n = buf.shape[1]
        return pltpu.make_async_copy(src.at[pl.ds(pl.multiple_of(c * n, SUBLANES), n), :], buf.at[slot],
                                     sem.at[2 * w + slot])

    for w in range(len(triples)):
        copy(w, 0, 0).start()

    def body(c, carry):
        slot = lax.rem(c, 2)

        @pl.when(c + 1 < n_chunks)
        def _prefetch():
            for w in range(len(triples)):
                copy(w, c + 1, 1 - slot).start()

        for w, (_, buf, dst) in enumerate(triples):
            copy(w, c, slot).wait()
            n = buf.shape[1]
            dst[pl.ds(pl.multiple_of(c * n, SUBLANES), n), :] = buf[slot].astype(dst.dtype)
        return carry

    lax.fori_loop(0, n_chunks, body, 0)


def _staging(rows, cols):
    return pltpu.VMEM((2, rows, cols), jnp.float32)


def _whole(shape):
    return pl.BlockSpec(shape, lambda i: (0,) * len(shape), pipeline_mode=pl.Buffered(1))


_IN_HBM = pl.BlockSpec(memory_space=pl.ANY)


def _glu_tail(y, u, d_ref, wglu_ref, bglu_ref):
    y = y + d_ref[...] * u
    z = jax.nn.gelu(y)
    gate = jax.nn.sigmoid(_bdot(z, wglu_ref[...]) + bglu_ref[...])
    return z * gate


def _dot3_nt(a, b):
    bf = jnp.bfloat16
    a_hi, b_hi = a.astype(bf), b.astype(bf)
    a_lo = (a - a_hi.astype(jnp.float32)).astype(bf)
    b_lo = (b - b_hi.astype(jnp.float32)).astype(bf)
    dot = lambda p, q: lax.dot_general(p, q, _NT, preferred_element_type=jnp.float32)
    return dot(a_hi, b_hi) + (dot(a_hi, b_lo) + dot(a_lo, b_hi))


def _discretize_kernel(lr_ref, li_ref, ldt_ref, bre_ref, bim_ref, ctre_ref, ctim_ref,
                       are_ref, aim_ref, a2re_ref, a2im_ref, wb_ref, wct_ref, wdm_ref):
    row_group = lax.broadcasted_iota(jnp.int32, (LANES, BLK_STATE), 0) // SSM_GROUP
    col_group = lax.broadcasted_iota(jnp.int32, (LANES, BLK_STATE), 1) // SSM_STATE
    on_diagonal = row_group == col_group

    def blocks(m):
        pair = jnp.concatenate([m, m], axis=1)
        return jnp.where(on_diagonal, jnp.concatenate([pair] * (BLK_STATE // LANES), axis=1), 0.0)

    gpb = N_SSM_GROUPS // N_BLK

    def state_rows(ref):
        return jnp.concatenate(
            [jnp.concatenate([ref[k * gpb + g:k * gpb + g + 1, :] for g in range(gpb)], axis=1)
             for k in range(N_BLK)], axis=0)

    def per_state(row_ref):
        vals = row_ref[...]
        col_group = lax.broadcasted_iota(jnp.int32, (1, BLK_STATE), 1) // SSM_STATE
        rows = []
        for k in range(N_BLK):
            row = jnp.zeros((1, BLK_STATE), jnp.float32)
            for g in range(gpb):
                row = jnp.where(col_group == g, vals[:, k * gpb + g:k * gpb + g + 1], row)
            rows.append(row)
        return jnp.concatenate(rows, axis=0)

    lr, li = state_rows(lr_ref), state_rows(li_ref)
    dt = jnp.exp(per_state(ldt_ref))
    mag = jnp.exp(lr * dt)
    ab_re = mag * jnp.cos(li * dt)
    ab_im = mag * jnp.sin(li * dt)
    den = lr * lr + li * li
    nr = ab_re - 1.0
    ni = ab_im
    coef_re = (nr * lr + ni * li) / den
    coef_im = (ni * lr - nr * li) / den
    are_ref[...] = ab_re
    aim_ref[...] = ab_im
    a2_re = ab_re * ab_re - ab_im * ab_im
    a2_im = 2.0 * (ab_re * ab_im)
    bf = jnp.bfloat16
    for k in range(N_BLK):
        ar, ai = ab_re[k:k + 1, :], ab_im[k:k + 1, :]
        for j in range(HALF):
            lanes = slice(j * LANES, (j + 1) * LANES)
            a2re_ref[k * HALF + j] = jnp.broadcast_to(a2_re[k:k + 1, lanes], (SUBLANES, LANES))
            a2im_ref[k * HALF + j] = jnp.broadcast_to(a2_im[k:k + 1, lanes], (SUBLANES, LANES))
        cr, ci = coef_re[k:k + 1, :], coef_im[k:k + 1, :]
        bre, bim = blocks(bre_ref[k]), blocks(bim_ref[k])
        bb_re = cr * bre - ci * bim
        bb_im = cr * bim + ci * bre
        wb_ref[k, 0:LANES, 0:BLK_STATE] = (ar * bb_re - ai * bb_im).astype(bf)
        wb_ref[k, 0:LANES, BLK_STATE:] = (ar * bb_im + ai * bb_re).astype(bf)
        wb_ref[k, LANES:, 0:BLK_STATE] = bb_re.astype(bf)
        wb_ref[k, LANES:, BLK_STATE:] = bb_im.astype(bf)
        ctre, ctim = blocks(ctre_ref[k]), blocks(ctim_ref[k])
        wct_ref[k, 0:LANES, 0:BLK_STATE] = ctre.astype(bf)
        wct_ref[k, 0:LANES, BLK_STATE:] = (-ctim).astype(bf)
        wct_ref[k, LANES:, 0:BLK_STATE] = (ctre * ar - ctim * ai).astype(bf)
        wct_ref[k, LANES:, BLK_STATE:] = (-(ctre * ai + ctim * ar)).astype(bf)
        wdm = _dot3_nt(bb_re, ctre) - _dot3_nt(bb_im, ctim)
        wdm_ref[k] = wdm.astype(bf)


def _discretize(lr, li, ldt, bre, bim, ctre, ctim):
    f32, bf = jnp.float32, jnp.bfloat16
    return pl.pallas_call(
        _discretize_kernel,
        out_shape=[
            jax.ShapeDtypeStruct((N_BLK, BLK_STATE), f32),
            jax.ShapeDtypeStruct((N_BLK, BLK_STATE), f32),
            jax.ShapeDtypeStruct((N_TILE, SUBLANES, LANES), f32),
            jax.ShapeDtypeStruct((N_TILE, SUBLANES, LANES), f32),
            jax.ShapeDtypeStruct((N_BLK, 2 * LANES, 2 * BLK_STATE), bf),
            jax.ShapeDtypeStruct((N_BLK, 2 * LANES, 2 * BLK_STATE), bf),
            jax.ShapeDtypeStruct((N_BLK, LANES, LANES), bf),
        ],
        name="ssm_discretize",
    )(lr, li, ldt, bre, bim, ctre, ctim)


def _ffn1_mixer_kernel(x_ref, xd_ref, n1_ref, wg_c, wu_c, wd_c, nm_ref, win_c,
                       a2re_ref, a2im_ref, wb_ref, wct_ref, wdm_ref, d_ref, wglu_ref, bglu_ref, cw_ref,
                       h_ref, mix_ref, sre_ref, sim_ref, nconv_ref, ud_ref, vd_ref, gd_ref,
                       xs0_ref, xs1_ref, xs2_ref, xs3_ref, us_ref, ys_ref, l_ref, yo_ref, ye_ref, e_ref,
                       ecarry_ref, v_ref, g_ref, state_ref, wg_ref, wu_ref, wd_ref, win_ref, a_ref, xdd_ref,
                       wg_s, wu_s, wd_s, win_s, cast_sem):
    xs_refs = (xs0_ref, xs1_ref, xs2_ref, xs3_ref)
    step = pl.program_id(0)
    last = pl.num_programs(0) - 1
    nb, t_len, _ = x_ref.shape
    n_dec = xd_ref.shape[0]
    prow = nb * PAIR_PITCH
    tiles = [(k, j) for k in range(N_BLK) for j in range(HALF)]

    @pl.when(step < W1_STEPS)
    def _load_weights():
        _stream_cast(((wg_c, wg_s, wg_ref), (wu_c, wu_s, wu_ref), (wd_c, wd_s, wd_ref),
                      (win_c, win_s, win_ref)), cast_sem, W1_CHUNKS)

    @pl.when(step == 0)
    def _init():
        for ref in (state_ref, us_ref, l_ref, e_ref, ecarry_ref, v_ref, g_ref):
            ref[...] = jnp.zeros_like(ref)
        xdd_ref[...] = xd_ref[...]

    @pl.when(step == last)
    def _decode_rows_in():
        db = n_dec // t_len
        x_ref[0:db] = xdd_ref[...].reshape(db, t_len, D_MODEL)

    @pl.when(step >= W1_STEPS)
    def _main():
        carry = {}

        def build_lhs():
            for b in range(nb):
                for k in range(N_BLK):
                    for par in range(2):
                        col = (2 * k + par) * LANES
                        l_ref[pl.ds(b * PAIR_PITCH, N_PAIR), col:col + LANES] = (
                            us_ref[k, pl.ds(b * t_len + par, N_PAIR, stride=2), :])

        def b_proj(ks):
            for k in ks:
                lhs = l_ref[:, 2 * k * LANES:2 * (k + 1) * LANES].astype(jnp.bfloat16)
                v = jnp.dot(lhs, wb_ref[k], preferred_element_type=jnp.float32)
                for j in range(SLABS_PER_BLK):
                    xs_refs[k][j] = v[:, j * LANES:(j + 1) * LANES]

        def scan(r0, r1):
            if r0 == 0:
                carry["re"] = [state_ref[k * SLABS_PER_BLK + j] for k, j in tiles]
                carry["im"] = [state_ref[k * SLABS_PER_BLK + HALF + j] for k, j in tiles]
            xr, xi = carry["re"], carry["im"]
            for r in range(r0, r1):
                rows = pl.ds(r, SUBLANES, stride=PAIR_PITCH)
                for n, (k, j) in enumerate(tiles):
                    a_re = a2re_ref[k * HALF + j]
                    a_im = a2im_ref[k * HALF + j]
                    nr = (a_re * xr[n] + xs_refs[k][j, rows, :]) - a_im * xi[n]
                    ni = (a_re * xi[n] + xs_refs[k][HALF + j, rows, :]) + a_im * xr[n]
                    xs_refs[k][j, rows, :] = nr
                    xs_refs[k][HALF + j, rows, :] = ni
                    xr[n], xi[n] = nr, ni
            if r1 == N_PAIR:
                for n, (k, j) in enumerate(tiles):
                    state_ref[k * SLABS_PER_BLK + j] = xr[n]
                    state_ref[k * SLABS_PER_BLK + HALF + j] = xi[n]

        def c_proj(ks):
            for k in ks:
                lanes = slice(k * LANES, (k + 1) * LANES)
                x = jnp.concatenate([xs_refs[k][j] for j in range(SLABS_PER_BLK)], axis=1)
                ce = lax.dot_general(x.astype(jnp.bfloat16), wct_ref[k], _NT,
                                     preferred_element_type=jnp.float32)
                yo_ref[:, lanes] = ce[:, 0:LANES]
                e_ref[k, SUBLANES:SUBLANES + prow, :] = ce[:, LANES:]
                nxt = e_ref[k, pl.ds(SUBLANES + N_PAIR - 1, nb, stride=PAIR_PITCH), :]
                e_ref[k, pl.ds(SUBLANES - 1, nb, stride=PAIR_PITCH), :] = ecarry_ref[k]
                ecarry_ref[k] = nxt
                u_even = l_ref[:, 2 * k * LANES:(2 * k + 1) * LANES].astype(jnp.bfloat16)
                ye_ref[:, lanes] = (e_ref[k, SUBLANES - 1:SUBLANES - 1 + prow, :]
                                    + jnp.dot(u_even, wdm_ref[k], preferred_element_type=jnp.float32))

        def glu_emit():
            for b in range(nb):
                for k in range(N_BLK):
                    lanes = slice(k * LANES, (k + 1) * LANES)
                    for par, src in ((0, ye_ref), (1, yo_ref)):
                        ys_ref[k, pl.ds(b * t_len + par, N_PAIR, stride=2), :] = (
                            src[pl.ds(b * PAIR_PITCH, N_PAIR), lanes])
            y = jnp.concatenate([ys_ref[k] for k in range(N_BLK)], axis=1)
            u = jnp.concatenate([us_ref[k] for k in range(N_BLK)], axis=1)
            s_out = _glu_tail(y, u, d_ref, wglu_ref, bglu_ref)
            mix_ref[:, :, 0:SSM_WIDTH] = s_out.reshape(nb, t_len, SSM_WIDTH).astype(mix_ref.dtype)

        def conv():
            v = v_ref[:, SUBLANES:SUBLANES + t_len, :]
            z = cw_ref[0] * v_ref[:, SUBLANES - 2:SUBLANES - 2 + t_len, :]
            z = z + cw_ref[1] * v_ref[:, SUBLANES - 1:SUBLANES - 1 + t_len, :]
            z = z + cw_ref[2] * v
            mix_ref[:, :, SSM_WIDTH:] = (g_ref[...] * z).astype(mix_ref.dtype)
            v_ref[:, 0:SUBLANES, :] = v_ref[:, t_len:t_len + SUBLANES, :]

        q = N_PAIR // 4
        pieces = [
            lambda: (conv(), build_lhs(), b_proj((0,))), lambda: b_proj((1,)), lambda: b_proj((2,)),
            lambda: b_proj((3,)),
            lambda: scan(0, q), lambda: scan(q, 2 * q), lambda: scan(2 * q, 3 * q), lambda: scan(3 * q, N_PAIR),
            lambda: c_proj((0, 1)), lambda: c_proj((2, 3)),
            glu_emit,
        ]
        assert len(pieces) <= D_FF // FFN_TF

        x = x_ref[...].reshape(nb * t_len, D_MODEL)
        xn = _rms(x, n1_ref[...]).astype(jnp.bfloat16)
        for c, j in enumerate(range(0, D_FF, FFN_TF)):
            _swiglu_stage(xn, wg_ref, wu_ref, a_ref, j, j + FFN_TF)
            if c < len(pieces):
                pieces[c]()
        h = x + 0.5 * jnp.dot(a_ref[...], wd_ref[...], preferred_element_type=jnp.float32)
        hn = _rms(h, nm_ref[...]).astype(jnp.bfloat16)
        p = jnp.dot(hn, win_ref[...], preferred_element_type=jnp.float32)
        h_ref[...] = h.reshape(nb, t_len, D_MODEL)
        v_new = p[:, SSM_WIDTH + 2 * CONV_WIDTH:] * p[:, SSM_WIDTH:SSM_WIDTH + CONV_WIDTH]
        v_ref[:, SUBLANES:SUBLANES + t_len, :] = v_new.reshape(nb, t_len, CONV_WIDTH)
        g_ref[...] = p[:, SSM_WIDTH + CONV_WIDTH:SSM_WIDTH + 2 * CONV_WIDTH].reshape(nb, t_len, CONV_WIDTH)
        for k in range(N_BLK):
            us_ref[k] = p[:, k * LANES:(k + 1) * LANES]

    @pl.when(step == last)
    def _final():
        for n, (k, j) in enumerate(tiles):
            for out_ref, slab in ((sre_ref, k * SLABS_PER_BLK + j), (sim_ref, k * SLABS_PER_BLK + HALF + j)):
                tile = state_ref[slab]
                for half in range(LANES // SSM_STATE):
                    group = (k * BLK_STATE + j * LANES) // SSM_STATE + half
                    out_ref[:, group, :] = tile[:, half * SSM_STATE:(half + 1) * SSM_STATE]
        nconv_ref[...] = v_ref[:, SUBLANES - 2:SUBLANES, :]
        for k in range(N_BLK):
            ud_ref[:, k * LANES:(k + 1) * LANES] = us_ref[k, 0:n_dec, :]
        for b in range(n_dec // t_len):
            vd_ref[b * t_len:(b + 1) * t_len, :] = v_ref[b, SUBLANES:SUBLANES + t_len, :]
            gd_ref[b * t_len:(b + 1) * t_len, :] = g_ref[b]


def _ffn1_mixer(x, xd, n1, wg, wu, wd, nm, win, a2_re, a2_im, wb, wct, wdm, d, wglu, bglu, cw):
    nb, seq, _ = x.shape
    n_dec = xd.shape[0]
    n_chunk = seq // SCAN_T
    prow = nb * PAIR_PITCH
    f32, bf = jnp.float32, jnp.bfloat16
    main = lambda i: jnp.maximum(i - W1_STEPS, 0)
    return pl.pallas_call(
        _ffn1_mixer_kernel,
        grid=(W1_STEPS + n_chunk + 1,),
        in_specs=[
            pl.BlockSpec((nb, SCAN_T, D_MODEL), lambda i: (0, jnp.minimum(main(i), n_chunk - 1), 0)),
            pl.BlockSpec((n_dec, None, D_MODEL), lambda i: (0, 0, 0), pipeline_mode=pl.Buffered(1)),
            _whole((1, D_MODEL)),
            _IN_HBM,
            _IN_HBM,
            _IN_HBM,
            _whole((1, D_MODEL)),
            _IN_HBM,
            _whole((N_TILE, SUBLANES, LANES)),
            _whole((N_TILE, SUBLANES, LANES)),
            _whole((N_BLK, 2 * LANES, 2 * BLK_STATE)),
            _whole((N_BLK, 2 * LANES, 2 * BLK_STATE)),
            _whole((N_BLK, LANES, LANES)),
            _whole((1, SSM_WIDTH)),
            _whole((SSM_WIDTH, SSM_WIDTH)),
            _whole((1, SSM_WIDTH)),
            _whole((CONV_K, 1, CONV_WIDTH)),
        ],
        out_specs=[
            pl.BlockSpec((nb, SCAN_T, D_MODEL), lambda i: (0, main(i), 0)),
            pl.BlockSpec((nb, SCAN_T, D_MODEL), lambda i: (0, jnp.clip(main(i) - 1, 0, n_chunk - 1), 0)),
            pl.BlockSpec((nb, N_SSM_GROUPS, SSM_STATE), lambda i: (0, 0, 0)),
            pl.BlockSpec((nb, N_SSM_GROUPS, SSM_STATE), lambda i: (0, 0, 0)),
            pl.BlockSpec((nb, CONV_K - 1, CONV_WIDTH), lambda i: (0, 0, 0)),
            pl.BlockSpec((n_dec, SSM_WIDTH), lambda i: (0, 0)),
            pl.BlockSpec((n_dec, CONV_WIDTH), lambda i: (0, 0)),
            pl.BlockSpec((n_dec, CONV_WIDTH), lambda i: (0, 0)),
        ],
        out_shape=[
            jax.ShapeDtypeStruct((nb, seq + SCAN_T, D_MODEL), f32),
            jax.ShapeDtypeStruct((nb, seq, D_MODEL), bf),
            jax.ShapeDtypeStruct((nb, N_SSM_GROUPS, SSM_STATE), f32),
            jax.ShapeDtypeStruct((nb, N_SSM_GROUPS, SSM_STATE), f32),
            jax.ShapeDtypeStruct((nb, CONV_K - 1, CONV_WIDTH), f32),
            jax.ShapeDtypeStruct((n_dec, SSM_WIDTH), f32),
            jax.ShapeDtypeStruct((n_dec, CONV_WIDTH), f32),
            jax.ShapeDtypeStruct((n_dec, CONV_WIDTH), f32),
        ],
        scratch_shapes=[
            *[pltpu.VMEM((SLABS_PER_BLK, prow, LANES), f32) for _ in range(N_BLK)],
            pltpu.VMEM((N_BLK, nb * SCAN_T, LANES), f32),
            pltpu.VMEM((N_BLK, nb * SCAN_T, LANES), f32),
            pltpu.VMEM((prow, 2 * SSM_WIDTH), f32),
            pltpu.VMEM((prow, SSM_WIDTH), f32),
            pltpu.VMEM((prow, SSM_WIDTH), f32),
            pltpu.VMEM((N_BLK, SUBLANES + prow, LANES), f32),
            pltpu.VMEM((N_BLK, SUBLANES, LANES), f32),
            pltpu.VMEM((nb, SCAN_T + SUBLANES, CONV_WIDTH), f32),
            pltpu.VMEM((nb, SCAN_T, CONV_WIDTH), f32),
            pltpu.VMEM((2 * N_TILE, SUBLANES, LANES), f32),
            pltpu.VMEM((D_MODEL, D_FF), bf),
            pltpu.VMEM((D_MODEL, D_FF), bf),
            pltpu.VMEM((D_FF, D_MODEL), bf),
            pltpu.VMEM((D_MODEL, IN_PROJ_WIDTH), bf),
            pltpu.VMEM((nb * SCAN_T, D_FF), bf),
            pltpu.VMEM((n_dec, D_MODEL), f32),
            _staging(D_MODEL // W1_CHUNKS, D_FF),
            _staging(D_MODEL // W1_CHUNKS, D_FF),
            _staging(D_FF // W1_CHUNKS, D_MODEL),
            _staging(D_MODEL // W1_CHUNKS, IN_PROJ_WIDTH),
            pltpu.SemaphoreType.DMA((8,)),
        ],
        compiler_params=pltpu.CompilerParams(
            dimension_semantics=("arbitrary",), vmem_limit_bytes=VMEM_LIMIT),
        name="ffn1_mixer",
    )(x, xd, n1, wg, wu, wd, nm, win, a2_re, a2_im, wb, wct, wdm, d, wglu, bglu, cw)


def _ffn2_rows(h, mix, weights, a_ref):
    wout_ref, n2_ref, wg_ref, wu_ref, wd_ref, nf_ref = weights
    h = h + jnp.dot(mix, wout_ref[...], preferred_element_type=jnp.float32)
    hn = _rms(h, n2_ref[...]).astype(jnp.bfloat16)
    h = h + 0.5 * _swiglu_staged(hn, wg_ref, wu_ref, wd_ref, a_ref)
    return _rms(h, nf_ref[...])


def _ffn2_kernel(h_ref, mix_ref, hd0_ref, hd1_ref, mixd_ref, wout_c, wg_c, wu_c, wd_c, n2_ref, nf_ref,
                 y_ref, yd_ref, wout_ref, wg_ref, wu_ref, wd_ref, a0_ref, a1_ref,
                 wout_s, wg_s, wu_s, wd_s, cast_sem):
    step = pl.program_id(0)
    last = pl.num_programs(0) - 1
    weights = (wout_ref, n2_ref, wg_ref, wu_ref, wd_ref, nf_ref)

    @pl.when(step < W2_STEPS)
    def _load_weights():
        _stream_cast(((wout_c, wout_s, wout_ref), (wg_c, wg_s, wg_ref), (wu_c, wu_s, wu_ref),
                      (wd_c, wd_s, wd_ref)), cast_sem, W2_CHUNKS)

    @pl.when((step >= W2_STEPS) & (step < last))
    def _prompt_tile():
        assert h_ref.shape[0] == 2 * FFN_SUB
        ra, rb = slice(0, FFN_SUB), slice(FFN_SUB, 2 * FFN_SUB)
        h_a = h_ref[ra, :] + jnp.dot(mix_ref[ra, :], wout_ref[...], preferred_element_type=jnp.float32)
        h_b = h_ref[rb, :] + jnp.dot(mix_ref[rb, :], wout_ref[...], preferred_element_type=jnp.float32)
        hn_a = _rms(h_a, n2_ref[...]).astype(jnp.bfloat16)
        _swiglu_stage(hn_a, wg_ref, wu_ref, a0_ref, 0, FFN_TF)
        hn_b = _rms(h_b, n2_ref[...]).astype(jnp.bfloat16)
        _swiglu_stage(hn_a, wg_ref, wu_ref, a0_ref, FFN_TF, D_FF)
        f_a = jnp.dot(a0_ref[...], wd_ref[...], preferred_element_type=jnp.float32)
        _swiglu_stage(hn_b, wg_ref, wu_ref, a1_ref, 0, FFN_TF)
        y_ref[ra, :] = _rms(h_a + 0.5 * f_a, nf_ref[...])
        _swiglu_stage(hn_b, wg_ref, wu_ref, a1_ref, FFN_TF, D_FF)
        f_b = jnp.dot(a1_ref[...], wd_ref[...], preferred_element_type=jnp.float32)
        y_ref[rb, :] = _rms(h_b + 0.5 * f_b, nf_ref[...])

    @pl.when(step == last)
    def _decode_rows():
        h_d = jnp.concatenate([hd0_ref[...], hd1_ref[...]], axis=0)
        yd_ref[...] = _ffn2_rows(h_d, mixd_ref[...], weights, a0_ref)


def _ffn2(h_all, mix, mixd, wout, n2, wg, wu, wd, nf):
    nb, seq, _ = mix.shape
    seq_all = h_all.shape[1]
    n_dec = mixd.shape[0]
    tail = seq_all - seq
    assert n_dec == 2 * tail
    per_b = seq // FFN_TM
    n_tiles = nb * per_b
    h_all = h_all.reshape(nb * seq_all, D_MODEL)
    mix = mix.reshape(nb * seq, D_MODEL)
    idx = lambda i: jnp.clip(i - W2_STEPS, 0, n_tiles - 1)
    tile = lambda i: (idx(i), 0)
    assert seq_all % SUBLANES == 0
    tile_all = lambda i: (pl.multiple_of((idx(i) // per_b) * seq_all + (idx(i) % per_b) * FFN_TM, SUBLANES), 0)
    yp, yd = pl.pallas_call(
        _ffn2_kernel,
        grid=(W2_STEPS + n_tiles + 1,),
        in_specs=[
            pl.BlockSpec((pl.Element(FFN_TM), pl.Element(D_MODEL)), tile_all),
            pl.BlockSpec((FFN_TM, D_MODEL), tile),
            pl.BlockSpec((pl.Element(tail), pl.Element(D_MODEL)), lambda i: (seq, 0),
                         pipeline_mode=pl.Buffered(1)),
            pl.BlockSpec((pl.Element(tail), pl.Element(D_MODEL)), lambda i: (seq_all + seq, 0),
                         pipeline_mode=pl.Buffered(1)),
            _whole(mixd.shape),
            _IN_HBM,
            _IN_HBM,
            _IN_HBM,
            _IN_HBM,
            _whole((1, D_MODEL)),
            _whole((1, D_MODEL)),
        ],
        out_specs=[
            pl.BlockSpec((FFN_TM, D_MODEL), tile),
            pl.BlockSpec((n_dec, None, D_MODEL), lambda i: (0, 0, 0)),
        ],
        out_shape=[
            jax.ShapeDtypeStruct((nb * seq, D_MODEL), jnp.float32),
            jax.ShapeDtypeStruct((n_dec, 1, D_MODEL), jnp.float32),
        ],
        scratch_shapes=[
            pltpu.VMEM((D_MODEL, D_MODEL), jnp.bfloat16),
            pltpu.VMEM((D_MODEL, D_FF), jnp.bfloat16),
            pltpu.VMEM((D_MODEL, D_FF), jnp.bfloat16),
            pltpu.VMEM((D_FF, D_MODEL), jnp.bfloat16),
            pltpu.VMEM((FFN_SUB, D_FF), jnp.bfloat16),
            pltpu.VMEM((FFN_SUB, D_FF), jnp.bfloat16),
            _staging(D_MODEL // W2_CHUNKS, D_MODEL),
            _staging(D_MODEL // W2_CHUNKS, D_FF),
            _staging(D_MODEL // W2_CHUNKS, D_FF),
            _staging(D_FF // W2_CHUNKS, D_MODEL),
            pltpu.SemaphoreType.DMA((8,)),
        ],
        compiler_params=pltpu.CompilerParams(
            dimension_semantics=("arbitrary",), vmem_limit_bytes=VMEM_LIMIT),
        name="outproj_ffn2",
    )(h_all, mix, h_all, h_all, mixd, wout, wg, wu, wd, n2, nf)
    return yp.reshape(nb, seq, D_MODEL), yd


def _mixer_step_kernel(u_ref, v_ref, g_ref, hre_ref, him_ref, buf_ref, are_ref, aim_ref, bb_ref, wct_ref,
                       d_ref, wglu_ref, bglu_ref, cw_ref, mix_ref, sre_ref, sim_ref, nconv_ref):
    u = u_ref[...]
    u_bf = u.astype(jnp.bfloat16)
    ys = []
    for k in range(N_BLK):
        bu = jnp.dot(u_bf[:, k * LANES:(k + 1) * LANES], bb_ref[k], preferred_element_type=jnp.float32)
        st = slice(k * BLK_STATE, (k + 1) * BLK_STATE)
        a_re = are_ref[k:k + 1, :]
        a_im = aim_ref[k:k + 1, :]
        h_re = hre_ref[st, :].T
        h_im = him_ref[st, :].T
        x_re = (a_re * h_re + bu[:, :BLK_STATE]) - a_im * h_im
        x_im = (a_re * h_im + bu[:, BLK_STATE:]) + a_im * h_re
        sre_ref[st, :] = x_re.T
        sim_ref[st, :] = x_im.T
        x = jnp.concatenate([x_re, x_im], axis=1).astype(jnp.bfloat16)
        ys.append(lax.dot_general(x, wct_ref[k], _NT, preferred_element_type=jnp.float32))
    y = jnp.concatenate(ys, axis=1)
    mix_ref[:, 0:SSM_WIDTH] = _glu_tail(y, u, d_ref, wglu_ref, bglu_ref).astype(mix_ref.dtype)

    v = v_ref[...]
    buf1 = buf_ref[:, 1, :]
    z = cw_ref[0] * buf_ref[:, 0, :] + cw_ref[1] * buf1 + cw_ref[2] * v
    mix_ref[:, SSM_WIDTH:] = (g_ref[...] * z).astype(mix_ref.dtype)
    nconv_ref[:, 0, :] = buf1
    nconv_ref[:, 1, :] = v


def _mixer_step(u, v, g, h_re, h_im, conv_state, a_re, a_im, wb, wct, d, wglu, bglu, cw):
    nb = u.shape[0]
    conv_spec = pl.BlockSpec((None, nb, CONV_K - 1, CONV_WIDTH), lambda i: (0, 0, 0, 0))
    half = (N_BLK, LANES, 2 * BLK_STATE)
    bb_spec = pl.BlockSpec(half, lambda i: (0, 1, 0), pipeline_mode=pl.Buffered(1))
    c_spec = pl.BlockSpec(half, lambda i: (0, 0, 0), pipeline_mode=pl.Buffered(1))
    rows = (u, v, g, h_re, h_im)
    tail = (d, wglu, bglu, cw)
    return pl.pallas_call(
        _mixer_step_kernel,
        grid=(1,),
        in_specs=([_whole(a.shape) for a in rows] + [conv_spec] + [_whole(a_re.shape), _whole(a_im.shape)]
                  + [bb_spec, c_spec] + [_whole(a.shape) for a in tail]),
        out_specs=[
            pl.BlockSpec((nb, D_MODEL), lambda i: (0, 0)),
            pl.BlockSpec((N_STATE, nb), lambda i: (0, 0)),
            pl.BlockSpec((N_STATE, nb), lambda i: (0, 0)),
            conv_spec,
        ],
        out_shape=[
            jax.ShapeDtypeStruct((nb, D_MODEL), jnp.bfloat16),
            jax.ShapeDtypeStruct((N_STATE, nb), jnp.float32),
            jax.ShapeDtypeStruct((N_STATE, nb), jnp.float32),
            jax.ShapeDtypeStruct(conv_state.shape, jnp.float32),
        ],
        compiler_params=pltpu.CompilerParams(
            dimension_semantics=("arbitrary",), vmem_limit_bytes=VMEM_LIMIT),
        name="mixer_step",
    )(*rows, conv_state, a_re, a_im, wb, wct, *tail)


def _ssm_layout(lam_re, lam_im, log_dt, b_re, b_im, c_re, c_im):
    gpb = N_SSM_GROUPS // N_BLK

    def b_rows(b):
        return b.reshape(N_BLK, gpb, SSM_STATE, SSM_GROUP).transpose(0, 1, 3, 2).reshape(N_BLK, LANES, SSM_STATE)

    def c_rows(c):
        return c.reshape(N_BLK, LANES, SSM_STATE)

    return lam_re, lam_im, log_dt, b_rows(b_re), b_rows(b_im), c_rows(c_re), c_rows(c_im)


def kernel(x_prompt, x_sample, state_ssm_re, state_ssm_im, state_conv, ffn1_norm, ffn1_w_gate, ffn1_w_up, ffn1_w_down, mix_norm, w_in, ssm_lambda_re, ssm_lambda_im, ssm_log_dt, ssm_b_re, ssm_b_im, ssm_c_re, ssm_c_im, ssm_d, ssm_w_glu, ssm_b_glu, conv_w, w_out, ffn2_norm, ffn2_w_gate, ffn2_w_up, ffn2_w_down, final_norm):
    assert ffn1_norm.shape[0] == 1, "single-layer trunk"
    n_prompt, seq, _ = x_prompt.shape
    n_dec, dec_seq, _ = x_sample.shape
    assert dec_seq == 1 and n_prompt == SUBLANES and seq % FFN_TM == 0 and seq % SCAN_T == 0
    assert n_dec % SCAN_T == 0 and n_dec <= n_prompt * SCAN_T and FFN_TM % FFN_SUB == 0

    n1 = ffn1_norm[0][None, :]
    nm = mix_norm[0][None, :]
    n2 = ffn2_norm[0][None, :]
    nf = final_norm[None, :]
    wglu = ssm_w_glu[0]
    layout = _ssm_layout(ssm_lambda_re[0], ssm_lambda_im[0], ssm_log_dt, ssm_b_re[0], ssm_b_im[0],
                         ssm_c_re[0], ssm_c_im[0])
    a_re, a_im, a2_re, a2_im, wb, wct, wdm = _discretize(*layout)
    d = ssm_d[0][None, :]
    bglu = ssm_b_glu[0][None, :]
    cw = conv_w.transpose(1, 0, 2)

    h_all, mixp, p_re, p_im, p_conv, ud, vd, gd = _ffn1_mixer(
        x_prompt, x_sample, n1, ffn1_w_gate[0], ffn1_w_up[0], ffn1_w_down[0], nm,
        w_in[0], a2_re, a2_im, wb, wct, wdm, d, wglu, bglu, cw)

    s_re0 = state_ssm_re[0].transpose(1, 2, 0).reshape(N_STATE, n_dec)
    s_im0 = state_ssm_im[0].transpose(1, 2, 0).reshape(N_STATE, n_dec)
    mixd, s_re, s_im, s_conv = _mixer_step(ud, vd, gd, s_re0, s_im0, state_conv, a_re, a_im, wb, wct,
                                           d, wglu, bglu, cw)

    yp, yd = _ffn2(h_all, mixp, mixd, w_out[0], n2, ffn2_w_gate[0], ffn2_w_up[0], ffn2_w_down[0], nf)

    gs = (N_SSM_GROUPS, SSM_STATE)
    return (yp,
            yd,
            p_re[None],
            p_im[None],
            p_conv[None],
            s_re.reshape(*gs, n_dec).transpose(2, 0, 1)[None],
            s_im.reshape(*gs, n_dec).transpose(2, 0, 1)[None],
            s_conv)
```

```python
import jax
import jax.numpy as jnp
from jax import lax
from jax.experimental import pallas as pl
from jax.experimental.pallas import tpu as pltpu

D_MODEL = 1024
D_FF = 2816
SSM_WIDTH = 512
CONV_WIDTH = 512
SSM_GROUP = 16
N_SSM_GROUPS = 32
SSM_STATE = 64
CONV_K = 3
IN_PROJ_WIDTH = SSM_WIDTH + 3 * CONV_WIDTH
EPS = 1e-6

LANES = 128
SUBLANES = 8
N_STATE = N_SSM_GROUPS * SSM_STATE
N_BLK = SSM_WIDTH // LANES
BLK_STATE = N_STATE // N_BLK
SLABS_PER_BLK = 2 * BLK_STATE // LANES
HALF = SLABS_PER_BLK // 2
N_TILE = N_BLK * HALF

FFN_TM = 1024
FFN_SUB = 512
W1_STEPS = 1
W2_STEPS = 1
W1_CHUNKS = 16
W2_CHUNKS = 8
FFN_TF = 256
SCAN_T = 64
N_PAIR = SCAN_T // 2
PAIR_PITCH = N_PAIR + 4
VMEM_LIMIT = 62 * 1024 * 1024

_NT = (((1,), (1,)), ((), ()))


def _rms(x, g):
    r = lax.rsqrt(jnp.mean(x * x, axis=-1, keepdims=True) + EPS)
    return x * r * g


def _bdot(a, b):
    return jnp.dot(a.astype(jnp.bfloat16), b.astype(jnp.bfloat16), preferred_element_type=jnp.float32)


def _swiglu_stage(xn, wg_ref, wu_ref, a_ref, lo, hi):
    rows = xn.shape[0]
    for j in range(lo, hi, FFN_TF):
        g = jnp.dot(xn, wg_ref[:, j:j + FFN_TF], preferred_element_type=jnp.float32)
        u = jnp.dot(xn, wu_ref[:, j:j + FFN_TF], preferred_element_type=jnp.float32)
        a_ref[0:rows, j:j + FFN_TF] = ((g * jax.nn.sigmoid(g)) * u).astype(a_ref.dtype)


def _swiglu_staged(xn, wg_ref, wu_ref, wd_ref, a_ref):
    rows = xn.shape[0]
    _swiglu_stage(xn, wg_ref, wu_ref, a_ref, 0, D_FF)
    return jnp.dot(a_ref[0:rows, :], wd_ref[...], preferred_element_type=jnp.float32)


def _stream_cast(triples, sem, n_chunks):
    def copy(w, c, slot):
        src, buf, _ = triples[w]
        n = buf.shape[1]
        return pltpu.make_async_copy(src.at[pl.ds(pl.multiple_of(c * n, SUBLANES), n), :], buf.at[slot],
                                     sem.at[2 * w + slot])

    for w in range(len(triples)):
        copy(w, 0, 0).start()

    def body(c, carry):
        slot = lax.rem(c, 2)

        @pl.when(c + 1 < n_chunks)
        def _prefetch():
            for w in range(len(triples)):
                copy(w, c + 1, 1 - slot).start()

        for w, (_, buf, dst) in enumerate(triples):
            copy(w, c, slot).wait()
            n = buf.shape[1]
            dst[pl.ds(pl.multiple_of(c * n, SUBLANES), n), :] = buf[slot].astype(dst.dtype)
        return carry

    lax.fori_loop(0, n_chunks, body, 0)


def _staging(rows, cols):
    return pltpu.VMEM((2, rows, cols), jnp.float32)


def _whole(shape):
    return pl.BlockSpec(shape, lambda i: (0,) * len(shape), pipeline_mode=pl.Buffered(1))


_IN_HBM = pl.BlockSpec(memory_space=pl.ANY)


def _glu_tail(y, u, d_ref, wglu_ref, bglu_ref):
    y = y + d_ref[...] * u
    z = jax.nn.gelu(y)
    gate = jax.nn.sigmoid(_bdot(z, wglu_ref[...]) + bglu_ref[...])
    return z * gate


def _dot3_nt(a, b):
    bf = jnp.bfloat16
    a_hi, b_hi = a.astype(bf), b.astype(bf)
    a_lo = (a - a_hi.astype(jnp.float32)).astype(bf)
    b_lo = (b - b_hi.astype(jnp.float32)).astype(bf)
    dot = lambda p, q: lax.dot_general(p, q, _NT, preferred_element_type=jnp.float32)
    return dot(a_hi, b_hi) + (dot(a_hi, b_lo) + dot(a_lo, b_hi))


def _discretize_kernel(lr_ref, li_ref, ldt_ref, bre_ref, bim_ref, ctre_ref, ctim_ref,
                       are_ref, aim_ref, a2re_ref, a2im_ref, wb_ref, wct_ref, wdm_ref):
    row_group = lax.broadcasted_iota(jnp.int32, (LANES, BLK_STATE), 0) // SSM_GROUP
    col_group = lax.broadcasted_iota(jnp.int32, (LANES, BLK_STATE), 1) // SSM_STATE
    on_diagonal = row_group == col_group

    def blocks(m):
        pair = jnp.concatenate([m, m], axis=1)
        return jnp.where(on_diagonal, jnp.concatenate([pair] * (BLK_STATE // LANES), axis=1), 0.0)

    gpb = N_SSM_GROUPS // N_BLK

    def state_rows(ref):
        return jnp.concatenate(
            [jnp.concatenate([ref[k * gpb + g:k * gpb + g + 1, :] for g in range(gpb)], axis=1)
             for k in range(N_BLK)], axis=0)

    def per_state(row_ref):
        vals = row_ref[...]
        col_group = lax.broadcasted_iota(jnp.int32, (1, BLK_STATE), 1) // SSM_STATE
        rows = []
        for k in range(N_BLK):
            row = jnp.zeros((1, BLK_STATE), jnp.float32)
            for g in range(gpb):
                row = jnp.where(col_group == g, vals[:, k * gpb + g:k * gpb + g + 1], row)
            rows.append(row)
        return jnp.concatenate(rows, axis=0)

    lr, li = state_rows(lr_ref), state_rows(li_ref)
    dt = jnp.exp(per_state(ldt_ref))
    mag = jnp.exp(lr * dt)
    ab_re = mag * jnp.cos(li * dt)
    ab_im = mag * jnp.sin(li * dt)
    den = lr * lr + li * li
    nr = ab_re - 1.0
    ni = ab_im
    coef_re = (nr * lr + ni * li) / den
    coef_im = (ni * lr - nr * li) / den
    are_ref[...] = ab_re
    aim_ref[...] = ab_im
    a2_re = ab_re * ab_re - ab_im * ab_im
    a2_im = 2.0 * (ab_re * ab_im)
    bf = jnp.bfloat16
    for k in range(N_BLK):
        ar, ai = ab_re[k:k + 1, :], ab_im[k:k + 1, :]
        for j in range(HALF):
            lanes = slice(j * LANES, (j + 1) * LANES)
            a2re_ref[k * HALF + j] = jnp.broadcast_to(a2_re[k:k + 1, lanes], (SUBLANES, LANES))
            a2im_ref[k * HALF + j] = jnp.broadcast_to(a2_im[k:k + 1, lanes], (SUBLANES, LANES))
        cr, ci = coef_re[k:k + 1, :], coef_im[k:k + 1, :]
        bre, bim = blocks(bre_ref[k]), blocks(bim_ref[k])
        bb_re = cr * bre - ci * bim
        bb_im = cr * bim + ci * bre
        wb_ref[k, 0:LANES, 0:BLK_STATE] = (ar * bb_re - ai * bb_im).astype(bf)
        wb_ref[k, 0:LANES, BLK_STATE:] = (ar * bb_im + ai * bb_re).astype(bf)
        wb_ref[k, LANES:, 0:BLK_STATE] = bb_re.astype(bf)
        wb_ref[k, LANES:, BLK_STATE:] = bb_im.astype(bf)
        ctre, ctim = blocks(ctre_ref[k]), blocks(ctim_ref[k])
        wct_ref[k, 0:LANES, 0:BLK_STATE] = ctre.astype(bf)
        wct_ref[k, 0:LANES, BLK_STATE:] = (-ctim).astype(bf)
        wct_ref[k, LANES:, 0:BLK_STATE] = (ctre * ar - ctim * ai).astype(bf)
        wct_ref[k, LANES:, BLK_STATE:] = (-(ctre * ai + ctim * ar)).astype(bf)
        wdm = _dot3_nt(bb_re, ctre) - _dot3_nt(bb_im, ctim)
        wdm_ref[k] = wdm.astype(bf)


def _discretize(lr, li, ldt, bre, bim, ctre, ctim):
    f32, bf = jnp.float32, jnp.bfloat16
    return pl.pallas_call(
        _discretize_kernel,
        out_shape=[
            jax.ShapeDtypeStruct((N_BLK, BLK_STATE), f32),
            jax.ShapeDtypeStruct((N_BLK, BLK_STATE), f32),
            jax.ShapeDtypeStruct((N_TILE, SUBLANES, LANES), f32),
            jax.ShapeDtypeStruct((N_TILE, SUBLANES, LANES), f32),
            jax.ShapeDtypeStruct((N_BLK, 2 * LANES, 2 * BLK_STATE), bf),
            jax.ShapeDtypeStruct((N_BLK, 2 * LANES, 2 * BLK_STATE), bf),
            jax.ShapeDtypeStruct((N_BLK, LANES, LANES), bf),
        ],
        name="ssm_discretize",
    )(lr, li, ldt, bre, bim, ctre, ctim)


def _ffn1_mixer_kernel(x_ref, xd_ref, n1_ref, wg_c, wu_c, wd_c, nm_ref, win_c,
                       a2re_ref, a2im_ref, wb_ref, wct_ref, wdm_ref, d_ref, wglu_ref, bglu_ref, cw_ref,
                       h_ref, mix_ref, sre_ref, sim_ref, nconv_ref, ud_ref, vd_ref, gd_ref,
                       xs0_ref, xs1_ref, xs2_ref, xs3_ref, us_ref, ys_ref, l_ref, yo_ref, ye_ref, e_ref,
                       ecarry_ref, v_ref, g_ref, state_ref, wg_ref, wu_ref, wd_ref, win_ref, a_ref, xdd_ref,
                       wg_s, wu_s, wd_s, win_s, cast_sem):
    xs_refs = (xs0_ref, xs1_ref, xs2_ref, xs3_ref)
    step = pl.program_id(0)
    last = pl.num_programs(0) - 1
    nb, t_len, _ = x_ref.shape
    n_dec = xd_ref.shape[0]
    prow = nb * PAIR_PITCH
    tiles = [(k, j) for k in range(N_BLK) for j in range(HALF)]

    @pl.when(step < W1_STEPS)
    def _load_weights():
        _stream_cast(((wg_c, wg_s, wg_ref), (wu_c, wu_s, wu_ref), (wd_c, wd_s, wd_ref),
                      (win_c, win_s, win_ref)), cast_sem, W1_CHUNKS)

    @pl.when(step == 0)
    def _init():
        for ref in (state_ref, us_ref, l_ref, e_ref, ecarry_ref, v_ref, g_ref):
            ref[...] = jnp.zeros_like(ref)
        xdd_ref[...] = xd_ref[...]

    @pl.when(step == last)
    def _decode_rows_in():
        db = n_dec // t_len
        x_ref[0:db] = xdd_ref[...].reshape(db, t_len, D_MODEL)

    @pl.when(step >= W1_STEPS)
    def _main():
        carry = {}

        def build_lhs():
            for b in range(nb):
                for k in range(N_BLK):
                    for par in range(2):
                        col = (2 * k + par) * LANES
                        l_ref[pl.ds(b * PAIR_PITCH, N_PAIR), col:col + LANES] = (
                            us_ref[k, pl.ds(b * t_len + par, N_PAIR, stride=2), :])

        def b_proj(ks):
            for k in ks:
                lhs = l_ref[:, 2 * k * LANES:2 * (k + 1) * LANES].astype(jnp.bfloat16)
                v = jnp.dot(lhs, wb_ref[k], preferred_element_type=jnp.float32)
                for j in range(SLABS_PER_BLK):
                    xs_refs[k][j] = v[:, j * LANES:(j + 1) * LANES]

        def scan(r0, r1):
            if r0 == 0:
                carry["re"] = [state_ref[k * SLABS_PER_BLK + j] for k, j in tiles]
                carry["im"] = [state_ref[k * SLABS_PER_BLK + HALF + j] for k, j in tiles]
            xr, xi = carry["re"], carry["im"]
            for r in range(r0, r1):
                rows = pl.ds(r, SUBLANES, stride=PAIR_PITCH)
                for n, (k, j) in enumerate(tiles):
                    a_re = a2re_ref[k * HALF + j]
                    a_im = a2im_ref[k * HALF + j]
                    nr = (a_re * xr[n] + xs_refs[k][j, rows, :]) - a_im * xi[n]
                    ni = (a_re * xi[n] + xs_refs[k][HALF + j, rows, :]) + a_im * xr[n]
                    xs_refs[k][j, rows, :] = nr
                    xs_refs[k][HALF + j, rows, :] = ni
                    xr[n], xi[n] = nr, ni
            if r1 == N_PAIR:
                for n, (k, j) in enumerate(tiles):
                    state_ref[k * SLABS_PER_BLK + j] = xr[n]
                    state_ref[k * SLABS_PER_BLK + HALF + j] = xi[n]

        def c_proj(ks):
            for k in ks:
                lanes = slice(k * LANES, (k + 1) * LANES)
                x = jnp.concatenate([xs_refs[k][j] for j in range(SLABS_PER_BLK)], axis=1)
                ce = lax.dot_general(x.astype(jnp.bfloat16), wct_ref[k], _NT,
                                     preferred_element_type=jnp.float32)
                yo_ref[:, lanes] = ce[:, 0:LANES]
                e_ref[k, SUBLANES:SUBLANES + prow, :] = ce[:, LANES:]
                nxt = e_ref[k, pl.ds(SUBLANES + N_PAIR - 1, nb, stride=PAIR_PITCH), :]
                e_ref[k, pl.ds(SUBLANES - 1, nb, stride=PAIR_PITCH), :] = ecarry_ref[k]
                ecarry_ref[k] = nxt
                u_even = l_ref[:, 2 * k * LANES:(2 * k + 1) * LANES].astype(jnp.bfloat16)
                ye_ref[:, lanes] = (e_ref[k, SUBLANES - 1:SUBLANES - 1 + prow, :]
                                    + jnp.dot(u_even, wdm_ref[k], preferred_element_type=jnp.float32))

        def glu_emit():
            for b in range(nb):
                for k in range(N_BLK):
                    lanes = slice(k * LANES, (k + 1) * LANES)
                    for par, src in ((0, ye_ref), (1, yo_ref)):
                        ys_ref[k, pl.ds(b * t_len + par, N_PAIR, stride=2), :] = (
                            src[pl.ds(b * PAIR_PITCH, N_PAIR), lanes])
            y = jnp.concatenate([ys_ref[k] for k in range(N_BLK)], axis=1)
            u = jnp.concatenate([us_ref[k] for k in range(N_BLK)], axis=1)
            s_out = _glu_tail(y, u, d_ref, wglu_ref, bglu_ref)
            mix_ref[:, :, 0:SSM_WIDTH] = s_out.reshape(nb, t_len, SSM_WIDTH).astype(mix_ref.dtype)

        def conv():
            v = v_ref[:, SUBLANES:SUBLANES + t_len, :]
            z = cw_ref[0] * v_ref[:, SUBLANES - 2:SUBLANES - 2 + t_len, :]
            z = z + cw_ref[1] * v_ref[:, SUBLANES - 1:SUBLANES - 1 + t_len, :]
            z = z + cw_ref[2] * v
            mix_ref[:, :, SSM_WIDTH:] = (g_ref[...] * z).astype(mix_ref.dtype)
            v_ref[:, 0:SUBLANES, :] = v_ref[:, t_len:t_len + SUBLANES, :]

        q = N_PAIR // 4
        pieces = [
            lambda: (conv(), build_lhs(), b_proj((0,))), lambda: b_proj((1,)), lambda: b_proj((2,)),
            lambda: b_proj((3,)),
            lambda: scan(0, q), lambda: scan(q, 2 * q), lambda: scan(2 * q, 3 * q), lambda: scan(3 * q, N_PAIR),
            lambda: c_proj((0, 1)), lambda: c_proj((2, 3)),
            glu_emit,
        ]
        assert len(pieces) <= D_FF // FFN_TF

        x = x_ref[...].reshape(nb * t_len, D_MODEL)
        xn = _rms(x, n1_ref[...]).astype(jnp.bfloat16)
        for c, j in enumerate(range(0, D_FF, FFN_TF)):
            _swiglu_stage(xn, wg_ref, wu_ref, a_ref, j, j + FFN_TF)
            if c < len(pieces):
                pieces[c]()
        h_ref[...] = (x + 0.5 * jnp.dot(a_ref[...], wd_ref[...], preferred_element_type=jnp.float32)
                      ).reshape(nb, t_len, D_MODEL)
        hn = _rms(h_ref[...].reshape(nb * t_len, D_MODEL), nm_ref[...]).astype(jnp.bfloat16)
        p = jnp.dot(hn, win_ref[...], preferred_element_type=jnp.float32)
        v_new = p[:, SSM_WIDTH + 2 * CONV_WIDTH:] * p[:, SSM_WIDTH:SSM_WIDTH + CONV_WIDTH]
        v_ref[:, SUBLANES:SUBLANES + t_len, :] = v_new.reshape(nb, t_len, CONV_WIDTH)
        g_ref[...] = p[:, SSM_WIDTH + CONV_WIDTH:SSM_WIDTH + 2 * CONV_WIDTH].reshape(nb, t_len, CONV_WIDTH)
        for k in range(N_BLK):
            us_ref[k] = p[:, k * LANES:(k + 1) * LANES]

    @pl.when(step == last)
    def _final():
        for n, (k, j) in enumerate(tiles):
            for out_ref, slab in ((sre_ref, k * SLABS_PER_BLK + j), (sim_ref, k * SLABS_PER_BLK + HALF + j)):
                tile = state_ref[slab]
                for half in range(LANES // SSM_STATE):
                    group = (k * BLK_STATE + j * LANES) // SSM_STATE + half
                    out_ref[:, group, :] = tile[:, half * SSM_STATE:(half + 1) * SSM_STATE]
        nconv_ref[...] = v_ref[:, SUBLANES - 2:SUBLANES, :]
        for k in range(N_BLK):
            ud_ref[:, k * LANES:(k + 1) * LANES] = us_ref[k, 0:n_dec, :]
        for b in range(n_dec // t_len):
            vd_ref[b * t_len:(b + 1) * t_len, :] = v_ref[b, SUBLANES:SUBLANES + t_len, :]
            gd_ref[b * t_len:(b + 1) * t_len, :] = g_ref[b]


def _ffn1_mixer(x, xd, n1, wg, wu, wd, nm, win, a2_re, a2_im, wb, wct, wdm, d, wglu, bglu, cw):
    nb, seq, _ = x.shape
    n_dec = xd.shape[0]
    n_chunk = seq // SCAN_T
    prow = nb * PAIR_PITCH
    f32, bf = jnp.float32, jnp.bfloat16
    main = lambda i: jnp.maximum(i - W1_STEPS, 0)
    return pl.pallas_call(
        _ffn1_mixer_kernel,
        grid=(W1_STEPS + n_chunk + 1,),
        in_specs=[
            pl.BlockSpec((nb, SCAN_T, D_MODEL), lambda i: (0, jnp.minimum(main(i), n_chunk - 1), 0)),
            pl.BlockSpec((n_dec, None, D_MODEL), lambda i: (0, 0, 0), pipeline_mode=pl.Buffered(1)),
            _whole((1, D_MODEL)),
            _IN_HBM,
            _IN_HBM,
            _IN_HBM,
            _whole((1, D_MODEL)),
            _IN_HBM,
            _whole((N_TILE, SUBLANES, LANES)),
            _whole((N_TILE, SUBLANES, LANES)),
            _whole((N_BLK, 2 * LANES, 2 * BLK_STATE)),
            _whole((N_BLK, 2 * LANES, 2 * BLK_STATE)),
            _whole((N_BLK, LANES, LANES)),
            _whole((1, SSM_WIDTH)),
            _whole((SSM_WIDTH, SSM_WIDTH)),
            _whole((1, SSM_WIDTH)),
            _whole((CONV_K, 1, CONV_WIDTH)),
        ],
        out_specs=[
            pl.BlockSpec((nb, SCAN_T, D_MODEL), lambda i: (0, main(i), 0)),
            pl.BlockSpec((nb, SCAN_T, D_MODEL), lambda i: (0, jnp.clip(main(i) - 1, 0, n_chunk - 1), 0)),
            pl.BlockSpec((nb, N_SSM_GROUPS, SSM_STATE), lambda i: (0, 0, 0)),
            pl.BlockSpec((nb, N_SSM_GROUPS, SSM_STATE), lambda i: (0, 0, 0)),
            pl.BlockSpec((nb, CONV_K - 1, CONV_WIDTH), lambda i: (0, 0, 0)),
            pl.BlockSpec((n_dec, SSM_WIDTH), lambda i: (0, 0)),
            pl.BlockSpec((n_dec, CONV_WIDTH), lambda i: (0, 0)),
            pl.BlockSpec((n_dec, CONV_WIDTH), lambda i: (0, 0)),
        ],
        out_shape=[
            jax.ShapeDtypeStruct((nb, seq + SCAN_T, D_MODEL), f32),
            jax.ShapeDtypeStruct((nb, seq, D_MODEL), bf),
            jax.ShapeDtypeStruct((nb, N_SSM_GROUPS, SSM_STATE), f32),
            jax.ShapeDtypeStruct((nb, N_SSM_GROUPS, SSM_STATE), f32),
            jax.ShapeDtypeStruct((nb, CONV_K - 1, CONV_WIDTH), f32),
            jax.ShapeDtypeStruct((n_dec, SSM_WIDTH), f32),
            jax.ShapeDtypeStruct((n_dec, CONV_WIDTH), f32),
            jax.ShapeDtypeStruct((n_dec, CONV_WIDTH), f32),
        ],
        scratch_shapes=[
            *[pltpu.VMEM((SLABS_PER_BLK, prow, LANES), f32) for _ in range(N_BLK)],
            pltpu.VMEM((N_BLK, nb * SCAN_T, LANES), f32),
            pltpu.VMEM((N_BLK, nb * SCAN_T, LANES), f32),
            pltpu.VMEM((prow, 2 * SSM_WIDTH), f32),
            pltpu.VMEM((prow, SSM_WIDTH), f32),
            pltpu.VMEM((prow, SSM_WIDTH), f32),
            pltpu.VMEM((N_BLK, SUBLANES + prow, LANES), f32),
            pltpu.VMEM((N_BLK, SUBLANES, LANES), f32),
            pltpu.VMEM((nb, SCAN_T + SUBLANES, CONV_WIDTH), f32),
            pltpu.VMEM((nb, SCAN_T, CONV_WIDTH), f32),
            pltpu.VMEM((2 * N_TILE, SUBLANES, LANES), f32),
            pltpu.VMEM((D_MODEL, D_FF), bf),
            pltpu.VMEM((D_MODEL, D_FF), bf),
            pltpu.VMEM((D_FF, D_MODEL), bf),
            pltpu.VMEM((D_MODEL, IN_PROJ_WIDTH), bf),
            pltpu.VMEM((nb * SCAN_T, D_FF), bf),
            pltpu.VMEM((n_dec, D_MODEL), f32),
            _staging(D_MODEL // W1_CHUNKS, D_FF),
            _staging(D_MODEL // W1_CHUNKS, D_FF),
            _staging(D_FF // W1_CHUNKS, D_MODEL),
            _staging(D_MODEL // W1_CHUNKS, IN_PROJ_WIDTH),
            pltpu.SemaphoreType.DMA((8,)),
        ],
        compiler_params=pltpu.CompilerParams(
            dimension_semantics=("arbitrary",), vmem_limit_bytes=VMEM_LIMIT),
        name="ffn1_mixer",
    )(x, xd, n1, wg, wu, wd, nm, win, a2_re, a2_im, wb, wct, wdm, d, wglu, bglu, cw)


def _ffn2_rows(h, mix, weights, a_ref):
    wout_ref, n2_ref, wg_ref, wu_ref, wd_ref, nf_ref = weights
    h = h + jnp.dot(mix, wout_ref[...], preferred_element_type=jnp.float32)
    hn = _rms(h, n2_ref[...]).astype(jnp.bfloat16)
    h = h + 0.5 * _swiglu_staged(hn, wg_ref, wu_ref, wd_ref, a_ref)
    return _rms(h, nf_ref[...])


def _ffn2_kernel(h_ref, mix_ref, hd0_ref, hd1_ref, mixd_ref, wout_c, wg_c, wu_c, wd_c, n2_ref, nf_ref,
                 y_ref, yd_ref, wout_ref, wg_ref, wu_ref, wd_ref, a0_ref, a1_ref,
                 wout_s, wg_s, wu_s, wd_s, cast_sem):
    step = pl.program_id(0)
    last = pl.num_programs(0) - 1
    weights = (wout_ref, n2_ref, wg_ref, wu_ref, wd_ref, nf_ref)

    @pl.when(step < W2_STEPS)
    def _load_weights():
        _stream_cast(((wout_c, wout_s, wout_ref), (wg_c, wg_s, wg_ref), (wu_c, wu_s, wu_ref),
                      (wd_c, wd_s, wd_ref)), cast_sem, W2_CHUNKS)

    @pl.when((step >= W2_STEPS) & (step < last))
    def _prompt_tile():
        assert h_ref.shape[0] == 2 * FFN_SUB
        ra, rb = slice(0, FFN_SUB), slice(FFN_SUB, 2 * FFN_SUB)
        h_a = h_ref[ra, :] + jnp.dot(mix_ref[ra, :], wout_ref[...], preferred_element_type=jnp.float32)
        h_b = h_ref[rb, :] + jnp.dot(mix_ref[rb, :], wout_ref[...], preferred_element_type=jnp.float32)
        hn_a = _rms(h_a, n2_ref[...]).astype(jnp.bfloat16)
        _swiglu_stage(hn_a, wg_ref, wu_ref, a0_ref, 0, FFN_TF)
        hn_b = _rms(h_b, n2_ref[...]).astype(jnp.bfloat16)
        _swiglu_stage(hn_a, wg_ref, wu_ref, a0_ref, FFN_TF, D_FF)
        f_a = jnp.dot(a0_ref[...], wd_ref[...], preferred_element_type=jnp.float32)
        _swiglu_stage(hn_b, wg_ref, wu_ref, a1_ref, 0, FFN_TF)
        y_ref[ra, :] = _rms(h_a + 0.5 * f_a, nf_ref[...])
        _swiglu_stage(hn_b, wg_ref, wu_ref, a1_ref, FFN_TF, D_FF)
        f_b = jnp.dot(a1_ref[...], wd_ref[...], preferred_element_type=jnp.float32)
        y_ref[rb, :] = _rms(h_b + 0.5 * f_b, nf_ref[...])

    @pl.when(step == last)
    def _decode_rows():
        h_d = jnp.concatenate([hd0_ref[...], hd1_ref[...]], axis=0)
        yd_ref[...] = _ffn2_rows(h_d, mixd_ref[...], weights, a0_ref)


def _ffn2(h_all, mix, mixd, wout, n2, wg, wu, wd, nf):
    nb, seq, _ = mix.shape
    seq_all = h_all.shape[1]
    n_dec = mixd.shape[0]
    tail = seq_all - seq
    assert n_dec == 2 * tail
    per_b = seq // FFN_TM
    n_tiles = nb * per_b
    h_all = h_all.reshape(nb * seq_all, D_MODEL)
    mix = mix.reshape(nb * seq, D_MODEL)
    idx = lambda i: jnp.clip(i - W2_STEPS, 0, n_tiles - 1)
    tile = lambda i: (idx(i), 0)
    assert seq_all % SUBLANES == 0
    tile_all = lambda i: (pl.multiple_of((idx(i) // per_b) * seq_all + (idx(i) % per_b) * FFN_TM, SUBLANES), 0)
    yp, yd = pl.pallas_call(
        _ffn2_kernel,
        grid=(W2_STEPS + n_tiles + 1,),
        in_specs=[
            pl.BlockSpec((pl.Element(FFN_TM), pl.Element(D_MODEL)), tile_all),
            pl.BlockSpec((FFN_TM, D_MODEL), tile),
            pl.BlockSpec((pl.Element(tail), pl.Element(D_MODEL)), lambda i: (seq, 0),
                         pipeline_mode=pl.Buffered(1)),
            pl.BlockSpec((pl.Element(tail), pl.Element(D_MODEL)), lambda i: (seq_all + seq, 0),
                         pipeline_mode=pl.Buffered(1)),
            _whole(mixd.shape),
            _IN_HBM,
            _IN_HBM,
            _IN_HBM,
            _IN_HBM,
            _whole((1, D_MODEL)),
            _whole((1, D_MODEL)),
        ],
        out_specs=[
            pl.BlockSpec((FFN_TM, D_MODEL), tile),
            pl.BlockSpec((n_dec, None, D_MODEL), lambda i: (0, 0, 0)),
        ],
        out_shape=[
            jax.ShapeDtypeStruct((nb * seq, D_MODEL), jnp.float32),
            jax.ShapeDtypeStruct((n_dec, 1, D_MODEL), jnp.float32),
        ],
        scratch_shapes=[
            pltpu.VMEM((D_MODEL, D_MODEL), jnp.bfloat16),
            pltpu.VMEM((D_MODEL, D_FF), jnp.bfloat16),
            pltpu.VMEM((D_MODEL, D_FF), jnp.bfloat16),
            pltpu.VMEM((D_FF, D_MODEL), jnp.bfloat16),
            pltpu.VMEM((FFN_SUB, D_FF), jnp.bfloat16),
            pltpu.VMEM((FFN_SUB, D_FF), jnp.bfloat16),
            _staging(D_MODEL // W2_CHUNKS, D_MODEL),
            _staging(D_MODEL // W2_CHUNKS, D_FF),
            _staging(D_MODEL // W2_CHUNKS, D_FF),
            _staging(D_FF // W2_CHUNKS, D_MODEL),
            pltpu.SemaphoreType.DMA((8,)),
        ],
        compiler_params=pltpu.CompilerParams(
            dimension_semantics=("arbitrary",), vmem_limit_bytes=VMEM_LIMIT),
        name="outproj_ffn2",
    )(h_all, mix, h_all, h_all, mixd, wout, wg, wu, wd, n2, nf)
    return yp.reshape(nb, seq, D_MODEL), yd


def _mixer_step_kernel(u_ref, v_ref, g_ref, hre_ref, him_ref, buf_ref, are_ref, aim_ref, bb_ref, wct_ref,
                       d_ref, wglu_ref, bglu_ref, cw_ref, mix_ref, sre_ref, sim_ref, nconv_ref):
    u = u_ref[...]
    u_bf = u.astype(jnp.bfloat16)
    ys = []
    for k in range(N_BLK):
        bu = jnp.dot(u_bf[:, k * LANES:(k + 1) * LANES], bb_ref[k], preferred_element_type=jnp.float32)
        st = slice(k * BLK_STATE, (k + 1) * BLK_STATE)
        a_re = are_ref[k:k + 1, :]
        a_im = aim_ref[k:k + 1, :]
        h_re = hre_ref[st, :].T
        h_im = him_ref[st, :].T
        x_re = (a_re * h_re + bu[:, :BLK_STATE]) - a_im * h_im
        x_im = (a_re * h_im + bu[:, BLK_STATE:]) + a_im * h_re
        sre_ref[st, :] = x_re.T
        sim_ref[st, :] = x_im.T
        x = jnp.concatenate([x_re, x_im], axis=1).astype(jnp.bfloat16)
        ys.append(lax.dot_general(x, wct_ref[k], _NT, preferred_element_type=jnp.float32))
    y = jnp.concatenate(ys, axis=1)
    mix_ref[:, 0:SSM_WIDTH] = _glu_tail(y, u, d_ref, wglu_ref, bglu_ref).astype(mix_ref.dtype)

    v = v_ref[...]
    buf1 = buf_ref[:, 1, :]
    z = cw_ref[0] * buf_ref[:, 0, :] + cw_ref[1] * buf1 + cw_ref[2] * v
    mix_ref[:, SSM_WIDTH:] = (g_ref[...] * z).astype(mix_ref.dtype)
    nconv_ref[:, 0, :] = buf1
    nconv_ref[:, 1, :] = v


def _mixer_step(u, v, g, h_re, h_im, conv_state, a_re, a_im, wb, wct, d, wglu, bglu, cw):
    nb = u.shape[0]
    conv_spec = pl.BlockSpec((None, nb, CONV_K - 1, CONV_WIDTH), lambda i: (0, 0, 0, 0))
    half = (N_BLK, LANES, 2 * BLK_STATE)
    bb_spec = pl.BlockSpec(half, lambda i: (0, 1, 0), pipeline_mode=pl.Buffered(1))
    c_spec = pl.BlockSpec(half, lambda i: (0, 0, 0), pipeline_mode=pl.Buffered(1))
    rows = (u, v, g, h_re, h_im)
    tail = (d, wglu, bglu, cw)
    return pl.pallas_call(
        _mixer_step_kernel,
        grid=(1,),
        in_specs=([_whole(a.shape) for a in rows] + [conv_spec] + [_whole(a_re.shape), _whole(a_im.shape)]
                  + [bb_spec, c_spec] + [_whole(a.shape) for a in tail]),
        out_specs=[
            pl.BlockSpec((nb, D_MODEL), lambda i: (0, 0)),
            pl.BlockSpec((N_STATE, nb), lambda i: (0, 0)),
            pl.BlockSpec((N_STATE, nb), lambda i: (0, 0)),
            conv_spec,
        ],
        out_shape=[
            jax.ShapeDtypeStruct((nb, D_MODEL), jnp.bfloat16),
            jax.ShapeDtypeStruct((N_STATE, nb), jnp.float32),
            jax.ShapeDtypeStruct((N_STATE, nb), jnp.float32),
            jax.ShapeDtypeStruct(conv_state.shape, jnp.float32),
        ],
        compiler_params=pltpu.CompilerParams(
            dimension_semantics=("arbitrary",), vmem_limit_bytes=VMEM_LIMIT),
        name="mixer_step",
    )(*rows, conv_state, a_re, a_im, wb, wct, *tail)


def _ssm_layout(lam_re, lam_im, log_dt, b_re, b_im, c_re, c_im):
    gpb = N_SSM_GROUPS // N_BLK

    def b_rows(b):
        return b.reshape(N_BLK, gpb, SSM_STATE, SSM_GROUP).transpose(0, 1, 3, 2).reshape(N_BLK, LANES, SSM_STATE)

    def c_rows(c):
        return c.reshape(N_BLK, LANES, SSM_STATE)

    return lam_re, lam_im, log_dt, b_rows(b_re), b_rows(b_im), c_rows(c_re), c_rows(c_im)


def kernel(x_prompt, x_sample, state_ssm_re, state_ssm_im, state_conv, ffn1_norm, ffn1_w_gate, ffn1_w_up, ffn1_w_down, mix_norm, w_in, ssm_lambda_re, ssm_lambda_im, ssm_log_dt, ssm_b_re, ssm_b_im, ssm_c_re, ssm_c_im, ssm_d, ssm_w_glu, ssm_b_glu, conv_w, w_out, ffn2_norm, ffn2_w_gate, ffn2_w_up, ffn2_w_down, final_norm):
    assert ffn1_norm.shape[0] == 1, "single-layer trunk"
    n_prompt, seq, _ = x_prompt.shape
    n_dec, dec_seq, _ = x_sample.shape
    assert dec_seq == 1 and n_prompt == SUBLANES and seq % FFN_TM == 0 and seq % SCAN_T == 0
    assert n_dec % SCAN_T == 0 and n_dec <= n_prompt * SCAN_T and FFN_TM % FFN_SUB == 0

    n1 = ffn1_norm[0][None, :]
    nm = mix_norm[0][None, :]
    n2 = ffn2_norm[0][None, :]
    nf = final_norm[None, :]
    wglu = ssm_w_glu[0]
    layout = _ssm_layout(ssm_lambda_re[0], ssm_lambda_im[0], ssm_log_dt, ssm_b_re[0], ssm_b_im[0],
                         ssm_c_re[0], ssm_c_im[0])
    a_re, a_im, a2_re, a2_im, wb, wct, wdm = _discretize(*layout)
    d = ssm_d[0][None, :]
    bglu = ssm_b_glu[0][None, :]
    cw = conv_w.transpose(1, 0, 2)

    h_all, mixp, p_re, p_im, p_conv, ud, vd, gd = _ffn1_mixer(
        x_prompt, x_sample, n1, ffn1_w_gate[0], ffn1_w_up[0], ffn1_w_down[0], nm,
        w_in[0], a2_re, a2_im, wb, wct, wdm, d, wglu, bglu, cw)

    s_re0 = state_ssm_re[0].transpose(1, 2, 0).reshape(N_STATE, n_dec)
    s_im0 = state_ssm_im[0].transpose(1, 2, 0).reshape(N_STATE, n_dec)
    mixd, s_re, s_im, s_conv = _mixer_step(ud, vd, gd, s_re0, s_im0, state_conv, a_re, a_im, wb, wct,
                                           d, wglu, bglu, cw)

    yp, yd = _ffn2(h_all, mixp, mixd, w_out[0], n2, ffn2_w_gate[0], ffn2_w_up[0], ffn2_w_down[0], nf)

    gs = (N_SSM_GROUPS, SSM_STATE)
    return (yp,
            yd,
            p_re[None],
            p_im[None],
            p_conv[None],
            s_re.reshape(*gs, n_dec).transpose(2, 0, 1)[None],
            s_im.reshape(*gs, n_dec).transpose(2, 0, 1)[None],
            s_conv)
```

```python
import jax
import jax.numpy as jnp
from jax import lax
from jax.experimental import pallas as pl
from jax.experimental.pallas import tpu as pltpu

D_MODEL = 1024
D_FF = 2816
SSM_WIDTH = 512
CONV_WIDTH = 512
SSM_GROUP = 16
N_SSM_GROUPS = 32
SSM_STATE = 64
CONV_K = 3
IN_PROJ_WIDTH = SSM_WIDTH + 3 * CONV_WIDTH
EPS = 1e-6

LANES = 128
SUBLANES = 8
N_STATE = N_SSM_GROUPS * SSM_STATE
N_BLK = SSM_WIDTH // LANES
BLK_STATE = N_STATE // N_BLK
SLABS_PER_BLK = 2 * BLK_STATE // LANES
HALF = SLABS_PER_BLK // 2
N_TILE = N_BLK * HALF

FFN_TM = 1024
FFN_SUB = 512
W1_STEPS = 1
W2_STEPS = 1
W1_CHUNKS = 16
W2_CHUNKS = 8
FFN_TF = 256
SCAN_T = 64
N_PAIR = SCAN_T // 2
PAIR_PITCH = N_PAIR + 4
VMEM_LIMIT = 62 * 1024 * 1024

_NT = (((1,), (1,)), ((), ()))


def _rms(x, g):
    r = lax.rsqrt(jnp.mean(x * x, axis=-1, keepdims=True) + EPS)
    return x * r * g


def _bdot(a, b):
    return jnp.dot(a.astype(jnp.bfloat16), b.astype(jnp.bfloat16), preferred_element_type=jnp.float32)


def _swiglu_stage(xn, wg_ref, wu_ref, a_ref, lo, hi):
    rows = xn.shape[0]
    for j in range(lo, hi, FFN_TF):
        g = jnp.dot(xn, wg_ref[:, j:j + FFN_TF], preferred_element_type=jnp.float32)
        u = jnp.dot(xn, wu_ref[:, j:j + FFN_TF], preferred_element_type=jnp.float32)
        a_ref[0:rows, j:j + FFN_TF] = ((g * jax.nn.sigmoid(g)) * u).astype(a_ref.dtype)


def _swiglu_staged(xn, wg_ref, wu_ref, wd_ref, a_ref):
    rows = xn.shape[0]
    _swiglu_stage(xn, wg_ref, wu_ref, a_ref, 0, D_FF)
    return jnp.dot(a_ref[0:rows, :], wd_ref[...], preferred_element_type=jnp.float32)


def _stream_cast(triples, sem, n_chunks):
    def copy(w, c, slot):
        src, buf, _ = triples[w]
        n = buf.shape[1]
        return pltpu.make_async_copy(src.at[pl.ds(pl.multiple_of(c * n, SUBLANES), n), :], buf.at[slot],
                                     sem.at[2 * w + slot])

    for w in range(len(triples)):
        copy(w, 0, 0).start()

    def body(c, carry):
        slot = lax.rem(c, 2)

        @pl.when(c + 1 < n_chunks)
        def _prefetch():
            for w in range(len(triples)):
                copy(w, c + 1, 1 - slot).start()

        for w, (_, buf, dst) in enumerate(triples):
            copy(w, c, slot).wait()
            n = buf.shape[1]
            dst[pl.ds(pl.multiple_of(c * n, SUBLANES), n), :] = buf[slot].astype(dst.dtype)
        return carry

    lax.fori_loop(0, n_chunks, body, 0)


def _staging(rows, cols):
    return pltpu.VMEM((2, rows, cols), jnp.float32)


def _whole(shape):
    return pl.BlockSpec(shape, lambda i: (0,) * len(shape), pipeline_mode=pl.Buffered(1))


_IN_HBM = pl.BlockSpec(memory_space=pl.ANY)


def _glu_tail(y, u, d_ref, wglu_ref, bglu_ref):
    y = y + d_ref[...] * u
    z = jax.nn.gelu(y)
    gate = jax.nn.sigmoid(_bdot(z, wglu_ref[...]) + bglu_ref[...])
    return z * gate


def _dot3_nt(a, b):
    bf = jnp.bfloat16
    a_hi, b_hi = a.astype(bf), b.astype(bf)
    a_lo = (a - a_hi.astype(jnp.float32)).astype(bf)
    b_lo = (b - b_hi.astype(jnp.float32)).astype(bf)
    dot = lambda p, q: lax.dot_general(p, q, _NT, preferred_element_type=jnp.float32)
    return dot(a_hi, b_hi) + (dot(a_hi, b_lo) + dot(a_lo, b_hi))


def _discretize_kernel(lr_ref, li_ref, ldt_ref, bre_ref, bim_ref, ctre_ref, ctim_ref,
                       are_ref, aim_ref, a2re_ref, a2im_ref, wb_ref, wct_ref, wdm_ref):
    row_group = lax.broadcasted_iota(jnp.int32, (LANES, BLK_STATE), 0) // SSM_GROUP
    col_group = lax.broadcasted_iota(jnp.int32, (LANES, BLK_STATE), 1) // SSM_STATE
    on_diagonal = row_group == col_group

    def blocks(m):
        pair = jnp.concatenate([m, m], axis=1)
        return jnp.where(on_diagonal, jnp.concatenate([pair] * (BLK_STATE // LANES), axis=1), 0.0)

    gpb = N_SSM_GROUPS // N_BLK

    def state_rows(ref):
        return jnp.concatenate(
            [jnp.concatenate([ref[k * gpb + g:k * gpb + g + 1, :] for g in range(gpb)], axis=1)
             for k in range(N_BLK)], axis=0)

    def per_state(row_ref):
        vals = row_ref[...]
        col_group = lax.broadcasted_iota(jnp.int32, (1, BLK_STATE), 1) // SSM_STATE
        rows = []
        for k in range(N_BLK):
            row = jnp.zeros((1, BLK_STATE), jnp.float32)
            for g in range(gpb):
                row = jnp.where(col_group == g, vals[:, k * gpb + g:k * gpb + g + 1], row)
            rows.append(row)
        return jnp.concatenate(rows, axis=0)

    lr, li = state_rows(lr_ref), state_rows(li_ref)
    dt = jnp.exp(per_state(ldt_ref))
    mag = jnp.exp(lr * dt)
    ab_re = mag * jnp.cos(li * dt)
    ab_im = mag * jnp.sin(li * dt)
    den = lr * lr + li * li
    nr = ab_re - 1.0
    ni = ab_im
    coef_re = (nr * lr + ni * li) / den
    coef_im = (ni * lr - nr * li) / den
    are_ref[...] = ab_re
    aim_ref[...] = ab_im
    a2_re = ab_re * ab_re - ab_im * ab_im
    a2_im = 2.0 * (ab_re * ab_im)
    bf = jnp.bfloat16
    for k in range(N_BLK):
        ar, ai = ab_re[k:k + 1, :], ab_im[k:k + 1, :]
        for j in range(HALF):
            lanes = slice(j * LANES, (j + 1) * LANES)
            a2re_ref[k * HALF + j] = jnp.broadcast_to(a2_re[k:k + 1, lanes], (SUBLANES, LANES))
            a2im_ref[k * HALF + j] = jnp.broadcast_to(a2_im[k:k + 1, lanes], (SUBLANES, LANES))
        cr, ci = coef_re[k:k + 1, :], coef_im[k:k + 1, :]
        bre, bim = blocks(bre_ref[k]), blocks(bim_ref[k])
        bb_re = cr * bre - ci * bim
        bb_im = cr * bim + ci * bre
        wb_ref[k, 0:LANES, 0:BLK_STATE] = (ar * bb_re - ai * bb_im).astype(bf)
        wb_ref[k, 0:LANES, BLK_STATE:] = (ar * bb_im + ai * bb_re).astype(bf)
        wb_ref[k, LANES:, 0:BLK_STATE] = bb_re.astype(bf)
        wb_ref[k, LANES:, BLK_STATE:] = bb_im.astype(bf)
        ctre, ctim = blocks(ctre_ref[k]), blocks(ctim_ref[k])
        wct_ref[k, 0:LANES, 0:BLK_STATE] = ctre.astype(bf)
        wct_ref[k, 0:LANES, BLK_STATE:] = (-ctim).astype(bf)
        wct_ref[k, LANES:, 0:BLK_STATE] = (ctre * ar - ctim * ai).astype(bf)
        wct_ref[k, LANES:, BLK_STATE:] = (-(ctre * ai + ctim * ar)).astype(bf)
        wdm = _dot3_nt(bb_re, ctre) - _dot3_nt(bb_im, ctim)
        wdm_ref[k] = wdm.astype(bf)


def _discretize(lr, li, ldt, bre, bim, ctre, ctim):
    f32, bf = jnp.float32, jnp.bfloat16
    return pl.pallas_call(
        _discretize_kernel,
        out_shape=[
            jax.ShapeDtypeStruct((N_BLK, BLK_STATE), f32),
            jax.ShapeDtypeStruct((N_BLK, BLK_STATE), f32),
            jax.ShapeDtypeStruct((N_TILE, SUBLANES, LANES), f32),
            jax.ShapeDtypeStruct((N_TILE, SUBLANES, LANES), f32),
            jax.ShapeDtypeStruct((N_BLK, 2 * LANES, 2 * BLK_STATE), bf),
            jax.ShapeDtypeStruct((N_BLK, 2 * LANES, 2 * BLK_STATE), bf),
            jax.ShapeDtypeStruct((N_BLK, LANES, LANES), bf),
        ],
        name="ssm_discretize",
    )(lr, li, ldt, bre, bim, ctre, ctim)


def _ffn1_mixer_kernel(x_ref, xd_ref, n1_ref, wg_c, wu_c, wd_c, nm_ref, win_c,
                       a2re_ref, a2im_ref, wb_ref, wct_ref, wdm_ref, d_ref, wglu_ref, bglu_ref, cw_ref,
                       h_ref, mix_ref, sre_ref, sim_ref, nconv_ref, ud_ref, vd_ref, gd_ref,
                       xs0_ref, xs1_ref, xs2_ref, xs3_ref, us_ref, ys_ref, l_ref, yo_ref, ye_ref, e_ref,
                       ecarry_ref, v_ref, g_ref, state_ref, wg_ref, wu_ref, wd_ref, win_ref, a_ref, xdd_ref,
                       wg_s, wu_s, wd_s, win_s, cast_sem):
    xs_refs = (xs0_ref, xs1_ref, xs2_ref, xs3_ref)
    step = pl.program_id(0)
    last = pl.num_programs(0) - 1
    nb, t_len, _ = x_ref.shape
    n_dec = xd_ref.shape[0]
    prow = nb * PAIR_PITCH
    tiles = [(k, j) for k in range(N_BLK) for j in range(HALF)]

    @pl.when(step < W1_STEPS)
    def _load_weights():
        _stream_cast(((wg_c, wg_s, wg_ref), (wu_c, wu_s, wu_ref), (wd_c, wd_s, wd_ref),
                      (win_c, win_s, win_ref)), cast_sem, W1_CHUNKS)

    @pl.when(step == 0)
    def _init():
        for ref in (state_ref, us_ref, l_ref, e_ref, ecarry_ref, v_ref, g_ref):
            ref[...] = jnp.zeros_like(ref)
        xdd_ref[...] = xd_ref[...]

    @pl.when(step == last)
    def _decode_rows_in():
        db = n_dec // t_len
        x_ref[0:db] = xdd_ref[...].reshape(db, t_len, D_MODEL)

    @pl.when(step >= W1_STEPS)
    def _main():
        carry = {}

        def build_lhs():
            for b in range(nb):
                for k in range(N_BLK):
                    for par in range(2):
                        col = (2 * k + par) * LANES
                        l_ref[pl.ds(b * PAIR_PITCH, N_PAIR), col:col + LANES] = (
                            us_ref[k, pl.ds(b * t_len + par, N_PAIR, stride=2), :])

        def b_proj(ks):
            for k in ks:
                lhs = l_ref[:, 2 * k * LANES:2 * (k + 1) * LANES].astype(jnp.bfloat16)
                v = jnp.dot(lhs, wb_ref[k], preferred_element_type=jnp.float32)
                for j in range(SLABS_PER_BLK):
                    xs_refs[k][j] = v[:, j * LANES:(j + 1) * LANES]

        def scan(r0, r1):
            if r0 == 0:
                carry["re"] = [state_ref[k * SLABS_PER_BLK + j] for k, j in tiles]
                carry["im"] = [state_ref[k * SLABS_PER_BLK + HALF + j] for k, j in tiles]
            xr, xi = carry["re"], carry["im"]
            for r in range(r0, r1):
                rows = pl.ds(r, SUBLANES, stride=PAIR_PITCH)
                for n, (k, j) in enumerate(tiles):
                    a_re = a2re_ref[k * HALF + j]
                    a_im = a2im_ref[k * HALF + j]
                    nr = (a_re * xr[n] + xs_refs[k][j, rows, :]) - a_im * xi[n]
                    ni = (a_re * xi[n] + xs_refs[k][HALF + j, rows, :]) + a_im * xr[n]
                    xs_refs[k][j, rows, :] = nr
                    xs_refs[k][HALF + j, rows, :] = ni
                    xr[n], xi[n] = nr, ni
            if r1 == N_PAIR:
                for n, (k, j) in enumerate(tiles):
                    state_ref[k * SLABS_PER_BLK + j] = xr[n]
                    state_ref[k * SLABS_PER_BLK + HALF + j] = xi[n]

        def c_proj(ks):
            for k in ks:
                lanes = slice(k * LANES, (k + 1) * LANES)
                x = jnp.concatenate([xs_refs[k][j] for j in range(SLABS_PER_BLK)], axis=1)
                ce = lax.dot_general(x.astype(jnp.bfloat16), wct_ref[k], _NT,
                                     preferred_element_type=jnp.float32)
                yo_ref[:, lanes] = ce[:, 0:LANES]
                e_ref[k, SUBLANES:SUBLANES + prow, :] = ce[:, LANES:]
                nxt = e_ref[k, pl.ds(SUBLANES + N_PAIR - 1, nb, stride=PAIR_PITCH), :]
                e_ref[k, pl.ds(SUBLANES - 1, nb, stride=PAIR_PITCH), :] = ecarry_ref[k]
                ecarry_ref[k] = nxt
                u_even = l_ref[:, 2 * k * LANES:(2 * k + 1) * LANES].astype(jnp.bfloat16)
                ye_ref[:, lanes] = (e_ref[k, SUBLANES - 1:SUBLANES - 1 + prow, :]
                                    + jnp.dot(u_even, wdm_ref[k], preferred_element_type=jnp.float32))

        def glu_emit():
            for b in range(nb):
                for k in range(N_BLK):
                    lanes = slice(k * LANES, (k + 1) * LANES)
                    for par, src in ((0, ye_ref), (1, yo_ref)):
                        ys_ref[k, pl.ds(b * t_len + par, N_PAIR, stride=2), :] = (
                            src[pl.ds(b * PAIR_PITCH, N_PAIR), lanes])
            y = jnp.concatenate([ys_ref[k] for k in range(N_BLK)], axis=1)
            u = jnp.concatenate([us_ref[k] for k in range(N_BLK)], axis=1)
            s_out = _glu_tail(y, u, d_ref, wglu_ref, bglu_ref)
            mix_ref[:, :, 0:SSM_WIDTH] = s_out.reshape(nb, t_len, SSM_WIDTH).astype(mix_ref.dtype)

        def conv():
            v = v_ref[:, SUBLANES:SUBLANES + t_len, :]
            z = cw_ref[0] * v_ref[:, SUBLANES - 2:SUBLANES - 2 + t_len, :]
            z = z + cw_ref[1] * v_ref[:, SUBLANES - 1:SUBLANES - 1 + t_len, :]
            z = z + cw_ref[2] * v
            mix_ref[:, :, SSM_WIDTH:] = (g_ref[...] * z).astype(mix_ref.dtype)
            v_ref[:, 0:SUBLANES, :] = v_ref[:, t_len:t_len + SUBLANES, :]

        q = N_PAIR // 4
        pieces = [
            lambda: (conv(), build_lhs(), b_proj((0,))), lambda: b_proj((1,)), lambda: b_proj((2,)),
            lambda: b_proj((3,)),
            lambda: scan(0, q), lambda: scan(q, 2 * q), lambda: scan(2 * q, 3 * q), lambda: scan(3 * q, N_PAIR),
            lambda: c_proj((0, 1)), lambda: c_proj((2, 3)),
            glu_emit,
        ]
        assert len(pieces) <= D_FF // FFN_TF

        x = x_ref[...].reshape(nb * t_len, D_MODEL)
        xn = _rms(x, n1_ref[...]).astype(jnp.bfloat16)
        for c, j in enumerate(range(0, D_FF, FFN_TF)):
            _swiglu_stage(xn, wg_ref, wu_ref, a_ref, j, j + FFN_TF)
            if c < len(pieces):
                pieces[c]()
        hb = nb // 2
        rows = hb * t_len
        hs = [x[i * rows:(i + 1) * rows]
              + 0.5 * jnp.dot(a_ref[i * rows:(i + 1) * rows, :], wd_ref[...], preferred_element_type=jnp.float32)
              for i in range(2)]
        for i, h in enumerate(hs):
            hn = _rms(h, nm_ref[...]).astype(jnp.bfloat16)
            p = jnp.dot(hn, win_ref[...], preferred_element_type=jnp.float32)
            bs = slice(i * hb, (i + 1) * hb)
            h_ref[bs] = h.reshape(hb, t_len, D_MODEL)
            v_new = p[:, SSM_WIDTH + 2 * CONV_WIDTH:] * p[:, SSM_WIDTH:SSM_WIDTH + CONV_WIDTH]
            v_ref[bs, SUBLANES:SUBLANES + t_len, :] = v_new.reshape(hb, t_len, CONV_WIDTH)
            g_ref[bs] = p[:, SSM_WIDTH + CONV_WIDTH:SSM_WIDTH + 2 * CONV_WIDTH].reshape(hb, t_len, CONV_WIDTH)
            for k in range(N_BLK):
                us_ref[k, i * rows:(i + 1) * rows, :] = p[:, k * LANES:(k + 1) * LANES]

    @pl.when(step == last)
    def _final():
        for n, (k, j) in enumerate(tiles):
            for out_ref, slab in ((sre_ref, k * SLABS_PER_BLK + j), (sim_ref, k * SLABS_PER_BLK + HALF + j)):
                tile = state_ref[slab]
                for half in range(LANES // SSM_STATE):
                    group = (k * BLK_STATE + j * LANES) // SSM_STATE + half
                    out_ref[:, group, :] = tile[:, half * SSM_STATE:(half + 1) * SSM_STATE]
        nconv_ref[...] = v_ref[:, SUBLANES - 2:SUBLANES, :]
        for k in range(N_BLK):
            ud_ref[:, k * LANES:(k + 1) * LANES] = us_ref[k, 0:n_dec, :]
        for b in range(n_dec // t_len):
            vd_ref[b * t_len:(b + 1) * t_len, :] = v_ref[b, SUBLANES:SUBLANES + t_len, :]
            gd_ref[b * t_len:(b + 1) * t_len, :] = g_ref[b]


def _ffn1_mixer(x, xd, n1, wg, wu, wd, nm, win, a2_re, a2_im, wb, wct, wdm, d, wglu, bglu, cw):
    nb, seq, _ = x.shape
    n_dec = xd.shape[0]
    n_chunk = seq // SCAN_T
    prow = nb * PAIR_PITCH
    f32, bf = jnp.float32, jnp.bfloat16
    main = lambda i: jnp.maximum(i - W1_STEPS, 0)
    return pl.pallas_call(
        _ffn1_mixer_kernel,
        grid=(W1_STEPS + n_chunk + 1,),
        in_specs=[
            pl.BlockSpec((nb, SCAN_T, D_MODEL), lambda i: (0, jnp.minimum(main(i), n_chunk - 1), 0)),
            pl.BlockSpec((n_dec, None, D_MODEL), lambda i: (0, 0, 0), pipeline_mode=pl.Buffered(1)),
            _whole((1, D_MODEL)),
            _IN_HBM,
            _IN_HBM,
            _IN_HBM,
            _whole((1, D_MODEL)),
            _IN_HBM,
            _whole((N_TILE, SUBLANES, LANES)),
            _whole((N_TILE, SUBLANES, LANES)),
            _whole((N_BLK, 2 * LANES, 2 * BLK_STATE)),
            _whole((N_BLK, 2 * LANES, 2 * BLK_STATE)),
            _whole((N_BLK, LANES, LANES)),
            _whole((1, SSM_WIDTH)),
            _whole((SSM_WIDTH, SSM_WIDTH)),
            _whole((1, SSM_WIDTH)),
            _whole((CONV_K, 1, CONV_WIDTH)),
        ],
        out_specs=[
            pl.BlockSpec((nb, SCAN_T, D_MODEL), lambda i: (0, main(i), 0)),
            pl.BlockSpec((nb, SCAN_T, D_MODEL), lambda i: (0, jnp.clip(main(i) - 1, 0, n_chunk - 1), 0)),
            pl.BlockSpec((nb, N_SSM_GROUPS, SSM_STATE), lambda i: (0, 0, 0)),
            pl.BlockSpec((nb, N_SSM_GROUPS, SSM_STATE), lambda i: (0, 0, 0)),
            pl.BlockSpec((nb, CONV_K - 1, CONV_WIDTH), lambda i: (0, 0, 0)),
            pl.BlockSpec((n_dec, SSM_WIDTH), lambda i: (0, 0)),
            pl.BlockSpec((n_dec, CONV_WIDTH), lambda i: (0, 0)),
            pl.BlockSpec((n_dec, CONV_WIDTH), lambda i: (0, 0)),
        ],
        out_shape=[
            jax.ShapeDtypeStruct((nb, seq + SCAN_T, D_MODEL), f32),
            jax.ShapeDtypeStruct((nb, seq, D_MODEL), bf),
            jax.ShapeDtypeStruct((nb, N_SSM_GROUPS, SSM_STATE), f32),
            jax.ShapeDtypeStruct((nb, N_SSM_GROUPS, SSM_STATE), f32),
            jax.ShapeDtypeStruct((nb, CONV_K - 1, CONV_WIDTH), f32),
            jax.ShapeDtypeStruct((n_dec, SSM_WIDTH), f32),
            jax.ShapeDtypeStruct((n_dec, CONV_WIDTH), f32),
            jax.ShapeDtypeStruct((n_dec, CONV_WIDTH), f32),
        ],
        scratch_shapes=[
            *[pltpu.VMEM((SLABS_PER_BLK, prow, LANES), f32) for _ in range(N_BLK)],
            pltpu.VMEM((N_BLK, nb * SCAN_T, LANES), f32),
            pltpu.VMEM((N_BLK, nb * SCAN_T, LANES), f32),
            pltpu.VMEM((prow, 2 * SSM_WIDTH), f32),
            pltpu.VMEM((prow, SSM_WIDTH), f32),
            pltpu.VMEM((prow, SSM_WIDTH), f32),
            pltpu.VMEM((N_BLK, SUBLANES + prow, LANES), f32),
            pltpu.VMEM((N_BLK, SUBLANES, LANES), f32),
            pltpu.VMEM((nb, SCAN_T + SUBLANES, CONV_WIDTH), f32),
            pltpu.VMEM((nb, SCAN_T, CONV_WIDTH), f32),
            pltpu.VMEM((2 * N_TILE, SUBLANES, LANES), f32),
            pltpu.VMEM((D_MODEL, D_FF), bf),
            pltpu.VMEM((D_MODEL, D_FF), bf),
            pltpu.VMEM((D_FF, D_MODEL), bf),
            pltpu.VMEM((D_MODEL, IN_PROJ_WIDTH), bf),
            pltpu.VMEM((nb * SCAN_T, D_FF), bf),
            pltpu.VMEM((n_dec, D_MODEL), f32),
            _staging(D_MODEL // W1_CHUNKS, D_FF),
            _staging(D_MODEL // W1_CHUNKS, D_FF),
            _staging(D_FF // W1_CHUNKS, D_MODEL),
            _staging(D_MODEL // W1_CHUNKS, IN_PROJ_WIDTH),
            pltpu.SemaphoreType.DMA((8,)),
        ],
        compiler_params=pltpu.CompilerParams(
            dimension_semantics=("arbitrary",), vmem_limit_bytes=VMEM_LIMIT),
        name="ffn1_mixer",
    )(x, xd, n1, wg, wu, wd, nm, win, a2_re, a2_im, wb, wct, wdm, d, wglu, bglu, cw)


def _ffn2_rows(h, mix, weights, a_ref):
    wout_ref, n2_ref, wg_ref, wu_ref, wd_ref, nf_ref = weights
    h = h + jnp.dot(mix, wout_ref[...], preferred_element_type=jnp.float32)
    hn = _rms(h, n2_ref[...]).astype(jnp.bfloat16)
    h = h + 0.5 * _swiglu_staged(hn, wg_ref, wu_ref, wd_ref, a_ref)
    return _rms(h, nf_ref[...])


def _ffn2_kernel(h_ref, mix_ref, hd0_ref, hd1_ref, mixd_ref, wout_c, wg_c, wu_c, wd_c, n2_ref, nf_ref,
                 y_ref, yd_ref, wout_ref, wg_ref, wu_ref, wd_ref, a0_ref, a1_ref,
                 wout_s, wg_s, wu_s, wd_s, cast_sem):
    step = pl.program_id(0)
    last = pl.num_programs(0) - 1
    weights = (wout_ref, n2_ref, wg_ref, wu_ref, wd_ref, nf_ref)

    @pl.when(step < W2_STEPS)
    def _load_weights():
        _stream_cast(((wout_c, wout_s, wout_ref), (wg_c, wg_s, wg_ref), (wu_c, wu_s, wu_ref),
                      (wd_c, wd_s, wd_ref)), cast_sem, W2_CHUNKS)

    @pl.when((step >= W2_STEPS) & (step < last))
    def _prompt_tile():
        assert h_ref.shape[0] == 2 * FFN_SUB
        ra, rb = slice(0, FFN_SUB), slice(FFN_SUB, 2 * FFN_SUB)
        h_a = h_ref[ra, :] + jnp.dot(mix_ref[ra, :], wout_ref[...], preferred_element_type=jnp.float32)
        h_b = h_ref[rb, :] + jnp.dot(mix_ref[rb, :], wout_ref[...], preferred_element_type=jnp.float32)
        hn_a = _rms(h_a, n2_ref[...]).astype(jnp.bfloat16)
        _swiglu_stage(hn_a, wg_ref, wu_ref, a0_ref, 0, FFN_TF)
        hn_b = _rms(h_b, n2_ref[...]).astype(jnp.bfloat16)
        _swiglu_stage(hn_a, wg_ref, wu_ref, a0_ref, FFN_TF, D_FF)
        f_a = jnp.dot(a0_ref[...], wd_ref[...], preferred_element_type=jnp.float32)
        _swiglu_stage(hn_b, wg_ref, wu_ref, a1_ref, 0, FFN_TF)
        y_ref[ra, :] = _rms(h_a + 0.5 * f_a, nf_ref[...])
        _swiglu_stage(hn_b, wg_ref, wu_ref, a1_ref, FFN_TF, D_FF)
        f_b = jnp.dot(a1_ref[...], wd_ref[...], preferred_element_type=jnp.float32)
        y_ref[rb, :] = _rms(h_b + 0.5 * f_b, nf_ref[...])

    @pl.when(step == last)
    def _decode_rows():
        h_d = jnp.concatenate([hd0_ref[...], hd1_ref[...]], axis=0)
        yd_ref[...] = _ffn2_rows(h_d, mixd_ref[...], weights, a0_ref)


def _ffn2(h_all, mix, mixd, wout, n2, wg, wu, wd, nf):
    nb, seq, _ = mix.shape
    seq_all = h_all.shape[1]
    n_dec = mixd.shape[0]
    tail = seq_all - seq
    assert n_dec == 2 * tail
    per_b = seq // FFN_TM
    n_tiles = nb * per_b
    h_all = h_all.reshape(nb * seq_all, D_MODEL)
    mix = mix.reshape(nb * seq, D_MODEL)
    idx = lambda i: jnp.clip(i - W2_STEPS, 0, n_tiles - 1)
    tile = lambda i: (idx(i), 0)
    assert seq_all % SUBLANES == 0
    tile_all = lambda i: (pl.multiple_of((idx(i) // per_b) * seq_all + (idx(i) % per_b) * FFN_TM, SUBLANES), 0)
    yp, yd = pl.pallas_call(
        _ffn2_kernel,
        grid=(W2_STEPS + n_tiles + 1,),
        in_specs=[
            pl.BlockSpec((pl.Element(FFN_TM), pl.Element(D_MODEL)), tile_all),
            pl.BlockSpec((FFN_TM, D_MODEL), tile),
            pl.BlockSpec((pl.Element(tail), pl.Element(D_MODEL)), lambda i: (seq, 0),
                         pipeline_mode=pl.Buffered(1)),
            pl.BlockSpec((pl.Element(tail), pl.Element(D_MODEL)), lambda i: (seq_all + seq, 0),
                         pipeline_mode=pl.Buffered(1)),
            _whole(mixd.shape),
            _IN_HBM,
            _IN_HBM,
            _IN_HBM,
            _IN_HBM,
            _whole((1, D_MODEL)),
            _whole((1, D_MODEL)),
        ],
        out_specs=[
            pl.BlockSpec((FFN_TM, D_MODEL), tile),
            pl.BlockSpec((n_dec, None, D_MODEL), lambda i: (0, 0, 0)),
        ],
        out_shape=[
            jax.ShapeDtypeStruct((nb * seq, D_MODEL), jnp.float32),
            jax.ShapeDtypeStruct((n_dec, 1, D_MODEL), jnp.float32),
        ],
        scratch_shapes=[
            pltpu.VMEM((D_MODEL, D_MODEL), jnp.bfloat16),
            pltpu.VMEM((D_MODEL, D_FF), jnp.bfloat16),
            pltpu.VMEM((D_MODEL, D_FF), jnp.bfloat16),
            pltpu.VMEM((D_FF, D_MODEL), jnp.bfloat16),
            pltpu.VMEM((FFN_SUB, D_FF), jnp.bfloat16),
            pltpu.VMEM((FFN_SUB, D_FF), jnp.bfloat16),
            _staging(D_MODEL // W2_CHUNKS, D_MODEL),
            _staging(D_MODEL // W2_CHUNKS, D_FF),
            _staging(D_MODEL // W2_CHUNKS, D_FF),
            _staging(D_FF // W2_CHUNKS, D_MODEL),
            pltpu.SemaphoreType.DMA((8,)),
        ],
        compiler_params=pltpu.CompilerParams(
            dimension_semantics=("arbitrary",), vmem_limit_bytes=VMEM_LIMIT),
        name="outproj_ffn2",
    )(h_all, mix, h_all, h_all, mixd, wout, wg, wu, wd, n2, nf)
    return yp.reshape(nb, seq, D_MODEL), yd


def _mixer_step_kernel(u_ref, v_ref, g_ref, hre_ref, him_ref, buf_ref, are_ref, aim_ref, bb_ref, wct_ref,
                       d_ref, wglu_ref, bglu_ref, cw_ref, mix_ref, sre_ref, sim_ref, nconv_ref):
    u = u_ref[...]
    u_bf = u.astype(jnp.bfloat16)
    ys = []
    for k in range(N_BLK):
        bu = jnp.dot(u_bf[:, k * LANES:(k + 1) * LANES], bb_ref[k], preferred_element_type=jnp.float32)
        st = slice(k * BLK_STATE, (k + 1) * BLK_STATE)
        a_re = are_ref[k:k + 1, :]
        a_im = aim_ref[k:k + 1, :]
        h_re = hre_ref[st, :].T
        h_im = him_ref[st, :].T
        x_re = (a_re * h_re + bu[:, :BLK_STATE]) - a_im * h_im
        x_im = (a_re * h_im + bu[:, BLK_STATE:]) + a_im * h_re
        sre_ref[st, :] = x_re.T
        sim_ref[st, :] = x_im.T
        x = jnp.concatenate([x_re, x_im], axis=1).astype(jnp.bfloat16)
        ys.append(lax.dot_general(x, wct_ref[k], _NT, preferred_element_type=jnp.float32))
    y = jnp.concatenate(ys, axis=1)
    mix_ref[:, 0:SSM_WIDTH] = _glu_tail(y, u, d_ref, wglu_ref, bglu_ref).astype(mix_ref.dtype)

    v = v_ref[...]
    buf1 = buf_ref[:, 1, :]
    z = cw_ref[0] * buf_ref[:, 0, :] + cw_ref[1] * buf1 + cw_ref[2] * v
    mix_ref[:, SSM_WIDTH:] = (g_ref[...] * z).astype(mix_ref.dtype)
    nconv_ref[:, 0, :] = buf1
    nconv_ref[:, 1, :] = v


def _mixer_step(u, v, g, h_re, h_im, conv_state, a_re, a_im, wb, wct, d, wglu, bglu, cw):
    nb = u.shape[0]
    conv_spec = pl.BlockSpec((None, nb, CONV_K - 1, CONV_WIDTH), lambda i: (0, 0, 0, 0))
    half = (N_BLK, LANES, 2 * BLK_STATE)
    bb_spec = pl.BlockSpec(half, lambda i: (0, 1, 0), pipeline_mode=pl.Buffered(1))
    c_spec = pl.BlockSpec(half, lambda i: (0, 0, 0), pipeline_mode=pl.Buffered(1))
    rows = (u, v, g, h_re, h_im)
    tail = (d, wglu, bglu, cw)
    return pl.pallas_call(
        _mixer_step_kernel,
        grid=(1,),
        in_specs=([_whole(a.shape) for a in rows] + [conv_spec] + [_whole(a_re.shape), _whole(a_im.shape)]
                  + [bb_spec, c_spec] + [_whole(a.shape) for a in tail]),
        out_specs=[
            pl.BlockSpec((nb, D_MODEL), lambda i: (0, 0)),
            pl.BlockSpec((N_STATE, nb), lambda i: (0, 0)),
            pl.BlockSpec((N_STATE, nb), lambda i: (0, 0)),
            conv_spec,
        ],
        out_shape=[
            jax.ShapeDtypeStruct((nb, D_MODEL), jnp.bfloat16),
            jax.ShapeDtypeStruct((N_STATE, nb), jnp.float32),
            jax.ShapeDtypeStruct((N_STATE, nb), jnp.float32),
            jax.ShapeDtypeStruct(conv_state.shape, jnp.float32),
        ],
        compiler_params=pltpu.CompilerParams(
            dimension_semantics=("arbitrary",), vmem_limit_bytes=VMEM_LIMIT),
        name="mixer_step",
    )(*rows, conv_state, a_re, a_im, wb, wct, *tail)


def _ssm_layout(lam_re, lam_im, log_dt, b_re, b_im, c_re, c_im):
    gpb = N_SSM_GROUPS // N_BLK

    def b_rows(b):
        return b.reshape(N_BLK, gpb, SSM_STATE, SSM_GROUP).transpose(0, 1, 3, 2).reshape(N_BLK, LANES, SSM_STATE)

    def c_rows(c):
        return c.reshape(N_BLK, LANES, SSM_STATE)

    return lam_re, lam_im, log_dt, b_rows(b_re), b_rows(b_im), c_rows(c_re), c_rows(c_im)


def kernel(x_prompt, x_sample, state_ssm_re, state_ssm_im, state_conv, ffn1_norm, ffn1_w_gate, ffn1_w_up, ffn1_w_down, mix_norm, w_in, ssm_lambda_re, ssm_lambda_im, ssm_log_dt, ssm_b_re, ssm_b_im, ssm_c_re, ssm_c_im, ssm_d, ssm_w_glu, ssm_b_glu, conv_w, w_out, ffn2_norm, ffn2_w_gate, ffn2_w_up, ffn2_w_down, final_norm):
    assert ffn1_norm.shape[0] == 1, "single-layer trunk"
    n_prompt, seq, _ = x_prompt.shape
    n_dec, dec_seq, _ = x_sample.shape
    assert dec_seq == 1 and n_prompt == SUBLANES and seq % FFN_TM == 0 and seq % SCAN_T == 0
    assert n_dec % SCAN_T == 0 and n_dec <= n_prompt * SCAN_T and FFN_TM % FFN_SUB == 0

    n1 = ffn1_norm[0][None, :]
    nm = mix_norm[0][None, :]
    n2 = ffn2_norm[0][None, :]
    nf = final_norm[None, :]
    wglu = ssm_w_glu[0]
    layout = _ssm_layout(ssm_lambda_re[0], ssm_lambda_im[0], ssm_log_dt, ssm_b_re[0], ssm_b_im[0],
                         ssm_c_re[0], ssm_c_im[0])
    a_re, a_im, a2_re, a2_im, wb, wct, wdm = _discretize(*layout)
    d = ssm_d[0][None, :]
    bglu = ssm_b_glu[0][None, :]
    cw = conv_w.transpose(1, 0, 2)

    h_all, mixp, p_re, p_im, p_conv, ud, vd, gd = _ffn1_mixer(
        x_prompt, x_sample, n1, ffn1_w_gate[0], ffn1_w_up[0], ffn1_w_down[0], nm,
        w_in[0], a2_re, a2_im, wb, wct, wdm, d, wglu, bglu, cw)

    s_re0 = state_ssm_re[0].transpose(1, 2, 0).reshape(N_STATE, n_dec)
    s_im0 = state_ssm_im[0].transpose(1, 2, 0).reshape(N_STATE, n_dec)
    mixd, s_re, s_im, s_conv = _mixer_step(ud, vd, gd, s_re0, s_im0, state_conv, a_re, a_im, wb, wct,
                                           d, wglu, bglu, cw)

    yp, yd = _ffn2(h_all, mixp, mixd, w_out[0], n2, ffn2_w_gate[0], ffn2_w_up[0], ffn2_w_down[0], nf)

    gs = (N_SSM_GROUPS, SSM_STATE)
    return (yp,
            yd,
            p_re[None],
            p_im[None],
            p_conv[None],
            s_re.reshape(*gs, n_dec).transpose(2, 0, 1)[None],
            s_im.reshape(*gs, n_dec).transpose(2, 0, 1)[None],
            s_conv)
```

```python
import functools

import jax
import jax.numpy as jnp
from jax import lax
from jax.experimental import pallas as pl
from jax.experimental.pallas import tpu as pltpu

D_MODEL = 1024
D_FF = 2816
SSM_WIDTH = 512
CONV_WIDTH = 512
SSM_GROUP = 16
N_SSM_GROUPS = 32
SSM_STATE = 64
CONV_K = 3
IN_PROJ_WIDTH = SSM_WIDTH + 3 * CONV_WIDTH
EPS = 1e-6

LANES = 128
SUBLANES = 8
N_STATE = N_SSM_GROUPS * SSM_STATE
N_BLK = SSM_WIDTH // LANES
BLK_STATE = N_STATE // N_BLK
SLABS_PER_BLK = 2 * BLK_STATE // LANES
HALF = SLABS_PER_BLK // 2
N_TILE = N_BLK * HALF

FFN_TM = 1024
FFN_SUB = 512
W1_STEPS = 1
W2_STEPS = 1
W1_CHUNKS = 16
W2_CHUNKS = 8
FFN_TF = 256
SCAN_T = 64
N_PAIR = SCAN_T // 2
PAIR_PITCH = N_PAIR + 4
VMEM_LIMIT = 62 * 1024 * 1024

_NT = (((1,), (1,)), ((), ()))


def _rms(x, g):
    r = lax.rsqrt(jnp.mean(x * x, axis=-1, keepdims=True) + EPS)
    return x * r * g


def _bdot(a, b):
    return jnp.dot(a.astype(jnp.bfloat16), b.astype(jnp.bfloat16), preferred_element_type=jnp.float32)


def _swiglu_stage(xn, wg_ref, wu_ref, a_ref, lo, hi):
    rows = xn.shape[0]
    for j in range(lo, hi, FFN_TF):
        g = jnp.dot(xn, wg_ref[:, j:j + FFN_TF], preferred_element_type=jnp.float32)
        u = jnp.dot(xn, wu_ref[:, j:j + FFN_TF], preferred_element_type=jnp.float32)
        a_ref[0:rows, j:j + FFN_TF] = ((g * jax.nn.sigmoid(g)) * u).astype(a_ref.dtype)


def _swiglu_staged(xn, wg_ref, wu_ref, wd_ref, a_ref):
    rows = xn.shape[0]
    _swiglu_stage(xn, wg_ref, wu_ref, a_ref, 0, D_FF)
    return jnp.dot(a_ref[0:rows, :], wd_ref[...], preferred_element_type=jnp.float32)


def _stream_cast(triples, sem, n_chunks):
    def copy(w, c, slot):
        src, buf, _ = triples[w]
        n = buf.shape[1]
        return pltpu.make_async_copy(src.at[pl.ds(pl.multiple_of(c * n, SUBLANES), n), :], buf.at[slot],
                                     sem.at[2 * w + slot])

    for w in range(len(triples)):
        copy(w, 0, 0).start()

    def body(c, carry):
        slot = lax.rem(c, 2)

        @pl.when(c + 1 < n_chunks)
        def _prefetch():
            for w in range(len(triples)):
                copy(w, c + 1, 1 - slot).start()

        for w, (_, buf, dst) in enumerate(triples):
            copy(w, c, slot).wait()
            n = buf.shape[1]
            dst[pl.ds(pl.multiple_of(c * n, SUBLANES), n), :] = buf[slot].astype(dst.dtype)
        return carry

    lax.fori_loop(0, n_chunks, body, 0)


def _staging(rows, cols):
    return pltpu.VMEM((2, rows, cols), jnp.float32)


def _whole(shape):
    return pl.BlockSpec(shape, lambda i: (0,) * len(shape), pipeline_mode=pl.Buffered(1))


_IN_HBM = pl.BlockSpec(memory_space=pl.ANY)


def _glu_tail(y, u, d_ref, wglu_ref, bglu_ref):
    y = y + d_ref[...] * u
    z = jax.nn.gelu(y)
    gate = jax.nn.sigmoid(_bdot(z, wglu_ref[...]) + bglu_ref[...])
    return z * gate


def _dot3_nt(a, b):
    bf = jnp.bfloat16
    a_hi, b_hi = a.astype(bf), b.astype(bf)
    a_lo = (a - a_hi.astype(jnp.float32)).astype(bf)
    b_lo = (b - b_hi.astype(jnp.float32)).astype(bf)
    dot = lambda p, q: lax.dot_general(p, q, _NT, preferred_element_type=jnp.float32)
    return dot(a_hi, b_hi) + (dot(a_hi, b_lo) + dot(a_lo, b_hi))


def _discretize_kernel(lr_ref, li_ref, ldt_ref, bre_ref, bim_ref, ctre_ref, ctim_ref,
                       are_ref, aim_ref, a2re_ref, a2im_ref, wb_ref, wct_ref, wdm_ref):
    row_group = lax.broadcasted_iota(jnp.int32, (LANES, BLK_STATE), 0) // SSM_GROUP
    col_group = lax.broadcasted_iota(jnp.int32, (LANES, BLK_STATE), 1) // SSM_STATE
    on_diagonal = row_group == col_group

    def blocks(m):
        pair = jnp.concatenate([m, m], axis=1)
        return jnp.where(on_diagonal, jnp.concatenate([pair] * (BLK_STATE // LANES), axis=1), 0.0)

    gpb = N_SSM_GROUPS // N_BLK

    def state_rows(ref):
        return jnp.concatenate(
            [jnp.concatenate([ref[k * gpb + g:k * gpb + g + 1, :] for g in range(gpb)], axis=1)
             for k in range(N_BLK)], axis=0)

    def per_state(row_ref):
        vals = row_ref[...]
        col_group = lax.broadcasted_iota(jnp.int32, (1, BLK_STATE), 1) // SSM_STATE
        rows = []
        for k in range(N_BLK):
            row = jnp.zeros((1, BLK_STATE), jnp.float32)
            for g in range(gpb):
                row = jnp.where(col_group == g, vals[:, k * gpb + g:k * gpb + g + 1], row)
            rows.append(row)
        return jnp.concatenate(rows, axis=0)

    lr, li = state_rows(lr_ref), state_rows(li_ref)
    dt = jnp.exp(per_state(ldt_ref))
    mag = jnp.exp(lr * dt)
    ab_re = mag * jnp.cos(li * dt)
    ab_im = mag * jnp.sin(li * dt)
    den = lr * lr + li * li
    nr = ab_re - 1.0
    ni = ab_im
    coef_re = (nr * lr + ni * li) / den
    coef_im = (ni * lr - nr * li) / den
    are_ref[...] = ab_re
    aim_ref[...] = ab_im
    a2_re = ab_re * ab_re - ab_im * ab_im
    a2_im = 2.0 * (ab_re * ab_im)
    bf = jnp.bfloat16
    for k in range(N_BLK):
        ar, ai = ab_re[k:k + 1, :], ab_im[k:k + 1, :]
        for j in range(HALF):
            lanes = slice(j * LANES, (j + 1) * LANES)
            a2re_ref[k * HALF + j] = jnp.broadcast_to(a2_re[k:k + 1, lanes], (SUBLANES, LANES))
            a2im_ref[k * HALF + j] = jnp.broadcast_to(a2_im[k:k + 1, lanes], (SUBLANES, LANES))
        cr, ci = coef_re[k:k + 1, :], coef_im[k:k + 1, :]
        bre, bim = blocks(bre_ref[k]), blocks(bim_ref[k])
        bb_re = cr * bre - ci * bim
        bb_im = cr * bim + ci * bre
        wb_ref[k, 0:LANES, 0:BLK_STATE] = (ar * bb_re - ai * bb_im).astype(bf)
        wb_ref[k, 0:LANES, BLK_STATE:] = (ar * bb_im + ai * bb_re).astype(bf)
        wb_ref[k, LANES:, 0:BLK_STATE] = bb_re.astype(bf)
        wb_ref[k, LANES:, BLK_STATE:] = bb_im.astype(bf)
        ctre, ctim = blocks(ctre_ref[k]), blocks(ctim_ref[k])
        wct_ref[k, 0:LANES, 0:BLK_STATE] = ctre.astype(bf)
        wct_ref[k, 0:LANES, BLK_STATE:] = (-ctim).astype(bf)
        wct_ref[k, LANES:, 0:BLK_STATE] = (ctre * ar - ctim * ai).astype(bf)
        wct_ref[k, LANES:, BLK_STATE:] = (-(ctre * ai + ctim * ar)).astype(bf)
        wdm = _dot3_nt(bb_re, ctre) - _dot3_nt(bb_im, ctim)
        wdm_ref[k] = wdm.astype(bf)


def _discretize(lr, li, ldt, bre, bim, ctre, ctim):
    f32, bf = jnp.float32, jnp.bfloat16
    return pl.pallas_call(
        _discretize_kernel,
        out_shape=[
            jax.ShapeDtypeStruct((N_BLK, BLK_STATE), f32),
            jax.ShapeDtypeStruct((N_BLK, BLK_STATE), f32),
            jax.ShapeDtypeStruct((N_TILE, SUBLANES, LANES), f32),
            jax.ShapeDtypeStruct((N_TILE, SUBLANES, LANES), f32),
            jax.ShapeDtypeStruct((N_BLK, 2 * LANES, 2 * BLK_STATE), bf),
            jax.ShapeDtypeStruct((N_BLK, 2 * LANES, 2 * BLK_STATE), bf),
            jax.ShapeDtypeStruct((N_BLK, LANES, LANES), bf),
        ],
        name="ssm_discretize",
    )(lr, li, ldt, bre, bim, ctre, ctim)


def _ffn1_mixer_kernel(x_ref, xd_ref, n1_ref, wg_c, wu_c, wd_c, nm_ref, win_c,
                       a2re_ref, a2im_ref, wb_ref, wct_ref, wdm_ref, d_ref, wglu_ref, bglu_ref, cw_ref,
                       h_ref, mix_ref, sre_ref, sim_ref, nconv_ref, ud_ref, vd_ref, gd_ref,
                       xs0_ref, xs1_ref, xs2_ref, xs3_ref, us_ref, ys_ref, l_ref, yo_ref, ye_ref, e_ref,
                       ecarry_ref, v_ref, g_ref, state_ref, wg_ref, wu_ref, wd_ref, win_ref, a_ref, xdd_ref,
                       wg_s, wu_s, wd_s, win_s, cast_sem):
    xs_refs = (xs0_ref, xs1_ref, xs2_ref, xs3_ref)
    step = pl.program_id(0)
    last = pl.num_programs(0) - 1
    nb, t_len, _ = x_ref.shape
    n_dec = xd_ref.shape[0]
    prow = nb * PAIR_PITCH
    tiles = [(k, j) for k in range(N_BLK) for j in range(HALF)]

    @pl.when(step < W1_STEPS)
    def _load_weights():
        _stream_cast(((wg_c, wg_s, wg_ref), (wu_c, wu_s, wu_ref), (wd_c, wd_s, wd_ref),
                      (win_c, win_s, win_ref)), cast_sem, W1_CHUNKS)

    @pl.when(step == 0)
    def _init():
        for ref in (state_ref, us_ref, l_ref, e_ref, ecarry_ref, v_ref, g_ref):
            ref[...] = jnp.zeros_like(ref)
        xdd_ref[...] = xd_ref[...]

    @pl.when(step == last)
    def _decode_rows_in():
        db = n_dec // t_len
        x_ref[0:db] = xdd_ref[...].reshape(db, t_len, D_MODEL)

    @pl.when(step >= W1_STEPS)
    def _main():
        carry = {}

        def build_lhs():
            for b in range(nb):
                for k in range(N_BLK):
                    for par in range(2):
                        col = (2 * k + par) * LANES
                        l_ref[pl.ds(b * PAIR_PITCH, N_PAIR), col:col + LANES] = (
                            us_ref[k, pl.ds(b * t_len + par, N_PAIR, stride=2), :])

        def b_proj(ks):
            for k in ks:
                lhs = l_ref[:, 2 * k * LANES:2 * (k + 1) * LANES].astype(jnp.bfloat16)
                v = jnp.dot(lhs, wb_ref[k], preferred_element_type=jnp.float32)
                for j in range(SLABS_PER_BLK):
                    xs_refs[k][j] = v[:, j * LANES:(j + 1) * LANES]

        def scan(r0, r1):
            if r0 == 0:
                carry["re"] = [state_ref[k * SLABS_PER_BLK + j] for k, j in tiles]
                carry["im"] = [state_ref[k * SLABS_PER_BLK + HALF + j] for k, j in tiles]
            xr, xi = carry["re"], carry["im"]
            for r in range(r0, r1):
                rows = pl.ds(r, SUBLANES, stride=PAIR_PITCH)
                for n, (k, j) in enumerate(tiles):
                    a_re = a2re_ref[k * HALF + j]
                    a_im = a2im_ref[k * HALF + j]
                    nr = (a_re * xr[n] + xs_refs[k][j, rows, :]) - a_im * xi[n]
                    ni = (a_re * xi[n] + xs_refs[k][HALF + j, rows, :]) + a_im * xr[n]
                    xs_refs[k][j, rows, :] = nr
                    xs_refs[k][HALF + j, rows, :] = ni
                    xr[n], xi[n] = nr, ni
            if r1 == N_PAIR:
                for n, (k, j) in enumerate(tiles):
                    state_ref[k * SLABS_PER_BLK + j] = xr[n]
                    state_ref[k * SLABS_PER_BLK + HALF + j] = xi[n]

        def c_proj(ks):
            for k in ks:
                lanes = slice(k * LANES, (k + 1) * LANES)
                x = jnp.concatenate([xs_refs[k][j] for j in range(SLABS_PER_BLK)], axis=1)
                ce = lax.dot_general(x.astype(jnp.bfloat16), wct_ref[k], _NT,
                                     preferred_element_type=jnp.float32)
                yo_ref[:, lanes] = ce[:, 0:LANES]
                e_ref[k, SUBLANES:SUBLANES + prow, :] = ce[:, LANES:]
                nxt = e_ref[k, pl.ds(SUBLANES + N_PAIR - 1, nb, stride=PAIR_PITCH), :]
                e_ref[k, pl.ds(SUBLANES - 1, nb, stride=PAIR_PITCH), :] = ecarry_ref[k]
                ecarry_ref[k] = nxt
                u_even = l_ref[:, 2 * k * LANES:(2 * k + 1) * LANES].astype(jnp.bfloat16)
                ye_ref[:, lanes] = (e_ref[k, SUBLANES - 1:SUBLANES - 1 + prow, :]
                                    + jnp.dot(u_even, wdm_ref[k], preferred_element_type=jnp.float32))

        def glu_emit():
            for b in range(nb):
                for k in range(N_BLK):
                    lanes = slice(k * LANES, (k + 1) * LANES)
                    for par, src in ((0, ye_ref), (1, yo_ref)):
                        ys_ref[k, pl.ds(b * t_len + par, N_PAIR, stride=2), :] = (
                            src[pl.ds(b * PAIR_PITCH, N_PAIR), lanes])
            y = jnp.concatenate([ys_ref[k] for k in range(N_BLK)], axis=1)
            u = jnp.concatenate([us_ref[k] for k in range(N_BLK)], axis=1)
            s_out = _glu_tail(y, u, d_ref, wglu_ref, bglu_ref)
            mix_ref[:, :, 0:SSM_WIDTH] = s_out.reshape(nb, t_len, SSM_WIDTH).astype(mix_ref.dtype)

        def conv():
            v = v_ref[:, SUBLANES:SUBLANES + t_len, :]
            z = cw_ref[0] * v_ref[:, SUBLANES - 2:SUBLANES - 2 + t_len, :]
            z = z + cw_ref[1] * v_ref[:, SUBLANES - 1:SUBLANES - 1 + t_len, :]
            z = z + cw_ref[2] * v
            mix_ref[:, :, SSM_WIDTH:] = (g_ref[...] * z).astype(mix_ref.dtype)
            v_ref[:, 0:SUBLANES, :] = v_ref[:, t_len:t_len + SUBLANES, :]

        q = N_PAIR // 4
        pieces = [
            lambda: (conv(), build_lhs(), b_proj((0,))), lambda: b_proj((1,)), lambda: b_proj((2,)),
            lambda: b_proj((3,)),
            lambda: scan(0, q), lambda: scan(q, 2 * q), lambda: scan(2 * q, 3 * q), lambda: scan(3 * q, N_PAIR),
            lambda: c_proj((0, 1)), lambda: c_proj((2, 3)),
            glu_emit,
        ]
        assert len(pieces) <= D_FF // FFN_TF

        x = x_ref[...].reshape(nb * t_len, D_MODEL)
        xn = _rms(x, n1_ref[...]).astype(jnp.bfloat16)
        for c, j in enumerate(range(0, D_FF, FFN_TF)):
            _swiglu_stage(xn, wg_ref, wu_ref, a_ref, j, j + FFN_TF)
            if c < len(pieces):
                pieces[c]()
        h = x + 0.5 * jnp.dot(a_ref[...], wd_ref[...], preferred_element_type=jnp.float32)
        hn = _rms(h, nm_ref[...]).astype(jnp.bfloat16)
        p = jnp.dot(hn, win_ref[...], preferred_element_type=jnp.float32)
        h_ref[...] = h.reshape(nb, t_len, D_MODEL)
        v_new = p[:, SSM_WIDTH + 2 * CONV_WIDTH:] * p[:, SSM_WIDTH:SSM_WIDTH + CONV_WIDTH]
        v_ref[:, SUBLANES:SUBLANES + t_len, :] = v_new.reshape(nb, t_len, CONV_WIDTH)
        g_ref[...] = p[:, SSM_WIDTH + CONV_WIDTH:SSM_WIDTH + 2 * CONV_WIDTH].reshape(nb, t_len, CONV_WIDTH)
        for k in range(N_BLK):
            us_ref[k] = p[:, k * LANES:(k + 1) * LANES]

    @pl.when(step == last)
    def _final():
        for n, (k, j) in enumerate(tiles):
            for out_ref, slab in ((sre_ref, k * SLABS_PER_BLK + j), (sim_ref, k * SLABS_PER_BLK + HALF + j)):
                tile = state_ref[slab]
                for half in range(LANES // SSM_STATE):
                    group = (k * BLK_STATE + j * LANES) // SSM_STATE + half
                    out_ref[:, group, :] = tile[:, half * SSM_STATE:(half + 1) * SSM_STATE]
        nconv_ref[...] = v_ref[:, SUBLANES - 2:SUBLANES, :]
        for k in range(N_BLK):
            ud_ref[:, k * LANES:(k + 1) * LANES] = us_ref[k, 0:n_dec, :]
        for b in range(n_dec // t_len):
            vd_ref[b * t_len:(b + 1) * t_len, :] = v_ref[b, SUBLANES:SUBLANES + t_len, :]
            gd_ref[b * t_len:(b + 1) * t_len, :] = g_ref[b]


def _ffn1_mixer(x, xd, n1, wg, wu, wd, nm, win, a2_re, a2_im, wb, wct, wdm, d, wglu, bglu, cw):
    nb, seq, _ = x.shape
    n_dec = xd.shape[0]
    n_chunk = seq // SCAN_T
    prow = nb * PAIR_PITCH
    f32, bf = jnp.float32, jnp.bfloat16
    main = lambda i: jnp.maximum(i - W1_STEPS, 0)
    return pl.pallas_call(
        _ffn1_mixer_kernel,
        grid=(W1_STEPS + n_chunk + 1,),
        in_specs=[
            pl.BlockSpec((nb, SCAN_T, D_MODEL), lambda i: (0, jnp.minimum(main(i), n_chunk - 1), 0)),
            pl.BlockSpec((n_dec, None, D_MODEL), lambda i: (0, 0, 0), pipeline_mode=pl.Buffered(1)),
            _whole((1, D_MODEL)),
            _IN_HBM,
            _IN_HBM,
            _IN_HBM,
            _whole((1, D_MODEL)),
            _IN_HBM,
            _whole((N_TILE, SUBLANES, LANES)),
            _whole((N_TILE, SUBLANES, LANES)),
            _whole((N_BLK, 2 * LANES, 2 * BLK_STATE)),
            _whole((N_BLK, 2 * LANES, 2 * BLK_STATE)),
            _whole((N_BLK, LANES, LANES)),
            _whole((1, SSM_WIDTH)),
            _whole((SSM_WIDTH, SSM_WIDTH)),
            _whole((1, SSM_WIDTH)),
            _whole((CONV_K, 1, CONV_WIDTH)),
        ],
        out_specs=[
            pl.BlockSpec((nb, SCAN_T, D_MODEL), lambda i: (0, main(i), 0)),
            pl.BlockSpec((nb, SCAN_T, D_MODEL), lambda i: (0, jnp.clip(main(i) - 1, 0, n_chunk - 1), 0)),
            pl.BlockSpec((nb, N_SSM_GROUPS, SSM_STATE), lambda i: (0, 0, 0)),
            pl.BlockSpec((nb, N_SSM_GROUPS, SSM_STATE), lambda i: (0, 0, 0)),
            pl.BlockSpec((nb, CONV_K - 1, CONV_WIDTH), lambda i: (0, 0, 0)),
            pl.BlockSpec((n_dec, SSM_WIDTH), lambda i: (0, 0)),
            pl.BlockSpec((n_dec, CONV_WIDTH), lambda i: (0, 0)),
            pl.BlockSpec((n_dec, CONV_WIDTH), lambda i: (0, 0)),
        ],
        out_shape=[
            jax.ShapeDtypeStruct((nb, seq + SCAN_T, D_MODEL), f32),
            jax.ShapeDtypeStruct((nb, seq, D_MODEL), bf),
            jax.ShapeDtypeStruct((nb, N_SSM_GROUPS, SSM_STATE), f32),
            jax.ShapeDtypeStruct((nb, N_SSM_GROUPS, SSM_STATE), f32),
            jax.ShapeDtypeStruct((nb, CONV_K - 1, CONV_WIDTH), f32),
            jax.ShapeDtypeStruct((n_dec, SSM_WIDTH), f32),
            jax.ShapeDtypeStruct((n_dec, CONV_WIDTH), f32),
            jax.ShapeDtypeStruct((n_dec, CONV_WIDTH), f32),
        ],
        scratch_shapes=[
            *[pltpu.VMEM((SLABS_PER_BLK, prow, LANES), f32) for _ in range(N_BLK)],
            pltpu.VMEM((N_BLK, nb * SCAN_T, LANES), f32),
            pltpu.VMEM((N_BLK, nb * SCAN_T, LANES), f32),
            pltpu.VMEM((prow, 2 * SSM_WIDTH), f32),
            pltpu.VMEM((prow, SSM_WIDTH), f32),
            pltpu.VMEM((prow, SSM_WIDTH), f32),
            pltpu.VMEM((N_BLK, SUBLANES + prow, LANES), f32),
            pltpu.VMEM((N_BLK, SUBLANES, LANES), f32),
            pltpu.VMEM((nb, SCAN_T + SUBLANES, CONV_WIDTH), f32),
            pltpu.VMEM((nb, SCAN_T, CONV_WIDTH), f32),
            pltpu.VMEM((2 * N_TILE, SUBLANES, LANES), f32),
            pltpu.VMEM((D_MODEL, D_FF), bf),
            pltpu.VMEM((D_MODEL, D_FF), bf),
            pltpu.VMEM((D_FF, D_MODEL), bf),
            pltpu.VMEM((D_MODEL, IN_PROJ_WIDTH), bf),
            pltpu.VMEM((nb * SCAN_T, D_FF), bf),
            pltpu.VMEM((n_dec, D_MODEL), f32),
            _staging(D_MODEL // W1_CHUNKS, D_FF),
            _staging(D_MODEL // W1_CHUNKS, D_FF),
            _staging(D_FF // W1_CHUNKS, D_MODEL),
            _staging(D_MODEL // W1_CHUNKS, IN_PROJ_WIDTH),
            pltpu.SemaphoreType.DMA((8,)),
        ],
        compiler_params=pltpu.CompilerParams(
            dimension_semantics=("arbitrary",), vmem_limit_bytes=VMEM_LIMIT),
        name="ffn1_mixer",
    )(x, xd, n1, wg, wu, wd, nm, win, a2_re, a2_im, wb, wct, wdm, d, wglu, bglu, cw)


def _ffn2_rows(h, mix, weights, a_ref):
    wout_ref, n2_ref, wg_ref, wu_ref, wd_ref, nf_ref = weights
    h = h + jnp.dot(mix, wout_ref[...], preferred_element_type=jnp.float32)
    hn = _rms(h, n2_ref[...]).astype(jnp.bfloat16)
    h = h + 0.5 * _swiglu_staged(hn, wg_ref, wu_ref, wd_ref, a_ref)
    return _rms(h, nf_ref[...])


def _ffn2_kernel(per_b, h_hbm, mix_hbm, hd0_ref, hd1_ref, mixd_ref, wout_c, wg_c, wu_c, wd_c, n2_ref, nf_ref,
                 y_hbm, yd_ref, wout_ref, wg_ref, wu_ref, wd_ref, a0_ref, a1_ref,
                 wout_s, wg_s, wu_s, wd_s, cast_sem):
    weights = (wout_ref, n2_ref, wg_ref, wu_ref, wd_ref, nf_ref)
    _stream_cast(((wout_c, wout_s, wout_ref), (wg_c, wg_s, wg_ref), (wu_c, wu_s, wu_ref),
                  (wd_c, wd_s, wd_ref)), cast_sem, W2_CHUNKS)

    def _prompt_tile(h_ref, mix_ref, y_ref):
        assert h_ref.shape[1] == 2 * FFN_SUB
        ra, rb = slice(0, FFN_SUB), slice(FFN_SUB, 2 * FFN_SUB)
        h_a = h_ref[0, ra, :] + jnp.dot(mix_ref[ra, :], wout_ref[...], preferred_element_type=jnp.float32)
        h_b = h_ref[0, rb, :] + jnp.dot(mix_ref[rb, :], wout_ref[...], preferred_element_type=jnp.float32)
        hn_a = _rms(h_a, n2_ref[...]).astype(jnp.bfloat16)
        _swiglu_stage(hn_a, wg_ref, wu_ref, a0_ref, 0, FFN_TF)
        hn_b = _rms(h_b, n2_ref[...]).astype(jnp.bfloat16)
        _swiglu_stage(hn_a, wg_ref, wu_ref, a0_ref, FFN_TF, D_FF)
        f_a = jnp.dot(a0_ref[...], wd_ref[...], preferred_element_type=jnp.float32)
        _swiglu_stage(hn_b, wg_ref, wu_ref, a1_ref, 0, FFN_TF)
        y_ref[ra, :] = _rms(h_a + 0.5 * f_a, nf_ref[...])
        _swiglu_stage(hn_b, wg_ref, wu_ref, a1_ref, FFN_TF, D_FF)
        f_b = jnp.dot(a1_ref[...], wd_ref[...], preferred_element_type=jnp.float32)
        y_ref[rb, :] = _rms(h_b + 0.5 * f_b, nf_ref[...])

    n_tiles = y_hbm.shape[0] // FFN_TM
    tile = pl.BlockSpec((FFN_TM, D_MODEL), lambda i: (i, 0))
    pltpu.emit_pipeline(
        _prompt_tile,
        grid=(n_tiles,),
        in_specs=[pl.BlockSpec((1, FFN_TM, D_MODEL), lambda i: (i // per_b, i % per_b, 0)), tile],
        out_specs=[tile],
    )(h_hbm, mix_hbm, y_hbm)

    h_d = jnp.concatenate([hd0_ref[...], hd1_ref[...]], axis=0)
    yd_ref[...] = _ffn2_rows(h_d, mixd_ref[...], weights, a0_ref)


def _ffn2(h_all, mix, mixd, wout, n2, wg, wu, wd, nf):
    nb, seq, _ = mix.shape
    seq_all = h_all.shape[1]
    n_dec = mixd.shape[0]
    tail = seq_all - seq
    assert n_dec == 2 * tail
    per_b = seq // FFN_TM
    n_tiles = nb * per_b
    h_flat = h_all.reshape(nb * seq_all, D_MODEL)
    mix = mix.reshape(nb * seq, D_MODEL)
    assert seq_all % SUBLANES == 0 and n_tiles * FFN_TM == nb * seq
    yp, yd = pl.pallas_call(
        functools.partial(_ffn2_kernel, per_b),
        grid=(1,),
        in_specs=[
            _IN_HBM,
            _IN_HBM,
            pl.BlockSpec((pl.Element(tail), pl.Element(D_MODEL)), lambda i: (seq, 0),
                         pipeline_mode=pl.Buffered(1)),
            pl.BlockSpec((pl.Element(tail), pl.Element(D_MODEL)), lambda i: (seq_all + seq, 0),
                         pipeline_mode=pl.Buffered(1)),
            _whole(mixd.shape),
            _IN_HBM,
            _IN_HBM,
            _IN_HBM,
            _IN_HBM,
            _whole((1, D_MODEL)),
            _whole((1, D_MODEL)),
        ],
        out_specs=[
            _IN_HBM,
            pl.BlockSpec((n_dec, None, D_MODEL), lambda i: (0, 0, 0)),
        ],
        out_shape=[
            jax.ShapeDtypeStruct((nb * seq, D_MODEL), jnp.float32),
            jax.ShapeDtypeStruct((n_dec, 1, D_MODEL), jnp.float32),
        ],
        scratch_shapes=[
            pltpu.VMEM((D_MODEL, D_MODEL), jnp.bfloat16),
            pltpu.VMEM((D_MODEL, D_FF), jnp.bfloat16),
            pltpu.VMEM((D_MODEL, D_FF), jnp.bfloat16),
            pltpu.VMEM((D_FF, D_MODEL), jnp.bfloat16),
            pltpu.VMEM((FFN_SUB, D_FF), jnp.bfloat16),
            pltpu.VMEM((FFN_SUB, D_FF), jnp.bfloat16),
            _staging(D_MODEL // W2_CHUNKS, D_MODEL),
            _staging(D_MODEL // W2_CHUNKS, D_FF),
            _staging(D_MODEL // W2_CHUNKS, D_FF),
            _staging(D_FF // W2_CHUNKS, D_MODEL),
            pltpu.SemaphoreType.DMA((8,)),
        ],
        compiler_params=pltpu.CompilerParams(
            dimension_semantics=("arbitrary",), vmem_limit_bytes=VMEM_LIMIT),
        name="outproj_ffn2",
    )(h_all, mix, h_flat, h_flat, mixd, wout, wg, wu, wd, n2, nf)
    return yp.reshape(nb, seq, D_MODEL), yd


def _mixer_step_kernel(u_ref, v_ref, g_ref, hre_ref, him_ref, buf_ref, are_ref, aim_ref, bb_ref, wct_ref,
                       d_ref, wglu_ref, bglu_ref, cw_ref, mix_ref, sre_ref, sim_ref, nconv_ref):
    u = u_ref[...]
    u_bf = u.astype(jnp.bfloat16)
    ys = []
    for k in range(N_BLK):
        bu = jnp.dot(u_bf[:, k * LANES:(k + 1) * LANES], bb_ref[k], preferred_element_type=jnp.float32)
        st = slice(k * BLK_STATE, (k + 1) * BLK_STATE)
        a_re = are_ref[k:k + 1, :]
        a_im = aim_ref[k:k + 1, :]
        h_re = hre_ref[st, :].T
        h_im = him_ref[st, :].T
        x_re = (a_re * h_re + bu[:, :BLK_STATE]) - a_im * h_im
        x_im = (a_re * h_im + bu[:, BLK_STATE:]) + a_im * h_re
        sre_ref[st, :] = x_re.T
        sim_ref[st, :] = x_im.T
        x = jnp.concatenate([x_re, x_im], axis=1).astype(jnp.bfloat16)
        ys.append(lax.dot_general(x, wct_ref[k], _NT, preferred_element_type=jnp.float32))
    y = jnp.concatenate(ys, axis=1)
    mix_ref[:, 0:SSM_WIDTH] = _glu_tail(y, u, d_ref, wglu_ref, bglu_ref).astype(mix_ref.dtype)

    v = v_ref[...]
    buf1 = buf_ref[:, 1, :]
    z = cw_ref[0] * buf_ref[:, 0, :] + cw_ref[1] * buf1 + cw_ref[2] * v
    mix_ref[:, SSM_WIDTH:] = (g_ref[...] * z).astype(mix_ref.dtype)
    nconv_ref[:, 0, :] = buf1
    nconv_ref[:, 1, :] = v


def _mixer_step(u, v, g, h_re, h_im, conv_state, a_re, a_im, wb, wct, d, wglu, bglu, cw):
    nb = u.shape[0]
    conv_spec = pl.BlockSpec((None, nb, CONV_K - 1, CONV_WIDTH), lambda i: (0, 0, 0, 0))
    half = (N_BLK, LANES, 2 * BLK_STATE)
    bb_spec = pl.BlockSpec(half, lambda i: (0, 1, 0), pipeline_mode=pl.Buffered(1))
    c_spec = pl.BlockSpec(half, lambda i: (0, 0, 0), pipeline_mode=pl.Buffered(1))
    rows = (u, v, g, h_re, h_im)
    tail = (d, wglu, bglu, cw)
    return pl.pallas_call(
        _mixer_step_kernel,
        grid=(1,),
        in_specs=([_whole(a.shape) for a in rows] + [conv_spec] + [_whole(a_re.shape), _whole(a_im.shape)]
                  + [bb_spec, c_spec] + [_whole(a.shape) for a in tail]),
        out_specs=[
            pl.BlockSpec((nb, D_MODEL), lambda i: (0, 0)),
            pl.BlockSpec((N_STATE, nb), lambda i: (0, 0)),
            pl.BlockSpec((N_STATE, nb), lambda i: (0, 0)),
            conv_spec,
        ],
        out_shape=[
            jax.ShapeDtypeStruct((nb, D_MODEL), jnp.bfloat16),
            jax.ShapeDtypeStruct((N_STATE, nb), jnp.float32),
            jax.ShapeDtypeStruct((N_STATE, nb), jnp.float32),
            jax.ShapeDtypeStruct(conv_state.shape, jnp.float32),
        ],
        compiler_params=pltpu.CompilerParams(
            dimension_semantics=("arbitrary",), vmem_limit_bytes=VMEM_LIMIT),
        name="mixer_step",
    )(*rows, conv_state, a_re, a_im, wb, wct, *tail)


def _ssm_layout(lam_re, lam_im, log_dt, b_re, b_im, c_re, c_im):
    gpb = N_SSM_GROUPS // N_BLK

    def b_rows(b):
        return b.reshape(N_BLK, gpb, SSM_STATE, SSM_GROUP).transpose(0, 1, 3, 2).reshape(N_BLK, LANES, SSM_STATE)

    def c_rows(c):
        return c.reshape(N_BLK, LANES, SSM_STATE)

    return lam_re, lam_im, log_dt, b_rows(b_re), b_rows(b_im), c_rows(c_re), c_rows(c_im)


def kernel(x_prompt, x_sample, state_ssm_re, state_ssm_im, state_conv, ffn1_norm, ffn1_w_gate, ffn1_w_up, ffn1_w_down, mix_norm, w_in, ssm_lambda_re, ssm_lambda_im, ssm_log_dt, ssm_b_re, ssm_b_im, ssm_c_re, ssm_c_im, ssm_d, ssm_w_glu, ssm_b_glu, conv_w, w_out, ffn2_norm, ffn2_w_gate, ffn2_w_up, ffn2_w_down, final_norm):
    assert ffn1_norm.shape[0] == 1, "single-layer trunk"
    n_prompt, seq, _ = x_prompt.shape
    n_dec, dec_seq, _ = x_sample.shape
    assert dec_seq == 1 and n_prompt == SUBLANES and seq % FFN_TM == 0 and seq % SCAN_T == 0
    assert n_dec % SCAN_T == 0 and n_dec <= n_prompt * SCAN_T and FFN_TM % FFN_SUB == 0

    n1 = ffn1_norm[0][None, :]
    nm = mix_norm[0][None, :]
    n2 = ffn2_norm[0][None, :]
    nf = final_norm[None, :]
    wglu = ssm_w_glu[0]
    layout = _ssm_layout(ssm_lambda_re[0], ssm_lambda_im[0], ssm_log_dt, ssm_b_re[0], ssm_b_im[0],
                         ssm_c_re[0], ssm_c_im[0])
    a_re, a_im, a2_re, a2_im, wb, wct, wdm = _discretize(*layout)
    d = ssm_d[0][None, :]
    bglu = ssm_b_glu[0][None, :]
    cw = conv_w.transpose(1, 0, 2)

    h_all, mixp, p_re, p_im, p_conv, ud, vd, gd = _ffn1_mixer(
        x_prompt, x_sample, n1, ffn1_w_gate[0], ffn1_w_up[0], ffn1_w_down[0], nm,
        w_in[0], a2_re, a2_im, wb, wct, wdm, d, wglu, bglu, cw)

    s_re0 = state_ssm_re[0].transpose(1, 2, 0).reshape(N_STATE, n_dec)
    s_im0 = state_ssm_im[0].transpose(1, 2, 0).reshape(N_STATE, n_dec)
    mixd, s_re, s_im, s_conv = _mixer_step(ud, vd, gd, s_re0, s_im0, state_conv, a_re, a_im, wb, wct,
                                           d, wglu, bglu, cw)

    yp, yd = _ffn2(h_all, mixp, mixd, w_out[0], n2, ffn2_w_gate[0], ffn2_w_up[0], ffn2_w_down[0], nf)

    gs = (N_SSM_GROUPS, SSM_STATE)
    return (yp,
            yd,
            p_re[None],
            p_im[None],
            p_conv[None],
            s_re.reshape(*gs, n_dec).transpose(2, 0, 1)[None],
            s_im.reshape(*gs, n_dec).transpose(2, 0, 1)[None],
            s_conv)
```
